```python
import math
import jax, jax.numpy as jnp
from jax import lax
import numpy as np

D_MODEL = 1024
BATCH = 8
SEQ = 4096
DEPTH = 1

CHUNK = 64
SSM_WIDTH = D_MODEL // 2
SSM_GROUP = 16
SSM_GROUPS = SSM_WIDTH // SSM_GROUP
SSM_STATE = 64
CONV_WIDTH = D_MODEL // 2
CONV_K = 3
FFN_HIDDEN = ((8 * D_MODEL // 3 + 255) // 256) * 256
IN_COLS = SSM_WIDTH + 3 * CONV_WIDTH + 2 * D_MODEL
ALPHA = (2.0 * DEPTH) ** 0.25
BETA = (8.0 * DEPTH) ** -0.25
DT_MIN = 0.001
DT_MAX = 0.1
LN_EPS = 1e-5

kernel_name = "hybrid_s5_shortconv_gated_deepnorm_block"


def layer_norm(x, g, b):
    xf = x.astype(jnp.float32)
    mu = jnp.mean(xf, axis=-1, keepdims=True)
    var = jnp.mean(jnp.square(xf - mu), axis=-1, keepdims=True)
    y = (xf - mu) * lax.rsqrt(var + LN_EPS) * g.astype(jnp.float32) + b.astype(jnp.float32)
    return y.astype(x.dtype)


def _complex_affine_combine(earlier, later):
    ar_i, ai_i, br_i, bi_i = earlier
    ar_j, ai_j, br_j, bi_j = later
    ar = ar_j * ar_i - ai_j * ai_i
    ai = ar_j * ai_i + ai_j * ar_i
    br = ar_j * br_i - ai_j * bi_i + br_j
    bi = ar_j * bi_i + ai_j * br_i + bi_j
    return (ar, ai, br, bi)


def s5_ssm(u, lam_re, lam_im, log_dt, b_re, b_im, c_re, c_im, d_skip):
    f32 = jnp.float32
    bsz, slen, _ = u.shape
    uf = u.astype(f32).reshape(bsz, slen, SSM_GROUPS, SSM_GROUP).transpose(1, 0, 2, 3)
    lr = lam_re.astype(f32)
    li = lam_im.astype(f32)
    dt = jnp.exp(log_dt.astype(f32))[:, None]
    mag = jnp.exp(lr * dt)
    lb_re = mag * jnp.cos(li * dt)
    lb_im = mag * jnp.sin(li * dt)
    den = lr * lr + li * li
    num_re = lb_re - 1.0
    fr = (num_re * lr + lb_im * li) / den
    fi = (lb_im * lr - num_re * li) / den
    br = b_re.astype(f32)
    bi = b_im.astype(f32)
    bb_re = fr[..., None] * br - fi[..., None] * bi
    bb_im = fr[..., None] * bi + fi[..., None] * br
    bu_re = jnp.einsum('sbgc,gpc->sbgp', uf, bb_re)
    bu_im = jnp.einsum('sbgc,gpc->sbgp', uf, bb_im)
    a_re = jnp.broadcast_to(lb_re[None, None], (slen, 1, SSM_GROUPS, SSM_STATE))
    a_im = jnp.broadcast_to(lb_im[None, None], (slen, 1, SSM_GROUPS, SSM_STATE))
    _, _, xs_re, xs_im = lax.associative_scan(
        _complex_affine_combine, (a_re, a_im, bu_re, bu_im), axis=0)
    y = (jnp.einsum('sbgp,gcp->sbgc', xs_re, c_re.astype(f32))
         - jnp.einsum('sbgp,gcp->sbgc', xs_im, c_im.astype(f32))
         + d_skip.astype(f32).reshape(SSM_GROUPS, SSM_GROUP) * uf)
    y = y.transpose(1, 0, 2, 3).reshape(bsz, slen, SSM_WIDTH)
    return y.astype(u.dtype)


def causal_depthwise_conv(z, w):
    return lax.conv_general_dilated(
        z, w[:, None, :].astype(z.dtype), window_strides=(1,), padding=[(CONV_K - 1, 0)],
        dimension_numbers=('NWC', 'WIO', 'NWC'), feature_group_count=CONV_WIDTH)


def hybrid_layer(x, w_in, b_in, ssm_lambda_re, ssm_lambda_im, ssm_log_dt, ssm_b_re, ssm_b_im,
                 ssm_c_re, ssm_c_im, ssm_d, glu_w, glu_b, w_ssm_out, conv_w, w_conv_out, w_o,
                 ln1_g, ln1_b, w_gate, w_up, w_down, ln2_g, ln2_b):
    proj = jnp.einsum('bsd,dn->bsn', x, w_in) + b_in
    o1 = SSM_WIDTH
    o2 = o1 + CONV_WIDTH
    o3 = o2 + CONV_WIDTH
    o4 = o3 + CONV_WIDTH
    o5 = o4 + D_MODEL
    u, h, c_gate, b_gate, gate_a, gate_b = jnp.split(proj, [o1, o2, o3, o4, o5], axis=-1)

    y_a = s5_ssm(u, ssm_lambda_re, ssm_lambda_im, ssm_log_dt, ssm_b_re, ssm_b_im,
                 ssm_c_re, ssm_c_im, ssm_d)
    g = jax.nn.gelu(y_a)
    y_a = g * jax.nn.sigmoid(jnp.einsum('bsc,ce->bse', g, glu_w) + glu_b)
    y_a = jnp.einsum('bsc,cd->bsd', y_a, w_ssm_out)

    z = causal_depthwise_conv(c_gate * h, conv_w)
    y_b = jnp.einsum('bsc,cd->bsd', b_gate * z, w_conv_out)

    merged = jax.nn.sigmoid(gate_a) * y_a + jax.nn.sigmoid(gate_b) * y_b
    mix = jnp.einsum('bsd,de->bse', merged, w_o)
    x = layer_norm(ALPHA * x + mix, ln1_g, ln1_b)

    hid = jax.nn.silu(jnp.einsum('bsd,df->bsf', x, w_gate)) * jnp.einsum('bsd,df->bsf', x, w_up)
    ffn = jnp.einsum('bsf,fd->bsd', hid, w_down)
    x = layer_norm(ALPHA * x + ffn, ln2_g, ln2_b)
    return x


def _fwd_setup_inputs(seed: int = 0) -> dict:
    key = jax.random.key(seed)
    ks = jax.random.split(key, 24)
    L = DEPTH
    f32 = jnp.float32
    nrm = lambda k, shape, s: jax.random.normal(k, shape, f32) * s
    x = jax.random.normal(ks[0], (BATCH, SEQ, D_MODEL), f32)
    w_in = nrm(ks[1], (L, D_MODEL, IN_COLS), D_MODEL ** -0.5)
    b_in = nrm(ks[2], (L, IN_COLS), 0.01)
    n_idx = jnp.arange(SSM_STATE, dtype=f32)
    ssm_lambda_re = -0.5 + nrm(ks[3], (L, SSM_GROUPS, SSM_STATE), 0.01)
    ssm_lambda_im = math.pi * n_idx[None, None, :] + nrm(ks[4], (L, SSM_GROUPS, SSM_STATE), 0.01)
    ssm_log_dt = jax.random.uniform(ks[5], (L, SSM_GROUPS), f32,
                                    minval=math.log(DT_MIN), maxval=math.log(DT_MAX))
    ssm_b_re = nrm(ks[6], (L, SSM_GROUPS, SSM_STATE, SSM_GROUP), (2.0 * SSM_GROUP) ** -0.5)
    ssm_b_im = nrm(ks[7], (L, SSM_GROUPS, SSM_STATE, SSM_GROUP), (2.0 * SSM_GROUP) ** -0.5)
    ssm_c_re = nrm(ks[8], (L, SSM_GROUPS, SSM_GROUP, SSM_STATE), SSM_STATE ** -0.5)
    ssm_c_im = nrm(ks[9], (L, SSM_GROUPS, SSM_GROUP, SSM_STATE), SSM_STATE ** -0.5)
    ssm_d = nrm(ks[10], (L, SSM_WIDTH), 1.0)
    glu_w = nrm(ks[11], (L, SSM_WIDTH, SSM_WIDTH), SSM_WIDTH ** -0.5)
    glu_b = nrm(ks[12], (L, SSM_WIDTH), 0.01)
    w_ssm_out = nrm(ks[13], (L, SSM_WIDTH, D_MODEL), BETA * SSM_WIDTH ** -0.5)
    conv_w = nrm(ks[14], (L, CONV_K, CONV_WIDTH), CONV_K ** -0.5)
    w_conv_out = nrm(ks[15], (L, CONV_WIDTH, D_MODEL), BETA * CONV_WIDTH ** -0.5)
    w_o = nrm(ks[16], (L, D_MODEL, D_MODEL), BETA * D_MODEL ** -0.5)
    ln1_g = 1.0 + nrm(ks[17], (L, D_MODEL), 0.01)
    ln1_b = nrm(ks[18], (L, D_MODEL), 0.01)
    w_gate = nrm(ks[19], (L, D_MODEL, FFN_HIDDEN), D_MODEL ** -0.5)
    w_up = nrm(ks[20], (L, D_MODEL, FFN_HIDDEN), D_MODEL ** -0.5)
    w_down = nrm(ks[21], (L, FFN_HIDDEN, D_MODEL), BETA * FFN_HIDDEN ** -0.5)
    ln2_g = 1.0 + nrm(ks[22], (L, D_MODEL), 0.01)
    ln2_b = nrm(ks[23], (L, D_MODEL), 0.01)
    return {"x": x, "w_in": w_in, "b_in": b_in,
            "ssm_lambda_re": ssm_lambda_re, "ssm_lambda_im": ssm_lambda_im,
            "ssm_log_dt": ssm_log_dt, "ssm_b_re": ssm_b_re, "ssm_b_im": ssm_b_im,
            "ssm_c_re": ssm_c_re, "ssm_c_im": ssm_c_im, "ssm_d": ssm_d,
            "glu_w": glu_w, "glu_b": glu_b, "w_ssm_out": w_ssm_out,
            "conv_w": conv_w, "w_conv_out": w_conv_out, "w_o": w_o,
            "ln1_g": ln1_g, "ln1_b": ln1_b, "w_gate": w_gate, "w_up": w_up,
            "w_down": w_down, "ln2_g": ln2_g, "ln2_b": ln2_b}


def _fwd_reference(x, w_in, b_in, ssm_lambda_re, ssm_lambda_im, ssm_log_dt, ssm_b_re, ssm_b_im,
              ssm_c_re, ssm_c_im, ssm_d, glu_w, glu_b, w_ssm_out, conv_w, w_conv_out, w_o,
              ln1_g, ln1_b, w_gate, w_up, w_down, ln2_g, ln2_b):
    for l in range(DEPTH):
        x = hybrid_layer(x, w_in[l], b_in[l], ssm_lambda_re[l], ssm_lambda_im[l], ssm_log_dt[l],
                         ssm_b_re[l], ssm_b_im[l], ssm_c_re[l], ssm_c_im[l], ssm_d[l],
                         glu_w[l], glu_b[l], w_ssm_out[l], conv_w[l], w_conv_out[l], w_o[l],
                         ln1_g[l], ln1_b[l], w_gate[l], w_up[l], w_down[l], ln2_g[l], ln2_b[l])
    return x


import jax as _jax
import jax.numpy as _jnp

TWIN_FORMAT = 'train_step'
FWD_PARAMS = ['x', 'w_in', 'b_in', 'ssm_lambda_re', 'ssm_lambda_im', 'ssm_log_dt', 'ssm_b_re', 'ssm_b_im', 'ssm_c_re', 'ssm_c_im', 'ssm_d', 'glu_w', 'glu_b', 'w_ssm_out', 'conv_w', 'w_conv_out', 'w_o', 'ln1_g', 'ln1_b', 'w_gate', 'w_up', 'w_down', 'ln2_g', 'ln2_b']
TWIN_WEIGHTS = ['w_in', 'b_in', 'ssm_lambda_re', 'ssm_lambda_im', 'ssm_log_dt', 'ssm_b_re', 'ssm_b_im', 'ssm_c_re', 'ssm_c_im', 'ssm_d', 'glu_w', 'glu_b', 'w_ssm_out', 'conv_w', 'w_conv_out', 'w_o', 'ln1_g', 'ln1_b', 'w_gate', 'w_up', 'w_down', 'ln2_g', 'ln2_b']
TWIN_DIFF_INPUT = 'x'
TWIN_INPUTS = ['x', 'w_in', 'b_in', 'ssm_lambda_re', 'ssm_lambda_im', 'ssm_log_dt', 'ssm_b_re', 'ssm_b_im', 'ssm_c_re', 'ssm_c_im', 'ssm_d', 'glu_w', 'glu_b', 'w_ssm_out', 'conv_w', 'w_conv_out', 'w_o', 'ln1_g', 'ln1_b', 'w_gate', 'w_up', 'w_down', 'ln2_g', 'ln2_b', 'loss_target', 'm_w_in', 'm_b_in', 'm_ssm_lambda_re', 'm_ssm_lambda_im', 'm_ssm_log_dt', 'm_ssm_b_re', 'm_ssm_b_im', 'm_ssm_c_re', 'm_ssm_c_im', 'm_ssm_d', 'm_glu_w', 'm_glu_b', 'm_w_ssm_out', 'm_conv_w', 'm_w_conv_out', 'm_w_o', 'm_ln1_g', 'm_ln1_b', 'm_w_gate', 'm_w_up', 'm_w_down', 'm_ln2_g', 'm_ln2_b', 'v_w_in', 'v_b_in', 'v_ssm_lambda_re', 'v_ssm_lambda_im', 'v_ssm_log_dt', 'v_ssm_b_re', 'v_ssm_b_im', 'v_ssm_c_re', 'v_ssm_c_im', 'v_ssm_d', 'v_glu_w', 'v_glu_b', 'v_w_ssm_out', 'v_conv_w', 'v_w_conv_out', 'v_w_o', 'v_ln1_g', 'v_ln1_b', 'v_w_gate', 'v_w_up', 'v_w_down', 'v_ln2_g', 'v_ln2_b']
TWIN_OUTPUTS = ['loss', 'grad_x', 'grad_w_in', 'grad_b_in', 'grad_ssm_lambda_re', 'grad_ssm_lambda_im', 'grad_ssm_log_dt', 'grad_ssm_b_re', 'grad_ssm_b_im', 'grad_ssm_c_re', 'grad_ssm_c_im', 'grad_ssm_d', 'grad_glu_w', 'grad_glu_b', 'grad_w_ssm_out', 'grad_conv_w', 'grad_w_conv_out', 'grad_w_o', 'grad_ln1_g', 'grad_ln1_b', 'grad_w_gate', 'grad_w_up', 'grad_w_down', 'grad_ln2_g', 'grad_ln2_b', 'delta_w_in', 'delta_b_in', 'delta_ssm_lambda_re', 'delta_ssm_lambda_im', 'delta_ssm_log_dt', 'delta_ssm_b_re', 'delta_ssm_b_im', 'delta_ssm_c_re', 'delta_ssm_c_im', 'delta_ssm_d', 'delta_glu_w', 'delta_glu_b', 'delta_w_ssm_out', 'delta_conv_w', 'delta_w_conv_out', 'delta_w_o', 'delta_ln1_g', 'delta_ln1_b', 'delta_w_gate', 'delta_w_up', 'delta_w_down', 'delta_ln2_g', 'delta_ln2_b', 'new_m_w_in', 'new_m_b_in', 'new_m_ssm_lambda_re', 'new_m_ssm_lambda_im', 'new_m_ssm_log_dt', 'new_m_ssm_b_re', 'new_m_ssm_b_im', 'new_m_ssm_c_re', 'new_m_ssm_c_im', 'new_m_ssm_d', 'new_m_glu_w', 'new_m_glu_b', 'new_m_w_ssm_out', 'new_m_conv_w', 'new_m_w_conv_out', 'new_m_w_o', 'new_m_ln1_g', 'new_m_ln1_b', 'new_m_w_gate', 'new_m_w_up', 'new_m_w_down', 'new_m_ln2_g', 'new_m_ln2_b', 'new_v_w_in', 'new_v_b_in', 'new_v_ssm_lambda_re', 'new_v_ssm_lambda_im', 'new_v_ssm_log_dt', 'new_v_ssm_b_re', 'new_v_ssm_b_im', 'new_v_ssm_c_re', 'new_v_ssm_c_im', 'new_v_ssm_d', 'new_v_glu_w', 'new_v_glu_b', 'new_v_w_ssm_out', 'new_v_conv_w', 'new_v_w_conv_out', 'new_v_w_o', 'new_v_ln1_g', 'new_v_ln1_b', 'new_v_w_gate', 'new_v_w_up', 'new_v_w_down', 'new_v_ln2_g', 'new_v_ln2_b']
TWIN_LEAF_KINDS = {'loss': 'loss', 'grad_x': 'grad_x', 'grad_w_in': 'grad_w', 'grad_b_in': 'grad_w', 'grad_ssm_lambda_re': 'grad_w', 'grad_ssm_lambda_im': 'grad_w', 'grad_ssm_log_dt': 'grad_w', 'grad_ssm_b_re': 'grad_w', 'grad_ssm_b_im': 'grad_w', 'grad_ssm_c_re': 'grad_w', 'grad_ssm_c_im': 'grad_w', 'grad_ssm_d': 'grad_w', 'grad_glu_w': 'grad_w', 'grad_glu_b': 'grad_w', 'grad_w_ssm_out': 'grad_w', 'grad_conv_w': 'grad_w', 'grad_w_conv_out': 'grad_w', 'grad_w_o': 'grad_w', 'grad_ln1_g': 'grad_w', 'grad_ln1_b': 'grad_w', 'grad_w_gate': 'grad_w', 'grad_w_up': 'grad_w', 'grad_w_down': 'grad_w', 'grad_ln2_g': 'grad_w', 'grad_ln2_b': 'grad_w', 'delta_w_in': 'delta_w', 'delta_b_in': 'delta_w', 'delta_ssm_lambda_re': 'delta_w', 'delta_ssm_lambda_im': 'delta_w', 'delta_ssm_log_dt': 'delta_w', 'delta_ssm_b_re': 'delta_w', 'delta_ssm_b_im': 'delta_w', 'delta_ssm_c_re': 'delta_w', 'delta_ssm_c_im': 'delta_w', 'delta_ssm_d': 'delta_w', 'delta_glu_w': 'delta_w', 'delta_glu_b': 'delta_w', 'delta_w_ssm_out': 'delta_w', 'delta_conv_w': 'delta_w', 'delta_w_conv_out': 'delta_w', 'delta_w_o': 'delta_w', 'delta_ln1_g': 'delta_w', 'delta_ln1_b': 'delta_w', 'delta_w_gate': 'delta_w', 'delta_w_up': 'delta_w', 'delta_w_down': 'delta_w', 'delta_ln2_g': 'delta_w', 'delta_ln2_b': 'delta_w', 'new_m_w_in': 'new_m', 'new_m_b_in': 'new_m', 'new_m_ssm_lambda_re': 'new_m', 'new_m_ssm_lambda_im': 'new_m', 'new_m_ssm_log_dt': 'new_m', 'new_m_ssm_b_re': 'new_m', 'new_m_ssm_b_im': 'new_m', 'new_m_ssm_c_re': 'new_m', 'new_m_ssm_c_im': 'new_m', 'new_m_ssm_d': 'new_m', 'new_m_glu_w': 'new_m', 'new_m_glu_b': 'new_m', 'new_m_w_ssm_out': 'new_m', 'new_m_conv_w': 'new_m', 'new_m_w_conv_out': 'new_m', 'new_m_w_o': 'new_m', 'new_m_ln1_g': 'new_m', 'new_m_ln1_b': 'new_m', 'new_m_w_gate': 'new_m', 'new_m_w_up': 'new_m', 'new_m_w_down': 'new_m', 'new_m_ln2_g': 'new_m', 'new_m_ln2_b': 'new_m', 'new_v_w_in': 'new_v', 'new_v_b_in': 'new_v', 'new_v_ssm_lambda_re': 'new_v', 'new_v_ssm_lambda_im': 'new_v', 'new_v_ssm_log_dt': 'new_v', 'new_v_ssm_b_re': 'new_v', 'new_v_ssm_b_im': 'new_v', 'new_v_ssm_c_re': 'new_v', 'new_v_ssm_c_im': 'new_v', 'new_v_ssm_d': 'new_v', 'new_v_glu_w': 'new_v', 'new_v_glu_b': 'new_v', 'new_v_w_ssm_out': 'new_v', 'new_v_conv_w': 'new_v', 'new_v_w_conv_out': 'new_v', 'new_v_w_o': 'new_v', 'new_v_ln1_g': 'new_v', 'new_v_ln1_b': 'new_v', 'new_v_w_gate': 'new_v', 'new_v_w_up': 'new_v', 'new_v_w_down': 'new_v', 'new_v_ln2_g': 'new_v', 'new_v_ln2_b': 'new_v'}


def _forward(args):
    return _fwd_reference(*[args[k] for k in FWD_PARAMS])


def _output_shape():
    out = _jax.eval_shape(lambda: _forward(_fwd_setup_inputs(0)))
    return out.shape, out.dtype

N_MICROBATCH = 1
ADAM_LR = 0.001
ADAM_B1 = 0.9
ADAM_B2 = 0.999
ADAM_EPS = 1e-08
ADAM_WD = 0.01
ADAM_STEP = 10
PER_EXAMPLE_BATCH_AXIS = {'x': 0, 'loss_target': 0}
SHARED_INPUTS = []
_WEIGHT_DTYPES = {'w_in': _jnp.float32, 'b_in': _jnp.float32, 'ssm_lambda_re': _jnp.float32, 'ssm_lambda_im': _jnp.float32, 'ssm_log_dt': _jnp.float32, 'ssm_b_re': _jnp.float32, 'ssm_b_im': _jnp.float32, 'ssm_c_re': _jnp.float32, 'ssm_c_im': _jnp.float32, 'ssm_d': _jnp.float32, 'glu_w': _jnp.float32, 'glu_b': _jnp.float32, 'w_ssm_out': _jnp.float32, 'conv_w': _jnp.float32, 'w_conv_out': _jnp.float32, 'w_o': _jnp.float32, 'ln1_g': _jnp.float32, 'ln1_b': _jnp.float32, 'w_gate': _jnp.float32, 'w_up': _jnp.float32, 'w_down': _jnp.float32, 'ln2_g': _jnp.float32, 'ln2_b': _jnp.float32}
MOMENT_SCALE = {'w_in': 2.681438e-02, 'b_in': 2.691505e-02, 'ssm_lambda_re': 1.040015e-03, 'ssm_lambda_im': 1.214372e-03, 'ssm_log_dt': 5.227751e-01, 'ssm_b_re': 7.709019e-04, 'ssm_b_im': 7.715790e-04, 'ssm_c_re': 1.056524e-03, 'ssm_c_im': 1.052922e-03, 'ssm_d': 1.618259e-02, 'glu_w': 4.611262e-03, 'glu_b': 6.652604e-03, 'w_ssm_out': 1.867865e-02, 'conv_w': 4.217860e-02, 'w_conv_out': 4.955229e-02, 'w_o': 5.293203e-02, 'ln1_g': 4.186561e-01, 'ln1_b': 2.488804e-01, 'w_gate': 3.146334e-02, 'w_up': 3.058733e-02, 'w_down': 8.502245e-02, 'ln2_g': 3.196122e+01, 'ln2_b': 6.612008e-01}


def _to_microbatches(a, axis):
    t = _jnp.moveaxis(a, axis, 0)
    t = t.reshape((N_MICROBATCH, t.shape[0] // N_MICROBATCH) + t.shape[1:])
    return _jnp.moveaxis(t, 1, axis + 1)


def setup_inputs(seed: int = 0) -> dict:
    inp = _fwd_setup_inputs(seed)
    key = _jax.random.fold_in(_jax.random.key(seed), 7919)
    shape, _ = _output_shape()
    out = dict(inp)
    out["loss_target"] = _jax.random.normal(_jax.random.fold_in(key, 0), shape, _jnp.float32)
    for i, name in enumerate(TWIN_WEIGHTS):
        w = inp[name].astype(_jnp.float32)
        if MOMENT_SCALE is None:
            s = _jnp.sqrt(_jnp.mean(_jnp.square(w)) + 1e-30)
        else:
            s = MOMENT_SCALE[name]
        km, kv = _jax.random.split(_jax.random.fold_in(key, i + 1))
        out[name] = w
        out["m_" + name] = s * _jax.random.normal(km, w.shape, _jnp.float32)
        out["v_" + name] = (s * s) * _jax.random.uniform(kv, w.shape, _jnp.float32, 0.5, 1.5)
    if N_MICROBATCH > 1:
        for name, axis in PER_EXAMPLE_BATCH_AXIS.items():
            out[name] = _to_microbatches(out[name], axis)
    return {'x': out['x'], 'w_in': out['w_in'], 'b_in': out['b_in'], 'ssm_lambda_re': out['ssm_lambda_re'], 'ssm_lambda_im': out['ssm_lambda_im'], 'ssm_log_dt': out['ssm_log_dt'], 'ssm_b_re': out['ssm_b_re'], 'ssm_b_im': out['ssm_b_im'], 'ssm_c_re': out['ssm_c_re'], 'ssm_c_im': out['ssm_c_im'], 'ssm_d': out['ssm_d'], 'glu_w': out['glu_w'], 'glu_b': out['glu_b'], 'w_ssm_out': out['w_ssm_out'], 'conv_w': out['conv_w'], 'w_conv_out': out['w_conv_out'], 'w_o': out['w_o'], 'ln1_g': out['ln1_g'], 'ln1_b': out['ln1_b'], 'w_gate': out['w_gate'], 'w_up': out['w_up'], 'w_down': out['w_down'], 'ln2_g': out['ln2_g'], 'ln2_b': out['ln2_b'], 'loss_target': out['loss_target'], 'm_w_in': out['m_w_in'], 'm_b_in': out['m_b_in'], 'm_ssm_lambda_re': out['m_ssm_lambda_re'], 'm_ssm_lambda_im': out['m_ssm_lambda_im'], 'm_ssm_log_dt': out['m_ssm_log_dt'], 'm_ssm_b_re': out['m_ssm_b_re'], 'm_ssm_b_im': out['m_ssm_b_im'], 'm_ssm_c_re': out['m_ssm_c_re'], 'm_ssm_c_im': out['m_ssm_c_im'], 'm_ssm_d': out['m_ssm_d'], 'm_glu_w': out['m_glu_w'], 'm_glu_b': out['m_glu_b'], 'm_w_ssm_out': out['m_w_ssm_out'], 'm_conv_w': out['m_conv_w'], 'm_w_conv_out': out['m_w_conv_out'], 'm_w_o': out['m_w_o'], 'm_ln1_g': out['m_ln1_g'], 'm_ln1_b': out['m_ln1_b'], 'm_w_gate': out['m_w_gate'], 'm_w_up': out['m_w_up'], 'm_w_down': out['m_w_down'], 'm_ln2_g': out['m_ln2_g'], 'm_ln2_b': out['m_ln2_b'], 'v_w_in': out['v_w_in'], 'v_b_in': out['v_b_in'], 'v_ssm_lambda_re': out['v_ssm_lambda_re'], 'v_ssm_lambda_im': out['v_ssm_lambda_im'], 'v_ssm_log_dt': out['v_ssm_log_dt'], 'v_ssm_b_re': out['v_ssm_b_re'], 'v_ssm_b_im': out['v_ssm_b_im'], 'v_ssm_c_re': out['v_ssm_c_re'], 'v_ssm_c_im': out['v_ssm_c_im'], 'v_ssm_d': out['v_ssm_d'], 'v_glu_w': out['v_glu_w'], 'v_glu_b': out['v_glu_b'], 'v_w_ssm_out': out['v_w_ssm_out'], 'v_conv_w': out['v_conv_w'], 'v_w_conv_out': out['v_w_conv_out'], 'v_w_o': out['v_w_o'], 'v_ln1_g': out['v_ln1_g'], 'v_ln1_b': out['v_ln1_b'], 'v_w_gate': out['v_w_gate'], 'v_w_up': out['v_w_up'], 'v_w_down': out['v_w_down'], 'v_ln2_g': out['v_ln2_g'], 'v_ln2_b': out['v_ln2_b']}


def _loss(weights, diff, rest, loss_target):
    with _jax.named_scope("forward"):
        args = {**rest, TWIN_DIFF_INPUT: diff, **{k: w.astype(_WEIGHT_DTYPES[k]) for k, w in weights.items()}}
        y = _forward(args)
    with _jax.named_scope("loss_head"):
        err = _jnp.square(y.astype(_jnp.float32) - loss_target)
        return 0.5 * _jnp.sum(_jnp.mean(err, axis=-1)) if err.ndim else 0.5 * err


def _adamw(w, g, m, v):
    m = ADAM_B1 * m + (1.0 - ADAM_B1) * g
    v = ADAM_B2 * v + (1.0 - ADAM_B2) * _jnp.square(g)
    m_hat = m / (1.0 - ADAM_B1 ** ADAM_STEP)
    v_hat = v / (1.0 - ADAM_B2 ** ADAM_STEP)
    delta = -ADAM_LR * (m_hat / (_jnp.sqrt(v_hat) + ADAM_EPS) + ADAM_WD * w)
    return delta, m, v


def reference(x, w_in, b_in, ssm_lambda_re, ssm_lambda_im, ssm_log_dt, ssm_b_re, ssm_b_im, ssm_c_re, ssm_c_im, ssm_d, glu_w, glu_b, w_ssm_out, conv_w, w_conv_out, w_o, ln1_g, ln1_b, w_gate, w_up, w_down, ln2_g, ln2_b, loss_target, m_w_in, m_b_in, m_ssm_lambda_re, m_ssm_lambda_im, m_ssm_log_dt, m_ssm_b_re, m_ssm_b_im, m_ssm_c_re, m_ssm_c_im, m_ssm_d, m_glu_w, m_glu_b, m_w_ssm_out, m_conv_w, m_w_conv_out, m_w_o, m_ln1_g, m_ln1_b, m_w_gate, m_w_up, m_w_down, m_ln2_g, m_ln2_b, v_w_in, v_b_in, v_ssm_lambda_re, v_ssm_lambda_im, v_ssm_log_dt, v_ssm_b_re, v_ssm_b_im, v_ssm_c_re, v_ssm_c_im, v_ssm_d, v_glu_w, v_glu_b, v_w_ssm_out, v_conv_w, v_w_conv_out, v_w_o, v_ln1_g, v_ln1_b, v_w_gate, v_w_up, v_w_down, v_ln2_g, v_ln2_b):
    given = dict(x=x, w_in=w_in, b_in=b_in, ssm_lambda_re=ssm_lambda_re, ssm_lambda_im=ssm_lambda_im, ssm_log_dt=ssm_log_dt, ssm_b_re=ssm_b_re, ssm_b_im=ssm_b_im, ssm_c_re=ssm_c_re, ssm_c_im=ssm_c_im, ssm_d=ssm_d, glu_w=glu_w, glu_b=glu_b, w_ssm_out=w_ssm_out, conv_w=conv_w, w_conv_out=w_conv_out, w_o=w_o, ln1_g=ln1_g, ln1_b=ln1_b, w_gate=w_gate, w_up=w_up, w_down=w_down, ln2_g=ln2_g, ln2_b=ln2_b, loss_target=loss_target, m_w_in=m_w_in, m_b_in=m_b_in, m_ssm_lambda_re=m_ssm_lambda_re, m_ssm_lambda_im=m_ssm_lambda_im, m_ssm_log_dt=m_ssm_log_dt, m_ssm_b_re=m_ssm_b_re, m_ssm_b_im=m_ssm_b_im, m_ssm_c_re=m_ssm_c_re, m_ssm_c_im=m_ssm_c_im, m_ssm_d=m_ssm_d, m_glu_w=m_glu_w, m_glu_b=m_glu_b, m_w_ssm_out=m_w_ssm_out, m_conv_w=m_conv_w, m_w_conv_out=m_w_conv_out, m_w_o=m_w_o, m_ln1_g=m_ln1_g, m_ln1_b=m_ln1_b, m_w_gate=m_w_gate, m_w_up=m_w_up, m_w_down=m_w_down, m_ln2_g=m_ln2_g, m_ln2_b=m_ln2_b, v_w_in=v_w_in, v_b_in=v_b_in, v_ssm_lambda_re=v_ssm_lambda_re, v_ssm_lambda_im=v_ssm_lambda_im, v_ssm_log_dt=v_ssm_log_dt, v_ssm_b_re=v_ssm_b_re, v_ssm_b_im=v_ssm_b_im, v_ssm_c_re=v_ssm_c_re, v_ssm_c_im=v_ssm_c_im, v_ssm_d=v_ssm_d, v_glu_w=v_glu_w, v_glu_b=v_glu_b, v_w_ssm_out=v_w_ssm_out, v_conv_w=v_conv_w, v_w_conv_out=v_w_conv_out, v_w_o=v_w_o, v_ln1_g=v_ln1_g, v_ln1_b=v_ln1_b, v_w_gate=v_w_gate, v_w_up=v_w_up, v_w_down=v_w_down, v_ln2_g=v_ln2_g, v_ln2_b=v_ln2_b)
    weights = {n: given[n] for n in TWIN_WEIGHTS}
    shared = {n: given[n] for n in SHARED_INPUTS}
    per_example = {n: given[n] for n in ['x']}
    grad_fn = _jax.value_and_grad(_loss, argnums=(0, 1))

    def one_microbatch(ex, loss_target):
        ex = dict(ex)
        diff = ex.pop(TWIN_DIFF_INPUT)
        return grad_fn(weights, diff, {**shared, **ex}, loss_target)

    if N_MICROBATCH == 1:
        loss, (grad_w, grad_x) = one_microbatch(per_example, given["loss_target"])
    else:
        def body(carry, xs):
            loss_sum, grad_sum = carry
            l_k, (gw_k, gx_k) = one_microbatch(xs[0], xs[1])
            with _jax.named_scope("update"):
                return (loss_sum + l_k, _jax.tree.map(_jnp.add, grad_sum, gw_k)), gx_k

        init = (_jnp.zeros((), _jnp.float32), _jax.tree.map(_jnp.zeros_like, weights))
        (loss, grad_w), grad_x = _jax.lax.scan(body, init, (per_example, given["loss_target"]))
    with _jax.named_scope("update"):
        delta_w, new_m, new_v = {}, {}, {}
        for n in TWIN_WEIGHTS:
            delta_w[n], new_m[n], new_v[n] = _adamw(weights[n], grad_w[n], given["m_" + n], given["v_" + n])
    return (loss, grad_x, *[grad_w[n] for n in TWIN_WEIGHTS], *[delta_w[n] for n in TWIN_WEIGHTS],
            *[new_m[n] for n in TWIN_WEIGHTS], *[new_v[n] for n in TWIN_WEIGHTS])
```

```python
import functools
import math

import jax
import jax.numpy as jnp
from jax import lax
from jax.experimental import pallas as pl
from jax.experimental.pallas import tpu as pltpu

F32 = jnp.float32
BF16 = jnp.bfloat16

D_MODEL = 1024
IN_COLS = 4096
SSM_W = 512
N_GROUPS = 32
N_STATE = 64
GROUP_C = 16
STATE_COLS = N_GROUPS * N_STATE
STRIP = 512
N_STRIPS = STATE_COLS // STRIP
FFN_SHARD = 704
N_SHARDS = 4
ALPHA = 2.0 ** 0.25
LN_EPS = 1e-5
GELU_K = math.sqrt(2.0 / math.pi)
GELU_C = 0.044715

ADAM_LR = 0.001
ADAM_B1 = 0.9
ADAM_B2 = 0.999
ADAM_EPS = 1e-08
ADAM_WD = 0.01
ADAM_STEP = 10

V7X_VMEM_BYTES = 64 * 1024 * 1024
VMEM_LIMIT = V7X_VMEM_BYTES - 8 * 1024 * 1024
SUBLANES = 8

MESH = pl.DeviceIdType.MESH
ANY = pl.BlockSpec(memory_space=pl.ANY)


def _dot(a, b):
    return jnp.dot(a.astype(BF16), b.astype(BF16), preferred_element_type=F32)


def _dot_t(a, b):
    return lax.dot_general(a.astype(BF16), b.astype(BF16), (((1,), (1,)), ((), ())),
                           preferred_element_type=F32)


def _tdot(a, b):
    return lax.dot_general(a.astype(BF16), b.astype(BF16), (((0,), (0,)), ((), ())),
                           preferred_element_type=F32)


def _sigmoid(v):
    return 1.0 / (1.0 + jnp.exp(-v))


def _split3(v):
    hi = v.astype(BF16)
    r1 = v - hi.astype(F32)
    mid = r1.astype(BF16)
    lo = (r1 - mid.astype(F32)).astype(BF16)
    return hi, mid, lo


def _exact_dot(v, sel):
    hi, mid, lo = _split3(v)
    return (jnp.dot(hi, sel, preferred_element_type=F32)
            + jnp.dot(mid, sel, preferred_element_type=F32)
            + jnp.dot(lo, sel, preferred_element_type=F32))


def _const(shape):
    nd = len(shape)
    return pl.BlockSpec(shape, lambda *_: (0,) * nd)


def _params(sem, vmem=VMEM_LIMIT):
    return pltpu.CompilerParams(dimension_semantics=sem, vmem_limit_bytes=vmem)


def _gelu_parts(v):
    inner = GELU_K * (v + GELU_C * v * v * v)
    t = jnp.tanh(inner)
    g = 0.5 * v * (1.0 + t)
    dg = 0.5 * (1.0 + t) + 0.5 * v * (1.0 - t * t) * GELU_K * (1.0 + 3.0 * GELU_C * v * v)
    return g, dg


def _ssm_discretise(lr, li, ldt, lr16, li16, ldt16, brt, bit):
    def lam_bar(lr_, li_, ldt_):
        dt = jnp.exp(ldt_)
        mag = jnp.exp(lr_ * dt)
        return mag * jnp.cos(li_ * dt), mag * jnp.sin(li_ * dt)

    lb_re, lb_im = lam_bar(lr, li, ldt)
    l16_re, l16_im = lam_bar(lr16, li16, ldt16)
    den = lr16 * lr16 + li16 * li16
    num_re = l16_re - 1.0
    fr = (num_re * lr16 + l16_im * li16) / den
    fi = (l16_im * lr16 - num_re * li16) / den
    bb_re = fr * brt - fi * bit
    bb_im = fr * bit + fi * brt
    return lb_re, lb_im, bb_re, bb_im


def _strip_selectors():
    p = lax.broadcasted_iota(jnp.int32, (N_STATE, STRIP), 0)
    col = lax.broadcasted_iota(jnp.int32, (N_STATE, STRIP), 1)
    rep = ((col & (N_STATE - 1)) == p).astype(BF16)
    row = lax.broadcasted_iota(jnp.int32, (SSM_W, STRIP), 0)
    col2 = lax.broadcasted_iota(jnp.int32, (SSM_W, STRIP), 1)
    mask = (((row >> 4) & 7) == (col2 >> 6))
    return rep, mask


def _ssm_prepare(lr, li, ldt, lr16, li16, ldt16, brt, bit, cre, cim):
    def body(lr_r, li_r, ldt_r, lr16_r, li16_r, ldt16_r, brt_r, bit_r, cre_r, cim_r,
             pwr_r, pwi_r, bbr_r, bbi_r, ctr_r, cti_r):
        lb_re, lb_im, bb_re, bb_im = _ssm_discretise(
            lr_r[...], li_r[...], ldt_r[...], lr16_r[...], li16_r[...], ldt16_r[...], brt_r[...], bit_r[...])
        pr, pi_ = lb_re, lb_im
        pwr_r[0] = pr
        pwi_r[0] = pi_
        for k in range(1, SUBLANES):
            pr, pi_ = pr * lb_re - pi_ * lb_im, pr * lb_im + pi_ * lb_re
            pwr_r[k] = pr
            pwi_r[k] = pi_
        rep, mask = _strip_selectors()
        for src, dst in ((bb_re, bbr_r), (bb_im, bbi_r), (cre_r[...], ctr_r), (cim_r[...], cti_r)):
            wide = jnp.dot(src.astype(BF16), rep, preferred_element_type=F32)
            dst[...] = jnp.where(mask, wide, 0.0).astype(BF16)

    vm = pl.BlockSpec(memory_space=pltpu.VMEM)
    return pl.pallas_call(
        body, name="ssm_prepare",
        out_shape=(jax.ShapeDtypeStruct((SUBLANES, N_GROUPS, N_STATE), F32),) * 2
        + (jax.ShapeDtypeStruct((SSM_W, STRIP), BF16),) * 4,
        in_specs=[vm] * 10, out_specs=(vm,) * 6,
    )(lr, li, ldt, lr16, li16, ldt16, brt, bit, cre, cim)


def _scan_tables(pwr, pwi):
    pr = pwr.reshape(SUBLANES, STATE_COLS)
    pi_ = pwi.reshape(SUBLANES, STATE_COLS)
    row = jnp.arange(SUBLANES)[:, None]
    z = jnp.zeros((SUBLANES, STATE_COLS), F32)

    def lvl(p, k, keep):
        return jnp.where(keep, jnp.broadcast_to(p[k - 1][None], (SUBLANES, STATE_COLS)), z)

    fwd_re = [lvl(pr, 1, row >= 1), lvl(pr, 2, row >= 2), lvl(pr, 4, row >= 4), pr]
    fwd_im = [lvl(pi_, 1, row >= 1), lvl(pi_, 2, row >= 2), lvl(pi_, 4, row >= 4), pi_]
    rev_re = [lvl(pr, 1, row <= 6), lvl(pr, 2, row <= 5), lvl(pr, 4, row <= 3), pr[::-1]]
    rev_im = [-lvl(pi_, 1, row <= 6), -lvl(pi_, 2, row <= 5), -lvl(pi_, 4, row <= 3), -pi_[::-1]]
    return jnp.stack(fwd_re + fwd_im + rev_re + rev_im)


def _ssm_param_grads(lr, li, ldt, lr16, li16, ldt16, brt, bit, dlbr, dlbi, dbbr, dbbi, dctr, dcti):
    def body(lr_r, li_r, ldt_r, lr16_r, li16_r, ldt16_r, brt_r, bit_r,
             dlbr_r, dlbi_r, dbbr_r, dbbi_r, dctr_r, dcti_r,
             glr_r, gli_r, gldt_r, gbrt_r, gbit_r, gcre_r, gcim_r):
        rep, mask = _strip_selectors()

        def fold(acc):
            return sum(lax.dot_general(t, rep, (((1,), (1,)), ((), ())), preferred_element_type=F32)
                       for t in _split3(jnp.where(mask, acc, 0.0)))

        g_lb_re = jnp.sum(dlbr_r[...], axis=0)
        g_lb_im = jnp.sum(dlbi_r[...], axis=0)
        g_bb_re = fold(dbbr_r[...])
        g_bb_im = fold(dbbi_r[...])
        gcre_r[...] = fold(dctr_r[...])
        gcim_r[...] = fold(dcti_r[...])
        prim = (lr_r[...], li_r[...], ldt_r[...], lr16_r[...], li16_r[...], ldt16_r[...], brt_r[...], bit_r[...])
        _, vjp = jax.vjp(_ssm_discretise, *prim)
        g_lr, g_li, g_ldt, g_lr16, g_li16, g_ldt16, g_brt, g_bit = vjp((g_lb_re, g_lb_im, g_bb_re, g_bb_im))
        grp = lax.broadcasted_iota(jnp.int32, (N_GROUPS, SSM_W), 0)
        rw = lax.broadcasted_iota(jnp.int32, (N_GROUPS, SSM_W), 1)
        gsum = ((rw >> 4) == grp).astype(BF16)

        def group_sum(v):
            return sum(jnp.dot(gsum, t, preferred_element_type=F32) for t in _split3(v))

        glr_r[...] = g_lr + group_sum(g_lr16)
        gli_r[...] = g_li + group_sum(g_li16)
        gldt_r[...] = g_ldt + jnp.sum(group_sum(g_ldt16), axis=1, keepdims=True)
        gbrt_r[...] = g_brt
        gbit_r[...] = g_bit

    vm = pl.BlockSpec(memory_space=pltpu.VMEM)
    gp = jax.ShapeDtypeStruct((N_GROUPS, N_STATE), F32)
    gb = jax.ShapeDtypeStruct((SSM_W, N_STATE), F32)
    return pl.pallas_call(
        body, name="ssm_param_grads",
        out_shape=(gp, gp, jax.ShapeDtypeStruct((N_GROUPS, 1), F32), gb, gb, gb, gb),
        in_specs=[vm] * 14, out_specs=(vm,) * 7,
    )(lr, li, ldt, lr16, li16, ldt16, brt, bit, dlbr, dlbi, dbbr, dbbi, dctr, dcti)


def _in_proj(x, w_in_st, b_in):
    t = x.shape[0]
    tm = 512

    def body(x_r, w_r, b_r, o_r):
        o_r[...] = _dot(x_r[...], w_r[0]) + b_r[...]

    return pl.pallas_call(
        body, name="in_proj", grid=(t // tm, N_SHARDS),
        out_shape=jax.ShapeDtypeStruct((t, IN_COLS), F32),
        in_specs=[pl.BlockSpec((tm, D_MODEL), lambda i, j: (i, 0)),
                  pl.BlockSpec((1, D_MODEL, D_MODEL), lambda i, j: (j, 0, 0)),
                  pl.BlockSpec((1, D_MODEL), lambda i, j: (0, j))],
        out_specs=pl.BlockSpec((tm, D_MODEL), lambda i, j: (i, j)),
        compiler_params=_params(("parallel", "arbitrary")),
    )(x, w_in_st, b_in)


def _cmul_add(xr, xi, mr, mi, sr, si):
    return xr + (mr * sr - mi * si), xi + (mr * si + mi * sr)


def _ssm_forward(proj, bbr, bbi, ctr, cti, d_skip, tab, tc=512):
    t = proj.shape[0]
    nb = tc // SUBLANES

    def body(u_r, bbr_r, bbi_r, ctr_r, cti_r, d_r, tab_r, xsr_r, xsi_r, y_r, car_r, car_i):
        @pl.when(pl.program_id(1) == 0)
        def _():
            car_r[...] = jnp.zeros_like(car_r)
            car_i[...] = jnp.zeros_like(car_i)

        u = u_r[...]
        xsr_r[...] = _dot(u, bbr_r[...])
        xsi_r[...] = _dot(u, bbi_r[...])

        def block(b, carry):
            cr, ci = carry
            rows = pl.ds(pl.multiple_of(b * SUBLANES, SUBLANES), SUBLANES)
            xr = xsr_r[rows, :]
            xi = xsi_r[rows, :]
            for lvl, s in ((0, 1), (1, 2), (2, 4)):
                xr, xi = _cmul_add(xr, xi, tab_r[lvl], tab_r[4 + lvl],
                                   pltpu.roll(xr, s, 0), pltpu.roll(xi, s, 0))
            xr, xi = _cmul_add(xr, xi, tab_r[3], tab_r[7],
                               jnp.broadcast_to(cr, xr.shape), jnp.broadcast_to(ci, xi.shape))
            xsr_r[rows, :] = xr
            xsi_r[rows, :] = xi
            return xr[SUBLANES - 1:SUBLANES, :], xi[SUBLANES - 1:SUBLANES, :]

        cr, ci = lax.fori_loop(0, nb, block, (car_r[...], car_i[...]))
        car_r[...] = cr
        car_i[...] = ci
        y_r[...] = _dot_t(xsr_r[...], ctr_r[...]) - _dot_t(xsi_r[...], cti_r[...]) + d_r[...] * u

    strip_mat = pl.BlockSpec((128, STRIP), lambda j, k: (j, 0))
    states = pl.BlockSpec((tc, STRIP), lambda j, k: (k, j))
    return pl.pallas_call(
        body, name="ssm_forward", grid=(N_STRIPS, t // tc),
        out_shape=(jax.ShapeDtypeStruct((t, STATE_COLS), F32), jax.ShapeDtypeStruct((t, STATE_COLS), F32),
                   jax.ShapeDtypeStruct((t, SSM_W), F32)),
        in_specs=[pl.BlockSpec((tc, 128), lambda j, k: (k, j)),
                  strip_mat, strip_mat, strip_mat, strip_mat,
                  pl.BlockSpec((1, 128), lambda j, k: (0, j)),
                  pl.BlockSpec((16, SUBLANES, STRIP), lambda j, k: (0, 0, j))],
        out_specs=(states, states, pl.BlockSpec((tc, 128), lambda j, k: (k, j))),
        scratch_shapes=[pltpu.VMEM((1, STRIP), F32), pltpu.VMEM((1, STRIP), F32)],
        compiler_params=_params(("parallel", "arbitrary")),
    )(proj, bbr, bbi, ctr, cti, d_skip, tab)


def _shift_down(v, prev, n):
    row = lax.broadcasted_iota(jnp.int32, v.shape, 0)
    out = pltpu.roll(v, n, 0)
    for r in range(n):
        src = prev[SUBLANES - n + r:SUBLANES - n + r + 1, :]
        out = jnp.where(row == r, jnp.broadcast_to(src, v.shape), out)
    return out


def _shift_up(v, nxt, n):
    rows = v.shape[0]
    row = lax.broadcasted_iota(jnp.int32, v.shape, 0)
    out = pltpu.roll(v, rows - n, 0)
    for r in range(n):
        src = nxt[r:r + 1, :]
        out = jnp.where(row == rows - n + r, jnp.broadcast_to(src, v.shape), out)
    return out


def _conv3(q, q_prev, w):
    return w[2:3, :] * q + w[1:2, :] * _shift_down(q, q_prev, 1) + w[0:1, :] * _shift_down(q, q_prev, 2)


def _mixer_forward(x, proj, ya0, glu_w, glu_b, wso_st, conv_w8, wco_st, w_o, tm=256):
    t = x.shape[0]
    hb = tm // SUBLANES

    def body(x_r, ya0_r, h_r, cg_r, bg_r, ga_r, gb_r, hp_r, cgp_r,
             glu_w_r, glu_b_r, wso_r, cw_r, wco_r, wo_r, xh_r, rstd_r, ya_r, yb_r):
        i = pl.program_id(0)
        g, _ = _gelu_parts(ya0_r[...])
        ya1 = g * _sigmoid(_dot(g, glu_w_r[...]) + glu_b_r[...])
        q = cg_r[...] * h_r[...]
        q_prev = jnp.where(i > 0, cgp_r[...] * hp_r[...], 0.0)
        yb0 = bg_r[...] * _conv3(q, q_prev, cw_r[...])
        for j in range(N_SHARDS):
            ya_r[:, 256 * j:256 * (j + 1)] = _dot(ya1, wso_r[j])
            yb_r[:, 256 * j:256 * (j + 1)] = _dot(yb0, wco_r[j])
        merged = _sigmoid(ga_r[...]) * ya_r[...] + _sigmoid(gb_r[...]) * yb_r[...]
        r1 = ALPHA * x_r[...] + _dot(merged, wo_r[...])
        mu = jnp.mean(r1, axis=-1, keepdims=True)
        cen = r1 - mu
        rstd = lax.rsqrt(jnp.mean(cen * cen, axis=-1, keepdims=True) + LN_EPS)
        xh_r[...] = cen * rstd
        rstd_r[...] = rstd

    def col(w, c):
        return pl.BlockSpec((tm, w), lambda i: (i, c))

    def prev(c):
        return pl.BlockSpec((SUBLANES, SSM_W), lambda i: (jnp.maximum(i * hb - 1, 0), c))

    return pl.pallas_call(
        body, name="mixer_forward", grid=(t // tm,),
        out_shape=(jax.ShapeDtypeStruct((t, D_MODEL), F32), jax.ShapeDtypeStruct((t, 1), F32),
                   jax.ShapeDtypeStruct((t, D_MODEL), F32), jax.ShapeDtypeStruct((t, D_MODEL), F32)),
        in_specs=[col(D_MODEL, 0), col(SSM_W, 0), col(SSM_W, 1), col(SSM_W, 2), col(SSM_W, 3),
                  col(D_MODEL, 2), col(D_MODEL, 3), prev(1), prev(2),
                  _const((SSM_W, SSM_W)), _const((1, SSM_W)), _const((N_SHARDS, SSM_W, 256)),
                  _const((SUBLANES, SSM_W)), _const((N_SHARDS, SSM_W, 256)), _const((D_MODEL, D_MODEL))],
        out_specs=(col(D_MODEL, 0), pl.BlockSpec((tm, 1), lambda i: (i, 0)), col(D_MODEL, 0), col(D_MODEL, 0)),
        compiler_params=_params(("parallel",)),
    )(x, ya0, proj, proj, proj, proj, proj, proj, proj, glu_w, glu_b, wso_st, conv_w8, wco_st, w_o)


def _layer_norm_bwd(dxhat, xhat, rstd):
    m1 = jnp.mean(dxhat, axis=-1, keepdims=True)
    m2 = jnp.mean(dxhat * xhat, axis=-1, keepdims=True)
    return rstd * (dxhat - m1 - xhat * m2)


def _ffn_step(xhat1, rstd1, target, ln1_g, ln1_b, ln2_g, ln2_b, wg_st, wu_st, wd_st, tm=256):
    t = xhat1.shape[0]

    def body(xh_r, rstd_r, tgt_r, g1_r, b1_r, g2_r, b2_r, wg_r, wu_r, wd_r,
             loss_r, dr1_r, x1b_r, dr2b_r, hid_r, dhg_r, dhu_r, dg2_r, db2_r, dg1_r, db1_r,
             hg_s, hu_s):
        @pl.when(pl.program_id(0) == 0)
        def _():
            for r in (loss_r, dg2_r, db2_r, dg1_r, db1_r):
                r[...] = jnp.zeros_like(r)

        xhat1_v = xh_r[...]
        x1 = xhat1_v * g1_r[...] + b1_r[...]
        x1b = x1.astype(BF16)
        x1b_r[...] = x1b
        ffn = jnp.zeros((tm, D_MODEL), F32)
        for j in range(N_SHARDS):
            hg = jnp.dot(x1b, wg_r[j], preferred_element_type=F32)
            hu = jnp.dot(x1b, wu_r[j], preferred_element_type=F32)
            hg_s[j] = hg
            hu_s[j] = hu
            hid = (hg * _sigmoid(hg) * hu).astype(BF16)
            hid_r[j] = hid
            ffn = ffn + jnp.dot(hid, wd_r[j], preferred_element_type=F32)
        r2 = ALPHA * x1 + ffn
        mu = jnp.mean(r2, axis=-1, keepdims=True)
        cen = r2 - mu
        rstd2 = lax.rsqrt(jnp.mean(cen * cen, axis=-1, keepdims=True) + LN_EPS)
        xhat2 = cen * rstd2
        diff = (xhat2 * g2_r[...] + b2_r[...]) - tgt_r[...]
        loss_r[...] += 0.5 * jnp.sum(jnp.mean(diff * diff, axis=-1, keepdims=True), axis=0, keepdims=True)
        dy = diff * (1.0 / D_MODEL)
        dg2_r[...] += jnp.sum(dy * xhat2, axis=0, keepdims=True)
        db2_r[...] += jnp.sum(dy, axis=0, keepdims=True)
        dr2 = _layer_norm_bwd(dy * g2_r[...], xhat2, rstd2)
        dr2b = dr2.astype(BF16)
        dr2b_r[...] = dr2b
        dx1 = ALPHA * dr2
        for j in range(N_SHARDS):
            dhid = lax.dot_general(dr2b, wd_r[j], (((1,), (1,)), ((), ())), preferred_element_type=F32)
            hg = hg_s[j]
            hu = hu_s[j]
            sg = _sigmoid(hg)
            dhu = (dhid * (hg * sg)).astype(BF16)
            dhg = (dhid * hu * (sg * (1.0 + hg * (1.0 - sg)))).astype(BF16)
            dhg_r[j] = dhg
            dhu_r[j] = dhu
            dx1 = dx1 + lax.dot_general(dhg, wg_r[j], (((1,), (1,)), ((), ())), preferred_element_type=F32)
            dx1 = dx1 + lax.dot_general(dhu, wu_r[j], (((1,), (1,)), ((), ())), preferred_element_type=F32)
        dg1_r[...] += jnp.sum(dx1 * xhat1_v, axis=0, keepdims=True)
        db1_r[...] += jnp.sum(dx1, axis=0, keepdims=True)
        dr1_r[...] = _layer_norm_bwd(dx1 * g1_r[...], xhat1_v, rstd_r[...])

    tile = pl.BlockSpec((tm, D_MODEL), lambda i: (i, 0))
    hidden = pl.BlockSpec((N_SHARDS, tm, FFN_SHARD), lambda i: (0, i, 0))
    vec = _const((1, D_MODEL))
    hid_shape = jax.ShapeDtypeStruct((N_SHARDS, t, FFN_SHARD), BF16)
    vec_shape = jax.ShapeDtypeStruct((1, D_MODEL), F32)
    return pl.pallas_call(
        body, name="ffn_step", grid=(t // tm,),
        out_shape=(jax.ShapeDtypeStruct((1, 1), F32), jax.ShapeDtypeStruct((t, D_MODEL), F32),
                   jax.ShapeDtypeStruct((t, D_MODEL), BF16), jax.ShapeDtypeStruct((t, D_MODEL), BF16),
                   hid_shape, hid_shape, hid_shape, vec_shape, vec_shape, vec_shape, vec_shape),
        in_specs=[tile, pl.BlockSpec((tm, 1), lambda i: (i, 0)), tile, vec, vec, vec, vec,
                  _const((N_SHARDS, D_MODEL, FFN_SHARD)), _const((N_SHARDS, D_MODEL, FFN_SHARD)),
                  _const((N_SHARDS, FFN_SHARD, D_MODEL))],
        out_specs=(_const((1, 1)), tile, tile, tile, hidden, hidden, hidden, vec, vec, vec, vec),
        scratch_shapes=[pltpu.VMEM((N_SHARDS, tm, FFN_SHARD), F32), pltpu.VMEM((N_SHARDS, tm, FFN_SHARD), F32)],
        compiler_params=_params(("arbitrary",)),
    )(xhat1, rstd1, target, ln1_g, ln1_b, ln2_g, ln2_b, wg_st, wu_st, wd_st)


def _ffn_weight_grads(x1b, dr2b, hid, dhg, dhu, tk=512):
    t = x1b.shape[0]

    def body(x_r, dr_r, hid_r, dhg_r, dhu_r, gwg_r, gwu_r, gwd_r):
        @pl.when(pl.program_id(1) == 0)
        def _():
            for r in (gwg_r, gwu_r, gwd_r):
                r[...] = jnp.zeros_like(r)

        gwg_r[0] += _tdot(x_r[...], dhg_r[0])
        gwu_r[0] += _tdot(x_r[...], dhu_r[0])
        gwd_r[0] += _tdot(hid_r[0], dr_r[...])

    tile = pl.BlockSpec((tk, D_MODEL), lambda j, k: (k, 0))
    hidden = pl.BlockSpec((1, tk, FFN_SHARD), lambda j, k: (j, k, 0))
    col = pl.BlockSpec((1, D_MODEL, FFN_SHARD), lambda j, k: (j, 0, 0))
    row = pl.BlockSpec((1, FFN_SHARD, D_MODEL), lambda j, k: (j, 0, 0))
    return pl.pallas_call(
        body, name="ffn_weight_grads", grid=(N_SHARDS, t // tk),
        out_shape=(jax.ShapeDtypeStruct((N_SHARDS, D_MODEL, FFN_SHARD), F32),) * 2
        + (jax.ShapeDtypeStruct((N_SHARDS, FFN_SHARD, D_MODEL), F32),),
        in_specs=[tile, tile, hidden, hidden, hidden],
        out_specs=(col, col, row),
        compiler_params=_params(("parallel", "arbitrary")),
    )(x1b, dr2b, hid, dhg, dhu)


def _mixer_backward(dr1, proj, ya0, ya, yb, glu_w, glu_b, wso_st, conv_w8, wco_st, w_o, tm=256):
    t = dr1.shape[0]
    hb = tm // SUBLANES
    last_block = t // SUBLANES - 1

    def body(dr1_r, dr1n_r, ya0_r, ya_r, yb_r, h_r, cg_r, bg_r, ga_r, gb_r, hp_r, cgp_r, bgn_r, gbn_r,
             glu_w_r, glu_b_r, wso_r, cw_r, wco_r, wo_r,
             dya0_r, dproj_r, dbias_r, gwo_r, gwso_r, gwco_r, gglu_w_r, gglu_b_r, gconv_r):
        i = pl.program_id(0)

        @pl.when(i == 0)
        def _():
            for r in (dbias_r, gwo_r, gwso_r, gwco_r, gglu_w_r, gglu_b_r, gconv_r):
                r[...] = jnp.zeros_like(r)

        dr1_v = dr1_r[...]
        dmerged = _dot_t(dr1_v, wo_r[...])
        sa = _sigmoid(ga_r[...])
        sb = _sigmoid(gb_r[...])
        ya_v = ya_r[...]
        yb_v = yb_r[...]
        gwo_r[...] += _tdot(sa * ya_v + sb * yb_v, dr1_v)
        dya = dmerged * sa
        dyb = dmerged * sb
        dga = dmerged * ya_v * (sa * (1.0 - sa))
        dgb = dmerged * yb_v * (sb * (1.0 - sb))

        g, gelu_grad = _gelu_parts(ya0_r[...])
        s1 = _sigmoid(_dot(g, glu_w_r[...]) + glu_b_r[...])
        ya1 = g * s1
        dya1 = jnp.zeros((tm, SSM_W), F32)
        for j in range(N_SHARDS):
            dya_j = dya[:, 256 * j:256 * (j + 1)]
            gwso_r[j] += _tdot(ya1, dya_j)
            dya1 = dya1 + _dot_t(dya_j, wso_r[j])
        dz1 = dya1 * g * (s1 * (1.0 - s1))
        gglu_b_r[...] += jnp.sum(dz1, axis=0, keepdims=True)
        gglu_w_r[...] += _tdot(g, dz1)
        dya0_r[...] = (dya1 * s1 + _dot_t(dz1, glu_w_r[...])) * gelu_grad

        cw = cw_r[...]
        h = h_r[...]
        cg = cg_r[...]
        bg = bg_r[...]
        q = cg * h
        q_prev = jnp.where(i > 0, cgp_r[...] * hp_r[...], 0.0)
        q1 = _shift_down(q, q_prev, 1)
        q2 = _shift_down(q, q_prev, 2)
        z = cw[2:3, :] * q + cw[1:2, :] * q1 + cw[0:1, :] * q2
        yb0 = bg * z
        dyb0 = jnp.zeros((tm, SSM_W), F32)
        for j in range(N_SHARDS):
            dyb_j = dyb[:, 256 * j:256 * (j + 1)]
            gwco_r[j] += _tdot(yb0, dyb_j)
            dyb0 = dyb0 + _dot_t(dyb_j, wco_r[j])
        dbg = dyb0 * z
        dz = dyb0 * bg
        dyb_n = _dot_t(dr1n_r[...], wo_r[...]) * _sigmoid(gbn_r[...])
        dyb0_n = jnp.zeros((SUBLANES, SSM_W), F32)
        for j in range(N_SHARDS):
            dyb0_n = dyb0_n + _dot_t(dyb_n[:, 256 * j:256 * (j + 1)], wco_r[j])
        dz_next = jnp.where(i < pl.num_programs(0) - 1, dyb0_n * bgn_r[...], 0.0)
        dq = cw[2:3, :] * dz + cw[1:2, :] * _shift_up(dz, dz_next, 1) + cw[0:1, :] * _shift_up(dz, dz_next, 2)
        gconv_r[0:1, :] += jnp.sum(dz * q2, axis=0, keepdims=True)
        gconv_r[1:2, :] += jnp.sum(dz * q1, axis=0, keepdims=True)
        gconv_r[2:3, :] += jnp.sum(dz * q, axis=0, keepdims=True)
        dh = dq * cg
        dcg = dq * h

        dproj_r[:, 0:512] = jnp.zeros((tm, SSM_W), BF16)
        pieces = ((512, dh), (1024, dcg), (1536, dbg), (2048, dga), (3072, dgb))
        for off, val in pieces:
            w = val.shape[1]
            dproj_r[:, off:off + w] = val.astype(BF16)
            dbias_r[:, off:off + w] += jnp.sum(val, axis=0, keepdims=True)

    def col(w, c):
        return pl.BlockSpec((tm, w), lambda i: (i, c))

    def prev(c):
        return pl.BlockSpec((SUBLANES, SSM_W), lambda i: (jnp.maximum(i * hb - 1, 0), c))

    def nxt(w, c):
        return pl.BlockSpec((SUBLANES, w), lambda i: (jnp.minimum((i + 1) * hb, last_block), c))

    sh = jax.ShapeDtypeStruct
    return pl.pallas_call(
        body, name="mixer_backward", grid=(t // tm,),
        out_shape=(sh((t, SSM_W), F32), sh((t, IN_COLS), BF16), sh((1, IN_COLS), F32),
                   sh((D_MODEL, D_MODEL), F32), sh((N_SHARDS, SSM_W, 256), F32), sh((N_SHARDS, SSM_W, 256), F32),
                   sh((SSM_W, SSM_W), F32), sh((1, SSM_W), F32), sh((SUBLANES, SSM_W), F32)),
        in_specs=[col(D_MODEL, 0), nxt(D_MODEL, 0), col(SSM_W, 0), col(D_MODEL, 0), col(D_MODEL, 0),
                  col(SSM_W, 1), col(SSM_W, 2), col(SSM_W, 3), col(D_MODEL, 2), col(D_MODEL, 3),
                  prev(1), prev(2), nxt(SSM_W, 3), nxt(D_MODEL, 3),
                  _const((SSM_W, SSM_W)), _const((1, SSM_W)), _const((N_SHARDS, SSM_W, 256)),
                  _const((SUBLANES, SSM_W)), _const((N_SHARDS, SSM_W, 256)), _const((D_MODEL, D_MODEL))],
        out_specs=(col(SSM_W, 0), col(IN_COLS, 0), _const((1, IN_COLS)),
                   _const((D_MODEL, D_MODEL)), _const((N_SHARDS, SSM_W, 256)), _const((N_SHARDS, SSM_W, 256)),
                   _const((SSM_W, SSM_W)), _const((1, SSM_W)), _const((SUBLANES, SSM_W))),
        compiler_params=_params(("arbitrary",)),
    )(dr1, dr1, ya0, ya, yb, proj, proj, proj, proj, proj, proj, proj, proj, proj,
      glu_w, glu_b, wso_st, conv_w8, wco_st, w_o)


def _ssm_backward(dya0, proj, xsr, xsi, bbr, bbi, ctr, cti, d_skip, tab, dproj, tc=512):
    t = proj.shape[0]
    nb = tc // SUBLANES
    nk = t // tc

    def body(dy_r, u_r, xsr_r, xsi_r, bbr_r, bbi_r, ctr_r, cti_r, d_r, tab_r, dproj_any,
             du_r, dus_r, gbbr_r, gbbi_r, gctr_r, gcti_r, glbr_r, glbi_r, gd_r,
             gr_s, gi_s, car_r, car_i):
        del dproj_any

        @pl.when(pl.program_id(1) == 0)
        def _():
            for r in (car_r, car_i, dus_r, gbbr_r, gbbi_r, gctr_r, gcti_r, glbr_r, glbi_r, gd_r):
                r[...] = jnp.zeros_like(r)

        dy = dy_r[...]
        u = u_r[...]
        gr_s[...] = _dot(dy, ctr_r[...])
        gi_s[...] = -_dot(dy, cti_r[...])
        last_row = lax.broadcasted_iota(jnp.int32, (SUBLANES, STRIP), 0) == SUBLANES - 1

        def block(n, carry):
            cr, ci, ar, ai = carry
            rows = pl.ds(pl.multiple_of((nb - 1 - n) * SUBLANES, SUBLANES), SUBLANES)
            gr = gr_s[rows, :]
            gi = gi_s[rows, :]
            for lvl, s in ((0, 1), (1, 2), (2, 4)):
                gr, gi = _cmul_add(gr, gi, tab_r[8 + lvl], tab_r[12 + lvl],
                                   pltpu.roll(gr, SUBLANES - s, 0), pltpu.roll(gi, SUBLANES - s, 0))
            crb = jnp.broadcast_to(cr, gr.shape)
            cib = jnp.broadcast_to(ci, gi.shape)
            gr, gi = _cmul_add(gr, gi, tab_r[11], tab_r[15], crb, cib)
            gr_s[rows, :] = gr
            gi_s[rows, :] = gi
            gnr = jnp.where(last_row, crb, pltpu.roll(gr, SUBLANES - 1, 0))
            gni = jnp.where(last_row, cib, pltpu.roll(gi, SUBLANES - 1, 0))
            xr = xsr_r[rows, :]
            xi = xsi_r[rows, :]
            ar = ar + (xr * gnr + xi * gni)
            ai = ai + (xr * gni - xi * gnr)
            return gr[0:1, :], gi[0:1, :], ar, ai

        zero = jnp.zeros((SUBLANES, STRIP), F32)
        cr, ci, ar, ai = lax.fori_loop(0, nb, block, (car_r[...], car_i[...], zero, zero))
        car_r[...] = cr
        car_i[...] = ci
        glbr_r[...] += ar
        glbi_r[...] += ai
        gr = gr_s[...]
        gi = gi_s[...]
        du = _dot_t(gr, bbr_r[...]) + _dot_t(gi, bbi_r[...]) + d_r[...] * dy
        du_r[...] = du.astype(BF16)
        dus_r[...] += jnp.sum(du, axis=0, keepdims=True)
        gd_r[...] += jnp.sum(dy * u, axis=0, keepdims=True)
        gbbr_r[...] += _tdot(u, gr)
        gbbi_r[...] += _tdot(u, gi)
        gctr_r[...] += _tdot(dy, xsr_r[...])
        gcti_r[...] -= _tdot(dy, xsi_r[...])

    def rev(w):
        return pl.BlockSpec((tc, w), lambda j, k: (nk - 1 - k, j))

    strip_mat = pl.BlockSpec((128, STRIP), lambda j, k: (j, 0))
    vec = pl.BlockSpec((1, 128), lambda j, k: (0, j))
    lbacc = pl.BlockSpec((SUBLANES, STRIP), lambda j, k: (0, j))
    sh = jax.ShapeDtypeStruct
    return pl.pallas_call(
        body, name="ssm_backward", grid=(N_STRIPS, nk),
        out_shape=(sh((t, IN_COLS), BF16), sh((1, SSM_W), F32),
                   sh((SSM_W, STRIP), F32), sh((SSM_W, STRIP), F32), sh((SSM_W, STRIP), F32), sh((SSM_W, STRIP), F32),
                   sh((SUBLANES, STATE_COLS), F32), sh((SUBLANES, STATE_COLS), F32), sh((1, SSM_W), F32)),
        in_specs=[rev(128), rev(128), rev(STRIP), rev(STRIP),
                  strip_mat, strip_mat, strip_mat, strip_mat, vec,
                  pl.BlockSpec((16, SUBLANES, STRIP), lambda j, k: (0, 0, j)), ANY],
        out_specs=(rev(128), vec, strip_mat, strip_mat, strip_mat, strip_mat, lbacc, lbacc, vec),
        scratch_shapes=[pltpu.VMEM((tc, STRIP), F32), pltpu.VMEM((tc, STRIP), F32),
                        pltpu.VMEM((1, STRIP), F32), pltpu.VMEM((1, STRIP), F32)],
        input_output_aliases={10: 0},
        compiler_params=_params(("parallel", "arbitrary")),
    )(dya0, proj, xsr, xsi, bbr, bbi, ctr, cti, d_skip, tab, dproj)


def _input_grad(dr1, dproj, w_in_st, tm=256):
    t = dr1.shape[0]

    def body(dr1_r, dp_r, w_r, dx_r):
        acc = ALPHA * dr1_r[...]
        for j in range(N_SHARDS):
            acc = acc + lax.dot_general(dp_r[:, D_MODEL * j:D_MODEL * (j + 1)], w_r[j],
                                        (((1,), (1,)), ((), ())), preferred_element_type=F32)
        dx_r[...] = acc

    return pl.pallas_call(
        body, name="input_grad", grid=(t // tm,),
        out_shape=jax.ShapeDtypeStruct((t, D_MODEL), F32),
        in_specs=[pl.BlockSpec((tm, D_MODEL), lambda i: (i, 0)), pl.BlockSpec((tm, IN_COLS), lambda i: (i, 0)),
                  _const((N_SHARDS, D_MODEL, D_MODEL))],
        out_specs=pl.BlockSpec((tm, D_MODEL), lambda i: (i, 0)),
        compiler_params=_params(("parallel",)),
    )(dr1, dproj, w_in_st)


def _in_weight_grad(x, dproj, tk=512):
    t = x.shape[0]

    def body(x_r, dp_r, gw_r):
        @pl.when(pl.program_id(1) == 0)
        def _():
            gw_r[...] = jnp.zeros_like(gw_r)

        gw_r[0] += _tdot(x_r[...], dp_r[...])

    return pl.pallas_call(
        body, name="in_weight_grad", grid=(N_SHARDS, t // tk),
        out_shape=jax.ShapeDtypeStruct((N_SHARDS, D_MODEL, D_MODEL), F32),
        in_specs=[pl.BlockSpec((tk, D_MODEL), lambda j, k: (k, 0)), pl.BlockSpec((tk, D_MODEL), lambda j, k: (k, j))],
        out_specs=pl.BlockSpec((1, D_MODEL, D_MODEL), lambda j, k: (j, 0, 0)),
        compiler_params=_params(("parallel", "arbitrary")),
    )(x, dproj)


def _local_step(x, target, small, wts):
    lr, li = small["ssm_lambda_re"][0], small["ssm_lambda_im"][0]
    ldt = small["ssm_log_dt"][0][:, None]
    rep16 = lambda a: jnp.broadcast_to(a[:, None, :], (N_GROUPS, GROUP_C, a.shape[-1])).reshape(SSM_W, a.shape[-1])
    lr16, li16 = rep16(lr), rep16(li)
    ldt16 = rep16(jnp.broadcast_to(ldt, (N_GROUPS, N_STATE)))
    brt = small["ssm_b_re"][0].transpose(0, 2, 1).reshape(SSM_W, N_STATE)
    bit = small["ssm_b_im"][0].transpose(0, 2, 1).reshape(SSM_W, N_STATE)
    cre = small["ssm_c_re"][0].reshape(SSM_W, N_STATE)
    cim = small["ssm_c_im"][0].reshape(SSM_W, N_STATE)
    disc = (lr, li, ldt, lr16, li16, ldt16, brt, bit)

    pwr, pwi, bbr, bbi, ctr, cti = _ssm_prepare(*disc, cre, cim)
    tab = _scan_tables(pwr, pwi)
    conv_w8 = jnp.pad(wts["conv_w"][:, :3, :].transpose(1, 0, 2).reshape(3, SSM_W), ((0, SUBLANES - 3), (0, 0)))
    w_o = wts["w_o"].reshape(D_MODEL, D_MODEL)
    glu_w = wts["glu_w"].reshape(SSM_W, SSM_W)

    proj = _in_proj(x, wts["w_in"], small["b_in"])
    xsr, xsi, ya0 = _ssm_forward(proj, bbr, bbi, ctr, cti, small["ssm_d"], tab)
    xhat1, rstd1, ya, yb = _mixer_forward(x, proj, ya0, glu_w, small["glu_b"], wts["w_ssm_out"], conv_w8,
                                          wts["w_conv_out"], w_o)
    (loss, dr1, x1b, dr2b, hid, dhg, dhu, g_ln2_g, g_ln2_b, g_ln1_g, g_ln1_b) = _ffn_step(
        xhat1, rstd1, target, small["ln1_g"], small["ln1_b"], small["ln2_g"], small["ln2_b"],
        wts["w_gate"], wts["w_up"], wts["w_down"])
    g_wg, g_wu, g_wd = _ffn_weight_grads(x1b, dr2b, hid, dhg, dhu)
    (dya0, dproj, dbias, g_wo, g_wso, g_wco, g_glu_w, g_glu_b, g_conv8) = _mixer_backward(
        dr1, proj, ya0, ya, yb, glu_w, small["glu_b"], wts["w_ssm_out"], conv_w8, wts["w_conv_out"], w_o)
    (dproj, dus, gbbr, gbbi, gctr, gcti, glbr, glbi, g_d) = _ssm_backward(
        dya0, proj, xsr, xsi, bbr, bbi, ctr, cti, small["ssm_d"], tab, dproj)
    g_lr, g_li, g_ldt, g_brt, g_bit, g_cre, g_cim = _ssm_param_grads(
        *disc, glbr.reshape(SUBLANES, N_GROUPS, N_STATE), glbi.reshape(SUBLANES, N_GROUPS, N_STATE),
        gbbr, gbbi, gctr, gcti)
    dx = _input_grad(dr1, dproj, wts["w_in"])
    g_w_in = _in_weight_grad(x, dproj)

    big = {
        "w_in": g_w_in,
        "glu_w": g_glu_w.reshape(N_SHARDS, 128, SSM_W),
        "w_ssm_out": g_wso,
        "conv_w": jnp.pad(g_conv8[:3].reshape(3, N_SHARDS, 128).transpose(1, 0, 2),
                          ((0, 0), (0, SUBLANES - 3), (0, 0))),
        "w_conv_out": g_wco,
        "w_o": g_wo.reshape(N_SHARDS, 256, D_MODEL),
        "w_gate": g_wg, "w_up": g_wu, "w_down": g_wd,
    }
    unT = lambda a: a.reshape(N_GROUPS, GROUP_C, N_STATE).transpose(0, 2, 1)[None]
    little = {
        "b_in": jnp.concatenate([dus, dbias[:, SSM_W:]], axis=1),
        "ssm_lambda_re": g_lr[None], "ssm_lambda_im": g_li[None], "ssm_log_dt": g_ldt.reshape(1, N_GROUPS),
        "ssm_b_re": unT(g_brt), "ssm_b_im": unT(g_bit),
        "ssm_c_re": g_cre.reshape(1, N_GROUPS, GROUP_C, N_STATE), "ssm_c_im": g_cim.reshape(1, N_GROUPS, GROUP_C, N_STATE),
        "ssm_d": g_d, "glu_b": g_glu_b,
        "ln1_g": g_ln1_g, "ln1_b": g_ln1_b, "ln2_g": g_ln2_g, "ln2_b": g_ln2_b,
    }
    return loss, dx, big, little


BIG = ("w_in", "glu_w", "w_ssm_out", "w_conv_out", "w_o", "w_gate", "w_up", "w_down")
SMALL = ("b_in", "ssm_lambda_re", "ssm_lambda_im", "ssm_log_dt", "ssm_b_re", "ssm_b_im", "ssm_c_re", "ssm_c_im",
         "ssm_d", "glu_b", "ln1_g", "ln1_b", "ln2_g", "ln2_b")
WEIGHTS = ("w_in", "b_in", "ssm_lambda_re", "ssm_lambda_im", "ssm_log_dt", "ssm_b_re", "ssm_b_im", "ssm_c_re",
           "ssm_c_im", "ssm_d", "glu_w", "glu_b", "w_ssm_out", "conv_w", "w_conv_out", "w_o", "ln1_g", "ln1_b",
           "w_gate", "w_up", "w_down", "ln2_g", "ln2_b")


def _place():
    x, y, c = lax.axis_index("x"), lax.axis_index("y"), lax.axis_index("c")
    chips = [(1 - x, y), (x, 1 - y), (1 - x, 1 - y)]
    return x, y, c, chips


def _shard_of(chip):
    return 2 * chip[0] + chip[1]


def _remote(src, dst, send_sem, recv_sem, to):
    return pltpu.make_async_remote_copy(src_ref=src, dst_ref=dst, send_sem=send_sem, recv_sem=recv_sem,
                                        device_id=to, device_id_type=MESH)


def _gather_weights(shards):
    n = len(shards)
    n_big = n - 1

    def body(*refs):
        src, dst = refs[:n], refs[n:2 * n]
        local_sem, send_sem, recv_sem, fsend_sem, frecv_sem = refs[2 * n:]
        x, y, c, chips = _place()
        me = _shard_of((x, y))
        sibling = (x, y, 1 - c)

        def half(a, which):
            r2 = shards[a].shape[0] // 2
            return pl.ds(pl.multiple_of(which * r2, 16), r2)

        local = [pltpu.make_async_copy(src[a], dst[a].at[me], local_sem.at[a]) for a in range(n)]
        for cp in local:
            cp.start()
        sends = []
        for a in range(n):
            for k, chip in enumerate(chips):
                if a < n_big:
                    cp = _remote(src[a].at[half(a, c)], dst[a].at[me, half(a, c)],
                                 send_sem.at[a, k], recv_sem.at[a, k], (*chip, c))
                else:
                    cp = _remote(src[a], dst[a].at[me], send_sem.at[a, k], recv_sem.at[a, k], (*chip, c))
                cp.start()
                sends.append(cp)
        for a in range(n):
            for k, chip in enumerate(chips):
                got = _shard_of(chip)
                if a < n_big:
                    rows = dst[a].at[got, half(a, c)]
                    _remote(rows, rows, send_sem.at[a, k], recv_sem.at[a, k], sibling).wait_recv()
                    cp = _remote(rows, rows, fsend_sem.at[a, k], frecv_sem.at[a, k], sibling)
                    cp.start()
                    sends.append(cp)
                else:
                    _remote(dst[a].at[got], dst[a].at[got], send_sem.at[a, k], recv_sem.at[a, k], sibling).wait_recv()
        for a in range(n_big):
            for k, chip in enumerate(chips):
                rows = dst[a].at[_shard_of(chip), half(a, 1 - c)]
                _remote(rows, rows, fsend_sem.at[a, k], frecv_sem.at[a, k], sibling).wait_recv()
        for cp in sends:
            cp.wait_send()
        for cp in local:
            cp.wait()

    return pl.pallas_call(
        body, name="gather_weights",
        out_shape=tuple(jax.ShapeDtypeStruct((N_SHARDS,) + s.shape, s.dtype) for s in shards),
        in_specs=[ANY] * n, out_specs=(ANY,) * n,
        scratch_shapes=[pltpu.SemaphoreType.DMA((n,)), pltpu.SemaphoreType.DMA((n, 3)), pltpu.SemaphoreType.DMA((n, 3)),
                        pltpu.SemaphoreType.DMA((n, 3)), pltpu.SemaphoreType.DMA((n, 3))],
    )(*shards)


def _swap_with_sibling(big, small):
    nb, n = len(big), len(big) + len(small)
    arrays = list(big) + list(small)

    def body(*refs):
        src, dst = refs[:n], refs[n:2 * n]
        send_sem, recv_sem = refs[2 * n:]
        x, y, c, _ = _place()
        sibling = (x, y, 1 - c)
        copies = []
        for a in range(n):
            if a < nb:
                r2 = arrays[a].shape[1] // 2
                part = src[a].at[:, pl.ds(pl.multiple_of((1 - c) * r2, SUBLANES), r2), :]
            else:
                part = src[a]
            cp = _remote(part, dst[a], send_sem.at[a], recv_sem.at[a], sibling)
            cp.start()
            copies.append(cp)
        for cp in copies:
            cp.wait()

    out_shape = tuple(jax.ShapeDtypeStruct((N_SHARDS, g.shape[1] // 2, g.shape[2]), g.dtype) for g in big)
    out_shape += tuple(jax.ShapeDtypeStruct(g.shape, g.dtype) for g in small)
    out = pl.pallas_call(
        body, name="swap_with_sibling", out_shape=out_shape, in_specs=[ANY] * n, out_specs=(ANY,) * n,
        scratch_shapes=[pltpu.SemaphoreType.DMA((n,)), pltpu.SemaphoreType.DMA((n,))],
    )(*arrays)
    return out[:nb], out[nb:]


def _scatter_to_chips(big, conv, small):
    slabbed = list(big) + [conv]
    ns, n = len(slabbed), len(slabbed) + len(small)
    arrays = slabbed + list(small)

    def body(*refs):
        src, dst = refs[:n], refs[n:2 * n]
        local_sem, send_sem, recv_sem = refs[2 * n:]
        x, y, c, chips = _place()
        me = _shard_of((x, y))
        copies = []
        for a in range(n):
            own = src[a].at[me] if a < ns else src[a]
            cp = pltpu.make_async_copy(own, dst[a].at[me], local_sem.at[a])
            cp.start()
            copies.append(cp)
        for a in range(n):
            for k, chip in enumerate(chips):
                part = src[a].at[_shard_of(chip)] if a < ns else src[a]
                cp = _remote(part, dst[a].at[me], send_sem.at[a, k], recv_sem.at[a, k], (*chip, c))
                cp.start()
                copies.append(cp)
        for cp in copies:
            cp.wait()

    out_shape = tuple(jax.ShapeDtypeStruct(g.shape, g.dtype) for g in slabbed)
    out_shape += tuple(jax.ShapeDtypeStruct((N_SHARDS,) + g.shape, g.dtype) for g in small)
    out = pl.pallas_call(
        body, name="scatter_to_chips", out_shape=out_shape, in_specs=[ANY] * n, out_specs=(ANY,) * n,
        scratch_shapes=[pltpu.SemaphoreType.DMA((n,)), pltpu.SemaphoreType.DMA((n, 3)), pltpu.SemaphoreType.DMA((n, 3))],
    )(*arrays)
    return out[:len(big)], out[len(big)], out[ns:]


def _join_halves(halves):
    n = len(halves)

    def body(*refs):
        src, dst = refs[:n], refs[n:2 * n]
        local_sem, send_sem, recv_sem = refs[2 * n:]
        x, y, c, _ = _place()
        sibling = (x, y, 1 - c)
        copies = []
        for a in range(n):
            r2 = halves[a].shape[0]
            rows = dst[a].at[pl.ds(pl.multiple_of(c * r2, SUBLANES), r2), :]
            cp = pltpu.make_async_copy(src[a], rows, local_sem.at[a])
            cp.start()
            copies.append(cp)
            cp = _remote(src[a], rows, send_sem.at[a], recv_sem.at[a], sibling)
            cp.start()
            copies.append(cp)
        for cp in copies:
            cp.wait()

    return pl.pallas_call(
        body, name="join_halves",
        out_shape=tuple(jax.ShapeDtypeStruct((2 * h.shape[0], h.shape[1]), h.dtype) for h in halves),
        in_specs=[ANY] * n, out_specs=(ANY,) * n,
        scratch_shapes=[pltpu.SemaphoreType.DMA((n,))] * 3,
    )(*halves)


def _row_chunk(rows):
    for cand in (256, 176, 128, 64):
        if rows % cand == 0:
            return cand
    return rows


def _add_own_half(stack, received, c):
    _, r2, cols = received.shape

    def body(c_ref, a_r, b_r, o_r):
        del c_ref
        o_r[...] = a_r[...] + b_r[...]

    return pl.pallas_call(
        body, name="add_own_half",
        grid_spec=pltpu.PrefetchScalarGridSpec(
            num_scalar_prefetch=1, grid=(N_SHARDS,),
            in_specs=[pl.BlockSpec((1, r2, cols), lambda s, c_ref: (s, c_ref[0], 0)),
                      pl.BlockSpec((1, r2, cols), lambda s, c_ref: (s, 0, 0))],
            out_specs=pl.BlockSpec((1, r2, cols), lambda s, c_ref: (s, 0, 0))),
        out_shape=jax.ShapeDtypeStruct(received.shape, F32),
        compiler_params=_params(("parallel",)),
    )(c, stack, received)


def _sum_slots(slots):
    _, rows, cols = slots.shape
    rc = _row_chunk(rows)

    def body(s_r, o_r):
        o_r[...] = ((s_r[0] + s_r[1]) + s_r[2]) + s_r[3]

    return pl.pallas_call(
        body, name="sum_slots", grid=(rows // rc,),
        out_shape=jax.ShapeDtypeStruct((rows, cols), F32),
        in_specs=[pl.BlockSpec((N_SHARDS, rc, cols), lambda i: (0, i, 0))],
        out_specs=pl.BlockSpec((rc, cols), lambda i: (i, 0)),
        compiler_params=_params(("parallel",)),
    )(slots)


def _small_pair_sums(mine, theirs):
    n = len(mine)

    def body(*refs):
        for a in range(n):
            refs[2 * n + a][...] = refs[a][...] + refs[n + a][...]

    vm = pl.BlockSpec(memory_space=pltpu.VMEM)
    return pl.pallas_call(
        body, name="small_pair_sums", out_shape=tuple(jax.ShapeDtypeStruct(g.shape, g.dtype) for g in mine),
        in_specs=[vm] * (2 * n), out_specs=(vm,) * n,
        compiler_params=pltpu.CompilerParams(vmem_limit_bytes=VMEM_LIMIT),
    )(*mine, *theirs)


def _adam_math(w, g, m, v):
    m = ADAM_B1 * m + (1.0 - ADAM_B1) * g
    v = ADAM_B2 * v + (1.0 - ADAM_B2) * (g * g)
    m_hat = m / (1.0 - ADAM_B1 ** ADAM_STEP)
    v_hat = v / (1.0 - ADAM_B2 ** ADAM_STEP)
    delta = -ADAM_LR * (m_hat / (jnp.sqrt(v_hat) + ADAM_EPS) + ADAM_WD * w)
    return delta, m, v


def _adam_small(slots, ws, ms, vs):
    n = len(slots)

    def body(*refs):
        for a in range(n):
            s_r, w_r, m_r, v_r = (refs[i * n + a] for i in range(4))
            g_r, d_r, nm_r, nv_r = (refs[(4 + i) * n + a] for i in range(4))
            g = ((s_r[0] + s_r[1]) + s_r[2]) + s_r[3]
            g_r[...] = g
            d_r[...], nm_r[...], nv_r[...] = _adam_math(w_r[...], g, m_r[...], v_r[...])

    vm = pl.BlockSpec(memory_space=pltpu.VMEM)
    shapes = tuple(jax.ShapeDtypeStruct(w.shape, F32) for w in ws)
    out = pl.pallas_call(
        body, name="adam_small", out_shape=shapes * 4, in_specs=[vm] * (4 * n), out_specs=(vm,) * (4 * n),
        compiler_params=pltpu.CompilerParams(vmem_limit_bytes=VMEM_LIMIT),
    )(*slots, *ws, *ms, *vs)
    return out[:n], out[n:2 * n], out[2 * n:3 * n], out[3 * n:]


def _adam_big(w, g, m, v):
    rows, cols = g.shape
    rc = _row_chunk(rows)

    def body(w_r, g_r, m_r, v_r, d_r, nm_r, nv_r):
        d_r[...], nm_r[...], nv_r[...] = _adam_math(w_r[...], g_r[...], m_r[...], v_r[...])

    tile = pl.BlockSpec((rc, cols), lambda i: (i, 0))
    shape = jax.ShapeDtypeStruct((rows, cols), F32)
    return pl.pallas_call(
        body, name="adam_big", grid=(rows // rc,), out_shape=(shape,) * 3,
        in_specs=[tile] * 4, out_specs=(tile,) * 3, compiler_params=_params(("parallel",)),
    )(w.reshape(rows, cols), g, m.reshape(rows, cols), v.reshape(rows, cols))


def kernel(x, w_in, b_in, ssm_lambda_re, ssm_lambda_im, ssm_log_dt, ssm_b_re, ssm_b_im, ssm_c_re, ssm_c_im, ssm_d, glu_w, glu_b, w_ssm_out, conv_w, w_conv_out, w_o, ln1_g, ln1_b, w_gate, w_up, w_down, ln2_g, ln2_b, loss_target, m_w_in, m_b_in, m_ssm_lambda_re, m_ssm_lambda_im, m_ssm_log_dt, m_ssm_b_re, m_ssm_b_im, m_ssm_c_re, m_ssm_c_im, m_ssm_d, m_glu_w, m_glu_b, m_w_ssm_out, m_conv_w, m_w_conv_out, m_w_o, m_ln1_g, m_ln1_b, m_w_gate, m_w_up, m_w_down, m_ln2_g, m_ln2_b, v_w_in, v_b_in, v_ssm_lambda_re, v_ssm_lambda_im, v_ssm_log_dt, v_ssm_b_re, v_ssm_b_im, v_ssm_c_re, v_ssm_c_im, v_ssm_d, v_glu_w, v_glu_b, v_w_ssm_out, v_conv_w, v_w_conv_out, v_w_o, v_ln1_g, v_ln1_b, v_w_gate, v_w_up, v_w_down, v_ln2_g, v_ln2_b):
    given = dict(locals())
    w = {n: given[n] for n in WEIGHTS}
    m = {n: given["m_" + n] for n in WEIGHTS}
    v = {n: given["v_" + n] for n in WEIGHTS}

    shards = [w[n][0].astype(BF16) for n in BIG]
    shards.append(jnp.pad(conv_w[0], ((0, SUBLANES - 3), (0, 0))))
    stacks = _gather_weights(shards)
    wts = dict(zip(BIG + ("conv_w",), stacks))

    small = {n: w[n] for n in SMALL}
    loss, dx, g_big, g_small = _local_step(x[0], loss_target[0], small, wts)

    c = lax.axis_index("c")
    c_arr = jnp.reshape(c, (1,)).astype(jnp.int32)
    mine_small = [g_big["conv_w"]] + [g_small[n] for n in SMALL]
    got_big, got_small = _swap_with_sibling([g_big[n] for n in BIG], mine_small)
    chip_big = [_add_own_half(g_big[n], r, c_arr) for n, r in zip(BIG, got_big)]
    chip_small = _small_pair_sums(mine_small, got_small)
    slots_big, slots_conv, slots_small = _scatter_to_chips(chip_big, chip_small[0], chip_small[1:])
    halves = [_sum_slots(s) for s in slots_big]
    grads_big = dict(zip(BIG, _join_halves(halves)))

    small_names = ("conv_w",) + SMALL
    conv_slots = slots_conv[:, :3, :][:, None]
    gs, ds, nms, nvs = _adam_small([conv_slots] + list(slots_small), [w[n] for n in small_names],
                                   [m[n] for n in small_names], [v[n] for n in small_names])
    grad, delta, new_m, new_v = {}, {}, {}, {}
    for i, n in enumerate(small_names):
        grad[n], delta[n], new_m[n], new_v[n] = gs[i], ds[i], nms[i], nvs[i]
    for n in BIG:
        shape = w[n].shape
        d_, nm_, nv_ = _adam_big(w[n], grads_big[n], m[n], v[n])
        grad[n] = grads_big[n].reshape(shape)
        delta[n], new_m[n], new_v[n] = d_.reshape(shape), nm_.reshape(shape), nv_.reshape(shape)

    loss_total = lax.psum(loss[0, 0], ("x", "y", "c"))
    return (loss_total, dx[None], *[grad[n] for n in WEIGHTS], *[delta[n] for n in WEIGHTS],
            *[new_m[n] for n in WEIGHTS], *[new_v[n] for n in WEIGHTS])
```

```python
import functools
import math

import jax
import jax.numpy as jnp
from jax import lax
from jax.experimental import pallas as pl
from jax.experimental.pallas import tpu as pltpu

F32 = jnp.float32
BF16 = jnp.bfloat16

D_MODEL = 1024
IN_COLS = 4096
SSM_W = 512
N_GROUPS = 32
N_STATE = 64
GROUP_C = 16
STATE_COLS = N_GROUPS * N_STATE
STRIP = 512
N_STRIPS = STATE_COLS // STRIP
FFN_SHARD = 704
N_SHARDS = 4
ALPHA = 2.0 ** 0.25
LN_EPS = 1e-5
GELU_K = math.sqrt(2.0 / math.pi)
GELU_C = 0.044715

ADAM_LR = 0.001
ADAM_B1 = 0.9
ADAM_B2 = 0.999
ADAM_EPS = 1e-08
ADAM_WD = 0.01
ADAM_STEP = 10

V7X_VMEM_BYTES = 64 * 1024 * 1024
VMEM_LIMIT = V7X_VMEM_BYTES - 8 * 1024 * 1024
SUBLANES = 8

MESH = pl.DeviceIdType.MESH
ANY = pl.BlockSpec(memory_space=pl.ANY)


def _dot(a, b):
    return jnp.dot(a.astype(BF16), b.astype(BF16), preferred_element_type=F32)


def _dot_t(a, b):
    return lax.dot_general(a.astype(BF16), b.astype(BF16), (((1,), (1,)), ((), ())),
                           preferred_element_type=F32)


def _tdot(a, b):
    return lax.dot_general(a.astype(BF16), b.astype(BF16), (((0,), (0,)), ((), ())),
                           preferred_element_type=F32)


def _sigmoid(v):
    return 1.0 / (1.0 + jnp.exp(-v))


def _split3(v):
    hi = v.astype(BF16)
    r1 = v - hi.astype(F32)
    mid = r1.astype(BF16)
    lo = (r1 - mid.astype(F32)).astype(BF16)
    return hi, mid, lo


def _exact_dot(v, sel):
    hi, mid, lo = _split3(v)
    return (jnp.dot(hi, sel, preferred_element_type=F32)
            + jnp.dot(mid, sel, preferred_element_type=F32)
            + jnp.dot(lo, sel, preferred_element_type=F32))


def _const(shape):
    nd = len(shape)
    return pl.BlockSpec(shape, lambda *_: (0,) * nd)


def _params(sem, vmem=VMEM_LIMIT):
    return pltpu.CompilerParams(dimension_semantics=sem, vmem_limit_bytes=vmem)


def _gelu_parts(v):
    inner = GELU_K * (v + GELU_C * v * v * v)
    t = jnp.tanh(inner)
    g = 0.5 * v * (1.0 + t)
    dg = 0.5 * (1.0 + t) + 0.5 * v * (1.0 - t * t) * GELU_K * (1.0 + 3.0 * GELU_C * v * v)
    return g, dg


def _ssm_discretise(lr, li, ldt, lr16, li16, ldt16, brt, bit):
    def lam_bar(lr_, li_, ldt_):
        dt = jnp.exp(ldt_)
        mag = jnp.exp(lr_ * dt)
        return mag * jnp.cos(li_ * dt), mag * jnp.sin(li_ * dt)

    lb_re, lb_im = lam_bar(lr, li, ldt)
    l16_re, l16_im = lam_bar(lr16, li16, ldt16)
    den = lr16 * lr16 + li16 * li16
    num_re = l16_re - 1.0
    fr = (num_re * lr16 + l16_im * li16) / den
    fi = (l16_im * lr16 - num_re * li16) / den
    bb_re = fr * brt - fi * bit
    bb_im = fr * bit + fi * brt
    return lb_re, lb_im, bb_re, bb_im


def _strip_selectors():
    p = lax.broadcasted_iota(jnp.int32, (N_STATE, STRIP), 0)
    col = lax.broadcasted_iota(jnp.int32, (N_STATE, STRIP), 1)
    rep = ((col & (N_STATE - 1)) == p).astype(BF16)
    row = lax.broadcasted_iota(jnp.int32, (SSM_W, STRIP), 0)
    col2 = lax.broadcasted_iota(jnp.int32, (SSM_W, STRIP), 1)
    mask = (((row >> 4) & 7) == (col2 >> 6))
    return rep, mask


def _ssm_prepare(lr, li, ldt, lr16, li16, ldt16, brt, bit, cre, cim):
    def body(lr_r, li_r, ldt_r, lr16_r, li16_r, ldt16_r, brt_r, bit_r, cre_r, cim_r,
             pwr_r, pwi_r, bbr_r, bbi_r, ctr_r, cti_r):
        lb_re, lb_im, bb_re, bb_im = _ssm_discretise(
            lr_r[...], li_r[...], ldt_r[...], lr16_r[...], li16_r[...], ldt16_r[...], brt_r[...], bit_r[...])
        pr, pi_ = lb_re, lb_im
        pwr_r[0] = pr
        pwi_r[0] = pi_
        for k in range(1, SUBLANES):
            pr, pi_ = pr * lb_re - pi_ * lb_im, pr * lb_im + pi_ * lb_re
            pwr_r[k] = pr
            pwi_r[k] = pi_
        rep, mask = _strip_selectors()
        for src, dst in ((bb_re, bbr_r), (bb_im, bbi_r), (cre_r[...], ctr_r), (cim_r[...], cti_r)):
            wide = jnp.dot(src.astype(BF16), rep, preferred_element_type=F32)
            dst[...] = jnp.where(mask, wide, 0.0).astype(BF16)

    vm = pl.BlockSpec(memory_space=pltpu.VMEM)
    return pl.pallas_call(
        body, name="ssm_prepare",
        out_shape=(jax.ShapeDtypeStruct((SUBLANES, N_GROUPS, N_STATE), F32),) * 2
        + (jax.ShapeDtypeStruct((SSM_W, STRIP), BF16),) * 4,
        in_specs=[vm] * 10, out_specs=(vm,) * 6,
    )(lr, li, ldt, lr16, li16, ldt16, brt, bit, cre, cim)


def _scan_tables(pwr, pwi):
    pr = pwr.reshape(SUBLANES, STATE_COLS)
    pi_ = pwi.reshape(SUBLANES, STATE_COLS)
    row = jnp.arange(SUBLANES)[:, None]
    z = jnp.zeros((SUBLANES, STATE_COLS), F32)

    def lvl(p, k, keep):
        return jnp.where(keep, jnp.broadcast_to(p[k - 1][None], (SUBLANES, STATE_COLS)), z)

    fwd_re = [lvl(pr, 1, row >= 1), lvl(pr, 2, row >= 2), lvl(pr, 4, row >= 4), pr]
    fwd_im = [lvl(pi_, 1, row >= 1), lvl(pi_, 2, row >= 2), lvl(pi_, 4, row >= 4), pi_]
    rev_re = [lvl(pr, 1, row <= 6), lvl(pr, 2, row <= 5), lvl(pr, 4, row <= 3), pr[::-1]]
    rev_im = [-lvl(pi_, 1, row <= 6), -lvl(pi_, 2, row <= 5), -lvl(pi_, 4, row <= 3), -pi_[::-1]]
    return jnp.stack(fwd_re + fwd_im + rev_re + rev_im)


def _ssm_param_grads(lr, li, ldt, lr16, li16, ldt16, brt, bit, dlbr, dlbi, dbbr, dbbi, dctr, dcti):
    def body(lr_r, li_r, ldt_r, lr16_r, li16_r, ldt16_r, brt_r, bit_r,
             dlbr_r, dlbi_r, dbbr_r, dbbi_r, dctr_r, dcti_r,
             glr_r, gli_r, gldt_r, gbrt_r, gbit_r, gcre_r, gcim_r):
        rep, mask = _strip_selectors()

        def fold(acc):
            return sum(lax.dot_general(t, rep, (((1,), (1,)), ((), ())), preferred_element_type=F32)
                       for t in _split3(jnp.where(mask, acc, 0.0)))

        g_lb_re = jnp.sum(dlbr_r[...], axis=0)
        g_lb_im = jnp.sum(dlbi_r[...], axis=0)
        g_bb_re = fold(dbbr_r[...])
        g_bb_im = fold(dbbi_r[...])
        gcre_r[...] = fold(dctr_r[...])
        gcim_r[...] = fold(dcti_r[...])
        prim = (lr_r[...], li_r[...], ldt_r[...], lr16_r[...], li16_r[...], ldt16_r[...], brt_r[...], bit_r[...])
        _, vjp = jax.vjp(_ssm_discretise, *prim)
        g_lr, g_li, g_ldt, g_lr16, g_li16, g_ldt16, g_brt, g_bit = vjp((g_lb_re, g_lb_im, g_bb_re, g_bb_im))
        grp = lax.broadcasted_iota(jnp.int32, (N_GROUPS, SSM_W), 0)
        rw = lax.broadcasted_iota(jnp.int32, (N_GROUPS, SSM_W), 1)
        gsum = ((rw >> 4) == grp).astype(BF16)

        def group_sum(v):
            return sum(jnp.dot(gsum, t, preferred_element_type=F32) for t in _split3(v))

        glr_r[...] = g_lr + group_sum(g_lr16)
        gli_r[...] = g_li + group_sum(g_li16)
        gldt_r[...] = g_ldt + jnp.sum(group_sum(g_ldt16), axis=1, keepdims=True)
        gbrt_r[...] = g_brt
        gbit_r[...] = g_bit

    vm = pl.BlockSpec(memory_space=pltpu.VMEM)
    gp = jax.ShapeDtypeStruct((N_GROUPS, N_STATE), F32)
    gb = jax.ShapeDtypeStruct((SSM_W, N_STATE), F32)
    return pl.pallas_call(
        body, name="ssm_param_grads",
        out_shape=(gp, gp, jax.ShapeDtypeStruct((N_GROUPS, 1), F32), gb, gb, gb, gb),
        in_specs=[vm] * 14, out_specs=(vm,) * 7,
    )(lr, li, ldt, lr16, li16, ldt16, brt, bit, dlbr, dlbi, dbbr, dbbi, dctr, dcti)


def _in_proj(x, w_in_st, b_in):
    t = x.shape[0]
    tm = 512

    def body(x_r, w_r, b_r, o_r):
        o_r[...] = _dot(x_r[...], w_r[0]) + b_r[...]

    return pl.pallas_call(
        body, name="in_proj", grid=(t // tm, N_SHARDS),
        out_shape=jax.ShapeDtypeStruct((t, IN_COLS), F32),
        in_specs=[pl.BlockSpec((tm, D_MODEL), lambda i, j: (i, 0)),
                  pl.BlockSpec((1, D_MODEL, D_MODEL), lambda i, j: (j, 0, 0)),
                  pl.BlockSpec((1, D_MODEL), lambda i, j: (0, j))],
        out_specs=pl.BlockSpec((tm, D_MODEL), lambda i, j: (i, j)),
        compiler_params=_params(("parallel", "arbitrary")),
    )(x, w_in_st, b_in)


def _cmul_add(xr, xi, mr, mi, sr, si):
    return xr + (mr * sr - mi * si), xi + (mr * si + mi * sr)


def _ssm_forward(proj, bbr, bbi, ctr, cti, d_skip, tab, tc=512):
    t = proj.shape[0]
    nb = tc // SUBLANES

    def body(u_r, bbr_r, bbi_r, ctr_r, cti_r, d_r, tab_r, xsr_r, xsi_r, y_r, car_r, car_i):
        @pl.when(pl.program_id(1) == 0)
        def _():
            car_r[...] = jnp.zeros_like(car_r)
            car_i[...] = jnp.zeros_like(car_i)

        u = u_r[...]
        xsr_r[...] = _dot(u, bbr_r[...])
        xsi_r[...] = _dot(u, bbi_r[...])

        def block(b, carry):
            cr, ci = carry
            rows = pl.ds(pl.multiple_of(b * SUBLANES, SUBLANES), SUBLANES)
            xr = xsr_r[rows, :]
            xi = xsi_r[rows, :]
            for lvl, s in ((0, 1), (1, 2), (2, 4)):
                xr, xi = _cmul_add(xr, xi, tab_r[lvl], tab_r[4 + lvl],
                                   pltpu.roll(xr, s, 0), pltpu.roll(xi, s, 0))
            xr, xi = _cmul_add(xr, xi, tab_r[3], tab_r[7],
                               jnp.broadcast_to(cr, xr.shape), jnp.broadcast_to(ci, xi.shape))
            xsr_r[rows, :] = xr
            xsi_r[rows, :] = xi
            return xr[SUBLANES - 1:SUBLANES, :], xi[SUBLANES - 1:SUBLANES, :]

        cr, ci = lax.fori_loop(0, nb, block, (car_r[...], car_i[...]))
        car_r[...] = cr
        car_i[...] = ci
        y_r[...] = _dot_t(xsr_r[...], ctr_r[...]) - _dot_t(xsi_r[...], cti_r[...]) + d_r[...] * u

    strip_mat = pl.BlockSpec((128, STRIP), lambda j, k: (j, 0))
    states = pl.BlockSpec((tc, STRIP), lambda j, k: (k, j))
    return pl.pallas_call(
        body, name="ssm_forward", grid=(N_STRIPS, t // tc),
        out_shape=(jax.ShapeDtypeStruct((t, STATE_COLS), F32), jax.ShapeDtypeStruct((t, STATE_COLS), F32),
                   jax.ShapeDtypeStruct((t, SSM_W), F32)),
        in_specs=[pl.BlockSpec((tc, 128), lambda j, k: (k, j)),
                  strip_mat, strip_mat, strip_mat, strip_mat,
                  pl.BlockSpec((1, 128), lambda j, k: (0, j)),
                  pl.BlockSpec((16, SUBLANES, STRIP), lambda j, k: (0, 0, j))],
        out_specs=(states, states, pl.BlockSpec((tc, 128), lambda j, k: (k, j))),
        scratch_shapes=[pltpu.VMEM((1, STRIP), F32), pltpu.VMEM((1, STRIP), F32)],
        compiler_params=_params(("parallel", "arbitrary")),
    )(proj, bbr, bbi, ctr, cti, d_skip, tab)


def _shift_down(v, prev, n):
    row = lax.broadcasted_iota(jnp.int32, v.shape, 0)
    out = pltpu.roll(v, n, 0)
    for r in range(n):
        src = prev[SUBLANES - n + r:SUBLANES - n + r + 1, :]
        out = jnp.where(row == r, jnp.broadcast_to(src, v.shape), out)
    return out


def _shift_up(v, nxt, n):
    rows = v.shape[0]
    row = lax.broadcasted_iota(jnp.int32, v.shape, 0)
    out = pltpu.roll(v, rows - n, 0)
    for r in range(n):
        src = nxt[r:r + 1, :]
        out = jnp.where(row == rows - n + r, jnp.broadcast_to(src, v.shape), out)
    return out


def _conv3(q, q_prev, w):
    return w[2:3, :] * q + w[1:2, :] * _shift_down(q, q_prev, 1) + w[0:1, :] * _shift_down(q, q_prev, 2)


def _mixer_forward(x, proj, ya0, glu_w, glu_b, wso_st, conv_w8, wco_st, w_o, tm=256):
    t = x.shape[0]
    hb = tm // SUBLANES

    def body(x_r, ya0_r, h_r, cg_r, bg_r, ga_r, gb_r, hp_r, cgp_r,
             glu_w_r, glu_b_r, wso_r, cw_r, wco_r, wo_r, xh_r, rstd_r, ya_r, yb_r):
        i = pl.program_id(0)
        g, _ = _gelu_parts(ya0_r[...])
        ya1 = g * _sigmoid(_dot(g, glu_w_r[...]) + glu_b_r[...])
        q = cg_r[...] * h_r[...]
        q_prev = jnp.where(i > 0, cgp_r[...] * hp_r[...], 0.0)
        yb0 = bg_r[...] * _conv3(q, q_prev, cw_r[...])
        for j in range(N_SHARDS):
            ya_r[:, 256 * j:256 * (j + 1)] = _dot(ya1, wso_r[j])
            yb_r[:, 256 * j:256 * (j + 1)] = _dot(yb0, wco_r[j])
        merged = _sigmoid(ga_r[...]) * ya_r[...] + _sigmoid(gb_r[...]) * yb_r[...]
        r1 = ALPHA * x_r[...] + _dot(merged, wo_r[...])
        mu = jnp.mean(r1, axis=-1, keepdims=True)
        cen = r1 - mu
        rstd = lax.rsqrt(jnp.mean(cen * cen, axis=-1, keepdims=True) + LN_EPS)
        xh_r[...] = cen * rstd
        rstd_r[...] = rstd

    def col(w, c):
        return pl.BlockSpec((tm, w), lambda i: (i, c))

    def prev(c):
        return pl.BlockSpec((SUBLANES, SSM_W), lambda i: (jnp.maximum(i * hb - 1, 0), c))

    return pl.pallas_call(
        body, name="mixer_forward", grid=(t // tm,),
        out_shape=(jax.ShapeDtypeStruct((t, D_MODEL), F32), jax.ShapeDtypeStruct((t, 1), F32),
                   jax.ShapeDtypeStruct((t, D_MODEL), F32), jax.ShapeDtypeStruct((t, D_MODEL), F32)),
        in_specs=[col(D_MODEL, 0), col(SSM_W, 0), col(SSM_W, 1), col(SSM_W, 2), col(SSM_W, 3),
                  col(D_MODEL, 2), col(D_MODEL, 3), prev(1), prev(2),
                  _const((SSM_W, SSM_W)), _const((1, SSM_W)), _const((N_SHARDS, SSM_W, 256)),
                  _const((SUBLANES, SSM_W)), _const((N_SHARDS, SSM_W, 256)), _const((D_MODEL, D_MODEL))],
        out_specs=(col(D_MODEL, 0), pl.BlockSpec((tm, 1), lambda i: (i, 0)), col(D_MODEL, 0), col(D_MODEL, 0)),
        compiler_params=_params(("parallel",)),
    )(x, ya0, proj, proj, proj, proj, proj, proj, proj, glu_w, glu_b, wso_st, conv_w8, wco_st, w_o)


def _layer_norm_bwd(dxhat, xhat, rstd):
    m1 = jnp.mean(dxhat, axis=-1, keepdims=True)
    m2 = jnp.mean(dxhat * xhat, axis=-1, keepdims=True)
    return rstd * (dxhat - m1 - xhat * m2)


def _ffn_step(xhat1, rstd1, target, ln1_g, ln1_b, ln2_g, ln2_b, wg_st, wu_st, wd_st, tm=256):
    t = xhat1.shape[0]

    def body(xh_r, rstd_r, tgt_r, g1_r, b1_r, g2_r, b2_r, wg_r, wu_r, wd_r,
             loss_r, dr1_r, x1b_r, dr2b_r, hid_r, dhg_r, dhu_r, dg2_r, db2_r, dg1_r, db1_r,
             hg_s, hu_s):
        @pl.when(pl.program_id(0) == 0)
        def _():
            for r in (loss_r, dg2_r, db2_r, dg1_r, db1_r):
                r[...] = jnp.zeros_like(r)

        xhat1_v = xh_r[...]
        x1 = xhat1_v * g1_r[...] + b1_r[...]
        x1b = x1.astype(BF16)
        x1b_r[...] = x1b
        ffn = jnp.zeros((tm, D_MODEL), F32)
        for j in range(N_SHARDS):
            hg = jnp.dot(x1b, wg_r[j], preferred_element_type=F32)
            hu = jnp.dot(x1b, wu_r[j], preferred_element_type=F32)
            hg_s[j] = hg
            hu_s[j] = hu
            hid = (hg * _sigmoid(hg) * hu).astype(BF16)
            hid_r[j] = hid
            ffn = ffn + jnp.dot(hid, wd_r[j], preferred_element_type=F32)
        r2 = ALPHA * x1 + ffn
        mu = jnp.mean(r2, axis=-1, keepdims=True)
        cen = r2 - mu
        rstd2 = lax.rsqrt(jnp.mean(cen * cen, axis=-1, keepdims=True) + LN_EPS)
        xhat2 = cen * rstd2
        diff = (xhat2 * g2_r[...] + b2_r[...]) - tgt_r[...]
        loss_r[...] += 0.5 * jnp.sum(jnp.mean(diff * diff, axis=-1, keepdims=True), axis=0, keepdims=True)
        dy = diff * (1.0 / D_MODEL)
        dg2_r[...] += jnp.sum(dy * xhat2, axis=0, keepdims=True)
        db2_r[...] += jnp.sum(dy, axis=0, keepdims=True)
        dr2 = _layer_norm_bwd(dy * g2_r[...], xhat2, rstd2)
        dr2b = dr2.astype(BF16)
        dr2b_r[...] = dr2b
        dx1 = ALPHA * dr2
        for j in range(N_SHARDS):
            dhid = lax.dot_general(dr2b, wd_r[j], (((1,), (1,)), ((), ())), preferred_element_type=F32)
            hg = hg_s[j]
            hu = hu_s[j]
            sg = _sigmoid(hg)
            dhu = (dhid * (hg * sg)).astype(BF16)
            dhg = (dhid * hu * (sg * (1.0 + hg * (1.0 - sg)))).astype(BF16)
            dhg_r[j] = dhg
            dhu_r[j] = dhu
            dx1 = dx1 + lax.dot_general(dhg, wg_r[j], (((1,), (1,)), ((), ())), preferred_element_type=F32)
            dx1 = dx1 + lax.dot_general(dhu, wu_r[j], (((1,), (1,)), ((), ())), preferred_element_type=F32)
        dg1_r[...] += jnp.sum(dx1 * xhat1_v, axis=0, keepdims=True)
        db1_r[...] += jnp.sum(dx1, axis=0, keepdims=True)
        dr1_r[...] = _layer_norm_bwd(dx1 * g1_r[...], xhat1_v, rstd_r[...])

    tile = pl.BlockSpec((tm, D_MODEL), lambda i: (i, 0))
    hidden = pl.BlockSpec((N_SHARDS, tm, FFN_SHARD), lambda i: (0, i, 0))
    vec = _const((1, D_MODEL))
    hid_shape = jax.ShapeDtypeStruct((N_SHARDS, t, FFN_SHARD), BF16)
    vec_shape = jax.ShapeDtypeStruct((1, D_MODEL), F32)
    return pl.pallas_call(
        body, name="ffn_step", grid=(t // tm,),
        out_shape=(jax.ShapeDtypeStruct((1, 1), F32), jax.ShapeDtypeStruct((t, D_MODEL), F32),
                   jax.ShapeDtypeStruct((t, D_MODEL), BF16), jax.ShapeDtypeStruct((t, D_MODEL), BF16),
                   hid_shape, hid_shape, hid_shape, vec_shape, vec_shape, vec_shape, vec_shape),
        in_specs=[tile, pl.BlockSpec((tm, 1), lambda i: (i, 0)), tile, vec, vec, vec, vec,
                  _const((N_SHARDS, D_MODEL, FFN_SHARD)), _const((N_SHARDS, D_MODEL, FFN_SHARD)),
                  _const((N_SHARDS, FFN_SHARD, D_MODEL))],
        out_specs=(_const((1, 1)), tile, tile, tile, hidden, hidden, hidden, vec, vec, vec, vec),
        scratch_shapes=[pltpu.VMEM((N_SHARDS, tm, FFN_SHARD), F32), pltpu.VMEM((N_SHARDS, tm, FFN_SHARD), F32)],
        compiler_params=_params(("arbitrary",)),
    )(xhat1, rstd1, target, ln1_g, ln1_b, ln2_g, ln2_b, wg_st, wu_st, wd_st)


def _ffn_weight_grads(x1b, dr2b, hid, dhg, dhu, tk=512):
    t = x1b.shape[0]

    def body(x_r, dr_r, hid_r, dhg_r, dhu_r, gwg_r, gwu_r, gwd_r):
        @pl.when(pl.program_id(1) == 0)
        def _():
            for r in (gwg_r, gwu_r, gwd_r):
                r[...] = jnp.zeros_like(r)

        gwg_r[0] += _tdot(x_r[...], dhg_r[0])
        gwu_r[0] += _tdot(x_r[...], dhu_r[0])
        gwd_r[0] += _tdot(hid_r[0], dr_r[...])

    tile = pl.BlockSpec((tk, D_MODEL), lambda j, k: (k, 0))
    hidden = pl.BlockSpec((1, tk, FFN_SHARD), lambda j, k: (j, k, 0))
    col = pl.BlockSpec((1, D_MODEL, FFN_SHARD), lambda j, k: (j, 0, 0))
    row = pl.BlockSpec((1, FFN_SHARD, D_MODEL), lambda j, k: (j, 0, 0))
    return pl.pallas_call(
        body, name="ffn_weight_grads", grid=(N_SHARDS, t // tk),
        out_shape=(jax.ShapeDtypeStruct((N_SHARDS, D_MODEL, FFN_SHARD), F32),) * 2
        + (jax.ShapeDtypeStruct((N_SHARDS, FFN_SHARD, D_MODEL), F32),),
        in_specs=[tile, tile, hidden, hidden, hidden],
        out_specs=(col, col, row),
        compiler_params=_params(("parallel", "arbitrary")),
    )(x1b, dr2b, hid, dhg, dhu)


def _mixer_backward(dr1, proj, ya0, ya, yb, glu_w, glu_b, wso_st, conv_w8, wco_st, w_o, tm=256):
    t = dr1.shape[0]
    hb = tm // SUBLANES
    last_block = t // SUBLANES - 1

    def body(dr1_r, dr1n_r, ya0_r, ya_r, yb_r, h_r, cg_r, bg_r, ga_r, gb_r, hp_r, cgp_r, bgn_r, gbn_r,
             glu_w_r, glu_b_r, wso_r, cw_r, wco_r, wo_r,
             dya0_r, dproj_r, dbias_r, gwo_r, gwso_r, gwco_r, gglu_w_r, gglu_b_r, gconv_r):
        i = pl.program_id(0)

        @pl.when(i == 0)
        def _():
            for r in (dbias_r, gwo_r, gwso_r, gwco_r, gglu_w_r, gglu_b_r, gconv_r):
                r[...] = jnp.zeros_like(r)

        dr1_v = dr1_r[...]
        dmerged = _dot_t(dr1_v, wo_r[...])
        sa = _sigmoid(ga_r[...])
        sb = _sigmoid(gb_r[...])
        ya_v = ya_r[...]
        yb_v = yb_r[...]
        gwo_r[...] += _tdot(sa * ya_v + sb * yb_v, dr1_v)
        dya = dmerged * sa
        dyb = dmerged * sb
        dga = dmerged * ya_v * (sa * (1.0 - sa))
        dgb = dmerged * yb_v * (sb * (1.0 - sb))

        g, gelu_grad = _gelu_parts(ya0_r[...])
        s1 = _sigmoid(_dot(g, glu_w_r[...]) + glu_b_r[...])
        ya1 = g * s1
        dya1 = jnp.zeros((tm, SSM_W), F32)
        for j in range(N_SHARDS):
            dya_j = dya[:, 256 * j:256 * (j + 1)]
            gwso_r[j] += _tdot(ya1, dya_j)
            dya1 = dya1 + _dot_t(dya_j, wso_r[j])
        dz1 = dya1 * g * (s1 * (1.0 - s1))
        gglu_b_r[...] += jnp.sum(dz1, axis=0, keepdims=True)
        gglu_w_r[...] += _tdot(g, dz1)
        dya0_r[...] = (dya1 * s1 + _dot_t(dz1, glu_w_r[...])) * gelu_grad

        cw = cw_r[...]
        h = h_r[...]
        cg = cg_r[...]
        bg = bg_r[...]
        q = cg * h
        q_prev = jnp.where(i > 0, cgp_r[...] * hp_r[...], 0.0)
        q1 = _shift_down(q, q_prev, 1)
        q2 = _shift_down(q, q_prev, 2)
        z = cw[2:3, :] * q + cw[1:2, :] * q1 + cw[0:1, :] * q2
        yb0 = bg * z
        dyb0 = jnp.zeros((tm, SSM_W), F32)
        for j in range(N_SHARDS):
            dyb_j = dyb[:, 256 * j:256 * (j + 1)]
            gwco_r[j] += _tdot(yb0, dyb_j)
            dyb0 = dyb0 + _dot_t(dyb_j, wco_r[j])
        dbg = dyb0 * z
        dz = dyb0 * bg
        dyb_n = _dot_t(dr1n_r[...], wo_r[...]) * _sigmoid(gbn_r[...])
        dyb0_n = jnp.zeros((SUBLANES, SSM_W), F32)
        for j in range(N_SHARDS):
            dyb0_n = dyb0_n + _dot_t(dyb_n[:, 256 * j:256 * (j + 1)], wco_r[j])
        dz_next = jnp.where(i < pl.num_programs(0) - 1, dyb0_n * bgn_r[...], 0.0)
        dq = cw[2:3, :] * dz + cw[1:2, :] * _shift_up(dz, dz_next, 1) + cw[0:1, :] * _shift_up(dz, dz_next, 2)
        gconv_r[0:1, :] += jnp.sum(dz * q2, axis=0, keepdims=True)
        gconv_r[1:2, :] += jnp.sum(dz * q1, axis=0, keepdims=True)
        gconv_r[2:3, :] += jnp.sum(dz * q, axis=0, keepdims=True)
        dh = dq * cg
        dcg = dq * h

        dproj_r[:, 0:512] = jnp.zeros((tm, SSM_W), BF16)
        pieces = ((512, dh), (1024, dcg), (1536, dbg), (2048, dga), (3072, dgb))
        for off, val in pieces:
            w = val.shape[1]
            dproj_r[:, off:off + w] = val.astype(BF16)
            dbias_r[:, off:off + w] += jnp.sum(val, axis=0, keepdims=True)

    def col(w, c):
        return pl.BlockSpec((tm, w), lambda i: (i, c))

    def prev(c):
        return pl.BlockSpec((SUBLANES, SSM_W), lambda i: (jnp.maximum(i * hb - 1, 0), c))

    def nxt(w, c):
        return pl.BlockSpec((SUBLANES, w), lambda i: (jnp.minimum((i + 1) * hb, last_block), c))

    sh = jax.ShapeDtypeStruct
    return pl.pallas_call(
        body, name="mixer_backward", grid=(t // tm,),
        out_shape=(sh((t, SSM_W), F32), sh((t, IN_COLS), BF16), sh((1, IN_COLS), F32),
                   sh((D_MODEL, D_MODEL), F32), sh((N_SHARDS, SSM_W, 256), F32), sh((N_SHARDS, SSM_W, 256), F32),
                   sh((SSM_W, SSM_W), F32), sh((1, SSM_W), F32), sh((SUBLANES, SSM_W), F32)),
        in_specs=[col(D_MODEL, 0), nxt(D_MODEL, 0), col(SSM_W, 0), col(D_MODEL, 0), col(D_MODEL, 0),
                  col(SSM_W, 1), col(SSM_W, 2), col(SSM_W, 3), col(D_MODEL, 2), col(D_MODEL, 3),
                  prev(1), prev(2), nxt(SSM_W, 3), nxt(D_MODEL, 3),
                  _const((SSM_W, SSM_W)), _const((1, SSM_W)), _const((N_SHARDS, SSM_W, 256)),
                  _const((SUBLANES, SSM_W)), _const((N_SHARDS, SSM_W, 256)), _const((D_MODEL, D_MODEL))],
        out_specs=(col(SSM_W, 0), col(IN_COLS, 0), _const((1, IN_COLS)),
                   _const((D_MODEL, D_MODEL)), _const((N_SHARDS, SSM_W, 256)), _const((N_SHARDS, SSM_W, 256)),
                   _const((SSM_W, SSM_W)), _const((1, SSM_W)), _const((SUBLANES, SSM_W))),
        compiler_params=_params(("arbitrary",)),
    )(dr1, dr1, ya0, ya, yb, proj, proj, proj, proj, proj, proj, proj, proj, proj,
      glu_w, glu_b, wso_st, conv_w8, wco_st, w_o)


def _ssm_backward(dya0, proj, xsr, xsi, bbr, bbi, ctr, cti, d_skip, tab, dproj, tc=512):
    t = proj.shape[0]
    nb = tc // SUBLANES
    nk = t // tc

    def body(dy_r, u_r, xsr_r, xsi_r, bbr_r, bbi_r, ctr_r, cti_r, d_r, tab_r, dproj_any,
             du_r, dus_r, gbbr_r, gbbi_r, gctr_r, gcti_r, glbr_r, glbi_r, gd_r,
             gr_s, gi_s, car_r, car_i):
        del dproj_any

        @pl.when(pl.program_id(1) == 0)
        def _():
            for r in (car_r, car_i, dus_r, gbbr_r, gbbi_r, gctr_r, gcti_r, glbr_r, glbi_r, gd_r):
                r[...] = jnp.zeros_like(r)

        dy = dy_r[...]
        u = u_r[...]
        gr_s[...] = _dot(dy, ctr_r[...])
        gi_s[...] = -_dot(dy, cti_r[...])
        last_row = lax.broadcasted_iota(jnp.int32, (SUBLANES, STRIP), 0) == SUBLANES - 1

        def block(n, carry):
            cr, ci, ar, ai = carry
            rows = pl.ds(pl.multiple_of((nb - 1 - n) * SUBLANES, SUBLANES), SUBLANES)
            gr = gr_s[rows, :]
            gi = gi_s[rows, :]
            for lvl, s in ((0, 1), (1, 2), (2, 4)):
                gr, gi = _cmul_add(gr, gi, tab_r[8 + lvl], tab_r[12 + lvl],
                                   pltpu.roll(gr, SUBLANES - s, 0), pltpu.roll(gi, SUBLANES - s, 0))
            crb = jnp.broadcast_to(cr, gr.shape)
            cib = jnp.broadcast_to(ci, gi.shape)
            gr, gi = _cmul_add(gr, gi, tab_r[11], tab_r[15], crb, cib)
            gr_s[rows, :] = gr
            gi_s[rows, :] = gi
            gnr = jnp.where(last_row, crb, pltpu.roll(gr, SUBLANES - 1, 0))
            gni = jnp.where(last_row, cib, pltpu.roll(gi, SUBLANES - 1, 0))
            xr = xsr_r[rows, :]
            xi = xsi_r[rows, :]
            ar = ar + (xr * gnr + xi * gni)
            ai = ai + (xr * gni - xi * gnr)
            return gr[0:1, :], gi[0:1, :], ar, ai

        zero = jnp.zeros((SUBLANES, STRIP), F32)
        cr, ci, ar, ai = lax.fori_loop(0, nb, block, (car_r[...], car_i[...], zero, zero))
        car_r[...] = cr
        car_i[...] = ci
        glbr_r[...] += ar
        glbi_r[...] += ai
        gr = gr_s[...]
        gi = gi_s[...]
        du = _dot_t(gr, bbr_r[...]) + _dot_t(gi, bbi_r[...]) + d_r[...] * dy
        du_r[...] = du.astype(BF16)
        dus_r[...] += jnp.sum(du, axis=0, keepdims=True)
        gd_r[...] += jnp.sum(dy * u, axis=0, keepdims=True)
        gbbr_r[...] += _tdot(u, gr)
        gbbi_r[...] += _tdot(u, gi)
        gctr_r[...] += _tdot(dy, xsr_r[...])
        gcti_r[...] -= _tdot(dy, xsi_r[...])

    def rev(w):
        return pl.BlockSpec((tc, w), lambda j, k: (nk - 1 - k, j))

    strip_mat = pl.BlockSpec((128, STRIP), lambda j, k: (j, 0))
    vec = pl.BlockSpec((1, 128), lambda j, k: (0, j))
    lbacc = pl.BlockSpec((SUBLANES, STRIP), lambda j, k: (0, j))
    sh = jax.ShapeDtypeStruct
    return pl.pallas_call(
        body, name="ssm_backward", grid=(N_STRIPS, nk),
        out_shape=(sh((t, IN_COLS), BF16), sh((1, SSM_W), F32),
                   sh((SSM_W, STRIP), F32), sh((SSM_W, STRIP), F32), sh((SSM_W, STRIP), F32), sh((SSM_W, STRIP), F32),
                   sh((SUBLANES, STATE_COLS), F32), sh((SUBLANES, STATE_COLS), F32), sh((1, SSM_W), F32)),
        in_specs=[rev(128), rev(128), rev(STRIP), rev(STRIP),
                  strip_mat, strip_mat, strip_mat, strip_mat, vec,
                  pl.BlockSpec((16, SUBLANES, STRIP), lambda j, k: (0, 0, j)), ANY],
        out_specs=(rev(128), vec, strip_mat, strip_mat, strip_mat, strip_mat, lbacc, lbacc, vec),
        scratch_shapes=[pltpu.VMEM((tc, STRIP), F32), pltpu.VMEM((tc, STRIP), F32),
                        pltpu.VMEM((1, STRIP), F32), pltpu.VMEM((1, STRIP), F32)],
        input_output_aliases={10: 0},
        compiler_params=_params(("parallel", "arbitrary")),
    )(dya0, proj, xsr, xsi, bbr, bbi, ctr, cti, d_skip, tab, dproj)


def _input_grad(dr1, dproj, w_in_st, tm=256):
    t = dr1.shape[0]

    def body(dr1_r, dp_r, w_r, dx_r):
        acc = ALPHA * dr1_r[...]
        for j in range(N_SHARDS):
            acc = acc + lax.dot_general(dp_r[:, D_MODEL * j:D_MODEL * (j + 1)], w_r[j],
                                        (((1,), (1,)), ((), ())), preferred_element_type=F32)
        dx_r[...] = acc

    return pl.pallas_call(
        body, name="input_grad", grid=(t // tm,),
        out_shape=jax.ShapeDtypeStruct((t, D_MODEL), F32),
        in_specs=[pl.BlockSpec((tm, D_MODEL), lambda i: (i, 0)), pl.BlockSpec((tm, IN_COLS), lambda i: (i, 0)),
                  _const((N_SHARDS, D_MODEL, D_MODEL))],
        out_specs=pl.BlockSpec((tm, D_MODEL), lambda i: (i, 0)),
        compiler_params=_params(("parallel",)),
    )(dr1, dproj, w_in_st)


def _in_weight_grad(x, dproj, tk=512):
    t = x.shape[0]

    def body(x_r, dp_r, gw_r):
        @pl.when(pl.program_id(1) == 0)
        def _():
            gw_r[...] = jnp.zeros_like(gw_r)

        gw_r[0] += _tdot(x_r[...], dp_r[...])

    return pl.pallas_call(
        body, name="in_weight_grad", grid=(N_SHARDS, t // tk),
        out_shape=jax.ShapeDtypeStruct((N_SHARDS, D_MODEL, D_MODEL), F32),
        in_specs=[pl.BlockSpec((tk, D_MODEL), lambda j, k: (k, 0)), pl.BlockSpec((tk, D_MODEL), lambda j, k: (k, j))],
        out_specs=pl.BlockSpec((1, D_MODEL, D_MODEL), lambda j, k: (j, 0, 0)),
        compiler_params=_params(("parallel", "arbitrary")),
    )(x, dproj)


def _local_step(x, target, small, wts):
    lr, li = small["ssm_lambda_re"][0], small["ssm_lambda_im"][0]
    ldt = small["ssm_log_dt"][0][:, None]
    rep16 = lambda a: jnp.broadcast_to(a[:, None, :], (N_GROUPS, GROUP_C, a.shape[-1])).reshape(SSM_W, a.shape[-1])
    lr16, li16 = rep16(lr), rep16(li)
    ldt16 = rep16(jnp.broadcast_to(ldt, (N_GROUPS, N_STATE)))
    brt = small["ssm_b_re"][0].transpose(0, 2, 1).reshape(SSM_W, N_STATE)
    bit = small["ssm_b_im"][0].transpose(0, 2, 1).reshape(SSM_W, N_STATE)
    cre = small["ssm_c_re"][0].reshape(SSM_W, N_STATE)
    cim = small["ssm_c_im"][0].reshape(SSM_W, N_STATE)
    disc = (lr, li, ldt, lr16, li16, ldt16, brt, bit)

    pwr, pwi, bbr, bbi, ctr, cti = _ssm_prepare(*disc, cre, cim)
    tab = _scan_tables(pwr, pwi)
    conv_w8 = jnp.pad(wts["conv_w"][:, :3, :].transpose(1, 0, 2).reshape(3, SSM_W), ((0, SUBLANES - 3), (0, 0)))
    w_o = wts["w_o"].reshape(D_MODEL, D_MODEL)
    glu_w = wts["glu_w"].reshape(SSM_W, SSM_W)

    proj = _in_proj(x, wts["w_in"], small["b_in"])
    xsr, xsi, ya0 = _ssm_forward(proj, bbr, bbi, ctr, cti, small["ssm_d"], tab)
    xhat1, rstd1, ya, yb = _mixer_forward(x, proj, ya0, glu_w, small["glu_b"], wts["w_ssm_out"], conv_w8,
                                          wts["w_conv_out"], w_o)
    (loss, dr1, x1b, dr2b, hid, dhg, dhu, g_ln2_g, g_ln2_b, g_ln1_g, g_ln1_b) = _ffn_step(
        xhat1, rstd1, target, small["ln1_g"], small["ln1_b"], small["ln2_g"], small["ln2_b"],
        wts["w_gate"], wts["w_up"], wts["w_down"])
    g_wg, g_wu, g_wd = _ffn_weight_grads(x1b, dr2b, hid, dhg, dhu)
    (dya0, dproj, dbias, g_wo, g_wso, g_wco, g_glu_w, g_glu_b, g_conv8) = _mixer_backward(
        dr1, proj, ya0, ya, yb, glu_w, small["glu_b"], wts["w_ssm_out"], conv_w8, wts["w_conv_out"], w_o)
    (dproj, dus, gbbr, gbbi, gctr, gcti, glbr, glbi, g_d) = _ssm_backward(
        dya0, proj, xsr, xsi, bbr, bbi, ctr, cti, small["ssm_d"], tab, dproj)
    g_lr, g_li, g_ldt, g_brt, g_bit, g_cre, g_cim = _ssm_param_grads(
        *disc, glbr.reshape(SUBLANES, N_GROUPS, N_STATE), glbi.reshape(SUBLANES, N_GROUPS, N_STATE),
        gbbr, gbbi, gctr, gcti)
    dx = _input_grad(dr1, dproj, wts["w_in"])
    g_w_in = _in_weight_grad(x, dproj)

    big = {
        "w_in": g_w_in,
        "glu_w": g_glu_w.reshape(N_SHARDS, 128, SSM_W),
        "w_ssm_out": g_wso,
        "conv_w": jnp.pad(g_conv8[:3].reshape(3, N_SHARDS, 128).transpose(1, 0, 2),
                          ((0, 0), (0, SUBLANES - 3), (0, 0))),
        "w_conv_out": g_wco,
        "w_o": g_wo.reshape(N_SHARDS, 256, D_MODEL),
        "w_gate": g_wg, "w_up": g_wu, "w_down": g_wd,
    }
    unT = lambda a: a.reshape(N_GROUPS, GROUP_C, N_STATE).transpose(0, 2, 1)[None]
    little = {
        "b_in": jnp.concatenate([dus, dbias[:, SSM_W:]], axis=1),
        "ssm_lambda_re": g_lr[None], "ssm_lambda_im": g_li[None], "ssm_log_dt": g_ldt.reshape(1, N_GROUPS),
        "ssm_b_re": unT(g_brt), "ssm_b_im": unT(g_bit),
        "ssm_c_re": g_cre.reshape(1, N_GROUPS, GROUP_C, N_STATE), "ssm_c_im": g_cim.reshape(1, N_GROUPS, GROUP_C, N_STATE),
        "ssm_d": g_d, "glu_b": g_glu_b,
        "ln1_g": g_ln1_g, "ln1_b": g_ln1_b, "ln2_g": g_ln2_g, "ln2_b": g_ln2_b,
    }
    return loss, dx, big, little


BIG = ("w_in", "glu_w", "w_ssm_out", "w_conv_out", "w_o", "w_gate", "w_up", "w_down")
SMALL = ("b_in", "ssm_lambda_re", "ssm_lambda_im", "ssm_log_dt", "ssm_b_re", "ssm_b_im", "ssm_c_re", "ssm_c_im",
         "ssm_d", "glu_b", "ln1_g", "ln1_b", "ln2_g", "ln2_b")
WEIGHTS = ("w_in", "b_in", "ssm_lambda_re", "ssm_lambda_im", "ssm_log_dt", "ssm_b_re", "ssm_b_im", "ssm_c_re",
           "ssm_c_im", "ssm_d", "glu_w", "glu_b", "w_ssm_out", "conv_w", "w_conv_out", "w_o", "ln1_g", "ln1_b",
           "w_gate", "w_up", "w_down", "ln2_g", "ln2_b")


def _place():
    x, y, c = lax.axis_index("x"), lax.axis_index("y"), lax.axis_index("c")
    chips = [(1 - x, y), (x, 1 - y), (1 - x, 1 - y)]
    return x, y, c, chips


def _shard_of(chip):
    return 2 * chip[0] + chip[1]


def _remote(src, dst, send_sem, recv_sem, to):
    return pltpu.make_async_remote_copy(src_ref=src, dst_ref=dst, send_sem=send_sem, recv_sem=recv_sem,
                                        device_id=to, device_id_type=MESH)


def _gather_weights(shards):
    n = len(shards)
    n_big = n - 1

    def body(*refs):
        src, dst = refs[:n], refs[n:2 * n]
        send_sem, recv_sem, fsend_sem, frecv_sem = refs[2 * n:]
        x, y, c, chips = _place()
        me = _shard_of((x, y))
        sibling = (x, y, 1 - c)

        def half(a, which):
            r2 = shards[a].shape[0] // 2
            return pl.ds(pl.multiple_of(which * r2, 16), r2)

        sends = []
        for a in range(n):
            for k, chip in enumerate(chips):
                if a < n_big:
                    cp = _remote(src[a].at[half(a, c)], dst[a].at[me, half(a, c)],
                                 send_sem.at[a, k], recv_sem.at[a, k], (*chip, c))
                else:
                    cp = _remote(src[a], dst[a].at[me], send_sem.at[a, k], recv_sem.at[a, k], (*chip, c))
                cp.start()
                sends.append(cp)
        for a in range(n):
            for k, chip in enumerate(chips):
                got = _shard_of(chip)
                if a < n_big:
                    rows = dst[a].at[got, half(a, c)]
                    _remote(rows, rows, send_sem.at[a, k], recv_sem.at[a, k], sibling).wait_recv()
                    cp = _remote(rows, rows, fsend_sem.at[a, k], frecv_sem.at[a, k], sibling)
                    cp.start()
                    sends.append(cp)
                else:
                    _remote(dst[a].at[got], dst[a].at[got], send_sem.at[a, k], recv_sem.at[a, k], sibling).wait_recv()
        for a in range(n_big):
            for k, chip in enumerate(chips):
                rows = dst[a].at[_shard_of(chip), half(a, 1 - c)]
                _remote(rows, rows, fsend_sem.at[a, k], frecv_sem.at[a, k], sibling).wait_recv()
        for cp in sends:
            cp.wait_send()

    stacks = pl.pallas_call(
        body, name="gather_weights",
        out_shape=tuple(jax.ShapeDtypeStruct((N_SHARDS,) + s.shape, s.dtype) for s in shards),
        in_specs=[ANY] * n, out_specs=(ANY,) * n,
        scratch_shapes=[pltpu.SemaphoreType.DMA((n, 3))] * 4,
    )(*shards)
    me = _shard_of((lax.axis_index("x"), lax.axis_index("y")))
    return [lax.dynamic_update_slice(st, sh[None], (me,) + (0,) * sh.ndim) for st, sh in zip(stacks, shards)]


def _swap_with_sibling(big, small):
    nb, n = len(big), len(big) + len(small)
    arrays = list(big) + list(small)

    def body(*refs):
        src, dst = refs[:n], refs[n:2 * n]
        send_sem, recv_sem = refs[2 * n:]
        x, y, c, _ = _place()
        sibling = (x, y, 1 - c)
        copies = []
        for a in range(n):
            if a < nb:
                r2 = arrays[a].shape[1] // 2
                part = src[a].at[:, pl.ds(pl.multiple_of((1 - c) * r2, SUBLANES), r2), :]
            else:
                part = src[a]
            cp = _remote(part, dst[a], send_sem.at[a], recv_sem.at[a], sibling)
            cp.start()
            copies.append(cp)
        for cp in copies:
            cp.wait()

    out_shape = tuple(jax.ShapeDtypeStruct((N_SHARDS, g.shape[1] // 2, g.shape[2]), g.dtype) for g in big)
    out_shape += tuple(jax.ShapeDtypeStruct(g.shape, g.dtype) for g in small)
    out = pl.pallas_call(
        body, name="swap_with_sibling", out_shape=out_shape, in_specs=[ANY] * n, out_specs=(ANY,) * n,
        scratch_shapes=[pltpu.SemaphoreType.DMA((n,)), pltpu.SemaphoreType.DMA((n,))],
    )(*arrays)
    return out[:nb], out[nb:]


def _scatter_to_chips(big, conv, small):
    slabbed = list(big) + [conv]
    ns, n = len(slabbed), len(slabbed) + len(small)
    arrays = slabbed + list(small)

    def body(*refs):
        src, dst = refs[:n], refs[n:2 * n]
        send_sem, recv_sem = refs[2 * n:]
        _, _, c, chips = _place()
        copies = []
        for a in range(n):
            for k, chip in enumerate(chips):
                part = src[a].at[_shard_of(chip)] if a < ns else src[a]
                cp = _remote(part, dst[a].at[k], send_sem.at[a, k], recv_sem.at[a, k], (*chip, c))
                cp.start()
                copies.append(cp)
        for cp in copies:
            cp.wait()

    out_shape = tuple(jax.ShapeDtypeStruct((3,) + g.shape[1:], g.dtype) for g in slabbed)
    out_shape += tuple(jax.ShapeDtypeStruct((3,) + g.shape, g.dtype) for g in small)
    out = pl.pallas_call(
        body, name="scatter_to_chips", out_shape=out_shape, in_specs=[ANY] * n, out_specs=(ANY,) * n,
        scratch_shapes=[pltpu.SemaphoreType.DMA((n, 3)), pltpu.SemaphoreType.DMA((n, 3))],
    )(*arrays)
    return out[:len(big)], out[len(big)], out[ns:]


def _send_to_sibling(arrays):
    n = len(arrays)

    def body(*refs):
        src, dst = refs[:n], refs[n:2 * n]
        send_sem, recv_sem = refs[2 * n:]
        x, y, c, _ = _place()
        copies = [_remote(src[a], dst[a], send_sem.at[a], recv_sem.at[a], (x, y, 1 - c)) for a in range(n)]
        for cp in copies:
            cp.start()
        for cp in copies:
            cp.wait()

    return pl.pallas_call(
        body, name="send_to_sibling",
        out_shape=tuple(jax.ShapeDtypeStruct(h.shape, h.dtype) for h in arrays),
        in_specs=[ANY] * n, out_specs=(ANY,) * n,
        scratch_shapes=[pltpu.SemaphoreType.DMA((n,))] * 2,
    )(*arrays)


def _row_chunk(rows):
    for cand in (256, 176, 128, 64):
        if rows % cand == 0:
            return cand
    return rows


def _add_own_half(stack, received, c):
    _, r2, cols = received.shape

    def body(c_ref, a_r, b_r, o_r):
        del c_ref
        o_r[...] = (a_r[...] + b_r[...]).astype(BF16)

    return pl.pallas_call(
        body, name="add_own_half",
        grid_spec=pltpu.PrefetchScalarGridSpec(
            num_scalar_prefetch=1, grid=(N_SHARDS,),
            in_specs=[pl.BlockSpec((1, r2, cols), lambda s, c_ref: (s, c_ref[0], 0)),
                      pl.BlockSpec((1, r2, cols), lambda s, c_ref: (s, 0, 0))],
            out_specs=pl.BlockSpec((1, r2, cols), lambda s, c_ref: (s, 0, 0))),
        out_shape=jax.ShapeDtypeStruct(received.shape, BF16),
        compiler_params=_params(("parallel",)),
    )(c, stack, received)


def _chip_order_sum(me, own, s0, s1, s2):
    terms = []
    for s in range(N_SHARDS):
        d = jnp.bitwise_xor(me, s)
        terms.append(jnp.where(d == 0, own, jnp.where(d == 2, s0, jnp.where(d == 1, s1, s2))))
    return ((terms[0] + terms[1]) + terms[2]) + terms[3]


def _sum_chips(own_stack, slots, me):
    _, rows, cols = slots.shape
    rc = _row_chunk(rows)

    def body(me_ref, own_r, s_r, o_r):
        o_r[...] = _chip_order_sum(me_ref[0], own_r[0].astype(F32), s_r[0].astype(F32), s_r[1].astype(F32),
                                   s_r[2].astype(F32))

    return pl.pallas_call(
        body, name="sum_chips",
        grid_spec=pltpu.PrefetchScalarGridSpec(
            num_scalar_prefetch=1, grid=(rows // rc,),
            in_specs=[pl.BlockSpec((1, rc, cols), lambda i, me_ref: (me_ref[0], i, 0)),
                      pl.BlockSpec((3, rc, cols), lambda i, me_ref: (0, i, 0))],
            out_specs=pl.BlockSpec((rc, cols), lambda i, me_ref: (i, 0))),
        out_shape=jax.ShapeDtypeStruct((rows, cols), F32),
        compiler_params=_params(("parallel",)),
    )(me, own_stack, slots)


def _small_pair_sums(mine, theirs):
    n = len(mine)

    def body(*refs):
        for a in range(n):
            refs[2 * n + a][...] = refs[a][...] + refs[n + a][...]

    vm = pl.BlockSpec(memory_space=pltpu.VMEM)
    return pl.pallas_call(
        body, name="small_pair_sums", out_shape=tuple(jax.ShapeDtypeStruct(g.shape, g.dtype) for g in mine),
        in_specs=[vm] * (2 * n), out_specs=(vm,) * n,
        compiler_params=pltpu.CompilerParams(vmem_limit_bytes=VMEM_LIMIT),
    )(*mine, *theirs)


def _adam_math(w, g, m, v):
    m = ADAM_B1 * m + (1.0 - ADAM_B1) * g
    v = ADAM_B2 * v + (1.0 - ADAM_B2) * (g * g)
    m_hat = m / (1.0 - ADAM_B1 ** ADAM_STEP)
    v_hat = v / (1.0 - ADAM_B2 ** ADAM_STEP)
    delta = -ADAM_LR * (m_hat / (jnp.sqrt(v_hat) + ADAM_EPS) + ADAM_WD * w)
    return delta, m, v


def _adam_small(me, owns, slots, ws, ms, vs):
    n = len(slots)

    def body(me_ref, *refs):
        for a in range(n):
            own_r, s_r, w_r, m_r, v_r = (refs[i * n + a] for i in range(5))
            g_r, d_r, nm_r, nv_r = (refs[(5 + i) * n + a] for i in range(4))
            g = _chip_order_sum(me_ref[0], own_r[...], s_r[0], s_r[1], s_r[2])
            g_r[...] = g
            d_r[...], nm_r[...], nv_r[...] = _adam_math(w_r[...], g, m_r[...], v_r[...])

    vm = pl.BlockSpec(memory_space=pltpu.VMEM)
    shapes = tuple(jax.ShapeDtypeStruct(w.shape, F32) for w in ws)
    out = pl.pallas_call(
        body, name="adam_small", out_shape=shapes * 4,
        in_specs=[pl.BlockSpec(memory_space=pltpu.SMEM)] + [vm] * (5 * n), out_specs=(vm,) * (4 * n),
        compiler_params=pltpu.CompilerParams(vmem_limit_bytes=VMEM_LIMIT),
    )(me, *owns, *slots, *ws, *ms, *vs)
    return out[:n], out[n:2 * n], out[2 * n:3 * n], out[3 * n:]


def _adam_big(w, mine, other, m, v, c):
    r2, cols = mine.shape
    rc = _row_chunk(r2)
    nch = r2 // rc

    def body(c_ref, w_r, mine_r, other_r, m_r, v_r, g_r, d_r, nm_r, nv_r):
        g = jnp.where(pl.program_id(0) == c_ref[0], mine_r[...], other_r[...])
        g_r[...] = g
        d_r[...], nm_r[...], nv_r[...] = _adam_math(w_r[...], g, m_r[...], v_r[...])

    full = pl.BlockSpec((rc, cols), lambda h, i, c_ref: (h * nch + i, 0))
    half = pl.BlockSpec((rc, cols), lambda h, i, c_ref: (i, 0))
    shape = jax.ShapeDtypeStruct((2 * r2, cols), F32)
    return pl.pallas_call(
        body, name="adam_big",
        grid_spec=pltpu.PrefetchScalarGridSpec(
            num_scalar_prefetch=1, grid=(2, nch),
            in_specs=[full, half, half, full, full], out_specs=(full,) * 4),
        out_shape=(shape,) * 4, compiler_params=_params(("parallel", "parallel")),
    )(c, w.reshape(2 * r2, cols), mine, other, m.reshape(2 * r2, cols), v.reshape(2 * r2, cols))


def kernel(x, w_in, b_in, ssm_lambda_re, ssm_lambda_im, ssm_log_dt, ssm_b_re, ssm_b_im, ssm_c_re, ssm_c_im, ssm_d, glu_w, glu_b, w_ssm_out, conv_w, w_conv_out, w_o, ln1_g, ln1_b, w_gate, w_up, w_down, ln2_g, ln2_b, loss_target, m_w_in, m_b_in, m_ssm_lambda_re, m_ssm_lambda_im, m_ssm_log_dt, m_ssm_b_re, m_ssm_b_im, m_ssm_c_re, m_ssm_c_im, m_ssm_d, m_glu_w, m_glu_b, m_w_ssm_out, m_conv_w, m_w_conv_out, m_w_o, m_ln1_g, m_ln1_b, m_w_gate, m_w_up, m_w_down, m_ln2_g, m_ln2_b, v_w_in, v_b_in, v_ssm_lambda_re, v_ssm_lambda_im, v_ssm_log_dt, v_ssm_b_re, v_ssm_b_im, v_ssm_c_re, v_ssm_c_im, v_ssm_d, v_glu_w, v_glu_b, v_w_ssm_out, v_conv_w, v_w_conv_out, v_w_o, v_ln1_g, v_ln1_b, v_w_gate, v_w_up, v_w_down, v_ln2_g, v_ln2_b):
    given = dict(locals())
    w = {n: given[n] for n in WEIGHTS}
    m = {n: given["m_" + n] for n in WEIGHTS}
    v = {n: given["v_" + n] for n in WEIGHTS}

    shards = [w[n][0].astype(BF16) for n in BIG]
    shards.append(jnp.pad(conv_w[0], ((0, SUBLANES - 3), (0, 0))))
    stacks = _gather_weights(shards)
    wts = dict(zip(BIG + ("conv_w",), stacks))

    small = {n: w[n] for n in SMALL}
    loss, dx, g_big, g_small = _local_step(x[0], loss_target[0], small, wts)

    c_arr = jnp.reshape(lax.axis_index("c"), (1,)).astype(jnp.int32)
    me = _shard_of((lax.axis_index("x"), lax.axis_index("y")))
    me_arr = jnp.reshape(me, (1,)).astype(jnp.int32)
    mine_small = [g_big["conv_w"]] + [g_small[n] for n in SMALL]
    got_big, got_small = _swap_with_sibling([g_big[n] for n in BIG], mine_small)
    chip_big = [_add_own_half(g_big[n], r, c_arr) for n, r in zip(BIG, got_big)]
    chip_small = _small_pair_sums(mine_small, got_small)
    slots_big, slots_conv, slots_small = _scatter_to_chips(chip_big, chip_small[0], chip_small[1:])
    halves = [_sum_chips(own, s, me_arr) for own, s in zip(chip_big, slots_big)]
    others = _send_to_sibling(halves)

    small_names = ("conv_w",) + SMALL
    conv_own = lax.dynamic_index_in_dim(chip_small[0], me, 0, keepdims=False)[:3][None]
    conv_slots = slots_conv[:, :3, :][:, None]
    gs, ds, nms, nvs = _adam_small(me_arr, [conv_own] + list(chip_small[1:]), [conv_slots] + list(slots_small),
                                   [w[n] for n in small_names], [m[n] for n in small_names],
                                   [v[n] for n in small_names])
    grad, delta, new_m, new_v = {}, {}, {}, {}
    for i, n in enumerate(small_names):
        grad[n], delta[n], new_m[n], new_v[n] = gs[i], ds[i], nms[i], nvs[i]
    for n, mine, other in zip(BIG, halves, others):
        shape = w[n].shape
        grad[n], delta[n], new_m[n], new_v[n] = (
            r.reshape(shape) for r in _adam_big(w[n], mine, other, m[n], v[n], c_arr))

    loss_total = lax.psum(loss[0, 0], ("x", "y", "c"))
    return (loss_total, dx[None], *[grad[n] for n in WEIGHTS], *[delta[n] for n in WEIGHTS],
            *[new_m[n] for n in WEIGHTS], *[new_v[n] for n in WEIGHTS])
```

```python
import functools
import math

import jax
import jax.numpy as jnp
from jax import lax
from jax.experimental import pallas as pl
from jax.experimental.pallas import tpu as pltpu

F32 = jnp.float32
BF16 = jnp.bfloat16

D_MODEL = 1024
IN_COLS = 4096
SSM_W = 512
N_GROUPS = 32
N_STATE = 64
GROUP_C = 16
STATE_COLS = N_GROUPS * N_STATE
STRIP = 512
N_STRIPS = STATE_COLS // STRIP
FFN_SHARD = 704
N_SHARDS = 4
ALPHA = 2.0 ** 0.25
LN_EPS = 1e-5
GELU_K = math.sqrt(2.0 / math.pi)
GELU_C = 0.044715

ADAM_LR = 0.001
ADAM_B1 = 0.9
ADAM_B2 = 0.999
ADAM_EPS = 1e-08
ADAM_WD = 0.01
ADAM_STEP = 10

V7X_VMEM_BYTES = 64 * 1024 * 1024
VMEM_LIMIT = V7X_VMEM_BYTES - 8 * 1024 * 1024
SUBLANES = 8

MESH = pl.DeviceIdType.MESH
ANY = pl.BlockSpec(memory_space=pl.ANY)


def _dot(a, b):
    return jnp.dot(a.astype(BF16), b.astype(BF16), preferred_element_type=F32)


def _dot_t(a, b):
    return lax.dot_general(a.astype(BF16), b.astype(BF16), (((1,), (1,)), ((), ())),
                           preferred_element_type=F32)


def _tdot(a, b):
    return lax.dot_general(a.astype(BF16), b.astype(BF16), (((0,), (0,)), ((), ())),
                           preferred_element_type=F32)


def _sigmoid(v):
    return 1.0 / (1.0 + jnp.exp(-v))


def _split3(v):
    hi = v.astype(BF16)
    r1 = v - hi.astype(F32)
    mid = r1.astype(BF16)
    lo = (r1 - mid.astype(F32)).astype(BF16)
    return hi, mid, lo


def _exact_dot(v, sel):
    hi, mid, lo = _split3(v)
    return (jnp.dot(hi, sel, preferred_element_type=F32)
            + jnp.dot(mid, sel, preferred_element_type=F32)
            + jnp.dot(lo, sel, preferred_element_type=F32))


def _const(shape):
    nd = len(shape)
    return pl.BlockSpec(shape, lambda *_: (0,) * nd)


def _params(sem, vmem=VMEM_LIMIT):
    return pltpu.CompilerParams(dimension_semantics=sem, vmem_limit_bytes=vmem)


def _gelu_parts(v):
    inner = GELU_K * (v + GELU_C * v * v * v)
    t = jnp.tanh(inner)
    g = 0.5 * v * (1.0 + t)
    dg = 0.5 * (1.0 + t) + 0.5 * v * (1.0 - t * t) * GELU_K * (1.0 + 3.0 * GELU_C * v * v)
    return g, dg


def _ssm_discretise(lr, li, ldt, lr16, li16, ldt16, brt, bit):
    def lam_bar(lr_, li_, ldt_):
        dt = jnp.exp(ldt_)
        mag = jnp.exp(lr_ * dt)
        return mag * jnp.cos(li_ * dt), mag * jnp.sin(li_ * dt)

    lb_re, lb_im = lam_bar(lr, li, ldt)
    l16_re, l16_im = lam_bar(lr16, li16, ldt16)
    den = lr16 * lr16 + li16 * li16
    num_re = l16_re - 1.0
    fr = (num_re * lr16 + l16_im * li16) / den
    fi = (l16_im * lr16 - num_re * li16) / den
    bb_re = fr * brt - fi * bit
    bb_im = fr * bit + fi * brt
    return lb_re, lb_im, bb_re, bb_im


def _strip_selectors():
    p = lax.broadcasted_iota(jnp.int32, (N_STATE, STRIP), 0)
    col = lax.broadcasted_iota(jnp.int32, (N_STATE, STRIP), 1)
    rep = ((col & (N_STATE - 1)) == p).astype(BF16)
    row = lax.broadcasted_iota(jnp.int32, (SSM_W, STRIP), 0)
    col2 = lax.broadcasted_iota(jnp.int32, (SSM_W, STRIP), 1)
    mask = (((row >> 4) & 7) == (col2 >> 6))
    return rep, mask


def _ssm_prepare(lr, li, ldt, lr16, li16, ldt16, brt, bit, cre, cim):
    def body(lr_r, li_r, ldt_r, lr16_r, li16_r, ldt16_r, brt_r, bit_r, cre_r, cim_r,
             pwr_r, pwi_r, bbr_r, bbi_r, ctr_r, cti_r):
        lb_re, lb_im, bb_re, bb_im = _ssm_discretise(
            lr_r[...], li_r[...], ldt_r[...], lr16_r[...], li16_r[...], ldt16_r[...], brt_r[...], bit_r[...])
        pr, pi_ = lb_re, lb_im
        pwr_r[0] = pr
        pwi_r[0] = pi_
        for k in range(1, SUBLANES):
            pr, pi_ = pr * lb_re - pi_ * lb_im, pr * lb_im + pi_ * lb_re
            pwr_r[k] = pr
            pwi_r[k] = pi_
        rep, mask = _strip_selectors()
        for src, dst in ((bb_re, bbr_r), (bb_im, bbi_r), (cre_r[...], ctr_r), (cim_r[...], cti_r)):
            wide = jnp.dot(src.astype(BF16), rep, preferred_element_type=F32)
            dst[...] = jnp.where(mask, wide, 0.0).astype(BF16)

    vm = pl.BlockSpec(memory_space=pltpu.VMEM)
    return pl.pallas_call(
        body, name="ssm_prepare",
        out_shape=(jax.ShapeDtypeStruct((SUBLANES, N_GROUPS, N_STATE), F32),) * 2
        + (jax.ShapeDtypeStruct((SSM_W, STRIP), BF16),) * 4,
        in_specs=[vm] * 10, out_specs=(vm,) * 6,
    )(lr, li, ldt, lr16, li16, ldt16, brt, bit, cre, cim)


def _scan_tables(pwr, pwi):
    pr = pwr.reshape(SUBLANES, STATE_COLS)
    pi_ = pwi.reshape(SUBLANES, STATE_COLS)
    row = jnp.arange(SUBLANES)[:, None]
    z = jnp.zeros((SUBLANES, STATE_COLS), F32)

    def lvl(p, k, keep):
        return jnp.where(keep, jnp.broadcast_to(p[k - 1][None], (SUBLANES, STATE_COLS)), z)

    fwd_re = [lvl(pr, 1, row >= 1), lvl(pr, 2, row >= 2), lvl(pr, 4, row >= 4), pr]
    fwd_im = [lvl(pi_, 1, row >= 1), lvl(pi_, 2, row >= 2), lvl(pi_, 4, row >= 4), pi_]
    rev_re = [lvl(pr, 1, row <= 6), lvl(pr, 2, row <= 5), lvl(pr, 4, row <= 3), pr[::-1]]
    rev_im = [-lvl(pi_, 1, row <= 6), -lvl(pi_, 2, row <= 5), -lvl(pi_, 4, row <= 3), -pi_[::-1]]
    return jnp.stack(fwd_re + fwd_im + rev_re + rev_im)


def _ssm_param_grads(lr, li, ldt, lr16, li16, ldt16, brt, bit, dlbr, dlbi, dbbr, dbbi, dctr, dcti):
    def body(lr_r, li_r, ldt_r, lr16_r, li16_r, ldt16_r, brt_r, bit_r,
             dlbr_r, dlbi_r, dbbr_r, dbbi_r, dctr_r, dcti_r,
             glr_r, gli_r, gldt_r, gbrt_r, gbit_r, gcre_r, gcim_r):
        rep, mask = _strip_selectors()

        def fold(acc):
            return sum(lax.dot_general(t, rep, (((1,), (1,)), ((), ())), preferred_element_type=F32)
                       for t in _split3(jnp.where(mask, acc, 0.0)))

        g_lb_re = jnp.sum(dlbr_r[...], axis=0)
        g_lb_im = jnp.sum(dlbi_r[...], axis=0)
        g_bb_re = fold(dbbr_r[...])
        g_bb_im = fold(dbbi_r[...])
        gcre_r[...] = fold(dctr_r[...])
        gcim_r[...] = fold(dcti_r[...])
        prim = (lr_r[...], li_r[...], ldt_r[...], lr16_r[...], li16_r[...], ldt16_r[...], brt_r[...], bit_r[...])
        _, vjp = jax.vjp(_ssm_discretise, *prim)
        g_lr, g_li, g_ldt, g_lr16, g_li16, g_ldt16, g_brt, g_bit = vjp((g_lb_re, g_lb_im, g_bb_re, g_bb_im))
        grp = lax.broadcasted_iota(jnp.int32, (N_GROUPS, SSM_W), 0)
        rw = lax.broadcasted_iota(jnp.int32, (N_GROUPS, SSM_W), 1)
        gsum = ((rw >> 4) == grp).astype(BF16)

        def group_sum(v):
            return sum(jnp.dot(gsum, t, preferred_element_type=F32) for t in _split3(v))

        glr_r[...] = g_lr + group_sum(g_lr16)
        gli_r[...] = g_li + group_sum(g_li16)
        gldt_r[...] = g_ldt + jnp.sum(group_sum(g_ldt16), axis=1, keepdims=True)
        gbrt_r[...] = g_brt
        gbit_r[...] = g_bit

    vm = pl.BlockSpec(memory_space=pltpu.VMEM)
    gp = jax.ShapeDtypeStruct((N_GROUPS, N_STATE), F32)
    gb = jax.ShapeDtypeStruct((SSM_W, N_STATE), F32)
    return pl.pallas_call(
        body, name="ssm_param_grads",
        out_shape=(gp, gp, jax.ShapeDtypeStruct((N_GROUPS, 1), F32), gb, gb, gb, gb),
        in_specs=[vm] * 14, out_specs=(vm,) * 7,
    )(lr, li, ldt, lr16, li16, ldt16, brt, bit, dlbr, dlbi, dbbr, dbbi, dctr, dcti)


def _in_proj(x, w_in_st, b_in):
    t = x.shape[0]
    tm = 512

    def body(x_r, w_r, b_r, o_r):
        o_r[...] = _dot(x_r[...], w_r[0]) + b_r[...]

    return pl.pallas_call(
        body, name="in_proj", grid=(t // tm, N_SHARDS),
        out_shape=jax.ShapeDtypeStruct((t, IN_COLS), F32),
        in_specs=[pl.BlockSpec((tm, D_MODEL), lambda i, j: (i, 0)),
                  pl.BlockSpec((1, D_MODEL, D_MODEL), lambda i, j: (j, 0, 0)),
                  pl.BlockSpec((1, D_MODEL), lambda i, j: (0, j))],
        out_specs=pl.BlockSpec((tm, D_MODEL), lambda i, j: (i, j)),
        compiler_params=_params(("parallel", "arbitrary")),
    )(x, w_in_st, b_in)


def _cmul_add(xr, xi, mr, mi, sr, si):
    return xr + (mr * sr - mi * si), xi + (mr * si + mi * sr)


def _ssm_forward(proj, bbr, bbi, ctr, cti, d_skip, tab, tc=512):
    t = proj.shape[0]
    nb = tc // SUBLANES

    def body(u_r, bbr_r, bbi_r, ctr_r, cti_r, d_r, tab_r, xsr_r, xsi_r, y_r, car_r, car_i):
        @pl.when(pl.program_id(1) == 0)
        def _():
            car_r[...] = jnp.zeros_like(car_r)
            car_i[...] = jnp.zeros_like(car_i)

        u = u_r[...]
        xsr_r[...] = _dot(u, bbr_r[...])
        xsi_r[...] = _dot(u, bbi_r[...])

        def block(b, carry):
            cr, ci = carry
            rows = pl.ds(pl.multiple_of(b * SUBLANES, SUBLANES), SUBLANES)
            xr = xsr_r[rows, :]
            xi = xsi_r[rows, :]
            for lvl, s in ((0, 1), (1, 2), (2, 4)):
                xr, xi = _cmul_add(xr, xi, tab_r[lvl], tab_r[4 + lvl],
                                   pltpu.roll(xr, s, 0), pltpu.roll(xi, s, 0))
            xr, xi = _cmul_add(xr, xi, tab_r[3], tab_r[7],
                               jnp.broadcast_to(cr, xr.shape), jnp.broadcast_to(ci, xi.shape))
            xsr_r[rows, :] = xr
            xsi_r[rows, :] = xi
            return xr[SUBLANES - 1:SUBLANES, :], xi[SUBLANES - 1:SUBLANES, :]

        cr, ci = lax.fori_loop(0, nb, block, (car_r[...], car_i[...]))
        car_r[...] = cr
        car_i[...] = ci
        y_r[...] = _dot_t(xsr_r[...], ctr_r[...]) - _dot_t(xsi_r[...], cti_r[...]) + d_r[...] * u

    strip_mat = pl.BlockSpec((128, STRIP), lambda j, k: (j, 0))
    states = pl.BlockSpec((tc, STRIP), lambda j, k: (k, j))
    return pl.pallas_call(
        body, name="ssm_forward", grid=(N_STRIPS, t // tc),
        out_shape=(jax.ShapeDtypeStruct((t, STATE_COLS), F32), jax.ShapeDtypeStruct((t, STATE_COLS), F32),
                   jax.ShapeDtypeStruct((t, SSM_W), F32)),
        in_specs=[pl.BlockSpec((tc, 128), lambda j, k: (k, j)),
                  strip_mat, strip_mat, strip_mat, strip_mat,
                  pl.BlockSpec((1, 128), lambda j, k: (0, j)),
                  pl.BlockSpec((16, SUBLANES, STRIP), lambda j, k: (0, 0, j))],
        out_specs=(states, states, pl.BlockSpec((tc, 128), lambda j, k: (k, j))),
        scratch_shapes=[pltpu.VMEM((1, STRIP), F32), pltpu.VMEM((1, STRIP), F32)],
        compiler_params=_params(("parallel", "arbitrary")),
    )(proj, bbr, bbi, ctr, cti, d_skip, tab)


def _shift_down(v, prev, n):
    row = lax.broadcasted_iota(jnp.int32, v.shape, 0)
    out = pltpu.roll(v, n, 0)
    for r in range(n):
        src = prev[SUBLANES - n + r:SUBLANES - n + r + 1, :]
        out = jnp.where(row == r, jnp.broadcast_to(src, v.shape), out)
    return out


def _shift_up(v, nxt, n):
    rows = v.shape[0]
    row = lax.broadcasted_iota(jnp.int32, v.shape, 0)
    out = pltpu.roll(v, rows - n, 0)
    for r in range(n):
        src = nxt[r:r + 1, :]
        out = jnp.where(row == rows - n + r, jnp.broadcast_to(src, v.shape), out)
    return out


def _conv3(q, q_prev, w):
    return w[2:3, :] * q + w[1:2, :] * _shift_down(q, q_prev, 1) + w[0:1, :] * _shift_down(q, q_prev, 2)


def _mixer_forward(x, proj, ya0, glu_w, glu_b, wso_st, conv_w8, wco_st, w_o, tm=256):
    t = x.shape[0]
    hb = tm // SUBLANES

    def body(x_r, ya0_r, h_r, cg_r, bg_r, ga_r, gb_r, hp_r, cgp_r,
             glu_w_r, glu_b_r, wso_r, cw_r, wco_r, wo_r, xh_r, rstd_r, ya_r, yb_r):
        i = pl.program_id(0)
        g, _ = _gelu_parts(ya0_r[...])
        ya1 = g * _sigmoid(_dot(g, glu_w_r[...]) + glu_b_r[...])
        q = cg_r[...] * h_r[...]
        q_prev = jnp.where(i > 0, cgp_r[...] * hp_r[...], 0.0)
        yb0 = bg_r[...] * _conv3(q, q_prev, cw_r[...])
        for j in range(N_SHARDS):
            ya_r[:, 256 * j:256 * (j + 1)] = _dot(ya1, wso_r[j])
            yb_r[:, 256 * j:256 * (j + 1)] = _dot(yb0, wco_r[j])
        merged = _sigmoid(ga_r[...]) * ya_r[...] + _sigmoid(gb_r[...]) * yb_r[...]
        r1 = ALPHA * x_r[...] + _dot(merged, wo_r[...])
        mu = jnp.mean(r1, axis=-1, keepdims=True)
        cen = r1 - mu
        rstd = lax.rsqrt(jnp.mean(cen * cen, axis=-1, keepdims=True) + LN_EPS)
        xh_r[...] = cen * rstd
        rstd_r[...] = rstd

    def col(w, c):
        return pl.BlockSpec((tm, w), lambda i: (i, c))

    def prev(c):
        return pl.BlockSpec((SUBLANES, SSM_W), lambda i: (jnp.maximum(i * hb - 1, 0), c))

    return pl.pallas_call(
        body, name="mixer_forward", grid=(t // tm,),
        out_shape=(jax.ShapeDtypeStruct((t, D_MODEL), F32), jax.ShapeDtypeStruct((t, 1), F32),
                   jax.ShapeDtypeStruct((t, D_MODEL), F32), jax.ShapeDtypeStruct((t, D_MODEL), F32)),
        in_specs=[col(D_MODEL, 0), col(SSM_W, 0), col(SSM_W, 1), col(SSM_W, 2), col(SSM_W, 3),
                  col(D_MODEL, 2), col(D_MODEL, 3), prev(1), prev(2),
                  _const((SSM_W, SSM_W)), _const((1, SSM_W)), _const((N_SHARDS, SSM_W, 256)),
                  _const((SUBLANES, SSM_W)), _const((N_SHARDS, SSM_W, 256)), _const((D_MODEL, D_MODEL))],
        out_specs=(col(D_MODEL, 0), pl.BlockSpec((tm, 1), lambda i: (i, 0)), col(D_MODEL, 0), col(D_MODEL, 0)),
        compiler_params=_params(("parallel",)),
    )(x, ya0, proj, proj, proj, proj, proj, proj, proj, glu_w, glu_b, wso_st, conv_w8, wco_st, w_o)


def _layer_norm_bwd(dxhat, xhat, rstd):
    m1 = jnp.mean(dxhat, axis=-1, keepdims=True)
    m2 = jnp.mean(dxhat * xhat, axis=-1, keepdims=True)
    return rstd * (dxhat - m1 - xhat * m2)


def _ffn_step(xhat1, rstd1, target, ln1_g, ln1_b, ln2_g, ln2_b, wg_st, wu_st, wd_st, tm=256):
    t = xhat1.shape[0]

    def body(xh_r, rstd_r, tgt_r, g1_r, b1_r, g2_r, b2_r, wg_r, wu_r, wd_r,
             loss_r, dr1_r, x1b_r, dr2b_r, hid_r, dhg_r, dhu_r, dg2_r, db2_r, dg1_r, db1_r,
             hg_s, hu_s):
        @pl.when(pl.program_id(0) == 0)
        def _():
            for r in (loss_r, dg2_r, db2_r, dg1_r, db1_r):
                r[...] = jnp.zeros_like(r)

        xhat1_v = xh_r[...]
        x1 = xhat1_v * g1_r[...] + b1_r[...]
        x1b = x1.astype(BF16)
        x1b_r[...] = x1b
        ffn = jnp.zeros((tm, D_MODEL), F32)
        for j in range(N_SHARDS):
            hg = jnp.dot(x1b, wg_r[j], preferred_element_type=F32)
            hu = jnp.dot(x1b, wu_r[j], preferred_element_type=F32)
            hg_s[j] = hg
            hu_s[j] = hu
            hid = (hg * _sigmoid(hg) * hu).astype(BF16)
            hid_r[j] = hid
            ffn = ffn + jnp.dot(hid, wd_r[j], preferred_element_type=F32)
        r2 = ALPHA * x1 + ffn
        mu = jnp.mean(r2, axis=-1, keepdims=True)
        cen = r2 - mu
        rstd2 = lax.rsqrt(jnp.mean(cen * cen, axis=-1, keepdims=True) + LN_EPS)
        xhat2 = cen * rstd2
        diff = (xhat2 * g2_r[...] + b2_r[...]) - tgt_r[...]
        loss_r[...] += 0.5 * jnp.sum(jnp.mean(diff * diff, axis=-1, keepdims=True), axis=0, keepdims=True)
        dy = diff * (1.0 / D_MODEL)
        dg2_r[...] += jnp.sum(dy * xhat2, axis=0, keepdims=True)
        db2_r[...] += jnp.sum(dy, axis=0, keepdims=True)
        dr2 = _layer_norm_bwd(dy * g2_r[...], xhat2, rstd2)
        dr2b = dr2.astype(BF16)
        dr2b_r[...] = dr2b
        dx1 = ALPHA * dr2
        for j in range(N_SHARDS):
            dhid = lax.dot_general(dr2b, wd_r[j], (((1,), (1,)), ((), ())), preferred_element_type=F32)
            hg = hg_s[j]
            hu = hu_s[j]
            sg = _sigmoid(hg)
            dhu = (dhid * (hg * sg)).astype(BF16)
            dhg = (dhid * hu * (sg * (1.0 + hg * (1.0 - sg)))).astype(BF16)
            dhg_r[j] = dhg
            dhu_r[j] = dhu
            dx1 = dx1 + lax.dot_general(dhg, wg_r[j], (((1,), (1,)), ((), ())), preferred_element_type=F32)
            dx1 = dx1 + lax.dot_general(dhu, wu_r[j], (((1,), (1,)), ((), ())), preferred_element_type=F32)
        dg1_r[...] += jnp.sum(dx1 * xhat1_v, axis=0, keepdims=True)
        db1_r[...] += jnp.sum(dx1, axis=0, keepdims=True)
        dr1_r[...] = _layer_norm_bwd(dx1 * g1_r[...], xhat1_v, rstd_r[...])

    tile = pl.BlockSpec((tm, D_MODEL), lambda i: (i, 0))
    hidden = pl.BlockSpec((N_SHARDS, tm, FFN_SHARD), lambda i: (0, i, 0))
    vec = _const((1, D_MODEL))
    hid_shape = jax.ShapeDtypeStruct((N_SHARDS, t, FFN_SHARD), BF16)
    vec_shape = jax.ShapeDtypeStruct((1, D_MODEL), F32)
    return pl.pallas_call(
        body, name="ffn_step", grid=(t // tm,),
        out_shape=(jax.ShapeDtypeStruct((1, 1), F32), jax.ShapeDtypeStruct((t, D_MODEL), F32),
                   jax.ShapeDtypeStruct((t, D_MODEL), BF16), jax.ShapeDtypeStruct((t, D_MODEL), BF16),
                   hid_shape, hid_shape, hid_shape, vec_shape, vec_shape, vec_shape, vec_shape),
        in_specs=[tile, pl.BlockSpec((tm, 1), lambda i: (i, 0)), tile, vec, vec, vec, vec,
                  _const((N_SHARDS, D_MODEL, FFN_SHARD)), _const((N_SHARDS, D_MODEL, FFN_SHARD)),
                  _const((N_SHARDS, FFN_SHARD, D_MODEL))],
        out_specs=(_const((1, 1)), tile, tile, tile, hidden, hidden, hidden, vec, vec, vec, vec),
        scratch_shapes=[pltpu.VMEM((N_SHARDS, tm, FFN_SHARD), F32), pltpu.VMEM((N_SHARDS, tm, FFN_SHARD), F32)],
        compiler_params=_params(("arbitrary",)),
    )(xhat1, rstd1, target, ln1_g, ln1_b, ln2_g, ln2_b, wg_st, wu_st, wd_st)


def _ffn_weight_grads(x1b, dr2b, hid, dhg, dhu, tk=512):
    t = x1b.shape[0]

    def body(x_r, dr_r, hid_r, dhg_r, dhu_r, gwg_r, gwu_r, gwd_r):
        @pl.when(pl.program_id(1) == 0)
        def _():
            for r in (gwg_r, gwu_r, gwd_r):
                r[...] = jnp.zeros_like(r)

        gwg_r[0] += _tdot(x_r[...], dhg_r[0])
        gwu_r[0] += _tdot(x_r[...], dhu_r[0])
        gwd_r[0] += _tdot(hid_r[0], dr_r[...])

    tile = pl.BlockSpec((tk, D_MODEL), lambda j, k: (k, 0))
    hidden = pl.BlockSpec((1, tk, FFN_SHARD), lambda j, k: (j, k, 0))
    col = pl.BlockSpec((1, D_MODEL, FFN_SHARD), lambda j, k: (j, 0, 0))
    row = pl.BlockSpec((1, FFN_SHARD, D_MODEL), lambda j, k: (j, 0, 0))
    return pl.pallas_call(
        body, name="ffn_weight_grads", grid=(N_SHARDS, t // tk),
        out_shape=(jax.ShapeDtypeStruct((N_SHARDS, D_MODEL, FFN_SHARD), F32),) * 2
        + (jax.ShapeDtypeStruct((N_SHARDS, FFN_SHARD, D_MODEL), F32),),
        in_specs=[tile, tile, hidden, hidden, hidden],
        out_specs=(col, col, row),
        compiler_params=_params(("parallel", "arbitrary")),
    )(x1b, dr2b, hid, dhg, dhu)


def _mixer_backward(dr1, proj, ya0, ya, yb, glu_w, glu_b, wso_st, conv_w8, wco_st, w_o, tm=256):
    t = dr1.shape[0]
    hb = tm // SUBLANES
    last_block = t // SUBLANES - 1

    def body(dr1_r, dr1n_r, ya0_r, ya_r, yb_r, h_r, cg_r, bg_r, ga_r, gb_r, hp_r, cgp_r, bgn_r, gbn_r,
             glu_w_r, glu_b_r, wso_r, cw_r, wco_r, wo_r,
             dya0_r, dproj_r, dbias_r, gwo_r, gwso_r, gwco_r, gglu_w_r, gglu_b_r, gconv_r):
        i = pl.program_id(0)

        @pl.when(i == 0)
        def _():
            for r in (dbias_r, gwo_r, gwso_r, gwco_r, gglu_w_r, gglu_b_r, gconv_r):
                r[...] = jnp.zeros_like(r)

        dr1_v = dr1_r[...]
        dmerged = _dot_t(dr1_v, wo_r[...])
        sa = _sigmoid(ga_r[...])
        sb = _sigmoid(gb_r[...])
        ya_v = ya_r[...]
        yb_v = yb_r[...]
        gwo_r[...] += _tdot(sa * ya_v + sb * yb_v, dr1_v)
        dya = dmerged * sa
        dyb = dmerged * sb
        dga = dmerged * ya_v * (sa * (1.0 - sa))
        dgb = dmerged * yb_v * (sb * (1.0 - sb))

        g, gelu_grad = _gelu_parts(ya0_r[...])
        s1 = _sigmoid(_dot(g, glu_w_r[...]) + glu_b_r[...])
        ya1 = g * s1
        dya1 = jnp.zeros((tm, SSM_W), F32)
        for j in range(N_SHARDS):
            dya_j = dya[:, 256 * j:256 * (j + 1)]
            gwso_r[j] += _tdot(ya1, dya_j)
            dya1 = dya1 + _dot_t(dya_j, wso_r[j])
        dz1 = dya1 * g * (s1 * (1.0 - s1))
        gglu_b_r[...] += jnp.sum(dz1, axis=0, keepdims=True)
        gglu_w_r[...] += _tdot(g, dz1)
        dya0_r[...] = (dya1 * s1 + _dot_t(dz1, glu_w_r[...])) * gelu_grad

        cw = cw_r[...]
        h = h_r[...]
        cg = cg_r[...]
        bg = bg_r[...]
        q = cg * h
        q_prev = jnp.where(i > 0, cgp_r[...] * hp_r[...], 0.0)
        q1 = _shift_down(q, q_prev, 1)
        q2 = _shift_down(q, q_prev, 2)
        z = cw[2:3, :] * q + cw[1:2, :] * q1 + cw[0:1, :] * q2
        yb0 = bg * z
        dyb0 = jnp.zeros((tm, SSM_W), F32)
        for j in range(N_SHARDS):
            dyb_j = dyb[:, 256 * j:256 * (j + 1)]
            gwco_r[j] += _tdot(yb0, dyb_j)
            dyb0 = dyb0 + _dot_t(dyb_j, wco_r[j])
        dbg = dyb0 * z
        dz = dyb0 * bg
        dyb_n = _dot_t(dr1n_r[...], wo_r[...]) * _sigmoid(gbn_r[...])
        dyb0_n = jnp.zeros((SUBLANES, SSM_W), F32)
        for j in range(N_SHARDS):
            dyb0_n = dyb0_n + _dot_t(dyb_n[:, 256 * j:256 * (j + 1)], wco_r[j])
        dz_next = jnp.where(i < pl.num_programs(0) - 1, dyb0_n * bgn_r[...], 0.0)
        dq = cw[2:3, :] * dz + cw[1:2, :] * _shift_up(dz, dz_next, 1) + cw[0:1, :] * _shift_up(dz, dz_next, 2)
        gconv_r[0:1, :] += jnp.sum(dz * q2, axis=0, keepdims=True)
        gconv_r[1:2, :] += jnp.sum(dz * q1, axis=0, keepdims=True)
        gconv_r[2:3, :] += jnp.sum(dz * q, axis=0, keepdims=True)
        dh = dq * cg
        dcg = dq * h

        dproj_r[:, 0:512] = jnp.zeros((tm, SSM_W), BF16)
        pieces = ((512, dh), (1024, dcg), (1536, dbg), (2048, dga), (3072, dgb))
        for off, val in pieces:
            w = val.shape[1]
            dproj_r[:, off:off + w] = val.astype(BF16)
            dbias_r[:, off:off + w] += jnp.sum(val, axis=0, keepdims=True)

    def col(w, c):
        return pl.BlockSpec((tm, w), lambda i: (i, c))

    def prev(c):
        return pl.BlockSpec((SUBLANES, SSM_W), lambda i: (jnp.maximum(i * hb - 1, 0), c))

    def nxt(w, c):
        return pl.BlockSpec((SUBLANES, w), lambda i: (jnp.minimum((i + 1) * hb, last_block), c))

    sh = jax.ShapeDtypeStruct
    return pl.pallas_call(
        body, name="mixer_backward", grid=(t // tm,),
        out_shape=(sh((t, SSM_W), F32), sh((t, IN_COLS), BF16), sh((1, IN_COLS), F32),
                   sh((D_MODEL, D_MODEL), F32), sh((N_SHARDS, SSM_W, 256), F32), sh((N_SHARDS, SSM_W, 256), F32),
                   sh((SSM_W, SSM_W), F32), sh((1, SSM_W), F32), sh((SUBLANES, SSM_W), F32)),
        in_specs=[col(D_MODEL, 0), nxt(D_MODEL, 0), col(SSM_W, 0), col(D_MODEL, 0), col(D_MODEL, 0),
                  col(SSM_W, 1), col(SSM_W, 2), col(SSM_W, 3), col(D_MODEL, 2), col(D_MODEL, 3),
                  prev(1), prev(2), nxt(SSM_W, 3), nxt(D_MODEL, 3),
                  _const((SSM_W, SSM_W)), _const((1, SSM_W)), _const((N_SHARDS, SSM_W, 256)),
                  _const((SUBLANES, SSM_W)), _const((N_SHARDS, SSM_W, 256)), _const((D_MODEL, D_MODEL))],
        out_specs=(col(SSM_W, 0), col(IN_COLS, 0), _const((1, IN_COLS)),
                   _const((D_MODEL, D_MODEL)), _const((N_SHARDS, SSM_W, 256)), _const((N_SHARDS, SSM_W, 256)),
                   _const((SSM_W, SSM_W)), _const((1, SSM_W)), _const((SUBLANES, SSM_W))),
        compiler_params=_params(("arbitrary",)),
    )(dr1, dr1, ya0, ya, yb, proj, proj, proj, proj, proj, proj, proj, proj, proj,
      glu_w, glu_b, wso_st, conv_w8, wco_st, w_o)


def _ssm_backward(dya0, proj, xsr, xsi, bbr, bbi, ctr, cti, d_skip, tab, dproj, tc=512):
    t = proj.shape[0]
    nb = tc // SUBLANES
    nk = t // tc

    def body(dy_r, u_r, xsr_r, xsi_r, bbr_r, bbi_r, ctr_r, cti_r, d_r, tab_r, dproj_any,
             du_r, dus_r, gbbr_r, gbbi_r, gctr_r, gcti_r, glbr_r, glbi_r, gd_r,
             gr_s, gi_s, car_r, car_i):
        del dproj_any

        @pl.when(pl.program_id(1) == 0)
        def _():
            for r in (car_r, car_i, dus_r, gbbr_r, gbbi_r, gctr_r, gcti_r, glbr_r, glbi_r, gd_r):
                r[...] = jnp.zeros_like(r)

        dy = dy_r[...]
        u = u_r[...]
        gr_s[...] = _dot(dy, ctr_r[...])
        gi_s[...] = -_dot(dy, cti_r[...])
        last_row = lax.broadcasted_iota(jnp.int32, (SUBLANES, STRIP), 0) == SUBLANES - 1

        def block(n, carry):
            cr, ci, ar, ai = carry
            rows = pl.ds(pl.multiple_of((nb - 1 - n) * SUBLANES, SUBLANES), SUBLANES)
            gr = gr_s[rows, :]
            gi = gi_s[rows, :]
            for lvl, s in ((0, 1), (1, 2), (2, 4)):
                gr, gi = _cmul_add(gr, gi, tab_r[8 + lvl], tab_r[12 + lvl],
                                   pltpu.roll(gr, SUBLANES - s, 0), pltpu.roll(gi, SUBLANES - s, 0))
            crb = jnp.broadcast_to(cr, gr.shape)
            cib = jnp.broadcast_to(ci, gi.shape)
            gr, gi = _cmul_add(gr, gi, tab_r[11], tab_r[15], crb, cib)
            gr_s[rows, :] = gr
            gi_s[rows, :] = gi
            gnr = jnp.where(last_row, crb, pltpu.roll(gr, SUBLANES - 1, 0))
            gni = jnp.where(last_row, cib, pltpu.roll(gi, SUBLANES - 1, 0))
            xr = xsr_r[rows, :]
            xi = xsi_r[rows, :]
            ar = ar + (xr * gnr + xi * gni)
            ai = ai + (xr * gni - xi * gnr)
            return gr[0:1, :], gi[0:1, :], ar, ai

        zero = jnp.zeros((SUBLANES, STRIP), F32)
        cr, ci, ar, ai = lax.fori_loop(0, nb, block, (car_r[...], car_i[...], zero, zero))
        car_r[...] = cr
        car_i[...] = ci
        glbr_r[...] += ar
        glbi_r[...] += ai
        gr = gr_s[...]
        gi = gi_s[...]
        du = _dot_t(gr, bbr_r[...]) + _dot_t(gi, bbi_r[...]) + d_r[...] * dy
        du_r[...] = du.astype(BF16)
        dus_r[...] += jnp.sum(du, axis=0, keepdims=True)
        gd_r[...] += jnp.sum(dy * u, axis=0, keepdims=True)
        gbbr_r[...] += _tdot(u, gr)
        gbbi_r[...] += _tdot(u, gi)
        gctr_r[...] += _tdot(dy, xsr_r[...])
        gcti_r[...] -= _tdot(dy, xsi_r[...])

    def rev(w):
        return pl.BlockSpec((tc, w), lambda j, k: (nk - 1 - k, j))

    strip_mat = pl.BlockSpec((128, STRIP), lambda j, k: (j, 0))
    vec = pl.BlockSpec((1, 128), lambda j, k: (0, j))
    lbacc = pl.BlockSpec((SUBLANES, STRIP), lambda j, k: (0, j))
    sh = jax.ShapeDtypeStruct
    return pl.pallas_call(
        body, name="ssm_backward", grid=(N_STRIPS, nk),
        out_shape=(sh((t, IN_COLS), BF16), sh((1, SSM_W), F32),
                   sh((SSM_W, STRIP), F32), sh((SSM_W, STRIP), F32), sh((SSM_W, STRIP), F32), sh((SSM_W, STRIP), F32),
                   sh((SUBLANES, STATE_COLS), F32), sh((SUBLANES, STATE_COLS), F32), sh((1, SSM_W), F32)),
        in_specs=[rev(128), rev(128), rev(STRIP), rev(STRIP),
                  strip_mat, strip_mat, strip_mat, strip_mat, vec,
                  pl.BlockSpec((16, SUBLANES, STRIP), lambda j, k: (0, 0, j)), ANY],
        out_specs=(rev(128), vec, strip_mat, strip_mat, strip_mat, strip_mat, lbacc, lbacc, vec),
        scratch_shapes=[pltpu.VMEM((tc, STRIP), F32), pltpu.VMEM((tc, STRIP), F32),
                        pltpu.VMEM((1, STRIP), F32), pltpu.VMEM((1, STRIP), F32)],
        input_output_aliases={10: 0},
        compiler_params=_params(("parallel", "arbitrary")),
    )(dya0, proj, xsr, xsi, bbr, bbi, ctr, cti, d_skip, tab, dproj)


def _input_grad(dr1, dproj, w_in_st, tm=256):
    t = dr1.shape[0]

    def body(dr1_r, dp_r, w_r, dx_r):
        acc = ALPHA * dr1_r[...]
        for j in range(N_SHARDS):
            acc = acc + lax.dot_general(dp_r[:, D_MODEL * j:D_MODEL * (j + 1)], w_r[j],
                                        (((1,), (1,)), ((), ())), preferred_element_type=F32)
        dx_r[...] = acc

    return pl.pallas_call(
        body, name="input_grad", grid=(t // tm,),
        out_shape=jax.ShapeDtypeStruct((t, D_MODEL), F32),
        in_specs=[pl.BlockSpec((tm, D_MODEL), lambda i: (i, 0)), pl.BlockSpec((tm, IN_COLS), lambda i: (i, 0)),
                  _const((N_SHARDS, D_MODEL, D_MODEL))],
        out_specs=pl.BlockSpec((tm, D_MODEL), lambda i: (i, 0)),
        compiler_params=_params(("parallel",)),
    )(dr1, dproj, w_in_st)


def _in_weight_grad(x, dproj, tk=512):
    t = x.shape[0]

    def body(x_r, dp_r, gw_r):
        @pl.when(pl.program_id(1) == 0)
        def _():
            gw_r[...] = jnp.zeros_like(gw_r)

        gw_r[0] += _tdot(x_r[...], dp_r[...])

    return pl.pallas_call(
        body, name="in_weight_grad", grid=(N_SHARDS, t // tk),
        out_shape=jax.ShapeDtypeStruct((N_SHARDS, D_MODEL, D_MODEL), F32),
        in_specs=[pl.BlockSpec((tk, D_MODEL), lambda j, k: (k, 0)), pl.BlockSpec((tk, D_MODEL), lambda j, k: (k, j))],
        out_specs=pl.BlockSpec((1, D_MODEL, D_MODEL), lambda j, k: (j, 0, 0)),
        compiler_params=_params(("parallel", "arbitrary")),
    )(x, dproj)


def _local_step(x, target, small, wts):
    lr, li = small["ssm_lambda_re"][0], small["ssm_lambda_im"][0]
    ldt = small["ssm_log_dt"][0][:, None]
    rep16 = lambda a: jnp.broadcast_to(a[:, None, :], (N_GROUPS, GROUP_C, a.shape[-1])).reshape(SSM_W, a.shape[-1])
    lr16, li16 = rep16(lr), rep16(li)
    ldt16 = rep16(jnp.broadcast_to(ldt, (N_GROUPS, N_STATE)))
    brt = small["ssm_b_re"][0].transpose(0, 2, 1).reshape(SSM_W, N_STATE)
    bit = small["ssm_b_im"][0].transpose(0, 2, 1).reshape(SSM_W, N_STATE)
    cre = small["ssm_c_re"][0].reshape(SSM_W, N_STATE)
    cim = small["ssm_c_im"][0].reshape(SSM_W, N_STATE)
    disc = (lr, li, ldt, lr16, li16, ldt16, brt, bit)

    pwr, pwi, bbr, bbi, ctr, cti = _ssm_prepare(*disc, cre, cim)
    tab = _scan_tables(pwr, pwi)
    conv_w8 = jnp.pad(wts["conv_w"][:, :3, :].transpose(1, 0, 2).reshape(3, SSM_W), ((0, SUBLANES - 3), (0, 0)))
    w_o = wts["w_o"].reshape(D_MODEL, D_MODEL)
    glu_w = wts["glu_w"].reshape(SSM_W, SSM_W)

    proj = _in_proj(x, wts["w_in"], small["b_in"])
    xsr, xsi, ya0 = _ssm_forward(proj, bbr, bbi, ctr, cti, small["ssm_d"], tab)
    xhat1, rstd1, ya, yb = _mixer_forward(x, proj, ya0, glu_w, small["glu_b"], wts["w_ssm_out"], conv_w8,
                                          wts["w_conv_out"], w_o)
    (loss, dr1, x1b, dr2b, hid, dhg, dhu, g_ln2_g, g_ln2_b, g_ln1_g, g_ln1_b) = _ffn_step(
        xhat1, rstd1, target, small["ln1_g"], small["ln1_b"], small["ln2_g"], small["ln2_b"],
        wts["w_gate"], wts["w_up"], wts["w_down"])
    g_wg, g_wu, g_wd = _ffn_weight_grads(x1b, dr2b, hid, dhg, dhu)
    (dya0, dproj, dbias, g_wo, g_wso, g_wco, g_glu_w, g_glu_b, g_conv8) = _mixer_backward(
        dr1, proj, ya0, ya, yb, glu_w, small["glu_b"], wts["w_ssm_out"], conv_w8, wts["w_conv_out"], w_o)
    (dproj, dus, gbbr, gbbi, gctr, gcti, glbr, glbi, g_d) = _ssm_backward(
        dya0, proj, xsr, xsi, bbr, bbi, ctr, cti, small["ssm_d"], tab, dproj)
    g_lr, g_li, g_ldt, g_brt, g_bit, g_cre, g_cim = _ssm_param_grads(
        *disc, glbr.reshape(SUBLANES, N_GROUPS, N_STATE), glbi.reshape(SUBLANES, N_GROUPS, N_STATE),
        gbbr, gbbi, gctr, gcti)
    dx = _input_grad(dr1, dproj, wts["w_in"])
    g_w_in = _in_weight_grad(x, dproj)

    big = {
        "w_in": g_w_in,
        "glu_w": g_glu_w.reshape(N_SHARDS, 128, SSM_W),
        "w_ssm_out": g_wso,
        "conv_w": jnp.pad(g_conv8[:3].reshape(3, N_SHARDS, 128).transpose(1, 0, 2),
                          ((0, 0), (0, SUBLANES - 3), (0, 0))),
        "w_conv_out": g_wco,
        "w_o": g_wo.reshape(N_SHARDS, 256, D_MODEL),
        "w_gate": g_wg, "w_up": g_wu, "w_down": g_wd,
    }
    pieces = [dus, dbias[:, SSM_W:], g_lr, g_li, g_ldt, g_brt, g_bit, g_cre, g_cim, g_d, g_glu_b,
              g_ln1_g, g_ln1_b, g_ln2_g, g_ln2_b]
    flat = jnp.concatenate([p.reshape(-1) for p in pieces])
    packed = jnp.pad(flat, (0, PACKED_ROWS * 128 - flat.shape[0])).reshape(PACKED_ROWS, 128)
    return loss, dx, big, packed


PACKED_ROWS = 1136
PACKED_LAYOUT = (("b_in", IN_COLS), ("ssm_lambda_re", STATE_COLS), ("ssm_lambda_im", STATE_COLS),
                 ("ssm_log_dt", N_GROUPS), ("ssm_b_re", SSM_W * N_STATE), ("ssm_b_im", SSM_W * N_STATE),
                 ("ssm_c_re", SSM_W * N_STATE), ("ssm_c_im", SSM_W * N_STATE), ("ssm_d", SSM_W), ("glu_b", SSM_W),
                 ("ln1_g", D_MODEL), ("ln1_b", D_MODEL), ("ln2_g", D_MODEL), ("ln2_b", D_MODEL))


def _unpack_small(packed):
    flat = packed.reshape(-1)
    out, off = {}, 0
    for name, size in PACKED_LAYOUT:
        out[name] = flat[off:off + size]
        off += size
    for name in ("ssm_b_re", "ssm_b_im"):
        out[name] = out[name].reshape(N_GROUPS, GROUP_C, N_STATE).transpose(0, 2, 1)[None]
    for name in ("ssm_c_re", "ssm_c_im"):
        out[name] = out[name].reshape(1, N_GROUPS, GROUP_C, N_STATE)
    for name in ("ssm_lambda_re", "ssm_lambda_im"):
        out[name] = out[name].reshape(1, N_GROUPS, N_STATE)
    for name in ("b_in", "ssm_log_dt", "ssm_d", "glu_b", "ln1_g", "ln1_b", "ln2_g", "ln2_b"):
        out[name] = out[name][None]
    return out


BIG = ("w_in", "glu_w", "w_ssm_out", "w_conv_out", "w_o", "w_gate", "w_up", "w_down")
SMALL = ("b_in", "ssm_lambda_re", "ssm_lambda_im", "ssm_log_dt", "ssm_b_re", "ssm_b_im", "ssm_c_re", "ssm_c_im",
         "ssm_d", "glu_b", "ln1_g", "ln1_b", "ln2_g", "ln2_b")
WEIGHTS = ("w_in", "b_in", "ssm_lambda_re", "ssm_lambda_im", "ssm_log_dt", "ssm_b_re", "ssm_b_im", "ssm_c_re",
           "ssm_c_im", "ssm_d", "glu_w", "glu_b", "w_ssm_out", "conv_w", "w_conv_out", "w_o", "ln1_g", "ln1_b",
           "w_gate", "w_up", "w_down", "ln2_g", "ln2_b")


def _place():
    x, y, c = lax.axis_index("x"), lax.axis_index("y"), lax.axis_index("c")
    chips = [(1 - x, y), (x, 1 - y), (1 - x, 1 - y)]
    return x, y, c, chips


def _shard_of(chip):
    return 2 * chip[0] + chip[1]


def _remote(src, dst, send_sem, recv_sem, to):
    return pltpu.make_async_remote_copy(src_ref=src, dst_ref=dst, send_sem=send_sem, recv_sem=recv_sem,
                                        device_id=to, device_id_type=MESH)


def _gather_weights(shards):
    n = len(shards)
    n_big = n - 1

    def body(*refs):
        src, dst = refs[:n], refs[n:2 * n]
        send_sem, recv_sem, fsend_sem, frecv_sem = refs[2 * n:]
        x, y, c, chips = _place()
        me = _shard_of((x, y))
        sibling = (x, y, 1 - c)

        def half(a, which):
            r2 = shards[a].shape[0] // 2
            return pl.ds(pl.multiple_of(which * r2, 16), r2)

        sends = []
        for a in range(n):
            for k, chip in enumerate(chips):
                if a < n_big:
                    cp = _remote(src[a].at[half(a, c)], dst[a].at[me, half(a, c)],
                                 send_sem.at[a, k], recv_sem.at[a, k], (*chip, c))
                else:
                    cp = _remote(src[a], dst[a].at[me], send_sem.at[a, k], recv_sem.at[a, k], (*chip, c))
                cp.start()
                sends.append(cp)
        for a in range(n):
            for k, chip in enumerate(chips):
                got = _shard_of(chip)
                if a < n_big:
                    rows = dst[a].at[got, half(a, c)]
                    _remote(rows, rows, send_sem.at[a, k], recv_sem.at[a, k], sibling).wait_recv()
                    cp = _remote(rows, rows, fsend_sem.at[a, k], frecv_sem.at[a, k], sibling)
                    cp.start()
                    sends.append(cp)
                else:
                    _remote(dst[a].at[got], dst[a].at[got], send_sem.at[a, k], recv_sem.at[a, k], sibling).wait_recv()
        for a in range(n_big):
            for k, chip in enumerate(chips):
                rows = dst[a].at[_shard_of(chip), half(a, 1 - c)]
                _remote(rows, rows, fsend_sem.at[a, k], frecv_sem.at[a, k], sibling).wait_recv()
        for cp in sends:
            cp.wait_send()

    stacks = pl.pallas_call(
        body, name="gather_weights",
        out_shape=tuple(jax.ShapeDtypeStruct((N_SHARDS,) + s.shape, s.dtype) for s in shards),
        in_specs=[ANY] * n, out_specs=(ANY,) * n,
        scratch_shapes=[pltpu.SemaphoreType.DMA((n, 3))] * 4,
    )(*shards)
    me = _shard_of((lax.axis_index("x"), lax.axis_index("y")))
    return [lax.dynamic_update_slice(st, sh[None], (me,) + (0,) * sh.ndim) for st, sh in zip(stacks, shards)]


def _swap_with_sibling(big, small):
    nb, n = len(big), len(big) + len(small)
    arrays = list(big) + list(small)

    def body(*refs):
        src, dst = refs[:n], refs[n:2 * n]
        send_sem, recv_sem = refs[2 * n:]
        x, y, c, _ = _place()
        sibling = (x, y, 1 - c)
        copies = []
        for a in range(n):
            if a < nb:
                r2 = arrays[a].shape[1] // 2
                part = src[a].at[:, pl.ds(pl.multiple_of((1 - c) * r2, SUBLANES), r2), :]
            else:
                part = src[a]
            cp = _remote(part, dst[a], send_sem.at[a], recv_sem.at[a], sibling)
            cp.start()
            copies.append(cp)
        for cp in copies:
            cp.wait()

    out_shape = tuple(jax.ShapeDtypeStruct((N_SHARDS, g.shape[1] // 2, g.shape[2]), g.dtype) for g in big)
    out_shape += tuple(jax.ShapeDtypeStruct(g.shape, g.dtype) for g in small)
    out = pl.pallas_call(
        body, name="swap_with_sibling", out_shape=out_shape, in_specs=[ANY] * n, out_specs=(ANY,) * n,
        scratch_shapes=[pltpu.SemaphoreType.DMA((n,)), pltpu.SemaphoreType.DMA((n,))],
    )(*arrays)
    return out[:nb], out[nb:]


def _scatter_to_chips(big, conv, small):
    slabbed = list(big) + [conv]
    ns, n = len(slabbed), len(slabbed) + len(small)
    arrays = slabbed + list(small)

    def body(*refs):
        src, dst = refs[:n], refs[n:2 * n]
        send_sem, recv_sem = refs[2 * n:]
        _, _, c, chips = _place()
        copies = []
        for a in range(n):
            for k, chip in enumerate(chips):
                part = src[a].at[_shard_of(chip)] if a < ns else src[a]
                cp = _remote(part, dst[a].at[k], send_sem.at[a, k], recv_sem.at[a, k], (*chip, c))
                cp.start()
                copies.append(cp)
        for cp in copies:
            cp.wait()

    out_shape = tuple(jax.ShapeDtypeStruct((3,) + g.shape[1:], g.dtype) for g in slabbed)
    out_shape += tuple(jax.ShapeDtypeStruct((3,) + g.shape, g.dtype) for g in small)
    out = pl.pallas_call(
        body, name="scatter_to_chips", out_shape=out_shape, in_specs=[ANY] * n, out_specs=(ANY,) * n,
        scratch_shapes=[pltpu.SemaphoreType.DMA((n, 3)), pltpu.SemaphoreType.DMA((n, 3))],
    )(*arrays)
    return out[:len(big)], out[len(big)], out[ns:]


def _send_to_sibling(arrays):
    n = len(arrays)

    def body(*refs):
        src, dst = refs[:n], refs[n:2 * n]
        send_sem, recv_sem = refs[2 * n:]
        x, y, c, _ = _place()
        copies = [_remote(src[a], dst[a], send_sem.at[a], recv_sem.at[a], (x, y, 1 - c)) for a in range(n)]
        for cp in copies:
            cp.start()
        for cp in copies:
            cp.wait()

    return pl.pallas_call(
        body, name="send_to_sibling",
        out_shape=tuple(jax.ShapeDtypeStruct(h.shape, h.dtype) for h in arrays),
        in_specs=[ANY] * n, out_specs=(ANY,) * n,
        scratch_shapes=[pltpu.SemaphoreType.DMA((n,))] * 2,
    )(*arrays)


def _row_chunk(rows):
    for cand in (256, 176, 128, 64):
        if rows % cand == 0:
            return cand
    return rows


def _add_own_half(stack, received, c):
    _, r2, cols = received.shape

    def body(c_ref, a_r, b_r, o_r):
        del c_ref
        o_r[...] = (a_r[...] + b_r[...]).astype(BF16)

    return pl.pallas_call(
        body, name="add_own_half",
        grid_spec=pltpu.PrefetchScalarGridSpec(
            num_scalar_prefetch=1, grid=(N_SHARDS,),
            in_specs=[pl.BlockSpec((1, r2, cols), lambda s, c_ref: (s, c_ref[0], 0)),
                      pl.BlockSpec((1, r2, cols), lambda s, c_ref: (s, 0, 0))],
            out_specs=pl.BlockSpec((1, r2, cols), lambda s, c_ref: (s, 0, 0))),
        out_shape=jax.ShapeDtypeStruct(received.shape, BF16),
        compiler_params=_params(("parallel",)),
    )(c, stack, received)


def _chip_order_sum(me, own, s0, s1, s2):
    terms = []
    for s in range(N_SHARDS):
        d = jnp.bitwise_xor(me, s)
        terms.append(jnp.where(d == 0, own, jnp.where(d == 2, s0, jnp.where(d == 1, s1, s2))))
    return ((terms[0] + terms[1]) + terms[2]) + terms[3]


def _sum_chips(own_stack, slots, me):
    _, rows, cols = slots.shape
    rc = _row_chunk(rows)

    def body(me_ref, own_r, s_r, o_r):
        o_r[...] = _chip_order_sum(me_ref[0], own_r[0].astype(F32), s_r[0].astype(F32), s_r[1].astype(F32),
                                   s_r[2].astype(F32))

    return pl.pallas_call(
        body, name="sum_chips",
        grid_spec=pltpu.PrefetchScalarGridSpec(
            num_scalar_prefetch=1, grid=(rows // rc,),
            in_specs=[pl.BlockSpec((1, rc, cols), lambda i, me_ref: (me_ref[0], i, 0)),
                      pl.BlockSpec((3, rc, cols), lambda i, me_ref: (0, i, 0))],
            out_specs=pl.BlockSpec((rc, cols), lambda i, me_ref: (i, 0))),
        out_shape=jax.ShapeDtypeStruct((rows, cols), F32),
        compiler_params=_params(("parallel",)),
    )(me, own_stack, slots)


def _small_pair_sums(mine, theirs):
    n = len(mine)

    def body(*refs):
        for a in range(n):
            refs[2 * n + a][...] = refs[a][...] + refs[n + a][...]

    vm = pl.BlockSpec(memory_space=pltpu.VMEM)
    return pl.pallas_call(
        body, name="small_pair_sums", out_shape=tuple(jax.ShapeDtypeStruct(g.shape, g.dtype) for g in mine),
        in_specs=[vm] * (2 * n), out_specs=(vm,) * n,
        compiler_params=pltpu.CompilerParams(vmem_limit_bytes=VMEM_LIMIT),
    )(*mine, *theirs)


def _adam_math(w, g, m, v):
    m = ADAM_B1 * m + (1.0 - ADAM_B1) * g
    v = ADAM_B2 * v + (1.0 - ADAM_B2) * (g * g)
    m_hat = m / (1.0 - ADAM_B1 ** ADAM_STEP)
    v_hat = v / (1.0 - ADAM_B2 ** ADAM_STEP)
    delta = -ADAM_LR * (m_hat / (jnp.sqrt(v_hat) + ADAM_EPS) + ADAM_WD * w)
    return delta, m, v


def _small_totals(me, conv_stack, conv_slots, packed, packed_slots):
    def body(me_ref, cs_r, cslot_r, p_r, pslot_r, conv_r, tot_r):
        me_ = me_ref[0]
        conv_r[...] = _chip_order_sum(me_, cs_r[me_], cslot_r[0], cslot_r[1], cslot_r[2])
        tot_r[...] = _chip_order_sum(me_, p_r[...], pslot_r[0], pslot_r[1], pslot_r[2])

    vm = pl.BlockSpec(memory_space=pltpu.VMEM)
    return pl.pallas_call(
        body, name="small_totals",
        out_shape=(jax.ShapeDtypeStruct(conv_stack.shape[1:], F32), jax.ShapeDtypeStruct(packed.shape, F32)),
        in_specs=[pl.BlockSpec(memory_space=pltpu.SMEM)] + [vm] * 4, out_specs=(vm, vm),
    )(me, conv_stack, conv_slots, packed, packed_slots)


def _adam_small(gs, ws, ms, vs):
    n = len(gs)

    def body(*refs):
        for a in range(n):
            g_r, w_r, m_r, v_r = (refs[i * n + a] for i in range(4))
            d_r, nm_r, nv_r = (refs[(4 + i) * n + a] for i in range(3))
            d_r[...], nm_r[...], nv_r[...] = _adam_math(w_r[...], g_r[...], m_r[...], v_r[...])

    vm = pl.BlockSpec(memory_space=pltpu.VMEM)
    shapes = tuple(jax.ShapeDtypeStruct(w.shape, F32) for w in ws)
    out = pl.pallas_call(
        body, name="adam_small", out_shape=shapes * 3, in_specs=[vm] * (4 * n), out_specs=(vm,) * (3 * n),
        compiler_params=pltpu.CompilerParams(vmem_limit_bytes=VMEM_LIMIT),
    )(*gs, *ws, *ms, *vs)
    return out[:n], out[n:2 * n], out[2 * n:]


def _adam_big(w, mine, other, m, v, c):
    r2, cols = mine.shape
    rc = _row_chunk(r2)
    nch = r2 // rc

    def body(c_ref, w_r, mine_r, other_r, m_r, v_r, g_r, d_r, nm_r, nv_r):
        g = jnp.where(pl.program_id(0) == c_ref[0], mine_r[...], other_r[...])
        g_r[...] = g
        d_r[...], nm_r[...], nv_r[...] = _adam_math(w_r[...], g, m_r[...], v_r[...])

    full = pl.BlockSpec((rc, cols), lambda h, i, c_ref: (h * nch + i, 0))
    half = pl.BlockSpec((rc, cols), lambda h, i, c_ref: (i, 0))
    shape = jax.ShapeDtypeStruct((2 * r2, cols), F32)
    return pl.pallas_call(
        body, name="adam_big",
        grid_spec=pltpu.PrefetchScalarGridSpec(
            num_scalar_prefetch=1, grid=(2, nch),
            in_specs=[full, half, half, full, full], out_specs=(full,) * 4),
        out_shape=(shape,) * 4, compiler_params=_params(("parallel", "parallel")),
    )(c, w.reshape(2 * r2, cols), mine, other, m.reshape(2 * r2, cols), v.reshape(2 * r2, cols))


def kernel(x, w_in, b_in, ssm_lambda_re, ssm_lambda_im, ssm_log_dt, ssm_b_re, ssm_b_im, ssm_c_re, ssm_c_im, ssm_d, glu_w, glu_b, w_ssm_out, conv_w, w_conv_out, w_o, ln1_g, ln1_b, w_gate, w_up, w_down, ln2_g, ln2_b, loss_target, m_w_in, m_b_in, m_ssm_lambda_re, m_ssm_lambda_im, m_ssm_log_dt, m_ssm_b_re, m_ssm_b_im, m_ssm_c_re, m_ssm_c_im, m_ssm_d, m_glu_w, m_glu_b, m_w_ssm_out, m_conv_w, m_w_conv_out, m_w_o, m_ln1_g, m_ln1_b, m_w_gate, m_w_up, m_w_down, m_ln2_g, m_ln2_b, v_w_in, v_b_in, v_ssm_lambda_re, v_ssm_lambda_im, v_ssm_log_dt, v_ssm_b_re, v_ssm_b_im, v_ssm_c_re, v_ssm_c_im, v_ssm_d, v_glu_w, v_glu_b, v_w_ssm_out, v_conv_w, v_w_conv_out, v_w_o, v_ln1_g, v_ln1_b, v_w_gate, v_w_up, v_w_down, v_ln2_g, v_ln2_b):
    given = dict(locals())
    w = {n: given[n] for n in WEIGHTS}
    m = {n: given["m_" + n] for n in WEIGHTS}
    v = {n: given["v_" + n] for n in WEIGHTS}

    shards = [w[n][0].astype(BF16) for n in BIG]
    shards.append(jnp.pad(conv_w[0], ((0, SUBLANES - 3), (0, 0))))
    stacks = _gather_weights(shards)
    wts = dict(zip(BIG + ("conv_w",), stacks))

    small = {n: w[n] for n in SMALL}
    loss, dx, g_big, g_packed = _local_step(x[0], loss_target[0], small, wts)

    c_arr = jnp.reshape(lax.axis_index("c"), (1,)).astype(jnp.int32)
    me = _shard_of((lax.axis_index("x"), lax.axis_index("y")))
    me_arr = jnp.reshape(me, (1,)).astype(jnp.int32)
    mine_small = [g_big["conv_w"], g_packed]
    got_big, got_small = _swap_with_sibling([g_big[n] for n in BIG], mine_small)
    chip_big = [_add_own_half(g_big[n], r, c_arr) for n, r in zip(BIG, got_big)]
    chip_conv, chip_packed = _small_pair_sums(mine_small, got_small)
    slots_big, slots_conv, (slots_packed,) = _scatter_to_chips(chip_big, chip_conv, [chip_packed])
    halves = [_sum_chips(own, s, me_arr) for own, s in zip(chip_big, slots_big)]
    others = _send_to_sibling(halves)

    conv_total, packed_total = _small_totals(me_arr, chip_conv, slots_conv, chip_packed, slots_packed)
    grad = _unpack_small(packed_total)
    grad["conv_w"] = conv_total[:3][None]
    small_names = ("conv_w",) + SMALL
    ds, nms, nvs = _adam_small([grad[n] for n in small_names], [w[n] for n in small_names],
                               [m[n] for n in small_names], [v[n] for n in small_names])
    delta, new_m, new_v = {}, {}, {}
    for i, n in enumerate(small_names):
        delta[n], new_m[n], new_v[n] = ds[i], nms[i], nvs[i]
    for n, mine, other in zip(BIG, halves, others):
        shape = w[n].shape
        grad[n], delta[n], new_m[n], new_v[n] = (
            r.reshape(shape) for r in _adam_big(w[n], mine, other, m[n], v[n], c_arr))

    loss_total = lax.psum(loss[0, 0], ("x", "y", "c"))
    return (loss_total, dx[None], *[grad[n] for n in WEIGHTS], *[delta[n] for n in WEIGHTS],
            *[new_m[n] for n in WEIGHTS], *[new_v[n] for n in WEIGHTS])
```

```python
import functools
import math

import jax
import jax.numpy as jnp
from jax import lax
from jax.experimental import pallas as pl
from jax.experimental.pallas import tpu as pltpu

F32 = jnp.float32
BF16 = jnp.bfloat16

D_MODEL = 1024
IN_COLS = 4096
SSM_W = 512
N_GROUPS = 32
N_STATE = 64
GROUP_C = 16
STATE_COLS = N_GROUPS * N_STATE
STRIP = 512
N_STRIPS = STATE_COLS // STRIP
FFN_SHARD = 704
N_SHARDS = 4
ALPHA = 2.0 ** 0.25
LN_EPS = 1e-5
GELU_K = math.sqrt(2.0 / math.pi)
GELU_C = 0.044715

ADAM_LR = 0.001
ADAM_B1 = 0.9
ADAM_B2 = 0.999
ADAM_EPS = 1e-08
ADAM_WD = 0.01
ADAM_STEP = 10

V7X_VMEM_BYTES = 64 * 1024 * 1024
VMEM_LIMIT = V7X_VMEM_BYTES - 8 * 1024 * 1024
SUBLANES = 8

MESH = pl.DeviceIdType.MESH
ANY = pl.BlockSpec(memory_space=pl.ANY)


def _dot(a, b):
    return jnp.dot(a.astype(BF16), b.astype(BF16), preferred_element_type=F32)


def _dot_t(a, b):
    return lax.dot_general(a.astype(BF16), b.astype(BF16), (((1,), (1,)), ((), ())),
                           preferred_element_type=F32)


def _tdot(a, b):
    return lax.dot_general(a.astype(BF16), b.astype(BF16), (((0,), (0,)), ((), ())),
                           preferred_element_type=F32)


def _sigmoid(v):
    return 1.0 / (1.0 + jnp.exp(-v))


def _split3(v):
    hi = v.astype(BF16)
    r1 = v - hi.astype(F32)
    mid = r1.astype(BF16)
    lo = (r1 - mid.astype(F32)).astype(BF16)
    return hi, mid, lo


def _exact_dot(v, sel):
    hi, mid, lo = _split3(v)
    return (jnp.dot(hi, sel, preferred_element_type=F32)
            + jnp.dot(mid, sel, preferred_element_type=F32)
            + jnp.dot(lo, sel, preferred_element_type=F32))


def _const(shape):
    nd = len(shape)
    return pl.BlockSpec(shape, lambda *_: (0,) * nd)


def _params(sem, vmem=VMEM_LIMIT):
    return pltpu.CompilerParams(dimension_semantics=sem, vmem_limit_bytes=vmem)


def _gelu_parts(v):
    inner = GELU_K * (v + GELU_C * v * v * v)
    t = jnp.tanh(inner)
    g = 0.5 * v * (1.0 + t)
    dg = 0.5 * (1.0 + t) + 0.5 * v * (1.0 - t * t) * GELU_K * (1.0 + 3.0 * GELU_C * v * v)
    return g, dg


class _Comm:
    def __init__(self, inputs, out_shape, sems, copies, aliased=False):
        self.inputs, self.out_shape, self.sems = list(inputs), tuple(out_shape), list(sems)
        self.copies, self.aliased = copies, aliased


def _launch(body, comms, *, name, grid, in_specs, out_specs, out_shape, scratch_shapes=(), aliases=None, sem=None):
    comms = list(comms)
    n_in, n_out, n_scr = len(in_specs), len(out_specs), len(scratch_shapes)
    aliases = dict(aliases or {})
    layout = []
    p_in, p_out, p_sem = n_in, n_out, 0
    for cm in comms:
        layout.append((p_in, p_out, p_sem))
        if cm.aliased:
            for i in range(len(cm.inputs)):
                aliases[p_in + i] = p_out + i
        p_in, p_out, p_sem = p_in + len(cm.inputs), p_out + len(cm.out_shape), p_sem + len(cm.sems)
    tot_in, tot_out = p_in, p_out

    def fused(*refs):
        ins, outs = refs[:tot_in], refs[tot_in:tot_in + tot_out]
        scr = refs[tot_in + tot_out:tot_in + tot_out + n_scr]
        sems = refs[tot_in + tot_out + n_scr:]

        def descriptors():
            out = []
            for cm, (a, b, s) in zip(comms, layout):
                out += cm.copies(ins[a:a + len(cm.inputs)], outs[b:b + len(cm.out_shape)], sems[s:s + len(cm.sems)])
            return out

        steps = [pl.program_id(d) for d in range(len(grid))]
        first = functools.reduce(jnp.logical_and, [s == 0 for s in steps]) if grid else None
        last = functools.reduce(jnp.logical_and, [s == g - 1 for s, g in zip(steps, grid)]) if grid else None

        def start():
            for cp in descriptors():
                cp.start()

        def finish():
            for cp in descriptors():
                cp.wait()

        if comms:
            pl.when(first)(start) if grid else start()
        if body is not None:
            body(*ins[:n_in], *outs[:n_out], *scr)
        if comms:
            pl.when(last)(finish) if grid else finish()

    specs_in = list(in_specs) + [ANY] * (tot_in - n_in)
    specs_out = tuple(out_specs) + (ANY,) * (tot_out - n_out)
    shapes = tuple(out_shape) + tuple(s for cm in comms for s in cm.out_shape)
    scratch = list(scratch_shapes) + [s for cm in comms for s in cm.sems]
    if comms or sem is None:
        sem = ("arbitrary",) * len(grid)
    kwargs = dict(grid=grid) if grid else {}
    call = pl.pallas_call(fused, name=name, out_shape=shapes, in_specs=specs_in, out_specs=specs_out,
                          scratch_shapes=scratch, input_output_aliases=aliases,
                          compiler_params=_params(sem) if grid else None, **kwargs)

    def run(*args):
        out = call(*args, *(a for cm in comms for a in cm.inputs))
        results, rest = out[:n_out], out[n_out:]
        per_comm = []
        for cm in comms:
            per_comm.append(rest[:len(cm.out_shape)])
            rest = rest[len(cm.out_shape):]
        return results, per_comm

    return run


def _ssm_discretise(lr, li, ldt, lr16, li16, ldt16, brt, bit):
    def lam_bar(lr_, li_, ldt_):
        dt = jnp.exp(ldt_)
        mag = jnp.exp(lr_ * dt)
        return mag * jnp.cos(li_ * dt), mag * jnp.sin(li_ * dt)

    lb_re, lb_im = lam_bar(lr, li, ldt)
    l16_re, l16_im = lam_bar(lr16, li16, ldt16)
    den = lr16 * lr16 + li16 * li16
    num_re = l16_re - 1.0
    fr = (num_re * lr16 + l16_im * li16) / den
    fi = (l16_im * lr16 - num_re * li16) / den
    bb_re = fr * brt - fi * bit
    bb_im = fr * bit + fi * brt
    return lb_re, lb_im, bb_re, bb_im


def _strip_selectors():
    p = lax.broadcasted_iota(jnp.int32, (N_STATE, STRIP), 0)
    col = lax.broadcasted_iota(jnp.int32, (N_STATE, STRIP), 1)
    rep = ((col & (N_STATE - 1)) == p).astype(BF16)
    row = lax.broadcasted_iota(jnp.int32, (SSM_W, STRIP), 0)
    col2 = lax.broadcasted_iota(jnp.int32, (SSM_W, STRIP), 1)
    mask = (((row >> 4) & 7) == (col2 >> 6))
    return rep, mask


def _ssm_prepare(lr, li, ldt, lr16, li16, ldt16, brt, bit, cre, cim):
    def body(lr_r, li_r, ldt_r, lr16_r, li16_r, ldt16_r, brt_r, bit_r, cre_r, cim_r,
             pwr_r, pwi_r, bbr_r, bbi_r, ctr_r, cti_r):
        lb_re, lb_im, bb_re, bb_im = _ssm_discretise(
            lr_r[...], li_r[...], ldt_r[...], lr16_r[...], li16_r[...], ldt16_r[...], brt_r[...], bit_r[...])
        pr, pi_ = lb_re, lb_im
        pwr_r[0] = pr
        pwi_r[0] = pi_
        for k in range(1, SUBLANES):
            pr, pi_ = pr * lb_re - pi_ * lb_im, pr * lb_im + pi_ * lb_re
            pwr_r[k] = pr
            pwi_r[k] = pi_
        rep, mask = _strip_selectors()
        for src, dst in ((bb_re, bbr_r), (bb_im, bbi_r), (cre_r[...], ctr_r), (cim_r[...], cti_r)):
            wide = jnp.dot(src.astype(BF16), rep, preferred_element_type=F32)
            dst[...] = jnp.where(mask, wide, 0.0).astype(BF16)

    vm = pl.BlockSpec(memory_space=pltpu.VMEM)
    return pl.pallas_call(
        body, name="ssm_prepare",
        out_shape=(jax.ShapeDtypeStruct((SUBLANES, N_GROUPS, N_STATE), F32),) * 2
        + (jax.ShapeDtypeStruct((SSM_W, STRIP), BF16),) * 4,
        in_specs=[vm] * 10, out_specs=(vm,) * 6,
    )(lr, li, ldt, lr16, li16, ldt16, brt, bit, cre, cim)


def _scan_tables(pwr, pwi):
    pr = pwr.reshape(SUBLANES, STATE_COLS)
    pi_ = pwi.reshape(SUBLANES, STATE_COLS)
    row = jnp.arange(SUBLANES)[:, None]
    z = jnp.zeros((SUBLANES, STATE_COLS), F32)

    def lvl(p, k, keep):
        return jnp.where(keep, jnp.broadcast_to(p[k - 1][None], (SUBLANES, STATE_COLS)), z)

    fwd_re = [lvl(pr, 1, row >= 1), lvl(pr, 2, row >= 2), lvl(pr, 4, row >= 4), pr]
    fwd_im = [lvl(pi_, 1, row >= 1), lvl(pi_, 2, row >= 2), lvl(pi_, 4, row >= 4), pi_]
    rev_re = [lvl(pr, 1, row <= 6), lvl(pr, 2, row <= 5), lvl(pr, 4, row <= 3), pr[::-1]]
    rev_im = [-lvl(pi_, 1, row <= 6), -lvl(pi_, 2, row <= 5), -lvl(pi_, 4, row <= 3), -pi_[::-1]]
    return jnp.stack(fwd_re + fwd_im + rev_re + rev_im)


def _ssm_param_grads(lr, li, ldt, lr16, li16, ldt16, brt, bit, dlbr, dlbi, dbbr, dbbi, dctr, dcti):
    def body(lr_r, li_r, ldt_r, lr16_r, li16_r, ldt16_r, brt_r, bit_r,
             dlbr_r, dlbi_r, dbbr_r, dbbi_r, dctr_r, dcti_r,
             glr_r, gli_r, gldt_r, gbrt_r, gbit_r, gcre_r, gcim_r):
        rep, mask = _strip_selectors()

        def fold(acc):
            return sum(lax.dot_general(t, rep, (((1,), (1,)), ((), ())), preferred_element_type=F32)
                       for t in _split3(jnp.where(mask, acc, 0.0)))

        g_lb_re = jnp.sum(dlbr_r[...], axis=0)
        g_lb_im = jnp.sum(dlbi_r[...], axis=0)
        g_bb_re = fold(dbbr_r[...])
        g_bb_im = fold(dbbi_r[...])
        gcre_r[...] = fold(dctr_r[...])
        gcim_r[...] = fold(dcti_r[...])
        prim = (lr_r[...], li_r[...], ldt_r[...], lr16_r[...], li16_r[...], ldt16_r[...], brt_r[...], bit_r[...])
        _, vjp = jax.vjp(_ssm_discretise, *prim)
        g_lr, g_li, g_ldt, g_lr16, g_li16, g_ldt16, g_brt, g_bit = vjp((g_lb_re, g_lb_im, g_bb_re, g_bb_im))
        grp = lax.broadcasted_iota(jnp.int32, (N_GROUPS, SSM_W), 0)
        rw = lax.broadcasted_iota(jnp.int32, (N_GROUPS, SSM_W), 1)
        gsum = ((rw >> 4) == grp).astype(BF16)

        def group_sum(v):
            return sum(jnp.dot(gsum, t, preferred_element_type=F32) for t in _split3(v))

        glr_r[...] = g_lr + group_sum(g_lr16)
        gli_r[...] = g_li + group_sum(g_li16)
        gldt_r[...] = g_ldt + jnp.sum(group_sum(g_ldt16), axis=1, keepdims=True)
        gbrt_r[...] = g_brt
        gbit_r[...] = g_bit

    vm = pl.BlockSpec(memory_space=pltpu.VMEM)
    gp = jax.ShapeDtypeStruct((N_GROUPS, N_STATE), F32)
    gb = jax.ShapeDtypeStruct((SSM_W, N_STATE), F32)
    return pl.pallas_call(
        body, name="ssm_param_grads",
        out_shape=(gp, gp, jax.ShapeDtypeStruct((N_GROUPS, 1), F32), gb, gb, gb, gb),
        in_specs=[vm] * 14, out_specs=(vm,) * 7,
    )(lr, li, ldt, lr16, li16, ldt16, brt, bit, dlbr, dlbi, dbbr, dbbi, dctr, dcti)


def _in_proj(x, w_in_st, b_in, comms=()):
    t = x.shape[0]
    tm = 512

    def body(x_r, w_r, b_r, o_r):
        o_r[...] = _dot(x_r[...], w_r[0]) + b_r[...]

    (proj,), sent = _launch(
        body, comms, name="in_proj", grid=(t // tm, N_SHARDS),
        out_shape=(jax.ShapeDtypeStruct((t, IN_COLS), F32),),
        in_specs=[pl.BlockSpec((tm, D_MODEL), lambda i, j: (i, 0)),
                  pl.BlockSpec((1, D_MODEL, D_MODEL), lambda i, j: (j, 0, 0)),
                  pl.BlockSpec((1, D_MODEL), lambda i, j: (0, j))],
        out_specs=(pl.BlockSpec((tm, D_MODEL), lambda i, j: (i, j)),),
        sem=("parallel", "arbitrary"),
    )(x, w_in_st, b_in)
    return proj, sent


def _cmul_add(xr, xi, mr, mi, sr, si):
    return xr + (mr * sr - mi * si), xi + (mr * si + mi * sr)


def _ssm_forward(proj, bbr, bbi, ctr, cti, d_skip, tab, comms=(), tc=512):
    t = proj.shape[0]
    nb = tc // SUBLANES

    def body(u_r, bbr_r, bbi_r, ctr_r, cti_r, d_r, tab_r, xsr_r, xsi_r, y_r, car_r, car_i):
        @pl.when(pl.program_id(1) == 0)
        def _():
            car_r[...] = jnp.zeros_like(car_r)
            car_i[...] = jnp.zeros_like(car_i)

        u = u_r[...]
        xsr_r[...] = _dot(u, bbr_r[...])
        xsi_r[...] = _dot(u, bbi_r[...])

        def block(b, carry):
            cr, ci = carry
            rows = pl.ds(pl.multiple_of(b * SUBLANES, SUBLANES), SUBLANES)
            xr = xsr_r[rows, :]
            xi = xsi_r[rows, :]
            for lvl, s in ((0, 1), (1, 2), (2, 4)):
                xr, xi = _cmul_add(xr, xi, tab_r[lvl], tab_r[4 + lvl],
                                   pltpu.roll(xr, s, 0), pltpu.roll(xi, s, 0))
            xr, xi = _cmul_add(xr, xi, tab_r[3], tab_r[7],
                               jnp.broadcast_to(cr, xr.shape), jnp.broadcast_to(ci, xi.shape))
            xsr_r[rows, :] = xr
            xsi_r[rows, :] = xi
            return xr[SUBLANES - 1:SUBLANES, :], xi[SUBLANES - 1:SUBLANES, :]

        cr, ci = lax.fori_loop(0, nb, block, (car_r[...], car_i[...]))
        car_r[...] = cr
        car_i[...] = ci
        y_r[...] = _dot_t(xsr_r[...], ctr_r[...]) - _dot_t(xsi_r[...], cti_r[...]) + d_r[...] * u

    strip_mat = pl.BlockSpec((128, STRIP), lambda j, k: (j, 0))
    states = pl.BlockSpec((tc, STRIP), lambda j, k: (k, j))
    return _launch(
        body, comms, name="ssm_forward", grid=(N_STRIPS, t // tc),
        out_shape=(jax.ShapeDtypeStruct((t, STATE_COLS), F32), jax.ShapeDtypeStruct((t, STATE_COLS), F32),
                   jax.ShapeDtypeStruct((t, SSM_W), F32)),
        in_specs=[pl.BlockSpec((tc, 128), lambda j, k: (k, j)),
                  strip_mat, strip_mat, strip_mat, strip_mat,
                  pl.BlockSpec((1, 128), lambda j, k: (0, j)),
                  pl.BlockSpec((16, SUBLANES, STRIP), lambda j, k: (0, 0, j))],
        out_specs=(states, states, pl.BlockSpec((tc, 128), lambda j, k: (k, j))),
        scratch_shapes=[pltpu.VMEM((1, STRIP), F32), pltpu.VMEM((1, STRIP), F32)],
        sem=("parallel", "arbitrary"),
    )(proj, bbr, bbi, ctr, cti, d_skip, tab)


def _shift_down(v, prev, n):
    row = lax.broadcasted_iota(jnp.int32, v.shape, 0)
    out = pltpu.roll(v, n, 0)
    for r in range(n):
        src = prev[SUBLANES - n + r:SUBLANES - n + r + 1, :]
        out = jnp.where(row == r, jnp.broadcast_to(src, v.shape), out)
    return out


def _shift_up(v, nxt, n):
    rows = v.shape[0]
    row = lax.broadcasted_iota(jnp.int32, v.shape, 0)
    out = pltpu.roll(v, rows - n, 0)
    for r in range(n):
        src = nxt[r:r + 1, :]
        out = jnp.where(row == rows - n + r, jnp.broadcast_to(src, v.shape), out)
    return out


def _conv3(q, q_prev, w):
    return w[2:3, :] * q + w[1:2, :] * _shift_down(q, q_prev, 1) + w[0:1, :] * _shift_down(q, q_prev, 2)


def _mixer_forward(x, proj, ya0, glu_w, glu_b, wso_st, conv_w8, wco_st, w_o, comms=(), tm=256):
    t = x.shape[0]
    hb = tm // SUBLANES

    def body(x_r, ya0_r, h_r, cg_r, bg_r, ga_r, gb_r, hp_r, cgp_r,
             glu_w_r, glu_b_r, wso_r, cw_r, wco_r, wo_r, xh_r, rstd_r, ya_r, yb_r):
        i = pl.program_id(0)
        g, _ = _gelu_parts(ya0_r[...])
        ya1 = g * _sigmoid(_dot(g, glu_w_r[...]) + glu_b_r[...])
        q = cg_r[...] * h_r[...]
        q_prev = jnp.where(i > 0, cgp_r[...] * hp_r[...], 0.0)
        yb0 = bg_r[...] * _conv3(q, q_prev, cw_r[...])
        for j in range(N_SHARDS):
            ya_r[:, 256 * j:256 * (j + 1)] = _dot(ya1, wso_r[j])
            yb_r[:, 256 * j:256 * (j + 1)] = _dot(yb0, wco_r[j])
        merged = _sigmoid(ga_r[...]) * ya_r[...] + _sigmoid(gb_r[...]) * yb_r[...]
        r1 = ALPHA * x_r[...] + _dot(merged, wo_r[...])
        mu = jnp.mean(r1, axis=-1, keepdims=True)
        cen = r1 - mu
        rstd = lax.rsqrt(jnp.mean(cen * cen, axis=-1, keepdims=True) + LN_EPS)
        xh_r[...] = cen * rstd
        rstd_r[...] = rstd

    def col(w, c):
        return pl.BlockSpec((tm, w), lambda i: (i, c))

    def prev(c):
        return pl.BlockSpec((SUBLANES, SSM_W), lambda i: (jnp.maximum(i * hb - 1, 0), c))

    return _launch(
        body, comms, name="mixer_forward", grid=(t // tm,),
        out_shape=(jax.ShapeDtypeStruct((t, D_MODEL), F32), jax.ShapeDtypeStruct((t, 1), F32),
                   jax.ShapeDtypeStruct((t, D_MODEL), F32), jax.ShapeDtypeStruct((t, D_MODEL), F32)),
        in_specs=[col(D_MODEL, 0), col(SSM_W, 0), col(SSM_W, 1), col(SSM_W, 2), col(SSM_W, 3),
                  col(D_MODEL, 2), col(D_MODEL, 3), prev(1), prev(2),
                  _const((SSM_W, SSM_W)), _const((1, SSM_W)), _const((N_SHARDS, SSM_W, 256)),
                  _const((SUBLANES, SSM_W)), _const((N_SHARDS, SSM_W, 256)), _const((D_MODEL, D_MODEL))],
        out_specs=(col(D_MODEL, 0), pl.BlockSpec((tm, 1), lambda i: (i, 0)), col(D_MODEL, 0), col(D_MODEL, 0)),
        sem=("parallel",),
    )(x, ya0, proj, proj, proj, proj, proj, proj, proj, glu_w, glu_b, wso_st, conv_w8, wco_st, w_o)


def _layer_norm_bwd(dxhat, xhat, rstd):
    m1 = jnp.mean(dxhat, axis=-1, keepdims=True)
    m2 = jnp.mean(dxhat * xhat, axis=-1, keepdims=True)
    return rstd * (dxhat - m1 - xhat * m2)


def _ffn_step(xhat1, rstd1, target, ln1_g, ln1_b, ln2_g, ln2_b, wg_st, wu_st, wd_st, tm=256):
    t = xhat1.shape[0]

    def body(xh_r, rstd_r, tgt_r, g1_r, b1_r, g2_r, b2_r, wg_r, wu_r, wd_r,
             loss_r, dr1_r, x1b_r, dr2b_r, hid_r, dhg_r, dhu_r, dg2_r, db2_r, dg1_r, db1_r,
             hg_s, hu_s):
        @pl.when(pl.program_id(0) == 0)
        def _():
            for r in (loss_r, dg2_r, db2_r, dg1_r, db1_r):
                r[...] = jnp.zeros_like(r)

        xhat1_v = xh_r[...]
        x1 = xhat1_v * g1_r[...] + b1_r[...]
        x1b = x1.astype(BF16)
        x1b_r[...] = x1b
        ffn = jnp.zeros((tm, D_MODEL), F32)
        for j in range(N_SHARDS):
            hg = jnp.dot(x1b, wg_r[j], preferred_element_type=F32)
            hu = jnp.dot(x1b, wu_r[j], preferred_element_type=F32)
            hg_s[j] = hg
            hu_s[j] = hu
            hid = (hg * _sigmoid(hg) * hu).astype(BF16)
            hid_r[j] = hid
            ffn = ffn + jnp.dot(hid, wd_r[j], preferred_element_type=F32)
        r2 = ALPHA * x1 + ffn
        mu = jnp.mean(r2, axis=-1, keepdims=True)
        cen = r2 - mu
        rstd2 = lax.rsqrt(jnp.mean(cen * cen, axis=-1, keepdims=True) + LN_EPS)
        xhat2 = cen * rstd2
        diff = (xhat2 * g2_r[...] + b2_r[...]) - tgt_r[...]
        loss_r[...] += 0.5 * jnp.sum(jnp.mean(diff * diff, axis=-1, keepdims=True), axis=0, keepdims=True)
        dy = diff * (1.0 / D_MODEL)
        dg2_r[...] += jnp.sum(dy * xhat2, axis=0, keepdims=True)
        db2_r[...] += jnp.sum(dy, axis=0, keepdims=True)
        dr2 = _layer_norm_bwd(dy * g2_r[...], xhat2, rstd2)
        dr2b = dr2.astype(BF16)
        dr2b_r[...] = dr2b
        dx1 = ALPHA * dr2
        for j in range(N_SHARDS):
            dhid = lax.dot_general(dr2b, wd_r[j], (((1,), (1,)), ((), ())), preferred_element_type=F32)
            hg = hg_s[j]
            hu = hu_s[j]
            sg = _sigmoid(hg)
            dhu = (dhid * (hg * sg)).astype(BF16)
            dhg = (dhid * hu * (sg * (1.0 + hg * (1.0 - sg)))).astype(BF16)
            dhg_r[j] = dhg
            dhu_r[j] = dhu
            dx1 = dx1 + lax.dot_general(dhg, wg_r[j], (((1,), (1,)), ((), ())), preferred_element_type=F32)
            dx1 = dx1 + lax.dot_general(dhu, wu_r[j], (((1,), (1,)), ((), ())), preferred_element_type=F32)
        dg1_r[...] += jnp.sum(dx1 * xhat1_v, axis=0, keepdims=True)
        db1_r[...] += jnp.sum(dx1, axis=0, keepdims=True)
        dr1_r[...] = _layer_norm_bwd(dx1 * g1_r[...], xhat1_v, rstd_r[...])

    tile = pl.BlockSpec((tm, D_MODEL), lambda i: (i, 0))
    hidden = pl.BlockSpec((N_SHARDS, tm, FFN_SHARD), lambda i: (0, i, 0))
    vec = _const((1, D_MODEL))
    hid_shape = jax.ShapeDtypeStruct((N_SHARDS, t, FFN_SHARD), BF16)
    vec_shape = jax.ShapeDtypeStruct((1, D_MODEL), F32)
    return pl.pallas_call(
        body, name="ffn_step", grid=(t // tm,),
        out_shape=(jax.ShapeDtypeStruct((1, 1), F32), jax.ShapeDtypeStruct((t, D_MODEL), F32),
                   jax.ShapeDtypeStruct((t, D_MODEL), BF16), jax.ShapeDtypeStruct((t, D_MODEL), BF16),
                   hid_shape, hid_shape, hid_shape, vec_shape, vec_shape, vec_shape, vec_shape),
        in_specs=[tile, pl.BlockSpec((tm, 1), lambda i: (i, 0)), tile, vec, vec, vec, vec,
                  _const((N_SHARDS, D_MODEL, FFN_SHARD)), _const((N_SHARDS, D_MODEL, FFN_SHARD)),
                  _const((N_SHARDS, FFN_SHARD, D_MODEL))],
        out_specs=(_const((1, 1)), tile, tile, tile, hidden, hidden, hidden, vec, vec, vec, vec),
        scratch_shapes=[pltpu.VMEM((N_SHARDS, tm, FFN_SHARD), F32), pltpu.VMEM((N_SHARDS, tm, FFN_SHARD), F32)],
        compiler_params=_params(("arbitrary",)),
    )(xhat1, rstd1, target, ln1_g, ln1_b, ln2_g, ln2_b, wg_st, wu_st, wd_st)


def _ffn_weight_grads(x1b, dr2b, hid, dhg, dhu, tk=512):
    t = x1b.shape[0]

    def body(x_r, dr_r, hid_r, dhg_r, dhu_r, gwg_r, gwu_r, gwd_r):
        @pl.when(pl.program_id(1) == 0)
        def _():
            for r in (gwg_r, gwu_r, gwd_r):
                r[...] = jnp.zeros_like(r)

        gwg_r[0] += _tdot(x_r[...], dhg_r[0])
        gwu_r[0] += _tdot(x_r[...], dhu_r[0])
        gwd_r[0] += _tdot(hid_r[0], dr_r[...])

    tile = pl.BlockSpec((tk, D_MODEL), lambda j, k: (k, 0))
    hidden = pl.BlockSpec((1, tk, FFN_SHARD), lambda j, k: (j, k, 0))
    col = pl.BlockSpec((1, D_MODEL, FFN_SHARD), lambda j, k: (j, 0, 0))
    row = pl.BlockSpec((1, FFN_SHARD, D_MODEL), lambda j, k: (j, 0, 0))
    return pl.pallas_call(
        body, name="ffn_weight_grads", grid=(N_SHARDS, t // tk),
        out_shape=(jax.ShapeDtypeStruct((N_SHARDS, D_MODEL, FFN_SHARD), F32),) * 2
        + (jax.ShapeDtypeStruct((N_SHARDS, FFN_SHARD, D_MODEL), F32),),
        in_specs=[tile, tile, hidden, hidden, hidden],
        out_specs=(col, col, row),
        compiler_params=_params(("parallel", "arbitrary")),
    )(x1b, dr2b, hid, dhg, dhu)


def _mixer_backward(dr1, proj, ya0, ya, yb, glu_w, glu_b, wso_st, conv_w8, wco_st, w_o, comms=(), tm=256):
    t = dr1.shape[0]
    hb = tm // SUBLANES
    last_block = t // SUBLANES - 1

    def body(dr1_r, dr1n_r, ya0_r, ya_r, yb_r, h_r, cg_r, bg_r, ga_r, gb_r, hp_r, cgp_r, bgn_r, gbn_r,
             glu_w_r, glu_b_r, wso_r, cw_r, wco_r, wo_r,
             dya0_r, dproj_r, dbias_r, gwo_r, gwso_r, gwco_r, gglu_w_r, gglu_b_r, gconv_r):
        i = pl.program_id(0)

        @pl.when(i == 0)
        def _():
            for r in (dbias_r, gwo_r, gwso_r, gwco_r, gglu_w_r, gglu_b_r, gconv_r):
                r[...] = jnp.zeros_like(r)

        dr1_v = dr1_r[...]
        dmerged = _dot_t(dr1_v, wo_r[...])
        sa = _sigmoid(ga_r[...])
        sb = _sigmoid(gb_r[...])
        ya_v = ya_r[...]
        yb_v = yb_r[...]
        gwo_r[...] += _tdot(sa * ya_v + sb * yb_v, dr1_v)
        dya = dmerged * sa
        dyb = dmerged * sb
        dga = dmerged * ya_v * (sa * (1.0 - sa))
        dgb = dmerged * yb_v * (sb * (1.0 - sb))

        g, gelu_grad = _gelu_parts(ya0_r[...])
        s1 = _sigmoid(_dot(g, glu_w_r[...]) + glu_b_r[...])
        ya1 = g * s1
        dya1 = jnp.zeros((tm, SSM_W), F32)
        for j in range(N_SHARDS):
            dya_j = dya[:, 256 * j:256 * (j + 1)]
            gwso_r[j] += _tdot(ya1, dya_j)
            dya1 = dya1 + _dot_t(dya_j, wso_r[j])
        dz1 = dya1 * g * (s1 * (1.0 - s1))
        gglu_b_r[...] += jnp.sum(dz1, axis=0, keepdims=True)
        gglu_w_r[...] += _tdot(g, dz1)
        dya0_r[...] = (dya1 * s1 + _dot_t(dz1, glu_w_r[...])) * gelu_grad

        cw = cw_r[...]
        h = h_r[...]
        cg = cg_r[...]
        bg = bg_r[...]
        q = cg * h
        q_prev = jnp.where(i > 0, cgp_r[...] * hp_r[...], 0.0)
        q1 = _shift_down(q, q_prev, 1)
        q2 = _shift_down(q, q_prev, 2)
        z = cw[2:3, :] * q + cw[1:2, :] * q1 + cw[0:1, :] * q2
        yb0 = bg * z
        dyb0 = jnp.zeros((tm, SSM_W), F32)
        for j in range(N_SHARDS):
            dyb_j = dyb[:, 256 * j:256 * (j + 1)]
            gwco_r[j] += _tdot(yb0, dyb_j)
            dyb0 = dyb0 + _dot_t(dyb_j, wco_r[j])
        dbg = dyb0 * z
        dz = dyb0 * bg
        dyb_n = _dot_t(dr1n_r[...], wo_r[...]) * _sigmoid(gbn_r[...])
        dyb0_n = jnp.zeros((SUBLANES, SSM_W), F32)
        for j in range(N_SHARDS):
            dyb0_n = dyb0_n + _dot_t(dyb_n[:, 256 * j:256 * (j + 1)], wco_r[j])
        dz_next = jnp.where(i < pl.num_programs(0) - 1, dyb0_n * bgn_r[...], 0.0)
        dq = cw[2:3, :] * dz + cw[1:2, :] * _shift_up(dz, dz_next, 1) + cw[0:1, :] * _shift_up(dz, dz_next, 2)
        gconv_r[0:1, :] += jnp.sum(dz * q2, axis=0, keepdims=True)
        gconv_r[1:2, :] += jnp.sum(dz * q1, axis=0, keepdims=True)
        gconv_r[2:3, :] += jnp.sum(dz * q, axis=0, keepdims=True)
        dh = dq * cg
        dcg = dq * h

        dproj_r[:, 0:512] = jnp.zeros((tm, SSM_W), BF16)
        pieces = ((512, dh), (1024, dcg), (1536, dbg), (2048, dga), (3072, dgb))
        for off, val in pieces:
            w = val.shape[1]
            dproj_r[:, off:off + w] = val.astype(BF16)
            dbias_r[:, off:off + w] += jnp.sum(val, axis=0, keepdims=True)

    def col(w, c):
        return pl.BlockSpec((tm, w), lambda i: (i, c))

    def prev(c):
        return pl.BlockSpec((SUBLANES, SSM_W), lambda i: (jnp.maximum(i * hb - 1, 0), c))

    def nxt(w, c):
        return pl.BlockSpec((SUBLANES, w), lambda i: (jnp.minimum((i + 1) * hb, last_block), c))

    sh = jax.ShapeDtypeStruct
    return _launch(
        body, comms, name="mixer_backward", grid=(t // tm,),
        out_shape=(sh((t, SSM_W), F32), sh((t, IN_COLS), BF16), sh((1, IN_COLS), F32),
                   sh((D_MODEL, D_MODEL), F32), sh((N_SHARDS, SSM_W, 256), F32), sh((N_SHARDS, SSM_W, 256), F32),
                   sh((SSM_W, SSM_W), F32), sh((1, SSM_W), F32), sh((SUBLANES, SSM_W), F32)),
        in_specs=[col(D_MODEL, 0), nxt(D_MODEL, 0), col(SSM_W, 0), col(D_MODEL, 0), col(D_MODEL, 0),
                  col(SSM_W, 1), col(SSM_W, 2), col(SSM_W, 3), col(D_MODEL, 2), col(D_MODEL, 3),
                  prev(1), prev(2), nxt(SSM_W, 3), nxt(D_MODEL, 3),
                  _const((SSM_W, SSM_W)), _const((1, SSM_W)), _const((N_SHARDS, SSM_W, 256)),
                  _const((SUBLANES, SSM_W)), _const((N_SHARDS, SSM_W, 256)), _const((D_MODEL, D_MODEL))],
        out_specs=(col(SSM_W, 0), col(IN_COLS, 0), _const((1, IN_COLS)),
                   _const((D_MODEL, D_MODEL)), _const((N_SHARDS, SSM_W, 256)), _const((N_SHARDS, SSM_W, 256)),
                   _const((SSM_W, SSM_W)), _const((1, SSM_W)), _const((SUBLANES, SSM_W))),
        sem=("arbitrary",),
    )(dr1, dr1, ya0, ya, yb, proj, proj, proj, proj, proj, proj, proj, proj, proj,
      glu_w, glu_b, wso_st, conv_w8, wco_st, w_o)


def _ssm_backward(dya0, proj, xsr, xsi, bbr, bbi, ctr, cti, d_skip, tab, dproj, comms=(), tc=512):
    t = proj.shape[0]
    nb = tc // SUBLANES
    nk = t // tc

    def body(dy_r, u_r, xsr_r, xsi_r, bbr_r, bbi_r, ctr_r, cti_r, d_r, tab_r, dproj_any,
             du_r, dus_r, gbbr_r, gbbi_r, gctr_r, gcti_r, glbr_r, glbi_r, gd_r,
             gr_s, gi_s, car_r, car_i):
        del dproj_any

        @pl.when(pl.program_id(1) == 0)
        def _():
            for r in (car_r, car_i, dus_r, gbbr_r, gbbi_r, gctr_r, gcti_r, glbr_r, glbi_r, gd_r):
                r[...] = jnp.zeros_like(r)

        dy = dy_r[...]
        u = u_r[...]
        gr_s[...] = _dot(dy, ctr_r[...])
        gi_s[...] = -_dot(dy, cti_r[...])
        last_row = lax.broadcasted_iota(jnp.int32, (SUBLANES, STRIP), 0) == SUBLANES - 1

        def block(n, carry):
            cr, ci, ar, ai = carry
            rows = pl.ds(pl.multiple_of((nb - 1 - n) * SUBLANES, SUBLANES), SUBLANES)
            gr = gr_s[rows, :]
            gi = gi_s[rows, :]
            for lvl, s in ((0, 1), (1, 2), (2, 4)):
                gr, gi = _cmul_add(gr, gi, tab_r[8 + lvl], tab_r[12 + lvl],
                                   pltpu.roll(gr, SUBLANES - s, 0), pltpu.roll(gi, SUBLANES - s, 0))
            crb = jnp.broadcast_to(cr, gr.shape)
            cib = jnp.broadcast_to(ci, gi.shape)
            gr, gi = _cmul_add(gr, gi, tab_r[11], tab_r[15], crb, cib)
            gr_s[rows, :] = gr
            gi_s[rows, :] = gi
            gnr = jnp.where(last_row, crb, pltpu.roll(gr, SUBLANES - 1, 0))
            gni = jnp.where(last_row, cib, pltpu.roll(gi, SUBLANES - 1, 0))
            xr = xsr_r[rows, :]
            xi = xsi_r[rows, :]
            ar = ar + (xr * gnr + xi * gni)
            ai = ai + (xr * gni - xi * gnr)
            return gr[0:1, :], gi[0:1, :], ar, ai

        zero = jnp.zeros((SUBLANES, STRIP), F32)
        cr, ci, ar, ai = lax.fori_loop(0, nb, block, (car_r[...], car_i[...], zero, zero))
        car_r[...] = cr
        car_i[...] = ci
        glbr_r[...] += ar
        glbi_r[...] += ai
        gr = gr_s[...]
        gi = gi_s[...]
        du = _dot_t(gr, bbr_r[...]) + _dot_t(gi, bbi_r[...]) + d_r[...] * dy
        du_r[...] = du.astype(BF16)
        dus_r[...] += jnp.sum(du, axis=0, keepdims=True)
        gd_r[...] += jnp.sum(dy * u, axis=0, keepdims=True)
        gbbr_r[...] += _tdot(u, gr)
        gbbi_r[...] += _tdot(u, gi)
        gctr_r[...] += _tdot(dy, xsr_r[...])
        gcti_r[...] -= _tdot(dy, xsi_r[...])

    def rev(w):
        return pl.BlockSpec((tc, w), lambda j, k: (nk - 1 - k, j))

    strip_mat = pl.BlockSpec((128, STRIP), lambda j, k: (j, 0))
    vec = pl.BlockSpec((1, 128), lambda j, k: (0, j))
    lbacc = pl.BlockSpec((SUBLANES, STRIP), lambda j, k: (0, j))
    sh = jax.ShapeDtypeStruct
    return _launch(
        body, comms, name="ssm_backward", grid=(N_STRIPS, nk),
        out_shape=(sh((t, IN_COLS), BF16), sh((1, SSM_W), F32),
                   sh((SSM_W, STRIP), F32), sh((SSM_W, STRIP), F32), sh((SSM_W, STRIP), F32), sh((SSM_W, STRIP), F32),
                   sh((SUBLANES, STATE_COLS), F32), sh((SUBLANES, STATE_COLS), F32), sh((1, SSM_W), F32)),
        in_specs=[rev(128), rev(128), rev(STRIP), rev(STRIP),
                  strip_mat, strip_mat, strip_mat, strip_mat, vec,
                  pl.BlockSpec((16, SUBLANES, STRIP), lambda j, k: (0, 0, j)), ANY],
        out_specs=(rev(128), vec, strip_mat, strip_mat, strip_mat, strip_mat, lbacc, lbacc, vec),
        scratch_shapes=[pltpu.VMEM((tc, STRIP), F32), pltpu.VMEM((tc, STRIP), F32),
                        pltpu.VMEM((1, STRIP), F32), pltpu.VMEM((1, STRIP), F32)],
        aliases={10: 0}, sem=("parallel", "arbitrary"),
    )(dya0, proj, xsr, xsi, bbr, bbi, ctr, cti, d_skip, tab, dproj)


def _input_grad(dr1, dproj, w_in_st, comms=(), tm=256):
    t = dr1.shape[0]

    def body(dr1_r, dp_r, w_r, dx_r):
        acc = ALPHA * dr1_r[...]
        for j in range(N_SHARDS):
            acc = acc + lax.dot_general(dp_r[:, D_MODEL * j:D_MODEL * (j + 1)], w_r[j],
                                        (((1,), (1,)), ((), ())), preferred_element_type=F32)
        dx_r[...] = acc

    (dx,), sent = _launch(
        body, comms, name="input_grad", grid=(t // tm,),
        out_shape=(jax.ShapeDtypeStruct((t, D_MODEL), F32),),
        in_specs=[pl.BlockSpec((tm, D_MODEL), lambda i: (i, 0)), pl.BlockSpec((tm, IN_COLS), lambda i: (i, 0)),
                  _const((N_SHARDS, D_MODEL, D_MODEL))],
        out_specs=(pl.BlockSpec((tm, D_MODEL), lambda i: (i, 0)),),
        sem=("parallel",),
    )(dr1, dproj, w_in_st)
    return dx, sent


def _in_weight_grad(x, dproj, tk=512):
    t = x.shape[0]

    def body(x_r, dp_r, gw_r):
        @pl.when(pl.program_id(1) == 0)
        def _():
            gw_r[...] = jnp.zeros_like(gw_r)

        gw_r[0] += _tdot(x_r[...], dp_r[...])

    return pl.pallas_call(
        body, name="in_weight_grad", grid=(N_SHARDS, t // tk),
        out_shape=jax.ShapeDtypeStruct((N_SHARDS, D_MODEL, D_MODEL), F32),
        in_specs=[pl.BlockSpec((tk, D_MODEL), lambda j, k: (k, 0)), pl.BlockSpec((tk, D_MODEL), lambda j, k: (k, j))],
        out_specs=pl.BlockSpec((1, D_MODEL, D_MODEL), lambda j, k: (j, 0, 0)),
        compiler_params=_params(("parallel", "arbitrary")),
    )(x, dproj)


MIXER_W = ("glu_w", "w_ssm_out", "w_conv_out", "w_o")
FFN_W = ("w_gate", "w_up", "w_down")


def _device_step(x, target, small, shards, c_arr, me_arr):
    lr, li = small["ssm_lambda_re"][0], small["ssm_lambda_im"][0]
    ldt = small["ssm_log_dt"][0][:, None]
    rep16 = lambda a: jnp.broadcast_to(a[:, None, :], (N_GROUPS, GROUP_C, a.shape[-1])).reshape(SSM_W, a.shape[-1])
    lr16, li16 = rep16(lr), rep16(li)
    ldt16 = rep16(jnp.broadcast_to(ldt, (N_GROUPS, N_STATE)))
    brt = small["ssm_b_re"][0].transpose(0, 2, 1).reshape(SSM_W, N_STATE)
    bit = small["ssm_b_im"][0].transpose(0, 2, 1).reshape(SSM_W, N_STATE)
    cre = small["ssm_c_re"][0].reshape(SSM_W, N_STATE)
    cim = small["ssm_c_im"][0].reshape(SSM_W, N_STATE)
    disc = (lr, li, ldt, lr16, li16, ldt16, brt, bit)

    pwr, pwi, bbr, bbi, ctr, cti = _ssm_prepare(*disc, cre, cim)
    tab = _scan_tables(pwr, pwi)

    mixer_sh = [shards[n] for n in MIXER_W]
    ffn_sh = [shards[n] for n in FFN_W]
    (w_in_st,) = _gather_weights([shards["w_in"]])
    proj, (arrived,) = _in_proj(x, w_in_st, small["b_in"], comms=[_gather_ici(mixer_sh, [shards["conv_w"]])])
    (xsr, xsi, ya0), (ffn_part, mixer_st) = _ssm_forward(
        proj, bbr, bbi, ctr, cti, small["ssm_d"], tab,
        comms=[_gather_ici(ffn_sh), _gather_d2d(arrived[:len(MIXER_W)], mixer_sh)])
    glu_st, wso_st, wco_st, wo_st = (_own_slot(st, sh) for st, sh in zip(mixer_st, mixer_sh))
    conv_st = _own_slot(arrived[len(MIXER_W)], shards["conv_w"])
    conv_w8 = jnp.pad(conv_st[:, :3, :].transpose(1, 0, 2).reshape(3, SSM_W), ((0, SUBLANES - 3), (0, 0)))
    w_o = wo_st.reshape(D_MODEL, D_MODEL)
    glu_w = glu_st.reshape(SSM_W, SSM_W)
    (xhat1, rstd1, ya, yb), (ffn_st,) = _mixer_forward(
        x, proj, ya0, glu_w, small["glu_b"], wso_st, conv_w8, wco_st, w_o, comms=[_gather_d2d(ffn_part, ffn_sh)])
    wg_st, wu_st, wd_st = (_own_slot(st, sh) for st, sh in zip(ffn_st, ffn_sh))
    (loss, dr1, x1b, dr2b, hid, dhg, dhu, g_ln2_g, g_ln2_b, g_ln1_g, g_ln1_b) = _ffn_step(
        xhat1, rstd1, target, small["ln1_g"], small["ln1_b"], small["ln2_g"], small["ln2_b"], wg_st, wu_st, wd_st)

    g_ffn = _ffn_weight_grads(x1b, dr2b, hid, dhg, dhu)
    (dya0, dproj, dbias, g_wo, g_wso, g_wco, g_glu_w, g_glu_b, g_conv8), (got_ffn,) = _mixer_backward(
        dr1, proj, ya0, ya, yb, glu_w, small["glu_b"], wso_st, conv_w8, wco_st, w_o, comms=[_swap_comm(g_ffn)])
    chip_ffn = [_add_own_half(g, r, c_arr) for g, r in zip(g_ffn, got_ffn)]
    (dproj, dus, gbbr, gbbi, gctr, gcti, glbr, glbi, g_d), (slots_ffn,) = _ssm_backward(
        dya0, proj, xsr, xsi, bbr, bbi, ctr, cti, small["ssm_d"], tab, dproj, comms=[_scatter_comm(chip_ffn)])
    halves_ffn = [_sum_chips(own, s, me_arr) for own, s in zip(chip_ffn, slots_ffn)]
    g_lr, g_li, g_ldt, g_brt, g_bit, g_cre, g_cim = _ssm_param_grads(
        *disc, glbr.reshape(SUBLANES, N_GROUPS, N_STATE), glbi.reshape(SUBLANES, N_GROUPS, N_STATE),
        gbbr, gbbi, gctr, gcti)
    g_w_in = _in_weight_grad(x, dproj)

    rest_names = ("w_in",) + MIXER_W
    g_rest = [g_w_in, g_glu_w.reshape(N_SHARDS, 128, SSM_W), g_wso, g_wco, g_wo.reshape(N_SHARDS, 256, D_MODEL)]
    g_conv = jnp.pad(g_conv8[:3].reshape(3, N_SHARDS, 128).transpose(1, 0, 2), ((0, 0), (0, SUBLANES - 3), (0, 0)))
    pieces = [dus, dbias[:, SSM_W:], g_lr, g_li, g_ldt, g_brt, g_bit, g_cre, g_cim, g_d, g_glu_b,
              g_ln1_g, g_ln1_b, g_ln2_g, g_ln2_b, loss]
    flat = jnp.concatenate([p.reshape(-1) for p in pieces])
    g_packed = jnp.pad(flat, (0, PACKED_ROWS * 128 - flat.shape[0])).reshape(PACKED_ROWS, 128)
    dx, (others_ffn, got_rest) = _input_grad(
        dr1, dproj, w_in_st, comms=[_send_comm(halves_ffn), _swap_comm(g_rest, [g_conv, g_packed])])

    chip_rest = [_add_own_half(g, r, c_arr) for g, r in zip(g_rest, got_rest[:len(g_rest)])]
    chip_conv, chip_packed = _small_pair_sums([g_conv, g_packed], got_rest[len(g_rest):])
    (slots,) = _standalone([_scatter_comm(chip_rest + [chip_conv], [chip_packed])], "scatter_to_chips")
    halves_rest = [_sum_chips(own, s, me_arr) for own, s in zip(chip_rest, slots[:len(g_rest)])]
    conv_total, packed_total = _small_totals(me_arr, chip_conv, slots[len(g_rest)], chip_packed, slots[len(g_rest) + 1])
    (others_rest,) = _standalone([_send_comm(halves_rest)], "send_to_sibling")

    pairs = dict(zip(FFN_W, zip(halves_ffn, others_ffn)))
    pairs.update(zip(rest_names, zip(halves_rest, others_rest)))
    return dx, pairs, conv_total, packed_total


PACKED_ROWS = 1136
PACKED_LAYOUT = (("b_in", IN_COLS), ("ssm_lambda_re", STATE_COLS), ("ssm_lambda_im", STATE_COLS),
                 ("ssm_log_dt", N_GROUPS), ("ssm_b_re", SSM_W * N_STATE), ("ssm_b_im", SSM_W * N_STATE),
                 ("ssm_c_re", SSM_W * N_STATE), ("ssm_c_im", SSM_W * N_STATE), ("ssm_d", SSM_W), ("glu_b", SSM_W),
                 ("ln1_g", D_MODEL), ("ln1_b", D_MODEL), ("ln2_g", D_MODEL), ("ln2_b", D_MODEL), ("loss", 1))


def _unpack_small(packed):
    flat = packed.reshape(-1)
    out, off = {}, 0
    for name, size in PACKED_LAYOUT:
        out[name] = flat[off:off + size]
        off += size
    for name in ("ssm_b_re", "ssm_b_im"):
        out[name] = out[name].reshape(N_GROUPS, GROUP_C, N_STATE).transpose(0, 2, 1)[None]
    for name in ("ssm_c_re", "ssm_c_im"):
        out[name] = out[name].reshape(1, N_GROUPS, GROUP_C, N_STATE)
    for name in ("ssm_lambda_re", "ssm_lambda_im"):
        out[name] = out[name].reshape(1, N_GROUPS, N_STATE)
    for name in ("b_in", "ssm_log_dt", "ssm_d", "glu_b", "ln1_g", "ln1_b", "ln2_g", "ln2_b"):
        out[name] = out[name][None]
    return out


BIG = ("w_in", "glu_w", "w_ssm_out", "w_conv_out", "w_o", "w_gate", "w_up", "w_down")
SMALL = ("b_in", "ssm_lambda_re", "ssm_lambda_im", "ssm_log_dt", "ssm_b_re", "ssm_b_im", "ssm_c_re", "ssm_c_im",
         "ssm_d", "glu_b", "ln1_g", "ln1_b", "ln2_g", "ln2_b")
WEIGHTS = ("w_in", "b_in", "ssm_lambda_re", "ssm_lambda_im", "ssm_log_dt", "ssm_b_re", "ssm_b_im", "ssm_c_re",
           "ssm_c_im", "ssm_d", "glu_w", "glu_b", "w_ssm_out", "conv_w", "w_conv_out", "w_o", "ln1_g", "ln1_b",
           "w_gate", "w_up", "w_down", "ln2_g", "ln2_b")


def _place():
    x, y, c = lax.axis_index("x"), lax.axis_index("y"), lax.axis_index("c")
    chips = [(1 - x, y), (x, 1 - y), (1 - x, 1 - y)]
    return x, y, c, chips


def _shard_of(chip):
    return 2 * chip[0] + chip[1]


def _remote(src, dst, send_sem, recv_sem, to):
    return pltpu.make_async_remote_copy(src_ref=src, dst_ref=dst, send_sem=send_sem, recv_sem=recv_sem,
                                        device_id=to, device_id_type=MESH)


def _half_rows(shard, which):
    r2 = shard.shape[0] // 2
    return pl.ds(pl.multiple_of(which * r2, 16), r2)


def _own_slot(stack, shard):
    me = _shard_of((lax.axis_index("x"), lax.axis_index("y")))
    return lax.dynamic_update_slice(stack, shard[None], (me,) + (0,) * shard.ndim)


def _gather_ici(halved, whole=()):
    shards = list(halved) + list(whole)
    nh = len(halved)

    def copies(src, dst, sems):
        send_sem, recv_sem = sems
        x, y, c, chips = _place()
        me = _shard_of((x, y))
        out = []
        for a in range(len(shards)):
            for k, chip in enumerate(chips):
                if a < nh:
                    rows = _half_rows(shards[a], c)
                    out.append(_remote(src[a].at[rows], dst[a].at[me, rows], send_sem.at[a, k], recv_sem.at[a, k],
                                       (*chip, c)))
                else:
                    out.append(_remote(src[a], dst[a].at[me], send_sem.at[a, k], recv_sem.at[a, k], (*chip, c)))
        return out

    n = len(shards)
    return _Comm(shards, [jax.ShapeDtypeStruct((N_SHARDS,) + s.shape, s.dtype) for s in shards],
                 [pltpu.SemaphoreType.DMA((n, 3))] * 2, copies)


def _gather_d2d(stacks, shards):
    def copies(src, dst, sems):
        del src
        send_sem, recv_sem = sems
        x, y, c, chips = _place()
        out = []
        for a in range(len(stacks)):
            for k, chip in enumerate(chips):
                rows = dst[a].at[_shard_of(chip), _half_rows(shards[a], c)]
                out.append(_remote(rows, rows, send_sem.at[a, k], recv_sem.at[a, k], (x, y, 1 - c)))
        return out

    n = len(stacks)
    return _Comm(stacks, [jax.ShapeDtypeStruct(s.shape, s.dtype) for s in stacks],
                 [pltpu.SemaphoreType.DMA((n, 3))] * 2, copies, aliased=True)


def _standalone(comms, name):
    return _launch(None, comms, name=name, grid=(), in_specs=[], out_specs=(), out_shape=())()[1]


def _gather_weights(shards):
    n = len(shards)

    def body(*refs):
        src, dst = refs[:n], refs[n:2 * n]
        send_sem, recv_sem, fsend_sem, frecv_sem = refs[2 * n:]
        x, y, c, chips = _place()
        me = _shard_of((x, y))
        sibling = (x, y, 1 - c)
        sends = []
        for a in range(n):
            mine = _half_rows(shards[a], c)
            for k, chip in enumerate(chips):
                cp = _remote(src[a].at[mine], dst[a].at[me, mine], send_sem.at[a, k], recv_sem.at[a, k], (*chip, c))
                cp.start()
                sends.append(cp)
        for a in range(n):
            for k, chip in enumerate(chips):
                rows = dst[a].at[_shard_of(chip), _half_rows(shards[a], c)]
                _remote(rows, rows, send_sem.at[a, k], recv_sem.at[a, k], sibling).wait_recv()
                cp = _remote(rows, rows, fsend_sem.at[a, k], frecv_sem.at[a, k], sibling)
                cp.start()
                sends.append(cp)
        for a in range(n):
            for k, chip in enumerate(chips):
                rows = dst[a].at[_shard_of(chip), _half_rows(shards[a], 1 - c)]
                _remote(rows, rows, fsend_sem.at[a, k], frecv_sem.at[a, k], sibling).wait_recv()
        for cp in sends:
            cp.wait_send()

    stacks = pl.pallas_call(
        body, name="gather_weights",
        out_shape=tuple(jax.ShapeDtypeStruct((N_SHARDS,) + s.shape, s.dtype) for s in shards),
        in_specs=[ANY] * n, out_specs=(ANY,) * n,
        scratch_shapes=[pltpu.SemaphoreType.DMA((n, 3))] * 4,
    )(*shards)
    return [_own_slot(st, sh) for st, sh in zip(stacks, shards)]


def _swap_comm(big, small=()):
    nb, n = len(big), len(big) + len(small)
    arrays = list(big) + list(small)

    def copies(src, dst, sems):
        send_sem, recv_sem = sems
        x, y, c, _ = _place()
        out = []
        for a in range(n):
            if a < nb:
                r2 = arrays[a].shape[1] // 2
                part = src[a].at[:, pl.ds(pl.multiple_of((1 - c) * r2, SUBLANES), r2), :]
            else:
                part = src[a]
            out.append(_remote(part, dst[a], send_sem.at[a], recv_sem.at[a], (x, y, 1 - c)))
        return out

    out_shape = [jax.ShapeDtypeStruct((N_SHARDS, g.shape[1] // 2, g.shape[2]), g.dtype) for g in big]
    out_shape += [jax.ShapeDtypeStruct(g.shape, g.dtype) for g in small]
    return _Comm(arrays, out_shape, [pltpu.SemaphoreType.DMA((n,))] * 2, copies)


def _scatter_comm(slabbed, small=()):
    ns, n = len(slabbed), len(slabbed) + len(small)
    arrays = list(slabbed) + list(small)

    def copies(src, dst, sems):
        send_sem, recv_sem = sems
        _, _, c, chips = _place()
        out = []
        for a in range(n):
            for k, chip in enumerate(chips):
                part = src[a].at[_shard_of(chip)] if a < ns else src[a]
                out.append(_remote(part, dst[a].at[k], send_sem.at[a, k], recv_sem.at[a, k], (*chip, c)))
        return out

    out_shape = [jax.ShapeDtypeStruct((3,) + g.shape[1:], g.dtype) for g in slabbed]
    out_shape += [jax.ShapeDtypeStruct((3,) + g.shape, g.dtype) for g in small]
    return _Comm(arrays, out_shape, [pltpu.SemaphoreType.DMA((n, 3))] * 2, copies)


def _send_comm(arrays):
    n = len(arrays)

    def copies(src, dst, sems):
        send_sem, recv_sem = sems
        x, y, c, _ = _place()
        return [_remote(src[a], dst[a], send_sem.at[a], recv_sem.at[a], (x, y, 1 - c)) for a in range(n)]

    return _Comm(arrays, [jax.ShapeDtypeStruct(h.shape, h.dtype) for h in arrays],
                 [pltpu.SemaphoreType.DMA((n,))] * 2, copies)


def _row_chunk(rows):
    for cand in (256, 176, 128, 64):
        if rows % cand == 0:
            return cand
    return rows


def _add_own_half(stack, received, c):
    _, r2, cols = received.shape

    def body(c_ref, a_r, b_r, o_r):
        del c_ref
        o_r[...] = (a_r[...] + b_r[...]).astype(BF16)

    return pl.pallas_call(
        body, name="add_own_half",
        grid_spec=pltpu.PrefetchScalarGridSpec(
            num_scalar_prefetch=1, grid=(N_SHARDS,),
            in_specs=[pl.BlockSpec((1, r2, cols), lambda s, c_ref: (s, c_ref[0], 0)),
                      pl.BlockSpec((1, r2, cols), lambda s, c_ref: (s, 0, 0))],
            out_specs=pl.BlockSpec((1, r2, cols), lambda s, c_ref: (s, 0, 0))),
        out_shape=jax.ShapeDtypeStruct(received.shape, BF16),
        compiler_params=_params(("parallel",)),
    )(c, stack, received)


def _chip_order_sum(me, own, s0, s1, s2):
    terms = []
    for s in range(N_SHARDS):
        d = jnp.bitwise_xor(me, s)
        terms.append(jnp.where(d == 0, own, jnp.where(d == 2, s0, jnp.where(d == 1, s1, s2))))
    return ((terms[0] + terms[1]) + terms[2]) + terms[3]


def _sum_chips(own_stack, slots, me):
    _, rows, cols = slots.shape
    rc = _row_chunk(rows)

    def body(me_ref, own_r, s_r, o_r):
        o_r[...] = _chip_order_sum(me_ref[0], own_r[0].astype(F32), s_r[0].astype(F32), s_r[1].astype(F32),
                                   s_r[2].astype(F32))

    return pl.pallas_call(
        body, name="sum_chips",
        grid_spec=pltpu.PrefetchScalarGridSpec(
            num_scalar_prefetch=1, grid=(rows // rc,),
            in_specs=[pl.BlockSpec((1, rc, cols), lambda i, me_ref: (me_ref[0], i, 0)),
                      pl.BlockSpec((3, rc, cols), lambda i, me_ref: (0, i, 0))],
            out_specs=pl.BlockSpec((rc, cols), lambda i, me_ref: (i, 0))),
        out_shape=jax.ShapeDtypeStruct((rows, cols), F32),
        compiler_params=_params(("parallel",)),
    )(me, own_stack, slots)


def _small_pair_sums(mine, theirs):
    n = len(mine)

    def body(*refs):
        for a in range(n):
            refs[2 * n + a][...] = refs[a][...] + refs[n + a][...]

    vm = pl.BlockSpec(memory_space=pltpu.VMEM)
    return pl.pallas_call(
        body, name="small_pair_sums", out_shape=tuple(jax.ShapeDtypeStruct(g.shape, g.dtype) for g in mine),
        in_specs=[vm] * (2 * n), out_specs=(vm,) * n,
        compiler_params=pltpu.CompilerParams(vmem_limit_bytes=VMEM_LIMIT),
    )(*mine, *theirs)


def _adam_math(w, g, m, v):
    m = ADAM_B1 * m + (1.0 - ADAM_B1) * g
    v = ADAM_B2 * v + (1.0 - ADAM_B2) * (g * g)
    m_hat = m / (1.0 - ADAM_B1 ** ADAM_STEP)
    v_hat = v / (1.0 - ADAM_B2 ** ADAM_STEP)
    delta = -ADAM_LR * (m_hat / (jnp.sqrt(v_hat) + ADAM_EPS) + ADAM_WD * w)
    return delta, m, v


def _small_totals(me, conv_stack, conv_slots, packed, packed_slots):
    def body(me_ref, cs_r, cslot_r, p_r, pslot_r, conv_r, tot_r):
        me_ = me_ref[0]
        conv_r[...] = _chip_order_sum(me_, cs_r[me_], cslot_r[0], cslot_r[1], cslot_r[2])
        tot_r[...] = _chip_order_sum(me_, p_r[...], pslot_r[0], pslot_r[1], pslot_r[2])

    vm = pl.BlockSpec(memory_space=pltpu.VMEM)
    return pl.pallas_call(
        body, name="small_totals",
        out_shape=(jax.ShapeDtypeStruct(conv_stack.shape[1:], F32), jax.ShapeDtypeStruct(packed.shape, F32)),
        in_specs=[pl.BlockSpec(memory_space=pltpu.SMEM)] + [vm] * 4, out_specs=(vm, vm),
    )(me, conv_stack, conv_slots, packed, packed_slots)


def _adam_small(gs, ws, ms, vs):
    n = len(gs)

    def body(*refs):
        for a in range(n):
            g_r, w_r, m_r, v_r = (refs[i * n + a] for i in range(4))
            d_r, nm_r, nv_r = (refs[(4 + i) * n + a] for i in range(3))
            d_r[...], nm_r[...], nv_r[...] = _adam_math(w_r[...], g_r[...], m_r[...], v_r[...])

    vm = pl.BlockSpec(memory_space=pltpu.VMEM)
    shapes = tuple(jax.ShapeDtypeStruct(w.shape, F32) for w in ws)
    out = pl.pallas_call(
        body, name="adam_small", out_shape=shapes * 3, in_specs=[vm] * (4 * n), out_specs=(vm,) * (3 * n),
        compiler_params=pltpu.CompilerParams(vmem_limit_bytes=VMEM_LIMIT),
    )(*gs, *ws, *ms, *vs)
    return out[:n], out[n:2 * n], out[2 * n:]


def _adam_big(w, mine, other, m, v, c):
    r2, cols = mine.shape
    rc = _row_chunk(r2)
    nch = r2 // rc

    def body(c_ref, w_r, mine_r, other_r, m_r, v_r, g_r, d_r, nm_r, nv_r):
        g = jnp.where(pl.program_id(0) == c_ref[0], mine_r[...], other_r[...])
        g_r[...] = g
        d_r[...], nm_r[...], nv_r[...] = _adam_math(w_r[...], g, m_r[...], v_r[...])

    full = pl.BlockSpec((rc, cols), lambda h, i, c_ref: (h * nch + i, 0))
    half = pl.BlockSpec((rc, cols), lambda h, i, c_ref: (i, 0))
    shape = jax.ShapeDtypeStruct((2 * r2, cols), F32)
    return pl.pallas_call(
        body, name="adam_big",
        grid_spec=pltpu.PrefetchScalarGridSpec(
            num_scalar_prefetch=1, grid=(2, nch),
            in_specs=[full, half, half, full, full], out_specs=(full,) * 4),
        out_shape=(shape,) * 4, compiler_params=_params(("parallel", "parallel")),
    )(c, w.reshape(2 * r2, cols), mine, other, m.reshape(2 * r2, cols), v.reshape(2 * r2, cols))


def kernel(x, w_in, b_in, ssm_lambda_re, ssm_lambda_im, ssm_log_dt, ssm_b_re, ssm_b_im, ssm_c_re, ssm_c_im, ssm_d, glu_w, glu_b, w_ssm_out, conv_w, w_conv_out, w_o, ln1_g, ln1_b, w_gate, w_up, w_down, ln2_g, ln2_b, loss_target, m_w_in, m_b_in, m_ssm_lambda_re, m_ssm_lambda_im, m_ssm_log_dt, m_ssm_b_re, m_ssm_b_im, m_ssm_c_re, m_ssm_c_im, m_ssm_d, m_glu_w, m_glu_b, m_w_ssm_out, m_conv_w, m_w_conv_out, m_w_o, m_ln1_g, m_ln1_b, m_w_gate, m_w_up, m_w_down, m_ln2_g, m_ln2_b, v_w_in, v_b_in, v_ssm_lambda_re, v_ssm_lambda_im, v_ssm_log_dt, v_ssm_b_re, v_ssm_b_im, v_ssm_c_re, v_ssm_c_im, v_ssm_d, v_glu_w, v_glu_b, v_w_ssm_out, v_conv_w, v_w_conv_out, v_w_o, v_ln1_g, v_ln1_b, v_w_gate, v_w_up, v_w_down, v_ln2_g, v_ln2_b):
    given = dict(locals())
    w = {n: given[n] for n in WEIGHTS}
    m = {n: given["m_" + n] for n in WEIGHTS}
    v = {n: given["v_" + n] for n in WEIGHTS}

    shards = {n: w[n][0].astype(BF16) for n in BIG}
    shards["conv_w"] = jnp.pad(conv_w[0], ((0, SUBLANES - 3), (0, 0)))
    c_arr = jnp.reshape(lax.axis_index("c"), (1,)).astype(jnp.int32)
    me = _shard_of((lax.axis_index("x"), lax.axis_index("y")))
    me_arr = jnp.reshape(me, (1,)).astype(jnp.int32)
    dx, pairs, conv_total, packed_total = _device_step(
        x[0], loss_target[0], {n: w[n] for n in SMALL}, shards, c_arr, me_arr)

    grad = _unpack_small(packed_total)
    loss_total = grad.pop("loss")[0]
    grad["conv_w"] = conv_total[:3][None]
    small_names = ("conv_w",) + SMALL
    ds, nms, nvs = _adam_small([grad[n] for n in small_names], [w[n] for n in small_names],
                               [m[n] for n in small_names], [v[n] for n in small_names])
    delta, new_m, new_v = {}, {}, {}
    for i, n in enumerate(small_names):
        delta[n], new_m[n], new_v[n] = ds[i], nms[i], nvs[i]
    for n in BIG:
        mine, other = pairs[n]
        shape = w[n].shape
        grad[n], delta[n], new_m[n], new_v[n] = (
            r.reshape(shape) for r in _adam_big(w[n], mine, other, m[n], v[n], c_arr))

    return (loss_total, dx[None], *[grad[n] for n in WEIGHTS], *[delta[n] for n in WEIGHTS],
            *[new_m[n] for n in WEIGHTS], *[new_v[n] for n in WEIGHTS])
```

```python
import functools
import math

import jax
import jax.numpy as jnp
from jax import lax
from jax.experimental import pallas as pl
from jax.experimental.pallas import tpu as pltpu

F32 = jnp.float32
BF16 = jnp.bfloat16

D_MODEL = 1024
IN_COLS = 4096
SSM_W = 512
N_GROUPS = 32
N_STATE = 64
GROUP_C = 16
STATE_COLS = N_GROUPS * N_STATE
STRIP = 512
N_STRIPS = STATE_COLS // STRIP
FFN_SHARD = 704
N_SHARDS = 4
ALPHA = 2.0 ** 0.25
LN_EPS = 1e-5
GELU_K = math.sqrt(2.0 / math.pi)
GELU_C = 0.044715

ADAM_LR = 0.001
ADAM_B1 = 0.9
ADAM_B2 = 0.999
ADAM_EPS = 1e-08
ADAM_WD = 0.01
ADAM_STEP = 10

V7X_VMEM_BYTES = 64 * 1024 * 1024
VMEM_LIMIT = V7X_VMEM_BYTES - 8 * 1024 * 1024
SUBLANES = 8

MESH = pl.DeviceIdType.MESH
ANY = pl.BlockSpec(memory_space=pl.ANY)


def _dot(a, b):
    return jnp.dot(a.astype(BF16), b.astype(BF16), preferred_element_type=F32)


def _dot_t(a, b):
    return lax.dot_general(a.astype(BF16), b.astype(BF16), (((1,), (1,)), ((), ())),
                           preferred_element_type=F32)


def _tdot(a, b):
    return lax.dot_general(a.astype(BF16), b.astype(BF16), (((0,), (0,)), ((), ())),
                           preferred_element_type=F32)


def _sigmoid(v):
    return 1.0 / (1.0 + jnp.exp(-v))


def _split3(v):
    hi = v.astype(BF16)
    r1 = v - hi.astype(F32)
    mid = r1.astype(BF16)
    lo = (r1 - mid.astype(F32)).astype(BF16)
    return hi, mid, lo


def _exact_dot(v, sel):
    hi, mid, lo = _split3(v)
    return (jnp.dot(hi, sel, preferred_element_type=F32)
            + jnp.dot(mid, sel, preferred_element_type=F32)
            + jnp.dot(lo, sel, preferred_element_type=F32))


def _const(shape):
    nd = len(shape)
    return pl.BlockSpec(shape, lambda *_: (0,) * nd)


def _params(sem, vmem=VMEM_LIMIT):
    return pltpu.CompilerParams(dimension_semantics=sem, vmem_limit_bytes=vmem)


def _gelu_parts(v):
    inner = GELU_K * (v + GELU_C * v * v * v)
    t = jnp.tanh(inner)
    g = 0.5 * v * (1.0 + t)
    dg = 0.5 * (1.0 + t) + 0.5 * v * (1.0 - t * t) * GELU_K * (1.0 + 3.0 * GELU_C * v * v)
    return g, dg


class _Comm:
    def __init__(self, inputs, out_shape, sems, copies, aliased=False):
        self.inputs, self.out_shape, self.sems = list(inputs), tuple(out_shape), list(sems)
        self.copies, self.aliased = copies, aliased


def _launch(body, comms, *, name, grid, in_specs, out_specs, out_shape, scratch_shapes=(), aliases=None, sem=None):
    comms = list(comms)
    n_in, n_out, n_scr = len(in_specs), len(out_specs), len(scratch_shapes)
    aliases = dict(aliases or {})
    layout = []
    p_in, p_out, p_sem = n_in, n_out, 0
    for cm in comms:
        layout.append((p_in, p_out, p_sem))
        if cm.aliased:
            for i in range(len(cm.inputs)):
                aliases[p_in + i] = p_out + i
        p_in, p_out, p_sem = p_in + len(cm.inputs), p_out + len(cm.out_shape), p_sem + len(cm.sems)
    tot_in, tot_out = p_in, p_out

    def fused(*refs):
        ins, outs = refs[:tot_in], refs[tot_in:tot_in + tot_out]
        scr = refs[tot_in + tot_out:tot_in + tot_out + n_scr]
        sems = refs[tot_in + tot_out + n_scr:]

        def descriptors():
            out = []
            for cm, (a, b, s) in zip(comms, layout):
                out += cm.copies(ins[a:a + len(cm.inputs)], outs[b:b + len(cm.out_shape)], sems[s:s + len(cm.sems)])
            return out

        steps = [pl.program_id(d) for d in range(len(grid))]
        first = functools.reduce(jnp.logical_and, [s == 0 for s in steps]) if grid else None
        last = functools.reduce(jnp.logical_and, [s == g - 1 for s, g in zip(steps, grid)]) if grid else None

        def start():
            for cp in descriptors():
                cp.start()

        def finish():
            for cp in descriptors():
                cp.wait()

        if comms:
            pl.when(first)(start) if grid else start()
        if body is not None:
            body(*ins[:n_in], *outs[:n_out], *scr)
        if comms:
            pl.when(last)(finish) if grid else finish()

    specs_in = list(in_specs) + [ANY] * (tot_in - n_in)
    specs_out = tuple(out_specs) + (ANY,) * (tot_out - n_out)
    shapes = tuple(out_shape) + tuple(s for cm in comms for s in cm.out_shape)
    scratch = list(scratch_shapes) + [s for cm in comms for s in cm.sems]
    if comms or sem is None:
        sem = ("arbitrary",) * len(grid)
    kwargs = dict(grid=grid) if grid else {}
    call = pl.pallas_call(fused, name=name, out_shape=shapes, in_specs=specs_in, out_specs=specs_out,
                          scratch_shapes=scratch, input_output_aliases=aliases,
                          compiler_params=_params(sem) if grid else None, **kwargs)

    def run(*args):
        out = call(*args, *(a for cm in comms for a in cm.inputs))
        results, rest = out[:n_out], out[n_out:]
        per_comm = []
        for cm in comms:
            per_comm.append(rest[:len(cm.out_shape)])
            rest = rest[len(cm.out_shape):]
        return results, per_comm

    return run


def _ssm_discretise(lr, li, ldt, lr16, li16, ldt16, brt, bit):
    def lam_bar(lr_, li_, ldt_):
        dt = jnp.exp(ldt_)
        mag = jnp.exp(lr_ * dt)
        return mag * jnp.cos(li_ * dt), mag * jnp.sin(li_ * dt)

    lb_re, lb_im = lam_bar(lr, li, ldt)
    l16_re, l16_im = lam_bar(lr16, li16, ldt16)
    den = lr16 * lr16 + li16 * li16
    num_re = l16_re - 1.0
    fr = (num_re * lr16 + l16_im * li16) / den
    fi = (l16_im * lr16 - num_re * li16) / den
    bb_re = fr * brt - fi * bit
    bb_im = fr * bit + fi * brt
    return lb_re, lb_im, bb_re, bb_im


def _strip_selectors():
    p = lax.broadcasted_iota(jnp.int32, (N_STATE, STRIP), 0)
    col = lax.broadcasted_iota(jnp.int32, (N_STATE, STRIP), 1)
    rep = ((col & (N_STATE - 1)) == p).astype(BF16)
    row = lax.broadcasted_iota(jnp.int32, (SSM_W, STRIP), 0)
    col2 = lax.broadcasted_iota(jnp.int32, (SSM_W, STRIP), 1)
    mask = (((row >> 4) & 7) == (col2 >> 6))
    return rep, mask


def _ssm_prepare(lr, li, ldt, lr16, li16, ldt16, brt, bit, cre, cim):
    def body(lr_r, li_r, ldt_r, lr16_r, li16_r, ldt16_r, brt_r, bit_r, cre_r, cim_r,
             pwr_r, pwi_r, bbr_r, bbi_r, ctr_r, cti_r):
        lb_re, lb_im, bb_re, bb_im = _ssm_discretise(
            lr_r[...], li_r[...], ldt_r[...], lr16_r[...], li16_r[...], ldt16_r[...], brt_r[...], bit_r[...])
        pr, pi_ = lb_re, lb_im
        pwr_r[0] = pr
        pwi_r[0] = pi_
        for k in range(1, SUBLANES):
            pr, pi_ = pr * lb_re - pi_ * lb_im, pr * lb_im + pi_ * lb_re
            pwr_r[k] = pr
            pwi_r[k] = pi_
        rep, mask = _strip_selectors()
        for src, dst in ((bb_re, bbr_r), (bb_im, bbi_r), (cre_r[...], ctr_r), (cim_r[...], cti_r)):
            wide = jnp.dot(src.astype(BF16), rep, preferred_element_type=F32)
            dst[...] = jnp.where(mask, wide, 0.0).astype(BF16)

    vm = pl.BlockSpec(memory_space=pltpu.VMEM)
    return pl.pallas_call(
        body, name="ssm_prepare",
        out_shape=(jax.ShapeDtypeStruct((SUBLANES, N_GROUPS, N_STATE), F32),) * 2
        + (jax.ShapeDtypeStruct((SSM_W, STRIP), BF16),) * 4,
        in_specs=[vm] * 10, out_specs=(vm,) * 6,
    )(lr, li, ldt, lr16, li16, ldt16, brt, bit, cre, cim)


def _scan_tables(pwr, pwi):
    pr = pwr.reshape(SUBLANES, STATE_COLS)
    pi_ = pwi.reshape(SUBLANES, STATE_COLS)
    row = jnp.arange(SUBLANES)[:, None]
    z = jnp.zeros((SUBLANES, STATE_COLS), F32)

    def lvl(p, k, keep):
        return jnp.where(keep, jnp.broadcast_to(p[k - 1][None], (SUBLANES, STATE_COLS)), z)

    fwd_re = [lvl(pr, 1, row >= 1), lvl(pr, 2, row >= 2), lvl(pr, 4, row >= 4), pr]
    fwd_im = [lvl(pi_, 1, row >= 1), lvl(pi_, 2, row >= 2), lvl(pi_, 4, row >= 4), pi_]
    rev_re = [lvl(pr, 1, row <= 6), lvl(pr, 2, row <= 5), lvl(pr, 4, row <= 3), pr[::-1]]
    rev_im = [-lvl(pi_, 1, row <= 6), -lvl(pi_, 2, row <= 5), -lvl(pi_, 4, row <= 3), -pi_[::-1]]
    return jnp.stack(fwd_re + fwd_im + rev_re + rev_im)


def _ssm_param_grads(lr, li, ldt, lr16, li16, ldt16, brt, bit, dlbr, dlbi, dbbr, dbbi, dctr, dcti):
    def body(lr_r, li_r, ldt_r, lr16_r, li16_r, ldt16_r, brt_r, bit_r,
             dlbr_r, dlbi_r, dbbr_r, dbbi_r, dctr_r, dcti_r,
             glr_r, gli_r, gldt_r, gbrt_r, gbit_r, gcre_r, gcim_r):
        rep, mask = _strip_selectors()

        def fold(acc):
            return sum(lax.dot_general(t, rep, (((1,), (1,)), ((), ())), preferred_element_type=F32)
                       for t in _split3(jnp.where(mask, acc, 0.0)))

        g_lb_re = jnp.sum(dlbr_r[...], axis=0)
        g_lb_im = jnp.sum(dlbi_r[...], axis=0)
        g_bb_re = fold(dbbr_r[...])
        g_bb_im = fold(dbbi_r[...])
        gcre_r[...] = fold(dctr_r[...])
        gcim_r[...] = fold(dcti_r[...])
        prim = (lr_r[...], li_r[...], ldt_r[...], lr16_r[...], li16_r[...], ldt16_r[...], brt_r[...], bit_r[...])
        _, vjp = jax.vjp(_ssm_discretise, *prim)
        g_lr, g_li, g_ldt, g_lr16, g_li16, g_ldt16, g_brt, g_bit = vjp((g_lb_re, g_lb_im, g_bb_re, g_bb_im))
        grp = lax.broadcasted_iota(jnp.int32, (N_GROUPS, SSM_W), 0)
        rw = lax.broadcasted_iota(jnp.int32, (N_GROUPS, SSM_W), 1)
        gsum = ((rw >> 4) == grp).astype(BF16)

        def group_sum(v):
            return sum(jnp.dot(gsum, t, preferred_element_type=F32) for t in _split3(v))

        glr_r[...] = g_lr + group_sum(g_lr16)
        gli_r[...] = g_li + group_sum(g_li16)
        gldt_r[...] = g_ldt + jnp.sum(group_sum(g_ldt16), axis=1, keepdims=True)
        gbrt_r[...] = g_brt
        gbit_r[...] = g_bit

    vm = pl.BlockSpec(memory_space=pltpu.VMEM)
    gp = jax.ShapeDtypeStruct((N_GROUPS, N_STATE), F32)
    gb = jax.ShapeDtypeStruct((SSM_W, N_STATE), F32)
    return pl.pallas_call(
        body, name="ssm_param_grads",
        out_shape=(gp, gp, jax.ShapeDtypeStruct((N_GROUPS, 1), F32), gb, gb, gb, gb),
        in_specs=[vm] * 14, out_specs=(vm,) * 7,
    )(lr, li, ldt, lr16, li16, ldt16, brt, bit, dlbr, dlbi, dbbr, dbbi, dctr, dcti)


def _in_proj(x, w_in_st, b_in, comms=()):
    t = x.shape[0]
    tm = 512

    def body(x_r, w_r, b_r, o_r):
        o_r[...] = _dot(x_r[...], w_r[0]) + b_r[...]

    (proj,), sent = _launch(
        body, comms, name="in_proj", grid=(t // tm, N_SHARDS),
        out_shape=(jax.ShapeDtypeStruct((t, IN_COLS), F32),),
        in_specs=[pl.BlockSpec((tm, D_MODEL), lambda i, j: (i, 0)),
                  pl.BlockSpec((1, D_MODEL, D_MODEL), lambda i, j: (j, 0, 0)),
                  pl.BlockSpec((1, D_MODEL), lambda i, j: (0, j))],
        out_specs=(pl.BlockSpec((tm, D_MODEL), lambda i, j: (i, j)),),
        sem=("parallel", "arbitrary"),
    )(x, w_in_st, b_in)
    return proj, sent


def _cmul_add(xr, xi, mr, mi, sr, si):
    return xr + (mr * sr - mi * si), xi + (mr * si + mi * sr)


def _ssm_forward(proj, bbr, bbi, ctr, cti, d_skip, tab, comms=(), tc=512):
    t = proj.shape[0]
    nb = tc // SUBLANES

    def body(u_r, bbr_r, bbi_r, ctr_r, cti_r, d_r, tab_r, xsr_r, xsi_r, y_r, car_r, car_i):
        @pl.when(pl.program_id(1) == 0)
        def _():
            car_r[...] = jnp.zeros_like(car_r)
            car_i[...] = jnp.zeros_like(car_i)

        u = u_r[...]
        xsr_r[...] = _dot(u, bbr_r[...])
        xsi_r[...] = _dot(u, bbi_r[...])

        def block(b, carry):
            cr, ci = carry
            rows = pl.ds(pl.multiple_of(b * SUBLANES, SUBLANES), SUBLANES)
            xr = xsr_r[rows, :]
            xi = xsi_r[rows, :]
            for lvl, s in ((0, 1), (1, 2), (2, 4)):
                xr, xi = _cmul_add(xr, xi, tab_r[lvl], tab_r[4 + lvl],
                                   pltpu.roll(xr, s, 0), pltpu.roll(xi, s, 0))
            xr, xi = _cmul_add(xr, xi, tab_r[3], tab_r[7],
                               jnp.broadcast_to(cr, xr.shape), jnp.broadcast_to(ci, xi.shape))
            xsr_r[rows, :] = xr
            xsi_r[rows, :] = xi
            return xr[SUBLANES - 1:SUBLANES, :], xi[SUBLANES - 1:SUBLANES, :]

        cr, ci = lax.fori_loop(0, nb, block, (car_r[...], car_i[...]))
        car_r[...] = cr
        car_i[...] = ci
        y_r[...] = _dot_t(xsr_r[...], ctr_r[...]) - _dot_t(xsi_r[...], cti_r[...]) + d_r[...] * u

    strip_mat = pl.BlockSpec((128, STRIP), lambda j, k: (j, 0))
    states = pl.BlockSpec((tc, STRIP), lambda j, k: (k, j))
    return _launch(
        body, comms, name="ssm_forward", grid=(N_STRIPS, t // tc),
        out_shape=(jax.ShapeDtypeStruct((t, STATE_COLS), F32), jax.ShapeDtypeStruct((t, STATE_COLS), F32),
                   jax.ShapeDtypeStruct((t, SSM_W), F32)),
        in_specs=[pl.BlockSpec((tc, 128), lambda j, k: (k, j)),
                  strip_mat, strip_mat, strip_mat, strip_mat,
                  pl.BlockSpec((1, 128), lambda j, k: (0, j)),
                  pl.BlockSpec((16, SUBLANES, STRIP), lambda j, k: (0, 0, j))],
        out_specs=(states, states, pl.BlockSpec((tc, 128), lambda j, k: (k, j))),
        scratch_shapes=[pltpu.VMEM((1, STRIP), F32), pltpu.VMEM((1, STRIP), F32)],
        sem=("parallel", "arbitrary"),
    )(proj, bbr, bbi, ctr, cti, d_skip, tab)


def _shift_down(v, prev, n):
    row = lax.broadcasted_iota(jnp.int32, v.shape, 0)
    out = pltpu.roll(v, n, 0)
    for r in range(n):
        src = prev[SUBLANES - n + r:SUBLANES - n + r + 1, :]
        out = jnp.where(row == r, jnp.broadcast_to(src, v.shape), out)
    return out


def _shift_up(v, nxt, n):
    rows = v.shape[0]
    row = lax.broadcasted_iota(jnp.int32, v.shape, 0)
    out = pltpu.roll(v, rows - n, 0)
    for r in range(n):
        src = nxt[r:r + 1, :]
        out = jnp.where(row == rows - n + r, jnp.broadcast_to(src, v.shape), out)
    return out


def _conv3(q, q_prev, w):
    return w[2:3, :] * q + w[1:2, :] * _shift_down(q, q_prev, 1) + w[0:1, :] * _shift_down(q, q_prev, 2)


def _mixer_forward(x, proj, ya0, glu_w, glu_b, wso_st, conv_w8, wco_st, w_o, comms=(), tm=256):
    t = x.shape[0]
    hb = tm // SUBLANES

    def body(x_r, ya0_r, h_r, cg_r, bg_r, ga_r, gb_r, hp_r, cgp_r,
             glu_w_r, glu_b_r, wso_r, cw_r, wco_r, wo_r, xh_r, rstd_r, ya_r, yb_r):
        i = pl.program_id(0)
        g, _ = _gelu_parts(ya0_r[...])
        ya1 = g * _sigmoid(_dot(g, glu_w_r[...]) + glu_b_r[...])
        q = cg_r[...] * h_r[...]
        q_prev = jnp.where(i > 0, cgp_r[...] * hp_r[...], 0.0)
        yb0 = bg_r[...] * _conv3(q, q_prev, cw_r[...])
        for j in range(N_SHARDS):
            ya_r[:, 256 * j:256 * (j + 1)] = _dot(ya1, wso_r[j])
            yb_r[:, 256 * j:256 * (j + 1)] = _dot(yb0, wco_r[j])
        merged = _sigmoid(ga_r[...]) * ya_r[...] + _sigmoid(gb_r[...]) * yb_r[...]
        r1 = ALPHA * x_r[...] + _dot(merged, wo_r[...])
        mu = jnp.mean(r1, axis=-1, keepdims=True)
        cen = r1 - mu
        rstd = lax.rsqrt(jnp.mean(cen * cen, axis=-1, keepdims=True) + LN_EPS)
        xh_r[...] = cen * rstd
        rstd_r[...] = rstd

    def col(w, c):
        return pl.BlockSpec((tm, w), lambda i: (i, c))

    def prev(c):
        return pl.BlockSpec((SUBLANES, SSM_W), lambda i: (jnp.maximum(i * hb - 1, 0), c))

    return _launch(
        body, comms, name="mixer_forward", grid=(t // tm,),
        out_shape=(jax.ShapeDtypeStruct((t, D_MODEL), F32), jax.ShapeDtypeStruct((t, 1), F32),
                   jax.ShapeDtypeStruct((t, D_MODEL), F32), jax.ShapeDtypeStruct((t, D_MODEL), F32)),
        in_specs=[col(D_MODEL, 0), col(SSM_W, 0), col(SSM_W, 1), col(SSM_W, 2), col(SSM_W, 3),
                  col(D_MODEL, 2), col(D_MODEL, 3), prev(1), prev(2),
                  _const((SSM_W, SSM_W)), _const((1, SSM_W)), _const((N_SHARDS, SSM_W, 256)),
                  _const((SUBLANES, SSM_W)), _const((N_SHARDS, SSM_W, 256)), _const((D_MODEL, D_MODEL))],
        out_specs=(col(D_MODEL, 0), pl.BlockSpec((tm, 1), lambda i: (i, 0)), col(D_MODEL, 0), col(D_MODEL, 0)),
        sem=("parallel",),
    )(x, ya0, proj, proj, proj, proj, proj, proj, proj, glu_w, glu_b, wso_st, conv_w8, wco_st, w_o)


def _layer_norm_bwd(dxhat, xhat, rstd):
    m1 = jnp.mean(dxhat, axis=-1, keepdims=True)
    m2 = jnp.mean(dxhat * xhat, axis=-1, keepdims=True)
    return rstd * (dxhat - m1 - xhat * m2)


def _ffn_step(xhat1, rstd1, target, ln1_g, ln1_b, ln2_g, ln2_b, wg_st, wu_st, wd_st, tm=256):
    t = xhat1.shape[0]

    def body(xh_r, rstd_r, tgt_r, g1_r, b1_r, g2_r, b2_r, wg_r, wu_r, wd_r,
             loss_r, dr1_r, x1b_r, dr2b_r, hid_r, dhg_r, dhu_r, dg2_r, db2_r, dg1_r, db1_r,
             hg_s, hu_s):
        @pl.when(pl.program_id(0) == 0)
        def _():
            for r in (loss_r, dg2_r, db2_r, dg1_r, db1_r):
                r[...] = jnp.zeros_like(r)

        xhat1_v = xh_r[...]
        x1 = xhat1_v * g1_r[...] + b1_r[...]
        x1b = x1.astype(BF16)
        x1b_r[...] = x1b
        ffn = jnp.zeros((tm, D_MODEL), F32)
        for j in range(N_SHARDS):
            hg = lax.dot_general(x1b, wg_r[j], (((1,), (1,)), ((), ())), preferred_element_type=F32)
            hu = lax.dot_general(x1b, wu_r[j], (((1,), (1,)), ((), ())), preferred_element_type=F32)
            hg_s[j] = hg
            hu_s[j] = hu
            hid = (hg * _sigmoid(hg) * hu).astype(BF16)
            hid_r[j] = hid
            ffn = ffn + jnp.dot(hid, wd_r[j], preferred_element_type=F32)
        r2 = ALPHA * x1 + ffn
        mu = jnp.mean(r2, axis=-1, keepdims=True)
        cen = r2 - mu
        rstd2 = lax.rsqrt(jnp.mean(cen * cen, axis=-1, keepdims=True) + LN_EPS)
        xhat2 = cen * rstd2
        diff = (xhat2 * g2_r[...] + b2_r[...]) - tgt_r[...]
        loss_r[...] += 0.5 * jnp.sum(jnp.mean(diff * diff, axis=-1, keepdims=True), axis=0, keepdims=True)
        dy = diff * (1.0 / D_MODEL)
        dg2_r[...] += jnp.sum(dy * xhat2, axis=0, keepdims=True)
        db2_r[...] += jnp.sum(dy, axis=0, keepdims=True)
        dr2 = _layer_norm_bwd(dy * g2_r[...], xhat2, rstd2)
        dr2b = dr2.astype(BF16)
        dr2b_r[...] = dr2b
        dx1 = ALPHA * dr2
        for j in range(N_SHARDS):
            dhid = lax.dot_general(dr2b, wd_r[j], (((1,), (1,)), ((), ())), preferred_element_type=F32)
            hg = hg_s[j]
            hu = hu_s[j]
            sg = _sigmoid(hg)
            dhu = (dhid * (hg * sg)).astype(BF16)
            dhg = (dhid * hu * (sg * (1.0 + hg * (1.0 - sg)))).astype(BF16)
            dhg_r[j] = dhg
            dhu_r[j] = dhu
            dx1 = dx1 + jnp.dot(dhg, wg_r[j], preferred_element_type=F32)
            dx1 = dx1 + jnp.dot(dhu, wu_r[j], preferred_element_type=F32)
        dg1_r[...] += jnp.sum(dx1 * xhat1_v, axis=0, keepdims=True)
        db1_r[...] += jnp.sum(dx1, axis=0, keepdims=True)
        dr1_r[...] = _layer_norm_bwd(dx1 * g1_r[...], xhat1_v, rstd_r[...])

    tile = pl.BlockSpec((tm, D_MODEL), lambda i: (i, 0))
    hidden = pl.BlockSpec((N_SHARDS, tm, FFN_SHARD), lambda i: (0, i, 0))
    vec = _const((1, D_MODEL))
    hid_shape = jax.ShapeDtypeStruct((N_SHARDS, t, FFN_SHARD), BF16)
    vec_shape = jax.ShapeDtypeStruct((1, D_MODEL), F32)
    return pl.pallas_call(
        body, name="ffn_step", grid=(t // tm,),
        out_shape=(jax.ShapeDtypeStruct((1, 1), F32), jax.ShapeDtypeStruct((t, D_MODEL), F32),
                   jax.ShapeDtypeStruct((t, D_MODEL), BF16), jax.ShapeDtypeStruct((t, D_MODEL), BF16),
                   hid_shape, hid_shape, hid_shape, vec_shape, vec_shape, vec_shape, vec_shape),
        in_specs=[tile, pl.BlockSpec((tm, 1), lambda i: (i, 0)), tile, vec, vec, vec, vec,
                  _const((N_SHARDS, FFN_SHARD, D_MODEL)), _const((N_SHARDS, FFN_SHARD, D_MODEL)),
                  _const((N_SHARDS, FFN_SHARD, D_MODEL))],
        out_specs=(_const((1, 1)), tile, tile, tile, hidden, hidden, hidden, vec, vec, vec, vec),
        scratch_shapes=[pltpu.VMEM((N_SHARDS, tm, FFN_SHARD), F32), pltpu.VMEM((N_SHARDS, tm, FFN_SHARD), F32)],
        compiler_params=_params(("arbitrary",)),
    )(xhat1, rstd1, target, ln1_g, ln1_b, ln2_g, ln2_b, wg_st, wu_st, wd_st)


def _ffn_weight_grads(x1b, dr2b, hid, dhg, dhu, tk=512):
    t = x1b.shape[0]

    def body(x_r, dr_r, hid_r, dhg_r, dhu_r, gwg_r, gwu_r, gwd_r):
        @pl.when(pl.program_id(1) == 0)
        def _():
            for r in (gwg_r, gwu_r, gwd_r):
                r[...] = jnp.zeros_like(r)

        gwg_r[0] += _tdot(dhg_r[0], x_r[...])
        gwu_r[0] += _tdot(dhu_r[0], x_r[...])
        gwd_r[0] += _tdot(hid_r[0], dr_r[...])

    tile = pl.BlockSpec((tk, D_MODEL), lambda j, k: (k, 0))
    hidden = pl.BlockSpec((1, tk, FFN_SHARD), lambda j, k: (j, k, 0))
    row = pl.BlockSpec((1, FFN_SHARD, D_MODEL), lambda j, k: (j, 0, 0))
    return pl.pallas_call(
        body, name="ffn_weight_grads", grid=(N_SHARDS, t // tk),
        out_shape=(jax.ShapeDtypeStruct((N_SHARDS, FFN_SHARD, D_MODEL), F32),) * 3,
        in_specs=[tile, tile, hidden, hidden, hidden],
        out_specs=(row, row, row),
        compiler_params=_params(("parallel", "arbitrary")),
    )(x1b, dr2b, hid, dhg, dhu)


def _mixer_backward(dr1, proj, ya0, ya, yb, glu_w, glu_b, wso_st, conv_w8, wco_st, w_o, comms=(), tm=256):
    t = dr1.shape[0]
    hb = tm // SUBLANES
    last_block = t // SUBLANES - 1

    def body(dr1_r, dr1n_r, ya0_r, ya_r, yb_r, h_r, cg_r, bg_r, ga_r, gb_r, hp_r, cgp_r, bgn_r, gbn_r,
             glu_w_r, glu_b_r, wso_r, cw_r, wco_r, wo_r,
             dya0_r, dproj_r, dbias_r, gwo_r, gwso_r, gwco_r, gglu_w_r, gglu_b_r, gconv_r):
        i = pl.program_id(0)

        @pl.when(i == 0)
        def _():
            for r in (dbias_r, gwo_r, gwso_r, gwco_r, gglu_w_r, gglu_b_r, gconv_r):
                r[...] = jnp.zeros_like(r)

        dr1_v = dr1_r[...]
        dmerged = _dot_t(dr1_v, wo_r[...])
        sa = _sigmoid(ga_r[...])
        sb = _sigmoid(gb_r[...])
        ya_v = ya_r[...]
        yb_v = yb_r[...]
        gwo_r[...] += _tdot(sa * ya_v + sb * yb_v, dr1_v)
        dya = dmerged * sa
        dyb = dmerged * sb
        dga = dmerged * ya_v * (sa * (1.0 - sa))
        dgb = dmerged * yb_v * (sb * (1.0 - sb))

        g, gelu_grad = _gelu_parts(ya0_r[...])
        s1 = _sigmoid(_dot(g, glu_w_r[...]) + glu_b_r[...])
        ya1 = g * s1
        dya1 = jnp.zeros((tm, SSM_W), F32)
        for j in range(N_SHARDS):
            dya_j = dya[:, 256 * j:256 * (j + 1)]
            gwso_r[j] += _tdot(ya1, dya_j)
            dya1 = dya1 + _dot_t(dya_j, wso_r[j])
        dz1 = dya1 * g * (s1 * (1.0 - s1))
        gglu_b_r[...] += jnp.sum(dz1, axis=0, keepdims=True)
        gglu_w_r[...] += _tdot(g, dz1)
        dya0_r[...] = (dya1 * s1 + _dot_t(dz1, glu_w_r[...])) * gelu_grad

        cw = cw_r[...]
        h = h_r[...]
        cg = cg_r[...]
        bg = bg_r[...]
        q = cg * h
        q_prev = jnp.where(i > 0, cgp_r[...] * hp_r[...], 0.0)
        q1 = _shift_down(q, q_prev, 1)
        q2 = _shift_down(q, q_prev, 2)
        z = cw[2:3, :] * q + cw[1:2, :] * q1 + cw[0:1, :] * q2
        yb0 = bg * z
        dyb0 = jnp.zeros((tm, SSM_W), F32)
        for j in range(N_SHARDS):
            dyb_j = dyb[:, 256 * j:256 * (j + 1)]
            gwco_r[j] += _tdot(yb0, dyb_j)
            dyb0 = dyb0 + _dot_t(dyb_j, wco_r[j])
        dbg = dyb0 * z
        dz = dyb0 * bg
        dyb_n = _dot_t(dr1n_r[...], wo_r[...]) * _sigmoid(gbn_r[...])
        dyb0_n = jnp.zeros((SUBLANES, SSM_W), F32)
        for j in range(N_SHARDS):
            dyb0_n = dyb0_n + _dot_t(dyb_n[:, 256 * j:256 * (j + 1)], wco_r[j])
        dz_next = jnp.where(i < pl.num_programs(0) - 1, dyb0_n * bgn_r[...], 0.0)
        dq = cw[2:3, :] * dz + cw[1:2, :] * _shift_up(dz, dz_next, 1) + cw[0:1, :] * _shift_up(dz, dz_next, 2)
        gconv_r[0:1, :] += jnp.sum(dz * q2, axis=0, keepdims=True)
        gconv_r[1:2, :] += jnp.sum(dz * q1, axis=0, keepdims=True)
        gconv_r[2:3, :] += jnp.sum(dz * q, axis=0, keepdims=True)
        dh = dq * cg
        dcg = dq * h

        dproj_r[:, 0:512] = jnp.zeros((tm, SSM_W), BF16)
        pieces = ((512, dh), (1024, dcg), (1536, dbg), (2048, dga), (3072, dgb))
        for off, val in pieces:
            w = val.shape[1]
            dproj_r[:, off:off + w] = val.astype(BF16)
            dbias_r[:, off:off + w] += jnp.sum(val, axis=0, keepdims=True)

    def col(w, c):
        return pl.BlockSpec((tm, w), lambda i: (i, c))

    def prev(c):
        return pl.BlockSpec((SUBLANES, SSM_W), lambda i: (jnp.maximum(i * hb - 1, 0), c))

    def nxt(w, c):
        return pl.BlockSpec((SUBLANES, w), lambda i: (jnp.minimum((i + 1) * hb, last_block), c))

    sh = jax.ShapeDtypeStruct
    return _launch(
        body, comms, name="mixer_backward", grid=(t // tm,),
        out_shape=(sh((t, SSM_W), F32), sh((t, IN_COLS), BF16), sh((1, IN_COLS), F32),
                   sh((D_MODEL, D_MODEL), F32), sh((N_SHARDS, SSM_W, 256), F32), sh((N_SHARDS, SSM_W, 256), F32),
                   sh((SSM_W, SSM_W), F32), sh((1, SSM_W), F32), sh((SUBLANES, SSM_W), F32)),
        in_specs=[col(D_MODEL, 0), nxt(D_MODEL, 0), col(SSM_W, 0), col(D_MODEL, 0), col(D_MODEL, 0),
                  col(SSM_W, 1), col(SSM_W, 2), col(SSM_W, 3), col(D_MODEL, 2), col(D_MODEL, 3),
                  prev(1), prev(2), nxt(SSM_W, 3), nxt(D_MODEL, 3),
                  _const((SSM_W, SSM_W)), _const((1, SSM_W)), _const((N_SHARDS, SSM_W, 256)),
                  _const((SUBLANES, SSM_W)), _const((N_SHARDS, SSM_W, 256)), _const((D_MODEL, D_MODEL))],
        out_specs=(col(SSM_W, 0), col(IN_COLS, 0), _const((1, IN_COLS)),
                   _const((D_MODEL, D_MODEL)), _const((N_SHARDS, SSM_W, 256)), _const((N_SHARDS, SSM_W, 256)),
                   _const((SSM_W, SSM_W)), _const((1, SSM_W)), _const((SUBLANES, SSM_W))),
        sem=("arbitrary",),
    )(dr1, dr1, ya0, ya, yb, proj, proj, proj, proj, proj, proj, proj, proj, proj,
      glu_w, glu_b, wso_st, conv_w8, wco_st, w_o)


def _ssm_backward(dya0, proj, xsr, xsi, bbr, bbi, ctr, cti, d_skip, tab, dproj, comms=(), tc=512):
    t = proj.shape[0]
    nb = tc // SUBLANES
    nk = t // tc

    def body(dy_r, u_r, xsr_r, xsi_r, bbr_r, bbi_r, ctr_r, cti_r, d_r, tab_r, dproj_any,
             du_r, dus_r, gbbr_r, gbbi_r, gctr_r, gcti_r, glbr_r, glbi_r, gd_r,
             gr_s, gi_s, car_r, car_i):
        del dproj_any

        @pl.when(pl.program_id(1) == 0)
        def _():
            for r in (car_r, car_i, dus_r, gbbr_r, gbbi_r, gctr_r, gcti_r, glbr_r, glbi_r, gd_r):
                r[...] = jnp.zeros_like(r)

        dy = dy_r[...]
        u = u_r[...]
        gr_s[...] = _dot(dy, ctr_r[...])
        gi_s[...] = -_dot(dy, cti_r[...])
        last_row = lax.broadcasted_iota(jnp.int32, (SUBLANES, STRIP), 0) == SUBLANES - 1

        def block(n, carry):
            cr, ci, ar, ai = carry
            rows = pl.ds(pl.multiple_of((nb - 1 - n) * SUBLANES, SUBLANES), SUBLANES)
            gr = gr_s[rows, :]
            gi = gi_s[rows, :]
            for lvl, s in ((0, 1), (1, 2), (2, 4)):
                gr, gi = _cmul_add(gr, gi, tab_r[8 + lvl], tab_r[12 + lvl],
                                   pltpu.roll(gr, SUBLANES - s, 0), pltpu.roll(gi, SUBLANES - s, 0))
            crb = jnp.broadcast_to(cr, gr.shape)
            cib = jnp.broadcast_to(ci, gi.shape)
            gr, gi = _cmul_add(gr, gi, tab_r[11], tab_r[15], crb, cib)
            gr_s[rows, :] = gr
            gi_s[rows, :] = gi
            gnr = jnp.where(last_row, crb, pltpu.roll(gr, SUBLANES - 1, 0))
            gni = jnp.where(last_row, cib, pltpu.roll(gi, SUBLANES - 1, 0))
            xr = xsr_r[rows, :]
            xi = xsi_r[rows, :]
            ar = ar + (xr * gnr + xi * gni)
            ai = ai + (xr * gni - xi * gnr)
            return gr[0:1, :], gi[0:1, :], ar, ai

        zero = jnp.zeros((SUBLANES, STRIP), F32)
        cr, ci, ar, ai = lax.fori_loop(0, nb, block, (car_r[...], car_i[...], zero, zero))
        car_r[...] = cr
        car_i[...] = ci
        glbr_r[...] += ar
        glbi_r[...] += ai
        gr = gr_s[...]
        gi = gi_s[...]
        du = _dot_t(gr, bbr_r[...]) + _dot_t(gi, bbi_r[...]) + d_r[...] * dy
        du_r[...] = du.astype(BF16)
        dus_r[...] += jnp.sum(du, axis=0, keepdims=True)
        gd_r[...] += jnp.sum(dy * u, axis=0, keepdims=True)
        gbbr_r[...] += _tdot(u, gr)
        gbbi_r[...] += _tdot(u, gi)
        gctr_r[...] += _tdot(dy, xsr_r[...])
        gcti_r[...] -= _tdot(dy, xsi_r[...])

    def rev(w):
        return pl.BlockSpec((tc, w), lambda j, k: (nk - 1 - k, j))

    strip_mat = pl.BlockSpec((128, STRIP), lambda j, k: (j, 0))
    vec = pl.BlockSpec((1, 128), lambda j, k: (0, j))
    lbacc = pl.BlockSpec((SUBLANES, STRIP), lambda j, k: (0, j))
    sh = jax.ShapeDtypeStruct
    return _launch(
        body, comms, name="ssm_backward", grid=(N_STRIPS, nk),
        out_shape=(sh((t, IN_COLS), BF16), sh((1, SSM_W), F32),
                   sh((SSM_W, STRIP), F32), sh((SSM_W, STRIP), F32), sh((SSM_W, STRIP), F32), sh((SSM_W, STRIP), F32),
                   sh((SUBLANES, STATE_COLS), F32), sh((SUBLANES, STATE_COLS), F32), sh((1, SSM_W), F32)),
        in_specs=[rev(128), rev(128), rev(STRIP), rev(STRIP),
                  strip_mat, strip_mat, strip_mat, strip_mat, vec,
                  pl.BlockSpec((16, SUBLANES, STRIP), lambda j, k: (0, 0, j)), ANY],
        out_specs=(rev(128), vec, strip_mat, strip_mat, strip_mat, strip_mat, lbacc, lbacc, vec),
        scratch_shapes=[pltpu.VMEM((tc, STRIP), F32), pltpu.VMEM((tc, STRIP), F32),
                        pltpu.VMEM((1, STRIP), F32), pltpu.VMEM((1, STRIP), F32)],
        aliases={10: 0}, sem=("parallel", "arbitrary"),
    )(dya0, proj, xsr, xsi, bbr, bbi, ctr, cti, d_skip, tab, dproj)


def _input_grad(dr1, dproj, w_in_st, comms=(), tm=256):
    t = dr1.shape[0]

    def body(dr1_r, dp_r, w_r, dx_r):
        acc = ALPHA * dr1_r[...]
        for j in range(N_SHARDS):
            acc = acc + lax.dot_general(dp_r[:, D_MODEL * j:D_MODEL * (j + 1)], w_r[j],
                                        (((1,), (1,)), ((), ())), preferred_element_type=F32)
        dx_r[...] = acc

    (dx,), sent = _launch(
        body, comms, name="input_grad", grid=(t // tm,),
        out_shape=(jax.ShapeDtypeStruct((t, D_MODEL), F32),),
        in_specs=[pl.BlockSpec((tm, D_MODEL), lambda i: (i, 0)), pl.BlockSpec((tm, IN_COLS), lambda i: (i, 0)),
                  _const((N_SHARDS, D_MODEL, D_MODEL))],
        out_specs=(pl.BlockSpec((tm, D_MODEL), lambda i: (i, 0)),),
        sem=("parallel",),
    )(dr1, dproj, w_in_st)
    return dx, sent


def _in_weight_grad(x, dproj, comms=(), tk=512):
    t = x.shape[0]

    def body(x_r, dp_r, gw_r):
        @pl.when(pl.program_id(1) == 0)
        def _():
            gw_r[...] = jnp.zeros_like(gw_r)

        gw_r[0] += _tdot(x_r[...], dp_r[...])

    (g_w_in,), sent = _launch(
        body, comms, name="in_weight_grad", grid=(N_SHARDS, t // tk),
        out_shape=(jax.ShapeDtypeStruct((N_SHARDS, D_MODEL, D_MODEL), F32),),
        in_specs=[pl.BlockSpec((tk, D_MODEL), lambda j, k: (k, 0)), pl.BlockSpec((tk, D_MODEL), lambda j, k: (k, j))],
        out_specs=(pl.BlockSpec((1, D_MODEL, D_MODEL), lambda j, k: (j, 0, 0)),),
        sem=("parallel", "arbitrary"),
    )(x, dproj)
    return g_w_in, sent


MIXER_W = ("glu_w", "w_ssm_out", "w_conv_out", "w_o")
FFN_W = ("w_gate", "w_up", "w_down")


def _device_step(x, target, small, shards, c_arr, me_arr):
    lr, li = small["ssm_lambda_re"][0], small["ssm_lambda_im"][0]
    ldt = small["ssm_log_dt"][0][:, None]
    rep16 = lambda a: jnp.broadcast_to(a[:, None, :], (N_GROUPS, GROUP_C, a.shape[-1])).reshape(SSM_W, a.shape[-1])
    lr16, li16 = rep16(lr), rep16(li)
    ldt16 = rep16(jnp.broadcast_to(ldt, (N_GROUPS, N_STATE)))
    brt = small["ssm_b_re"][0].transpose(0, 2, 1).reshape(SSM_W, N_STATE)
    bit = small["ssm_b_im"][0].transpose(0, 2, 1).reshape(SSM_W, N_STATE)
    cre = small["ssm_c_re"][0].reshape(SSM_W, N_STATE)
    cim = small["ssm_c_im"][0].reshape(SSM_W, N_STATE)
    disc = (lr, li, ldt, lr16, li16, ldt16, brt, bit)

    pwr, pwi, bbr, bbi, ctr, cti = _ssm_prepare(*disc, cre, cim)
    tab = _scan_tables(pwr, pwi)

    mixer_sh = [shards[n] for n in MIXER_W]
    ffn_sh = [shards[n] for n in FFN_W]
    (w_in_st,) = _gather_weights([shards["w_in"]])
    proj, (arrived,) = _in_proj(x, w_in_st, small["b_in"], comms=[_gather_ici(mixer_sh, [shards["conv_w"]])])
    (xsr, xsi, ya0), (ffn_part, mixer_st) = _ssm_forward(
        proj, bbr, bbi, ctr, cti, small["ssm_d"], tab,
        comms=[_gather_ici(ffn_sh), _gather_d2d(arrived[:len(MIXER_W)], mixer_sh)])
    glu_st, wso_st, wco_st, wo_st = (_own_slot(st, sh) for st, sh in zip(mixer_st, mixer_sh))
    conv_st = _own_slot(arrived[len(MIXER_W)], shards["conv_w"])
    conv_w8 = jnp.pad(conv_st[:, :3, :].transpose(1, 0, 2).reshape(3, SSM_W), ((0, SUBLANES - 3), (0, 0)))
    w_o = wo_st.reshape(D_MODEL, D_MODEL)
    glu_w = glu_st.reshape(SSM_W, SSM_W)
    (xhat1, rstd1, ya, yb), (ffn_st,) = _mixer_forward(
        x, proj, ya0, glu_w, small["glu_b"], wso_st, conv_w8, wco_st, w_o, comms=[_gather_d2d(ffn_part, ffn_sh)])
    wg_st, wu_st, wd_st = (_own_slot(st, sh) for st, sh in zip(ffn_st, ffn_sh))
    (loss, dr1, x1b, dr2b, hid, dhg, dhu, g_ln2_g, g_ln2_b, g_ln1_g, g_ln1_b) = _ffn_step(
        xhat1, rstd1, target, small["ln1_g"], small["ln1_b"], small["ln2_g"], small["ln2_b"], wg_st, wu_st, wd_st)

    g_ffn = _ffn_weight_grads(x1b, dr2b, hid, dhg, dhu)
    (dya0, dproj, dbias, g_wo, g_wso, g_wco, g_glu_w, g_glu_b, g_conv8), (got_ffn,) = _mixer_backward(
        dr1, proj, ya0, ya, yb, glu_w, small["glu_b"], wso_st, conv_w8, wco_st, w_o, comms=[_swap_comm(g_ffn)])
    chip_ffn = [_add_own_half(g, r, c_arr) for g, r in zip(g_ffn, got_ffn)]
    g_mix = [g_glu_w.reshape(N_SHARDS, 128, SSM_W), g_wso, g_wco, g_wo.reshape(N_SHARDS, 256, D_MODEL)]
    (dproj, dus, gbbr, gbbi, gctr, gcti, glbr, glbi, g_d), (slots_ffn, got_mix) = _ssm_backward(
        dya0, proj, xsr, xsi, bbr, bbi, ctr, cti, small["ssm_d"], tab, dproj,
        comms=[_scatter_comm(chip_ffn), _swap_comm(g_mix)])
    halves_ffn = [_sum_chips(own, s, me_arr) for own, s in zip(chip_ffn, slots_ffn)]
    chip_mix = [_add_own_half(g, r, c_arr) for g, r in zip(g_mix, got_mix)]
    g_lr, g_li, g_ldt, g_brt, g_bit, g_cre, g_cim = _ssm_param_grads(
        *disc, glbr.reshape(SUBLANES, N_GROUPS, N_STATE), glbi.reshape(SUBLANES, N_GROUPS, N_STATE),
        gbbr, gbbi, gctr, gcti)
    g_w_in, (others_ffn, slots_mix) = _in_weight_grad(
        x, dproj, comms=[_send_comm(halves_ffn), _scatter_comm(chip_mix)])
    halves_mix = [_sum_chips(own, s, me_arr) for own, s in zip(chip_mix, slots_mix)]
    dx, _ = _input_grad(dr1, dproj, w_in_st)

    g_conv = jnp.pad(g_conv8[:3].reshape(3, N_SHARDS, 128).transpose(1, 0, 2), ((0, 0), (0, SUBLANES - 3), (0, 0)))
    pieces = [dus, dbias[:, SSM_W:], g_lr, g_li, g_ldt, g_brt, g_bit, g_cre, g_cim, g_d, g_glu_b,
              g_ln1_g, g_ln1_b, g_ln2_g, g_ln2_b, loss]
    flat = jnp.concatenate([p.reshape(-1) for p in pieces])
    g_packed = jnp.pad(flat, (0, PACKED_ROWS * 128 - flat.shape[0])).reshape(PACKED_ROWS, 128)
    ((got_w, got_conv, got_packed),) = _standalone([_swap_comm([g_w_in], [g_conv, g_packed])], "swap_with_sibling")
    chip_w = _add_own_half(g_w_in, got_w, c_arr)
    chip_conv, chip_packed = _small_pair_sums([g_conv, g_packed], [got_conv, got_packed])
    ((slots_w, slots_conv, slots_packed),) = _standalone(
        [_scatter_comm([chip_w, chip_conv], [chip_packed])], "scatter_to_chips")
    halves_w = _sum_chips(chip_w, slots_w, me_arr)
    conv_total, packed_total = _small_totals(me_arr, chip_conv, slots_conv, chip_packed, slots_packed)
    (others_rest,) = _standalone([_send_comm([halves_w] + halves_mix)], "send_to_sibling")

    pairs = dict(zip(FFN_W, zip(halves_ffn, others_ffn)))
    pairs.update(zip(("w_in",) + MIXER_W, zip([halves_w] + halves_mix, others_rest)))
    return dx, pairs, conv_total, packed_total


PACKED_ROWS = 1136
PACKED_LAYOUT = (("b_in", IN_COLS), ("ssm_lambda_re", STATE_COLS), ("ssm_lambda_im", STATE_COLS),
                 ("ssm_log_dt", N_GROUPS), ("ssm_b_re", SSM_W * N_STATE), ("ssm_b_im", SSM_W * N_STATE),
                 ("ssm_c_re", SSM_W * N_STATE), ("ssm_c_im", SSM_W * N_STATE), ("ssm_d", SSM_W), ("glu_b", SSM_W),
                 ("ln1_g", D_MODEL), ("ln1_b", D_MODEL), ("ln2_g", D_MODEL), ("ln2_b", D_MODEL), ("loss", 1))


def _unpack_small(packed):
    flat = packed.reshape(-1)
    out, off = {}, 0
    for name, size in PACKED_LAYOUT:
        out[name] = flat[off:off + size]
        off += size
    for name in ("ssm_b_re", "ssm_b_im"):
        out[name] = out[name].reshape(N_GROUPS, GROUP_C, N_STATE).transpose(0, 2, 1)[None]
    for name in ("ssm_c_re", "ssm_c_im"):
        out[name] = out[name].reshape(1, N_GROUPS, GROUP_C, N_STATE)
    for name in ("ssm_lambda_re", "ssm_lambda_im"):
        out[name] = out[name].reshape(1, N_GROUPS, N_STATE)
    for name in ("b_in", "ssm_log_dt", "ssm_d", "glu_b", "ln1_g", "ln1_b", "ln2_g", "ln2_b"):
        out[name] = out[name][None]
    return out


BIG = ("w_in", "glu_w", "w_ssm_out", "w_conv_out", "w_o", "w_gate", "w_up", "w_down")
SMALL = ("b_in", "ssm_lambda_re", "ssm_lambda_im", "ssm_log_dt", "ssm_b_re", "ssm_b_im", "ssm_c_re", "ssm_c_im",
         "ssm_d", "glu_b", "ln1_g", "ln1_b", "ln2_g", "ln2_b")
WEIGHTS = ("w_in", "b_in", "ssm_lambda_re", "ssm_lambda_im", "ssm_log_dt", "ssm_b_re", "ssm_b_im", "ssm_c_re",
           "ssm_c_im", "ssm_d", "glu_w", "glu_b", "w_ssm_out", "conv_w", "w_conv_out", "w_o", "ln1_g", "ln1_b",
           "w_gate", "w_up", "w_down", "ln2_g", "ln2_b")


def _place():
    x, y, c = lax.axis_index("x"), lax.axis_index("y"), lax.axis_index("c")
    chips = [(1 - x, y), (x, 1 - y), (1 - x, 1 - y)]
    return x, y, c, chips


def _shard_of(chip):
    return 2 * chip[0] + chip[1]


def _remote(src, dst, send_sem, recv_sem, to):
    return pltpu.make_async_remote_copy(src_ref=src, dst_ref=dst, send_sem=send_sem, recv_sem=recv_sem,
                                        device_id=to, device_id_type=MESH)


def _half_rows(shard, which):
    r2 = shard.shape[0] // 2
    return pl.ds(pl.multiple_of(which * r2, 16), r2)


def _own_slot(stack, shard):
    me = _shard_of((lax.axis_index("x"), lax.axis_index("y")))
    return lax.dynamic_update_slice(stack, shard[None], (me,) + (0,) * shard.ndim)


def _gather_ici(halved, whole=()):
    shards = list(halved) + list(whole)
    nh = len(halved)

    def copies(src, dst, sems):
        send_sem, recv_sem = sems
        x, y, c, chips = _place()
        me = _shard_of((x, y))
        out = []
        for a in range(len(shards)):
            for k, chip in enumerate(chips):
                if a < nh:
                    rows = _half_rows(shards[a], c)
                    out.append(_remote(src[a].at[rows], dst[a].at[me, rows], send_sem.at[a, k], recv_sem.at[a, k],
                                       (*chip, c)))
                else:
                    out.append(_remote(src[a], dst[a].at[me], send_sem.at[a, k], recv_sem.at[a, k], (*chip, c)))
        return out

    n = len(shards)
    return _Comm(shards, [jax.ShapeDtypeStruct((N_SHARDS,) + s.shape, s.dtype) for s in shards],
                 [pltpu.SemaphoreType.DMA((n, 3))] * 2, copies)


def _gather_d2d(stacks, shards):
    def copies(src, dst, sems):
        del src
        send_sem, recv_sem = sems
        x, y, c, chips = _place()
        out = []
        for a in range(len(stacks)):
            for k, chip in enumerate(chips):
                rows = dst[a].at[_shard_of(chip), _half_rows(shards[a], c)]
                out.append(_remote(rows, rows, send_sem.at[a, k], recv_sem.at[a, k], (x, y, 1 - c)))
        return out

    n = len(stacks)
    return _Comm(stacks, [jax.ShapeDtypeStruct(s.shape, s.dtype) for s in stacks],
                 [pltpu.SemaphoreType.DMA((n, 3))] * 2, copies, aliased=True)


def _standalone(comms, name):
    return _launch(None, comms, name=name, grid=(), in_specs=[], out_specs=(), out_shape=())()[1]


def _gather_weights(shards):
    n = len(shards)

    def body(*refs):
        src, dst = refs[:n], refs[n:2 * n]
        send_sem, recv_sem, fsend_sem, frecv_sem = refs[2 * n:]
        x, y, c, chips = _place()
        me = _shard_of((x, y))
        sibling = (x, y, 1 - c)
        sends = []
        for a in range(n):
            mine = _half_rows(shards[a], c)
            for k, chip in enumerate(chips):
                cp = _remote(src[a].at[mine], dst[a].at[me, mine], send_sem.at[a, k], recv_sem.at[a, k], (*chip, c))
                cp.start()
                sends.append(cp)
        for a in range(n):
            for k, chip in enumerate(chips):
                rows = dst[a].at[_shard_of(chip), _half_rows(shards[a], c)]
                _remote(rows, rows, send_sem.at[a, k], recv_sem.at[a, k], sibling).wait_recv()
                cp = _remote(rows, rows, fsend_sem.at[a, k], frecv_sem.at[a, k], sibling)
                cp.start()
                sends.append(cp)
        for a in range(n):
            for k, chip in enumerate(chips):
                rows = dst[a].at[_shard_of(chip), _half_rows(shards[a], 1 - c)]
                _remote(rows, rows, fsend_sem.at[a, k], frecv_sem.at[a, k], sibling).wait_recv()
        for cp in sends:
            cp.wait_send()

    stacks = pl.pallas_call(
        body, name="gather_weights",
        out_shape=tuple(jax.ShapeDtypeStruct((N_SHARDS,) + s.shape, s.dtype) for s in shards),
        in_specs=[ANY] * n, out_specs=(ANY,) * n,
        scratch_shapes=[pltpu.SemaphoreType.DMA((n, 3))] * 4,
    )(*shards)
    return [_own_slot(st, sh) for st, sh in zip(stacks, shards)]


def _swap_comm(big, small=()):
    nb, n = len(big), len(big) + len(small)
    arrays = list(big) + list(small)

    def copies(src, dst, sems):
        send_sem, recv_sem = sems
        x, y, c, _ = _place()
        out = []
        for a in range(n):
            if a < nb:
                r2 = arrays[a].shape[1] // 2
                part = src[a].at[:, pl.ds(pl.multiple_of((1 - c) * r2, SUBLANES), r2), :]
            else:
                part = src[a]
            out.append(_remote(part, dst[a], send_sem.at[a], recv_sem.at[a], (x, y, 1 - c)))
        return out

    out_shape = [jax.ShapeDtypeStruct((N_SHARDS, g.shape[1] // 2, g.shape[2]), g.dtype) for g in big]
    out_shape += [jax.ShapeDtypeStruct(g.shape, g.dtype) for g in small]
    return _Comm(arrays, out_shape, [pltpu.SemaphoreType.DMA((n,))] * 2, copies)


def _scatter_comm(slabbed, small=()):
    ns, n = len(slabbed), len(slabbed) + len(small)
    arrays = list(slabbed) + list(small)

    def copies(src, dst, sems):
        send_sem, recv_sem = sems
        _, _, c, chips = _place()
        out = []
        for a in range(n):
            for k, chip in enumerate(chips):
                part = src[a].at[_shard_of(chip)] if a < ns else src[a]
                out.append(_remote(part, dst[a].at[k], send_sem.at[a, k], recv_sem.at[a, k], (*chip, c)))
        return out

    out_shape = [jax.ShapeDtypeStruct((3,) + g.shape[1:], g.dtype) for g in slabbed]
    out_shape += [jax.ShapeDtypeStruct((3,) + g.shape, g.dtype) for g in small]
    return _Comm(arrays, out_shape, [pltpu.SemaphoreType.DMA((n, 3))] * 2, copies)


def _send_comm(arrays):
    n = len(arrays)

    def copies(src, dst, sems):
        send_sem, recv_sem = sems
        x, y, c, _ = _place()
        return [_remote(src[a], dst[a], send_sem.at[a], recv_sem.at[a], (x, y, 1 - c)) for a in range(n)]

    return _Comm(arrays, [jax.ShapeDtypeStruct(h.shape, h.dtype) for h in arrays],
                 [pltpu.SemaphoreType.DMA((n,))] * 2, copies)


def _row_chunk(rows):
    for cand in (256, 176, 128, 64):
        if rows % cand == 0:
            return cand
    return rows


def _add_own_half(stack, received, c):
    _, r2, cols = received.shape

    def body(c_ref, a_r, b_r, o_r):
        del c_ref
        o_r[...] = (a_r[...] + b_r[...]).astype(BF16)

    return pl.pallas_call(
        body, name="add_own_half",
        grid_spec=pltpu.PrefetchScalarGridSpec(
            num_scalar_prefetch=1, grid=(N_SHARDS,),
            in_specs=[pl.BlockSpec((1, r2, cols), lambda s, c_ref: (s, c_ref[0], 0)),
                      pl.BlockSpec((1, r2, cols), lambda s, c_ref: (s, 0, 0))],
            out_specs=pl.BlockSpec((1, r2, cols), lambda s, c_ref: (s, 0, 0))),
        out_shape=jax.ShapeDtypeStruct(received.shape, BF16),
        compiler_params=_params(("parallel",)),
    )(c, stack, received)


def _chip_order_sum(me, own, s0, s1, s2):
    terms = []
    for s in range(N_SHARDS):
        d = jnp.bitwise_xor(me, s)
        terms.append(jnp.where(d == 0, own, jnp.where(d == 2, s0, jnp.where(d == 1, s1, s2))))
    return ((terms[0] + terms[1]) + terms[2]) + terms[3]


def _sum_chips(own_stack, slots, me):
    _, rows, cols = slots.shape
    rc = _row_chunk(rows)

    def body(me_ref, own_r, s_r, o_r):
        o_r[...] = _chip_order_sum(me_ref[0], own_r[0].astype(F32), s_r[0].astype(F32), s_r[1].astype(F32),
                                   s_r[2].astype(F32))

    return pl.pallas_call(
        body, name="sum_chips",
        grid_spec=pltpu.PrefetchScalarGridSpec(
            num_scalar_prefetch=1, grid=(rows // rc,),
            in_specs=[pl.BlockSpec((1, rc, cols), lambda i, me_ref: (me_ref[0], i, 0)),
                      pl.BlockSpec((3, rc, cols), lambda i, me_ref: (0, i, 0))],
            out_specs=pl.BlockSpec((rc, cols), lambda i, me_ref: (i, 0))),
        out_shape=jax.ShapeDtypeStruct((rows, cols), F32),
        compiler_params=_params(("parallel",)),
    )(me, own_stack, slots)


def _small_pair_sums(mine, theirs):
    n = len(mine)

    def body(*refs):
        for a in range(n):
            refs[2 * n + a][...] = refs[a][...] + refs[n + a][...]

    vm = pl.BlockSpec(memory_space=pltpu.VMEM)
    return pl.pallas_call(
        body, name="small_pair_sums", out_shape=tuple(jax.ShapeDtypeStruct(g.shape, g.dtype) for g in mine),
        in_specs=[vm] * (2 * n), out_specs=(vm,) * n,
        compiler_params=pltpu.CompilerParams(vmem_limit_bytes=VMEM_LIMIT),
    )(*mine, *theirs)


def _adam_math(w, g, m, v):
    m = ADAM_B1 * m + (1.0 - ADAM_B1) * g
    v = ADAM_B2 * v + (1.0 - ADAM_B2) * (g * g)
    m_hat = m / (1.0 - ADAM_B1 ** ADAM_STEP)
    v_hat = v / (1.0 - ADAM_B2 ** ADAM_STEP)
    delta = -ADAM_LR * (m_hat / (jnp.sqrt(v_hat) + ADAM_EPS) + ADAM_WD * w)
    return delta, m, v


def _small_totals(me, conv_stack, conv_slots, packed, packed_slots):
    def body(me_ref, cs_r, cslot_r, p_r, pslot_r, conv_r, tot_r):
        me_ = me_ref[0]
        conv_r[...] = _chip_order_sum(me_, cs_r[me_], cslot_r[0], cslot_r[1], cslot_r[2])
        tot_r[...] = _chip_order_sum(me_, p_r[...], pslot_r[0], pslot_r[1], pslot_r[2])

    vm = pl.BlockSpec(memory_space=pltpu.VMEM)
    return pl.pallas_call(
        body, name="small_totals",
        out_shape=(jax.ShapeDtypeStruct(conv_stack.shape[1:], F32), jax.ShapeDtypeStruct(packed.shape, F32)),
        in_specs=[pl.BlockSpec(memory_space=pltpu.SMEM)] + [vm] * 4, out_specs=(vm, vm),
    )(me, conv_stack, conv_slots, packed, packed_slots)


def _adam_small(gs, ws, ms, vs):
    n = len(gs)

    def body(*refs):
        for a in range(n):
            g_r, w_r, m_r, v_r = (refs[i * n + a] for i in range(4))
            d_r, nm_r, nv_r = (refs[(4 + i) * n + a] for i in range(3))
            d_r[...], nm_r[...], nv_r[...] = _adam_math(w_r[...], g_r[...], m_r[...], v_r[...])

    vm = pl.BlockSpec(memory_space=pltpu.VMEM)
    shapes = tuple(jax.ShapeDtypeStruct(w.shape, F32) for w in ws)
    out = pl.pallas_call(
        body, name="adam_small", out_shape=shapes * 3, in_specs=[vm] * (4 * n), out_specs=(vm,) * (3 * n),
        compiler_params=pltpu.CompilerParams(vmem_limit_bytes=VMEM_LIMIT),
    )(*gs, *ws, *ms, *vs)
    return out[:n], out[n:2 * n], out[2 * n:]


def _adam_big(w, mine, other, m, v, c):
    r2, cols = mine.shape
    rc = _row_chunk(r2)
    nch = r2 // rc

    def body(c_ref, w_r, mine_r, other_r, m_r, v_r, g_r, d_r, nm_r, nv_r):
        g = jnp.where(pl.program_id(0) == c_ref[0], mine_r[...], other_r[...])
        g_r[...] = g
        d_r[...], nm_r[...], nv_r[...] = _adam_math(w_r[...], g, m_r[...], v_r[...])

    full = pl.BlockSpec((rc, cols), lambda h, i, c_ref: (h * nch + i, 0))
    half = pl.BlockSpec((rc, cols), lambda h, i, c_ref: (i, 0))
    shape = jax.ShapeDtypeStruct((2 * r2, cols), F32)
    return pl.pallas_call(
        body, name="adam_big",
        grid_spec=pltpu.PrefetchScalarGridSpec(
            num_scalar_prefetch=1, grid=(2, nch),
            in_specs=[full, half, half, full, full], out_specs=(full,) * 4),
        out_shape=(shape,) * 4, compiler_params=_params(("parallel", "parallel")),
    )(c, w.reshape(2 * r2, cols), mine, other, m.reshape(2 * r2, cols), v.reshape(2 * r2, cols))


def kernel(x, w_in, b_in, ssm_lambda_re, ssm_lambda_im, ssm_log_dt, ssm_b_re, ssm_b_im, ssm_c_re, ssm_c_im, ssm_d, glu_w, glu_b, w_ssm_out, conv_w, w_conv_out, w_o, ln1_g, ln1_b, w_gate, w_up, w_down, ln2_g, ln2_b, loss_target, m_w_in, m_b_in, m_ssm_lambda_re, m_ssm_lambda_im, m_ssm_log_dt, m_ssm_b_re, m_ssm_b_im, m_ssm_c_re, m_ssm_c_im, m_ssm_d, m_glu_w, m_glu_b, m_w_ssm_out, m_conv_w, m_w_conv_out, m_w_o, m_ln1_g, m_ln1_b, m_w_gate, m_w_up, m_w_down, m_ln2_g, m_ln2_b, v_w_in, v_b_in, v_ssm_lambda_re, v_ssm_lambda_im, v_ssm_log_dt, v_ssm_b_re, v_ssm_b_im, v_ssm_c_re, v_ssm_c_im, v_ssm_d, v_glu_w, v_glu_b, v_w_ssm_out, v_conv_w, v_w_conv_out, v_w_o, v_ln1_g, v_ln1_b, v_w_gate, v_w_up, v_w_down, v_ln2_g, v_ln2_b):
    given = dict(locals())
    w = {n: given[n] for n in WEIGHTS}
    m = {n: given["m_" + n] for n in WEIGHTS}
    v = {n: given["v_" + n] for n in WEIGHTS}

    flip = lambda n, a: a.T if n in ("w_gate", "w_up") else a
    shards = {n: flip(n, w[n][0]).astype(BF16) for n in BIG}
    shards["conv_w"] = jnp.pad(conv_w[0], ((0, SUBLANES - 3), (0, 0)))
    c_arr = jnp.reshape(lax.axis_index("c"), (1,)).astype(jnp.int32)
    me = _shard_of((lax.axis_index("x"), lax.axis_index("y")))
    me_arr = jnp.reshape(me, (1,)).astype(jnp.int32)
    dx, pairs, conv_total, packed_total = _device_step(
        x[0], loss_target[0], {n: w[n] for n in SMALL}, shards, c_arr, me_arr)

    grad = _unpack_small(packed_total)
    loss_total = grad.pop("loss")[0]
    grad["conv_w"] = conv_total[:3][None]
    small_names = ("conv_w",) + SMALL
    swap = lambda n, a: a.transpose(0, 1, 3, 2) if n in ("ssm_b_re", "ssm_b_im") else a
    ds, nms, nvs = _adam_small(*([swap(n, d[n]) for n in small_names] for d in (grad, w, m, v)))
    delta, new_m, new_v = {}, {}, {}
    for i, n in enumerate(small_names):
        delta[n], new_m[n], new_v[n] = swap(n, ds[i]), swap(n, nms[i]), swap(n, nvs[i])
    for n in BIG:
        mine, other = pairs[n]
        grad[n], delta[n], new_m[n], new_v[n] = (
            flip(n, r)[None] for r in _adam_big(flip(n, w[n][0]), mine, other, flip(n, m[n][0]), flip(n, v[n][0]),
                                                c_arr))

    return (loss_total, dx[None], *[grad[n] for n in WEIGHTS], *[delta[n] for n in WEIGHTS],
            *[new_m[n] for n in WEIGHTS], *[new_v[n] for n in WEIGHTS])
```

```python
import functools
import math

import jax
import jax.numpy as jnp
from jax import lax
from jax.experimental import pallas as pl
from jax.experimental.pallas import tpu as pltpu

F32 = jnp.float32
BF16 = jnp.bfloat16

D_MODEL = 1024
IN_COLS = 4096
SSM_W = 512
N_GROUPS = 32
N_STATE = 64
GROUP_C = 16
STATE_COLS = N_GROUPS * N_STATE
STRIP = 512
N_STRIPS = STATE_COLS // STRIP
FFN_SHARD = 704
N_SHARDS = 4
ALPHA = 2.0 ** 0.25
LN_EPS = 1e-5
GELU_K = math.sqrt(2.0 / math.pi)
GELU_C = 0.044715

ADAM_LR = 0.001
ADAM_B1 = 0.9
ADAM_B2 = 0.999
ADAM_EPS = 1e-08
ADAM_WD = 0.01
ADAM_STEP = 10

V7X_VMEM_BYTES = 64 * 1024 * 1024
VMEM_LIMIT = V7X_VMEM_BYTES - 8 * 1024 * 1024
SUBLANES = 8
N_POWERS = 64

MESH = pl.DeviceIdType.MESH
ANY = pl.BlockSpec(memory_space=pl.ANY)


def _dot(a, b):
    return jnp.dot(a.astype(BF16), b.astype(BF16), preferred_element_type=F32)


def _dot_t(a, b):
    return lax.dot_general(a.astype(BF16), b.astype(BF16), (((1,), (1,)), ((), ())),
                           preferred_element_type=F32)


def _tdot(a, b):
    return lax.dot_general(a.astype(BF16), b.astype(BF16), (((0,), (0,)), ((), ())),
                           preferred_element_type=F32)


def _sigmoid(v):
    return 1.0 / (1.0 + jnp.exp(-v))


def _split3(v):
    hi = v.astype(BF16)
    r1 = v - hi.astype(F32)
    mid = r1.astype(BF16)
    lo = (r1 - mid.astype(F32)).astype(BF16)
    return hi, mid, lo


def _exact_dot(v, sel):
    hi, mid, lo = _split3(v)
    return (jnp.dot(hi, sel, preferred_element_type=F32)
            + jnp.dot(mid, sel, preferred_element_type=F32)
            + jnp.dot(lo, sel, preferred_element_type=F32))


def _const(shape):
    nd = len(shape)
    return pl.BlockSpec(shape, lambda *_: (0,) * nd)


def _params(sem, vmem=VMEM_LIMIT):
    return pltpu.CompilerParams(dimension_semantics=sem, vmem_limit_bytes=vmem)


def _gelu_parts(v):
    inner = GELU_K * (v + GELU_C * v * v * v)
    t = jnp.tanh(inner)
    g = 0.5 * v * (1.0 + t)
    dg = 0.5 * (1.0 + t) + 0.5 * v * (1.0 - t * t) * GELU_K * (1.0 + 3.0 * GELU_C * v * v)
    return g, dg


class _Comm:
    def __init__(self, inputs, out_shape, sems, copies, aliased=False):
        self.inputs, self.out_shape, self.sems = list(inputs), tuple(out_shape), list(sems)
        self.copies, self.aliased = copies, aliased


def _launch(body, comms, *, name, grid, in_specs, out_specs, out_shape, scratch_shapes=(), aliases=None, sem=None):
    comms = list(comms)
    n_in, n_out, n_scr = len(in_specs), len(out_specs), len(scratch_shapes)
    aliases = dict(aliases or {})
    layout = []
    p_in, p_out, p_sem = n_in, n_out, 0
    for cm in comms:
        layout.append((p_in, p_out, p_sem))
        if cm.aliased:
            for i in range(len(cm.inputs)):
                aliases[p_in + i] = p_out + i
        p_in, p_out, p_sem = p_in + len(cm.inputs), p_out + len(cm.out_shape), p_sem + len(cm.sems)
    tot_in, tot_out = p_in, p_out

    def fused(*refs):
        ins, outs = refs[:tot_in], refs[tot_in:tot_in + tot_out]
        scr = refs[tot_in + tot_out:tot_in + tot_out + n_scr]
        sems = refs[tot_in + tot_out + n_scr:]

        def descriptors():
            out = []
            for cm, (a, b, s) in zip(comms, layout):
                out += cm.copies(ins[a:a + len(cm.inputs)], outs[b:b + len(cm.out_shape)], sems[s:s + len(cm.sems)])
            return out

        steps = [pl.program_id(d) for d in range(len(grid))]
        first = functools.reduce(jnp.logical_and, [s == 0 for s in steps]) if grid else None
        last = functools.reduce(jnp.logical_and, [s == g - 1 for s, g in zip(steps, grid)]) if grid else None

        def start():
            for cp in descriptors():
                cp.start()

        def finish():
            for cp in descriptors():
                cp.wait()

        if comms:
            pl.when(first)(start) if grid else start()
        if body is not None:
            body(*ins[:n_in], *outs[:n_out], *scr)
        if comms:
            pl.when(last)(finish) if grid else finish()

    specs_in = list(in_specs) + [ANY] * (tot_in - n_in)
    specs_out = tuple(out_specs) + (ANY,) * (tot_out - n_out)
    shapes = tuple(out_shape) + tuple(s for cm in comms for s in cm.out_shape)
    scratch = list(scratch_shapes) + [s for cm in comms for s in cm.sems]
    if comms or sem is None:
        sem = ("arbitrary",) * len(grid)
    kwargs = dict(grid=grid) if grid else {}
    call = pl.pallas_call(fused, name=name, out_shape=shapes, in_specs=specs_in, out_specs=specs_out,
                          scratch_shapes=scratch, input_output_aliases=aliases,
                          compiler_params=_params(sem) if grid else None, **kwargs)

    def run(*args):
        out = call(*args, *(a for cm in comms for a in cm.inputs))
        results, rest = out[:n_out], out[n_out:]
        per_comm = []
        for cm in comms:
            per_comm.append(rest[:len(cm.out_shape)])
            rest = rest[len(cm.out_shape):]
        return results, per_comm

    return run


def _ssm_discretise(lr, li, ldt, lr16, li16, ldt16, brt, bit):
    def lam_bar(lr_, li_, ldt_):
        dt = jnp.exp(ldt_)
        mag = jnp.exp(lr_ * dt)
        return mag * jnp.cos(li_ * dt), mag * jnp.sin(li_ * dt)

    lb_re, lb_im = lam_bar(lr, li, ldt)
    l16_re, l16_im = lam_bar(lr16, li16, ldt16)
    den = lr16 * lr16 + li16 * li16
    num_re = l16_re - 1.0
    fr = (num_re * lr16 + l16_im * li16) / den
    fi = (l16_im * lr16 - num_re * li16) / den
    bb_re = fr * brt - fi * bit
    bb_im = fr * bit + fi * brt
    return lb_re, lb_im, bb_re, bb_im


def _strip_selectors():
    p = lax.broadcasted_iota(jnp.int32, (N_STATE, STRIP), 0)
    col = lax.broadcasted_iota(jnp.int32, (N_STATE, STRIP), 1)
    rep = ((col & (N_STATE - 1)) == p).astype(BF16)
    row = lax.broadcasted_iota(jnp.int32, (SSM_W, STRIP), 0)
    col2 = lax.broadcasted_iota(jnp.int32, (SSM_W, STRIP), 1)
    mask = (((row >> 4) & 7) == (col2 >> 6))
    return rep, mask


def _ssm_prepare(lr, li, ldt, lr16, li16, ldt16, brt, bit, cre, cim):
    def body(lr_r, li_r, ldt_r, lr16_r, li16_r, ldt16_r, brt_r, bit_r, cre_r, cim_r,
             pwr_r, pwi_r, bbr_r, bbi_r, ctr_r, cti_r):
        lb_re, lb_im, bb_re, bb_im = _ssm_discretise(
            lr_r[...], li_r[...], ldt_r[...], lr16_r[...], li16_r[...], ldt16_r[...], brt_r[...], bit_r[...])
        pr, pi_ = lb_re, lb_im
        pwr_r[0] = pr
        pwi_r[0] = pi_
        for k in range(1, N_POWERS):
            pr, pi_ = pr * lb_re - pi_ * lb_im, pr * lb_im + pi_ * lb_re
            pwr_r[k] = pr
            pwi_r[k] = pi_
        rep, mask = _strip_selectors()
        for src, dst in ((bb_re, bbr_r), (bb_im, bbi_r), (cre_r[...], ctr_r), (cim_r[...], cti_r)):
            wide = jnp.dot(src.astype(BF16), rep, preferred_element_type=F32)
            dst[...] = jnp.where(mask, wide, 0.0).astype(BF16)

    vm = pl.BlockSpec(memory_space=pltpu.VMEM)
    return pl.pallas_call(
        body, name="ssm_prepare",
        out_shape=(jax.ShapeDtypeStruct((N_POWERS, N_GROUPS, N_STATE), F32),) * 2
        + (jax.ShapeDtypeStruct((SSM_W, STRIP), BF16),) * 4,
        in_specs=[vm] * 10, out_specs=(vm,) * 6,
    )(lr, li, ldt, lr16, li16, ldt16, brt, bit, cre, cim)


def _scan_tables(pwr, pwi):
    pr = pwr.reshape(N_POWERS, STATE_COLS)
    pi_ = pwi.reshape(N_POWERS, STATE_COLS)
    rows8 = lambda v: jnp.broadcast_to(v[None], (SUBLANES, STATE_COLS))
    tab_a = jnp.stack([rows8(pr[0]), rows8(pi_[0]), rows8(pr[-1]), rows8(pi_[-1])])
    tab_p = jnp.stack([jnp.repeat(pr, SUBLANES, axis=0), jnp.repeat(pi_, SUBLANES, axis=0)])
    return tab_a, tab_p


def _ssm_param_grads(lr, li, ldt, lr16, li16, ldt16, brt, bit, dlbr, dlbi, dbbr, dbbi, dctr, dcti):
    def body(lr_r, li_r, ldt_r, lr16_r, li16_r, ldt16_r, brt_r, bit_r,
             dlbr_r, dlbi_r, dbbr_r, dbbi_r, dctr_r, dcti_r,
             glr_r, gli_r, gldt_r, gbrt_r, gbit_r, gcre_r, gcim_r):
        rep, mask = _strip_selectors()

        def fold(acc):
            return sum(lax.dot_general(t, rep, (((1,), (1,)), ((), ())), preferred_element_type=F32)
                       for t in _split3(jnp.where(mask, acc, 0.0)))

        g_lb_re = jnp.sum(dlbr_r[...], axis=0)
        g_lb_im = jnp.sum(dlbi_r[...], axis=0)
        g_bb_re = fold(dbbr_r[...])
        g_bb_im = fold(dbbi_r[...])
        gcre_r[...] = fold(dctr_r[...])
        gcim_r[...] = fold(dcti_r[...])
        prim = (lr_r[...], li_r[...], ldt_r[...], lr16_r[...], li16_r[...], ldt16_r[...], brt_r[...], bit_r[...])
        _, vjp = jax.vjp(_ssm_discretise, *prim)
        g_lr, g_li, g_ldt, g_lr16, g_li16, g_ldt16, g_brt, g_bit = vjp((g_lb_re, g_lb_im, g_bb_re, g_bb_im))
        grp = lax.broadcasted_iota(jnp.int32, (N_GROUPS, SSM_W), 0)
        rw = lax.broadcasted_iota(jnp.int32, (N_GROUPS, SSM_W), 1)
        gsum = ((rw >> 4) == grp).astype(BF16)

        def group_sum(v):
            return sum(jnp.dot(gsum, t, preferred_element_type=F32) for t in _split3(v))

        glr_r[...] = g_lr + group_sum(g_lr16)
        gli_r[...] = g_li + group_sum(g_li16)
        gldt_r[...] = g_ldt + jnp.sum(group_sum(g_ldt16), axis=1, keepdims=True)
        gbrt_r[...] = g_brt
        gbit_r[...] = g_bit

    vm = pl.BlockSpec(memory_space=pltpu.VMEM)
    gp = jax.ShapeDtypeStruct((N_GROUPS, N_STATE), F32)
    gb = jax.ShapeDtypeStruct((SSM_W, N_STATE), F32)
    return pl.pallas_call(
        body, name="ssm_param_grads",
        out_shape=(gp, gp, jax.ShapeDtypeStruct((N_GROUPS, 1), F32), gb, gb, gb, gb),
        in_specs=[vm] * 14, out_specs=(vm,) * 7,
    )(lr, li, ldt, lr16, li16, ldt16, brt, bit, dlbr, dlbi, dbbr, dbbi, dctr, dcti)


def _in_proj(x, w_in_st, b_in, comms=()):
    t = x.shape[0]
    tm = 512

    def body(x_r, w_r, b_r, o_r):
        o_r[...] = _dot(x_r[...], w_r[0]) + b_r[...]

    (proj,), sent = _launch(
        body, comms, name="in_proj", grid=(t // tm, N_SHARDS),
        out_shape=(jax.ShapeDtypeStruct((t, IN_COLS), F32),),
        in_specs=[pl.BlockSpec((tm, D_MODEL), lambda i, j: (i, 0)),
                  pl.BlockSpec((1, D_MODEL, D_MODEL), lambda i, j: (j, 0, 0)),
                  pl.BlockSpec((1, D_MODEL), lambda i, j: (0, j))],
        out_specs=(pl.BlockSpec((tm, D_MODEL), lambda i, j: (i, j)),),
        sem=("parallel", "arbitrary"),
    )(x, w_in_st, b_in)
    return proj, sent


def _cmul_add(xr, xi, mr, mi, sr, si):
    return xr + (mr * sr - mi * si), xi + (mr * si + mi * sr)


SCAN_STEPS = N_POWERS
SCAN_CHUNK = SUBLANES * SCAN_STEPS


def _interleave(src_r, dst_r):
    for step in range(SCAN_STEPS):
        dst_r[SUBLANES * step:SUBLANES * (step + 1), :] = src_r[pl.ds(step, SUBLANES, stride=SCAN_STEPS), :]


def _deinterleave(src_r, dst_r):
    for step in range(SCAN_STEPS):
        dst_r[pl.ds(step, SUBLANES, stride=SCAN_STEPS), :] = src_r[SUBLANES * step:SUBLANES * (step + 1), :]


def _step_rows(step):
    return pl.ds(pl.multiple_of(step * SUBLANES, SUBLANES), SUBLANES)


def _segment_states(first_r, first_i, ends_r, ends_i, a64_r, a64_i, order):
    row = lax.broadcasted_iota(jnp.int32, ends_r.shape, 0)
    cur_r, cur_i = first_r, first_i
    ent_r = jnp.zeros_like(ends_r)
    ent_i = jnp.zeros_like(ends_i)
    for s in order:
        ent_r = jnp.where(row == s, jnp.broadcast_to(cur_r, ends_r.shape), ent_r)
        ent_i = jnp.where(row == s, jnp.broadcast_to(cur_i, ends_i.shape), ent_i)
        cur_r, cur_i = _cmul_add(ends_r[s:s + 1, :], ends_i[s:s + 1, :], a64_r, a64_i, cur_r, cur_i)
    return ent_r, ent_i, cur_r, cur_i


def _ssm_forward(proj, bbr, bbi, ctr, cti, d_skip, tab_a, tab_p, comms=(), tc=SCAN_CHUNK):
    t = proj.shape[0]

    def body(u_r, bbr_r, bbi_r, ctr_r, cti_r, d_r, ta_r, tp_r, xsr_r, xsi_r, y_r, ui_s, yi_s, car_r, car_i):
        @pl.when(pl.program_id(1) == 0)
        def _():
            car_r[...] = jnp.zeros_like(car_r)
            car_i[...] = jnp.zeros_like(car_i)

        _interleave(u_r, ui_s)
        u = ui_s[...]
        xsr_r[...] = _dot(u, bbr_r[...])
        xsi_r[...] = _dot(u, bbi_r[...])
        a_r, a_i = ta_r[0], ta_r[1]

        def local(step, carry):
            rows = _step_rows(step)
            xr, xi = _cmul_add(xsr_r[rows, :], xsi_r[rows, :], a_r, a_i, *carry)
            xsr_r[rows, :] = xr
            xsi_r[rows, :] = xi
            return xr, xi

        zero = jnp.zeros((SUBLANES, STRIP), F32)
        ends_r, ends_i = lax.fori_loop(0, SCAN_STEPS, local, (zero, zero), unroll=2)
        ent_r, ent_i, out_r, out_i = _segment_states(
            car_r[...], car_i[...], ends_r, ends_i, ta_r[2, 0:1, :], ta_r[3, 0:1, :], range(SUBLANES))
        car_r[...] = out_r
        car_i[...] = out_i

        def entering(step, _):
            rows = _step_rows(step)
            xr, xi = _cmul_add(xsr_r[rows, :], xsi_r[rows, :], tp_r[0, rows, :], tp_r[1, rows, :], ent_r, ent_i)
            xsr_r[rows, :] = xr
            xsi_r[rows, :] = xi
            return 0

        lax.fori_loop(0, SCAN_STEPS, entering, 0, unroll=4)
        yi_s[...] = _dot_t(xsr_r[...], ctr_r[...]) - _dot_t(xsi_r[...], cti_r[...]) + d_r[...] * u
        _deinterleave(yi_s, y_r)

    strip_mat = pl.BlockSpec((128, STRIP), lambda j, k: (j, 0))
    states = pl.BlockSpec((tc, STRIP), lambda j, k: (k, j))
    return _launch(
        body, comms, name="ssm_forward", grid=(N_STRIPS, t // tc),
        out_shape=(jax.ShapeDtypeStruct((t, STATE_COLS), F32), jax.ShapeDtypeStruct((t, STATE_COLS), F32),
                   jax.ShapeDtypeStruct((t, SSM_W), F32)),
        in_specs=[pl.BlockSpec((tc, 128), lambda j, k: (k, j)),
                  strip_mat, strip_mat, strip_mat, strip_mat,
                  pl.BlockSpec((1, 128), lambda j, k: (0, j)),
                  pl.BlockSpec((4, SUBLANES, STRIP), lambda j, k: (0, 0, j)),
                  pl.BlockSpec((2, tc, STRIP), lambda j, k: (0, 0, j))],
        out_specs=(states, states, pl.BlockSpec((tc, 128), lambda j, k: (k, j))),
        scratch_shapes=[pltpu.VMEM((tc, 128), F32), pltpu.VMEM((tc, 128), F32),
                        pltpu.VMEM((1, STRIP), F32), pltpu.VMEM((1, STRIP), F32)],
        sem=("parallel", "arbitrary"),
    )(proj, bbr, bbi, ctr, cti, d_skip, tab_a, tab_p)


def _shift_down(v, prev, n):
    row = lax.broadcasted_iota(jnp.int32, v.shape, 0)
    out = pltpu.roll(v, n, 0)
    for r in range(n):
        src = prev[SUBLANES - n + r:SUBLANES - n + r + 1, :]
        out = jnp.where(row == r, jnp.broadcast_to(src, v.shape), out)
    return out


def _shift_up(v, nxt, n):
    rows = v.shape[0]
    row = lax.broadcasted_iota(jnp.int32, v.shape, 0)
    out = pltpu.roll(v, rows - n, 0)
    for r in range(n):
        src = nxt[r:r + 1, :]
        out = jnp.where(row == rows - n + r, jnp.broadcast_to(src, v.shape), out)
    return out


def _conv3(q, q_prev, w):
    return w[2:3, :] * q + w[1:2, :] * _shift_down(q, q_prev, 1) + w[0:1, :] * _shift_down(q, q_prev, 2)


def _mixer_forward(x, proj, ya0, glu_w, glu_b, wso_st, conv_w8, wco_st, w_o, comms=(), tm=256):
    t = x.shape[0]
    hb = tm // SUBLANES

    def body(x_r, ya0_r, h_r, cg_r, bg_r, ga_r, gb_r, hp_r, cgp_r,
             glu_w_r, glu_b_r, wso_r, cw_r, wco_r, wo_r, xh_r, rstd_r, ya_r, yb_r):
        i = pl.program_id(0)
        g, _ = _gelu_parts(ya0_r[...])
        ya1 = g * _sigmoid(_dot(g, glu_w_r[...]) + glu_b_r[...])
        q = cg_r[...] * h_r[...]
        q_prev = jnp.where(i > 0, cgp_r[...] * hp_r[...], 0.0)
        yb0 = bg_r[...] * _conv3(q, q_prev, cw_r[...])
        for j in range(N_SHARDS):
            ya_r[:, 256 * j:256 * (j + 1)] = _dot(ya1, wso_r[j])
            yb_r[:, 256 * j:256 * (j + 1)] = _dot(yb0, wco_r[j])
        merged = _sigmoid(ga_r[...]) * ya_r[...] + _sigmoid(gb_r[...]) * yb_r[...]
        r1 = ALPHA * x_r[...] + _dot(merged, wo_r[...])
        mu = jnp.mean(r1, axis=-1, keepdims=True)
        cen = r1 - mu
        rstd = lax.rsqrt(jnp.mean(cen * cen, axis=-1, keepdims=True) + LN_EPS)
        xh_r[...] = cen * rstd
        rstd_r[...] = rstd

    def col(w, c):
        return pl.BlockSpec((tm, w), lambda i: (i, c))

    def prev(c):
        return pl.BlockSpec((SUBLANES, SSM_W), lambda i: (jnp.maximum(i * hb - 1, 0), c))

    return _launch(
        body, comms, name="mixer_forward", grid=(t // tm,),
        out_shape=(jax.ShapeDtypeStruct((t, D_MODEL), F32), jax.ShapeDtypeStruct((t, 1), F32),
                   jax.ShapeDtypeStruct((t, D_MODEL), F32), jax.ShapeDtypeStruct((t, D_MODEL), F32)),
        in_specs=[col(D_MODEL, 0), col(SSM_W, 0), col(SSM_W, 1), col(SSM_W, 2), col(SSM_W, 3),
                  col(D_MODEL, 2), col(D_MODEL, 3), prev(1), prev(2),
                  _const((SSM_W, SSM_W)), _const((1, SSM_W)), _const((N_SHARDS, SSM_W, 256)),
                  _const((SUBLANES, SSM_W)), _const((N_SHARDS, SSM_W, 256)), _const((D_MODEL, D_MODEL))],
        out_specs=(col(D_MODEL, 0), pl.BlockSpec((tm, 1), lambda i: (i, 0)), col(D_MODEL, 0), col(D_MODEL, 0)),
        sem=("parallel",),
    )(x, ya0, proj, proj, proj, proj, proj, proj, proj, glu_w, glu_b, wso_st, conv_w8, wco_st, w_o)


def _layer_norm_bwd(dxhat, xhat, rstd):
    m1 = jnp.mean(dxhat, axis=-1, keepdims=True)
    m2 = jnp.mean(dxhat * xhat, axis=-1, keepdims=True)
    return rstd * (dxhat - m1 - xhat * m2)


def _ffn_step(xhat1, rstd1, target, ln1_g, ln1_b, ln2_g, ln2_b, wg_st, wu_st, wd_st, tm=256):
    t = xhat1.shape[0]

    def body(xh_r, rstd_r, tgt_r, g1_r, b1_r, g2_r, b2_r, wg_r, wu_r, wd_r,
             loss_r, dr1_r, x1b_r, dr2b_r, hid_r, dhg_r, dhu_r, dg2_r, db2_r, dg1_r, db1_r,
             hg_s, hu_s):
        @pl.when(pl.program_id(0) == 0)
        def _():
            for r in (loss_r, dg2_r, db2_r, dg1_r, db1_r):
                r[...] = jnp.zeros_like(r)

        xhat1_v = xh_r[...]
        x1 = xhat1_v * g1_r[...] + b1_r[...]
        x1b = x1.astype(BF16)
        x1b_r[...] = x1b
        ffn = jnp.zeros((tm, D_MODEL), F32)
        for j in range(N_SHARDS):
            hg = lax.dot_general(x1b, wg_r[j], (((1,), (1,)), ((), ())), preferred_element_type=F32)
            hu = lax.dot_general(x1b, wu_r[j], (((1,), (1,)), ((), ())), preferred_element_type=F32)
            hg_s[j] = hg
            hu_s[j] = hu
            hid = (hg * _sigmoid(hg) * hu).astype(BF16)
            hid_r[j] = hid
            ffn = ffn + jnp.dot(hid, wd_r[j], preferred_element_type=F32)
        r2 = ALPHA * x1 + ffn
        mu = jnp.mean(r2, axis=-1, keepdims=True)
        cen = r2 - mu
        rstd2 = lax.rsqrt(jnp.mean(cen * cen, axis=-1, keepdims=True) + LN_EPS)
        xhat2 = cen * rstd2
        diff = (xhat2 * g2_r[...] + b2_r[...]) - tgt_r[...]
        loss_r[...] += 0.5 * jnp.sum(jnp.mean(diff * diff, axis=-1, keepdims=True), axis=0, keepdims=True)
        dy = diff * (1.0 / D_MODEL)
        dg2_r[...] += jnp.sum(dy * xhat2, axis=0, keepdims=True)
        db2_r[...] += jnp.sum(dy, axis=0, keepdims=True)
        dr2 = _layer_norm_bwd(dy * g2_r[...], xhat2, rstd2)
        dr2b = dr2.astype(BF16)
        dr2b_r[...] = dr2b
        dx1 = ALPHA * dr2
        for j in range(N_SHARDS):
            dhid = lax.dot_general(dr2b, wd_r[j], (((1,), (1,)), ((), ())), preferred_element_type=F32)
            hg = hg_s[j]
            hu = hu_s[j]
            sg = _sigmoid(hg)
            dhu = (dhid * (hg * sg)).astype(BF16)
            dhg = (dhid * hu * (sg * (1.0 + hg * (1.0 - sg)))).astype(BF16)
            dhg_r[j] = dhg
            dhu_r[j] = dhu
            dx1 = dx1 + jnp.dot(dhg, wg_r[j], preferred_element_type=F32)
            dx1 = dx1 + jnp.dot(dhu, wu_r[j], preferred_element_type=F32)
        dg1_r[...] += jnp.sum(dx1 * xhat1_v, axis=0, keepdims=True)
        db1_r[...] += jnp.sum(dx1, axis=0, keepdims=True)
        dr1_r[...] = _layer_norm_bwd(dx1 * g1_r[...], xhat1_v, rstd_r[...])

    tile = pl.BlockSpec((tm, D_MODEL), lambda i: (i, 0))
    hidden = pl.BlockSpec((N_SHARDS, tm, FFN_SHARD), lambda i: (0, i, 0))
    vec = _const((1, D_MODEL))
    hid_shape = jax.ShapeDtypeStruct((N_SHARDS, t, FFN_SHARD), BF16)
    vec_shape = jax.ShapeDtypeStruct((1, D_MODEL), F32)
    return pl.pallas_call(
        body, name="ffn_step", grid=(t // tm,),
        out_shape=(jax.ShapeDtypeStruct((1, 1), F32), jax.ShapeDtypeStruct((t, D_MODEL), F32),
                   jax.ShapeDtypeStruct((t, D_MODEL), BF16), jax.ShapeDtypeStruct((t, D_MODEL), BF16),
                   hid_shape, hid_shape, hid_shape, vec_shape, vec_shape, vec_shape, vec_shape),
        in_specs=[tile, pl.BlockSpec((tm, 1), lambda i: (i, 0)), tile, vec, vec, vec, vec,
                  _const((N_SHARDS, FFN_SHARD, D_MODEL)), _const((N_SHARDS, FFN_SHARD, D_MODEL)),
                  _const((N_SHARDS, FFN_SHARD, D_MODEL))],
        out_specs=(_const((1, 1)), tile, tile, tile, hidden, hidden, hidden, vec, vec, vec, vec),
        scratch_shapes=[pltpu.VMEM((N_SHARDS, tm, FFN_SHARD), F32), pltpu.VMEM((N_SHARDS, tm, FFN_SHARD), F32)],
        compiler_params=_params(("arbitrary",)),
    )(xhat1, rstd1, target, ln1_g, ln1_b, ln2_g, ln2_b, wg_st, wu_st, wd_st)


def _ffn_weight_grads(x1b, dr2b, hid, dhg, dhu, tk=512):
    t = x1b.shape[0]

    def body(x_r, dr_r, hid_r, dhg_r, dhu_r, gwg_r, gwu_r, gwd_r):
        @pl.when(pl.program_id(1) == 0)
        def _():
            for r in (gwg_r, gwu_r, gwd_r):
                r[...] = jnp.zeros_like(r)

        gwg_r[0] += _tdot(dhg_r[0], x_r[...])
        gwu_r[0] += _tdot(dhu_r[0], x_r[...])
        gwd_r[0] += _tdot(hid_r[0], dr_r[...])

    tile = pl.BlockSpec((tk, D_MODEL), lambda j, k: (k, 0))
    hidden = pl.BlockSpec((1, tk, FFN_SHARD), lambda j, k: (j, k, 0))
    row = pl.BlockSpec((1, FFN_SHARD, D_MODEL), lambda j, k: (j, 0, 0))
    return pl.pallas_call(
        body, name="ffn_weight_grads", grid=(N_SHARDS, t // tk),
        out_shape=(jax.ShapeDtypeStruct((N_SHARDS, FFN_SHARD, D_MODEL), F32),) * 3,
        in_specs=[tile, tile, hidden, hidden, hidden],
        out_specs=(row, row, row),
        compiler_params=_params(("parallel", "arbitrary")),
    )(x1b, dr2b, hid, dhg, dhu)


def _mixer_backward(dr1, proj, ya0, ya, yb, glu_w, glu_b, wso_st, conv_w8, wco_st, w_o, comms=(), tm=256):
    t = dr1.shape[0]
    hb = tm // SUBLANES
    last_block = t // SUBLANES - 1

    def body(dr1_r, dr1n_r, ya0_r, ya_r, yb_r, h_r, cg_r, bg_r, ga_r, gb_r, hp_r, cgp_r, bgn_r, gbn_r,
             glu_w_r, glu_b_r, wso_r, cw_r, wco_r, wo_r,
             dya0_r, dproj_r, dbias_r, gwo_r, gwso_r, gwco_r, gglu_w_r, gglu_b_r, gconv_r):
        i = pl.program_id(0)

        @pl.when(i == 0)
        def _():
            for r in (dbias_r, gwo_r, gwso_r, gwco_r, gglu_w_r, gglu_b_r, gconv_r):
                r[...] = jnp.zeros_like(r)

        dr1_v = dr1_r[...]
        dmerged = _dot_t(dr1_v, wo_r[...])
        sa = _sigmoid(ga_r[...])
        sb = _sigmoid(gb_r[...])
        ya_v = ya_r[...]
        yb_v = yb_r[...]
        gwo_r[...] += _tdot(sa * ya_v + sb * yb_v, dr1_v)
        dya = dmerged * sa
        dyb = dmerged * sb
        dga = dmerged * ya_v * (sa * (1.0 - sa))
        dgb = dmerged * yb_v * (sb * (1.0 - sb))

        g, gelu_grad = _gelu_parts(ya0_r[...])
        s1 = _sigmoid(_dot(g, glu_w_r[...]) + glu_b_r[...])
        ya1 = g * s1
        dya1 = jnp.zeros((tm, SSM_W), F32)
        for j in range(N_SHARDS):
            dya_j = dya[:, 256 * j:256 * (j + 1)]
            gwso_r[j] += _tdot(ya1, dya_j)
            dya1 = dya1 + _dot_t(dya_j, wso_r[j])
        dz1 = dya1 * g * (s1 * (1.0 - s1))
        gglu_b_r[...] += jnp.sum(dz1, axis=0, keepdims=True)
        gglu_w_r[...] += _tdot(g, dz1)
        dya0_r[...] = (dya1 * s1 + _dot_t(dz1, glu_w_r[...])) * gelu_grad

        cw = cw_r[...]
        h = h_r[...]
        cg = cg_r[...]
        bg = bg_r[...]
        q = cg * h
        q_prev = jnp.where(i > 0, cgp_r[...] * hp_r[...], 0.0)
        q1 = _shift_down(q, q_prev, 1)
        q2 = _shift_down(q, q_prev, 2)
        z = cw[2:3, :] * q + cw[1:2, :] * q1 + cw[0:1, :] * q2
        yb0 = bg * z
        dyb0 = jnp.zeros((tm, SSM_W), F32)
        for j in range(N_SHARDS):
            dyb_j = dyb[:, 256 * j:256 * (j + 1)]
            gwco_r[j] += _tdot(yb0, dyb_j)
            dyb0 = dyb0 + _dot_t(dyb_j, wco_r[j])
        dbg = dyb0 * z
        dz = dyb0 * bg
        dyb_n = _dot_t(dr1n_r[...], wo_r[...]) * _sigmoid(gbn_r[...])
        dyb0_n = jnp.zeros((SUBLANES, SSM_W), F32)
        for j in range(N_SHARDS):
            dyb0_n = dyb0_n + _dot_t(dyb_n[:, 256 * j:256 * (j + 1)], wco_r[j])
        dz_next = jnp.where(i < pl.num_programs(0) - 1, dyb0_n * bgn_r[...], 0.0)
        dq = cw[2:3, :] * dz + cw[1:2, :] * _shift_up(dz, dz_next, 1) + cw[0:1, :] * _shift_up(dz, dz_next, 2)
        gconv_r[0:1, :] += jnp.sum(dz * q2, axis=0, keepdims=True)
        gconv_r[1:2, :] += jnp.sum(dz * q1, axis=0, keepdims=True)
        gconv_r[2:3, :] += jnp.sum(dz * q, axis=0, keepdims=True)
        dh = dq * cg
        dcg = dq * h

        dproj_r[:, 0:512] = jnp.zeros((tm, SSM_W), BF16)
        pieces = ((512, dh), (1024, dcg), (1536, dbg), (2048, dga), (3072, dgb))
        for off, val in pieces:
            w = val.shape[1]
            dproj_r[:, off:off + w] = val.astype(BF16)
            dbias_r[:, off:off + w] += jnp.sum(val, axis=0, keepdims=True)

    def col(w, c):
        return pl.BlockSpec((tm, w), lambda i: (i, c))

    def prev(c):
        return pl.BlockSpec((SUBLANES, SSM_W), lambda i: (jnp.maximum(i * hb - 1, 0), c))

    def nxt(w, c):
        return pl.BlockSpec((SUBLANES, w), lambda i: (jnp.minimum((i + 1) * hb, last_block), c))

    sh = jax.ShapeDtypeStruct
    return _launch(
        body, comms, name="mixer_backward", grid=(t // tm,),
        out_shape=(sh((t, SSM_W), F32), sh((t, IN_COLS), BF16), sh((1, IN_COLS), F32),
                   sh((D_MODEL, D_MODEL), F32), sh((N_SHARDS, SSM_W, 256), F32), sh((N_SHARDS, SSM_W, 256), F32),
                   sh((SSM_W, SSM_W), F32), sh((1, SSM_W), F32), sh((SUBLANES, SSM_W), F32)),
        in_specs=[col(D_MODEL, 0), nxt(D_MODEL, 0), col(SSM_W, 0), col(D_MODEL, 0), col(D_MODEL, 0),
                  col(SSM_W, 1), col(SSM_W, 2), col(SSM_W, 3), col(D_MODEL, 2), col(D_MODEL, 3),
                  prev(1), prev(2), nxt(SSM_W, 3), nxt(D_MODEL, 3),
                  _const((SSM_W, SSM_W)), _const((1, SSM_W)), _const((N_SHARDS, SSM_W, 256)),
                  _const((SUBLANES, SSM_W)), _const((N_SHARDS, SSM_W, 256)), _const((D_MODEL, D_MODEL))],
        out_specs=(col(SSM_W, 0), col(IN_COLS, 0), _const((1, IN_COLS)),
                   _const((D_MODEL, D_MODEL)), _const((N_SHARDS, SSM_W, 256)), _const((N_SHARDS, SSM_W, 256)),
                   _const((SSM_W, SSM_W)), _const((1, SSM_W)), _const((SUBLANES, SSM_W))),
        sem=("arbitrary",),
    )(dr1, dr1, ya0, ya, yb, proj, proj, proj, proj, proj, proj, proj, proj, proj,
      glu_w, glu_b, wso_st, conv_w8, wco_st, w_o)


def _cmulc_add(xr, xi, mr, mi, sr, si):
    return xr + (mr * sr + mi * si), xi + (mr * si - mi * sr)


def _ssm_backward(dya0, proj, xsr, xsi, bbr, bbi, ctr, cti, d_skip, tab_a, tab_p, dproj, comms=(), tc=SCAN_CHUNK):
    t = proj.shape[0]
    nk = t // tc

    def body(dy_r, u_r, xsr_r, xsi_r, bbr_r, bbi_r, ctr_r, cti_r, d_r, ta_r, tp_r, dproj_any,
             du_r, dus_r, gbbr_r, gbbi_r, gctr_r, gcti_r, glbr_r, glbi_r, gd_r,
             gr_s, gi_s, dyi_s, ui_s, dui_s, dun_s, car_r, car_i):
        del dproj_any

        @pl.when(pl.program_id(1) == 0)
        def _():
            for r in (car_r, car_i, dus_r, gbbr_r, gbbi_r, gctr_r, gcti_r, glbr_r, glbi_r, gd_r):
                r[...] = jnp.zeros_like(r)

        _interleave(dy_r, dyi_s)
        _interleave(u_r, ui_s)
        dy = dyi_s[...]
        u = ui_s[...]
        gr_s[...] = _dot(dy, ctr_r[...])
        gi_s[...] = -_dot(dy, cti_r[...])
        a_r, a_i = ta_r[0], ta_r[1]

        def local(n, carry):
            rows = _step_rows(SCAN_STEPS - 1 - n)
            gr, gi = _cmulc_add(gr_s[rows, :], gi_s[rows, :], a_r, a_i, *carry)
            gr_s[rows, :] = gr
            gi_s[rows, :] = gi
            return gr, gi

        zero = jnp.zeros((SUBLANES, STRIP), F32)
        ends_r, ends_i = lax.fori_loop(0, SCAN_STEPS, local, (zero, zero), unroll=2)
        ent_r, ent_i, out_r, out_i = _segment_states(
            car_r[...], car_i[...], ends_r, ends_i, ta_r[2, 0:1, :], -ta_r[3, 0:1, :], range(SUBLANES - 1, -1, -1))
        car_r[...] = out_r
        car_i[...] = out_i

        def entering(n, carry):
            gnr, gni, ar, ai = carry
            rows = _step_rows(SCAN_STEPS - 1 - n)
            power = _step_rows(n)
            gr, gi = _cmulc_add(gr_s[rows, :], gi_s[rows, :], tp_r[0, power, :], tp_r[1, power, :], ent_r, ent_i)
            gr_s[rows, :] = gr
            gi_s[rows, :] = gi
            xr = xsr_r[rows, :]
            xi = xsi_r[rows, :]
            return gr, gi, ar + (xr * gnr + xi * gni), ai + (xr * gni - xi * gnr)

        _, _, ar, ai = lax.fori_loop(0, SCAN_STEPS, entering, (ent_r, ent_i, zero, zero), unroll=2)
        glbr_r[...] += ar
        glbi_r[...] += ai
        gr = gr_s[...]
        gi = gi_s[...]
        dui_s[...] = _dot_t(gr, bbr_r[...]) + _dot_t(gi, bbi_r[...]) + d_r[...] * dy
        _deinterleave(dui_s, dun_s)
        du = dun_s[...]
        du_r[...] = du.astype(BF16)
        dus_r[...] += jnp.sum(du, axis=0, keepdims=True)
        gd_r[...] += jnp.sum(dy * u, axis=0, keepdims=True)
        gbbr_r[...] += _tdot(u, gr)
        gbbi_r[...] += _tdot(u, gi)
        gctr_r[...] += _tdot(dy, xsr_r[...])
        gcti_r[...] -= _tdot(dy, xsi_r[...])

    def rev(w):
        return pl.BlockSpec((tc, w), lambda j, k: (nk - 1 - k, j))

    strip_mat = pl.BlockSpec((128, STRIP), lambda j, k: (j, 0))
    vec = pl.BlockSpec((1, 128), lambda j, k: (0, j))
    lbacc = pl.BlockSpec((SUBLANES, STRIP), lambda j, k: (0, j))
    sh = jax.ShapeDtypeStruct
    return _launch(
        body, comms, name="ssm_backward", grid=(N_STRIPS, nk),
        out_shape=(sh((t, IN_COLS), BF16), sh((1, SSM_W), F32),
                   sh((SSM_W, STRIP), F32), sh((SSM_W, STRIP), F32), sh((SSM_W, STRIP), F32), sh((SSM_W, STRIP), F32),
                   sh((SUBLANES, STATE_COLS), F32), sh((SUBLANES, STATE_COLS), F32), sh((1, SSM_W), F32)),
        in_specs=[rev(128), rev(128), rev(STRIP), rev(STRIP),
                  strip_mat, strip_mat, strip_mat, strip_mat, vec,
                  pl.BlockSpec((4, SUBLANES, STRIP), lambda j, k: (0, 0, j)),
                  pl.BlockSpec((2, tc, STRIP), lambda j, k: (0, 0, j)), ANY],
        out_specs=(rev(128), vec, strip_mat, strip_mat, strip_mat, strip_mat, lbacc, lbacc, vec),
        scratch_shapes=[pltpu.VMEM((tc, STRIP), F32), pltpu.VMEM((tc, STRIP), F32)]
        + [pltpu.VMEM((tc, 128), F32)] * 4 + [pltpu.VMEM((1, STRIP), F32)] * 2,
        aliases={11: 0}, sem=("parallel", "arbitrary"),
    )(dya0, proj, xsr, xsi, bbr, bbi, ctr, cti, d_skip, tab_a, tab_p, dproj)


def _input_grad(dr1, dproj, w_in_st, comms=(), tm=256):
    t = dr1.shape[0]

    def body(dr1_r, dp_r, w_r, dx_r):
        acc = ALPHA * dr1_r[...]
        for j in range(N_SHARDS):
            acc = acc + lax.dot_general(dp_r[:, D_MODEL * j:D_MODEL * (j + 1)], w_r[j],
                                        (((1,), (1,)), ((), ())), preferred_element_type=F32)
        dx_r[...] = acc

    (dx,), sent = _launch(
        body, comms, name="input_grad", grid=(t // tm,),
        out_shape=(jax.ShapeDtypeStruct((t, D_MODEL), F32),),
        in_specs=[pl.BlockSpec((tm, D_MODEL), lambda i: (i, 0)), pl.BlockSpec((tm, IN_COLS), lambda i: (i, 0)),
                  _const((N_SHARDS, D_MODEL, D_MODEL))],
        out_specs=(pl.BlockSpec((tm, D_MODEL), lambda i: (i, 0)),),
        sem=("parallel",),
    )(dr1, dproj, w_in_st)
    return dx, sent


def _in_weight_grad(x, dproj, comms=(), tk=512):
    t = x.shape[0]

    def body(x_r, dp_r, gw_r):
        @pl.when(pl.program_id(1) == 0)
        def _():
            gw_r[...] = jnp.zeros_like(gw_r)

        gw_r[0] += _tdot(x_r[...], dp_r[...])

    (g_w_in,), sent = _launch(
        body, comms, name="in_weight_grad", grid=(N_SHARDS, t // tk),
        out_shape=(jax.ShapeDtypeStruct((N_SHARDS, D_MODEL, D_MODEL), F32),),
        in_specs=[pl.BlockSpec((tk, D_MODEL), lambda j, k: (k, 0)), pl.BlockSpec((tk, D_MODEL), lambda j, k: (k, j))],
        out_specs=(pl.BlockSpec((1, D_MODEL, D_MODEL), lambda j, k: (j, 0, 0)),),
        sem=("parallel", "arbitrary"),
    )(x, dproj)
    return g_w_in, sent


MIXER_W = ("glu_w", "w_ssm_out", "w_conv_out", "w_o")
FFN_W = ("w_gate", "w_up", "w_down")


def _device_step(x, target, small, shards, c_arr, me_arr):
    lr, li = small["ssm_lambda_re"][0], small["ssm_lambda_im"][0]
    ldt = small["ssm_log_dt"][0][:, None]
    rep16 = lambda a: jnp.broadcast_to(a[:, None, :], (N_GROUPS, GROUP_C, a.shape[-1])).reshape(SSM_W, a.shape[-1])
    lr16, li16 = rep16(lr), rep16(li)
    ldt16 = rep16(jnp.broadcast_to(ldt, (N_GROUPS, N_STATE)))
    brt = small["ssm_b_re"][0].transpose(0, 2, 1).reshape(SSM_W, N_STATE)
    bit = small["ssm_b_im"][0].transpose(0, 2, 1).reshape(SSM_W, N_STATE)
    cre = small["ssm_c_re"][0].reshape(SSM_W, N_STATE)
    cim = small["ssm_c_im"][0].reshape(SSM_W, N_STATE)
    disc = (lr, li, ldt, lr16, li16, ldt16, brt, bit)

    pwr, pwi, bbr, bbi, ctr, cti = _ssm_prepare(*disc, cre, cim)
    tab_a, tab_p = _scan_tables(pwr, pwi)

    mixer_sh = [shards[n] for n in MIXER_W]
    ffn_sh = [shards[n] for n in FFN_W]
    (w_in_st,) = _gather_weights([shards["w_in"]])
    proj, (arrived,) = _in_proj(x, w_in_st, small["b_in"], comms=[_gather_ici(mixer_sh, [shards["conv_w"]])])
    (xsr, xsi, ya0), (ffn_part, mixer_st) = _ssm_forward(
        proj, bbr, bbi, ctr, cti, small["ssm_d"], tab_a, tab_p,
        comms=[_gather_ici(ffn_sh), _gather_d2d(arrived[:len(MIXER_W)], mixer_sh)])
    glu_st, wso_st, wco_st, wo_st = (_own_slot(st, sh) for st, sh in zip(mixer_st, mixer_sh))
    conv_st = _own_slot(arrived[len(MIXER_W)], shards["conv_w"])
    conv_w8 = jnp.pad(conv_st[:, :3, :].transpose(1, 0, 2).reshape(3, SSM_W), ((0, SUBLANES - 3), (0, 0)))
    w_o = wo_st.reshape(D_MODEL, D_MODEL)
    glu_w = glu_st.reshape(SSM_W, SSM_W)
    (xhat1, rstd1, ya, yb), (ffn_st,) = _mixer_forward(
        x, proj, ya0, glu_w, small["glu_b"], wso_st, conv_w8, wco_st, w_o, comms=[_gather_d2d(ffn_part, ffn_sh)])
    wg_st, wu_st, wd_st = (_own_slot(st, sh) for st, sh in zip(ffn_st, ffn_sh))
    (loss, dr1, x1b, dr2b, hid, dhg, dhu, g_ln2_g, g_ln2_b, g_ln1_g, g_ln1_b) = _ffn_step(
        xhat1, rstd1, target, small["ln1_g"], small["ln1_b"], small["ln2_g"], small["ln2_b"], wg_st, wu_st, wd_st)

    g_ffn = _ffn_weight_grads(x1b, dr2b, hid, dhg, dhu)
    (dya0, dproj, dbias, g_wo, g_wso, g_wco, g_glu_w, g_glu_b, g_conv8), (got_ffn,) = _mixer_backward(
        dr1, proj, ya0, ya, yb, glu_w, small["glu_b"], wso_st, conv_w8, wco_st, w_o, comms=[_swap_comm(g_ffn)])
    chip_ffn = [_add_own_half(g, r, c_arr) for g, r in zip(g_ffn, got_ffn)]
    g_mix = [g_glu_w.reshape(N_SHARDS, 128, SSM_W), g_wso, g_wco, g_wo.reshape(N_SHARDS, 256, D_MODEL)]
    (dproj, dus, gbbr, gbbi, gctr, gcti, glbr, glbi, g_d), (slots_ffn, got_mix) = _ssm_backward(
        dya0, proj, xsr, xsi, bbr, bbi, ctr, cti, small["ssm_d"], tab_a, tab_p, dproj,
        comms=[_scatter_comm(chip_ffn), _swap_comm(g_mix)])
    halves_ffn = [_sum_chips(own, s, me_arr) for own, s in zip(chip_ffn, slots_ffn)]
    chip_mix = [_add_own_half(g, r, c_arr) for g, r in zip(g_mix, got_mix)]
    g_lr, g_li, g_ldt, g_brt, g_bit, g_cre, g_cim = _ssm_param_grads(
        *disc, glbr.reshape(SUBLANES, N_GROUPS, N_STATE), glbi.reshape(SUBLANES, N_GROUPS, N_STATE),
        gbbr, gbbi, gctr, gcti)
    g_w_in, (others_ffn, slots_mix) = _in_weight_grad(
        x, dproj, comms=[_send_comm(halves_ffn), _scatter_comm(chip_mix)])
    halves_mix = [_sum_chips(own, s, me_arr) for own, s in zip(chip_mix, slots_mix)]
    dx, _ = _input_grad(dr1, dproj, w_in_st)

    g_conv = jnp.pad(g_conv8[:3].reshape(3, N_SHARDS, 128).transpose(1, 0, 2), ((0, 0), (0, SUBLANES - 3), (0, 0)))
    pieces = [dus, dbias[:, SSM_W:], g_lr, g_li, g_ldt, g_brt, g_bit, g_cre, g_cim, g_d, g_glu_b,
              g_ln1_g, g_ln1_b, g_ln2_g, g_ln2_b, loss]
    flat = jnp.concatenate([p.reshape(-1) for p in pieces])
    g_packed = jnp.pad(flat, (0, PACKED_ROWS * 128 - flat.shape[0])).reshape(PACKED_ROWS, 128)
    ((got_w, got_conv, got_packed),) = _standalone([_swap_comm([g_w_in], [g_conv, g_packed])], "swap_with_sibling")
    chip_w = _add_own_half(g_w_in, got_w, c_arr)
    chip_conv, chip_packed = _small_pair_sums([g_conv, g_packed], [got_conv, got_packed])
    ((slots_w, slots_conv, slots_packed),) = _standalone(
        [_scatter_comm([chip_w, chip_conv], [chip_packed])], "scatter_to_chips")
    halves_w = _sum_chips(chip_w, slots_w, me_arr)
    conv_total, packed_total = _small_totals(me_arr, chip_conv, slots_conv, chip_packed, slots_packed)
    (others_rest,) = _standalone([_send_comm([halves_w] + halves_mix)], "send_to_sibling")

    pairs = dict(zip(FFN_W, zip(halves_ffn, others_ffn)))
    pairs.update(zip(("w_in",) + MIXER_W, zip([halves_w] + halves_mix, others_rest)))
    return dx, pairs, conv_total, packed_total


PACKED_ROWS = 1136
PACKED_LAYOUT = (("b_in", IN_COLS), ("ssm_lambda_re", STATE_COLS), ("ssm_lambda_im", STATE_COLS),
                 ("ssm_log_dt", N_GROUPS), ("ssm_b_re", SSM_W * N_STATE), ("ssm_b_im", SSM_W * N_STATE),
                 ("ssm_c_re", SSM_W * N_STATE), ("ssm_c_im", SSM_W * N_STATE), ("ssm_d", SSM_W), ("glu_b", SSM_W),
                 ("ln1_g", D_MODEL), ("ln1_b", D_MODEL), ("ln2_g", D_MODEL), ("ln2_b", D_MODEL), ("loss", 1))


def _unpack_small(packed):
    flat = packed.reshape(-1)
    out, off = {}, 0
    for name, size in PACKED_LAYOUT:
        out[name] = flat[off:off + size]
        off += size
    for name in ("ssm_b_re", "ssm_b_im"):
        out[name] = out[name].reshape(N_GROUPS, GROUP_C, N_STATE).transpose(0, 2, 1)[None]
    for name in ("ssm_c_re", "ssm_c_im"):
        out[name] = out[name].reshape(1, N_GROUPS, GROUP_C, N_STATE)
    for name in ("ssm_lambda_re", "ssm_lambda_im"):
        out[name] = out[name].reshape(1, N_GROUPS, N_STATE)
    for name in ("b_in", "ssm_log_dt", "ssm_d", "glu_b", "ln1_g", "ln1_b", "ln2_g", "ln2_b"):
        out[name] = out[name][None]
    return out


BIG = ("w_in", "glu_w", "w_ssm_out", "w_conv_out", "w_o", "w_gate", "w_up", "w_down")
SMALL = ("b_in", "ssm_lambda_re", "ssm_lambda_im", "ssm_log_dt", "ssm_b_re", "ssm_b_im", "ssm_c_re", "ssm_c_im",
         "ssm_d", "glu_b", "ln1_g", "ln1_b", "ln2_g", "ln2_b")
WEIGHTS = ("w_in", "b_in", "ssm_lambda_re", "ssm_lambda_im", "ssm_log_dt", "ssm_b_re", "ssm_b_im", "ssm_c_re",
           "ssm_c_im", "ssm_d", "glu_w", "glu_b", "w_ssm_out", "conv_w", "w_conv_out", "w_o", "ln1_g", "ln1_b",
           "w_gate", "w_up", "w_down", "ln2_g", "ln2_b")


def _place():
    x, y, c = lax.axis_index("x"), lax.axis_index("y"), lax.axis_index("c")
    chips = [(1 - x, y), (x, 1 - y), (1 - x, 1 - y)]
    return x, y, c, chips


def _shard_of(chip):
    return 2 * chip[0] + chip[1]


def _remote(src, dst, send_sem, recv_sem, to):
    return pltpu.make_async_remote_copy(src_ref=src, dst_ref=dst, send_sem=send_sem, recv_sem=recv_sem,
                                        device_id=to, device_id_type=MESH)


def _half_rows(shard, which):
    r2 = shard.shape[0] // 2
    return pl.ds(pl.multiple_of(which * r2, 16), r2)


def _own_slot(stack, shard):
    me = _shard_of((lax.axis_index("x"), lax.axis_index("y")))
    return lax.dynamic_update_slice(stack, shard[None], (me,) + (0,) * shard.ndim)


def _gather_ici(halved, whole=()):
    shards = list(halved) + list(whole)
    nh = len(halved)

    def copies(src, dst, sems):
        send_sem, recv_sem = sems
        x, y, c, chips = _place()
        me = _shard_of((x, y))
        out = []
        for a in range(len(shards)):
            for k, chip in enumerate(chips):
                if a < nh:
                    rows = _half_rows(shards[a], c)
                    out.append(_remote(src[a].at[rows], dst[a].at[me, rows], send_sem.at[a, k], recv_sem.at[a, k],
                                       (*chip, c)))
                else:
                    out.append(_remote(src[a], dst[a].at[me], send_sem.at[a, k], recv_sem.at[a, k], (*chip, c)))
        return out

    n = len(shards)
    return _Comm(shards, [jax.ShapeDtypeStruct((N_SHARDS,) + s.shape, s.dtype) for s in shards],
                 [pltpu.SemaphoreType.DMA((n, 3))] * 2, copies)


def _gather_d2d(stacks, shards):
    def copies(src, dst, sems):
        del src
        send_sem, recv_sem = sems
        x, y, c, chips = _place()
        out = []
        for a in range(len(stacks)):
            for k, chip in enumerate(chips):
                rows = dst[a].at[_shard_of(chip), _half_rows(shards[a], c)]
                out.append(_remote(rows, rows, send_sem.at[a, k], recv_sem.at[a, k], (x, y, 1 - c)))
        return out

    n = len(stacks)
    return _Comm(stacks, [jax.ShapeDtypeStruct(s.shape, s.dtype) for s in stacks],
                 [pltpu.SemaphoreType.DMA((n, 3))] * 2, copies, aliased=True)


def _standalone(comms, name):
    return _launch(None, comms, name=name, grid=(), in_specs=[], out_specs=(), out_shape=())()[1]


def _gather_weights(shards):
    n = len(shards)

    def body(*refs):
        src, dst = refs[:n], refs[n:2 * n]
        send_sem, recv_sem, fsend_sem, frecv_sem = refs[2 * n:]
        x, y, c, chips = _place()
        me = _shard_of((x, y))
        sibling = (x, y, 1 - c)
        sends = []
        for a in range(n):
            mine = _half_rows(shards[a], c)
            for k, chip in enumerate(chips):
                cp = _remote(src[a].at[mine], dst[a].at[me, mine], send_sem.at[a, k], recv_sem.at[a, k], (*chip, c))
                cp.start()
                sends.append(cp)
        for a in range(n):
            for k, chip in enumerate(chips):
                rows = dst[a].at[_shard_of(chip), _half_rows(shards[a], c)]
                _remote(rows, rows, send_sem.at[a, k], recv_sem.at[a, k], sibling).wait_recv()
                cp = _remote(rows, rows, fsend_sem.at[a, k], frecv_sem.at[a, k], sibling)
                cp.start()
                sends.append(cp)
        for a in range(n):
            for k, chip in enumerate(chips):
                rows = dst[a].at[_shard_of(chip), _half_rows(shards[a], 1 - c)]
                _remote(rows, rows, fsend_sem.at[a, k], frecv_sem.at[a, k], sibling).wait_recv()
        for cp in sends:
            cp.wait_send()

    stacks = pl.pallas_call(
        body, name="gather_weights",
        out_shape=tuple(jax.ShapeDtypeStruct((N_SHARDS,) + s.shape, s.dtype) for s in shards),
        in_specs=[ANY] * n, out_specs=(ANY,) * n,
        scratch_shapes=[pltpu.SemaphoreType.DMA((n, 3))] * 4,
    )(*shards)
    return [_own_slot(st, sh) for st, sh in zip(stacks, shards)]


def _swap_comm(big, small=()):
    nb, n = len(big), len(big) + len(small)
    arrays = list(big) + list(small)

    def copies(src, dst, sems):
        send_sem, recv_sem = sems
        x, y, c, _ = _place()
        out = []
        for a in range(n):
            if a < nb:
                r2 = arrays[a].shape[1] // 2
                part = src[a].at[:, pl.ds(pl.multiple_of((1 - c) * r2, SUBLANES), r2), :]
            else:
                part = src[a]
            out.append(_remote(part, dst[a], send_sem.at[a], recv_sem.at[a], (x, y, 1 - c)))
        return out

    out_shape = [jax.ShapeDtypeStruct((N_SHARDS, g.shape[1] // 2, g.shape[2]), g.dtype) for g in big]
    out_shape += [jax.ShapeDtypeStruct(g.shape, g.dtype) for g in small]
    return _Comm(arrays, out_shape, [pltpu.SemaphoreType.DMA((n,))] * 2, copies)


def _scatter_comm(slabbed, small=()):
    ns, n = len(slabbed), len(slabbed) + len(small)
    arrays = list(slabbed) + list(small)

    def copies(src, dst, sems):
        send_sem, recv_sem = sems
        _, _, c, chips = _place()
        out = []
        for a in range(n):
            for k, chip in enumerate(chips):
                part = src[a].at[_shard_of(chip)] if a < ns else src[a]
                out.append(_remote(part, dst[a].at[k], send_sem.at[a, k], recv_sem.at[a, k], (*chip, c)))
        return out

    out_shape = [jax.ShapeDtypeStruct((3,) + g.shape[1:], g.dtype) for g in slabbed]
    out_shape += [jax.ShapeDtypeStruct((3,) + g.shape, g.dtype) for g in small]
    return _Comm(arrays, out_shape, [pltpu.SemaphoreType.DMA((n, 3))] * 2, copies)


def _send_comm(arrays):
    n = len(arrays)

    def copies(src, dst, sems):
        send_sem, recv_sem = sems
        x, y, c, _ = _place()
        return [_remote(src[a], dst[a], send_sem.at[a], recv_sem.at[a], (x, y, 1 - c)) for a in range(n)]

    return _Comm(arrays, [jax.ShapeDtypeStruct(h.shape, h.dtype) for h in arrays],
                 [pltpu.SemaphoreType.DMA((n,))] * 2, copies)


def _row_chunk(rows):
    for cand in (256, 176, 128, 64):
        if rows % cand == 0:
            return cand
    return rows


def _add_own_half(stack, received, c):
    _, r2, cols = received.shape

    def body(c_ref, a_r, b_r, o_r):
        del c_ref
        o_r[...] = (a_r[...] + b_r[...]).astype(BF16)

    return pl.pallas_call(
        body, name="add_own_half",
        grid_spec=pltpu.PrefetchScalarGridSpec(
            num_scalar_prefetch=1, grid=(N_SHARDS,),
            in_specs=[pl.BlockSpec((1, r2, cols), lambda s, c_ref: (s, c_ref[0], 0)),
                      pl.BlockSpec((1, r2, cols), lambda s, c_ref: (s, 0, 0))],
            out_specs=pl.BlockSpec((1, r2, cols), lambda s, c_ref: (s, 0, 0))),
        out_shape=jax.ShapeDtypeStruct(received.shape, BF16),
        compiler_params=_params(("parallel",)),
    )(c, stack, received)


def _chip_order_sum(me, own, s0, s1, s2):
    terms = []
    for s in range(N_SHARDS):
        d = jnp.bitwise_xor(me, s)
        terms.append(jnp.where(d == 0, own, jnp.where(d == 2, s0, jnp.where(d == 1, s1, s2))))
    return ((terms[0] + terms[1]) + terms[2]) + terms[3]


def _sum_chips(own_stack, slots, me):
    _, rows, cols = slots.shape
    rc = _row_chunk(rows)

    def body(me_ref, own_r, s_r, o_r):
        del me_ref
        o_r[...] = ((own_r[0].astype(F32) + s_r[0].astype(F32)) + s_r[1].astype(F32)) + s_r[2].astype(F32)

    return pl.pallas_call(
        body, name="sum_chips",
        grid_spec=pltpu.PrefetchScalarGridSpec(
            num_scalar_prefetch=1, grid=(rows // rc,),
            in_specs=[pl.BlockSpec((1, rc, cols), lambda i, me_ref: (me_ref[0], i, 0)),
                      pl.BlockSpec((3, rc, cols), lambda i, me_ref: (0, i, 0))],
            out_specs=pl.BlockSpec((rc, cols), lambda i, me_ref: (i, 0))),
        out_shape=jax.ShapeDtypeStruct((rows, cols), F32),
        compiler_params=_params(("parallel",)),
    )(me, own_stack, slots)


def _small_pair_sums(mine, theirs):
    n = len(mine)

    def body(*refs):
        for a in range(n):
            refs[2 * n + a][...] = refs[a][...] + refs[n + a][...]

    vm = pl.BlockSpec(memory_space=pltpu.VMEM)
    return pl.pallas_call(
        body, name="small_pair_sums", out_shape=tuple(jax.ShapeDtypeStruct(g.shape, g.dtype) for g in mine),
        in_specs=[vm] * (2 * n), out_specs=(vm,) * n,
        compiler_params=pltpu.CompilerParams(vmem_limit_bytes=VMEM_LIMIT),
    )(*mine, *theirs)


def _adam_math(w, g, m, v):
    m = ADAM_B1 * m + (1.0 - ADAM_B1) * g
    v = ADAM_B2 * v + (1.0 - ADAM_B2) * (g * g)
    m_hat = m / (1.0 - ADAM_B1 ** ADAM_STEP)
    v_hat = v / (1.0 - ADAM_B2 ** ADAM_STEP)
    delta = -ADAM_LR * (m_hat / (jnp.sqrt(v_hat) + ADAM_EPS) + ADAM_WD * w)
    return delta, m, v


def _small_totals(me, conv_stack, conv_slots, packed, packed_slots):
    def body(me_ref, cs_r, cslot_r, p_r, pslot_r, conv_r, tot_r):
        me_ = me_ref[0]
        conv_r[...] = _chip_order_sum(me_, cs_r[me_], cslot_r[0], cslot_r[1], cslot_r[2])
        tot_r[...] = _chip_order_sum(me_, p_r[...], pslot_r[0], pslot_r[1], pslot_r[2])

    vm = pl.BlockSpec(memory_space=pltpu.VMEM)
    return pl.pallas_call(
        body, name="small_totals",
        out_shape=(jax.ShapeDtypeStruct(conv_stack.shape[1:], F32), jax.ShapeDtypeStruct(packed.shape, F32)),
        in_specs=[pl.BlockSpec(memory_space=pltpu.SMEM)] + [vm] * 4, out_specs=(vm, vm),
    )(me, conv_stack, conv_slots, packed, packed_slots)


def _adam_small(gs, ws, ms, vs):
    n = len(gs)

    def body(*refs):
        for a in range(n):
            g_r, w_r, m_r, v_r = (refs[i * n + a] for i in range(4))
            d_r, nm_r, nv_r = (refs[(4 + i) * n + a] for i in range(3))
            d_r[...], nm_r[...], nv_r[...] = _adam_math(w_r[...], g_r[...], m_r[...], v_r[...])

    vm = pl.BlockSpec(memory_space=pltpu.VMEM)
    shapes = tuple(jax.ShapeDtypeStruct(w.shape, F32) for w in ws)
    out = pl.pallas_call(
        body, name="adam_small", out_shape=shapes * 3, in_specs=[vm] * (4 * n), out_specs=(vm,) * (3 * n),
        compiler_params=pltpu.CompilerParams(vmem_limit_bytes=VMEM_LIMIT),
    )(*gs, *ws, *ms, *vs)
    return out[:n], out[n:2 * n], out[2 * n:]


def _adam_big(w, mine, other, m, v, c):
    r2, cols = mine.shape
    rc = _row_chunk(r2)
    nch = r2 // rc

    def body(c_ref, w_r, mine_r, other_r, m_r, v_r, g_r, d_r, nm_r, nv_r):
        g = jnp.where(pl.program_id(0) == c_ref[0], mine_r[...], other_r[...])
        g_r[...] = g
        d_r[...], nm_r[...], nv_r[...] = _adam_math(w_r[...], g, m_r[...], v_r[...])

    full = pl.BlockSpec((rc, cols), lambda h, i, c_ref: (h * nch + i, 0))
    half = pl.BlockSpec((rc, cols), lambda h, i, c_ref: (i, 0))
    shape = jax.ShapeDtypeStruct((2 * r2, cols), F32)
    return pl.pallas_call(
        body, name="adam_big",
        grid_spec=pltpu.PrefetchScalarGridSpec(
            num_scalar_prefetch=1, grid=(2, nch),
            in_specs=[full, half, half, full, full], out_specs=(full,) * 4),
        out_shape=(shape,) * 4, compiler_params=_params(("parallel", "parallel")),
    )(c, w.reshape(2 * r2, cols), mine, other, m.reshape(2 * r2, cols), v.reshape(2 * r2, cols))


def kernel(x, w_in, b_in, ssm_lambda_re, ssm_lambda_im, ssm_log_dt, ssm_b_re, ssm_b_im, ssm_c_re, ssm_c_im, ssm_d, glu_w, glu_b, w_ssm_out, conv_w, w_conv_out, w_o, ln1_g, ln1_b, w_gate, w_up, w_down, ln2_g, ln2_b, loss_target, m_w_in, m_b_in, m_ssm_lambda_re, m_ssm_lambda_im, m_ssm_log_dt, m_ssm_b_re, m_ssm_b_im, m_ssm_c_re, m_ssm_c_im, m_ssm_d, m_glu_w, m_glu_b, m_w_ssm_out, m_conv_w, m_w_conv_out, m_w_o, m_ln1_g, m_ln1_b, m_w_gate, m_w_up, m_w_down, m_ln2_g, m_ln2_b, v_w_in, v_b_in, v_ssm_lambda_re, v_ssm_lambda_im, v_ssm_log_dt, v_ssm_b_re, v_ssm_b_im, v_ssm_c_re, v_ssm_c_im, v_ssm_d, v_glu_w, v_glu_b, v_w_ssm_out, v_conv_w, v_w_conv_out, v_w_o, v_ln1_g, v_ln1_b, v_w_gate, v_w_up, v_w_down, v_ln2_g, v_ln2_b):
    given = dict(locals())
    w = {n: given[n] for n in WEIGHTS}
    m = {n: given["m_" + n] for n in WEIGHTS}
    v = {n: given["v_" + n] for n in WEIGHTS}

    flip = lambda n, a: a.T if n in ("w_gate", "w_up") else a
    shards = {n: flip(n, w[n][0]).astype(BF16) for n in BIG}
    shards["conv_w"] = jnp.pad(conv_w[0], ((0, SUBLANES - 3), (0, 0)))
    c_arr = jnp.reshape(lax.axis_index("c"), (1,)).astype(jnp.int32)
    me = _shard_of((lax.axis_index("x"), lax.axis_index("y")))
    me_arr = jnp.reshape(me, (1,)).astype(jnp.int32)
    dx, pairs, conv_total, packed_total = _device_step(
        x[0], loss_target[0], {n: w[n] for n in SMALL}, shards, c_arr, me_arr)

    grad = _unpack_small(packed_total)
    loss_total = grad.pop("loss")[0]
    grad["conv_w"] = conv_total[:3][None]
    small_names = ("conv_w",) + SMALL
    swap = lambda n, a: a.transpose(0, 1, 3, 2) if n in ("ssm_b_re", "ssm_b_im") else a
    ds, nms, nvs = _adam_small(*([swap(n, d[n]) for n in small_names] for d in (grad, w, m, v)))
    delta, new_m, new_v = {}, {}, {}
    for i, n in enumerate(small_names):
        delta[n], new_m[n], new_v[n] = swap(n, ds[i]), swap(n, nms[i]), swap(n, nvs[i])
    for n in BIG:
        mine, other = pairs[n]
        grad[n], delta[n], new_m[n], new_v[n] = (
            flip(n, r)[None] for r in _adam_big(flip(n, w[n][0]), mine, other, flip(n, m[n][0]), flip(n, v[n][0]),
                                                c_arr))

    return (loss_total, dx[None], *[grad[n] for n in WEIGHTS], *[delta[n] for n in WEIGHTS],
            *[new_m[n] for n in WEIGHTS], *[new_v[n] for n in WEIGHTS])
```

```python
import functools
import math

import jax
import jax.numpy as jnp
from jax import lax
from jax.experimental import pallas as pl
from jax.experimental.pallas import tpu as pltpu

F32 = jnp.float32
BF16 = jnp.bfloat16

D_MODEL = 1024
IN_COLS = 4096
SSM_W = 512
N_GROUPS = 32
N_STATE = 64
GROUP_C = 16
STATE_COLS = N_GROUPS * N_STATE
STRIP = 512
N_STRIPS = STATE_COLS // STRIP
FFN_SHARD = 704
N_SHARDS = 4
ALPHA = 2.0 ** 0.25
LN_EPS = 1e-5
GELU_K = math.sqrt(2.0 / math.pi)
GELU_C = 0.044715

ADAM_LR = 0.001
ADAM_B1 = 0.9
ADAM_B2 = 0.999
ADAM_EPS = 1e-08
ADAM_WD = 0.01
ADAM_STEP = 10

V7X_VMEM_BYTES = 64 * 1024 * 1024
VMEM_LIMIT = V7X_VMEM_BYTES - 8 * 1024 * 1024
SUBLANES = 8
N_POWERS = 64

MESH = pl.DeviceIdType.MESH
ANY = pl.BlockSpec(memory_space=pl.ANY)


def _dot(a, b):
    return jnp.dot(a.astype(BF16), b.astype(BF16), preferred_element_type=F32)


def _dot_t(a, b):
    return lax.dot_general(a.astype(BF16), b.astype(BF16), (((1,), (1,)), ((), ())),
                           preferred_element_type=F32)


def _tdot(a, b):
    return lax.dot_general(a.astype(BF16), b.astype(BF16), (((0,), (0,)), ((), ())),
                           preferred_element_type=F32)


def _sigmoid(v):
    return 1.0 / (1.0 + jnp.exp(-v))


def _split3(v):
    hi = v.astype(BF16)
    r1 = v - hi.astype(F32)
    mid = r1.astype(BF16)
    lo = (r1 - mid.astype(F32)).astype(BF16)
    return hi, mid, lo


def _exact_dot(v, sel):
    hi, mid, lo = _split3(v)
    return (jnp.dot(hi, sel, preferred_element_type=F32)
            + jnp.dot(mid, sel, preferred_element_type=F32)
            + jnp.dot(lo, sel, preferred_element_type=F32))


def _const(shape):
    nd = len(shape)
    return pl.BlockSpec(shape, lambda *_: (0,) * nd)


def _params(sem, vmem=VMEM_LIMIT):
    return pltpu.CompilerParams(dimension_semantics=sem, vmem_limit_bytes=vmem)


def _gelu_parts(v):
    inner = GELU_K * (v + GELU_C * v * v * v)
    t = jnp.tanh(inner)
    g = 0.5 * v * (1.0 + t)
    dg = 0.5 * (1.0 + t) + 0.5 * v * (1.0 - t * t) * GELU_K * (1.0 + 3.0 * GELU_C * v * v)
    return g, dg


class _Comm:
    def __init__(self, inputs, out_shape, sems, copies, aliased=False):
        self.inputs, self.out_shape, self.sems = list(inputs), tuple(out_shape), list(sems)
        self.copies, self.aliased = copies, aliased


def _launch(body, comms, *, name, grid, in_specs, out_specs, out_shape, scratch_shapes=(), aliases=None, sem=None):
    comms = list(comms)
    n_in, n_out, n_scr = len(in_specs), len(out_specs), len(scratch_shapes)
    aliases = dict(aliases or {})
    layout = []
    p_in, p_out, p_sem = n_in, n_out, 0
    for cm in comms:
        layout.append((p_in, p_out, p_sem))
        if cm.aliased:
            for i in range(len(cm.inputs)):
                aliases[p_in + i] = p_out + i
        p_in, p_out, p_sem = p_in + len(cm.inputs), p_out + len(cm.out_shape), p_sem + len(cm.sems)
    tot_in, tot_out = p_in, p_out

    def fused(*refs):
        ins, outs = refs[:tot_in], refs[tot_in:tot_in + tot_out]
        scr = refs[tot_in + tot_out:tot_in + tot_out + n_scr]
        sems = refs[tot_in + tot_out + n_scr:]

        def descriptors():
            out = []
            for cm, (a, b, s) in zip(comms, layout):
                out += cm.copies(ins[a:a + len(cm.inputs)], outs[b:b + len(cm.out_shape)], sems[s:s + len(cm.sems)])
            return out

        steps = [pl.program_id(d) for d in range(len(grid))]
        first = functools.reduce(jnp.logical_and, [s == 0 for s in steps]) if grid else None
        last = functools.reduce(jnp.logical_and, [s == g - 1 for s, g in zip(steps, grid)]) if grid else None

        def start():
            for cp in descriptors():
                cp.start()

        def finish():
            for cp in descriptors():
                cp.wait()

        if comms:
            pl.when(first)(start) if grid else start()
        if body is not None:
            body(*ins[:n_in], *outs[:n_out], *scr)
        if comms:
            pl.when(last)(finish) if grid else finish()

    specs_in = list(in_specs) + [ANY] * (tot_in - n_in)
    specs_out = tuple(out_specs) + (ANY,) * (tot_out - n_out)
    shapes = tuple(out_shape) + tuple(s for cm in comms for s in cm.out_shape)
    scratch = list(scratch_shapes) + [s for cm in comms for s in cm.sems]
    if comms or sem is None:
        sem = ("arbitrary",) * len(grid)
    kwargs = dict(grid=grid) if grid else {}
    call = pl.pallas_call(fused, name=name, out_shape=shapes, in_specs=specs_in, out_specs=specs_out,
                          scratch_shapes=scratch, input_output_aliases=aliases,
                          compiler_params=_params(sem) if grid else None, **kwargs)

    def run(*args):
        out = call(*args, *(a for cm in comms for a in cm.inputs))
        results, rest = out[:n_out], out[n_out:]
        per_comm = []
        for cm in comms:
            per_comm.append(rest[:len(cm.out_shape)])
            rest = rest[len(cm.out_shape):]
        return results, per_comm

    return run


def _ssm_discretise(lr, li, ldt, lr16, li16, ldt16, brt, bit):
    def lam_bar(lr_, li_, ldt_):
        dt = jnp.exp(ldt_)
        mag = jnp.exp(lr_ * dt)
        return mag * jnp.cos(li_ * dt), mag * jnp.sin(li_ * dt)

    lb_re, lb_im = lam_bar(lr, li, ldt)
    l16_re, l16_im = lam_bar(lr16, li16, ldt16)
    den = lr16 * lr16 + li16 * li16
    num_re = l16_re - 1.0
    fr = (num_re * lr16 + l16_im * li16) / den
    fi = (l16_im * lr16 - num_re * li16) / den
    bb_re = fr * brt - fi * bit
    bb_im = fr * bit + fi * brt
    return lb_re, lb_im, bb_re, bb_im


def _strip_selectors():
    p = lax.broadcasted_iota(jnp.int32, (N_STATE, STRIP), 0)
    col = lax.broadcasted_iota(jnp.int32, (N_STATE, STRIP), 1)
    rep = ((col & (N_STATE - 1)) == p).astype(BF16)
    row = lax.broadcasted_iota(jnp.int32, (SSM_W, STRIP), 0)
    col2 = lax.broadcasted_iota(jnp.int32, (SSM_W, STRIP), 1)
    mask = (((row >> 4) & 7) == (col2 >> 6))
    return rep, mask


def _ssm_prepare(lr, li, ldt, lr16, li16, ldt16, brt, bit, cre, cim):
    def body(lr_r, li_r, ldt_r, lr16_r, li16_r, ldt16_r, brt_r, bit_r, cre_r, cim_r,
             pwr_r, pwi_r, bbr_r, bbi_r, ctr_r, cti_r):
        lb_re, lb_im, bb_re, bb_im = _ssm_discretise(
            lr_r[...], li_r[...], ldt_r[...], lr16_r[...], li16_r[...], ldt16_r[...], brt_r[...], bit_r[...])
        pr, pi_ = lb_re, lb_im
        pwr_r[0] = pr
        pwi_r[0] = pi_
        for k in range(1, N_POWERS):
            pr, pi_ = pr * lb_re - pi_ * lb_im, pr * lb_im + pi_ * lb_re
            pwr_r[k] = pr
            pwi_r[k] = pi_
        rep, mask = _strip_selectors()
        for src, dst in ((bb_re, bbr_r), (bb_im, bbi_r), (cre_r[...], ctr_r), (cim_r[...], cti_r)):
            wide = jnp.dot(src.astype(BF16), rep, preferred_element_type=F32)
            dst[...] = jnp.where(mask, wide, 0.0).astype(BF16)

    vm = pl.BlockSpec(memory_space=pltpu.VMEM)
    return pl.pallas_call(
        body, name="ssm_prepare",
        out_shape=(jax.ShapeDtypeStruct((N_POWERS, N_GROUPS, N_STATE), F32),) * 2
        + (jax.ShapeDtypeStruct((SSM_W, STRIP), BF16),) * 4,
        in_specs=[vm] * 10, out_specs=(vm,) * 6,
    )(lr, li, ldt, lr16, li16, ldt16, brt, bit, cre, cim)


def _scan_tables(pwr, pwi):
    pr = pwr.reshape(N_POWERS, STATE_COLS)
    pi_ = pwi.reshape(N_POWERS, STATE_COLS)
    rows8 = lambda v: jnp.broadcast_to(v[None], (SUBLANES, STATE_COLS))
    tab_a = jnp.stack([rows8(pr[0]), rows8(pi_[0]), rows8(pr[-1]), rows8(pi_[-1])])
    tab_p = jnp.stack([jnp.repeat(pr, SUBLANES, axis=0), jnp.repeat(pi_, SUBLANES, axis=0)])
    return tab_a, tab_p


def _ssm_param_grads(lr, li, ldt, lr16, li16, ldt16, brt, bit, dlbr, dlbi, dbbr, dbbi, dctr, dcti):
    def body(lr_r, li_r, ldt_r, lr16_r, li16_r, ldt16_r, brt_r, bit_r,
             dlbr_r, dlbi_r, dbbr_r, dbbi_r, dctr_r, dcti_r,
             glr_r, gli_r, gldt_r, gbrt_r, gbit_r, gcre_r, gcim_r):
        rep, mask = _strip_selectors()

        def fold(acc):
            return sum(lax.dot_general(t, rep, (((1,), (1,)), ((), ())), preferred_element_type=F32)
                       for t in _split3(jnp.where(mask, acc, 0.0)))

        g_lb_re = jnp.sum(dlbr_r[...], axis=0)
        g_lb_im = jnp.sum(dlbi_r[...], axis=0)
        g_bb_re = fold(dbbr_r[...])
        g_bb_im = fold(dbbi_r[...])
        gcre_r[...] = fold(dctr_r[...])
        gcim_r[...] = fold(dcti_r[...])
        prim = (lr_r[...], li_r[...], ldt_r[...], lr16_r[...], li16_r[...], ldt16_r[...], brt_r[...], bit_r[...])
        _, vjp = jax.vjp(_ssm_discretise, *prim)
        g_lr, g_li, g_ldt, g_lr16, g_li16, g_ldt16, g_brt, g_bit = vjp((g_lb_re, g_lb_im, g_bb_re, g_bb_im))
        grp = lax.broadcasted_iota(jnp.int32, (N_GROUPS, SSM_W), 0)
        rw = lax.broadcasted_iota(jnp.int32, (N_GROUPS, SSM_W), 1)
        gsum = ((rw >> 4) == grp).astype(BF16)

        def group_sum(v):
            return sum(jnp.dot(gsum, t, preferred_element_type=F32) for t in _split3(v))

        glr_r[...] = g_lr + group_sum(g_lr16)
        gli_r[...] = g_li + group_sum(g_li16)
        gldt_r[...] = g_ldt + jnp.sum(group_sum(g_ldt16), axis=1, keepdims=True)
        gbrt_r[...] = g_brt
        gbit_r[...] = g_bit

    vm = pl.BlockSpec(memory_space=pltpu.VMEM)
    gp = jax.ShapeDtypeStruct((N_GROUPS, N_STATE), F32)
    gb = jax.ShapeDtypeStruct((SSM_W, N_STATE), F32)
    return pl.pallas_call(
        body, name="ssm_param_grads",
        out_shape=(gp, gp, jax.ShapeDtypeStruct((N_GROUPS, 1), F32), gb, gb, gb, gb),
        in_specs=[vm] * 14, out_specs=(vm,) * 7,
    )(lr, li, ldt, lr16, li16, ldt16, brt, bit, dlbr, dlbi, dbbr, dbbi, dctr, dcti)


def _in_proj(x, w_in_st, b_in, comms=()):
    t = x.shape[0]
    tm = 512

    def body(x_r, w_r, b_r, o_r):
        xb = x_r[...].astype(BF16)
        for j in range(N_SHARDS):
            cols = slice(D_MODEL * j, D_MODEL * (j + 1))
            o_r[:, cols] = jnp.dot(xb, w_r[j], preferred_element_type=F32) + b_r[:, cols]

    (proj,), sent = _launch(
        body, comms, name="in_proj", grid=(t // tm,),
        out_shape=(jax.ShapeDtypeStruct((t, IN_COLS), F32),),
        in_specs=[pl.BlockSpec((tm, D_MODEL), lambda i: (i, 0)), _const((N_SHARDS, D_MODEL, D_MODEL)),
                  _const((1, IN_COLS))],
        out_specs=(pl.BlockSpec((tm, IN_COLS), lambda i: (i, 0)),),
        sem=("parallel",),
    )(x, w_in_st, b_in)
    return proj, sent


def _cmul_add(xr, xi, mr, mi, sr, si):
    return xr + (mr * sr - mi * si), xi + (mr * si + mi * sr)


SCAN_STEPS = N_POWERS
SCAN_CHUNK = SUBLANES * SCAN_STEPS


def _interleave(src_r, dst_r):
    for step in range(SCAN_STEPS):
        dst_r[SUBLANES * step:SUBLANES * (step + 1), :] = src_r[pl.ds(step, SUBLANES, stride=SCAN_STEPS), :]


def _deinterleave(src_r, dst_r):
    for step in range(SCAN_STEPS):
        dst_r[pl.ds(step, SUBLANES, stride=SCAN_STEPS), :] = src_r[SUBLANES * step:SUBLANES * (step + 1), :]


def _step_rows(step):
    return pl.ds(pl.multiple_of(step * SUBLANES, SUBLANES), SUBLANES)


def _segment_states(first_r, first_i, ends_r, ends_i, a64_r, a64_i, order):
    row = lax.broadcasted_iota(jnp.int32, ends_r.shape, 0)
    cur_r, cur_i = first_r, first_i
    ent_r = jnp.zeros_like(ends_r)
    ent_i = jnp.zeros_like(ends_i)
    for s in order:
        ent_r = jnp.where(row == s, jnp.broadcast_to(cur_r, ends_r.shape), ent_r)
        ent_i = jnp.where(row == s, jnp.broadcast_to(cur_i, ends_i.shape), ent_i)
        cur_r, cur_i = _cmul_add(ends_r[s:s + 1, :], ends_i[s:s + 1, :], a64_r, a64_i, cur_r, cur_i)
    return ent_r, ent_i, cur_r, cur_i


def _ssm_forward(proj, bbr, bbi, ctr, cti, d_skip, tab_a, tab_p, comms=(), tc=SCAN_CHUNK):
    t = proj.shape[0]

    def body(u_r, bbr_r, bbi_r, ctr_r, cti_r, d_r, ta_r, tp_r, xsr_r, xsi_r, y_r, ui_s, yi_s, car_r, car_i):
        @pl.when(pl.program_id(1) == 0)
        def _():
            car_r[...] = jnp.zeros_like(car_r)
            car_i[...] = jnp.zeros_like(car_i)

        _interleave(u_r, ui_s)
        u = ui_s[...]
        xsr_r[...] = _dot(u, bbr_r[...])
        xsi_r[...] = _dot(u, bbi_r[...])
        a_r, a_i = ta_r[0], ta_r[1]

        def local(step, carry):
            rows = _step_rows(step)
            xr, xi = _cmul_add(xsr_r[rows, :], xsi_r[rows, :], a_r, a_i, *carry)
            xsr_r[rows, :] = xr
            xsi_r[rows, :] = xi
            return xr, xi

        zero = jnp.zeros((SUBLANES, STRIP), F32)
        ends_r, ends_i = lax.fori_loop(0, SCAN_STEPS, local, (zero, zero), unroll=2)
        ent_r, ent_i, out_r, out_i = _segment_states(
            car_r[...], car_i[...], ends_r, ends_i, ta_r[2, 0:1, :], ta_r[3, 0:1, :], range(SUBLANES))
        car_r[...] = out_r
        car_i[...] = out_i

        def entering(step, _):
            rows = _step_rows(step)
            xr, xi = _cmul_add(xsr_r[rows, :], xsi_r[rows, :], tp_r[0, rows, :], tp_r[1, rows, :], ent_r, ent_i)
            xsr_r[rows, :] = xr
            xsi_r[rows, :] = xi
            return 0

        lax.fori_loop(0, SCAN_STEPS, entering, 0, unroll=4)
        yi_s[...] = _dot_t(xsr_r[...], ctr_r[...]) - _dot_t(xsi_r[...], cti_r[...]) + d_r[...] * u
        _deinterleave(yi_s, y_r)

    strip_mat = pl.BlockSpec((128, STRIP), lambda j, k: (j, 0))
    states = pl.BlockSpec((tc, STRIP), lambda j, k: (k, j))
    return _launch(
        body, comms, name="ssm_forward", grid=(N_STRIPS, t // tc),
        out_shape=(jax.ShapeDtypeStruct((t, STATE_COLS), F32), jax.ShapeDtypeStruct((t, STATE_COLS), F32),
                   jax.ShapeDtypeStruct((t, SSM_W), F32)),
        in_specs=[pl.BlockSpec((tc, 128), lambda j, k: (k, j)),
                  strip_mat, strip_mat, strip_mat, strip_mat,
                  pl.BlockSpec((1, 128), lambda j, k: (0, j)),
                  pl.BlockSpec((4, SUBLANES, STRIP), lambda j, k: (0, 0, j)),
                  pl.BlockSpec((2, tc, STRIP), lambda j, k: (0, 0, j))],
        out_specs=(states, states, pl.BlockSpec((tc, 128), lambda j, k: (k, j))),
        scratch_shapes=[pltpu.VMEM((tc, 128), F32), pltpu.VMEM((tc, 128), F32),
                        pltpu.VMEM((1, STRIP), F32), pltpu.VMEM((1, STRIP), F32)],
        sem=("parallel", "arbitrary"),
    )(proj, bbr, bbi, ctr, cti, d_skip, tab_a, tab_p)


def _shift_down(v, prev, n):
    row = lax.broadcasted_iota(jnp.int32, v.shape, 0)
    out = pltpu.roll(v, n, 0)
    for r in range(n):
        src = prev[SUBLANES - n + r:SUBLANES - n + r + 1, :]
        out = jnp.where(row == r, jnp.broadcast_to(src, v.shape), out)
    return out


def _shift_up(v, nxt, n):
    rows = v.shape[0]
    row = lax.broadcasted_iota(jnp.int32, v.shape, 0)
    out = pltpu.roll(v, rows - n, 0)
    for r in range(n):
        src = nxt[r:r + 1, :]
        out = jnp.where(row == rows - n + r, jnp.broadcast_to(src, v.shape), out)
    return out


def _conv3(q, q_prev, w):
    return w[2:3, :] * q + w[1:2, :] * _shift_down(q, q_prev, 1) + w[0:1, :] * _shift_down(q, q_prev, 2)


def _mixer_forward(x, proj, ya0, glu_w, glu_b, wso_st, conv_w8, wco_st, w_o, comms=(), tm=256):
    t = x.shape[0]
    hb = tm // SUBLANES

    def body(x_r, ya0_r, h_r, cg_r, bg_r, ga_r, gb_r, hp_r, cgp_r,
             glu_w_r, glu_b_r, wso_r, cw_r, wco_r, wo_r, xh_r, rstd_r, ya_r, yb_r):
        i = pl.program_id(0)
        g, _ = _gelu_parts(ya0_r[...])
        ya1 = g * _sigmoid(_dot(g, glu_w_r[...]) + glu_b_r[...])
        q = cg_r[...] * h_r[...]
        q_prev = jnp.where(i > 0, cgp_r[...] * hp_r[...], 0.0)
        yb0 = bg_r[...] * _conv3(q, q_prev, cw_r[...])
        for j in range(N_SHARDS):
            ya_r[:, 256 * j:256 * (j + 1)] = _dot(ya1, wso_r[j])
            yb_r[:, 256 * j:256 * (j + 1)] = _dot(yb0, wco_r[j])
        merged = _sigmoid(ga_r[...]) * ya_r[...] + _sigmoid(gb_r[...]) * yb_r[...]
        r1 = ALPHA * x_r[...] + _dot(merged, wo_r[...])
        mu = jnp.mean(r1, axis=-1, keepdims=True)
        cen = r1 - mu
        rstd = lax.rsqrt(jnp.mean(cen * cen, axis=-1, keepdims=True) + LN_EPS)
        xh_r[...] = cen * rstd
        rstd_r[...] = rstd

    def col(w, c):
        return pl.BlockSpec((tm, w), lambda i: (i, c))

    def prev(c):
        return pl.BlockSpec((SUBLANES, SSM_W), lambda i: (jnp.maximum(i * hb - 1, 0), c))

    return _launch(
        body, comms, name="mixer_forward", grid=(t // tm,),
        out_shape=(jax.ShapeDtypeStruct((t, D_MODEL), F32), jax.ShapeDtypeStruct((t, 1), F32),
                   jax.ShapeDtypeStruct((t, D_MODEL), F32), jax.ShapeDtypeStruct((t, D_MODEL), F32)),
        in_specs=[col(D_MODEL, 0), col(SSM_W, 0), col(SSM_W, 1), col(SSM_W, 2), col(SSM_W, 3),
                  col(D_MODEL, 2), col(D_MODEL, 3), prev(1), prev(2),
                  _const((SSM_W, SSM_W)), _const((1, SSM_W)), _const((N_SHARDS, SSM_W, 256)),
                  _const((SUBLANES, SSM_W)), _const((N_SHARDS, SSM_W, 256)), _const((D_MODEL, D_MODEL))],
        out_specs=(col(D_MODEL, 0), pl.BlockSpec((tm, 1), lambda i: (i, 0)), col(D_MODEL, 0), col(D_MODEL, 0)),
        sem=("parallel",),
    )(x, ya0, proj, proj, proj, proj, proj, proj, proj, glu_w, glu_b, wso_st, conv_w8, wco_st, w_o)


def _layer_norm_bwd(dxhat, xhat, rstd):
    m1 = jnp.mean(dxhat, axis=-1, keepdims=True)
    m2 = jnp.mean(dxhat * xhat, axis=-1, keepdims=True)
    return rstd * (dxhat - m1 - xhat * m2)


def _ffn_step(xhat1, rstd1, target, ln1_g, ln1_b, ln2_g, ln2_b, wg_st, wu_st, wd_st, tm=256):
    t = xhat1.shape[0]

    def body(xh_r, rstd_r, tgt_r, g1_r, b1_r, g2_r, b2_r, wg_r, wu_r, wd_r,
             loss_r, dr1_r, x1b_r, dr2b_r, hid_r, dhg_r, dhu_r, dg2_r, db2_r, dg1_r, db1_r,
             hg_s, hu_s):
        @pl.when(pl.program_id(0) == 0)
        def _():
            for r in (loss_r, dg2_r, db2_r, dg1_r, db1_r):
                r[...] = jnp.zeros_like(r)

        xhat1_v = xh_r[...]
        x1 = xhat1_v * g1_r[...] + b1_r[...]
        x1b = x1.astype(BF16)
        x1b_r[...] = x1b
        ffn = jnp.zeros((tm, D_MODEL), F32)
        for j in range(N_SHARDS):
            hg = lax.dot_general(x1b, wg_r[j], (((1,), (1,)), ((), ())), preferred_element_type=F32)
            hu = lax.dot_general(x1b, wu_r[j], (((1,), (1,)), ((), ())), preferred_element_type=F32)
            hg_s[j] = hg
            hu_s[j] = hu
            hid = (hg * _sigmoid(hg) * hu).astype(BF16)
            hid_r[j] = hid
            ffn = ffn + jnp.dot(hid, wd_r[j], preferred_element_type=F32)
        r2 = ALPHA * x1 + ffn
        mu = jnp.mean(r2, axis=-1, keepdims=True)
        cen = r2 - mu
        rstd2 = lax.rsqrt(jnp.mean(cen * cen, axis=-1, keepdims=True) + LN_EPS)
        xhat2 = cen * rstd2
        diff = (xhat2 * g2_r[...] + b2_r[...]) - tgt_r[...]
        loss_r[...] += 0.5 * jnp.sum(jnp.mean(diff * diff, axis=-1, keepdims=True), axis=0, keepdims=True)
        dy = diff * (1.0 / D_MODEL)
        dg2_r[...] += jnp.sum(dy * xhat2, axis=0, keepdims=True)
        db2_r[...] += jnp.sum(dy, axis=0, keepdims=True)
        dr2 = _layer_norm_bwd(dy * g2_r[...], xhat2, rstd2)
        dr2b = dr2.astype(BF16)
        dr2b_r[...] = dr2b
        dx1 = ALPHA * dr2
        for j in range(N_SHARDS):
            dhid = lax.dot_general(dr2b, wd_r[j], (((1,), (1,)), ((), ())), preferred_element_type=F32)
            hg = hg_s[j]
            hu = hu_s[j]
            sg = _sigmoid(hg)
            dhu = (dhid * (hg * sg)).astype(BF16)
            dhg = (dhid * hu * (sg * (1.0 + hg * (1.0 - sg)))).astype(BF16)
            dhg_r[j] = dhg
            dhu_r[j] = dhu
            dx1 = dx1 + jnp.dot(dhg, wg_r[j], preferred_element_type=F32)
            dx1 = dx1 + jnp.dot(dhu, wu_r[j], preferred_element_type=F32)
        dg1_r[...] += jnp.sum(dx1 * xhat1_v, axis=0, keepdims=True)
        db1_r[...] += jnp.sum(dx1, axis=0, keepdims=True)
        dr1_r[...] = _layer_norm_bwd(dx1 * g1_r[...], xhat1_v, rstd_r[...])

    tile = pl.BlockSpec((tm, D_MODEL), lambda i: (i, 0))
    hidden = pl.BlockSpec((N_SHARDS, tm, FFN_SHARD), lambda i: (0, i, 0))
    vec = _const((1, D_MODEL))
    hid_shape = jax.ShapeDtypeStruct((N_SHARDS, t, FFN_SHARD), BF16)
    vec_shape = jax.ShapeDtypeStruct((1, D_MODEL), F32)
    return pl.pallas_call(
        body, name="ffn_step", grid=(t // tm,),
        out_shape=(jax.ShapeDtypeStruct((1, 1), F32), jax.ShapeDtypeStruct((t, D_MODEL), F32),
                   jax.ShapeDtypeStruct((t, D_MODEL), BF16), jax.ShapeDtypeStruct((t, D_MODEL), BF16),
                   hid_shape, hid_shape, hid_shape, vec_shape, vec_shape, vec_shape, vec_shape),
        in_specs=[tile, pl.BlockSpec((tm, 1), lambda i: (i, 0)), tile, vec, vec, vec, vec,
                  _const((N_SHARDS, FFN_SHARD, D_MODEL)), _const((N_SHARDS, FFN_SHARD, D_MODEL)),
                  _const((N_SHARDS, FFN_SHARD, D_MODEL))],
        out_specs=(_const((1, 1)), tile, tile, tile, hidden, hidden, hidden, vec, vec, vec, vec),
        scratch_shapes=[pltpu.VMEM((N_SHARDS, tm, FFN_SHARD), F32), pltpu.VMEM((N_SHARDS, tm, FFN_SHARD), F32)],
        compiler_params=_params(("arbitrary",)),
    )(xhat1, rstd1, target, ln1_g, ln1_b, ln2_g, ln2_b, wg_st, wu_st, wd_st)


def _ffn_weight_grads(x1b, dr2b, hid, dhg, dhu, tk=512):
    t = x1b.shape[0]

    def body(x_r, dr_r, hid_r, dhg_r, dhu_r, gwg_r, gwu_r, gwd_r):
        @pl.when(pl.program_id(1) == 0)
        def _():
            for r in (gwg_r, gwu_r, gwd_r):
                r[...] = jnp.zeros_like(r)

        gwg_r[0] += _tdot(dhg_r[0], x_r[...])
        gwu_r[0] += _tdot(dhu_r[0], x_r[...])
        gwd_r[0] += _tdot(hid_r[0], dr_r[...])

    tile = pl.BlockSpec((tk, D_MODEL), lambda j, k: (k, 0))
    hidden = pl.BlockSpec((1, tk, FFN_SHARD), lambda j, k: (j, k, 0))
    row = pl.BlockSpec((1, FFN_SHARD, D_MODEL), lambda j, k: (j, 0, 0))
    return pl.pallas_call(
        body, name="ffn_weight_grads", grid=(N_SHARDS, t // tk),
        out_shape=(jax.ShapeDtypeStruct((N_SHARDS, FFN_SHARD, D_MODEL), F32),) * 3,
        in_specs=[tile, tile, hidden, hidden, hidden],
        out_specs=(row, row, row),
        compiler_params=_params(("parallel", "arbitrary")),
    )(x1b, dr2b, hid, dhg, dhu)


def _mixer_backward(dr1, proj, ya0, ya, yb, glu_w, glu_b, wso_st, conv_w8, wco_st, w_o, comms=(), tm=256):
    t = dr1.shape[0]
    hb = tm // SUBLANES
    last_block = t // SUBLANES - 1

    def body(dr1_r, dr1n_r, ya0_r, ya_r, yb_r, h_r, cg_r, bg_r, ga_r, gb_r, hp_r, cgp_r, bgn_r, gbn_r,
             glu_w_r, glu_b_r, wso_r, cw_r, wco_r, wo_r,
             dya0_r, dproj_r, dbias_r, gwo_r, gwso_r, gwco_r, gglu_w_r, gglu_b_r, gconv_r):
        i = pl.program_id(0)

        @pl.when(i == 0)
        def _():
            for r in (dbias_r, gwo_r, gwso_r, gwco_r, gglu_w_r, gglu_b_r, gconv_r):
                r[...] = jnp.zeros_like(r)

        dr1_v = dr1_r[...]
        dmerged = _dot_t(dr1_v, wo_r[...])
        sa = _sigmoid(ga_r[...])
        sb = _sigmoid(gb_r[...])
        ya_v = ya_r[...]
        yb_v = yb_r[...]
        gwo_r[...] += _tdot(sa * ya_v + sb * yb_v, dr1_v)
        dya = dmerged * sa
        dyb = dmerged * sb
        dga = dmerged * ya_v * (sa * (1.0 - sa))
        dgb = dmerged * yb_v * (sb * (1.0 - sb))

        g, gelu_grad = _gelu_parts(ya0_r[...])
        s1 = _sigmoid(_dot(g, glu_w_r[...]) + glu_b_r[...])
        ya1 = g * s1
        dya1 = jnp.zeros((tm, SSM_W), F32)
        for j in range(N_SHARDS):
            dya_j = dya[:, 256 * j:256 * (j + 1)]
            gwso_r[j] += _tdot(ya1, dya_j)
            dya1 = dya1 + _dot_t(dya_j, wso_r[j])
        dz1 = dya1 * g * (s1 * (1.0 - s1))
        gglu_b_r[...] += jnp.sum(dz1, axis=0, keepdims=True)
        gglu_w_r[...] += _tdot(g, dz1)
        dya0_r[...] = (dya1 * s1 + _dot_t(dz1, glu_w_r[...])) * gelu_grad

        cw = cw_r[...]
        h = h_r[...]
        cg = cg_r[...]
        bg = bg_r[...]
        q = cg * h
        q_prev = jnp.where(i > 0, cgp_r[...] * hp_r[...], 0.0)
        q1 = _shift_down(q, q_prev, 1)
        q2 = _shift_down(q, q_prev, 2)
        z = cw[2:3, :] * q + cw[1:2, :] * q1 + cw[0:1, :] * q2
        yb0 = bg * z
        dyb0 = jnp.zeros((tm, SSM_W), F32)
        for j in range(N_SHARDS):
            dyb_j = dyb[:, 256 * j:256 * (j + 1)]
            gwco_r[j] += _tdot(yb0, dyb_j)
            dyb0 = dyb0 + _dot_t(dyb_j, wco_r[j])
        dbg = dyb0 * z
        dz = dyb0 * bg
        dyb_n = _dot_t(dr1n_r[...], wo_r[...]) * _sigmoid(gbn_r[...])
        dyb0_n = jnp.zeros((SUBLANES, SSM_W), F32)
        for j in range(N_SHARDS):
            dyb0_n = dyb0_n + _dot_t(dyb_n[:, 256 * j:256 * (j + 1)], wco_r[j])
        dz_next = jnp.where(i < pl.num_programs(0) - 1, dyb0_n * bgn_r[...], 0.0)
        dq = cw[2:3, :] * dz + cw[1:2, :] * _shift_up(dz, dz_next, 1) + cw[0:1, :] * _shift_up(dz, dz_next, 2)
        gconv_r[0:1, :] += jnp.sum(dz * q2, axis=0, keepdims=True)
        gconv_r[1:2, :] += jnp.sum(dz * q1, axis=0, keepdims=True)
        gconv_r[2:3, :] += jnp.sum(dz * q, axis=0, keepdims=True)
        dh = dq * cg
        dcg = dq * h

        dproj_r[:, 0:512] = jnp.zeros((tm, SSM_W), BF16)
        pieces = ((512, dh), (1024, dcg), (1536, dbg), (2048, dga), (3072, dgb))
        for off, val in pieces:
            w = val.shape[1]
            dproj_r[:, off:off + w] = val.astype(BF16)
            dbias_r[:, off:off + w] += jnp.sum(val, axis=0, keepdims=True)

    def col(w, c):
        return pl.BlockSpec((tm, w), lambda i: (i, c))

    def prev(c):
        return pl.BlockSpec((SUBLANES, SSM_W), lambda i: (jnp.maximum(i * hb - 1, 0), c))

    def nxt(w, c):
        return pl.BlockSpec((SUBLANES, w), lambda i: (jnp.minimum((i + 1) * hb, last_block), c))

    sh = jax.ShapeDtypeStruct
    return _launch(
        body, comms, name="mixer_backward", grid=(t // tm,),
        out_shape=(sh((t, SSM_W), F32), sh((t, IN_COLS), BF16), sh((1, IN_COLS), F32),
                   sh((D_MODEL, D_MODEL), F32), sh((N_SHARDS, SSM_W, 256), F32), sh((N_SHARDS, SSM_W, 256), F32),
                   sh((SSM_W, SSM_W), F32), sh((1, SSM_W), F32), sh((SUBLANES, SSM_W), F32)),
        in_specs=[col(D_MODEL, 0), nxt(D_MODEL, 0), col(SSM_W, 0), col(D_MODEL, 0), col(D_MODEL, 0),
                  col(SSM_W, 1), col(SSM_W, 2), col(SSM_W, 3), col(D_MODEL, 2), col(D_MODEL, 3),
                  prev(1), prev(2), nxt(SSM_W, 3), nxt(D_MODEL, 3),
                  _const((SSM_W, SSM_W)), _const((1, SSM_W)), _const((N_SHARDS, SSM_W, 256)),
                  _const((SUBLANES, SSM_W)), _const((N_SHARDS, SSM_W, 256)), _const((D_MODEL, D_MODEL))],
        out_specs=(col(SSM_W, 0), col(IN_COLS, 0), _const((1, IN_COLS)),
                   _const((D_MODEL, D_MODEL)), _const((N_SHARDS, SSM_W, 256)), _const((N_SHARDS, SSM_W, 256)),
                   _const((SSM_W, SSM_W)), _const((1, SSM_W)), _const((SUBLANES, SSM_W))),
        sem=("arbitrary",),
    )(dr1, dr1, ya0, ya, yb, proj, proj, proj, proj, proj, proj, proj, proj, proj,
      glu_w, glu_b, wso_st, conv_w8, wco_st, w_o)


def _cmulc_add(xr, xi, mr, mi, sr, si):
    return xr + (mr * sr + mi * si), xi + (mr * si - mi * sr)


def _ssm_backward(dya0, proj, xsr, xsi, bbr, bbi, ctr, cti, d_skip, tab_a, tab_p, dproj, comms=(), tc=SCAN_CHUNK):
    t = proj.shape[0]
    nk = t // tc

    def body(dy_r, u_r, xsr_r, xsi_r, bbr_r, bbi_r, ctr_r, cti_r, d_r, ta_r, tp_r, dproj_any,
             du_r, dus_r, gbbr_r, gbbi_r, gctr_r, gcti_r, glbr_r, glbi_r, gd_r,
             gr_s, gi_s, dyi_s, ui_s, dui_s, dun_s, car_r, car_i):
        del dproj_any

        @pl.when(pl.program_id(1) == 0)
        def _():
            for r in (car_r, car_i, dus_r, gbbr_r, gbbi_r, gctr_r, gcti_r, glbr_r, glbi_r, gd_r):
                r[...] = jnp.zeros_like(r)

        _interleave(dy_r, dyi_s)
        _interleave(u_r, ui_s)
        dy = dyi_s[...]
        u = ui_s[...]
        gr_s[...] = _dot(dy, ctr_r[...])
        gi_s[...] = -_dot(dy, cti_r[...])
        a_r, a_i = ta_r[0], ta_r[1]

        def local(n, carry):
            rows = _step_rows(SCAN_STEPS - 1 - n)
            gr, gi = _cmulc_add(gr_s[rows, :], gi_s[rows, :], a_r, a_i, *carry)
            gr_s[rows, :] = gr
            gi_s[rows, :] = gi
            return gr, gi

        zero = jnp.zeros((SUBLANES, STRIP), F32)
        ends_r, ends_i = lax.fori_loop(0, SCAN_STEPS, local, (zero, zero), unroll=2)
        ent_r, ent_i, out_r, out_i = _segment_states(
            car_r[...], car_i[...], ends_r, ends_i, ta_r[2, 0:1, :], -ta_r[3, 0:1, :], range(SUBLANES - 1, -1, -1))
        car_r[...] = out_r
        car_i[...] = out_i

        def entering(n, carry):
            gnr, gni, ar, ai = carry
            rows = _step_rows(SCAN_STEPS - 1 - n)
            power = _step_rows(n)
            gr, gi = _cmulc_add(gr_s[rows, :], gi_s[rows, :], tp_r[0, power, :], tp_r[1, power, :], ent_r, ent_i)
            gr_s[rows, :] = gr
            gi_s[rows, :] = gi
            xr = xsr_r[rows, :]
            xi = xsi_r[rows, :]
            return gr, gi, ar + (xr * gnr + xi * gni), ai + (xr * gni - xi * gnr)

        _, _, ar, ai = lax.fori_loop(0, SCAN_STEPS, entering, (ent_r, ent_i, zero, zero), unroll=2)
        glbr_r[...] += ar
        glbi_r[...] += ai
        gr = gr_s[...]
        gi = gi_s[...]
        dui_s[...] = _dot_t(gr, bbr_r[...]) + _dot_t(gi, bbi_r[...]) + d_r[...] * dy
        _deinterleave(dui_s, dun_s)
        du = dun_s[...]
        du_r[...] = du.astype(BF16)
        dus_r[...] += jnp.sum(du, axis=0, keepdims=True)
        gd_r[...] += jnp.sum(dy * u, axis=0, keepdims=True)
        gbbr_r[...] += _tdot(u, gr)
        gbbi_r[...] += _tdot(u, gi)
        gctr_r[...] += _tdot(dy, xsr_r[...])
        gcti_r[...] -= _tdot(dy, xsi_r[...])

    def rev(w):
        return pl.BlockSpec((tc, w), lambda j, k: (nk - 1 - k, j))

    strip_mat = pl.BlockSpec((128, STRIP), lambda j, k: (j, 0))
    vec = pl.BlockSpec((1, 128), lambda j, k: (0, j))
    lbacc = pl.BlockSpec((SUBLANES, STRIP), lambda j, k: (0, j))
    sh = jax.ShapeDtypeStruct
    return _launch(
        body, comms, name="ssm_backward", grid=(N_STRIPS, nk),
        out_shape=(sh((t, IN_COLS), BF16), sh((1, SSM_W), F32),
                   sh((SSM_W, STRIP), F32), sh((SSM_W, STRIP), F32), sh((SSM_W, STRIP), F32), sh((SSM_W, STRIP), F32),
                   sh((SUBLANES, STATE_COLS), F32), sh((SUBLANES, STATE_COLS), F32), sh((1, SSM_W), F32)),
        in_specs=[rev(128), rev(128), rev(STRIP), rev(STRIP),
                  strip_mat, strip_mat, strip_mat, strip_mat, vec,
                  pl.BlockSpec((4, SUBLANES, STRIP), lambda j, k: (0, 0, j)),
                  pl.BlockSpec((2, tc, STRIP), lambda j, k: (0, 0, j)), ANY],
        out_specs=(rev(128), vec, strip_mat, strip_mat, strip_mat, strip_mat, lbacc, lbacc, vec),
        scratch_shapes=[pltpu.VMEM((tc, STRIP), F32), pltpu.VMEM((tc, STRIP), F32)]
        + [pltpu.VMEM((tc, 128), F32)] * 4 + [pltpu.VMEM((1, STRIP), F32)] * 2,
        aliases={11: 0}, sem=("parallel", "arbitrary"),
    )(dya0, proj, xsr, xsi, bbr, bbi, ctr, cti, d_skip, tab_a, tab_p, dproj)


def _input_grad(dr1, dproj, w_in_st, comms=(), tm=256):
    t = dr1.shape[0]

    def body(dr1_r, dp_r, w_r, dx_r):
        acc = ALPHA * dr1_r[...]
        for j in range(N_SHARDS):
            acc = acc + lax.dot_general(dp_r[:, D_MODEL * j:D_MODEL * (j + 1)], w_r[j],
                                        (((1,), (1,)), ((), ())), preferred_element_type=F32)
        dx_r[...] = acc

    (dx,), sent = _launch(
        body, comms, name="input_grad", grid=(t // tm,),
        out_shape=(jax.ShapeDtypeStruct((t, D_MODEL), F32),),
        in_specs=[pl.BlockSpec((tm, D_MODEL), lambda i: (i, 0)), pl.BlockSpec((tm, IN_COLS), lambda i: (i, 0)),
                  _const((N_SHARDS, D_MODEL, D_MODEL))],
        out_specs=(pl.BlockSpec((tm, D_MODEL), lambda i: (i, 0)),),
        sem=("parallel",),
    )(dr1, dproj, w_in_st)
    return dx, sent


def _in_weight_grad(x, dproj, comms=(), tk=512):
    t = x.shape[0]

    def body(x_r, dp_r, gw_r):
        @pl.when(pl.program_id(1) == 0)
        def _():
            gw_r[...] = jnp.zeros_like(gw_r)

        gw_r[0] += _tdot(x_r[...], dp_r[...])

    (g_w_in,), sent = _launch(
        body, comms, name="in_weight_grad", grid=(N_SHARDS, t // tk),
        out_shape=(jax.ShapeDtypeStruct((N_SHARDS, D_MODEL, D_MODEL), F32),),
        in_specs=[pl.BlockSpec((tk, D_MODEL), lambda j, k: (k, 0)), pl.BlockSpec((tk, D_MODEL), lambda j, k: (k, j))],
        out_specs=(pl.BlockSpec((1, D_MODEL, D_MODEL), lambda j, k: (j, 0, 0)),),
        sem=("parallel", "arbitrary"),
    )(x, dproj)
    return g_w_in, sent


MIXER_W = ("glu_w", "w_ssm_out", "w_conv_out", "w_o")
FFN_W = ("w_gate", "w_up", "w_down")


def _device_step(x, target, small, shards, c_arr, me_arr):
    lr, li = small["ssm_lambda_re"][0], small["ssm_lambda_im"][0]
    ldt = small["ssm_log_dt"][0][:, None]
    rep16 = lambda a: jnp.broadcast_to(a[:, None, :], (N_GROUPS, GROUP_C, a.shape[-1])).reshape(SSM_W, a.shape[-1])
    lr16, li16 = rep16(lr), rep16(li)
    ldt16 = rep16(jnp.broadcast_to(ldt, (N_GROUPS, N_STATE)))
    brt = small["ssm_b_re"][0].transpose(0, 2, 1).reshape(SSM_W, N_STATE)
    bit = small["ssm_b_im"][0].transpose(0, 2, 1).reshape(SSM_W, N_STATE)
    cre = small["ssm_c_re"][0].reshape(SSM_W, N_STATE)
    cim = small["ssm_c_im"][0].reshape(SSM_W, N_STATE)
    disc = (lr, li, ldt, lr16, li16, ldt16, brt, bit)

    pwr, pwi, bbr, bbi, ctr, cti = _ssm_prepare(*disc, cre, cim)
    tab_a, tab_p = _scan_tables(pwr, pwi)

    first_sh = [shards[n] for n in MIXER_W + FFN_W[:1]]
    second_sh = [shards[n] for n in FFN_W[1:]]
    (w_in_st,) = _gather_weights([shards["w_in"]])
    proj, (arrived,) = _in_proj(x, w_in_st, small["b_in"], comms=[_gather_ici(first_sh, [shards["conv_w"]])])
    (xsr, xsi, ya0), (second_part, first_st) = _ssm_forward(
        proj, bbr, bbi, ctr, cti, small["ssm_d"], tab_a, tab_p,
        comms=[_gather_ici(second_sh), _gather_d2d(arrived[:len(first_sh)], first_sh)])
    glu_st, wso_st, wco_st, wo_st, wg_st = (_own_slot(st, sh) for st, sh in zip(first_st, first_sh))
    conv_st = _own_slot(arrived[len(first_sh)], shards["conv_w"])
    conv_w8 = jnp.pad(conv_st[:, :3, :].transpose(1, 0, 2).reshape(3, SSM_W), ((0, SUBLANES - 3), (0, 0)))
    w_o = wo_st.reshape(D_MODEL, D_MODEL)
    glu_w = glu_st.reshape(SSM_W, SSM_W)
    (xhat1, rstd1, ya, yb), (second_st,) = _mixer_forward(
        x, proj, ya0, glu_w, small["glu_b"], wso_st, conv_w8, wco_st, w_o, comms=[_gather_d2d(second_part, second_sh)])
    wu_st, wd_st = (_own_slot(st, sh) for st, sh in zip(second_st, second_sh))
    (loss, dr1, x1b, dr2b, hid, dhg, dhu, g_ln2_g, g_ln2_b, g_ln1_g, g_ln1_b) = _ffn_step(
        xhat1, rstd1, target, small["ln1_g"], small["ln1_b"], small["ln2_g"], small["ln2_b"], wg_st, wu_st, wd_st)

    g_ffn = _ffn_weight_grads(x1b, dr2b, hid, dhg, dhu)
    (dya0, dproj, dbias, g_wo, g_wso, g_wco, g_glu_w, g_glu_b, g_conv8), (got_ffn,) = _mixer_backward(
        dr1, proj, ya0, ya, yb, glu_w, small["glu_b"], wso_st, conv_w8, wco_st, w_o, comms=[_swap_comm(g_ffn)])
    add_halves = lambda gs, rs: _per_shape(lambda a, b: _add_own_half(a, b, c_arr), list(gs), list(rs))
    sum_chips = lambda owns, slots: _per_shape(lambda a, b: _sum_chips(a, b, me_arr), list(owns), list(slots))
    chip_ffn = add_halves(g_ffn, got_ffn)
    g_mix = [g_glu_w.reshape(N_SHARDS, 128, SSM_W), g_wso, g_wco, g_wo.reshape(N_SHARDS, 256, D_MODEL)]
    (dproj, dus, gbbr, gbbi, gctr, gcti, glbr, glbi, g_d), (slots_ffn, got_mix) = _ssm_backward(
        dya0, proj, xsr, xsi, bbr, bbi, ctr, cti, small["ssm_d"], tab_a, tab_p, dproj,
        comms=[_scatter_comm(chip_ffn), _swap_comm(g_mix)])
    halves_ffn = sum_chips(chip_ffn, slots_ffn)
    chip_mix = add_halves(g_mix, got_mix)
    g_lr, g_li, g_ldt, g_brt, g_bit, g_cre, g_cim = _ssm_param_grads(
        *disc, glbr.reshape(SUBLANES, N_GROUPS, N_STATE), glbi.reshape(SUBLANES, N_GROUPS, N_STATE),
        gbbr, gbbi, gctr, gcti)
    g_w_in, (others_ffn, slots_mix) = _in_weight_grad(
        x, dproj, comms=[_send_comm(halves_ffn), _scatter_comm(chip_mix)])
    halves_mix = sum_chips(chip_mix, slots_mix)
    dx, _ = _input_grad(dr1, dproj, w_in_st)

    g_conv = jnp.pad(g_conv8[:3].reshape(3, N_SHARDS, 128).transpose(1, 0, 2), ((0, 0), (0, SUBLANES - 3), (0, 0)))
    pieces = [dus, dbias[:, SSM_W:], g_lr, g_li, g_ldt, g_brt, g_bit, g_cre, g_cim, g_d, g_glu_b,
              g_ln1_g, g_ln1_b, g_ln2_g, g_ln2_b, loss]
    flat = jnp.concatenate([p.reshape(-1) for p in pieces])
    g_packed = jnp.pad(flat, (0, PACKED_ROWS * 128 - flat.shape[0])).reshape(PACKED_ROWS, 128)
    ((got_w, got_conv, got_packed),) = _standalone([_swap_comm([g_w_in], [g_conv, g_packed])], "swap_with_sibling")
    (chip_w,) = add_halves([g_w_in], [got_w])
    chip_conv, chip_packed = _small_pair_sums([g_conv, g_packed], [got_conv, got_packed])
    ((slots_w, slots_conv, slots_packed),) = _standalone(
        [_scatter_comm([chip_w, chip_conv], [chip_packed])], "scatter_to_chips")
    (halves_w,) = sum_chips([chip_w], [slots_w])
    conv_total, packed_total = _small_totals(me_arr, chip_conv, slots_conv, chip_packed, slots_packed)
    (others_rest,) = _standalone([_send_comm([halves_w] + halves_mix)], "send_to_sibling")

    pairs = dict(zip(FFN_W, zip(halves_ffn, others_ffn)))
    pairs.update(zip(("w_in",) + MIXER_W, zip([halves_w] + halves_mix, others_rest)))
    return dx, pairs, conv_total, packed_total


PACKED_ROWS = 1136
PACKED_LAYOUT = (("b_in", IN_COLS), ("ssm_lambda_re", STATE_COLS), ("ssm_lambda_im", STATE_COLS),
                 ("ssm_log_dt", N_GROUPS), ("ssm_b_re", SSM_W * N_STATE), ("ssm_b_im", SSM_W * N_STATE),
                 ("ssm_c_re", SSM_W * N_STATE), ("ssm_c_im", SSM_W * N_STATE), ("ssm_d", SSM_W), ("glu_b", SSM_W),
                 ("ln1_g", D_MODEL), ("ln1_b", D_MODEL), ("ln2_g", D_MODEL), ("ln2_b", D_MODEL), ("loss", 1))


def _unpack_small(packed):
    flat = packed.reshape(-1)
    out, off = {}, 0
    for name, size in PACKED_LAYOUT:
        out[name] = flat[off:off + size]
        off += size
    for name in ("ssm_b_re", "ssm_b_im"):
        out[name] = out[name].reshape(N_GROUPS, GROUP_C, N_STATE).transpose(0, 2, 1)[None]
    for name in ("ssm_c_re", "ssm_c_im"):
        out[name] = out[name].reshape(1, N_GROUPS, GROUP_C, N_STATE)
    for name in ("ssm_lambda_re", "ssm_lambda_im"):
        out[name] = out[name].reshape(1, N_GROUPS, N_STATE)
    for name in ("b_in", "ssm_log_dt", "ssm_d", "glu_b", "ln1_g", "ln1_b", "ln2_g", "ln2_b"):
        out[name] = out[name][None]
    return out


BIG = ("w_in", "glu_w", "w_ssm_out", "w_conv_out", "w_o", "w_gate", "w_up", "w_down")
SMALL = ("b_in", "ssm_lambda_re", "ssm_lambda_im", "ssm_log_dt", "ssm_b_re", "ssm_b_im", "ssm_c_re", "ssm_c_im",
         "ssm_d", "glu_b", "ln1_g", "ln1_b", "ln2_g", "ln2_b")
WEIGHTS = ("w_in", "b_in", "ssm_lambda_re", "ssm_lambda_im", "ssm_log_dt", "ssm_b_re", "ssm_b_im", "ssm_c_re",
           "ssm_c_im", "ssm_d", "glu_w", "glu_b", "w_ssm_out", "conv_w", "w_conv_out", "w_o", "ln1_g", "ln1_b",
           "w_gate", "w_up", "w_down", "ln2_g", "ln2_b")


def _place():
    x, y, c = lax.axis_index("x"), lax.axis_index("y"), lax.axis_index("c")
    chips = [(1 - x, y), (x, 1 - y), (1 - x, 1 - y)]
    return x, y, c, chips


def _shard_of(chip):
    return 2 * chip[0] + chip[1]


def _remote(src, dst, send_sem, recv_sem, to):
    return pltpu.make_async_remote_copy(src_ref=src, dst_ref=dst, send_sem=send_sem, recv_sem=recv_sem,
                                        device_id=to, device_id_type=MESH)


def _half_rows(shard, which):
    r2 = shard.shape[0] // 2
    return pl.ds(pl.multiple_of(which * r2, 16), r2)


def _own_slot(stack, shard):
    me = _shard_of((lax.axis_index("x"), lax.axis_index("y")))
    return lax.dynamic_update_slice(stack, shard[None], (me,) + (0,) * shard.ndim)


def _gather_ici(halved, whole=()):
    shards = list(halved) + list(whole)
    nh = len(halved)

    def copies(src, dst, sems):
        send_sem, recv_sem = sems
        x, y, c, chips = _place()
        me = _shard_of((x, y))
        out = []
        for a in range(len(shards)):
            for k, chip in enumerate(chips):
                if a < nh:
                    rows = _half_rows(shards[a], c)
                    out.append(_remote(src[a].at[rows], dst[a].at[me, rows], send_sem.at[a, k], recv_sem.at[a, k],
                                       (*chip, c)))
                else:
                    out.append(_remote(src[a], dst[a].at[me], send_sem.at[a, k], recv_sem.at[a, k], (*chip, c)))
        return out

    n = len(shards)
    return _Comm(shards, [jax.ShapeDtypeStruct((N_SHARDS,) + s.shape, s.dtype) for s in shards],
                 [pltpu.SemaphoreType.DMA((n, 3))] * 2, copies)


def _gather_d2d(stacks, shards):
    def copies(src, dst, sems):
        del src
        send_sem, recv_sem = sems
        x, y, c, chips = _place()
        out = []
        for a in range(len(stacks)):
            for k, chip in enumerate(chips):
                rows = dst[a].at[_shard_of(chip), _half_rows(shards[a], c)]
                out.append(_remote(rows, rows, send_sem.at[a, k], recv_sem.at[a, k], (x, y, 1 - c)))
        return out

    n = len(stacks)
    return _Comm(stacks, [jax.ShapeDtypeStruct(s.shape, s.dtype) for s in stacks],
                 [pltpu.SemaphoreType.DMA((n, 3))] * 2, copies, aliased=True)


def _standalone(comms, name):
    return _launch(None, comms, name=name, grid=(), in_specs=[], out_specs=(), out_shape=())()[1]


def _gather_weights(shards):
    n = len(shards)

    def body(*refs):
        src, dst = refs[:n], refs[n:2 * n]
        send_sem, recv_sem, fsend_sem, frecv_sem = refs[2 * n:]
        x, y, c, chips = _place()
        me = _shard_of((x, y))
        sibling = (x, y, 1 - c)
        sends = []
        for a in range(n):
            mine = _half_rows(shards[a], c)
            for k, chip in enumerate(chips):
                cp = _remote(src[a].at[mine], dst[a].at[me, mine], send_sem.at[a, k], recv_sem.at[a, k], (*chip, c))
                cp.start()
                sends.append(cp)
        for a in range(n):
            for k, chip in enumerate(chips):
                rows = dst[a].at[_shard_of(chip), _half_rows(shards[a], c)]
                _remote(rows, rows, send_sem.at[a, k], recv_sem.at[a, k], sibling).wait_recv()
                cp = _remote(rows, rows, fsend_sem.at[a, k], frecv_sem.at[a, k], sibling)
                cp.start()
                sends.append(cp)
        for a in range(n):
            for k, chip in enumerate(chips):
                rows = dst[a].at[_shard_of(chip), _half_rows(shards[a], 1 - c)]
                _remote(rows, rows, fsend_sem.at[a, k], frecv_sem.at[a, k], sibling).wait_recv()
        for cp in sends:
            cp.wait_send()

    stacks = pl.pallas_call(
        body, name="gather_weights",
        out_shape=tuple(jax.ShapeDtypeStruct((N_SHARDS,) + s.shape, s.dtype) for s in shards),
        in_specs=[ANY] * n, out_specs=(ANY,) * n,
        scratch_shapes=[pltpu.SemaphoreType.DMA((n, 3))] * 4,
    )(*shards)
    return [_own_slot(st, sh) for st, sh in zip(stacks, shards)]


def _swap_comm(big, small=()):
    nb, n = len(big), len(big) + len(small)
    arrays = list(big) + list(small)

    def copies(src, dst, sems):
        send_sem, recv_sem = sems
        x, y, c, _ = _place()
        out = []
        for a in range(n):
            if a < nb:
                r2 = arrays[a].shape[1] // 2
                part = src[a].at[:, pl.ds(pl.multiple_of((1 - c) * r2, SUBLANES), r2), :]
            else:
                part = src[a]
            out.append(_remote(part, dst[a], send_sem.at[a], recv_sem.at[a], (x, y, 1 - c)))
        return out

    out_shape = [jax.ShapeDtypeStruct((N_SHARDS, g.shape[1] // 2, g.shape[2]), g.dtype) for g in big]
    out_shape += [jax.ShapeDtypeStruct(g.shape, g.dtype) for g in small]
    return _Comm(arrays, out_shape, [pltpu.SemaphoreType.DMA((n,))] * 2, copies)


def _scatter_comm(slabbed, small=()):
    ns, n = len(slabbed), len(slabbed) + len(small)
    arrays = list(slabbed) + list(small)

    def copies(src, dst, sems):
        send_sem, recv_sem = sems
        _, _, c, chips = _place()
        out = []
        for a in range(n):
            for k, chip in enumerate(chips):
                part = src[a].at[_shard_of(chip)] if a < ns else src[a]
                out.append(_remote(part, dst[a].at[k], send_sem.at[a, k], recv_sem.at[a, k], (*chip, c)))
        return out

    out_shape = [jax.ShapeDtypeStruct((3,) + g.shape[1:], g.dtype) for g in slabbed]
    out_shape += [jax.ShapeDtypeStruct((3,) + g.shape, g.dtype) for g in small]
    return _Comm(arrays, out_shape, [pltpu.SemaphoreType.DMA((n, 3))] * 2, copies)


def _send_comm(arrays):
    n = len(arrays)

    def copies(src, dst, sems):
        send_sem, recv_sem = sems
        x, y, c, _ = _place()
        return [_remote(src[a], dst[a], send_sem.at[a], recv_sem.at[a], (x, y, 1 - c)) for a in range(n)]

    return _Comm(arrays, [jax.ShapeDtypeStruct(h.shape, h.dtype) for h in arrays],
                 [pltpu.SemaphoreType.DMA((n,))] * 2, copies)


def _row_chunk(rows):
    for cand in (256, 176, 128, 64):
        if rows % cand == 0:
            return cand
    return rows


def _per_shape(fn, *lists):
    groups = {}
    for i, items in enumerate(zip(*lists)):
        groups.setdefault(tuple(a.shape for a in items), []).append(i)
    out = [None] * len(lists[0])
    for idx in groups.values():
        for i, r in zip(idx, fn(*([lst[i] for i in idx] for lst in lists))):
            out[i] = r
    return out


def _add_own_half(stacks, receiveds, c):
    n = len(stacks)
    _, r2, cols = receiveds[0].shape

    def body(c_ref, *refs):
        del c_ref
        for a in range(n):
            refs[2 * n + a][...] = (refs[a][...] + refs[n + a][...]).astype(BF16)

    own = pl.BlockSpec((1, r2, cols), lambda s, c_ref: (s, c_ref[0], 0))
    got = pl.BlockSpec((1, r2, cols), lambda s, c_ref: (s, 0, 0))
    return pl.pallas_call(
        body, name="add_own_half",
        grid_spec=pltpu.PrefetchScalarGridSpec(
            num_scalar_prefetch=1, grid=(N_SHARDS,), in_specs=[own] * n + [got] * n, out_specs=(got,) * n),
        out_shape=(jax.ShapeDtypeStruct(receiveds[0].shape, BF16),) * n,
        compiler_params=_params(("parallel",)),
    )(c, *stacks, *receiveds)


def _chip_order_sum(me, own, s0, s1, s2):
    terms = []
    for s in range(N_SHARDS):
        d = jnp.bitwise_xor(me, s)
        terms.append(jnp.where(d == 0, own, jnp.where(d == 2, s0, jnp.where(d == 1, s1, s2))))
    return ((terms[0] + terms[1]) + terms[2]) + terms[3]


def _sum_chips(own_stacks, slots, me):
    n = len(slots)
    _, rows, cols = slots[0].shape
    rc = _row_chunk(rows)

    def body(me_ref, *refs):
        del me_ref
        for a in range(n):
            own_r, s_r = refs[a], refs[n + a]
            refs[2 * n + a][...] = (((own_r[0].astype(F32) + s_r[0].astype(F32)) + s_r[1].astype(F32))
                                    + s_r[2].astype(F32))

    own = pl.BlockSpec((1, rc, cols), lambda i, me_ref: (me_ref[0], i, 0))
    three = pl.BlockSpec((3, rc, cols), lambda i, me_ref: (0, i, 0))
    total = pl.BlockSpec((rc, cols), lambda i, me_ref: (i, 0))
    return pl.pallas_call(
        body, name="sum_chips",
        grid_spec=pltpu.PrefetchScalarGridSpec(
            num_scalar_prefetch=1, grid=(rows // rc,), in_specs=[own] * n + [three] * n, out_specs=(total,) * n),
        out_shape=(jax.ShapeDtypeStruct((rows, cols), F32),) * n,
        compiler_params=_params(("parallel",)),
    )(me, *own_stacks, *slots)


def _small_pair_sums(mine, theirs):
    n = len(mine)

    def body(*refs):
        for a in range(n):
            refs[2 * n + a][...] = refs[a][...] + refs[n + a][...]

    vm = pl.BlockSpec(memory_space=pltpu.VMEM)
    return pl.pallas_call(
        body, name="small_pair_sums", out_shape=tuple(jax.ShapeDtypeStruct(g.shape, g.dtype) for g in mine),
        in_specs=[vm] * (2 * n), out_specs=(vm,) * n,
        compiler_params=pltpu.CompilerParams(vmem_limit_bytes=VMEM_LIMIT),
    )(*mine, *theirs)


def _adam_math(w, g, m, v):
    m = ADAM_B1 * m + (1.0 - ADAM_B1) * g
    v = ADAM_B2 * v + (1.0 - ADAM_B2) * (g * g)
    m_hat = m / (1.0 - ADAM_B1 ** ADAM_STEP)
    v_hat = v / (1.0 - ADAM_B2 ** ADAM_STEP)
    delta = -ADAM_LR * (m_hat / (jnp.sqrt(v_hat) + ADAM_EPS) + ADAM_WD * w)
    return delta, m, v


def _small_totals(me, conv_stack, conv_slots, packed, packed_slots):
    def body(me_ref, cs_r, cslot_r, p_r, pslot_r, conv_r, tot_r):
        me_ = me_ref[0]
        conv_r[...] = _chip_order_sum(me_, cs_r[me_], cslot_r[0], cslot_r[1], cslot_r[2])
        tot_r[...] = _chip_order_sum(me_, p_r[...], pslot_r[0], pslot_r[1], pslot_r[2])

    vm = pl.BlockSpec(memory_space=pltpu.VMEM)
    return pl.pallas_call(
        body, name="small_totals",
        out_shape=(jax.ShapeDtypeStruct(conv_stack.shape[1:], F32), jax.ShapeDtypeStruct(packed.shape, F32)),
        in_specs=[pl.BlockSpec(memory_space=pltpu.SMEM)] + [vm] * 4, out_specs=(vm, vm),
    )(me, conv_stack, conv_slots, packed, packed_slots)


def _adam_small(gs, ws, ms, vs):
    n = len(gs)

    def body(*refs):
        for a in range(n):
            g_r, w_r, m_r, v_r = (refs[i * n + a] for i in range(4))
            d_r, nm_r, nv_r = (refs[(4 + i) * n + a] for i in range(3))
            d_r[...], nm_r[...], nv_r[...] = _adam_math(w_r[...], g_r[...], m_r[...], v_r[...])

    vm = pl.BlockSpec(memory_space=pltpu.VMEM)
    shapes = tuple(jax.ShapeDtypeStruct(w.shape, F32) for w in ws)
    out = pl.pallas_call(
        body, name="adam_small", out_shape=shapes * 3, in_specs=[vm] * (4 * n), out_specs=(vm,) * (3 * n),
        compiler_params=pltpu.CompilerParams(vmem_limit_bytes=VMEM_LIMIT),
    )(*gs, *ws, *ms, *vs)
    return out[:n], out[n:2 * n], out[2 * n:]


def _adam_big(ws, mines, others, ms, vs, c):
    n = len(ws)
    r2, cols = mines[0].shape
    rc = _row_chunk(r2)
    nch = r2 // rc

    def body(c_ref, *refs):
        mine_is_here = pl.program_id(0) == c_ref[0]
        for a in range(n):
            w_r, mine_r, other_r, m_r, v_r = (refs[i * n + a] for i in range(5))
            g_r, d_r, nm_r, nv_r = (refs[(5 + i) * n + a] for i in range(4))
            g = jnp.where(mine_is_here, mine_r[...], other_r[...])
            g_r[...] = g
            d_r[...], nm_r[...], nv_r[...] = _adam_math(w_r[...], g, m_r[...], v_r[...])

    full = pl.BlockSpec((rc, cols), lambda h, i, c_ref: (h * nch + i, 0))
    half = pl.BlockSpec((rc, cols), lambda h, i, c_ref: (i, 0))
    shape = jax.ShapeDtypeStruct((2 * r2, cols), F32)
    out = pl.pallas_call(
        body, name="adam_big",
        grid_spec=pltpu.PrefetchScalarGridSpec(
            num_scalar_prefetch=1, grid=(2, nch),
            in_specs=[full] * n + [half] * (2 * n) + [full] * (2 * n), out_specs=(full,) * (4 * n)),
        out_shape=(shape,) * (4 * n), compiler_params=_params(("parallel", "parallel")),
    )(c, *ws, *mines, *others, *ms, *vs)
    return [tuple(out[i * n + a] for i in range(4)) for a in range(n)]


def kernel(x, w_in, b_in, ssm_lambda_re, ssm_lambda_im, ssm_log_dt, ssm_b_re, ssm_b_im, ssm_c_re, ssm_c_im, ssm_d, glu_w, glu_b, w_ssm_out, conv_w, w_conv_out, w_o, ln1_g, ln1_b, w_gate, w_up, w_down, ln2_g, ln2_b, loss_target, m_w_in, m_b_in, m_ssm_lambda_re, m_ssm_lambda_im, m_ssm_log_dt, m_ssm_b_re, m_ssm_b_im, m_ssm_c_re, m_ssm_c_im, m_ssm_d, m_glu_w, m_glu_b, m_w_ssm_out, m_conv_w, m_w_conv_out, m_w_o, m_ln1_g, m_ln1_b, m_w_gate, m_w_up, m_w_down, m_ln2_g, m_ln2_b, v_w_in, v_b_in, v_ssm_lambda_re, v_ssm_lambda_im, v_ssm_log_dt, v_ssm_b_re, v_ssm_b_im, v_ssm_c_re, v_ssm_c_im, v_ssm_d, v_glu_w, v_glu_b, v_w_ssm_out, v_conv_w, v_w_conv_out, v_w_o, v_ln1_g, v_ln1_b, v_w_gate, v_w_up, v_w_down, v_ln2_g, v_ln2_b):
    given = dict(locals())
    w = {n: given[n] for n in WEIGHTS}
    m = {n: given["m_" + n] for n in WEIGHTS}
    v = {n: given["v_" + n] for n in WEIGHTS}

    flip = lambda n, a: a.T if n in ("w_gate", "w_up") else a
    shards = {n: flip(n, w[n][0]).astype(BF16) for n in BIG}
    shards["conv_w"] = jnp.pad(conv_w[0], ((0, SUBLANES - 3), (0, 0)))
    c_arr = jnp.reshape(lax.axis_index("c"), (1,)).astype(jnp.int32)
    me = _shard_of((lax.axis_index("x"), lax.axis_index("y")))
    me_arr = jnp.reshape(me, (1,)).astype(jnp.int32)
    dx, pairs, conv_total, packed_total = _device_step(
        x[0], loss_target[0], {n: w[n] for n in SMALL}, shards, c_arr, me_arr)

    grad = _unpack_small(packed_total)
    loss_total = grad.pop("loss")[0]
    grad["conv_w"] = conv_total[:3][None]
    small_names = ("conv_w",) + SMALL
    swap = lambda n, a: a.transpose(0, 1, 3, 2) if n in ("ssm_b_re", "ssm_b_im") else a
    ds, nms, nvs = _adam_small(*([swap(n, d[n]) for n in small_names] for d in (grad, w, m, v)))
    delta, new_m, new_v = {}, {}, {}
    for i, n in enumerate(small_names):
        delta[n], new_m[n], new_v[n] = swap(n, ds[i]), swap(n, nms[i]), swap(n, nvs[i])
    updated = _per_shape(
        lambda *a: _adam_big(*a, c_arr),
        [flip(n, w[n][0]) for n in BIG], [pairs[n][0] for n in BIG], [pairs[n][1] for n in BIG],
        [flip(n, m[n][0]) for n in BIG], [flip(n, v[n][0]) for n in BIG])
    for n, results in zip(BIG, updated):
        grad[n], delta[n], new_m[n], new_v[n] = (flip(n, r)[None] for r in results)

    return (loss_total, dx[None], *[grad[n] for n in WEIGHTS], *[delta[n] for n in WEIGHTS],
            *[new_m[n] for n in WEIGHTS], *[new_v[n] for n in WEIGHTS])
```

```python
import functools
import math

import jax
import jax.numpy as jnp
from jax import lax
from jax.experimental import pallas as pl
from jax.experimental.pallas import tpu as pltpu

F32 = jnp.float32
BF16 = jnp.bfloat16

D_MODEL = 1024
IN_COLS = 4096
SSM_W = 512
N_GROUPS = 32
N_STATE = 64
GROUP_C = 16
STATE_COLS = N_GROUPS * N_STATE
STRIP = 512
N_STRIPS = STATE_COLS // STRIP
FFN_SHARD = 704
N_SHARDS = 4
ALPHA = 2.0 ** 0.25
LN_EPS = 1e-5
GELU_K = math.sqrt(2.0 / math.pi)
GELU_C = 0.044715

ADAM_LR = 0.001
ADAM_B1 = 0.9
ADAM_B2 = 0.999
ADAM_EPS = 1e-08
ADAM_WD = 0.01
ADAM_STEP = 10

V7X_VMEM_BYTES = 64 * 1024 * 1024
VMEM_LIMIT = V7X_VMEM_BYTES - 8 * 1024 * 1024
SUBLANES = 8
N_POWERS = 64

MESH = pl.DeviceIdType.MESH
ANY = pl.BlockSpec(memory_space=pl.ANY)


def _dot(a, b):
    return jnp.dot(a.astype(BF16), b.astype(BF16), preferred_element_type=F32)


def _dot_t(a, b):
    return lax.dot_general(a.astype(BF16), b.astype(BF16), (((1,), (1,)), ((), ())),
                           preferred_element_type=F32)


def _tdot(a, b):
    return lax.dot_general(a.astype(BF16), b.astype(BF16), (((0,), (0,)), ((), ())),
                           preferred_element_type=F32)


def _sigmoid(v):
    return 0.5 * jnp.tanh(0.5 * v) + 0.5


def _split3(v):
    hi = v.astype(BF16)
    r1 = v - hi.astype(F32)
    mid = r1.astype(BF16)
    lo = (r1 - mid.astype(F32)).astype(BF16)
    return hi, mid, lo


def _exact_dot(v, sel):
    hi, mid, lo = _split3(v)
    return (jnp.dot(hi, sel, preferred_element_type=F32)
            + jnp.dot(mid, sel, preferred_element_type=F32)
            + jnp.dot(lo, sel, preferred_element_type=F32))


def _const(shape):
    nd = len(shape)
    return pl.BlockSpec(shape, lambda *_: (0,) * nd)


def _params(sem, vmem=VMEM_LIMIT):
    return pltpu.CompilerParams(dimension_semantics=sem, vmem_limit_bytes=vmem)


def _gelu_parts(v):
    inner = GELU_K * (v + GELU_C * v * v * v)
    t = jnp.tanh(inner)
    g = 0.5 * v * (1.0 + t)
    dg = 0.5 * (1.0 + t) + 0.5 * v * (1.0 - t * t) * GELU_K * (1.0 + 3.0 * GELU_C * v * v)
    return g, dg


class _Comm:
    def __init__(self, inputs, out_shape, sems, copies, aliased=False):
        self.inputs, self.out_shape, self.sems = list(inputs), tuple(out_shape), list(sems)
        self.copies, self.aliased = copies, aliased


def _launch(body, comms, *, name, grid, in_specs, out_specs, out_shape, scratch_shapes=(), aliases=None, sem=None):
    comms = list(comms)
    n_in, n_out, n_scr = len(in_specs), len(out_specs), len(scratch_shapes)
    aliases = dict(aliases or {})
    layout = []
    p_in, p_out, p_sem = n_in, n_out, 0
    for cm in comms:
        layout.append((p_in, p_out, p_sem))
        if cm.aliased:
            for i in range(len(cm.inputs)):
                aliases[p_in + i] = p_out + i
        p_in, p_out, p_sem = p_in + len(cm.inputs), p_out + len(cm.out_shape), p_sem + len(cm.sems)
    tot_in, tot_out = p_in, p_out

    def fused(*refs):
        ins, outs = refs[:tot_in], refs[tot_in:tot_in + tot_out]
        scr = refs[tot_in + tot_out:tot_in + tot_out + n_scr]
        sems = refs[tot_in + tot_out + n_scr:]

        def descriptors():
            out = []
            for cm, (a, b, s) in zip(comms, layout):
                out += cm.copies(ins[a:a + len(cm.inputs)], outs[b:b + len(cm.out_shape)], sems[s:s + len(cm.sems)])
            return out

        steps = [pl.program_id(d) for d in range(len(grid))]
        first = functools.reduce(jnp.logical_and, [s == 0 for s in steps]) if grid else None
        last = functools.reduce(jnp.logical_and, [s == g - 1 for s, g in zip(steps, grid)]) if grid else None

        def start():
            for cp in descriptors():
                cp.start()

        def finish():
            for cp in descriptors():
                cp.wait()

        if comms:
            pl.when(first)(start) if grid else start()
        if body is not None:
            body(*ins[:n_in], *outs[:n_out], *scr)
        if comms:
            pl.when(last)(finish) if grid else finish()

    specs_in = list(in_specs) + [ANY] * (tot_in - n_in)
    specs_out = tuple(out_specs) + (ANY,) * (tot_out - n_out)
    shapes = tuple(out_shape) + tuple(s for cm in comms for s in cm.out_shape)
    scratch = list(scratch_shapes) + [s for cm in comms for s in cm.sems]
    if comms or sem is None:
        sem = ("arbitrary",) * len(grid)
    kwargs = dict(grid=grid) if grid else {}
    call = pl.pallas_call(fused, name=name, out_shape=shapes, in_specs=specs_in, out_specs=specs_out,
                          scratch_shapes=scratch, input_output_aliases=aliases,
                          compiler_params=_params(sem) if grid else None, **kwargs)

    def run(*args):
        out = call(*args, *(a for cm in comms for a in cm.inputs))
        results, rest = out[:n_out], out[n_out:]
        per_comm = []
        for cm in comms:
            per_comm.append(rest[:len(cm.out_shape)])
            rest = rest[len(cm.out_shape):]
        return results, per_comm

    return run


def _ssm_discretise(lr, li, ldt, lr16, li16, ldt16, brt, bit):
    def lam_bar(lr_, li_, ldt_):
        dt = jnp.exp(ldt_)
        mag = jnp.exp(lr_ * dt)
        return mag * jnp.cos(li_ * dt), mag * jnp.sin(li_ * dt)

    lb_re, lb_im = lam_bar(lr, li, ldt)
    l16_re, l16_im = lam_bar(lr16, li16, ldt16)
    den = lr16 * lr16 + li16 * li16
    num_re = l16_re - 1.0
    fr = (num_re * lr16 + l16_im * li16) / den
    fi = (l16_im * lr16 - num_re * li16) / den
    bb_re = fr * brt - fi * bit
    bb_im = fr * bit + fi * brt
    return lb_re, lb_im, bb_re, bb_im


def _strip_selectors():
    p = lax.broadcasted_iota(jnp.int32, (N_STATE, STRIP), 0)
    col = lax.broadcasted_iota(jnp.int32, (N_STATE, STRIP), 1)
    rep = ((col & (N_STATE - 1)) == p).astype(BF16)
    row = lax.broadcasted_iota(jnp.int32, (SSM_W, STRIP), 0)
    col2 = lax.broadcasted_iota(jnp.int32, (SSM_W, STRIP), 1)
    mask = (((row >> 4) & 7) == (col2 >> 6))
    return rep, mask


def _ssm_prepare(lr, li, ldt, lr16, li16, ldt16, brt, bit, cre, cim):
    def body(lr_r, li_r, ldt_r, lr16_r, li16_r, ldt16_r, brt_r, bit_r, cre_r, cim_r,
             pwr_r, pwi_r, bbr_r, bbi_r, ctr_r, cti_r):
        lb_re, lb_im, bb_re, bb_im = _ssm_discretise(
            lr_r[...], li_r[...], ldt_r[...], lr16_r[...], li16_r[...], ldt16_r[...], brt_r[...], bit_r[...])
        pr, pi_ = lb_re, lb_im
        pwr_r[0] = pr
        pwi_r[0] = pi_
        for k in range(1, N_POWERS):
            pr, pi_ = pr * lb_re - pi_ * lb_im, pr * lb_im + pi_ * lb_re
            pwr_r[k] = pr
            pwi_r[k] = pi_
        rep, mask = _strip_selectors()
        for src, dst in ((bb_re, bbr_r), (bb_im, bbi_r), (cre_r[...], ctr_r), (cim_r[...], cti_r)):
            wide = jnp.dot(src.astype(BF16), rep, preferred_element_type=F32)
            dst[...] = jnp.where(mask, wide, 0.0).astype(BF16)

    vm = pl.BlockSpec(memory_space=pltpu.VMEM)
    return pl.pallas_call(
        body, name="ssm_prepare",
        out_shape=(jax.ShapeDtypeStruct((N_POWERS, N_GROUPS, N_STATE), F32),) * 2
        + (jax.ShapeDtypeStruct((SSM_W, STRIP), BF16),) * 4,
        in_specs=[vm] * 10, out_specs=(vm,) * 6,
    )(lr, li, ldt, lr16, li16, ldt16, brt, bit, cre, cim)


def _scan_tables(pwr, pwi):
    pr = pwr.reshape(N_POWERS, STATE_COLS)
    pi_ = pwi.reshape(N_POWERS, STATE_COLS)
    rows8 = lambda v: jnp.broadcast_to(v[None], (SUBLANES, STATE_COLS))
    tab_a = jnp.stack([rows8(pr[0]), rows8(pi_[0]), rows8(pr[-1]), rows8(pi_[-1])])
    tab_p = jnp.stack([jnp.repeat(pr, SUBLANES, axis=0), jnp.repeat(pi_, SUBLANES, axis=0)])
    return tab_a, tab_p


def _ssm_param_grads(lr, li, ldt, lr16, li16, ldt16, brt, bit, dlbr, dlbi, dbbr, dbbi, dctr, dcti):
    def body(lr_r, li_r, ldt_r, lr16_r, li16_r, ldt16_r, brt_r, bit_r,
             dlbr_r, dlbi_r, dbbr_r, dbbi_r, dctr_r, dcti_r,
             glr_r, gli_r, gldt_r, gbrt_r, gbit_r, gcre_r, gcim_r):
        rep, mask = _strip_selectors()

        def fold(acc):
            return sum(lax.dot_general(t, rep, (((1,), (1,)), ((), ())), preferred_element_type=F32)
                       for t in _split3(jnp.where(mask, acc, 0.0)))

        g_lb_re = jnp.sum(dlbr_r[...], axis=0)
        g_lb_im = jnp.sum(dlbi_r[...], axis=0)
        g_bb_re = fold(dbbr_r[...])
        g_bb_im = fold(dbbi_r[...])
        gcre_r[...] = fold(dctr_r[...])
        gcim_r[...] = fold(dcti_r[...])
        prim = (lr_r[...], li_r[...], ldt_r[...], lr16_r[...], li16_r[...], ldt16_r[...], brt_r[...], bit_r[...])
        _, vjp = jax.vjp(_ssm_discretise, *prim)
        g_lr, g_li, g_ldt, g_lr16, g_li16, g_ldt16, g_brt, g_bit = vjp((g_lb_re, g_lb_im, g_bb_re, g_bb_im))
        grp = lax.broadcasted_iota(jnp.int32, (N_GROUPS, SSM_W), 0)
        rw = lax.broadcasted_iota(jnp.int32, (N_GROUPS, SSM_W), 1)
        gsum = ((rw >> 4) == grp).astype(BF16)

        def group_sum(v):
            return sum(jnp.dot(gsum, t, preferred_element_type=F32) for t in _split3(v))

        glr_r[...] = g_lr + group_sum(g_lr16)
        gli_r[...] = g_li + group_sum(g_li16)
        gldt_r[...] = g_ldt + jnp.sum(group_sum(g_ldt16), axis=1, keepdims=True)
        gbrt_r[...] = g_brt
        gbit_r[...] = g_bit

    vm = pl.BlockSpec(memory_space=pltpu.VMEM)
    gp = jax.ShapeDtypeStruct((N_GROUPS, N_STATE), F32)
    gb = jax.ShapeDtypeStruct((SSM_W, N_STATE), F32)
    return pl.pallas_call(
        body, name="ssm_param_grads",
        out_shape=(gp, gp, jax.ShapeDtypeStruct((N_GROUPS, 1), F32), gb, gb, gb, gb),
        in_specs=[vm] * 14, out_specs=(vm,) * 7,
    )(lr, li, ldt, lr16, li16, ldt16, brt, bit, dlbr, dlbi, dbbr, dbbi, dctr, dcti)


def _in_proj(x, w_in_st, b_in, comms=()):
    t = x.shape[0]
    tm = 512

    def body(x_r, w_r, b_r, o_r):
        xb = x_r[...].astype(BF16)
        for j in range(N_SHARDS):
            cols = slice(D_MODEL * j, D_MODEL * (j + 1))
            o_r[:, cols] = jnp.dot(xb, w_r[j], preferred_element_type=F32) + b_r[:, cols]

    (proj,), sent = _launch(
        body, comms, name="in_proj", grid=(t // tm,),
        out_shape=(jax.ShapeDtypeStruct((t, IN_COLS), F32),),
        in_specs=[pl.BlockSpec((tm, D_MODEL), lambda i: (i, 0)), _const((N_SHARDS, D_MODEL, D_MODEL)),
                  _const((1, IN_COLS))],
        out_specs=(pl.BlockSpec((tm, IN_COLS), lambda i: (i, 0)),),
        sem=("parallel",),
    )(x, w_in_st, b_in)
    return proj, sent


def _cmul_add(xr, xi, mr, mi, sr, si):
    return xr + (mr * sr - mi * si), xi + (mr * si + mi * sr)


SCAN_STEPS = N_POWERS
SCAN_CHUNK = SUBLANES * SCAN_STEPS


def _interleave(src_r, dst_r):
    for step in range(SCAN_STEPS):
        dst_r[SUBLANES * step:SUBLANES * (step + 1), :] = src_r[pl.ds(step, SUBLANES, stride=SCAN_STEPS), :]


def _deinterleave(src_r, dst_r):
    for step in range(SCAN_STEPS):
        dst_r[pl.ds(step, SUBLANES, stride=SCAN_STEPS), :] = src_r[SUBLANES * step:SUBLANES * (step + 1), :]


def _step_rows(step):
    return pl.ds(pl.multiple_of(step * SUBLANES, SUBLANES), SUBLANES)


def _segment_states(first_r, first_i, ends_r, ends_i, a64_r, a64_i, order):
    row = lax.broadcasted_iota(jnp.int32, ends_r.shape, 0)
    cur_r, cur_i = first_r, first_i
    ent_r = jnp.zeros_like(ends_r)
    ent_i = jnp.zeros_like(ends_i)
    for s in order:
        ent_r = jnp.where(row == s, jnp.broadcast_to(cur_r, ends_r.shape), ent_r)
        ent_i = jnp.where(row == s, jnp.broadcast_to(cur_i, ends_i.shape), ent_i)
        cur_r, cur_i = _cmul_add(ends_r[s:s + 1, :], ends_i[s:s + 1, :], a64_r, a64_i, cur_r, cur_i)
    return ent_r, ent_i, cur_r, cur_i


def _ssm_forward(proj, bbr, bbi, ctr, cti, d_skip, tab_a, tab_p, comms=(), tc=SCAN_CHUNK):
    t = proj.shape[0]

    def body(u_r, bbr_r, bbi_r, ctr_r, cti_r, d_r, ta_r, tp_r, xsr_r, xsi_r, y_r, ui_s, yi_s, car_r, car_i):
        @pl.when(pl.program_id(1) == 0)
        def _():
            car_r[...] = jnp.zeros_like(car_r)
            car_i[...] = jnp.zeros_like(car_i)

        _interleave(u_r, ui_s)
        u = ui_s[...]
        xsr_r[...] = _dot(u, bbr_r[...])
        xsi_r[...] = _dot(u, bbi_r[...])
        a_r, a_i = ta_r[0], ta_r[1]

        def local(step, carry):
            rows = _step_rows(step)
            xr, xi = _cmul_add(xsr_r[rows, :], xsi_r[rows, :], a_r, a_i, *carry)
            xsr_r[rows, :] = xr
            xsi_r[rows, :] = xi
            return xr, xi

        zero = jnp.zeros((SUBLANES, STRIP), F32)
        ends_r, ends_i = lax.fori_loop(0, SCAN_STEPS, local, (zero, zero), unroll=2)
        ent_r, ent_i, out_r, out_i = _segment_states(
            car_r[...], car_i[...], ends_r, ends_i, ta_r[2, 0:1, :], ta_r[3, 0:1, :], range(SUBLANES))
        car_r[...] = out_r
        car_i[...] = out_i

        def entering(step, _):
            rows = _step_rows(step)
            xr, xi = _cmul_add(xsr_r[rows, :], xsi_r[rows, :], tp_r[0, rows, :], tp_r[1, rows, :], ent_r, ent_i)
            xsr_r[rows, :] = xr
            xsi_r[rows, :] = xi
            return 0

        lax.fori_loop(0, SCAN_STEPS, entering, 0, unroll=4)
        yi_s[...] = _dot_t(xsr_r[...], ctr_r[...]) - _dot_t(xsi_r[...], cti_r[...]) + d_r[...] * u
        _deinterleave(yi_s, y_r)

    strip_mat = pl.BlockSpec((128, STRIP), lambda j, k: (j, 0))
    states = pl.BlockSpec((tc, STRIP), lambda j, k: (k, j))
    return _launch(
        body, comms, name="ssm_forward", grid=(N_STRIPS, t // tc),
        out_shape=(jax.ShapeDtypeStruct((t, STATE_COLS), F32), jax.ShapeDtypeStruct((t, STATE_COLS), F32),
                   jax.ShapeDtypeStruct((t, SSM_W), F32)),
        in_specs=[pl.BlockSpec((tc, 128), lambda j, k: (k, j)),
                  strip_mat, strip_mat, strip_mat, strip_mat,
                  pl.BlockSpec((1, 128), lambda j, k: (0, j)),
                  pl.BlockSpec((4, SUBLANES, STRIP), lambda j, k: (0, 0, j)),
                  pl.BlockSpec((2, tc, STRIP), lambda j, k: (0, 0, j))],
        out_specs=(states, states, pl.BlockSpec((tc, 128), lambda j, k: (k, j))),
        scratch_shapes=[pltpu.VMEM((tc, 128), F32), pltpu.VMEM((tc, 128), F32),
                        pltpu.VMEM((1, STRIP), F32), pltpu.VMEM((1, STRIP), F32)],
        sem=("parallel", "arbitrary"),
    )(proj, bbr, bbi, ctr, cti, d_skip, tab_a, tab_p)


def _shift_down(v, prev, n):
    row = lax.broadcasted_iota(jnp.int32, v.shape, 0)
    out = pltpu.roll(v, n, 0)
    for r in range(n):
        src = prev[SUBLANES - n + r:SUBLANES - n + r + 1, :]
        out = jnp.where(row == r, jnp.broadcast_to(src, v.shape), out)
    return out


def _shift_up(v, nxt, n):
    rows = v.shape[0]
    row = lax.broadcasted_iota(jnp.int32, v.shape, 0)
    out = pltpu.roll(v, rows - n, 0)
    for r in range(n):
        src = nxt[r:r + 1, :]
        out = jnp.where(row == rows - n + r, jnp.broadcast_to(src, v.shape), out)
    return out


def _conv3(q, q_prev, w):
    return w[2:3, :] * q + w[1:2, :] * _shift_down(q, q_prev, 1) + w[0:1, :] * _shift_down(q, q_prev, 2)


def _mixer_forward(x, proj, ya0, glu_w, glu_b, wso_st, conv_w8, wco_st, w_o, comms=(), tm=256):
    t = x.shape[0]
    hb = tm // SUBLANES

    def body(x_r, ya0_r, h_r, cg_r, bg_r, ga_r, gb_r, hp_r, cgp_r,
             glu_w_r, glu_b_r, wso_r, cw_r, wco_r, wo_r, xh_r, rstd_r, ya_r, yb_r):
        i = pl.program_id(0)
        g, _ = _gelu_parts(ya0_r[...])
        ya1 = g * _sigmoid(_dot(g, glu_w_r[...]) + glu_b_r[...])
        q = cg_r[...] * h_r[...]
        q_prev = jnp.where(i > 0, cgp_r[...] * hp_r[...], 0.0)
        yb0 = bg_r[...] * _conv3(q, q_prev, cw_r[...])
        for j in range(N_SHARDS):
            ya_r[:, 256 * j:256 * (j + 1)] = _dot(ya1, wso_r[j])
            yb_r[:, 256 * j:256 * (j + 1)] = _dot(yb0, wco_r[j])
        merged = _sigmoid(ga_r[...]) * ya_r[...] + _sigmoid(gb_r[...]) * yb_r[...]
        r1 = ALPHA * x_r[...] + _dot(merged, wo_r[...])
        mu = jnp.mean(r1, axis=-1, keepdims=True)
        cen = r1 - mu
        rstd = lax.rsqrt(jnp.mean(cen * cen, axis=-1, keepdims=True) + LN_EPS)
        xh_r[...] = cen * rstd
        rstd_r[...] = rstd

    def col(w, c):
        return pl.BlockSpec((tm, w), lambda i: (i, c))

    def prev(c):
        return pl.BlockSpec((SUBLANES, SSM_W), lambda i: (jnp.maximum(i * hb - 1, 0), c))

    return _launch(
        body, comms, name="mixer_forward", grid=(t // tm,),
        out_shape=(jax.ShapeDtypeStruct((t, D_MODEL), F32), jax.ShapeDtypeStruct((t, 1), F32),
                   jax.ShapeDtypeStruct((t, D_MODEL), F32), jax.ShapeDtypeStruct((t, D_MODEL), F32)),
        in_specs=[col(D_MODEL, 0), col(SSM_W, 0), col(SSM_W, 1), col(SSM_W, 2), col(SSM_W, 3),
                  col(D_MODEL, 2), col(D_MODEL, 3), prev(1), prev(2),
                  _const((SSM_W, SSM_W)), _const((1, SSM_W)), _const((N_SHARDS, SSM_W, 256)),
                  _const((SUBLANES, SSM_W)), _const((N_SHARDS, SSM_W, 256)), _const((D_MODEL, D_MODEL))],
        out_specs=(col(D_MODEL, 0), pl.BlockSpec((tm, 1), lambda i: (i, 0)), col(D_MODEL, 0), col(D_MODEL, 0)),
        sem=("parallel",),
    )(x, ya0, proj, proj, proj, proj, proj, proj, proj, glu_w, glu_b, wso_st, conv_w8, wco_st, w_o)


def _layer_norm_bwd(dxhat, xhat, rstd):
    m1 = jnp.mean(dxhat, axis=-1, keepdims=True)
    m2 = jnp.mean(dxhat * xhat, axis=-1, keepdims=True)
    return rstd * (dxhat - m1 - xhat * m2)


def _ffn_step(xhat1, rstd1, target, ln1_g, ln1_b, ln2_g, ln2_b, wg_st, wu_st, wd_st, tm=256):
    t = xhat1.shape[0]

    def body(xh_r, rstd_r, tgt_r, g1_r, b1_r, g2_r, b2_r, wg_r, wu_r, wd_r,
             loss_r, dr1_r, x1b_r, dr2b_r, hid_r, dhg_r, dhu_r, dg2_r, db2_r, dg1_r, db1_r,
             hg_s, hu_s):
        @pl.when(pl.program_id(0) == 0)
        def _():
            for r in (loss_r, dg2_r, db2_r, dg1_r, db1_r):
                r[...] = jnp.zeros_like(r)

        xhat1_v = xh_r[...]
        x1 = xhat1_v * g1_r[...] + b1_r[...]
        x1b = x1.astype(BF16)
        x1b_r[...] = x1b
        ffn = jnp.zeros((tm, D_MODEL), F32)
        for j in range(N_SHARDS):
            hg = lax.dot_general(x1b, wg_r[j], (((1,), (1,)), ((), ())), preferred_element_type=F32)
            hu = lax.dot_general(x1b, wu_r[j], (((1,), (1,)), ((), ())), preferred_element_type=F32)
            hg_s[j] = hg
            hu_s[j] = hu
            hid = (hg * _sigmoid(hg) * hu).astype(BF16)
            hid_r[j] = hid
            ffn = ffn + jnp.dot(hid, wd_r[j], preferred_element_type=F32)
        r2 = ALPHA * x1 + ffn
        mu = jnp.mean(r2, axis=-1, keepdims=True)
        cen = r2 - mu
        rstd2 = lax.rsqrt(jnp.mean(cen * cen, axis=-1, keepdims=True) + LN_EPS)
        xhat2 = cen * rstd2
        diff = (xhat2 * g2_r[...] + b2_r[...]) - tgt_r[...]
        loss_r[...] += 0.5 * jnp.sum(jnp.mean(diff * diff, axis=-1, keepdims=True), axis=0, keepdims=True)
        dy = diff * (1.0 / D_MODEL)
        dg2_r[...] += jnp.sum(dy * xhat2, axis=0, keepdims=True)
        db2_r[...] += jnp.sum(dy, axis=0, keepdims=True)
        dr2 = _layer_norm_bwd(dy * g2_r[...], xhat2, rstd2)
        dr2b = dr2.astype(BF16)
        dr2b_r[...] = dr2b
        dx1 = ALPHA * dr2
        for j in range(N_SHARDS):
            dhid = lax.dot_general(dr2b, wd_r[j], (((1,), (1,)), ((), ())), preferred_element_type=F32)
            hg = hg_s[j]
            hu = hu_s[j]
            sg = _sigmoid(hg)
            dhu = (dhid * (hg * sg)).astype(BF16)
            dhg = (dhid * hu * (sg * (1.0 + hg * (1.0 - sg)))).astype(BF16)
            dhg_r[j] = dhg
            dhu_r[j] = dhu
            dx1 = dx1 + jnp.dot(dhg, wg_r[j], preferred_element_type=F32)
            dx1 = dx1 + jnp.dot(dhu, wu_r[j], preferred_element_type=F32)
        dg1_r[...] += jnp.sum(dx1 * xhat1_v, axis=0, keepdims=True)
        db1_r[...] += jnp.sum(dx1, axis=0, keepdims=True)
        dr1_r[...] = _layer_norm_bwd(dx1 * g1_r[...], xhat1_v, rstd_r[...])

    tile = pl.BlockSpec((tm, D_MODEL), lambda i: (i, 0))
    hidden = pl.BlockSpec((N_SHARDS, tm, FFN_SHARD), lambda i: (0, i, 0))
    vec = _const((1, D_MODEL))
    hid_shape = jax.ShapeDtypeStruct((N_SHARDS, t, FFN_SHARD), BF16)
    vec_shape = jax.ShapeDtypeStruct((1, D_MODEL), F32)
    return pl.pallas_call(
        body, name="ffn_step", grid=(t // tm,),
        out_shape=(jax.ShapeDtypeStruct((1, 1), F32), jax.ShapeDtypeStruct((t, D_MODEL), F32),
                   jax.ShapeDtypeStruct((t, D_MODEL), BF16), jax.ShapeDtypeStruct((t, D_MODEL), BF16),
                   hid_shape, hid_shape, hid_shape, vec_shape, vec_shape, vec_shape, vec_shape),
        in_specs=[tile, pl.BlockSpec((tm, 1), lambda i: (i, 0)), tile, vec, vec, vec, vec,
                  _const((N_SHARDS, FFN_SHARD, D_MODEL)), _const((N_SHARDS, FFN_SHARD, D_MODEL)),
                  _const((N_SHARDS, FFN_SHARD, D_MODEL))],
        out_specs=(_const((1, 1)), tile, tile, tile, hidden, hidden, hidden, vec, vec, vec, vec),
        scratch_shapes=[pltpu.VMEM((N_SHARDS, tm, FFN_SHARD), F32), pltpu.VMEM((N_SHARDS, tm, FFN_SHARD), F32)],
        compiler_params=_params(("arbitrary",)),
    )(xhat1, rstd1, target, ln1_g, ln1_b, ln2_g, ln2_b, wg_st, wu_st, wd_st)


def _ffn_weight_grads(hiddens, tile_arr, name, comms=(), tk=512):
    n = len(hiddens)
    t = tile_arr.shape[0]

    def body(tile_r, *refs):
        @pl.when(pl.program_id(1) == 0)
        def _():
            for r in refs[n:]:
                r[...] = jnp.zeros_like(r)

        for a in range(n):
            refs[n + a][0] += _tdot(refs[a][0], tile_r[...])

    tile = pl.BlockSpec((tk, D_MODEL), lambda j, k: (k, 0))
    hidden = pl.BlockSpec((1, tk, FFN_SHARD), lambda j, k: (j, k, 0))
    row = pl.BlockSpec((1, FFN_SHARD, D_MODEL), lambda j, k: (j, 0, 0))
    return _launch(
        body, comms, name=name, grid=(N_SHARDS, t // tk),
        out_shape=(jax.ShapeDtypeStruct((N_SHARDS, FFN_SHARD, D_MODEL), F32),) * n,
        in_specs=[tile] + [hidden] * n, out_specs=(row,) * n, sem=("parallel", "arbitrary"),
    )(tile_arr, *hiddens)


def _mixer_backward(dr1, proj, ya0, ya, yb, glu_w, glu_b, wso_st, conv_w8, wco_st, w_o, comms=(), tm=256):
    t = dr1.shape[0]
    hb = tm // SUBLANES
    last_block = t // SUBLANES - 1

    def body(dr1_r, dr1n_r, ya0_r, ya_r, yb_r, h_r, cg_r, bg_r, ga_r, gb_r, hp_r, cgp_r, bgn_r, gbn_r,
             glu_w_r, glu_b_r, wso_r, cw_r, wco_r, wo_r,
             dya0_r, dproj_r, dbias_r, gwo_r, gwso_r, gwco_r, gglu_w_r, gglu_b_r, gconv_r):
        i = pl.program_id(0)

        @pl.when(i == 0)
        def _():
            for r in (dbias_r, gwo_r, gwso_r, gwco_r, gglu_w_r, gglu_b_r, gconv_r):
                r[...] = jnp.zeros_like(r)

        dr1_v = dr1_r[...]
        dmerged = _dot_t(dr1_v, wo_r[...])
        sa = _sigmoid(ga_r[...])
        sb = _sigmoid(gb_r[...])
        ya_v = ya_r[...]
        yb_v = yb_r[...]
        gwo_r[...] += _tdot(sa * ya_v + sb * yb_v, dr1_v)
        dya = dmerged * sa
        dyb = dmerged * sb
        dga = dmerged * ya_v * (sa * (1.0 - sa))
        dgb = dmerged * yb_v * (sb * (1.0 - sb))

        g, gelu_grad = _gelu_parts(ya0_r[...])
        s1 = _sigmoid(_dot(g, glu_w_r[...]) + glu_b_r[...])
        ya1 = g * s1
        dya1 = jnp.zeros((tm, SSM_W), F32)
        for j in range(N_SHARDS):
            dya_j = dya[:, 256 * j:256 * (j + 1)]
            gwso_r[j] += _tdot(ya1, dya_j)
            dya1 = dya1 + _dot_t(dya_j, wso_r[j])
        dz1 = dya1 * g * (s1 * (1.0 - s1))
        gglu_b_r[...] += jnp.sum(dz1, axis=0, keepdims=True)
        gglu_w_r[...] += _tdot(g, dz1)
        dya0_r[...] = (dya1 * s1 + _dot_t(dz1, glu_w_r[...])) * gelu_grad

        cw = cw_r[...]
        h = h_r[...]
        cg = cg_r[...]
        bg = bg_r[...]
        q = cg * h
        q_prev = jnp.where(i > 0, cgp_r[...] * hp_r[...], 0.0)
        q1 = _shift_down(q, q_prev, 1)
        q2 = _shift_down(q, q_prev, 2)
        z = cw[2:3, :] * q + cw[1:2, :] * q1 + cw[0:1, :] * q2
        yb0 = bg * z
        dyb0 = jnp.zeros((tm, SSM_W), F32)
        for j in range(N_SHARDS):
            dyb_j = dyb[:, 256 * j:256 * (j + 1)]
            gwco_r[j] += _tdot(yb0, dyb_j)
            dyb0 = dyb0 + _dot_t(dyb_j, wco_r[j])
        dbg = dyb0 * z
        dz = dyb0 * bg
        dyb_n = _dot_t(dr1n_r[...], wo_r[...]) * _sigmoid(gbn_r[...])
        dyb0_n = jnp.zeros((SUBLANES, SSM_W), F32)
        for j in range(N_SHARDS):
            dyb0_n = dyb0_n + _dot_t(dyb_n[:, 256 * j:256 * (j + 1)], wco_r[j])
        dz_next = jnp.where(i < pl.num_programs(0) - 1, dyb0_n * bgn_r[...], 0.0)
        dq = cw[2:3, :] * dz + cw[1:2, :] * _shift_up(dz, dz_next, 1) + cw[0:1, :] * _shift_up(dz, dz_next, 2)
        gconv_r[0:1, :] += jnp.sum(dz * q2, axis=0, keepdims=True)
        gconv_r[1:2, :] += jnp.sum(dz * q1, axis=0, keepdims=True)
        gconv_r[2:3, :] += jnp.sum(dz * q, axis=0, keepdims=True)
        dh = dq * cg
        dcg = dq * h

        dproj_r[:, 0:512] = jnp.zeros((tm, SSM_W), BF16)
        pieces = ((512, dh), (1024, dcg), (1536, dbg), (2048, dga), (3072, dgb))
        for off, val in pieces:
            w = val.shape[1]
            dproj_r[:, off:off + w] = val.astype(BF16)
            dbias_r[:, off:off + w] += jnp.sum(val, axis=0, keepdims=True)

    def col(w, c):
        return pl.BlockSpec((tm, w), lambda i: (i, c))

    def prev(c):
        return pl.BlockSpec((SUBLANES, SSM_W), lambda i: (jnp.maximum(i * hb - 1, 0), c))

    def nxt(w, c):
        return pl.BlockSpec((SUBLANES, w), lambda i: (jnp.minimum((i + 1) * hb, last_block), c))

    sh = jax.ShapeDtypeStruct
    return _launch(
        body, comms, name="mixer_backward", grid=(t // tm,),
        out_shape=(sh((t, SSM_W), F32), sh((t, IN_COLS), BF16), sh((1, IN_COLS), F32),
                   sh((D_MODEL, D_MODEL), F32), sh((N_SHARDS, SSM_W, 256), F32), sh((N_SHARDS, SSM_W, 256), F32),
                   sh((SSM_W, SSM_W), F32), sh((1, SSM_W), F32), sh((SUBLANES, SSM_W), F32)),
        in_specs=[col(D_MODEL, 0), nxt(D_MODEL, 0), col(SSM_W, 0), col(D_MODEL, 0), col(D_MODEL, 0),
                  col(SSM_W, 1), col(SSM_W, 2), col(SSM_W, 3), col(D_MODEL, 2), col(D_MODEL, 3),
                  prev(1), prev(2), nxt(SSM_W, 3), nxt(D_MODEL, 3),
                  _const((SSM_W, SSM_W)), _const((1, SSM_W)), _const((N_SHARDS, SSM_W, 256)),
                  _const((SUBLANES, SSM_W)), _const((N_SHARDS, SSM_W, 256)), _const((D_MODEL, D_MODEL))],
        out_specs=(col(SSM_W, 0), col(IN_COLS, 0), _const((1, IN_COLS)),
                   _const((D_MODEL, D_MODEL)), _const((N_SHARDS, SSM_W, 256)), _const((N_SHARDS, SSM_W, 256)),
                   _const((SSM_W, SSM_W)), _const((1, SSM_W)), _const((SUBLANES, SSM_W))),
        sem=("arbitrary",),
    )(dr1, dr1, ya0, ya, yb, proj, proj, proj, proj, proj, proj, proj, proj, proj,
      glu_w, glu_b, wso_st, conv_w8, wco_st, w_o)


def _cmulc_add(xr, xi, mr, mi, sr, si):
    return xr + (mr * sr + mi * si), xi + (mr * si - mi * sr)


def _ssm_backward(dya0, proj, xsr, xsi, bbr, bbi, ctr, cti, d_skip, tab_a, tab_p, dproj, comms=(), tc=SCAN_CHUNK):
    t = proj.shape[0]
    nk = t // tc

    def body(dy_r, u_r, xsr_r, xsi_r, bbr_r, bbi_r, ctr_r, cti_r, d_r, ta_r, tp_r, dproj_any,
             du_r, dus_r, gbbr_r, gbbi_r, gctr_r, gcti_r, glbr_r, glbi_r, gd_r,
             gr_s, gi_s, dyi_s, ui_s, dui_s, dun_s, car_r, car_i):
        del dproj_any

        @pl.when(pl.program_id(1) == 0)
        def _():
            for r in (car_r, car_i, dus_r, gbbr_r, gbbi_r, gctr_r, gcti_r, glbr_r, glbi_r, gd_r):
                r[...] = jnp.zeros_like(r)

        _interleave(dy_r, dyi_s)
        _interleave(u_r, ui_s)
        dy = dyi_s[...]
        u = ui_s[...]
        gr_s[...] = _dot(dy, ctr_r[...])
        gi_s[...] = -_dot(dy, cti_r[...])
        a_r, a_i = ta_r[0], ta_r[1]

        def local(n, carry):
            rows = _step_rows(SCAN_STEPS - 1 - n)
            gr, gi = _cmulc_add(gr_s[rows, :], gi_s[rows, :], a_r, a_i, *carry)
            gr_s[rows, :] = gr
            gi_s[rows, :] = gi
            return gr, gi

        zero = jnp.zeros((SUBLANES, STRIP), F32)
        ends_r, ends_i = lax.fori_loop(0, SCAN_STEPS, local, (zero, zero), unroll=2)
        ent_r, ent_i, out_r, out_i = _segment_states(
            car_r[...], car_i[...], ends_r, ends_i, ta_r[2, 0:1, :], -ta_r[3, 0:1, :], range(SUBLANES - 1, -1, -1))
        car_r[...] = out_r
        car_i[...] = out_i

        def entering(n, carry):
            gnr, gni, ar, ai = carry
            rows = _step_rows(SCAN_STEPS - 1 - n)
            power = _step_rows(n)
            gr, gi = _cmulc_add(gr_s[rows, :], gi_s[rows, :], tp_r[0, power, :], tp_r[1, power, :], ent_r, ent_i)
            gr_s[rows, :] = gr
            gi_s[rows, :] = gi
            xr = xsr_r[rows, :]
            xi = xsi_r[rows, :]
            return gr, gi, ar + (xr * gnr + xi * gni), ai + (xr * gni - xi * gnr)

        _, _, ar, ai = lax.fori_loop(0, SCAN_STEPS, entering, (ent_r, ent_i, zero, zero), unroll=2)
        glbr_r[...] += ar
        glbi_r[...] += ai
        gr = gr_s[...]
        gi = gi_s[...]
        dui_s[...] = _dot_t(gr, bbr_r[...]) + _dot_t(gi, bbi_r[...]) + d_r[...] * dy
        _deinterleave(dui_s, dun_s)
        du = dun_s[...]
        du_r[...] = du.astype(BF16)
        dus_r[...] += jnp.sum(du, axis=0, keepdims=True)
        gd_r[...] += jnp.sum(dy * u, axis=0, keepdims=True)
        gbbr_r[...] += _tdot(u, gr)
        gbbi_r[...] += _tdot(u, gi)
        gctr_r[...] += _tdot(dy, xsr_r[...])
        gcti_r[...] -= _tdot(dy, xsi_r[...])

    def rev(w):
        return pl.BlockSpec((tc, w), lambda j, k: (nk - 1 - k, j))

    strip_mat = pl.BlockSpec((128, STRIP), lambda j, k: (j, 0))
    vec = pl.BlockSpec((1, 128), lambda j, k: (0, j))
    lbacc = pl.BlockSpec((SUBLANES, STRIP), lambda j, k: (0, j))
    sh = jax.ShapeDtypeStruct
    return _launch(
        body, comms, name="ssm_backward", grid=(N_STRIPS, nk),
        out_shape=(sh((t, IN_COLS), BF16), sh((1, SSM_W), F32),
                   sh((SSM_W, STRIP), F32), sh((SSM_W, STRIP), F32), sh((SSM_W, STRIP), F32), sh((SSM_W, STRIP), F32),
                   sh((SUBLANES, STATE_COLS), F32), sh((SUBLANES, STATE_COLS), F32), sh((1, SSM_W), F32)),
        in_specs=[rev(128), rev(128), rev(STRIP), rev(STRIP),
                  strip_mat, strip_mat, strip_mat, strip_mat, vec,
                  pl.BlockSpec((4, SUBLANES, STRIP), lambda j, k: (0, 0, j)),
                  pl.BlockSpec((2, tc, STRIP), lambda j, k: (0, 0, j)), ANY],
        out_specs=(rev(128), vec, strip_mat, strip_mat, strip_mat, strip_mat, lbacc, lbacc, vec),
        scratch_shapes=[pltpu.VMEM((tc, STRIP), F32), pltpu.VMEM((tc, STRIP), F32)]
        + [pltpu.VMEM((tc, 128), F32)] * 4 + [pltpu.VMEM((1, STRIP), F32)] * 2,
        aliases={11: 0}, sem=("parallel", "arbitrary"),
    )(dya0, proj, xsr, xsi, bbr, bbi, ctr, cti, d_skip, tab_a, tab_p, dproj)


def _input_grad(dr1, dproj, w_in_st, comms=(), tm=256):
    t = dr1.shape[0]

    def body(dr1_r, dp_r, w_r, dx_r):
        acc = ALPHA * dr1_r[...]
        for j in range(N_SHARDS):
            acc = acc + lax.dot_general(dp_r[:, D_MODEL * j:D_MODEL * (j + 1)], w_r[j],
                                        (((1,), (1,)), ((), ())), preferred_element_type=F32)
        dx_r[...] = acc

    (dx,), sent = _launch(
        body, comms, name="input_grad", grid=(t // tm,),
        out_shape=(jax.ShapeDtypeStruct((t, D_MODEL), F32),),
        in_specs=[pl.BlockSpec((tm, D_MODEL), lambda i: (i, 0)), pl.BlockSpec((tm, IN_COLS), lambda i: (i, 0)),
                  _const((N_SHARDS, D_MODEL, D_MODEL))],
        out_specs=(pl.BlockSpec((tm, D_MODEL), lambda i: (i, 0)),),
        sem=("parallel",),
    )(dr1, dproj, w_in_st)
    return dx, sent


def _in_weight_grad(x, dproj, comms=(), tk=512):
    t = x.shape[0]

    def body(x_r, dp_r, gw_r):
        @pl.when(pl.program_id(1) == 0)
        def _():
            gw_r[...] = jnp.zeros_like(gw_r)

        gw_r[0] += _tdot(x_r[...], dp_r[...])

    (g_w_in,), sent = _launch(
        body, comms, name="in_weight_grad", grid=(N_SHARDS, t // tk),
        out_shape=(jax.ShapeDtypeStruct((N_SHARDS, D_MODEL, D_MODEL), F32),),
        in_specs=[pl.BlockSpec((tk, D_MODEL), lambda j, k: (k, 0)), pl.BlockSpec((tk, D_MODEL), lambda j, k: (k, j))],
        out_specs=(pl.BlockSpec((1, D_MODEL, D_MODEL), lambda j, k: (j, 0, 0)),),
        sem=("parallel", "arbitrary"),
    )(x, dproj)
    return g_w_in, sent


MIXER_W = ("glu_w", "w_ssm_out", "w_conv_out", "w_o")
FFN_W = ("w_gate", "w_up", "w_down")


def _device_step(x, target, small, shards, c_arr, me_arr):
    lr, li = small["ssm_lambda_re"][0], small["ssm_lambda_im"][0]
    ldt = small["ssm_log_dt"][0][:, None]
    rep16 = lambda a: jnp.broadcast_to(a[:, None, :], (N_GROUPS, GROUP_C, a.shape[-1])).reshape(SSM_W, a.shape[-1])
    lr16, li16 = rep16(lr), rep16(li)
    ldt16 = rep16(jnp.broadcast_to(ldt, (N_GROUPS, N_STATE)))
    brt = small["ssm_b_re"][0].transpose(0, 2, 1).reshape(SSM_W, N_STATE)
    bit = small["ssm_b_im"][0].transpose(0, 2, 1).reshape(SSM_W, N_STATE)
    cre = small["ssm_c_re"][0].reshape(SSM_W, N_STATE)
    cim = small["ssm_c_im"][0].reshape(SSM_W, N_STATE)
    disc = (lr, li, ldt, lr16, li16, ldt16, brt, bit)

    pwr, pwi, bbr, bbi, ctr, cti = _ssm_prepare(*disc, cre, cim)
    tab_a, tab_p = _scan_tables(pwr, pwi)

    first_sh = [shards[n] for n in MIXER_W + FFN_W[:1]]
    second_sh = [shards[n] for n in FFN_W[1:]]
    (w_in_st,) = _gather_weights([shards["w_in"]])
    proj, (arrived,) = _in_proj(x, w_in_st, small["b_in"], comms=[_gather_ici(first_sh, [shards["conv_w"]])])
    (xsr, xsi, ya0), (second_part, first_st) = _ssm_forward(
        proj, bbr, bbi, ctr, cti, small["ssm_d"], tab_a, tab_p,
        comms=[_gather_ici(second_sh), _gather_d2d(arrived[:len(first_sh)], first_sh)])
    glu_st, wso_st, wco_st, wo_st, wg_st = (_own_slot(st, sh) for st, sh in zip(first_st, first_sh))
    conv_st = _own_slot(arrived[len(first_sh)], shards["conv_w"])
    conv_w8 = jnp.pad(conv_st[:, :3, :].transpose(1, 0, 2).reshape(3, SSM_W), ((0, SUBLANES - 3), (0, 0)))
    w_o = wo_st.reshape(D_MODEL, D_MODEL)
    glu_w = glu_st.reshape(SSM_W, SSM_W)
    (xhat1, rstd1, ya, yb), (second_st,) = _mixer_forward(
        x, proj, ya0, glu_w, small["glu_b"], wso_st, conv_w8, wco_st, w_o, comms=[_gather_d2d(second_part, second_sh)])
    wu_st, wd_st = (_own_slot(st, sh) for st, sh in zip(second_st, second_sh))
    (loss, dr1, x1b, dr2b, hid, dhg, dhu, g_ln2_g, g_ln2_b, g_ln1_g, g_ln1_b) = _ffn_step(
        xhat1, rstd1, target, small["ln1_g"], small["ln1_b"], small["ln2_g"], small["ln2_b"], wg_st, wu_st, wd_st)

    add_halves = lambda gs, rs: _per_shape(lambda a, b: _add_own_half(a, b, c_arr), list(gs), list(rs))
    sum_chips = lambda owns, slots: _per_shape(lambda a, b: _sum_chips(a, b, me_arr), list(owns), list(slots))
    g_down, _ = _ffn_weight_grads([hid], dr2b, "ffn_down_grad")
    g_gu, (got_down,) = _ffn_weight_grads([dhg, dhu], x1b, "ffn_gate_up_grads", comms=[_swap_comm(g_down)])
    chip_down = add_halves(g_down, got_down)
    (dya0, dproj, dbias, g_wo, g_wso, g_wco, g_glu_w, g_glu_b, g_conv8), (slots_down, got_gu) = _mixer_backward(
        dr1, proj, ya0, ya, yb, glu_w, small["glu_b"], wso_st, conv_w8, wco_st, w_o,
        comms=[_scatter_comm(chip_down), _swap_comm(g_gu)])
    chip_gu = add_halves(g_gu, got_gu)
    g_mix = [g_glu_w.reshape(N_SHARDS, 128, SSM_W), g_wso, g_wco, g_wo.reshape(N_SHARDS, 256, D_MODEL)]
    (dproj, dus, gbbr, gbbi, gctr, gcti, glbr, glbi, g_d), (slots_gu, got_mix) = _ssm_backward(
        dya0, proj, xsr, xsi, bbr, bbi, ctr, cti, small["ssm_d"], tab_a, tab_p, dproj,
        comms=[_scatter_comm(chip_gu), _swap_comm(g_mix)])
    halves_ffn = sum_chips(chip_gu + chip_down, list(slots_gu) + list(slots_down))
    chip_mix = add_halves(g_mix, got_mix)
    g_lr, g_li, g_ldt, g_brt, g_bit, g_cre, g_cim = _ssm_param_grads(
        *disc, glbr.reshape(SUBLANES, N_GROUPS, N_STATE), glbi.reshape(SUBLANES, N_GROUPS, N_STATE),
        gbbr, gbbi, gctr, gcti)
    g_w_in, (others_ffn, slots_mix) = _in_weight_grad(
        x, dproj, comms=[_send_comm(halves_ffn), _scatter_comm(chip_mix)])
    halves_mix = sum_chips(chip_mix, slots_mix)
    dx, _ = _input_grad(dr1, dproj, w_in_st)

    g_conv = jnp.pad(g_conv8[:3].reshape(3, N_SHARDS, 128).transpose(1, 0, 2), ((0, 0), (0, SUBLANES - 3), (0, 0)))
    pieces = [dus, dbias[:, SSM_W:], g_lr, g_li, g_ldt, g_brt, g_bit, g_cre, g_cim, g_d, g_glu_b,
              g_ln1_g, g_ln1_b, g_ln2_g, g_ln2_b, loss]
    flat = jnp.concatenate([p.reshape(-1) for p in pieces])
    g_packed = jnp.pad(flat, (0, PACKED_ROWS * 128 - flat.shape[0])).reshape(PACKED_ROWS, 128)
    ((got_w, got_conv, got_packed),) = _standalone([_swap_comm([g_w_in], [g_conv, g_packed])], "swap_with_sibling")
    (chip_w,) = add_halves([g_w_in], [got_w])
    chip_conv, chip_packed = _small_pair_sums([g_conv, g_packed], [got_conv, got_packed])
    ((slots_w, slots_conv, slots_packed),) = _standalone(
        [_scatter_comm([chip_w, chip_conv], [chip_packed])], "scatter_to_chips")
    (halves_w,) = sum_chips([chip_w], [slots_w])
    conv_total, packed_total = _small_totals(me_arr, chip_conv, slots_conv, chip_packed, slots_packed)
    (others_rest,) = _standalone([_send_comm([halves_w] + halves_mix)], "send_to_sibling")

    pairs = dict(zip(FFN_W, zip(halves_ffn, others_ffn)))
    pairs.update(zip(("w_in",) + MIXER_W, zip([halves_w] + halves_mix, others_rest)))
    return dx, pairs, conv_total, packed_total


PACKED_ROWS = 1136
PACKED_LAYOUT = (("b_in", IN_COLS), ("ssm_lambda_re", STATE_COLS), ("ssm_lambda_im", STATE_COLS),
                 ("ssm_log_dt", N_GROUPS), ("ssm_b_re", SSM_W * N_STATE), ("ssm_b_im", SSM_W * N_STATE),
                 ("ssm_c_re", SSM_W * N_STATE), ("ssm_c_im", SSM_W * N_STATE), ("ssm_d", SSM_W), ("glu_b", SSM_W),
                 ("ln1_g", D_MODEL), ("ln1_b", D_MODEL), ("ln2_g", D_MODEL), ("ln2_b", D_MODEL), ("loss", 1))


def _unpack_small(packed):
    flat = packed.reshape(-1)
    out, off = {}, 0
    for name, size in PACKED_LAYOUT:
        out[name] = flat[off:off + size]
        off += size
    for name in ("ssm_b_re", "ssm_b_im"):
        out[name] = out[name].reshape(N_GROUPS, GROUP_C, N_STATE).transpose(0, 2, 1)[None]
    for name in ("ssm_c_re", "ssm_c_im"):
        out[name] = out[name].reshape(1, N_GROUPS, GROUP_C, N_STATE)
    for name in ("ssm_lambda_re", "ssm_lambda_im"):
        out[name] = out[name].reshape(1, N_GROUPS, N_STATE)
    for name in ("b_in", "ssm_log_dt", "ssm_d", "glu_b", "ln1_g", "ln1_b", "ln2_g", "ln2_b"):
        out[name] = out[name][None]
    return out


BIG = ("w_in", "glu_w", "w_ssm_out", "w_conv_out", "w_o", "w_gate", "w_up", "w_down")
SMALL = ("b_in", "ssm_lambda_re", "ssm_lambda_im", "ssm_log_dt", "ssm_b_re", "ssm_b_im", "ssm_c_re", "ssm_c_im",
         "ssm_d", "glu_b", "ln1_g", "ln1_b", "ln2_g", "ln2_b")
WEIGHTS = ("w_in", "b_in", "ssm_lambda_re", "ssm_lambda_im", "ssm_log_dt", "ssm_b_re", "ssm_b_im", "ssm_c_re",
           "ssm_c_im", "ssm_d", "glu_w", "glu_b", "w_ssm_out", "conv_w", "w_conv_out", "w_o", "ln1_g", "ln1_b",
           "w_gate", "w_up", "w_down", "ln2_g", "ln2_b")


def _place():
    x, y, c = lax.axis_index("x"), lax.axis_index("y"), lax.axis_index("c")
    chips = [(1 - x, y), (x, 1 - y), (1 - x, 1 - y)]
    return x, y, c, chips


def _shard_of(chip):
    return 2 * chip[0] + chip[1]


def _remote(src, dst, send_sem, recv_sem, to):
    return pltpu.make_async_remote_copy(src_ref=src, dst_ref=dst, send_sem=send_sem, recv_sem=recv_sem,
                                        device_id=to, device_id_type=MESH)


def _half_rows(shard, which):
    r2 = shard.shape[0] // 2
    return pl.ds(pl.multiple_of(which * r2, 16), r2)


def _own_slot(stack, shard):
    me = _shard_of((lax.axis_index("x"), lax.axis_index("y")))
    return lax.dynamic_update_slice(stack, shard[None], (me,) + (0,) * shard.ndim)


def _gather_ici(halved, whole=()):
    shards = list(halved) + list(whole)
    nh = len(halved)

    def copies(src, dst, sems):
        send_sem, recv_sem = sems
        x, y, c, chips = _place()
        me = _shard_of((x, y))
        out = []
        for a in range(len(shards)):
            for k, chip in enumerate(chips):
                if a < nh:
                    rows = _half_rows(shards[a], c)
                    out.append(_remote(src[a].at[rows], dst[a].at[me, rows], send_sem.at[a, k], recv_sem.at[a, k],
                                       (*chip, c)))
                else:
                    out.append(_remote(src[a], dst[a].at[me], send_sem.at[a, k], recv_sem.at[a, k], (*chip, c)))
        return out

    n = len(shards)
    return _Comm(shards, [jax.ShapeDtypeStruct((N_SHARDS,) + s.shape, s.dtype) for s in shards],
                 [pltpu.SemaphoreType.DMA((n, 3))] * 2, copies)


def _gather_d2d(stacks, shards):
    def copies(src, dst, sems):
        del src
        send_sem, recv_sem = sems
        x, y, c, chips = _place()
        out = []
        for a in range(len(stacks)):
            for k, chip in enumerate(chips):
                rows = dst[a].at[_shard_of(chip), _half_rows(shards[a], c)]
                out.append(_remote(rows, rows, send_sem.at[a, k], recv_sem.at[a, k], (x, y, 1 - c)))
        return out

    n = len(stacks)
    return _Comm(stacks, [jax.ShapeDtypeStruct(s.shape, s.dtype) for s in stacks],
                 [pltpu.SemaphoreType.DMA((n, 3))] * 2, copies, aliased=True)


def _standalone(comms, name):
    return _launch(None, comms, name=name, grid=(), in_specs=[], out_specs=(), out_shape=())()[1]


def _gather_weights(shards):
    n = len(shards)

    def body(*refs):
        src, dst = refs[:n], refs[n:2 * n]
        send_sem, recv_sem, fsend_sem, frecv_sem = refs[2 * n:]
        x, y, c, chips = _place()
        me = _shard_of((x, y))
        sibling = (x, y, 1 - c)
        sends = []
        for a in range(n):
            mine = _half_rows(shards[a], c)
            for k, chip in enumerate(chips):
                cp = _remote(src[a].at[mine], dst[a].at[me, mine], send_sem.at[a, k], recv_sem.at[a, k], (*chip, c))
                cp.start()
                sends.append(cp)
        for a in range(n):
            for k, chip in enumerate(chips):
                rows = dst[a].at[_shard_of(chip), _half_rows(shards[a], c)]
                _remote(rows, rows, send_sem.at[a, k], recv_sem.at[a, k], sibling).wait_recv()
                cp = _remote(rows, rows, fsend_sem.at[a, k], frecv_sem.at[a, k], sibling)
                cp.start()
                sends.append(cp)
        for a in range(n):
            for k, chip in enumerate(chips):
                rows = dst[a].at[_shard_of(chip), _half_rows(shards[a], 1 - c)]
                _remote(rows, rows, fsend_sem.at[a, k], frecv_sem.at[a, k], sibling).wait_recv()
        for cp in sends:
            cp.wait_send()

    stacks = pl.pallas_call(
        body, name="gather_weights",
        out_shape=tuple(jax.ShapeDtypeStruct((N_SHARDS,) + s.shape, s.dtype) for s in shards),
        in_specs=[ANY] * n, out_specs=(ANY,) * n,
        scratch_shapes=[pltpu.SemaphoreType.DMA((n, 3))] * 4,
    )(*shards)
    return [_own_slot(st, sh) for st, sh in zip(stacks, shards)]


def _swap_comm(big, small=()):
    nb, n = len(big), len(big) + len(small)
    arrays = list(big) + list(small)

    def copies(src, dst, sems):
        send_sem, recv_sem = sems
        x, y, c, _ = _place()
        out = []
        for a in range(n):
            if a < nb:
                r2 = arrays[a].shape[1] // 2
                part = src[a].at[:, pl.ds(pl.multiple_of((1 - c) * r2, SUBLANES), r2), :]
            else:
                part = src[a]
            out.append(_remote(part, dst[a], send_sem.at[a], recv_sem.at[a], (x, y, 1 - c)))
        return out

    out_shape = [jax.ShapeDtypeStruct((N_SHARDS, g.shape[1] // 2, g.shape[2]), g.dtype) for g in big]
    out_shape += [jax.ShapeDtypeStruct(g.shape, g.dtype) for g in small]
    return _Comm(arrays, out_shape, [pltpu.SemaphoreType.DMA((n,))] * 2, copies)


def _scatter_comm(slabbed, small=()):
    ns, n = len(slabbed), len(slabbed) + len(small)
    arrays = list(slabbed) + list(small)

    def copies(src, dst, sems):
        send_sem, recv_sem = sems
        _, _, c, chips = _place()
        out = []
        for a in range(n):
            for k, chip in enumerate(chips):
                part = src[a].at[_shard_of(chip)] if a < ns else src[a]
                out.append(_remote(part, dst[a].at[k], send_sem.at[a, k], recv_sem.at[a, k], (*chip, c)))
        return out

    out_shape = [jax.ShapeDtypeStruct((3,) + g.shape[1:], g.dtype) for g in slabbed]
    out_shape += [jax.ShapeDtypeStruct((3,) + g.shape, g.dtype) for g in small]
    return _Comm(arrays, out_shape, [pltpu.SemaphoreType.DMA((n, 3))] * 2, copies)


def _send_comm(arrays):
    n = len(arrays)

    def copies(src, dst, sems):
        send_sem, recv_sem = sems
        x, y, c, _ = _place()
        return [_remote(src[a], dst[a], send_sem.at[a], recv_sem.at[a], (x, y, 1 - c)) for a in range(n)]

    return _Comm(arrays, [jax.ShapeDtypeStruct(h.shape, h.dtype) for h in arrays],
                 [pltpu.SemaphoreType.DMA((n,))] * 2, copies)


def _row_chunk(rows):
    for cand in (256, 176, 128, 64):
        if rows % cand == 0:
            return cand
    return rows


def _per_shape(fn, *lists):
    groups = {}
    for i, items in enumerate(zip(*lists)):
        groups.setdefault(tuple(a.shape for a in items), []).append(i)
    out = [None] * len(lists[0])
    for idx in groups.values():
        for i, r in zip(idx, fn(*([lst[i] for i in idx] for lst in lists))):
            out[i] = r
    return out


def _add_own_half(stacks, receiveds, c):
    n = len(stacks)
    _, r2, cols = receiveds[0].shape

    def body(c_ref, *refs):
        del c_ref
        for a in range(n):
            refs[2 * n + a][...] = (refs[a][...] + refs[n + a][...]).astype(BF16)

    own = pl.BlockSpec((1, r2, cols), lambda s, c_ref: (s, c_ref[0], 0))
    got = pl.BlockSpec((1, r2, cols), lambda s, c_ref: (s, 0, 0))
    return pl.pallas_call(
        body, name="add_own_half",
        grid_spec=pltpu.PrefetchScalarGridSpec(
            num_scalar_prefetch=1, grid=(N_SHARDS,), in_specs=[own] * n + [got] * n, out_specs=(got,) * n),
        out_shape=(jax.ShapeDtypeStruct(receiveds[0].shape, BF16),) * n,
        compiler_params=_params(("parallel",)),
    )(c, *stacks, *receiveds)


def _chip_order_sum(me, own, s0, s1, s2):
    terms = []
    for s in range(N_SHARDS):
        d = jnp.bitwise_xor(me, s)
        terms.append(jnp.where(d == 0, own, jnp.where(d == 2, s0, jnp.where(d == 1, s1, s2))))
    return ((terms[0] + terms[1]) + terms[2]) + terms[3]


def _sum_chips(own_stacks, slots, me):
    n = len(slots)
    _, rows, cols = slots[0].shape
    rc = _row_chunk(rows)

    def body(me_ref, *refs):
        del me_ref
        for a in range(n):
            own_r, s_r = refs[a], refs[n + a]
            refs[2 * n + a][...] = (((own_r[0].astype(F32) + s_r[0].astype(F32)) + s_r[1].astype(F32))
                                    + s_r[2].astype(F32))

    own = pl.BlockSpec((1, rc, cols), lambda i, me_ref: (me_ref[0], i, 0))
    three = pl.BlockSpec((3, rc, cols), lambda i, me_ref: (0, i, 0))
    total = pl.BlockSpec((rc, cols), lambda i, me_ref: (i, 0))
    return pl.pallas_call(
        body, name="sum_chips",
        grid_spec=pltpu.PrefetchScalarGridSpec(
            num_scalar_prefetch=1, grid=(rows // rc,), in_specs=[own] * n + [three] * n, out_specs=(total,) * n),
        out_shape=(jax.ShapeDtypeStruct((rows, cols), F32),) * n,
        compiler_params=_params(("parallel",)),
    )(me, *own_stacks, *slots)


def _small_pair_sums(mine, theirs):
    n = len(mine)

    def body(*refs):
        for a in range(n):
            refs[2 * n + a][...] = refs[a][...] + refs[n + a][...]

    vm = pl.BlockSpec(memory_space=pltpu.VMEM)
    return pl.pallas_call(
        body, name="small_pair_sums", out_shape=tuple(jax.ShapeDtypeStruct(g.shape, g.dtype) for g in mine),
        in_specs=[vm] * (2 * n), out_specs=(vm,) * n,
        compiler_params=pltpu.CompilerParams(vmem_limit_bytes=VMEM_LIMIT),
    )(*mine, *theirs)


def _adam_math(w, g, m, v):
    m = ADAM_B1 * m + (1.0 - ADAM_B1) * g
    v = ADAM_B2 * v + (1.0 - ADAM_B2) * (g * g)
    m_hat = m / (1.0 - ADAM_B1 ** ADAM_STEP)
    v_hat = v / (1.0 - ADAM_B2 ** ADAM_STEP)
    delta = -ADAM_LR * (m_hat / (jnp.sqrt(v_hat) + ADAM_EPS) + ADAM_WD * w)
    return delta, m, v


def _small_totals(me, conv_stack, conv_slots, packed, packed_slots):
    def body(me_ref, cs_r, cslot_r, p_r, pslot_r, conv_r, tot_r):
        me_ = me_ref[0]
        conv_r[...] = _chip_order_sum(me_, cs_r[me_], cslot_r[0], cslot_r[1], cslot_r[2])
        tot_r[...] = _chip_order_sum(me_, p_r[...], pslot_r[0], pslot_r[1], pslot_r[2])

    vm = pl.BlockSpec(memory_space=pltpu.VMEM)
    return pl.pallas_call(
        body, name="small_totals",
        out_shape=(jax.ShapeDtypeStruct(conv_stack.shape[1:], F32), jax.ShapeDtypeStruct(packed.shape, F32)),
        in_specs=[pl.BlockSpec(memory_space=pltpu.SMEM)] + [vm] * 4, out_specs=(vm, vm),
    )(me, conv_stack, conv_slots, packed, packed_slots)


def _adam_small(gs, ws, ms, vs):
    n = len(gs)

    def body(*refs):
        for a in range(n):
            g_r, w_r, m_r, v_r = (refs[i * n + a] for i in range(4))
            d_r, nm_r, nv_r = (refs[(4 + i) * n + a] for i in range(3))
            d_r[...], nm_r[...], nv_r[...] = _adam_math(w_r[...], g_r[...], m_r[...], v_r[...])

    vm = pl.BlockSpec(memory_space=pltpu.VMEM)
    shapes = tuple(jax.ShapeDtypeStruct(w.shape, F32) for w in ws)
    out = pl.pallas_call(
        body, name="adam_small", out_shape=shapes * 3, in_specs=[vm] * (4 * n), out_specs=(vm,) * (3 * n),
        compiler_params=pltpu.CompilerParams(vmem_limit_bytes=VMEM_LIMIT),
    )(*gs, *ws, *ms, *vs)
    return out[:n], out[n:2 * n], out[2 * n:]


def _adam_big(ws, mines, others, ms, vs, c):
    n = len(ws)
    r2, cols = mines[0].shape
    rc = _row_chunk(r2)
    nch = r2 // rc

    def body(c_ref, *refs):
        mine_is_here = pl.program_id(0) == c_ref[0]
        for a in range(n):
            w_r, mine_r, other_r, m_r, v_r = (refs[i * n + a] for i in range(5))
            g_r, d_r, nm_r, nv_r = (refs[(5 + i) * n + a] for i in range(4))
            g = jnp.where(mine_is_here, mine_r[...], other_r[...])
            g_r[...] = g
            d_r[...], nm_r[...], nv_r[...] = _adam_math(w_r[...], g, m_r[...], v_r[...])

    full = pl.BlockSpec((rc, cols), lambda h, i, c_ref: (h * nch + i, 0))
    half = pl.BlockSpec((rc, cols), lambda h, i, c_ref: (i, 0))
    shape = jax.ShapeDtypeStruct((2 * r2, cols), F32)
    out = pl.pallas_call(
        body, name="adam_big",
        grid_spec=pltpu.PrefetchScalarGridSpec(
            num_scalar_prefetch=1, grid=(2, nch),
            in_specs=[full] * n + [half] * (2 * n) + [full] * (2 * n), out_specs=(full,) * (4 * n)),
        out_shape=(shape,) * (4 * n), compiler_params=_params(("parallel", "parallel")),
    )(c, *ws, *mines, *others, *ms, *vs)
    return [tuple(out[i * n + a] for i in range(4)) for a in range(n)]


def kernel(x, w_in, b_in, ssm_lambda_re, ssm_lambda_im, ssm_log_dt, ssm_b_re, ssm_b_im, ssm_c_re, ssm_c_im, ssm_d, glu_w, glu_b, w_ssm_out, conv_w, w_conv_out, w_o, ln1_g, ln1_b, w_gate, w_up, w_down, ln2_g, ln2_b, loss_target, m_w_in, m_b_in, m_ssm_lambda_re, m_ssm_lambda_im, m_ssm_log_dt, m_ssm_b_re, m_ssm_b_im, m_ssm_c_re, m_ssm_c_im, m_ssm_d, m_glu_w, m_glu_b, m_w_ssm_out, m_conv_w, m_w_conv_out, m_w_o, m_ln1_g, m_ln1_b, m_w_gate, m_w_up, m_w_down, m_ln2_g, m_ln2_b, v_w_in, v_b_in, v_ssm_lambda_re, v_ssm_lambda_im, v_ssm_log_dt, v_ssm_b_re, v_ssm_b_im, v_ssm_c_re, v_ssm_c_im, v_ssm_d, v_glu_w, v_glu_b, v_w_ssm_out, v_conv_w, v_w_conv_out, v_w_o, v_ln1_g, v_ln1_b, v_w_gate, v_w_up, v_w_down, v_ln2_g, v_ln2_b):
    given = dict(locals())
    w = {n: given[n] for n in WEIGHTS}
    m = {n: given["m_" + n] for n in WEIGHTS}
    v = {n: given["v_" + n] for n in WEIGHTS}

    flip = lambda n, a: a.T if n in ("w_gate", "w_up") else a
    shards = {n: flip(n, w[n][0]).astype(BF16) for n in BIG}
    shards["conv_w"] = jnp.pad(conv_w[0], ((0, SUBLANES - 3), (0, 0)))
    c_arr = jnp.reshape(lax.axis_index("c"), (1,)).astype(jnp.int32)
    me = _shard_of((lax.axis_index("x"), lax.axis_index("y")))
    me_arr = jnp.reshape(me, (1,)).astype(jnp.int32)
    dx, pairs, conv_total, packed_total = _device_step(
        x[0], loss_target[0], {n: w[n] for n in SMALL}, shards, c_arr, me_arr)

    grad = _unpack_small(packed_total)
    loss_total = grad.pop("loss")[0]
    grad["conv_w"] = conv_total[:3][None]
    small_names = ("conv_w",) + SMALL
    swap = lambda n, a: a.transpose(0, 1, 3, 2) if n in ("ssm_b_re", "ssm_b_im") else a
    ds, nms, nvs = _adam_small(*([swap(n, d[n]) for n in small_names] for d in (grad, w, m, v)))
    delta, new_m, new_v = {}, {}, {}
    for i, n in enumerate(small_names):
        delta[n], new_m[n], new_v[n] = swap(n, ds[i]), swap(n, nms[i]), swap(n, nvs[i])
    updated = _per_shape(
        lambda *a: _adam_big(*a, c_arr),
        [flip(n, w[n][0]) for n in BIG], [pairs[n][0] for n in BIG], [pairs[n][1] for n in BIG],
        [flip(n, m[n][0]) for n in BIG], [flip(n, v[n][0]) for n in BIG])
    for n, results in zip(BIG, updated):
        grad[n], delta[n], new_m[n], new_v[n] = (flip(n, r)[None] for r in results)

    return (loss_total, dx[None], *[grad[n] for n in WEIGHTS], *[delta[n] for n in WEIGHTS],
            *[new_m[n] for n in WEIGHTS], *[new_v[n] for n in WEIGHTS])
```

```python
import functools
import math

import jax
import jax.numpy as jnp
from jax import lax
from jax.experimental import pallas as pl
from jax.experimental.pallas import tpu as pltpu

F32 = jnp.float32
BF16 = jnp.bfloat16

D_MODEL = 1024
IN_COLS = 4096
SSM_W = 512
N_GROUPS = 32
N_STATE = 64
GROUP_C = 16
STATE_COLS = N_GROUPS * N_STATE
STRIP = 512
N_STRIPS = STATE_COLS // STRIP
FFN_SHARD = 704
N_SHARDS = 4
ALPHA = 2.0 ** 0.25
LN_EPS = 1e-5
GELU_K = math.sqrt(2.0 / math.pi)
GELU_C = 0.044715

ADAM_LR = 0.001
ADAM_B1 = 0.9
ADAM_B2 = 0.999
ADAM_EPS = 1e-08
ADAM_WD = 0.01
ADAM_STEP = 10

V7X_VMEM_BYTES = 64 * 1024 * 1024
VMEM_LIMIT = V7X_VMEM_BYTES - 8 * 1024 * 1024
SUBLANES = 8
N_POWERS = 128

MESH = pl.DeviceIdType.MESH
ANY = pl.BlockSpec(memory_space=pl.ANY)


def _dot(a, b):
    return jnp.dot(a.astype(BF16), b.astype(BF16), preferred_element_type=F32)


def _dot_t(a, b):
    return lax.dot_general(a.astype(BF16), b.astype(BF16), (((1,), (1,)), ((), ())),
                           preferred_element_type=F32)


def _tdot(a, b):
    return lax.dot_general(a.astype(BF16), b.astype(BF16), (((0,), (0,)), ((), ())),
                           preferred_element_type=F32)


def _sigmoid(v):
    return 1.0 / (1.0 + jnp.exp(-v))


def _split3(v):
    hi = v.astype(BF16)
    r1 = v - hi.astype(F32)
    mid = r1.astype(BF16)
    lo = (r1 - mid.astype(F32)).astype(BF16)
    return hi, mid, lo


def _exact_dot(v, sel):
    hi, mid, lo = _split3(v)
    return (jnp.dot(hi, sel, preferred_element_type=F32)
            + jnp.dot(mid, sel, preferred_element_type=F32)
            + jnp.dot(lo, sel, preferred_element_type=F32))


def _const(shape):
    nd = len(shape)
    return pl.BlockSpec(shape, lambda *_: (0,) * nd)


def _params(sem, vmem=VMEM_LIMIT):
    return pltpu.CompilerParams(dimension_semantics=sem, vmem_limit_bytes=vmem)


def _gelu_parts(v):
    inner = GELU_K * (v + GELU_C * v * v * v)
    t = jnp.tanh(inner)
    g = 0.5 * v * (1.0 + t)
    dg = 0.5 * (1.0 + t) + 0.5 * v * (1.0 - t * t) * GELU_K * (1.0 + 3.0 * GELU_C * v * v)
    return g, dg


class _Comm:
    def __init__(self, inputs, out_shape, sems, copies, aliased=False):
        self.inputs, self.out_shape, self.sems = list(inputs), tuple(out_shape), list(sems)
        self.copies, self.aliased = copies, aliased


def _launch(body, comms, *, name, grid, in_specs, out_specs, out_shape, scratch_shapes=(), aliases=None, sem=None):
    comms = list(comms)
    n_in, n_out, n_scr = len(in_specs), len(out_specs), len(scratch_shapes)
    aliases = dict(aliases or {})
    layout = []
    p_in, p_out, p_sem = n_in, n_out, 0
    for cm in comms:
        layout.append((p_in, p_out, p_sem))
        if cm.aliased:
            for i in range(len(cm.inputs)):
                aliases[p_in + i] = p_out + i
        p_in, p_out, p_sem = p_in + len(cm.inputs), p_out + len(cm.out_shape), p_sem + len(cm.sems)
    tot_in, tot_out = p_in, p_out

    def fused(*refs):
        ins, outs = refs[:tot_in], refs[tot_in:tot_in + tot_out]
        scr = refs[tot_in + tot_out:tot_in + tot_out + n_scr]
        sems = refs[tot_in + tot_out + n_scr:]

        def descriptors():
            out = []
            for cm, (a, b, s) in zip(comms, layout):
                out += cm.copies(ins[a:a + len(cm.inputs)], outs[b:b + len(cm.out_shape)], sems[s:s + len(cm.sems)])
            return out

        steps = [pl.program_id(d) for d in range(len(grid))]
        first = functools.reduce(jnp.logical_and, [s == 0 for s in steps]) if grid else None
        last = functools.reduce(jnp.logical_and, [s == g - 1 for s, g in zip(steps, grid)]) if grid else None

        def start():
            for cp in descriptors():
                cp.start()

        def finish():
            for cp in descriptors():
                cp.wait()

        if comms:
            pl.when(first)(start) if grid else start()
        if body is not None:
            body(*ins[:n_in], *outs[:n_out], *scr)
        if comms:
            pl.when(last)(finish) if grid else finish()

    specs_in = list(in_specs) + [ANY] * (tot_in - n_in)
    specs_out = tuple(out_specs) + (ANY,) * (tot_out - n_out)
    shapes = tuple(out_shape) + tuple(s for cm in comms for s in cm.out_shape)
    scratch = list(scratch_shapes) + [s for cm in comms for s in cm.sems]
    if comms or sem is None:
        sem = ("arbitrary",) * len(grid)
    kwargs = dict(grid=grid) if grid else {}
    call = pl.pallas_call(fused, name=name, out_shape=shapes, in_specs=specs_in, out_specs=specs_out,
                          scratch_shapes=scratch, input_output_aliases=aliases,
                          compiler_params=_params(sem) if grid else None, **kwargs)

    def run(*args):
        out = call(*args, *(a for cm in comms for a in cm.inputs))
        results, rest = out[:n_out], out[n_out:]
        per_comm = []
        for cm in comms:
            per_comm.append(rest[:len(cm.out_shape)])
            rest = rest[len(cm.out_shape):]
        return results, per_comm

    return run


def _ssm_discretise(lr, li, ldt, lr16, li16, ldt16, brt, bit):
    def lam_bar(lr_, li_, ldt_):
        dt = jnp.exp(ldt_)
        mag = jnp.exp(lr_ * dt)
        return mag * jnp.cos(li_ * dt), mag * jnp.sin(li_ * dt)

    lb_re, lb_im = lam_bar(lr, li, ldt)
    l16_re, l16_im = lam_bar(lr16, li16, ldt16)
    den = lr16 * lr16 + li16 * li16
    num_re = l16_re - 1.0
    fr = (num_re * lr16 + l16_im * li16) / den
    fi = (l16_im * lr16 - num_re * li16) / den
    bb_re = fr * brt - fi * bit
    bb_im = fr * bit + fi * brt
    return lb_re, lb_im, bb_re, bb_im


def _strip_selectors():
    p = lax.broadcasted_iota(jnp.int32, (N_STATE, STRIP), 0)
    col = lax.broadcasted_iota(jnp.int32, (N_STATE, STRIP), 1)
    rep = ((col & (N_STATE - 1)) == p).astype(BF16)
    row = lax.broadcasted_iota(jnp.int32, (SSM_W, STRIP), 0)
    col2 = lax.broadcasted_iota(jnp.int32, (SSM_W, STRIP), 1)
    mask = (((row >> 4) & 7) == (col2 >> 6))
    return rep, mask


def _ssm_prepare(lr, li, ldt, lr16, li16, ldt16, brt, bit, cre, cim):
    def body(lr_r, li_r, ldt_r, lr16_r, li16_r, ldt16_r, brt_r, bit_r, cre_r, cim_r,
             pwr_r, pwi_r, bbr_r, bbi_r, ctr_r, cti_r):
        lb_re, lb_im, bb_re, bb_im = _ssm_discretise(
            lr_r[...], li_r[...], ldt_r[...], lr16_r[...], li16_r[...], ldt16_r[...], brt_r[...], bit_r[...])
        pr, pi_ = lb_re, lb_im
        pwr_r[0] = pr
        pwi_r[0] = pi_
        for k in range(1, N_POWERS):
            pr, pi_ = pr * lb_re - pi_ * lb_im, pr * lb_im + pi_ * lb_re
            pwr_r[k] = pr
            pwi_r[k] = pi_
        rep, mask = _strip_selectors()
        for src, dst in ((bb_re, bbr_r), (bb_im, bbi_r), (cre_r[...], ctr_r), (cim_r[...], cti_r)):
            wide = jnp.dot(src.astype(BF16), rep, preferred_element_type=F32)
            dst[...] = jnp.where(mask, wide, 0.0).astype(BF16)

    vm = pl.BlockSpec(memory_space=pltpu.VMEM)
    return pl.pallas_call(
        body, name="ssm_prepare",
        out_shape=(jax.ShapeDtypeStruct((N_POWERS, N_GROUPS, N_STATE), F32),) * 2
        + (jax.ShapeDtypeStruct((SSM_W, STRIP), BF16),) * 4,
        in_specs=[vm] * 10, out_specs=(vm,) * 6,
    )(lr, li, ldt, lr16, li16, ldt16, brt, bit, cre, cim)


def _scan_tables(pwr, pwi):
    pr = pwr.reshape(N_POWERS, STATE_COLS)
    pi_ = pwi.reshape(N_POWERS, STATE_COLS)
    rows8 = lambda v: jnp.broadcast_to(v[None], (SUBLANES, STATE_COLS))
    tab_a = jnp.stack([rows8(pr[0]), rows8(pi_[0]), rows8(pr[-1]), rows8(pi_[-1])])
    tab_p = jnp.stack([jnp.repeat(pr, SUBLANES, axis=0), jnp.repeat(pi_, SUBLANES, axis=0)])
    return tab_a, tab_p


def _ssm_param_grads(lr, li, ldt, lr16, li16, ldt16, brt, bit, dlbr, dlbi, dbbr, dbbi, dctr, dcti):
    def body(lr_r, li_r, ldt_r, lr16_r, li16_r, ldt16_r, brt_r, bit_r,
             dlbr_r, dlbi_r, dbbr_r, dbbi_r, dctr_r, dcti_r,
             glr_r, gli_r, gldt_r, gbrt_r, gbit_r, gcre_r, gcim_r):
        rep, mask = _strip_selectors()

        def fold(acc):
            return sum(lax.dot_general(t, rep, (((1,), (1,)), ((), ())), preferred_element_type=F32)
                       for t in _split3(jnp.where(mask, acc, 0.0)))

        g_lb_re = jnp.sum(dlbr_r[...], axis=0)
        g_lb_im = jnp.sum(dlbi_r[...], axis=0)
        g_bb_re = fold(dbbr_r[...])
        g_bb_im = fold(dbbi_r[...])
        gcre_r[...] = fold(dctr_r[...])
        gcim_r[...] = fold(dcti_r[...])
        prim = (lr_r[...], li_r[...], ldt_r[...], lr16_r[...], li16_r[...], ldt16_r[...], brt_r[...], bit_r[...])
        _, vjp = jax.vjp(_ssm_discretise, *prim)
        g_lr, g_li, g_ldt, g_lr16, g_li16, g_ldt16, g_brt, g_bit = vjp((g_lb_re, g_lb_im, g_bb_re, g_bb_im))
        grp = lax.broadcasted_iota(jnp.int32, (N_GROUPS, SSM_W), 0)
        rw = lax.broadcasted_iota(jnp.int32, (N_GROUPS, SSM_W), 1)
        gsum = ((rw >> 4) == grp).astype(BF16)

        def group_sum(v):
            return sum(jnp.dot(gsum, t, preferred_element_type=F32) for t in _split3(v))

        glr_r[...] = g_lr + group_sum(g_lr16)
        gli_r[...] = g_li + group_sum(g_li16)
        gldt_r[...] = g_ldt + jnp.sum(group_sum(g_ldt16), axis=1, keepdims=True)
        gbrt_r[...] = g_brt
        gbit_r[...] = g_bit

    vm = pl.BlockSpec(memory_space=pltpu.VMEM)
    gp = jax.ShapeDtypeStruct((N_GROUPS, N_STATE), F32)
    gb = jax.ShapeDtypeStruct((SSM_W, N_STATE), F32)
    return pl.pallas_call(
        body, name="ssm_param_grads",
        out_shape=(gp, gp, jax.ShapeDtypeStruct((N_GROUPS, 1), F32), gb, gb, gb, gb),
        in_specs=[vm] * 14, out_specs=(vm,) * 7,
    )(lr, li, ldt, lr16, li16, ldt16, brt, bit, dlbr, dlbi, dbbr, dbbi, dctr, dcti)


def _in_proj(x, w_in_st, b_in, comms=()):
    t = x.shape[0]
    tm = 512

    def body(x_r, w_r, b_r, o_r):
        xb = x_r[...].astype(BF16)
        for j in range(N_SHARDS):
            cols = slice(D_MODEL * j, D_MODEL * (j + 1))
            o_r[:, cols] = jnp.dot(xb, w_r[j], preferred_element_type=F32) + b_r[:, cols]

    (proj,), sent = _launch(
        body, comms, name="in_proj", grid=(t // tm,),
        out_shape=(jax.ShapeDtypeStruct((t, IN_COLS), F32),),
        in_specs=[pl.BlockSpec((tm, D_MODEL), lambda i: (i, 0)), _const((N_SHARDS, D_MODEL, D_MODEL)),
                  _const((1, IN_COLS))],
        out_specs=(pl.BlockSpec((tm, IN_COLS), lambda i: (i, 0)),),
        sem=("parallel",),
    )(x, w_in_st, b_in)
    return proj, sent


def _cmul_add(xr, xi, mr, mi, sr, si):
    return xr + (mr * sr - mi * si), xi + (mr * si + mi * sr)


SCAN_STEPS = N_POWERS
SCAN_CHUNK = SUBLANES * SCAN_STEPS


def _interleave(src_r, dst_r):
    for step in range(SCAN_STEPS):
        dst_r[SUBLANES * step:SUBLANES * (step + 1), :] = src_r[pl.ds(step, SUBLANES, stride=SCAN_STEPS), :]


def _deinterleave(src_r, dst_r):
    for step in range(SCAN_STEPS):
        dst_r[pl.ds(step, SUBLANES, stride=SCAN_STEPS), :] = src_r[SUBLANES * step:SUBLANES * (step + 1), :]


def _step_rows(step):
    return pl.ds(pl.multiple_of(step * SUBLANES, SUBLANES), SUBLANES)


def _segment_states(first_r, first_i, ends_r, ends_i, a64_r, a64_i, order):
    row = lax.broadcasted_iota(jnp.int32, ends_r.shape, 0)
    cur_r, cur_i = first_r, first_i
    ent_r = jnp.zeros_like(ends_r)
    ent_i = jnp.zeros_like(ends_i)
    for s in order:
        ent_r = jnp.where(row == s, jnp.broadcast_to(cur_r, ends_r.shape), ent_r)
        ent_i = jnp.where(row == s, jnp.broadcast_to(cur_i, ends_i.shape), ent_i)
        cur_r, cur_i = _cmul_add(ends_r[s:s + 1, :], ends_i[s:s + 1, :], a64_r, a64_i, cur_r, cur_i)
    return ent_r, ent_i, cur_r, cur_i


def _ssm_forward(proj, bbr, bbi, ctr, cti, d_skip, tab_a, tab_p, comms=(), tc=SCAN_CHUNK):
    t = proj.shape[0]

    def body(u_r, bbr_r, bbi_r, ctr_r, cti_r, d_r, ta_r, tp_r, xsr_r, xsi_r, y_r, ui_s, yi_s, car_r, car_i):
        @pl.when(pl.program_id(1) == 0)
        def _():
            car_r[...] = jnp.zeros_like(car_r)
            car_i[...] = jnp.zeros_like(car_i)

        _interleave(u_r, ui_s)
        u = ui_s[...]
        xsr_r[...] = _dot(u, bbr_r[...])
        xsi_r[...] = _dot(u, bbi_r[...])
        a_r, a_i = ta_r[0], ta_r[1]

        def local(step, carry):
            rows = _step_rows(step)
            xr, xi = _cmul_add(xsr_r[rows, :], xsi_r[rows, :], a_r, a_i, *carry)
            xsr_r[rows, :] = xr
            xsi_r[rows, :] = xi
            return xr, xi

        zero = jnp.zeros((SUBLANES, STRIP), F32)
        ends_r, ends_i = lax.fori_loop(0, SCAN_STEPS, local, (zero, zero), unroll=2)
        ent_r, ent_i, out_r, out_i = _segment_states(
            car_r[...], car_i[...], ends_r, ends_i, ta_r[2, 0:1, :], ta_r[3, 0:1, :], range(SUBLANES))
        car_r[...] = out_r
        car_i[...] = out_i

        def entering(step, _):
            rows = _step_rows(step)
            xr, xi = _cmul_add(xsr_r[rows, :], xsi_r[rows, :], tp_r[0, rows, :], tp_r[1, rows, :], ent_r, ent_i)
            xsr_r[rows, :] = xr
            xsi_r[rows, :] = xi
            return 0

        lax.fori_loop(0, SCAN_STEPS, entering, 0, unroll=4)
        yi_s[...] = _dot_t(xsr_r[...], ctr_r[...]) - _dot_t(xsi_r[...], cti_r[...]) + d_r[...] * u
        _deinterleave(yi_s, y_r)

    strip_mat = pl.BlockSpec((128, STRIP), lambda j, k: (j, 0))
    states = pl.BlockSpec((tc, STRIP), lambda j, k: (k, j))
    return _launch(
        body, comms, name="ssm_forward", grid=(N_STRIPS, t // tc),
        out_shape=(jax.ShapeDtypeStruct((t, STATE_COLS), F32), jax.ShapeDtypeStruct((t, STATE_COLS), F32),
                   jax.ShapeDtypeStruct((t, SSM_W), F32)),
        in_specs=[pl.BlockSpec((tc, 128), lambda j, k: (k, j)),
                  strip_mat, strip_mat, strip_mat, strip_mat,
                  pl.BlockSpec((1, 128), lambda j, k: (0, j)),
                  pl.BlockSpec((4, SUBLANES, STRIP), lambda j, k: (0, 0, j)),
                  pl.BlockSpec((2, tc, STRIP), lambda j, k: (0, 0, j))],
        out_specs=(states, states, pl.BlockSpec((tc, 128), lambda j, k: (k, j))),
        scratch_shapes=[pltpu.VMEM((tc, 128), F32), pltpu.VMEM((tc, 128), F32),
                        pltpu.VMEM((1, STRIP), F32), pltpu.VMEM((1, STRIP), F32)],
        sem=("parallel", "arbitrary"),
    )(proj, bbr, bbi, ctr, cti, d_skip, tab_a, tab_p)


def _shift_down(v, prev, n):
    row = lax.broadcasted_iota(jnp.int32, v.shape, 0)
    out = pltpu.roll(v, n, 0)
    for r in range(n):
        src = prev[SUBLANES - n + r:SUBLANES - n + r + 1, :]
        out = jnp.where(row == r, jnp.broadcast_to(src, v.shape), out)
    return out


def _shift_up(v, nxt, n):
    rows = v.shape[0]
    row = lax.broadcasted_iota(jnp.int32, v.shape, 0)
    out = pltpu.roll(v, rows - n, 0)
    for r in range(n):
        src = nxt[r:r + 1, :]
        out = jnp.where(row == rows - n + r, jnp.broadcast_to(src, v.shape), out)
    return out


def _conv3(q, q_prev, w):
    return w[2:3, :] * q + w[1:2, :] * _shift_down(q, q_prev, 1) + w[0:1, :] * _shift_down(q, q_prev, 2)


def _mixer_forward(x, proj, ya0, glu_w, glu_b, wso_st, conv_w8, wco_st, w_o, comms=(), tm=256):
    t = x.shape[0]
    hb = tm // SUBLANES

    def body(x_r, ya0_r, h_r, cg_r, bg_r, ga_r, gb_r, hp_r, cgp_r,
             glu_w_r, glu_b_r, wso_r, cw_r, wco_r, wo_r, xh_r, rstd_r, ya_r, yb_r):
        i = pl.program_id(0)
        g, _ = _gelu_parts(ya0_r[...])
        ya1 = g * _sigmoid(_dot(g, glu_w_r[...]) + glu_b_r[...])
        q = cg_r[...] * h_r[...]
        q_prev = jnp.where(i > 0, cgp_r[...] * hp_r[...], 0.0)
        yb0 = bg_r[...] * _conv3(q, q_prev, cw_r[...])
        for j in range(N_SHARDS):
            ya_r[:, 256 * j:256 * (j + 1)] = _dot(ya1, wso_r[j])
            yb_r[:, 256 * j:256 * (j + 1)] = _dot(yb0, wco_r[j])
        merged = _sigmoid(ga_r[...]) * ya_r[...] + _sigmoid(gb_r[...]) * yb_r[...]
        r1 = ALPHA * x_r[...] + _dot(merged, wo_r[...])
        mu = jnp.mean(r1, axis=-1, keepdims=True)
        cen = r1 - mu
        rstd = lax.rsqrt(jnp.mean(cen * cen, axis=-1, keepdims=True) + LN_EPS)
        xh_r[...] = cen * rstd
        rstd_r[...] = rstd

    def col(w, c):
        return pl.BlockSpec((tm, w), lambda i: (i, c))

    def prev(c):
        return pl.BlockSpec((SUBLANES, SSM_W), lambda i: (jnp.maximum(i * hb - 1, 0), c))

    return _launch(
        body, comms, name="mixer_forward", grid=(t // tm,),
        out_shape=(jax.ShapeDtypeStruct((t, D_MODEL), F32), jax.ShapeDtypeStruct((t, 1), F32),
                   jax.ShapeDtypeStruct((t, D_MODEL), F32), jax.ShapeDtypeStruct((t, D_MODEL), F32)),
        in_specs=[col(D_MODEL, 0), col(SSM_W, 0), col(SSM_W, 1), col(SSM_W, 2), col(SSM_W, 3),
                  col(D_MODEL, 2), col(D_MODEL, 3), prev(1), prev(2),
                  _const((SSM_W, SSM_W)), _const((1, SSM_W)), _const((N_SHARDS, SSM_W, 256)),
                  _const((SUBLANES, SSM_W)), _const((N_SHARDS, SSM_W, 256)), _const((D_MODEL, D_MODEL))],
        out_specs=(col(D_MODEL, 0), pl.BlockSpec((tm, 1), lambda i: (i, 0)), col(D_MODEL, 0), col(D_MODEL, 0)),
        sem=("parallel",),
    )(x, ya0, proj, proj, proj, proj, proj, proj, proj, glu_w, glu_b, wso_st, conv_w8, wco_st, w_o)


def _layer_norm_bwd(dxhat, xhat, rstd):
    m1 = jnp.mean(dxhat, axis=-1, keepdims=True)
    m2 = jnp.mean(dxhat * xhat, axis=-1, keepdims=True)
    return rstd * (dxhat - m1 - xhat * m2)


def _ffn_step(xhat1, rstd1, target, ln1_g, ln1_b, ln2_g, ln2_b, wg_st, wu_st, wd_st, tm=256):
    t = xhat1.shape[0]

    def body(xh_r, rstd_r, tgt_r, g1_r, b1_r, g2_r, b2_r, wg_r, wu_r, wd_r,
             loss_r, dr1_r, x1b_r, dr2b_r, hid_r, dhg_r, dhu_r, dg2_r, db2_r, dg1_r, db1_r,
             hg_s, hu_s):
        @pl.when(pl.program_id(0) == 0)
        def _():
            for r in (loss_r, dg2_r, db2_r, dg1_r, db1_r):
                r[...] = jnp.zeros_like(r)

        xhat1_v = xh_r[...]
        x1 = xhat1_v * g1_r[...] + b1_r[...]
        x1b = x1.astype(BF16)
        x1b_r[...] = x1b
        ffn = jnp.zeros((tm, D_MODEL), F32)
        for j in range(N_SHARDS):
            hg = lax.dot_general(x1b, wg_r[j], (((1,), (1,)), ((), ())), preferred_element_type=F32)
            hu = lax.dot_general(x1b, wu_r[j], (((1,), (1,)), ((), ())), preferred_element_type=F32)
            hg_s[j] = hg
            hu_s[j] = hu
            hid = (hg * _sigmoid(hg) * hu).astype(BF16)
            hid_r[j] = hid
            ffn = ffn + jnp.dot(hid, wd_r[j], preferred_element_type=F32)
        r2 = ALPHA * x1 + ffn
        mu = jnp.mean(r2, axis=-1, keepdims=True)
        cen = r2 - mu
        rstd2 = lax.rsqrt(jnp.mean(cen * cen, axis=-1, keepdims=True) + LN_EPS)
        xhat2 = cen * rstd2
        diff = (xhat2 * g2_r[...] + b2_r[...]) - tgt_r[...]
        loss_r[...] += 0.5 * jnp.sum(jnp.mean(diff * diff, axis=-1, keepdims=True), axis=0, keepdims=True)
        dy = diff * (1.0 / D_MODEL)
        dg2_r[...] += jnp.sum(dy * xhat2, axis=0, keepdims=True)
        db2_r[...] += jnp.sum(dy, axis=0, keepdims=True)
        dr2 = _layer_norm_bwd(dy * g2_r[...], xhat2, rstd2)
        dr2b = dr2.astype(BF16)
        dr2b_r[...] = dr2b
        dx1 = ALPHA * dr2
        for j in range(N_SHARDS):
            dhid = lax.dot_general(dr2b, wd_r[j], (((1,), (1,)), ((), ())), preferred_element_type=F32)
            hg = hg_s[j]
            hu = hu_s[j]
            sg = _sigmoid(hg)
            dhu = (dhid * (hg * sg)).astype(BF16)
            dhg = (dhid * hu * (sg * (1.0 + hg * (1.0 - sg)))).astype(BF16)
            dhg_r[j] = dhg
            dhu_r[j] = dhu
            dx1 = dx1 + jnp.dot(dhg, wg_r[j], preferred_element_type=F32)
            dx1 = dx1 + jnp.dot(dhu, wu_r[j], preferred_element_type=F32)
        dg1_r[...] += jnp.sum(dx1 * xhat1_v, axis=0, keepdims=True)
        db1_r[...] += jnp.sum(dx1, axis=0, keepdims=True)
        dr1_r[...] = _layer_norm_bwd(dx1 * g1_r[...], xhat1_v, rstd_r[...])

    tile = pl.BlockSpec((tm, D_MODEL), lambda i: (i, 0))
    hidden = pl.BlockSpec((N_SHARDS, tm, FFN_SHARD), lambda i: (0, i, 0))
    vec = _const((1, D_MODEL))
    hid_shape = jax.ShapeDtypeStruct((N_SHARDS, t, FFN_SHARD), BF16)
    vec_shape = jax.ShapeDtypeStruct((1, D_MODEL), F32)
    return pl.pallas_call(
        body, name="ffn_step", grid=(t // tm,),
        out_shape=(jax.ShapeDtypeStruct((1, 1), F32), jax.ShapeDtypeStruct((t, D_MODEL), F32),
                   jax.ShapeDtypeStruct((t, D_MODEL), BF16), jax.ShapeDtypeStruct((t, D_MODEL), BF16),
                   hid_shape, hid_shape, hid_shape, vec_shape, vec_shape, vec_shape, vec_shape),
        in_specs=[tile, pl.BlockSpec((tm, 1), lambda i: (i, 0)), tile, vec, vec, vec, vec,
                  _const((N_SHARDS, FFN_SHARD, D_MODEL)), _const((N_SHARDS, FFN_SHARD, D_MODEL)),
                  _const((N_SHARDS, FFN_SHARD, D_MODEL))],
        out_specs=(_const((1, 1)), tile, tile, tile, hidden, hidden, hidden, vec, vec, vec, vec),
        scratch_shapes=[pltpu.VMEM((N_SHARDS, tm, FFN_SHARD), F32), pltpu.VMEM((N_SHARDS, tm, FFN_SHARD), F32)],
        compiler_params=_params(("arbitrary",)),
    )(xhat1, rstd1, target, ln1_g, ln1_b, ln2_g, ln2_b, wg_st, wu_st, wd_st)


def _ffn_weight_grads(x1b, dr2b, hid, dhg, dhu, tk=512):
    t = x1b.shape[0]

    def body(x_r, dr_r, hid_r, dhg_r, dhu_r, gwg_r, gwu_r, gwd_r):
        @pl.when(pl.program_id(1) == 0)
        def _():
            for r in (gwg_r, gwu_r, gwd_r):
                r[...] = jnp.zeros_like(r)

        gwg_r[0] += _tdot(dhg_r[0], x_r[...])
        gwu_r[0] += _tdot(dhu_r[0], x_r[...])
        gwd_r[0] += _tdot(hid_r[0], dr_r[...])

    tile = pl.BlockSpec((tk, D_MODEL), lambda j, k: (k, 0))
    hidden = pl.BlockSpec((1, tk, FFN_SHARD), lambda j, k: (j, k, 0))
    row = pl.BlockSpec((1, FFN_SHARD, D_MODEL), lambda j, k: (j, 0, 0))
    return pl.pallas_call(
        body, name="ffn_weight_grads", grid=(N_SHARDS, t // tk),
        out_shape=(jax.ShapeDtypeStruct((N_SHARDS, FFN_SHARD, D_MODEL), F32),) * 3,
        in_specs=[tile, tile, hidden, hidden, hidden],
        out_specs=(row, row, row),
        compiler_params=_params(("parallel", "arbitrary")),
    )(x1b, dr2b, hid, dhg, dhu)


def _mixer_backward(dr1, proj, ya0, ya, yb, glu_w, glu_b, wso_st, conv_w8, wco_st, w_o, comms=(), tm=256):
    t = dr1.shape[0]
    hb = tm // SUBLANES
    last_block = t // SUBLANES - 1

    def body(dr1_r, dr1n_r, ya0_r, ya_r, yb_r, h_r, cg_r, bg_r, ga_r, gb_r, hp_r, cgp_r, bgn_r, gbn_r,
             glu_w_r, glu_b_r, wso_r, cw_r, wco_r, wo_r,
             dya0_r, dproj_r, dbias_r, gwo_r, gwso_r, gwco_r, gglu_w_r, gglu_b_r, gconv_r):
        i = pl.program_id(0)

        @pl.when(i == 0)
        def _():
            for r in (dbias_r, gwo_r, gwso_r, gwco_r, gglu_w_r, gglu_b_r, gconv_r):
                r[...] = jnp.zeros_like(r)

        dr1_v = dr1_r[...]
        dmerged = _dot_t(dr1_v, wo_r[...])
        sa = _sigmoid(ga_r[...])
        sb = _sigmoid(gb_r[...])
        ya_v = ya_r[...]
        yb_v = yb_r[...]
        gwo_r[...] += _tdot(sa * ya_v + sb * yb_v, dr1_v)
        dya = dmerged * sa
        dyb = dmerged * sb
        dga = dmerged * ya_v * (sa * (1.0 - sa))
        dgb = dmerged * yb_v * (sb * (1.0 - sb))

        g, gelu_grad = _gelu_parts(ya0_r[...])
        s1 = _sigmoid(_dot(g, glu_w_r[...]) + glu_b_r[...])
        ya1 = g * s1
        dya1 = jnp.zeros((tm, SSM_W), F32)
        for j in range(N_SHARDS):
            dya_j = dya[:, 256 * j:256 * (j + 1)]
            gwso_r[j] += _tdot(ya1, dya_j)
            dya1 = dya1 + _dot_t(dya_j, wso_r[j])
        dz1 = dya1 * g * (s1 * (1.0 - s1))
        gglu_b_r[...] += jnp.sum(dz1, axis=0, keepdims=True)
        gglu_w_r[...] += _tdot(g, dz1)
        dya0_r[...] = (dya1 * s1 + _dot_t(dz1, glu_w_r[...])) * gelu_grad

        cw = cw_r[...]
        h = h_r[...]
        cg = cg_r[...]
        bg = bg_r[...]
        q = cg * h
        q_prev = jnp.where(i > 0, cgp_r[...] * hp_r[...], 0.0)
        q1 = _shift_down(q, q_prev, 1)
        q2 = _shift_down(q, q_prev, 2)
        z = cw[2:3, :] * q + cw[1:2, :] * q1 + cw[0:1, :] * q2
        yb0 = bg * z
        dyb0 = jnp.zeros((tm, SSM_W), F32)
        for j in range(N_SHARDS):
            dyb_j = dyb[:, 256 * j:256 * (j + 1)]
            gwco_r[j] += _tdot(yb0, dyb_j)
            dyb0 = dyb0 + _dot_t(dyb_j, wco_r[j])
        dbg = dyb0 * z
        dz = dyb0 * bg
        dyb_n = _dot_t(dr1n_r[...], wo_r[...]) * _sigmoid(gbn_r[...])
        dyb0_n = jnp.zeros((SUBLANES, SSM_W), F32)
        for j in range(N_SHARDS):
            dyb0_n = dyb0_n + _dot_t(dyb_n[:, 256 * j:256 * (j + 1)], wco_r[j])
        dz_next = jnp.where(i < pl.num_programs(0) - 1, dyb0_n * bgn_r[...], 0.0)
        dq = cw[2:3, :] * dz + cw[1:2, :] * _shift_up(dz, dz_next, 1) + cw[0:1, :] * _shift_up(dz, dz_next, 2)
        gconv_r[0:1, :] += jnp.sum(dz * q2, axis=0, keepdims=True)
        gconv_r[1:2, :] += jnp.sum(dz * q1, axis=0, keepdims=True)
        gconv_r[2:3, :] += jnp.sum(dz * q, axis=0, keepdims=True)
        dh = dq * cg
        dcg = dq * h

        dproj_r[:, 0:512] = jnp.zeros((tm, SSM_W), BF16)
        pieces = ((512, dh), (1024, dcg), (1536, dbg), (2048, dga), (3072, dgb))
        for off, val in pieces:
            w = val.shape[1]
            dproj_r[:, off:off + w] = val.astype(BF16)
            dbias_r[:, off:off + w] += jnp.sum(val, axis=0, keepdims=True)

    def col(w, c):
        return pl.BlockSpec((tm, w), lambda i: (i, c))

    def prev(c):
        return pl.BlockSpec((SUBLANES, SSM_W), lambda i: (jnp.maximum(i * hb - 1, 0), c))

    def nxt(w, c):
        return pl.BlockSpec((SUBLANES, w), lambda i: (jnp.minimum((i + 1) * hb, last_block), c))

    sh = jax.ShapeDtypeStruct
    return _launch(
        body, comms, name="mixer_backward", grid=(t // tm,),
        out_shape=(sh((t, SSM_W), F32), sh((t, IN_COLS), BF16), sh((1, IN_COLS), F32),
                   sh((D_MODEL, D_MODEL), F32), sh((N_SHARDS, SSM_W, 256), F32), sh((N_SHARDS, SSM_W, 256), F32),
                   sh((SSM_W, SSM_W), F32), sh((1, SSM_W), F32), sh((SUBLANES, SSM_W), F32)),
        in_specs=[col(D_MODEL, 0), nxt(D_MODEL, 0), col(SSM_W, 0), col(D_MODEL, 0), col(D_MODEL, 0),
                  col(SSM_W, 1), col(SSM_W, 2), col(SSM_W, 3), col(D_MODEL, 2), col(D_MODEL, 3),
                  prev(1), prev(2), nxt(SSM_W, 3), nxt(D_MODEL, 3),
                  _const((SSM_W, SSM_W)), _const((1, SSM_W)), _const((N_SHARDS, SSM_W, 256)),
                  _const((SUBLANES, SSM_W)), _const((N_SHARDS, SSM_W, 256)), _const((D_MODEL, D_MODEL))],
        out_specs=(col(SSM_W, 0), col(IN_COLS, 0), _const((1, IN_COLS)),
                   _const((D_MODEL, D_MODEL)), _const((N_SHARDS, SSM_W, 256)), _const((N_SHARDS, SSM_W, 256)),
                   _const((SSM_W, SSM_W)), _const((1, SSM_W)), _const((SUBLANES, SSM_W))),
        sem=("arbitrary",),
    )(dr1, dr1, ya0, ya, yb, proj, proj, proj, proj, proj, proj, proj, proj, proj,
      glu_w, glu_b, wso_st, conv_w8, wco_st, w_o)


def _cmulc_add(xr, xi, mr, mi, sr, si):
    return xr + (mr * sr + mi * si), xi + (mr * si - mi * sr)


def _ssm_backward(dya0, proj, xsr, xsi, bbr, bbi, ctr, cti, d_skip, tab_a, tab_p, dproj, comms=(), tc=SCAN_CHUNK):
    t = proj.shape[0]
    nk = t // tc

    def body(dy_r, u_r, xsr_r, xsi_r, bbr_r, bbi_r, ctr_r, cti_r, d_r, ta_r, tp_r, dproj_any,
             du_r, dus_r, gbbr_r, gbbi_r, gctr_r, gcti_r, glbr_r, glbi_r, gd_r,
             gr_s, gi_s, dyi_s, ui_s, dui_s, dun_s, car_r, car_i):
        del dproj_any

        @pl.when(pl.program_id(1) == 0)
        def _():
            for r in (car_r, car_i, dus_r, gbbr_r, gbbi_r, gctr_r, gcti_r, glbr_r, glbi_r, gd_r):
                r[...] = jnp.zeros_like(r)

        _interleave(dy_r, dyi_s)
        _interleave(u_r, ui_s)
        dy = dyi_s[...]
        u = ui_s[...]
        gr_s[...] = _dot(dy, ctr_r[...])
        gi_s[...] = -_dot(dy, cti_r[...])
        a_r, a_i = ta_r[0], ta_r[1]

        def local(n, carry):
            rows = _step_rows(SCAN_STEPS - 1 - n)
            gr, gi = _cmulc_add(gr_s[rows, :], gi_s[rows, :], a_r, a_i, *carry)
            gr_s[rows, :] = gr
            gi_s[rows, :] = gi
            return gr, gi

        zero = jnp.zeros((SUBLANES, STRIP), F32)
        ends_r, ends_i = lax.fori_loop(0, SCAN_STEPS, local, (zero, zero), unroll=2)
        ent_r, ent_i, out_r, out_i = _segment_states(
            car_r[...], car_i[...], ends_r, ends_i, ta_r[2, 0:1, :], -ta_r[3, 0:1, :], range(SUBLANES - 1, -1, -1))
        car_r[...] = out_r
        car_i[...] = out_i

        def entering(n, carry):
            gnr, gni, ar, ai = carry
            rows = _step_rows(SCAN_STEPS - 1 - n)
            power = _step_rows(n)
            gr, gi = _cmulc_add(gr_s[rows, :], gi_s[rows, :], tp_r[0, power, :], tp_r[1, power, :], ent_r, ent_i)
            gr_s[rows, :] = gr
            gi_s[rows, :] = gi
            xr = xsr_r[rows, :]
            xi = xsi_r[rows, :]
            return gr, gi, ar + (xr * gnr + xi * gni), ai + (xr * gni - xi * gnr)

        _, _, ar, ai = lax.fori_loop(0, SCAN_STEPS, entering, (ent_r, ent_i, zero, zero), unroll=2)
        glbr_r[...] += ar
        glbi_r[...] += ai
        gr = gr_s[...]
        gi = gi_s[...]
        dui_s[...] = _dot_t(gr, bbr_r[...]) + _dot_t(gi, bbi_r[...]) + d_r[...] * dy
        _deinterleave(dui_s, dun_s)
        du = dun_s[...]
        du_r[...] = du.astype(BF16)
        dus_r[...] += jnp.sum(du, axis=0, keepdims=True)
        gd_r[...] += jnp.sum(dy * u, axis=0, keepdims=True)
        gbbr_r[...] += _tdot(u, gr)
        gbbi_r[...] += _tdot(u, gi)
        gctr_r[...] += _tdot(dy, xsr_r[...])
        gcti_r[...] -= _tdot(dy, xsi_r[...])

    def rev(w):
        return pl.BlockSpec((tc, w), lambda j, k: (nk - 1 - k, j))

    strip_mat = pl.BlockSpec((128, STRIP), lambda j, k: (j, 0))
    vec = pl.BlockSpec((1, 128), lambda j, k: (0, j))
    lbacc = pl.BlockSpec((SUBLANES, STRIP), lambda j, k: (0, j))
    sh = jax.ShapeDtypeStruct
    return _launch(
        body, comms, name="ssm_backward", grid=(N_STRIPS, nk),
        out_shape=(sh((t, IN_COLS), BF16), sh((1, SSM_W), F32),
                   sh((SSM_W, STRIP), F32), sh((SSM_W, STRIP), F32), sh((SSM_W, STRIP), F32), sh((SSM_W, STRIP), F32),
                   sh((SUBLANES, STATE_COLS), F32), sh((SUBLANES, STATE_COLS), F32), sh((1, SSM_W), F32)),
        in_specs=[rev(128), rev(128), rev(STRIP), rev(STRIP),
                  strip_mat, strip_mat, strip_mat, strip_mat, vec,
                  pl.BlockSpec((4, SUBLANES, STRIP), lambda j, k: (0, 0, j)),
                  pl.BlockSpec((2, tc, STRIP), lambda j, k: (0, 0, j)), ANY],
        out_specs=(rev(128), vec, strip_mat, strip_mat, strip_mat, strip_mat, lbacc, lbacc, vec),
        scratch_shapes=[pltpu.VMEM((tc, STRIP), F32), pltpu.VMEM((tc, STRIP), F32)]
        + [pltpu.VMEM((tc, 128), F32)] * 4 + [pltpu.VMEM((1, STRIP), F32)] * 2,
        aliases={11: 0}, sem=("parallel", "arbitrary"),
    )(dya0, proj, xsr, xsi, bbr, bbi, ctr, cti, d_skip, tab_a, tab_p, dproj)


def _input_grad(dr1, dproj, w_in_st, comms=(), tm=256):
    t = dr1.shape[0]

    def body(dr1_r, dp_r, w_r, dx_r):
        acc = ALPHA * dr1_r[...]
        for j in range(N_SHARDS):
            acc = acc + lax.dot_general(dp_r[:, D_MODEL * j:D_MODEL * (j + 1)], w_r[j],
                                        (((1,), (1,)), ((), ())), preferred_element_type=F32)
        dx_r[...] = acc

    (dx,), sent = _launch(
        body, comms, name="input_grad", grid=(t // tm,),
        out_shape=(jax.ShapeDtypeStruct((t, D_MODEL), F32),),
        in_specs=[pl.BlockSpec((tm, D_MODEL), lambda i: (i, 0)), pl.BlockSpec((tm, IN_COLS), lambda i: (i, 0)),
                  _const((N_SHARDS, D_MODEL, D_MODEL))],
        out_specs=(pl.BlockSpec((tm, D_MODEL), lambda i: (i, 0)),),
        sem=("parallel",),
    )(dr1, dproj, w_in_st)
    return dx, sent


def _in_weight_grad(x, dproj, comms=(), tk=512):
    t = x.shape[0]

    def body(x_r, dp_r, gw_r):
        @pl.when(pl.program_id(1) == 0)
        def _():
            gw_r[...] = jnp.zeros_like(gw_r)

        gw_r[0] += _tdot(x_r[...], dp_r[...])

    (g_w_in,), sent = _launch(
        body, comms, name="in_weight_grad", grid=(N_SHARDS, t // tk),
        out_shape=(jax.ShapeDtypeStruct((N_SHARDS, D_MODEL, D_MODEL), F32),),
        in_specs=[pl.BlockSpec((tk, D_MODEL), lambda j, k: (k, 0)), pl.BlockSpec((tk, D_MODEL), lambda j, k: (k, j))],
        out_specs=(pl.BlockSpec((1, D_MODEL, D_MODEL), lambda j, k: (j, 0, 0)),),
        sem=("parallel", "arbitrary"),
    )(x, dproj)
    return g_w_in, sent


MIXER_W = ("glu_w", "w_ssm_out", "w_conv_out", "w_o")
FFN_W = ("w_gate", "w_up", "w_down")


def _device_step(x, target, small, shards, c_arr, me_arr):
    lr, li = small["ssm_lambda_re"][0], small["ssm_lambda_im"][0]
    ldt = small["ssm_log_dt"][0][:, None]
    rep16 = lambda a: jnp.broadcast_to(a[:, None, :], (N_GROUPS, GROUP_C, a.shape[-1])).reshape(SSM_W, a.shape[-1])
    lr16, li16 = rep16(lr), rep16(li)
    ldt16 = rep16(jnp.broadcast_to(ldt, (N_GROUPS, N_STATE)))
    brt = small["ssm_b_re"][0].transpose(0, 2, 1).reshape(SSM_W, N_STATE)
    bit = small["ssm_b_im"][0].transpose(0, 2, 1).reshape(SSM_W, N_STATE)
    cre = small["ssm_c_re"][0].reshape(SSM_W, N_STATE)
    cim = small["ssm_c_im"][0].reshape(SSM_W, N_STATE)
    disc = (lr, li, ldt, lr16, li16, ldt16, brt, bit)

    pwr, pwi, bbr, bbi, ctr, cti = _ssm_prepare(*disc, cre, cim)
    tab_a, tab_p = _scan_tables(pwr, pwi)

    first_sh = [shards[n] for n in MIXER_W + FFN_W[:1]]
    second_sh = [shards[n] for n in FFN_W[1:]]
    (w_in_st,) = _gather_weights([shards["w_in"]])
    proj, (arrived,) = _in_proj(x, w_in_st, small["b_in"], comms=[_gather_ici(first_sh, [shards["conv_w"]])])
    (xsr, xsi, ya0), (second_part, first_st) = _ssm_forward(
        proj, bbr, bbi, ctr, cti, small["ssm_d"], tab_a, tab_p,
        comms=[_gather_ici(second_sh), _gather_d2d(arrived[:len(first_sh)], first_sh)])
    glu_st, wso_st, wco_st, wo_st, wg_st = (_own_slot(st, sh) for st, sh in zip(first_st, first_sh))
    conv_st = _own_slot(arrived[len(first_sh)], shards["conv_w"])
    conv_w8 = jnp.pad(conv_st[:, :3, :].transpose(1, 0, 2).reshape(3, SSM_W), ((0, SUBLANES - 3), (0, 0)))
    w_o = wo_st.reshape(D_MODEL, D_MODEL)
    glu_w = glu_st.reshape(SSM_W, SSM_W)
    (xhat1, rstd1, ya, yb), (second_st,) = _mixer_forward(
        x, proj, ya0, glu_w, small["glu_b"], wso_st, conv_w8, wco_st, w_o, comms=[_gather_d2d(second_part, second_sh)])
    wu_st, wd_st = (_own_slot(st, sh) for st, sh in zip(second_st, second_sh))
    (loss, dr1, x1b, dr2b, hid, dhg, dhu, g_ln2_g, g_ln2_b, g_ln1_g, g_ln1_b) = _ffn_step(
        xhat1, rstd1, target, small["ln1_g"], small["ln1_b"], small["ln2_g"], small["ln2_b"], wg_st, wu_st, wd_st)

    add_halves = lambda gs, rs: _per_shape(lambda a, b: _add_own_half(a, b, c_arr), list(gs), list(rs))
    sum_chips = lambda owns, slots: _per_shape(lambda a, b: _sum_chips(a, b, me_arr), list(owns), list(slots))
    g_ffn = _ffn_weight_grads(x1b, dr2b, hid, dhg, dhu)
    (dya0, dproj, dbias, g_wo, g_wso, g_wco, g_glu_w, g_glu_b, g_conv8), (got_ffn,) = _mixer_backward(
        dr1, proj, ya0, ya, yb, glu_w, small["glu_b"], wso_st, conv_w8, wco_st, w_o, comms=[_swap_comm(g_ffn)])
    chip_ffn = add_halves(g_ffn, got_ffn)
    g_mix = [g_glu_w.reshape(N_SHARDS, 128, SSM_W), g_wso, g_wco, g_wo.reshape(N_SHARDS, 256, D_MODEL)]
    (dproj, dus, gbbr, gbbi, gctr, gcti, glbr, glbi, g_d), (slots_ffn, got_mix) = _ssm_backward(
        dya0, proj, xsr, xsi, bbr, bbi, ctr, cti, small["ssm_d"], tab_a, tab_p, dproj,
        comms=[_scatter_comm(chip_ffn), _swap_comm(g_mix)])
    halves_ffn = sum_chips(chip_ffn, slots_ffn)
    chip_mix = add_halves(g_mix, got_mix)
    g_lr, g_li, g_ldt, g_brt, g_bit, g_cre, g_cim = _ssm_param_grads(
        *disc, glbr.reshape(SUBLANES, N_GROUPS, N_STATE), glbi.reshape(SUBLANES, N_GROUPS, N_STATE),
        gbbr, gbbi, gctr, gcti)
    g_w_in, (others_ffn, slots_mix) = _in_weight_grad(
        x, dproj, comms=[_send_comm(halves_ffn), _scatter_comm(chip_mix)])
    halves_mix = sum_chips(chip_mix, slots_mix)
    dx, _ = _input_grad(dr1, dproj, w_in_st)

    g_conv = jnp.pad(g_conv8[:3].reshape(3, N_SHARDS, 128).transpose(1, 0, 2), ((0, 0), (0, SUBLANES - 3), (0, 0)))
    pieces = [dus, dbias[:, SSM_W:], g_lr, g_li, g_ldt, g_brt, g_bit, g_cre, g_cim, g_d, g_glu_b,
              g_ln1_g, g_ln1_b, g_ln2_g, g_ln2_b, loss]
    flat = jnp.concatenate([p.reshape(-1) for p in pieces])
    g_packed = jnp.pad(flat, (0, PACKED_ROWS * 128 - flat.shape[0])).reshape(PACKED_ROWS, 128)
    ((got_w, got_conv, got_packed),) = _standalone([_swap_comm([g_w_in], [g_conv, g_packed])], "swap_with_sibling")
    (chip_w,) = add_halves([g_w_in], [got_w])
    chip_conv, chip_packed = _small_pair_sums([g_conv, g_packed], [got_conv, got_packed])
    ((slots_w, slots_conv, slots_packed),) = _standalone(
        [_scatter_comm([chip_w, chip_conv], [chip_packed])], "scatter_to_chips")
    (halves_w,) = sum_chips([chip_w], [slots_w])
    conv_total, packed_total = _small_totals(me_arr, chip_conv, slots_conv, chip_packed, slots_packed)
    (others_rest,) = _standalone([_send_comm([halves_w] + halves_mix)], "send_to_sibling")

    pairs = dict(zip(FFN_W, zip(halves_ffn, others_ffn)))
    pairs.update(zip(("w_in",) + MIXER_W, zip([halves_w] + halves_mix, others_rest)))
    return dx, pairs, conv_total, packed_total


PACKED_ROWS = 1136
PACKED_LAYOUT = (("b_in", IN_COLS), ("ssm_lambda_re", STATE_COLS), ("ssm_lambda_im", STATE_COLS),
                 ("ssm_log_dt", N_GROUPS), ("ssm_b_re", SSM_W * N_STATE), ("ssm_b_im", SSM_W * N_STATE),
                 ("ssm_c_re", SSM_W * N_STATE), ("ssm_c_im", SSM_W * N_STATE), ("ssm_d", SSM_W), ("glu_b", SSM_W),
                 ("ln1_g", D_MODEL), ("ln1_b", D_MODEL), ("ln2_g", D_MODEL), ("ln2_b", D_MODEL), ("loss", 1))


def _unpack_small(packed):
    flat = packed.reshape(-1)
    out, off = {}, 0
    for name, size in PACKED_LAYOUT:
        out[name] = flat[off:off + size]
        off += size
    for name in ("ssm_b_re", "ssm_b_im"):
        out[name] = out[name].reshape(N_GROUPS, GROUP_C, N_STATE).transpose(0, 2, 1)[None]
    for name in ("ssm_c_re", "ssm_c_im"):
        out[name] = out[name].reshape(1, N_GROUPS, GROUP_C, N_STATE)
    for name in ("ssm_lambda_re", "ssm_lambda_im"):
        out[name] = out[name].reshape(1, N_GROUPS, N_STATE)
    for name in ("b_in", "ssm_log_dt", "ssm_d", "glu_b", "ln1_g", "ln1_b", "ln2_g", "ln2_b"):
        out[name] = out[name][None]
    return out


BIG = ("w_in", "glu_w", "w_ssm_out", "w_conv_out", "w_o", "w_gate", "w_up", "w_down")
SMALL = ("b_in", "ssm_lambda_re", "ssm_lambda_im", "ssm_log_dt", "ssm_b_re", "ssm_b_im", "ssm_c_re", "ssm_c_im",
         "ssm_d", "glu_b", "ln1_g", "ln1_b", "ln2_g", "ln2_b")
WEIGHTS = ("w_in", "b_in", "ssm_lambda_re", "ssm_lambda_im", "ssm_log_dt", "ssm_b_re", "ssm_b_im", "ssm_c_re",
           "ssm_c_im", "ssm_d", "glu_w", "glu_b", "w_ssm_out", "conv_w", "w_conv_out", "w_o", "ln1_g", "ln1_b",
           "w_gate", "w_up", "w_down", "ln2_g", "ln2_b")


def _place():
    x, y, c = lax.axis_index("x"), lax.axis_index("y"), lax.axis_index("c")
    chips = [(1 - x, y), (x, 1 - y), (1 - x, 1 - y)]
    return x, y, c, chips


def _shard_of(chip):
    return 2 * chip[0] + chip[1]


def _remote(src, dst, send_sem, recv_sem, to):
    return pltpu.make_async_remote_copy(src_ref=src, dst_ref=dst, send_sem=send_sem, recv_sem=recv_sem,
                                        device_id=to, device_id_type=MESH)


def _half_rows(shard, which):
    r2 = shard.shape[0] // 2
    return pl.ds(pl.multiple_of(which * r2, 16), r2)


def _own_slot(stack, shard):
    me = _shard_of((lax.axis_index("x"), lax.axis_index("y")))
    return lax.dynamic_update_slice(stack, shard[None], (me,) + (0,) * shard.ndim)


def _gather_ici(halved, whole=()):
    shards = list(halved) + list(whole)
    nh = len(halved)

    def copies(src, dst, sems):
        send_sem, recv_sem = sems
        x, y, c, chips = _place()
        me = _shard_of((x, y))
        out = []
        for a in range(len(shards)):
            for k, chip in enumerate(chips):
                if a < nh:
                    rows = _half_rows(shards[a], c)
                    out.append(_remote(src[a].at[rows], dst[a].at[me, rows], send_sem.at[a, k], recv_sem.at[a, k],
                                       (*chip, c)))
                else:
                    out.append(_remote(src[a], dst[a].at[me], send_sem.at[a, k], recv_sem.at[a, k], (*chip, c)))
        return out

    n = len(shards)
    return _Comm(shards, [jax.ShapeDtypeStruct((N_SHARDS,) + s.shape, s.dtype) for s in shards],
                 [pltpu.SemaphoreType.DMA((n, 3))] * 2, copies)


def _gather_d2d(stacks, shards):
    def copies(src, dst, sems):
        del src
        send_sem, recv_sem = sems
        x, y, c, chips = _place()
        out = []
        for a in range(len(stacks)):
            for k, chip in enumerate(chips):
                rows = dst[a].at[_shard_of(chip), _half_rows(shards[a], c)]
                out.append(_remote(rows, rows, send_sem.at[a, k], recv_sem.at[a, k], (x, y, 1 - c)))
        return out

    n = len(stacks)
    return _Comm(stacks, [jax.ShapeDtypeStruct(s.shape, s.dtype) for s in stacks],
                 [pltpu.SemaphoreType.DMA((n, 3))] * 2, copies, aliased=True)


def _standalone(comms, name):
    return _launch(None, comms, name=name, grid=(), in_specs=[], out_specs=(), out_shape=())()[1]


def _gather_weights(shards):
    n = len(shards)

    def body(*refs):
        src, dst = refs[:n], refs[n:2 * n]
        send_sem, recv_sem, fsend_sem, frecv_sem = refs[2 * n:]
        x, y, c, chips = _place()
        me = _shard_of((x, y))
        sibling = (x, y, 1 - c)
        sends = []
        for a in range(n):
            mine = _half_rows(shards[a], c)
            for k, chip in enumerate(chips):
                cp = _remote(src[a].at[mine], dst[a].at[me, mine], send_sem.at[a, k], recv_sem.at[a, k], (*chip, c))
                cp.start()
                sends.append(cp)
        for a in range(n):
            for k, chip in enumerate(chips):
                rows = dst[a].at[_shard_of(chip), _half_rows(shards[a], c)]
                _remote(rows, rows, send_sem.at[a, k], recv_sem.at[a, k], sibling).wait_recv()
                cp = _remote(rows, rows, fsend_sem.at[a, k], frecv_sem.at[a, k], sibling)
                cp.start()
                sends.append(cp)
        for a in range(n):
            for k, chip in enumerate(chips):
                rows = dst[a].at[_shard_of(chip), _half_rows(shards[a], 1 - c)]
                _remote(rows, rows, fsend_sem.at[a, k], frecv_sem.at[a, k], sibling).wait_recv()
        for cp in sends:
            cp.wait_send()

    stacks = pl.pallas_call(
        body, name="gather_weights",
        out_shape=tuple(jax.ShapeDtypeStruct((N_SHARDS,) + s.shape, s.dtype) for s in shards),
        in_specs=[ANY] * n, out_specs=(ANY,) * n,
        scratch_shapes=[pltpu.SemaphoreType.DMA((n, 3))] * 4,
    )(*shards)
    return [_own_slot(st, sh) for st, sh in zip(stacks, shards)]


def _swap_comm(big, small=()):
    nb, n = len(big), len(big) + len(small)
    arrays = list(big) + list(small)

    def copies(src, dst, sems):
        send_sem, recv_sem = sems
        x, y, c, _ = _place()
        out = []
        for a in range(n):
            if a < nb:
                r2 = arrays[a].shape[1] // 2
                part = src[a].at[:, pl.ds(pl.multiple_of((1 - c) * r2, SUBLANES), r2), :]
            else:
                part = src[a]
            out.append(_remote(part, dst[a], send_sem.at[a], recv_sem.at[a], (x, y, 1 - c)))
        return out

    out_shape = [jax.ShapeDtypeStruct((N_SHARDS, g.shape[1] // 2, g.shape[2]), g.dtype) for g in big]
    out_shape += [jax.ShapeDtypeStruct(g.shape, g.dtype) for g in small]
    return _Comm(arrays, out_shape, [pltpu.SemaphoreType.DMA((n,))] * 2, copies)


def _scatter_comm(slabbed, small=()):
    ns, n = len(slabbed), len(slabbed) + len(small)
    arrays = list(slabbed) + list(small)

    def copies(src, dst, sems):
        send_sem, recv_sem = sems
        _, _, c, chips = _place()
        out = []
        for a in range(n):
            for k, chip in enumerate(chips):
                part = src[a].at[_shard_of(chip)] if a < ns else src[a]
                out.append(_remote(part, dst[a].at[k], send_sem.at[a, k], recv_sem.at[a, k], (*chip, c)))
        return out

    out_shape = [jax.ShapeDtypeStruct((3,) + g.shape[1:], g.dtype) for g in slabbed]
    out_shape += [jax.ShapeDtypeStruct((3,) + g.shape, g.dtype) for g in small]
    return _Comm(arrays, out_shape, [pltpu.SemaphoreType.DMA((n, 3))] * 2, copies)


def _send_comm(arrays):
    n = len(arrays)

    def copies(src, dst, sems):
        send_sem, recv_sem = sems
        x, y, c, _ = _place()
        return [_remote(src[a], dst[a], send_sem.at[a], recv_sem.at[a], (x, y, 1 - c)) for a in range(n)]

    return _Comm(arrays, [jax.ShapeDtypeStruct(h.shape, h.dtype) for h in arrays],
                 [pltpu.SemaphoreType.DMA((n,))] * 2, copies)


def _row_chunk(rows):
    for cand in (256, 176, 128, 64):
        if rows % cand == 0:
            return cand
    return rows


def _per_shape(fn, *lists):
    groups = {}
    for i, items in enumerate(zip(*lists)):
        groups.setdefault(tuple(a.shape for a in items), []).append(i)
    out = [None] * len(lists[0])
    for idx in groups.values():
        for i, r in zip(idx, fn(*([lst[i] for i in idx] for lst in lists))):
            out[i] = r
    return out


def _add_own_half(stacks, receiveds, c):
    n = len(stacks)
    _, r2, cols = receiveds[0].shape

    def body(c_ref, *refs):
        del c_ref
        for a in range(n):
            refs[2 * n + a][...] = (refs[a][...] + refs[n + a][...]).astype(BF16)

    own = pl.BlockSpec((1, r2, cols), lambda s, c_ref: (s, c_ref[0], 0))
    got = pl.BlockSpec((1, r2, cols), lambda s, c_ref: (s, 0, 0))
    return pl.pallas_call(
        body, name="add_own_half",
        grid_spec=pltpu.PrefetchScalarGridSpec(
            num_scalar_prefetch=1, grid=(N_SHARDS,), in_specs=[own] * n + [got] * n, out_specs=(got,) * n),
        out_shape=(jax.ShapeDtypeStruct(receiveds[0].shape, BF16),) * n,
        compiler_params=_params(("parallel",)),
    )(c, *stacks, *receiveds)


def _chip_order_sum(me, own, s0, s1, s2):
    terms = []
    for s in range(N_SHARDS):
        d = jnp.bitwise_xor(me, s)
        terms.append(jnp.where(d == 0, own, jnp.where(d == 2, s0, jnp.where(d == 1, s1, s2))))
    return ((terms[0] + terms[1]) + terms[2]) + terms[3]


def _sum_chips(own_stacks, slots, me):
    n = len(slots)
    _, rows, cols = slots[0].shape
    rc = _row_chunk(rows)

    def body(me_ref, *refs):
        del me_ref
        for a in range(n):
            own_r, s_r = refs[a], refs[n + a]
            refs[2 * n + a][...] = (((own_r[0].astype(F32) + s_r[0].astype(F32)) + s_r[1].astype(F32))
                                    + s_r[2].astype(F32))

    own = pl.BlockSpec((1, rc, cols), lambda i, me_ref: (me_ref[0], i, 0))
    three = pl.BlockSpec((3, rc, cols), lambda i, me_ref: (0, i, 0))
    total = pl.BlockSpec((rc, cols), lambda i, me_ref: (i, 0))
    return pl.pallas_call(
        body, name="sum_chips",
        grid_spec=pltpu.PrefetchScalarGridSpec(
            num_scalar_prefetch=1, grid=(rows // rc,), in_specs=[own] * n + [three] * n, out_specs=(total,) * n),
        out_shape=(jax.ShapeDtypeStruct((rows, cols), F32),) * n,
        compiler_params=_params(("parallel",)),
    )(me, *own_stacks, *slots)


def _small_pair_sums(mine, theirs):
    n = len(mine)

    def body(*refs):
        for a in range(n):
            refs[2 * n + a][...] = refs[a][...] + refs[n + a][...]

    vm = pl.BlockSpec(memory_space=pltpu.VMEM)
    return pl.pallas_call(
        body, name="small_pair_sums", out_shape=tuple(jax.ShapeDtypeStruct(g.shape, g.dtype) for g in mine),
        in_specs=[vm] * (2 * n), out_specs=(vm,) * n,
        compiler_params=pltpu.CompilerParams(vmem_limit_bytes=VMEM_LIMIT),
    )(*mine, *theirs)


def _adam_math(w, g, m, v):
    m = ADAM_B1 * m + (1.0 - ADAM_B1) * g
    v = ADAM_B2 * v + (1.0 - ADAM_B2) * (g * g)
    m_hat = m / (1.0 - ADAM_B1 ** ADAM_STEP)
    v_hat = v / (1.0 - ADAM_B2 ** ADAM_STEP)
    delta = -ADAM_LR * (m_hat / (jnp.sqrt(v_hat) + ADAM_EPS) + ADAM_WD * w)
    return delta, m, v


def _small_totals(me, conv_stack, conv_slots, packed, packed_slots):
    def body(me_ref, cs_r, cslot_r, p_r, pslot_r, conv_r, tot_r):
        me_ = me_ref[0]
        conv_r[...] = _chip_order_sum(me_, cs_r[me_], cslot_r[0], cslot_r[1], cslot_r[2])
        tot_r[...] = _chip_order_sum(me_, p_r[...], pslot_r[0], pslot_r[1], pslot_r[2])

    vm = pl.BlockSpec(memory_space=pltpu.VMEM)
    return pl.pallas_call(
        body, name="small_totals",
        out_shape=(jax.ShapeDtypeStruct(conv_stack.shape[1:], F32), jax.ShapeDtypeStruct(packed.shape, F32)),
        in_specs=[pl.BlockSpec(memory_space=pltpu.SMEM)] + [vm] * 4, out_specs=(vm, vm),
    )(me, conv_stack, conv_slots, packed, packed_slots)


def _adam_small(gs, ws, ms, vs):
    n = len(gs)

    def body(*refs):
        for a in range(n):
            g_r, w_r, m_r, v_r = (refs[i * n + a] for i in range(4))
            d_r, nm_r, nv_r = (refs[(4 + i) * n + a] for i in range(3))
            d_r[...], nm_r[...], nv_r[...] = _adam_math(w_r[...], g_r[...], m_r[...], v_r[...])

    vm = pl.BlockSpec(memory_space=pltpu.VMEM)
    shapes = tuple(jax.ShapeDtypeStruct(w.shape, F32) for w in ws)
    out = pl.pallas_call(
        body, name="adam_small", out_shape=shapes * 3, in_specs=[vm] * (4 * n), out_specs=(vm,) * (3 * n),
        compiler_params=pltpu.CompilerParams(vmem_limit_bytes=VMEM_LIMIT),
    )(*gs, *ws, *ms, *vs)
    return out[:n], out[n:2 * n], out[2 * n:]


def _adam_big(ws, mines, others, ms, vs, c):
    n = len(ws)
    r2, cols = mines[0].shape
    rc = _row_chunk(r2)
    nch = r2 // rc

    def body(c_ref, *refs):
        mine_is_here = pl.program_id(0) == c_ref[0]
        for a in range(n):
            w_r, mine_r, other_r, m_r, v_r = (refs[i * n + a] for i in range(5))
            g_r, d_r, nm_r, nv_r = (refs[(5 + i) * n + a] for i in range(4))
            g = jnp.where(mine_is_here, mine_r[...], other_r[...])
            g_r[...] = g
            d_r[...], nm_r[...], nv_r[...] = _adam_math(w_r[...], g, m_r[...], v_r[...])

    full = pl.BlockSpec((rc, cols), lambda h, i, c_ref: (h * nch + i, 0))
    half = pl.BlockSpec((rc, cols), lambda h, i, c_ref: (i, 0))
    shape = jax.ShapeDtypeStruct((2 * r2, cols), F32)
    out = pl.pallas_call(
        body, name="adam_big",
        grid_spec=pltpu.PrefetchScalarGridSpec(
            num_scalar_prefetch=1, grid=(2, nch),
            in_specs=[full] * n + [half] * (2 * n) + [full] * (2 * n), out_specs=(full,) * (4 * n)),
        out_shape=(shape,) * (4 * n), compiler_params=_params(("parallel", "parallel")),
    )(c, *ws, *mines, *others, *ms, *vs)
    return [tuple(out[i * n + a] for i in range(4)) for a in range(n)]


def kernel(x, w_in, b_in, ssm_lambda_re, ssm_lambda_im, ssm_log_dt, ssm_b_re, ssm_b_im, ssm_c_re, ssm_c_im, ssm_d, glu_w, glu_b, w_ssm_out, conv_w, w_conv_out, w_o, ln1_g, ln1_b, w_gate, w_up, w_down, ln2_g, ln2_b, loss_target, m_w_in, m_b_in, m_ssm_lambda_re, m_ssm_lambda_im, m_ssm_log_dt, m_ssm_b_re, m_ssm_b_im, m_ssm_c_re, m_ssm_c_im, m_ssm_d, m_glu_w, m_glu_b, m_w_ssm_out, m_conv_w, m_w_conv_out, m_w_o, m_ln1_g, m_ln1_b, m_w_gate, m_w_up, m_w_down, m_ln2_g, m_ln2_b, v_w_in, v_b_in, v_ssm_lambda_re, v_ssm_lambda_im, v_ssm_log_dt, v_ssm_b_re, v_ssm_b_im, v_ssm_c_re, v_ssm_c_im, v_ssm_d, v_glu_w, v_glu_b, v_w_ssm_out, v_conv_w, v_w_conv_out, v_w_o, v_ln1_g, v_ln1_b, v_w_gate, v_w_up, v_w_down, v_ln2_g, v_ln2_b):
    given = dict(locals())
    w = {n: given[n] for n in WEIGHTS}
    m = {n: given["m_" + n] for n in WEIGHTS}
    v = {n: given["v_" + n] for n in WEIGHTS}

    flip = lambda n, a: a.T if n in ("w_gate", "w_up") else a
    shards = {n: flip(n, w[n][0]).astype(BF16) for n in BIG}
    shards["conv_w"] = jnp.pad(conv_w[0], ((0, SUBLANES - 3), (0, 0)))
    c_arr = jnp.reshape(lax.axis_index("c"), (1,)).astype(jnp.int32)
    me = _shard_of((lax.axis_index("x"), lax.axis_index("y")))
    me_arr = jnp.reshape(me, (1,)).astype(jnp.int32)
    dx, pairs, conv_total, packed_total = _device_step(
        x[0], loss_target[0], {n: w[n] for n in SMALL}, shards, c_arr, me_arr)

    grad = _unpack_small(packed_total)
    loss_total = grad.pop("loss")[0]
    grad["conv_w"] = conv_total[:3][None]
    small_names = ("conv_w",) + SMALL
    swap = lambda n, a: a.transpose(0, 1, 3, 2) if n in ("ssm_b_re", "ssm_b_im") else a
    ds, nms, nvs = _adam_small(*([swap(n, d[n]) for n in small_names] for d in (grad, w, m, v)))
    delta, new_m, new_v = {}, {}, {}
    for i, n in enumerate(small_names):
        delta[n], new_m[n], new_v[n] = swap(n, ds[i]), swap(n, nms[i]), swap(n, nvs[i])
    updated = _per_shape(
        lambda *a: _adam_big(*a, c_arr),
        [flip(n, w[n][0]) for n in BIG], [pairs[n][0] for n in BIG], [pairs[n][1] for n in BIG],
        [flip(n, m[n][0]) for n in BIG], [flip(n, v[n][0]) for n in BIG])
    for n, results in zip(BIG, updated):
        grad[n], delta[n], new_m[n], new_v[n] = (flip(n, r)[None] for r in results)

    return (loss_total, dx[None], *[grad[n] for n in WEIGHTS], *[delta[n] for n in WEIGHTS],
            *[new_m[n] for n in WEIGHTS], *[new_v[n] for n in WEIGHTS])
```

```python
import functools
import math

import jax
import jax.numpy as jnp
from jax import lax
from jax.experimental import pallas as pl
from jax.experimental.pallas import tpu as pltpu

F32 = jnp.float32
BF16 = jnp.bfloat16

D_MODEL = 1024
IN_COLS = 4096
SSM_W = 512
N_GROUPS = 32
N_STATE = 64
GROUP_C = 16
STATE_COLS = N_GROUPS * N_STATE
STRIP = 512
N_STRIPS = STATE_COLS // STRIP
FFN_SHARD = 704
N_SHARDS = 4
ALPHA = 2.0 ** 0.25
LN_EPS = 1e-5
GELU_K = math.sqrt(2.0 / math.pi)
GELU_C = 0.044715

ADAM_LR = 0.001
ADAM_B1 = 0.9
ADAM_B2 = 0.999
ADAM_EPS = 1e-08
ADAM_WD = 0.01
ADAM_STEP = 10

V7X_VMEM_BYTES = 64 * 1024 * 1024
VMEM_LIMIT = V7X_VMEM_BYTES - 8 * 1024 * 1024
SUBLANES = 8
N_POWERS = 128

MESH = pl.DeviceIdType.MESH
ANY = pl.BlockSpec(memory_space=pl.ANY)


def _dot(a, b):
    return jnp.dot(a.astype(BF16), b.astype(BF16), preferred_element_type=F32)


def _dot_t(a, b):
    return lax.dot_general(a.astype(BF16), b.astype(BF16), (((1,), (1,)), ((), ())),
                           preferred_element_type=F32)


def _tdot(a, b):
    return lax.dot_general(a.astype(BF16), b.astype(BF16), (((0,), (0,)), ((), ())),
                           preferred_element_type=F32)


def _sigmoid(v):
    return 1.0 / (1.0 + jnp.exp(-v))


def _split3(v):
    hi = v.astype(BF16)
    r1 = v - hi.astype(F32)
    mid = r1.astype(BF16)
    lo = (r1 - mid.astype(F32)).astype(BF16)
    return hi, mid, lo


def _exact_dot(v, sel):
    hi, mid, lo = _split3(v)
    return (jnp.dot(hi, sel, preferred_element_type=F32)
            + jnp.dot(mid, sel, preferred_element_type=F32)
            + jnp.dot(lo, sel, preferred_element_type=F32))


def _const(shape):
    nd = len(shape)
    return pl.BlockSpec(shape, lambda *_: (0,) * nd)


def _params(sem, vmem=VMEM_LIMIT):
    return pltpu.CompilerParams(dimension_semantics=sem, vmem_limit_bytes=vmem)


def _gelu_parts(v):
    inner = GELU_K * (v + GELU_C * v * v * v)
    t = jnp.tanh(inner)
    g = 0.5 * v * (1.0 + t)
    dg = 0.5 * (1.0 + t) + 0.5 * v * (1.0 - t * t) * GELU_K * (1.0 + 3.0 * GELU_C * v * v)
    return g, dg


class _Comm:
    def __init__(self, inputs, out_shape, sems, copies, aliased=False):
        self.inputs, self.out_shape, self.sems = list(inputs), tuple(out_shape), list(sems)
        self.copies, self.aliased = copies, aliased


def _launch(body, comms, *, name, grid, in_specs, out_specs, out_shape, scratch_shapes=(), aliases=None, sem=None):
    comms = list(comms)
    n_in, n_out, n_scr = len(in_specs), len(out_specs), len(scratch_shapes)
    aliases = dict(aliases or {})
    layout = []
    p_in, p_out, p_sem = n_in, n_out, 0
    for cm in comms:
        layout.append((p_in, p_out, p_sem))
        if cm.aliased:
            for i in range(len(cm.inputs)):
                aliases[p_in + i] = p_out + i
        p_in, p_out, p_sem = p_in + len(cm.inputs), p_out + len(cm.out_shape), p_sem + len(cm.sems)
    tot_in, tot_out = p_in, p_out

    def fused(*refs):
        ins, outs = refs[:tot_in], refs[tot_in:tot_in + tot_out]
        scr = refs[tot_in + tot_out:tot_in + tot_out + n_scr]
        sems = refs[tot_in + tot_out + n_scr:]

        def descriptors():
            out = []
            for cm, (a, b, s) in zip(comms, layout):
                out += cm.copies(ins[a:a + len(cm.inputs)], outs[b:b + len(cm.out_shape)], sems[s:s + len(cm.sems)])
            return out

        steps = [pl.program_id(d) for d in range(len(grid))]
        first = functools.reduce(jnp.logical_and, [s == 0 for s in steps]) if grid else None
        last = functools.reduce(jnp.logical_and, [s == g - 1 for s, g in zip(steps, grid)]) if grid else None

        def start():
            for cp in descriptors():
                cp.start()

        def finish():
            for cp in descriptors():
                cp.wait()

        if comms:
            pl.when(first)(start) if grid else start()
        if body is not None:
            body(*ins[:n_in], *outs[:n_out], *scr)
        if comms:
            pl.when(last)(finish) if grid else finish()

    specs_in = list(in_specs) + [ANY] * (tot_in - n_in)
    specs_out = tuple(out_specs) + (ANY,) * (tot_out - n_out)
    shapes = tuple(out_shape) + tuple(s for cm in comms for s in cm.out_shape)
    scratch = list(scratch_shapes) + [s for cm in comms for s in cm.sems]
    if comms or sem is None:
        sem = ("arbitrary",) * len(grid)
    kwargs = dict(grid=grid) if grid else {}
    call = pl.pallas_call(fused, name=name, out_shape=shapes, in_specs=specs_in, out_specs=specs_out,
                          scratch_shapes=scratch, input_output_aliases=aliases,
                          compiler_params=_params(sem) if grid else None, **kwargs)

    def run(*args):
        out = call(*args, *(a for cm in comms for a in cm.inputs))
        results, rest = out[:n_out], out[n_out:]
        per_comm = []
        for cm in comms:
            per_comm.append(rest[:len(cm.out_shape)])
            rest = rest[len(cm.out_shape):]
        return results, per_comm

    return run


def _ssm_discretise(lr, li, ldt, lr16, li16, ldt16, brt, bit):
    def lam_bar(lr_, li_, ldt_):
        dt = jnp.exp(ldt_)
        mag = jnp.exp(lr_ * dt)
        return mag * jnp.cos(li_ * dt), mag * jnp.sin(li_ * dt)

    lb_re, lb_im = lam_bar(lr, li, ldt)
    l16_re, l16_im = lam_bar(lr16, li16, ldt16)
    den = lr16 * lr16 + li16 * li16
    num_re = l16_re - 1.0
    fr = (num_re * lr16 + l16_im * li16) / den
    fi = (l16_im * lr16 - num_re * li16) / den
    bb_re = fr * brt - fi * bit
    bb_im = fr * bit + fi * brt
    return lb_re, lb_im, bb_re, bb_im


def _strip_selectors():
    p = lax.broadcasted_iota(jnp.int32, (N_STATE, STRIP), 0)
    col = lax.broadcasted_iota(jnp.int32, (N_STATE, STRIP), 1)
    rep = ((col & (N_STATE - 1)) == p).astype(BF16)
    row = lax.broadcasted_iota(jnp.int32, (SSM_W, STRIP), 0)
    col2 = lax.broadcasted_iota(jnp.int32, (SSM_W, STRIP), 1)
    mask = (((row >> 4) & 7) == (col2 >> 6))
    return rep, mask


def _ssm_prepare(lr, li, ldt, lr16, li16, ldt16, brt, bit, cre, cim):
    def body(lr_r, li_r, ldt_r, lr16_r, li16_r, ldt16_r, brt_r, bit_r, cre_r, cim_r,
             pwr_r, pwi_r, bbr_r, bbi_r, ctr_r, cti_r):
        lb_re, lb_im, bb_re, bb_im = _ssm_discretise(
            lr_r[...], li_r[...], ldt_r[...], lr16_r[...], li16_r[...], ldt16_r[...], brt_r[...], bit_r[...])
        pr, pi_ = lb_re, lb_im
        pwr_r[0] = pr
        pwi_r[0] = pi_
        for k in range(1, N_POWERS):
            pr, pi_ = pr * lb_re - pi_ * lb_im, pr * lb_im + pi_ * lb_re
            pwr_r[k] = pr
            pwi_r[k] = pi_
        rep, mask = _strip_selectors()
        for src, dst in ((bb_re, bbr_r), (bb_im, bbi_r), (cre_r[...], ctr_r), (cim_r[...], cti_r)):
            wide = jnp.dot(src.astype(BF16), rep, preferred_element_type=F32)
            dst[...] = jnp.where(mask, wide, 0.0).astype(BF16)

    vm = pl.BlockSpec(memory_space=pltpu.VMEM)
    return pl.pallas_call(
        body, name="ssm_prepare",
        out_shape=(jax.ShapeDtypeStruct((N_POWERS, N_GROUPS, N_STATE), F32),) * 2
        + (jax.ShapeDtypeStruct((SSM_W, STRIP), BF16),) * 4,
        in_specs=[vm] * 10, out_specs=(vm,) * 6,
    )(lr, li, ldt, lr16, li16, ldt16, brt, bit, cre, cim)


def _scan_tables(pwr, pwi):
    pr = pwr.reshape(N_POWERS, STATE_COLS)
    pi_ = pwi.reshape(N_POWERS, STATE_COLS)
    rows8 = lambda v: jnp.broadcast_to(v[None], (SUBLANES, STATE_COLS))
    tab_a = jnp.stack([rows8(pr[0]), rows8(pi_[0]), rows8(pr[-1]), rows8(pi_[-1])])
    tab_p = jnp.stack([jnp.repeat(pr, SUBLANES, axis=0), jnp.repeat(pi_, SUBLANES, axis=0)])
    return tab_a, tab_p


def _ssm_param_grads(lr, li, ldt, lr16, li16, ldt16, brt, bit, dlbr, dlbi, dbbr, dbbi, dctr, dcti):
    def body(lr_r, li_r, ldt_r, lr16_r, li16_r, ldt16_r, brt_r, bit_r,
             dlbr_r, dlbi_r, dbbr_r, dbbi_r, dctr_r, dcti_r,
             glr_r, gli_r, gldt_r, gbrt_r, gbit_r, gcre_r, gcim_r):
        rep, mask = _strip_selectors()

        def fold(acc):
            return sum(lax.dot_general(t, rep, (((1,), (1,)), ((), ())), preferred_element_type=F32)
                       for t in _split3(jnp.where(mask, acc, 0.0)))

        g_lb_re = jnp.sum(dlbr_r[...], axis=0)
        g_lb_im = jnp.sum(dlbi_r[...], axis=0)
        g_bb_re = fold(dbbr_r[...])
        g_bb_im = fold(dbbi_r[...])
        gcre_r[...] = fold(dctr_r[...])
        gcim_r[...] = fold(dcti_r[...])
        prim = (lr_r[...], li_r[...], ldt_r[...], lr16_r[...], li16_r[...], ldt16_r[...], brt_r[...], bit_r[...])
        _, vjp = jax.vjp(_ssm_discretise, *prim)
        g_lr, g_li, g_ldt, g_lr16, g_li16, g_ldt16, g_brt, g_bit = vjp((g_lb_re, g_lb_im, g_bb_re, g_bb_im))
        grp = lax.broadcasted_iota(jnp.int32, (N_GROUPS, SSM_W), 0)
        rw = lax.broadcasted_iota(jnp.int32, (N_GROUPS, SSM_W), 1)
        gsum = ((rw >> 4) == grp).astype(BF16)

        def group_sum(v):
            return sum(jnp.dot(gsum, t, preferred_element_type=F32) for t in _split3(v))

        glr_r[...] = g_lr + group_sum(g_lr16)
        gli_r[...] = g_li + group_sum(g_li16)
        gldt_r[...] = g_ldt + jnp.sum(group_sum(g_ldt16), axis=1, keepdims=True)
        gbrt_r[...] = g_brt
        gbit_r[...] = g_bit

    vm = pl.BlockSpec(memory_space=pltpu.VMEM)
    gp = jax.ShapeDtypeStruct((N_GROUPS, N_STATE), F32)
    gb = jax.ShapeDtypeStruct((SSM_W, N_STATE), F32)
    return pl.pallas_call(
        body, name="ssm_param_grads",
        out_shape=(gp, gp, jax.ShapeDtypeStruct((N_GROUPS, 1), F32), gb, gb, gb, gb),
        in_specs=[vm] * 14, out_specs=(vm,) * 7,
    )(lr, li, ldt, lr16, li16, ldt16, brt, bit, dlbr, dlbi, dbbr, dbbi, dctr, dcti)


def _in_proj(x, w_in_st, b_in, comms=()):
    t = x.shape[0]
    tm = 512

    def body(x_r, w_r, b_r, o_r):
        xb = x_r[...].astype(BF16)
        for j in range(N_SHARDS):
            cols = slice(D_MODEL * j, D_MODEL * (j + 1))
            o_r[:, cols] = jnp.dot(xb, w_r[j], preferred_element_type=F32) + b_r[:, cols]

    (proj,), sent = _launch(
        body, comms, name="in_proj", grid=(t // tm,),
        out_shape=(jax.ShapeDtypeStruct((t, IN_COLS), F32),),
        in_specs=[pl.BlockSpec((tm, D_MODEL), lambda i: (i, 0)), _const((N_SHARDS, D_MODEL, D_MODEL)),
                  _const((1, IN_COLS))],
        out_specs=(pl.BlockSpec((tm, IN_COLS), lambda i: (i, 0)),),
        sem=("parallel",),
    )(x, w_in_st, b_in)
    return proj, sent


def _cmul_add(xr, xi, mr, mi, sr, si):
    return xr + (mr * sr - mi * si), xi + (mr * si + mi * sr)


SCAN_STEPS = N_POWERS
SCAN_CHUNK = SUBLANES * SCAN_STEPS


def _interleave(src_r, dst_r):
    for step in range(SCAN_STEPS):
        dst_r[SUBLANES * step:SUBLANES * (step + 1), :] = src_r[pl.ds(step, SUBLANES, stride=SCAN_STEPS), :]


def _deinterleave(src_r, dst_r):
    for step in range(SCAN_STEPS):
        dst_r[pl.ds(step, SUBLANES, stride=SCAN_STEPS), :] = src_r[SUBLANES * step:SUBLANES * (step + 1), :]


def _step_rows(step):
    return pl.ds(pl.multiple_of(step * SUBLANES, SUBLANES), SUBLANES)


def _segment_states(first_r, first_i, ends_r, ends_i, a64_r, a64_i, order):
    row = lax.broadcasted_iota(jnp.int32, ends_r.shape, 0)
    cur_r, cur_i = first_r, first_i
    ent_r = jnp.zeros_like(ends_r)
    ent_i = jnp.zeros_like(ends_i)
    for s in order:
        ent_r = jnp.where(row == s, jnp.broadcast_to(cur_r, ends_r.shape), ent_r)
        ent_i = jnp.where(row == s, jnp.broadcast_to(cur_i, ends_i.shape), ent_i)
        cur_r, cur_i = _cmul_add(ends_r[s:s + 1, :], ends_i[s:s + 1, :], a64_r, a64_i, cur_r, cur_i)
    return ent_r, ent_i, cur_r, cur_i


def _ssm_forward(proj, bbr, bbi, ctr, cti, d_skip, tab_a, tab_p, comms=(), tc=SCAN_CHUNK):
    t = proj.shape[0]

    def body(u_r, bbr_r, bbi_r, ctr_r, cti_r, d_r, ta_r, tp_r, xsr_r, xsi_r, y_r, ui_s, yi_s, car_r, car_i):
        @pl.when(pl.program_id(1) == 0)
        def _():
            car_r[...] = jnp.zeros_like(car_r)
            car_i[...] = jnp.zeros_like(car_i)

        _interleave(u_r, ui_s)
        u = ui_s[...]
        xsr_r[...] = _dot(u, bbr_r[...])
        xsi_r[...] = _dot(u, bbi_r[...])
        a_r, a_i = ta_r[0], ta_r[1]

        def local(step, carry):
            rows = _step_rows(step)
            xr, xi = _cmul_add(xsr_r[rows, :], xsi_r[rows, :], a_r, a_i, *carry)
            xsr_r[rows, :] = xr
            xsi_r[rows, :] = xi
            return xr, xi

        zero = jnp.zeros((SUBLANES, STRIP), F32)
        ends_r, ends_i = lax.fori_loop(0, SCAN_STEPS, local, (zero, zero), unroll=2)
        ent_r, ent_i, out_r, out_i = _segment_states(
            car_r[...], car_i[...], ends_r, ends_i, ta_r[2, 0:1, :], ta_r[3, 0:1, :], range(SUBLANES))
        car_r[...] = out_r
        car_i[...] = out_i

        def entering(step, _):
            rows = _step_rows(step)
            xr, xi = _cmul_add(xsr_r[rows, :], xsi_r[rows, :], tp_r[0, rows, :], tp_r[1, rows, :], ent_r, ent_i)
            xsr_r[rows, :] = xr
            xsi_r[rows, :] = xi
            return 0

        lax.fori_loop(0, SCAN_STEPS, entering, 0, unroll=4)
        yi_s[...] = _dot_t(xsr_r[...], ctr_r[...]) - _dot_t(xsi_r[...], cti_r[...]) + d_r[...] * u
        _deinterleave(yi_s, y_r)

    strip_mat = pl.BlockSpec((128, STRIP), lambda j, k: (j, 0))
    states = pl.BlockSpec((tc, STRIP), lambda j, k: (k, j))
    return _launch(
        body, comms, name="ssm_forward", grid=(N_STRIPS, t // tc),
        out_shape=(jax.ShapeDtypeStruct((t, STATE_COLS), F32), jax.ShapeDtypeStruct((t, STATE_COLS), F32),
                   jax.ShapeDtypeStruct((t, SSM_W), F32)),
        in_specs=[pl.BlockSpec((tc, 128), lambda j, k: (k, j)),
                  strip_mat, strip_mat, strip_mat, strip_mat,
                  pl.BlockSpec((1, 128), lambda j, k: (0, j)),
                  pl.BlockSpec((4, SUBLANES, STRIP), lambda j, k: (0, 0, j)),
                  pl.BlockSpec((2, tc, STRIP), lambda j, k: (0, 0, j))],
        out_specs=(states, states, pl.BlockSpec((tc, 128), lambda j, k: (k, j))),
        scratch_shapes=[pltpu.VMEM((tc, 128), F32), pltpu.VMEM((tc, 128), F32),
                        pltpu.VMEM((1, STRIP), F32), pltpu.VMEM((1, STRIP), F32)],
        sem=("parallel", "arbitrary"),
    )(proj, bbr, bbi, ctr, cti, d_skip, tab_a, tab_p)


def _shift_down(v, prev, n):
    row = lax.broadcasted_iota(jnp.int32, v.shape, 0)
    out = pltpu.roll(v, n, 0)
    for r in range(n):
        src = prev[SUBLANES - n + r:SUBLANES - n + r + 1, :]
        out = jnp.where(row == r, jnp.broadcast_to(src, v.shape), out)
    return out


def _shift_up(v, nxt, n):
    rows = v.shape[0]
    row = lax.broadcasted_iota(jnp.int32, v.shape, 0)
    out = pltpu.roll(v, rows - n, 0)
    for r in range(n):
        src = nxt[r:r + 1, :]
        out = jnp.where(row == rows - n + r, jnp.broadcast_to(src, v.shape), out)
    return out


def _conv3(q, q_prev, w):
    return w[2:3, :] * q + w[1:2, :] * _shift_down(q, q_prev, 1) + w[0:1, :] * _shift_down(q, q_prev, 2)


def _mixer_forward(x, proj, ya0, glu_w, glu_b, wso_st, conv_w8, wco_st, w_o, comms=(), tm=256):
    t = x.shape[0]
    hb = tm // SUBLANES

    def body(x_r, ya0_r, h_r, cg_r, bg_r, ga_r, gb_r, hp_r, cgp_r,
             glu_w_r, glu_b_r, wso_r, cw_r, wco_r, wo_r, xh_r, rstd_r, ya_r, yb_r):
        i = pl.program_id(0)
        g, _ = _gelu_parts(ya0_r[...])
        ya1 = g * _sigmoid(_dot(g, glu_w_r[...]) + glu_b_r[...])
        q = cg_r[...] * h_r[...]
        q_prev = jnp.where(i > 0, cgp_r[...] * hp_r[...], 0.0)
        yb0 = bg_r[...] * _conv3(q, q_prev, cw_r[...])
        for j in range(N_SHARDS):
            ya_r[:, 256 * j:256 * (j + 1)] = _dot(ya1, wso_r[j])
            yb_r[:, 256 * j:256 * (j + 1)] = _dot(yb0, wco_r[j])
        merged = _sigmoid(ga_r[...]) * ya_r[...] + _sigmoid(gb_r[...]) * yb_r[...]
        r1 = ALPHA * x_r[...] + _dot(merged, wo_r[...])
        mu = jnp.mean(r1, axis=-1, keepdims=True)
        cen = r1 - mu
        rstd = lax.rsqrt(jnp.mean(cen * cen, axis=-1, keepdims=True) + LN_EPS)
        xh_r[...] = cen * rstd
        rstd_r[...] = rstd

    def col(w, c):
        return pl.BlockSpec((tm, w), lambda i: (i, c))

    def prev(c):
        return pl.BlockSpec((SUBLANES, SSM_W), lambda i: (jnp.maximum(i * hb - 1, 0), c))

    return _launch(
        body, comms, name="mixer_forward", grid=(t // tm,),
        out_shape=(jax.ShapeDtypeStruct((t, D_MODEL), F32), jax.ShapeDtypeStruct((t, 1), F32),
                   jax.ShapeDtypeStruct((t, D_MODEL), F32), jax.ShapeDtypeStruct((t, D_MODEL), F32)),
        in_specs=[col(D_MODEL, 0), col(SSM_W, 0), col(SSM_W, 1), col(SSM_W, 2), col(SSM_W, 3),
                  col(D_MODEL, 2), col(D_MODEL, 3), prev(1), prev(2),
                  _const((SSM_W, SSM_W)), _const((1, SSM_W)), _const((N_SHARDS, SSM_W, 256)),
                  _const((SUBLANES, SSM_W)), _const((N_SHARDS, SSM_W, 256)), _const((D_MODEL, D_MODEL))],
        out_specs=(col(D_MODEL, 0), pl.BlockSpec((tm, 1), lambda i: (i, 0)), col(D_MODEL, 0), col(D_MODEL, 0)),
        sem=("parallel",),
    )(x, ya0, proj, proj, proj, proj, proj, proj, proj, glu_w, glu_b, wso_st, conv_w8, wco_st, w_o)


def _layer_norm_bwd(dxhat, xhat, rstd):
    m1 = jnp.mean(dxhat, axis=-1, keepdims=True)
    m2 = jnp.mean(dxhat * xhat, axis=-1, keepdims=True)
    return rstd * (dxhat - m1 - xhat * m2)


def _ffn_step(xhat1, rstd1, target, ln1_g, ln1_b, ln2_g, ln2_b, wg_st, wu_st, wd_st, tm=256):
    t = xhat1.shape[0]

    def body(xh_r, rstd_r, tgt_r, g1_r, b1_r, g2_r, b2_r, wg_r, wu_r, wd_r,
             loss_r, dr1_r, x1b_r, dr2b_r, hid_r, dhg_r, dhu_r, dg2_r, db2_r, dg1_r, db1_r,
             hg_s, hu_s):
        @pl.when(pl.program_id(0) == 0)
        def _():
            for r in (loss_r, dg2_r, db2_r, dg1_r, db1_r):
                r[...] = jnp.zeros_like(r)

        xhat1_v = xh_r[...]
        x1 = xhat1_v * g1_r[...] + b1_r[...]
        x1b = x1.astype(BF16)
        x1b_r[...] = x1b
        ffn = jnp.zeros((tm, D_MODEL), F32)
        for j in range(N_SHARDS):
            hg = lax.dot_general(x1b, wg_r[j], (((1,), (1,)), ((), ())), preferred_element_type=F32)
            hu = lax.dot_general(x1b, wu_r[j], (((1,), (1,)), ((), ())), preferred_element_type=F32)
            hg_s[j] = hg
            hu_s[j] = hu
            hid = (hg * _sigmoid(hg) * hu).astype(BF16)
            hid_r[j] = hid
            ffn = ffn + jnp.dot(hid, wd_r[j], preferred_element_type=F32)
        r2 = ALPHA * x1 + ffn
        mu = jnp.mean(r2, axis=-1, keepdims=True)
        cen = r2 - mu
        rstd2 = lax.rsqrt(jnp.mean(cen * cen, axis=-1, keepdims=True) + LN_EPS)
        xhat2 = cen * rstd2
        diff = (xhat2 * g2_r[...] + b2_r[...]) - tgt_r[...]
        loss_r[...] += 0.5 * jnp.sum(jnp.mean(diff * diff, axis=-1, keepdims=True), axis=0, keepdims=True)
        dy = diff * (1.0 / D_MODEL)
        dg2_r[...] += jnp.sum(dy * xhat2, axis=0, keepdims=True)
        db2_r[...] += jnp.sum(dy, axis=0, keepdims=True)
        dr2 = _layer_norm_bwd(dy * g2_r[...], xhat2, rstd2)
        dr2b = dr2.astype(BF16)
        dr2b_r[...] = dr2b
        dx1 = ALPHA * dr2
        for j in range(N_SHARDS):
            dhid = lax.dot_general(dr2b, wd_r[j], (((1,), (1,)), ((), ())), preferred_element_type=F32)
            hg = hg_s[j]
            hu = hu_s[j]
            sg = _sigmoid(hg)
            dhu = (dhid * (hg * sg)).astype(BF16)
            dhg = (dhid * hu * (sg * (1.0 + hg * (1.0 - sg)))).astype(BF16)
            dhg_r[j] = dhg
            dhu_r[j] = dhu
            dx1 = dx1 + jnp.dot(dhg, wg_r[j], preferred_element_type=F32)
            dx1 = dx1 + jnp.dot(dhu, wu_r[j], preferred_element_type=F32)
        dg1_r[...] += jnp.sum(dx1 * xhat1_v, axis=0, keepdims=True)
        db1_r[...] += jnp.sum(dx1, axis=0, keepdims=True)
        dr1_r[...] = _layer_norm_bwd(dx1 * g1_r[...], xhat1_v, rstd_r[...])

    tile = pl.BlockSpec((tm, D_MODEL), lambda i: (i, 0))
    hidden = pl.BlockSpec((N_SHARDS, tm, FFN_SHARD), lambda i: (0, i, 0))
    vec = _const((1, D_MODEL))
    hid_shape = jax.ShapeDtypeStruct((N_SHARDS, t, FFN_SHARD), BF16)
    vec_shape = jax.ShapeDtypeStruct((1, D_MODEL), F32)
    return pl.pallas_call(
        body, name="ffn_step", grid=(t // tm,),
        out_shape=(jax.ShapeDtypeStruct((1, 1), F32), jax.ShapeDtypeStruct((t, D_MODEL), F32),
                   jax.ShapeDtypeStruct((t, D_MODEL), BF16), jax.ShapeDtypeStruct((t, D_MODEL), BF16),
                   hid_shape, hid_shape, hid_shape, vec_shape, vec_shape, vec_shape, vec_shape),
        in_specs=[tile, pl.BlockSpec((tm, 1), lambda i: (i, 0)), tile, vec, vec, vec, vec,
                  _const((N_SHARDS, FFN_SHARD, D_MODEL)), _const((N_SHARDS, FFN_SHARD, D_MODEL)),
                  _const((N_SHARDS, FFN_SHARD, D_MODEL))],
        out_specs=(_const((1, 1)), tile, tile, tile, hidden, hidden, hidden, vec, vec, vec, vec),
        scratch_shapes=[pltpu.VMEM((N_SHARDS, tm, FFN_SHARD), F32), pltpu.VMEM((N_SHARDS, tm, FFN_SHARD), F32)],
        compiler_params=_params(("arbitrary",)),
    )(xhat1, rstd1, target, ln1_g, ln1_b, ln2_g, ln2_b, wg_st, wu_st, wd_st)


def _ffn_weight_grads(x1b, dr2b, hid, dhg, dhu, tk=1024):
    t = x1b.shape[0]

    def body(x_r, dr_r, hid_r, dhg_r, dhu_r, gwg_r, gwu_r, gwd_r):
        @pl.when(pl.program_id(1) == 0)
        def _():
            for r in (gwg_r, gwu_r, gwd_r):
                r[...] = jnp.zeros_like(r)

        gwg_r[0] += _tdot(dhg_r[0], x_r[...])
        gwu_r[0] += _tdot(dhu_r[0], x_r[...])
        gwd_r[0] += _tdot(hid_r[0], dr_r[...])

    tile = pl.BlockSpec((tk, D_MODEL), lambda j, k: (k, 0))
    hidden = pl.BlockSpec((1, tk, FFN_SHARD), lambda j, k: (j, k, 0))
    row = pl.BlockSpec((1, FFN_SHARD, D_MODEL), lambda j, k: (j, 0, 0))
    return pl.pallas_call(
        body, name="ffn_weight_grads", grid=(N_SHARDS, t // tk),
        out_shape=(jax.ShapeDtypeStruct((N_SHARDS, FFN_SHARD, D_MODEL), F32),) * 3,
        in_specs=[tile, tile, hidden, hidden, hidden],
        out_specs=(row, row, row),
        compiler_params=_params(("parallel", "arbitrary")),
    )(x1b, dr2b, hid, dhg, dhu)


def _mixer_backward(dr1, proj, ya0, ya, yb, glu_w, glu_b, wso_st, conv_w8, wco_st, w_o, comms=(), tm=256):
    t = dr1.shape[0]
    hb = tm // SUBLANES
    last_block = t // SUBLANES - 1

    def body(dr1_r, dr1n_r, ya0_r, ya_r, yb_r, h_r, cg_r, bg_r, ga_r, gb_r, hp_r, cgp_r, bgn_r, gbn_r,
             glu_w_r, glu_b_r, wso_r, cw_r, wco_r, wo_r,
             dya0_r, dproj_r, dbias_r, gwo_r, gwso_r, gwco_r, gglu_w_r, gglu_b_r, gconv_r):
        i = pl.program_id(0)

        @pl.when(i == 0)
        def _():
            for r in (dbias_r, gwo_r, gwso_r, gwco_r, gglu_w_r, gglu_b_r, gconv_r):
                r[...] = jnp.zeros_like(r)

        dr1_v = dr1_r[...]
        dmerged = _dot_t(dr1_v, wo_r[...])
        sa = _sigmoid(ga_r[...])
        sb = _sigmoid(gb_r[...])
        ya_v = ya_r[...]
        yb_v = yb_r[...]
        gwo_r[...] += _tdot(sa * ya_v + sb * yb_v, dr1_v)
        dya = dmerged * sa
        dyb = dmerged * sb
        dga = dmerged * ya_v * (sa * (1.0 - sa))
        dgb = dmerged * yb_v * (sb * (1.0 - sb))

        g, gelu_grad = _gelu_parts(ya0_r[...])
        s1 = _sigmoid(_dot(g, glu_w_r[...]) + glu_b_r[...])
        ya1 = g * s1
        dya1 = jnp.zeros((tm, SSM_W), F32)
        for j in range(N_SHARDS):
            dya_j = dya[:, 256 * j:256 * (j + 1)]
            gwso_r[j] += _tdot(ya1, dya_j)
            dya1 = dya1 + _dot_t(dya_j, wso_r[j])
        dz1 = dya1 * g * (s1 * (1.0 - s1))
        gglu_b_r[...] += jnp.sum(dz1, axis=0, keepdims=True)
        gglu_w_r[...] += _tdot(g, dz1)
        dya0_r[...] = (dya1 * s1 + _dot_t(dz1, glu_w_r[...])) * gelu_grad

        cw = cw_r[...]
        h = h_r[...]
        cg = cg_r[...]
        bg = bg_r[...]
        q = cg * h
        q_prev = jnp.where(i > 0, cgp_r[...] * hp_r[...], 0.0)
        q1 = _shift_down(q, q_prev, 1)
        q2 = _shift_down(q, q_prev, 2)
        z = cw[2:3, :] * q + cw[1:2, :] * q1 + cw[0:1, :] * q2
        yb0 = bg * z
        dyb0 = jnp.zeros((tm, SSM_W), F32)
        for j in range(N_SHARDS):
            dyb_j = dyb[:, 256 * j:256 * (j + 1)]
            gwco_r[j] += _tdot(yb0, dyb_j)
            dyb0 = dyb0 + _dot_t(dyb_j, wco_r[j])
        dbg = dyb0 * z
        dz = dyb0 * bg
        dyb_n = _dot_t(dr1n_r[...], wo_r[...]) * _sigmoid(gbn_r[...])
        dyb0_n = jnp.zeros((SUBLANES, SSM_W), F32)
        for j in range(N_SHARDS):
            dyb0_n = dyb0_n + _dot_t(dyb_n[:, 256 * j:256 * (j + 1)], wco_r[j])
        dz_next = jnp.where(i < pl.num_programs(0) - 1, dyb0_n * bgn_r[...], 0.0)
        dq = cw[2:3, :] * dz + cw[1:2, :] * _shift_up(dz, dz_next, 1) + cw[0:1, :] * _shift_up(dz, dz_next, 2)
        gconv_r[0:1, :] += jnp.sum(dz * q2, axis=0, keepdims=True)
        gconv_r[1:2, :] += jnp.sum(dz * q1, axis=0, keepdims=True)
        gconv_r[2:3, :] += jnp.sum(dz * q, axis=0, keepdims=True)
        dh = dq * cg
        dcg = dq * h

        dproj_r[:, 0:512] = jnp.zeros((tm, SSM_W), BF16)
        pieces = ((512, dh), (1024, dcg), (1536, dbg), (2048, dga), (3072, dgb))
        for off, val in pieces:
            w = val.shape[1]
            dproj_r[:, off:off + w] = val.astype(BF16)
            dbias_r[:, off:off + w] += jnp.sum(val, axis=0, keepdims=True)

    def col(w, c):
        return pl.BlockSpec((tm, w), lambda i: (i, c))

    def prev(c):
        return pl.BlockSpec((SUBLANES, SSM_W), lambda i: (jnp.maximum(i * hb - 1, 0), c))

    def nxt(w, c):
        return pl.BlockSpec((SUBLANES, w), lambda i: (jnp.minimum((i + 1) * hb, last_block), c))

    sh = jax.ShapeDtypeStruct
    return _launch(
        body, comms, name="mixer_backward", grid=(t // tm,),
        out_shape=(sh((t, SSM_W), F32), sh((t, IN_COLS), BF16), sh((1, IN_COLS), F32),
                   sh((D_MODEL, D_MODEL), F32), sh((N_SHARDS, SSM_W, 256), F32), sh((N_SHARDS, SSM_W, 256), F32),
                   sh((SSM_W, SSM_W), F32), sh((1, SSM_W), F32), sh((SUBLANES, SSM_W), F32)),
        in_specs=[col(D_MODEL, 0), nxt(D_MODEL, 0), col(SSM_W, 0), col(D_MODEL, 0), col(D_MODEL, 0),
                  col(SSM_W, 1), col(SSM_W, 2), col(SSM_W, 3), col(D_MODEL, 2), col(D_MODEL, 3),
                  prev(1), prev(2), nxt(SSM_W, 3), nxt(D_MODEL, 3),
                  _const((SSM_W, SSM_W)), _const((1, SSM_W)), _const((N_SHARDS, SSM_W, 256)),
                  _const((SUBLANES, SSM_W)), _const((N_SHARDS, SSM_W, 256)), _const((D_MODEL, D_MODEL))],
        out_specs=(col(SSM_W, 0), col(IN_COLS, 0), _const((1, IN_COLS)),
                   _const((D_MODEL, D_MODEL)), _const((N_SHARDS, SSM_W, 256)), _const((N_SHARDS, SSM_W, 256)),
                   _const((SSM_W, SSM_W)), _const((1, SSM_W)), _const((SUBLANES, SSM_W))),
        sem=("arbitrary",),
    )(dr1, dr1, ya0, ya, yb, proj, proj, proj, proj, proj, proj, proj, proj, proj,
      glu_w, glu_b, wso_st, conv_w8, wco_st, w_o)


def _cmulc_add(xr, xi, mr, mi, sr, si):
    return xr + (mr * sr + mi * si), xi + (mr * si - mi * sr)


def _ssm_backward(dya0, proj, xsr, xsi, bbr, bbi, ctr, cti, d_skip, tab_a, tab_p, dproj, comms=(), tc=SCAN_CHUNK):
    t = proj.shape[0]
    nk = t // tc

    def body(dy_r, u_r, xsr_r, xsi_r, bbr_r, bbi_r, ctr_r, cti_r, d_r, ta_r, tp_r, dproj_any,
             du_r, dus_r, gbbr_r, gbbi_r, gctr_r, gcti_r, glbr_r, glbi_r, gd_r,
             gr_s, gi_s, dyi_s, ui_s, dui_s, dun_s, car_r, car_i):
        del dproj_any

        @pl.when(pl.program_id(1) == 0)
        def _():
            for r in (car_r, car_i, dus_r, gbbr_r, gbbi_r, gctr_r, gcti_r, glbr_r, glbi_r, gd_r):
                r[...] = jnp.zeros_like(r)

        _interleave(dy_r, dyi_s)
        _interleave(u_r, ui_s)
        dy = dyi_s[...]
        u = ui_s[...]
        gr_s[...] = _dot(dy, ctr_r[...])
        gi_s[...] = -_dot(dy, cti_r[...])
        a_r, a_i = ta_r[0], ta_r[1]

        def local(n, carry):
            rows = _step_rows(SCAN_STEPS - 1 - n)
            gr, gi = _cmulc_add(gr_s[rows, :], gi_s[rows, :], a_r, a_i, *carry)
            gr_s[rows, :] = gr
            gi_s[rows, :] = gi
            return gr, gi

        zero = jnp.zeros((SUBLANES, STRIP), F32)
        ends_r, ends_i = lax.fori_loop(0, SCAN_STEPS, local, (zero, zero), unroll=2)
        ent_r, ent_i, out_r, out_i = _segment_states(
            car_r[...], car_i[...], ends_r, ends_i, ta_r[2, 0:1, :], -ta_r[3, 0:1, :], range(SUBLANES - 1, -1, -1))
        car_r[...] = out_r
        car_i[...] = out_i

        def entering(n, carry):
            gnr, gni, ar, ai = carry
            rows = _step_rows(SCAN_STEPS - 1 - n)
            power = _step_rows(n)
            gr, gi = _cmulc_add(gr_s[rows, :], gi_s[rows, :], tp_r[0, power, :], tp_r[1, power, :], ent_r, ent_i)
            gr_s[rows, :] = gr
            gi_s[rows, :] = gi
            xr = xsr_r[rows, :]
            xi = xsi_r[rows, :]
            return gr, gi, ar + (xr * gnr + xi * gni), ai + (xr * gni - xi * gnr)

        _, _, ar, ai = lax.fori_loop(0, SCAN_STEPS, entering, (ent_r, ent_i, zero, zero), unroll=2)
        glbr_r[...] += ar
        glbi_r[...] += ai
        gr = gr_s[...]
        gi = gi_s[...]
        dui_s[...] = _dot_t(gr, bbr_r[...]) + _dot_t(gi, bbi_r[...]) + d_r[...] * dy
        _deinterleave(dui_s, dun_s)
        du = dun_s[...]
        du_r[...] = du.astype(BF16)
        dus_r[...] += jnp.sum(du, axis=0, keepdims=True)
        gd_r[...] += jnp.sum(dy * u, axis=0, keepdims=True)
        gbbr_r[...] += _tdot(u, gr)
        gbbi_r[...] += _tdot(u, gi)
        gctr_r[...] += _tdot(dy, xsr_r[...])
        gcti_r[...] -= _tdot(dy, xsi_r[...])

    def rev(w):
        return pl.BlockSpec((tc, w), lambda j, k: (nk - 1 - k, j))

    strip_mat = pl.BlockSpec((128, STRIP), lambda j, k: (j, 0))
    vec = pl.BlockSpec((1, 128), lambda j, k: (0, j))
    lbacc = pl.BlockSpec((SUBLANES, STRIP), lambda j, k: (0, j))
    sh = jax.ShapeDtypeStruct
    return _launch(
        body, comms, name="ssm_backward", grid=(N_STRIPS, nk),
        out_shape=(sh((t, IN_COLS), BF16), sh((1, SSM_W), F32),
                   sh((SSM_W, STRIP), F32), sh((SSM_W, STRIP), F32), sh((SSM_W, STRIP), F32), sh((SSM_W, STRIP), F32),
                   sh((SUBLANES, STATE_COLS), F32), sh((SUBLANES, STATE_COLS), F32), sh((1, SSM_W), F32)),
        in_specs=[rev(128), rev(128), rev(STRIP), rev(STRIP),
                  strip_mat, strip_mat, strip_mat, strip_mat, vec,
                  pl.BlockSpec((4, SUBLANES, STRIP), lambda j, k: (0, 0, j)),
                  pl.BlockSpec((2, tc, STRIP), lambda j, k: (0, 0, j)), ANY],
        out_specs=(rev(128), vec, strip_mat, strip_mat, strip_mat, strip_mat, lbacc, lbacc, vec),
        scratch_shapes=[pltpu.VMEM((tc, STRIP), F32), pltpu.VMEM((tc, STRIP), F32)]
        + [pltpu.VMEM((tc, 128), F32)] * 4 + [pltpu.VMEM((1, STRIP), F32)] * 2,
        aliases={11: 0}, sem=("parallel", "arbitrary"),
    )(dya0, proj, xsr, xsi, bbr, bbi, ctr, cti, d_skip, tab_a, tab_p, dproj)


def _input_grad(dr1, dproj, w_in_st, comms=(), tm=512):
    t = dr1.shape[0]

    def body(dr1_r, dp_r, w_r, dx_r):
        acc = ALPHA * dr1_r[...]
        for j in range(N_SHARDS):
            acc = acc + lax.dot_general(dp_r[:, D_MODEL * j:D_MODEL * (j + 1)], w_r[j],
                                        (((1,), (1,)), ((), ())), preferred_element_type=F32)
        dx_r[...] = acc

    (dx,), sent = _launch(
        body, comms, name="input_grad", grid=(t // tm,),
        out_shape=(jax.ShapeDtypeStruct((t, D_MODEL), F32),),
        in_specs=[pl.BlockSpec((tm, D_MODEL), lambda i: (i, 0)), pl.BlockSpec((tm, IN_COLS), lambda i: (i, 0)),
                  _const((N_SHARDS, D_MODEL, D_MODEL))],
        out_specs=(pl.BlockSpec((tm, D_MODEL), lambda i: (i, 0)),),
        sem=("parallel",),
    )(dr1, dproj, w_in_st)
    return dx, sent


def _in_weight_grad(x, dproj, comms=(), tk=1024):
    t = x.shape[0]

    def body(x_r, dp_r, gw_r):
        @pl.when(pl.program_id(1) == 0)
        def _():
            gw_r[...] = jnp.zeros_like(gw_r)

        gw_r[0] += _tdot(x_r[...], dp_r[...])

    (g_w_in,), sent = _launch(
        body, comms, name="in_weight_grad", grid=(N_SHARDS, t // tk),
        out_shape=(jax.ShapeDtypeStruct((N_SHARDS, D_MODEL, D_MODEL), F32),),
        in_specs=[pl.BlockSpec((tk, D_MODEL), lambda j, k: (k, 0)), pl.BlockSpec((tk, D_MODEL), lambda j, k: (k, j))],
        out_specs=(pl.BlockSpec((1, D_MODEL, D_MODEL), lambda j, k: (j, 0, 0)),),
        sem=("parallel", "arbitrary"),
    )(x, dproj)
    return g_w_in, sent


MIXER_W = ("glu_w", "w_ssm_out", "w_conv_out", "w_o")
FFN_W = ("w_gate", "w_up", "w_down")


def _device_step(x, target, small, shards, c_arr, me_arr):
    lr, li = small["ssm_lambda_re"][0], small["ssm_lambda_im"][0]
    ldt = small["ssm_log_dt"][0][:, None]
    rep16 = lambda a: jnp.broadcast_to(a[:, None, :], (N_GROUPS, GROUP_C, a.shape[-1])).reshape(SSM_W, a.shape[-1])
    lr16, li16 = rep16(lr), rep16(li)
    ldt16 = rep16(jnp.broadcast_to(ldt, (N_GROUPS, N_STATE)))
    brt = small["ssm_b_re"][0].transpose(0, 2, 1).reshape(SSM_W, N_STATE)
    bit = small["ssm_b_im"][0].transpose(0, 2, 1).reshape(SSM_W, N_STATE)
    cre = small["ssm_c_re"][0].reshape(SSM_W, N_STATE)
    cim = small["ssm_c_im"][0].reshape(SSM_W, N_STATE)
    disc = (lr, li, ldt, lr16, li16, ldt16, brt, bit)

    pwr, pwi, bbr, bbi, ctr, cti = _ssm_prepare(*disc, cre, cim)
    tab_a, tab_p = _scan_tables(pwr, pwi)

    first_sh = [shards[n] for n in MIXER_W + FFN_W[:1]]
    second_sh = [shards[n] for n in FFN_W[1:]]
    (w_in_st,) = _gather_weights([shards["w_in"]])
    proj, (arrived,) = _in_proj(x, w_in_st, small["b_in"], comms=[_gather_ici(first_sh, [shards["conv_w"]])])
    (xsr, xsi, ya0), (second_part, first_st) = _ssm_forward(
        proj, bbr, bbi, ctr, cti, small["ssm_d"], tab_a, tab_p,
        comms=[_gather_ici(second_sh), _gather_d2d(arrived[:len(first_sh)], first_sh)])
    glu_st, wso_st, wco_st, wo_st, wg_st = (_own_slot(st, sh) for st, sh in zip(first_st, first_sh))
    conv_st = _own_slot(arrived[len(first_sh)], shards["conv_w"])
    conv_w8 = jnp.pad(conv_st[:, :3, :].transpose(1, 0, 2).reshape(3, SSM_W), ((0, SUBLANES - 3), (0, 0)))
    w_o = wo_st.reshape(D_MODEL, D_MODEL)
    glu_w = glu_st.reshape(SSM_W, SSM_W)
    (xhat1, rstd1, ya, yb), (second_st,) = _mixer_forward(
        x, proj, ya0, glu_w, small["glu_b"], wso_st, conv_w8, wco_st, w_o, comms=[_gather_d2d(second_part, second_sh)])
    wu_st, wd_st = (_own_slot(st, sh) for st, sh in zip(second_st, second_sh))
    (loss, dr1, x1b, dr2b, hid, dhg, dhu, g_ln2_g, g_ln2_b, g_ln1_g, g_ln1_b) = _ffn_step(
        xhat1, rstd1, target, small["ln1_g"], small["ln1_b"], small["ln2_g"], small["ln2_b"], wg_st, wu_st, wd_st)

    add_halves = lambda gs, rs: _per_shape(lambda a, b: _add_own_half(a, b, c_arr), list(gs), list(rs))
    sum_chips = lambda owns, slots: _per_shape(lambda a, b: _sum_chips(a, b, me_arr), list(owns), list(slots))
    g_ffn = _ffn_weight_grads(x1b, dr2b, hid, dhg, dhu)
    (dya0, dproj, dbias, g_wo, g_wso, g_wco, g_glu_w, g_glu_b, g_conv8), (got_ffn,) = _mixer_backward(
        dr1, proj, ya0, ya, yb, glu_w, small["glu_b"], wso_st, conv_w8, wco_st, w_o, comms=[_swap_comm(g_ffn)])
    chip_ffn = add_halves(g_ffn, got_ffn)
    g_mix = [g_glu_w.reshape(N_SHARDS, 128, SSM_W), g_wso, g_wco, g_wo.reshape(N_SHARDS, 256, D_MODEL)]
    (dproj, dus, gbbr, gbbi, gctr, gcti, glbr, glbi, g_d), (slots_ffn, got_mix) = _ssm_backward(
        dya0, proj, xsr, xsi, bbr, bbi, ctr, cti, small["ssm_d"], tab_a, tab_p, dproj,
        comms=[_scatter_comm(chip_ffn), _swap_comm(g_mix)])
    halves_ffn = sum_chips(chip_ffn, slots_ffn)
    chip_mix = add_halves(g_mix, got_mix)
    g_lr, g_li, g_ldt, g_brt, g_bit, g_cre, g_cim = _ssm_param_grads(
        *disc, glbr.reshape(SUBLANES, N_GROUPS, N_STATE), glbi.reshape(SUBLANES, N_GROUPS, N_STATE),
        gbbr, gbbi, gctr, gcti)
    g_w_in, (others_ffn, slots_mix) = _in_weight_grad(
        x, dproj, comms=[_send_comm(halves_ffn), _scatter_comm(chip_mix)])
    halves_mix = sum_chips(chip_mix, slots_mix)
    dx, _ = _input_grad(dr1, dproj, w_in_st)

    g_conv = jnp.pad(g_conv8[:3].reshape(3, N_SHARDS, 128).transpose(1, 0, 2), ((0, 0), (0, SUBLANES - 3), (0, 0)))
    pieces = [dus, dbias[:, SSM_W:], g_lr, g_li, g_ldt, g_brt, g_bit, g_cre, g_cim, g_d, g_glu_b,
              g_ln1_g, g_ln1_b, g_ln2_g, g_ln2_b, loss]
    flat = jnp.concatenate([p.reshape(-1) for p in pieces])
    g_packed = jnp.pad(flat, (0, PACKED_ROWS * 128 - flat.shape[0])).reshape(PACKED_ROWS, 128)
    ((got_w, got_conv, got_packed),) = _standalone([_swap_comm([g_w_in], [g_conv, g_packed])], "swap_with_sibling")
    (chip_w,) = add_halves([g_w_in], [got_w])
    chip_conv, chip_packed = _small_pair_sums([g_conv, g_packed], [got_conv, got_packed])
    ((slots_w, slots_conv, slots_packed),) = _standalone(
        [_scatter_comm([chip_w, chip_conv], [chip_packed])], "scatter_to_chips")
    (halves_w,) = sum_chips([chip_w], [slots_w])
    conv_total, packed_total = _small_totals(me_arr, chip_conv, slots_conv, chip_packed, slots_packed)
    (others_rest,) = _standalone([_send_comm([halves_w] + halves_mix)], "send_to_sibling")

    pairs = dict(zip(FFN_W, zip(halves_ffn, others_ffn)))
    pairs.update(zip(("w_in",) + MIXER_W, zip([halves_w] + halves_mix, others_rest)))
    return dx, pairs, conv_total, packed_total


PACKED_ROWS = 1136
PACKED_LAYOUT = (("b_in", IN_COLS), ("ssm_lambda_re", STATE_COLS), ("ssm_lambda_im", STATE_COLS),
                 ("ssm_log_dt", N_GROUPS), ("ssm_b_re", SSM_W * N_STATE), ("ssm_b_im", SSM_W * N_STATE),
                 ("ssm_c_re", SSM_W * N_STATE), ("ssm_c_im", SSM_W * N_STATE), ("ssm_d", SSM_W), ("glu_b", SSM_W),
                 ("ln1_g", D_MODEL), ("ln1_b", D_MODEL), ("ln2_g", D_MODEL), ("ln2_b", D_MODEL), ("loss", 1))


def _unpack_small(packed):
    flat = packed.reshape(-1)
    out, off = {}, 0
    for name, size in PACKED_LAYOUT:
        out[name] = flat[off:off + size]
        off += size
    for name in ("ssm_b_re", "ssm_b_im"):
        out[name] = out[name].reshape(N_GROUPS, GROUP_C, N_STATE).transpose(0, 2, 1)[None]
    for name in ("ssm_c_re", "ssm_c_im"):
        out[name] = out[name].reshape(1, N_GROUPS, GROUP_C, N_STATE)
    for name in ("ssm_lambda_re", "ssm_lambda_im"):
        out[name] = out[name].reshape(1, N_GROUPS, N_STATE)
    for name in ("b_in", "ssm_log_dt", "ssm_d", "glu_b", "ln1_g", "ln1_b", "ln2_g", "ln2_b"):
        out[name] = out[name][None]
    return out


BIG = ("w_in", "glu_w", "w_ssm_out", "w_conv_out", "w_o", "w_gate", "w_up", "w_down")
SMALL = ("b_in", "ssm_lambda_re", "ssm_lambda_im", "ssm_log_dt", "ssm_b_re", "ssm_b_im", "ssm_c_re", "ssm_c_im",
         "ssm_d", "glu_b", "ln1_g", "ln1_b", "ln2_g", "ln2_b")
WEIGHTS = ("w_in", "b_in", "ssm_lambda_re", "ssm_lambda_im", "ssm_log_dt", "ssm_b_re", "ssm_b_im", "ssm_c_re",
           "ssm_c_im", "ssm_d", "glu_w", "glu_b", "w_ssm_out", "conv_w", "w_conv_out", "w_o", "ln1_g", "ln1_b",
           "w_gate", "w_up", "w_down", "ln2_g", "ln2_b")


def _place():
    x, y, c = lax.axis_index("x"), lax.axis_index("y"), lax.axis_index("c")
    chips = [(1 - x, y), (x, 1 - y), (1 - x, 1 - y)]
    return x, y, c, chips


def _shard_of(chip):
    return 2 * chip[0] + chip[1]


def _remote(src, dst, send_sem, recv_sem, to):
    return pltpu.make_async_remote_copy(src_ref=src, dst_ref=dst, send_sem=send_sem, recv_sem=recv_sem,
                                        device_id=to, device_id_type=MESH)


def _half_rows(shard, which):
    r2 = shard.shape[0] // 2
    return pl.ds(pl.multiple_of(which * r2, 16), r2)


def _own_slot(stack, shard):
    me = _shard_of((lax.axis_index("x"), lax.axis_index("y")))
    return lax.dynamic_update_slice(stack, shard[None], (me,) + (0,) * shard.ndim)


def _gather_ici(halved, whole=()):
    shards = list(halved) + list(whole)
    nh = len(halved)

    def copies(src, dst, sems):
        send_sem, recv_sem = sems
        x, y, c, chips = _place()
        me = _shard_of((x, y))
        out = []
        for a in range(len(shards)):
            for k, chip in enumerate(chips):
                if a < nh:
                    rows = _half_rows(shards[a], c)
                    out.append(_remote(src[a].at[rows], dst[a].at[me, rows], send_sem.at[a, k], recv_sem.at[a, k],
                                       (*chip, c)))
                else:
                    out.append(_remote(src[a], dst[a].at[me], send_sem.at[a, k], recv_sem.at[a, k], (*chip, c)))
        return out

    n = len(shards)
    return _Comm(shards, [jax.ShapeDtypeStruct((N_SHARDS,) + s.shape, s.dtype) for s in shards],
                 [pltpu.SemaphoreType.DMA((n, 3))] * 2, copies)


def _gather_d2d(stacks, shards):
    def copies(src, dst, sems):
        del src
        send_sem, recv_sem = sems
        x, y, c, chips = _place()
        out = []
        for a in range(len(stacks)):
            for k, chip in enumerate(chips):
                rows = dst[a].at[_shard_of(chip), _half_rows(shards[a], c)]
                out.append(_remote(rows, rows, send_sem.at[a, k], recv_sem.at[a, k], (x, y, 1 - c)))
        return out

    n = len(stacks)
    return _Comm(stacks, [jax.ShapeDtypeStruct(s.shape, s.dtype) for s in stacks],
                 [pltpu.SemaphoreType.DMA((n, 3))] * 2, copies, aliased=True)


def _standalone(comms, name):
    return _launch(None, comms, name=name, grid=(), in_specs=[], out_specs=(), out_shape=())()[1]


def _gather_weights(shards):
    n = len(shards)

    def body(*refs):
        src, dst = refs[:n], refs[n:2 * n]
        send_sem, recv_sem, fsend_sem, frecv_sem = refs[2 * n:]
        x, y, c, chips = _place()
        me = _shard_of((x, y))
        sibling = (x, y, 1 - c)
        sends = []
        for a in range(n):
            mine = _half_rows(shards[a], c)
            for k, chip in enumerate(chips):
                cp = _remote(src[a].at[mine], dst[a].at[me, mine], send_sem.at[a, k], recv_sem.at[a, k], (*chip, c))
                cp.start()
                sends.append(cp)
        for a in range(n):
            for k, chip in enumerate(chips):
                rows = dst[a].at[_shard_of(chip), _half_rows(shards[a], c)]
                _remote(rows, rows, send_sem.at[a, k], recv_sem.at[a, k], sibling).wait_recv()
                cp = _remote(rows, rows, fsend_sem.at[a, k], frecv_sem.at[a, k], sibling)
                cp.start()
                sends.append(cp)
        for a in range(n):
            for k, chip in enumerate(chips):
                rows = dst[a].at[_shard_of(chip), _half_rows(shards[a], 1 - c)]
                _remote(rows, rows, fsend_sem.at[a, k], frecv_sem.at[a, k], sibling).wait_recv()
        for cp in sends:
            cp.wait_send()

    stacks = pl.pallas_call(
        body, name="gather_weights",
        out_shape=tuple(jax.ShapeDtypeStruct((N_SHARDS,) + s.shape, s.dtype) for s in shards),
        in_specs=[ANY] * n, out_specs=(ANY,) * n,
        scratch_shapes=[pltpu.SemaphoreType.DMA((n, 3))] * 4,
    )(*shards)
    return [_own_slot(st, sh) for st, sh in zip(stacks, shards)]


def _swap_comm(big, small=()):
    nb, n = len(big), len(big) + len(small)
    arrays = list(big) + list(small)

    def copies(src, dst, sems):
        send_sem, recv_sem = sems
        x, y, c, _ = _place()
        out = []
        for a in range(n):
            if a < nb:
                r2 = arrays[a].shape[1] // 2
                part = src[a].at[:, pl.ds(pl.multiple_of((1 - c) * r2, SUBLANES), r2), :]
            else:
                part = src[a]
            out.append(_remote(part, dst[a], send_sem.at[a], recv_sem.at[a], (x, y, 1 - c)))
        return out

    out_shape = [jax.ShapeDtypeStruct((N_SHARDS, g.shape[1] // 2, g.shape[2]), g.dtype) for g in big]
    out_shape += [jax.ShapeDtypeStruct(g.shape, g.dtype) for g in small]
    return _Comm(arrays, out_shape, [pltpu.SemaphoreType.DMA((n,))] * 2, copies)


def _scatter_comm(slabbed, small=()):
    ns, n = len(slabbed), len(slabbed) + len(small)
    arrays = list(slabbed) + list(small)

    def copies(src, dst, sems):
        send_sem, recv_sem = sems
        _, _, c, chips = _place()
        out = []
        for a in range(n):
            for k, chip in enumerate(chips):
                part = src[a].at[_shard_of(chip)] if a < ns else src[a]
                out.append(_remote(part, dst[a].at[k], send_sem.at[a, k], recv_sem.at[a, k], (*chip, c)))
        return out

    out_shape = [jax.ShapeDtypeStruct((3,) + g.shape[1:], g.dtype) for g in slabbed]
    out_shape += [jax.ShapeDtypeStruct((3,) + g.shape, g.dtype) for g in small]
    return _Comm(arrays, out_shape, [pltpu.SemaphoreType.DMA((n, 3))] * 2, copies)


def _send_comm(arrays):
    n = len(arrays)

    def copies(src, dst, sems):
        send_sem, recv_sem = sems
        x, y, c, _ = _place()
        return [_remote(src[a], dst[a], send_sem.at[a], recv_sem.at[a], (x, y, 1 - c)) for a in range(n)]

    return _Comm(arrays, [jax.ShapeDtypeStruct(h.shape, h.dtype) for h in arrays],
                 [pltpu.SemaphoreType.DMA((n,))] * 2, copies)


def _row_chunk(rows):
    for cand in (256, 176, 128, 64):
        if rows % cand == 0:
            return cand
    return rows


def _per_shape(fn, *lists):
    groups = {}
    for i, items in enumerate(zip(*lists)):
        groups.setdefault(tuple(a.shape for a in items), []).append(i)
    out = [None] * len(lists[0])
    for idx in groups.values():
        for i, r in zip(idx, fn(*([lst[i] for i in idx] for lst in lists))):
            out[i] = r
    return out


def _add_own_half(stacks, receiveds, c):
    n = len(stacks)
    _, r2, cols = receiveds[0].shape

    def body(c_ref, *refs):
        del c_ref
        for a in range(n):
            refs[2 * n + a][...] = (refs[a][...] + refs[n + a][...]).astype(BF16)

    own = pl.BlockSpec((1, r2, cols), lambda s, c_ref: (s, c_ref[0], 0))
    got = pl.BlockSpec((1, r2, cols), lambda s, c_ref: (s, 0, 0))
    return pl.pallas_call(
        body, name="add_own_half",
        grid_spec=pltpu.PrefetchScalarGridSpec(
            num_scalar_prefetch=1, grid=(N_SHARDS,), in_specs=[own] * n + [got] * n, out_specs=(got,) * n),
        out_shape=(jax.ShapeDtypeStruct(receiveds[0].shape, BF16),) * n,
        compiler_params=_params(("parallel",)),
    )(c, *stacks, *receiveds)


def _chip_order_sum(me, own, s0, s1, s2):
    terms = []
    for s in range(N_SHARDS):
        d = jnp.bitwise_xor(me, s)
        terms.append(jnp.where(d == 0, own, jnp.where(d == 2, s0, jnp.where(d == 1, s1, s2))))
    return ((terms[0] + terms[1]) + terms[2]) + terms[3]


def _sum_chips(own_stacks, slots, me):
    n = len(slots)
    _, rows, cols = slots[0].shape
    rc = _row_chunk(rows)

    def body(me_ref, *refs):
        del me_ref
        for a in range(n):
            own_r, s_r = refs[a], refs[n + a]
            refs[2 * n + a][...] = (((own_r[0].astype(F32) + s_r[0].astype(F32)) + s_r[1].astype(F32))
                                    + s_r[2].astype(F32))

    own = pl.BlockSpec((1, rc, cols), lambda i, me_ref: (me_ref[0], i, 0))
    three = pl.BlockSpec((3, rc, cols), lambda i, me_ref: (0, i, 0))
    total = pl.BlockSpec((rc, cols), lambda i, me_ref: (i, 0))
    return pl.pallas_call(
        body, name="sum_chips",
        grid_spec=pltpu.PrefetchScalarGridSpec(
            num_scalar_prefetch=1, grid=(rows // rc,), in_specs=[own] * n + [three] * n, out_specs=(total,) * n),
        out_shape=(jax.ShapeDtypeStruct((rows, cols), F32),) * n,
        compiler_params=_params(("parallel",)),
    )(me, *own_stacks, *slots)


def _small_pair_sums(mine, theirs):
    n = len(mine)

    def body(*refs):
        for a in range(n):
            refs[2 * n + a][...] = refs[a][...] + refs[n + a][...]

    vm = pl.BlockSpec(memory_space=pltpu.VMEM)
    return pl.pallas_call(
        body, name="small_pair_sums", out_shape=tuple(jax.ShapeDtypeStruct(g.shape, g.dtype) for g in mine),
        in_specs=[vm] * (2 * n), out_specs=(vm,) * n,
        compiler_params=pltpu.CompilerParams(vmem_limit_bytes=VMEM_LIMIT),
    )(*mine, *theirs)


def _adam_math(w, g, m, v):
    m = ADAM_B1 * m + (1.0 - ADAM_B1) * g
    v = ADAM_B2 * v + (1.0 - ADAM_B2) * (g * g)
    m_hat = m / (1.0 - ADAM_B1 ** ADAM_STEP)
    v_hat = v / (1.0 - ADAM_B2 ** ADAM_STEP)
    delta = -ADAM_LR * (m_hat / (jnp.sqrt(v_hat) + ADAM_EPS) + ADAM_WD * w)
    return delta, m, v


def _small_totals(me, conv_stack, conv_slots, packed, packed_slots):
    def body(me_ref, cs_r, cslot_r, p_r, pslot_r, conv_r, tot_r):
        me_ = me_ref[0]
        conv_r[...] = _chip_order_sum(me_, cs_r[me_], cslot_r[0], cslot_r[1], cslot_r[2])
        tot_r[...] = _chip_order_sum(me_, p_r[...], pslot_r[0], pslot_r[1], pslot_r[2])

    vm = pl.BlockSpec(memory_space=pltpu.VMEM)
    return pl.pallas_call(
        body, name="small_totals",
        out_shape=(jax.ShapeDtypeStruct(conv_stack.shape[1:], F32), jax.ShapeDtypeStruct(packed.shape, F32)),
        in_specs=[pl.BlockSpec(memory_space=pltpu.SMEM)] + [vm] * 4, out_specs=(vm, vm),
    )(me, conv_stack, conv_slots, packed, packed_slots)


def _adam_small(gs, ws, ms, vs):
    n = len(gs)

    def body(*refs):
        for a in range(n):
            g_r, w_r, m_r, v_r = (refs[i * n + a] for i in range(4))
            d_r, nm_r, nv_r = (refs[(4 + i) * n + a] for i in range(3))
            d_r[...], nm_r[...], nv_r[...] = _adam_math(w_r[...], g_r[...], m_r[...], v_r[...])

    vm = pl.BlockSpec(memory_space=pltpu.VMEM)
    shapes = tuple(jax.ShapeDtypeStruct(w.shape, F32) for w in ws)
    out = pl.pallas_call(
        body, name="adam_small", out_shape=shapes * 3, in_specs=[vm] * (4 * n), out_specs=(vm,) * (3 * n),
        compiler_params=pltpu.CompilerParams(vmem_limit_bytes=VMEM_LIMIT),
    )(*gs, *ws, *ms, *vs)
    return out[:n], out[n:2 * n], out[2 * n:]


def _adam_big(ws, mines, others, ms, vs, c):
    n = len(ws)
    r2, cols = mines[0].shape
    rc = _row_chunk(r2)
    nch = r2 // rc

    def body(c_ref, *refs):
        mine_is_here = pl.program_id(0) == c_ref[0]
        for a in range(n):
            w_r, mine_r, other_r, m_r, v_r = (refs[i * n + a] for i in range(5))
            g_r, d_r, nm_r, nv_r = (refs[(5 + i) * n + a] for i in range(4))
            g = jnp.where(mine_is_here, mine_r[...], other_r[...])
            g_r[...] = g
            d_r[...], nm_r[...], nv_r[...] = _adam_math(w_r[...], g, m_r[...], v_r[...])

    full = pl.BlockSpec((rc, cols), lambda h, i, c_ref: (h * nch + i, 0))
    half = pl.BlockSpec((rc, cols), lambda h, i, c_ref: (i, 0))
    shape = jax.ShapeDtypeStruct((2 * r2, cols), F32)
    out = pl.pallas_call(
        body, name="adam_big",
        grid_spec=pltpu.PrefetchScalarGridSpec(
            num_scalar_prefetch=1, grid=(2, nch),
            in_specs=[full] * n + [half] * (2 * n) + [full] * (2 * n), out_specs=(full,) * (4 * n)),
        out_shape=(shape,) * (4 * n), compiler_params=_params(("parallel", "parallel")),
    )(c, *ws, *mines, *others, *ms, *vs)
    return [tuple(out[i * n + a] for i in range(4)) for a in range(n)]


def kernel(x, w_in, b_in, ssm_lambda_re, ssm_lambda_im, ssm_log_dt, ssm_b_re, ssm_b_im, ssm_c_re, ssm_c_im, ssm_d, glu_w, glu_b, w_ssm_out, conv_w, w_conv_out, w_o, ln1_g, ln1_b, w_gate, w_up, w_down, ln2_g, ln2_b, loss_target, m_w_in, m_b_in, m_ssm_lambda_re, m_ssm_lambda_im, m_ssm_log_dt, m_ssm_b_re, m_ssm_b_im, m_ssm_c_re, m_ssm_c_im, m_ssm_d, m_glu_w, m_glu_b, m_w_ssm_out, m_conv_w, m_w_conv_out, m_w_o, m_ln1_g, m_ln1_b, m_w_gate, m_w_up, m_w_down, m_ln2_g, m_ln2_b, v_w_in, v_b_in, v_ssm_lambda_re, v_ssm_lambda_im, v_ssm_log_dt, v_ssm_b_re, v_ssm_b_im, v_ssm_c_re, v_ssm_c_im, v_ssm_d, v_glu_w, v_glu_b, v_w_ssm_out, v_conv_w, v_w_conv_out, v_w_o, v_ln1_g, v_ln1_b, v_w_gate, v_w_up, v_w_down, v_ln2_g, v_ln2_b):
    given = dict(locals())
    w = {n: given[n] for n in WEIGHTS}
    m = {n: given["m_" + n] for n in WEIGHTS}
    v = {n: given["v_" + n] for n in WEIGHTS}

    flip = lambda n, a: a.T if n in ("w_gate", "w_up") else a
    shards = {n: flip(n, w[n][0]).astype(BF16) for n in BIG}
    shards["conv_w"] = jnp.pad(conv_w[0], ((0, SUBLANES - 3), (0, 0)))
    c_arr = jnp.reshape(lax.axis_index("c"), (1,)).astype(jnp.int32)
    me = _shard_of((lax.axis_index("x"), lax.axis_index("y")))
    me_arr = jnp.reshape(me, (1,)).astype(jnp.int32)
    dx, pairs, conv_total, packed_total = _device_step(
        x[0], loss_target[0], {n: w[n] for n in SMALL}, shards, c_arr, me_arr)

    grad = _unpack_small(packed_total)
    loss_total = grad.pop("loss")[0]
    grad["conv_w"] = conv_total[:3][None]
    small_names = ("conv_w",) + SMALL
    swap = lambda n, a: a.transpose(0, 1, 3, 2) if n in ("ssm_b_re", "ssm_b_im") else a
    ds, nms, nvs = _adam_small(*([swap(n, d[n]) for n in small_names] for d in (grad, w, m, v)))
    delta, new_m, new_v = {}, {}, {}
    for i, n in enumerate(small_names):
        delta[n], new_m[n], new_v[n] = swap(n, ds[i]), swap(n, nms[i]), swap(n, nvs[i])
    updated = _per_shape(
        lambda *a: _adam_big(*a, c_arr),
        [flip(n, w[n][0]) for n in BIG], [pairs[n][0] for n in BIG], [pairs[n][1] for n in BIG],
        [flip(n, m[n][0]) for n in BIG], [flip(n, v[n][0]) for n in BIG])
    for n, results in zip(BIG, updated):
        grad[n], delta[n], new_m[n], new_v[n] = (flip(n, r)[None] for r in results)

    return (loss_total, dx[None], *[grad[n] for n in WEIGHTS], *[delta[n] for n in WEIGHTS],
            *[new_m[n] for n in WEIGHTS], *[new_v[n] for n in WEIGHTS])
```

```python
import functools
import math

import jax
import jax.numpy as jnp
from jax import lax
from jax.experimental import pallas as pl
from jax.experimental.pallas import tpu as pltpu

F32 = jnp.float32
BF16 = jnp.bfloat16

D_MODEL = 1024
IN_COLS = 4096
SSM_W = 512
N_GROUPS = 32
N_STATE = 64
GROUP_C = 16
STATE_COLS = N_GROUPS * N_STATE
STRIP = 512
N_STRIPS = STATE_COLS // STRIP
FFN_SHARD = 704
N_SHARDS = 4
ALPHA = 2.0 ** 0.25
LN_EPS = 1e-5
GELU_K = math.sqrt(2.0 / math.pi)
GELU_C = 0.044715

ADAM_LR = 0.001
ADAM_B1 = 0.9
ADAM_B2 = 0.999
ADAM_EPS = 1e-08
ADAM_WD = 0.01
ADAM_STEP = 10

V7X_VMEM_BYTES = 64 * 1024 * 1024
VMEM_LIMIT = V7X_VMEM_BYTES - 8 * 1024 * 1024
SUBLANES = 8
N_POWERS = 128

MESH = pl.DeviceIdType.MESH
ANY = pl.BlockSpec(memory_space=pl.ANY)


def _dot(a, b):
    return jnp.dot(a.astype(BF16), b.astype(BF16), preferred_element_type=F32)


def _dot_t(a, b):
    return lax.dot_general(a.astype(BF16), b.astype(BF16), (((1,), (1,)), ((), ())),
                           preferred_element_type=F32)


def _tdot(a, b):
    return lax.dot_general(a.astype(BF16), b.astype(BF16), (((0,), (0,)), ((), ())),
                           preferred_element_type=F32)


def _sigmoid(v):
    return 1.0 / (1.0 + jnp.exp(-v))


def _split3(v):
    hi = v.astype(BF16)
    r1 = v - hi.astype(F32)
    mid = r1.astype(BF16)
    lo = (r1 - mid.astype(F32)).astype(BF16)
    return hi, mid, lo


def _exact_dot(v, sel):
    hi, mid, lo = _split3(v)
    return (jnp.dot(hi, sel, preferred_element_type=F32)
            + jnp.dot(mid, sel, preferred_element_type=F32)
            + jnp.dot(lo, sel, preferred_element_type=F32))


def _const(shape):
    nd = len(shape)
    return pl.BlockSpec(shape, lambda *_: (0,) * nd)


def _params(sem, vmem=VMEM_LIMIT):
    return pltpu.CompilerParams(dimension_semantics=sem, vmem_limit_bytes=vmem)


def _gelu_parts(v):
    inner = GELU_K * (v + GELU_C * v * v * v)
    t = jnp.tanh(inner)
    g = 0.5 * v * (1.0 + t)
    dg = 0.5 * (1.0 + t) + 0.5 * v * (1.0 - t * t) * GELU_K * (1.0 + 3.0 * GELU_C * v * v)
    return g, dg


class _Comm:
    def __init__(self, inputs, out_shape, sems, copies, aliased=False):
        self.inputs, self.out_shape, self.sems = list(inputs), tuple(out_shape), list(sems)
        self.copies, self.aliased = copies, aliased


def _launch(body, comms, *, name, grid, in_specs, out_specs, out_shape, scratch_shapes=(), aliases=None, sem=None):
    comms = list(comms)
    n_in, n_out, n_scr = len(in_specs), len(out_specs), len(scratch_shapes)
    aliases = dict(aliases or {})
    layout = []
    p_in, p_out, p_sem = n_in, n_out, 0
    for cm in comms:
        layout.append((p_in, p_out, p_sem))
        if cm.aliased:
            for i in range(len(cm.inputs)):
                aliases[p_in + i] = p_out + i
        p_in, p_out, p_sem = p_in + len(cm.inputs), p_out + len(cm.out_shape), p_sem + len(cm.sems)
    tot_in, tot_out = p_in, p_out

    def fused(*refs):
        ins, outs = refs[:tot_in], refs[tot_in:tot_in + tot_out]
        scr = refs[tot_in + tot_out:tot_in + tot_out + n_scr]
        sems = refs[tot_in + tot_out + n_scr:]

        def descriptors():
            out = []
            for cm, (a, b, s) in zip(comms, layout):
                out += cm.copies(ins[a:a + len(cm.inputs)], outs[b:b + len(cm.out_shape)], sems[s:s + len(cm.sems)])
            return out

        steps = [pl.program_id(d) for d in range(len(grid))]
        first = functools.reduce(jnp.logical_and, [s == 0 for s in steps]) if grid else None
        last = functools.reduce(jnp.logical_and, [s == g - 1 for s, g in zip(steps, grid)]) if grid else None

        def start():
            for cp in descriptors():
                cp.start()

        def finish():
            for cp in descriptors():
                cp.wait()

        if comms:
            pl.when(first)(start) if grid else start()
        if body is not None:
            body(*ins[:n_in], *outs[:n_out], *scr)
        if comms:
            pl.when(last)(finish) if grid else finish()

    specs_in = list(in_specs) + [ANY] * (tot_in - n_in)
    specs_out = tuple(out_specs) + (ANY,) * (tot_out - n_out)
    shapes = tuple(out_shape) + tuple(s for cm in comms for s in cm.out_shape)
    scratch = list(scratch_shapes) + [s for cm in comms for s in cm.sems]
    if comms or sem is None:
        sem = ("arbitrary",) * len(grid)
    kwargs = dict(grid=grid) if grid else {}
    call = pl.pallas_call(fused, name=name, out_shape=shapes, in_specs=specs_in, out_specs=specs_out,
                          scratch_shapes=scratch, input_output_aliases=aliases,
                          compiler_params=_params(sem) if grid else None, **kwargs)

    def run(*args):
        out = call(*args, *(a for cm in comms for a in cm.inputs))
        results, rest = out[:n_out], out[n_out:]
        per_comm = []
        for cm in comms:
            per_comm.append(rest[:len(cm.out_shape)])
            rest = rest[len(cm.out_shape):]
        return results, per_comm

    return run


def _ssm_discretise(lr, li, ldt, lr16, li16, ldt16, brt, bit):
    def lam_bar(lr_, li_, ldt_):
        dt = jnp.exp(ldt_)
        mag = jnp.exp(lr_ * dt)
        return mag * jnp.cos(li_ * dt), mag * jnp.sin(li_ * dt)

    lb_re, lb_im = lam_bar(lr, li, ldt)
    l16_re, l16_im = lam_bar(lr16, li16, ldt16)
    den = lr16 * lr16 + li16 * li16
    num_re = l16_re - 1.0
    fr = (num_re * lr16 + l16_im * li16) / den
    fi = (l16_im * lr16 - num_re * li16) / den
    bb_re = fr * brt - fi * bit
    bb_im = fr * bit + fi * brt
    return lb_re, lb_im, bb_re, bb_im


def _strip_selectors():
    p = lax.broadcasted_iota(jnp.int32, (N_STATE, STRIP), 0)
    col = lax.broadcasted_iota(jnp.int32, (N_STATE, STRIP), 1)
    rep = ((col & (N_STATE - 1)) == p).astype(BF16)
    row = lax.broadcasted_iota(jnp.int32, (SSM_W, STRIP), 0)
    col2 = lax.broadcasted_iota(jnp.int32, (SSM_W, STRIP), 1)
    mask = (((row >> 4) & 7) == (col2 >> 6))
    return rep, mask


def _ssm_prepare(lr, li, ldt, lr16, li16, ldt16, brt, bit, cre, cim):
    def body(lr_r, li_r, ldt_r, lr16_r, li16_r, ldt16_r, brt_r, bit_r, cre_r, cim_r,
             pwr_r, pwi_r, bbr_r, bbi_r, ctr_r, cti_r):
        lb_re, lb_im, bb_re, bb_im = _ssm_discretise(
            lr_r[...], li_r[...], ldt_r[...], lr16_r[...], li16_r[...], ldt16_r[...], brt_r[...], bit_r[...])
        pr, pi_ = lb_re, lb_im
        pwr_r[0] = pr
        pwi_r[0] = pi_
        for k in range(1, N_POWERS):
            pr, pi_ = pr * lb_re - pi_ * lb_im, pr * lb_im + pi_ * lb_re
            pwr_r[k] = pr
            pwi_r[k] = pi_
        rep, mask = _strip_selectors()
        for src, dst in ((bb_re, bbr_r), (bb_im, bbi_r), (cre_r[...], ctr_r), (cim_r[...], cti_r)):
            wide = jnp.dot(src.astype(BF16), rep, preferred_element_type=F32)
            dst[...] = jnp.where(mask, wide, 0.0).astype(BF16)

    vm = pl.BlockSpec(memory_space=pltpu.VMEM)
    return pl.pallas_call(
        body, name="ssm_prepare",
        out_shape=(jax.ShapeDtypeStruct((N_POWERS, N_GROUPS, N_STATE), F32),) * 2
        + (jax.ShapeDtypeStruct((SSM_W, STRIP), BF16),) * 4,
        in_specs=[vm] * 10, out_specs=(vm,) * 6,
    )(lr, li, ldt, lr16, li16, ldt16, brt, bit, cre, cim)


def _scan_tables(pwr, pwi):
    pr = pwr.reshape(N_POWERS, STATE_COLS)
    pi_ = pwi.reshape(N_POWERS, STATE_COLS)
    rows8 = lambda v: jnp.broadcast_to(v[None], (SUBLANES, STATE_COLS))
    tab_a = jnp.stack([rows8(pr[0]), rows8(pi_[0]), rows8(pr[-1]), rows8(pi_[-1])])
    tab_p = jnp.stack([jnp.repeat(pr, SUBLANES, axis=0), jnp.repeat(pi_, SUBLANES, axis=0)])
    return tab_a, tab_p


def _ssm_param_grads(lr, li, ldt, lr16, li16, ldt16, brt, bit, dlbr, dlbi, dbbr, dbbi, dctr, dcti):
    def body(lr_r, li_r, ldt_r, lr16_r, li16_r, ldt16_r, brt_r, bit_r,
             dlbr_r, dlbi_r, dbbr_r, dbbi_r, dctr_r, dcti_r,
             glr_r, gli_r, gldt_r, gbrt_r, gbit_r, gcre_r, gcim_r):
        rep, mask = _strip_selectors()

        def fold(acc):
            return sum(lax.dot_general(t, rep, (((1,), (1,)), ((), ())), preferred_element_type=F32)
                       for t in _split3(jnp.where(mask, acc, 0.0)))

        g_lb_re = jnp.sum(dlbr_r[...], axis=0)
        g_lb_im = jnp.sum(dlbi_r[...], axis=0)
        g_bb_re = fold(dbbr_r[...])
        g_bb_im = fold(dbbi_r[...])
        gcre_r[...] = fold(dctr_r[...])
        gcim_r[...] = fold(dcti_r[...])
        prim = (lr_r[...], li_r[...], ldt_r[...], lr16_r[...], li16_r[...], ldt16_r[...], brt_r[...], bit_r[...])
        _, vjp = jax.vjp(_ssm_discretise, *prim)
        g_lr, g_li, g_ldt, g_lr16, g_li16, g_ldt16, g_brt, g_bit = vjp((g_lb_re, g_lb_im, g_bb_re, g_bb_im))
        grp = lax.broadcasted_iota(jnp.int32, (N_GROUPS, SSM_W), 0)
        rw = lax.broadcasted_iota(jnp.int32, (N_GROUPS, SSM_W), 1)
        gsum = ((rw >> 4) == grp).astype(BF16)

        def group_sum(v):
            return sum(jnp.dot(gsum, t, preferred_element_type=F32) for t in _split3(v))

        glr_r[...] = g_lr + group_sum(g_lr16)
        gli_r[...] = g_li + group_sum(g_li16)
        gldt_r[...] = g_ldt + jnp.sum(group_sum(g_ldt16), axis=1, keepdims=True)
        gbrt_r[...] = g_brt
        gbit_r[...] = g_bit

    vm = pl.BlockSpec(memory_space=pltpu.VMEM)
    gp = jax.ShapeDtypeStruct((N_GROUPS, N_STATE), F32)
    gb = jax.ShapeDtypeStruct((SSM_W, N_STATE), F32)
    return pl.pallas_call(
        body, name="ssm_param_grads",
        out_shape=(gp, gp, jax.ShapeDtypeStruct((N_GROUPS, 1), F32), gb, gb, gb, gb),
        in_specs=[vm] * 14, out_specs=(vm,) * 7,
    )(lr, li, ldt, lr16, li16, ldt16, brt, bit, dlbr, dlbi, dbbr, dbbi, dctr, dcti)


def _in_proj(x, w_in_st, b_in, comms=()):
    t = x.shape[0]
    tm = 512

    def body(x_r, w_r, b_r, o_r):
        xb = x_r[...].astype(BF16)
        for j in range(N_SHARDS):
            cols = slice(D_MODEL * j, D_MODEL * (j + 1))
            o_r[:, cols] = jnp.dot(xb, w_r[j], preferred_element_type=F32) + b_r[:, cols]

    (proj,), sent = _launch(
        body, comms, name="in_proj", grid=(t // tm,),
        out_shape=(jax.ShapeDtypeStruct((t, IN_COLS), F32),),
        in_specs=[pl.BlockSpec((tm, D_MODEL), lambda i: (i, 0)), _const((N_SHARDS, D_MODEL, D_MODEL)),
                  _const((1, IN_COLS))],
        out_specs=(pl.BlockSpec((tm, IN_COLS), lambda i: (i, 0)),),
        sem=("parallel",),
    )(x, w_in_st, b_in)
    return proj, sent


def _cmul_add(xr, xi, mr, mi, sr, si):
    return xr + (mr * sr - mi * si), xi + (mr * si + mi * sr)


SCAN_STEPS = N_POWERS
SCAN_CHUNK = SUBLANES * SCAN_STEPS


def _interleave(src_r, dst_r):
    for step in range(SCAN_STEPS):
        dst_r[SUBLANES * step:SUBLANES * (step + 1), :] = src_r[pl.ds(step, SUBLANES, stride=SCAN_STEPS), :]


def _deinterleave(src_r, dst_r):
    for step in range(SCAN_STEPS):
        dst_r[pl.ds(step, SUBLANES, stride=SCAN_STEPS), :] = src_r[SUBLANES * step:SUBLANES * (step + 1), :]


def _step_rows(step):
    return pl.ds(pl.multiple_of(step * SUBLANES, SUBLANES), SUBLANES)


def _segment_states(first_r, first_i, ends_r, ends_i, a64_r, a64_i, order):
    row = lax.broadcasted_iota(jnp.int32, ends_r.shape, 0)
    cur_r, cur_i = first_r, first_i
    ent_r = jnp.zeros_like(ends_r)
    ent_i = jnp.zeros_like(ends_i)
    for s in order:
        ent_r = jnp.where(row == s, jnp.broadcast_to(cur_r, ends_r.shape), ent_r)
        ent_i = jnp.where(row == s, jnp.broadcast_to(cur_i, ends_i.shape), ent_i)
        cur_r, cur_i = _cmul_add(ends_r[s:s + 1, :], ends_i[s:s + 1, :], a64_r, a64_i, cur_r, cur_i)
    return ent_r, ent_i, cur_r, cur_i


def _ssm_forward(proj, bbr, bbi, ctr, cti, d_skip, tab_a, tab_p, comms=(), tc=SCAN_CHUNK):
    t = proj.shape[0]

    def body(u_r, bbr_r, bbi_r, ctr_r, cti_r, d_r, ta_r, tp_r, xsr_r, xsi_r, y_r, ui_s, yi_s, car_r, car_i):
        @pl.when(pl.program_id(1) == 0)
        def _():
            car_r[...] = jnp.zeros_like(car_r)
            car_i[...] = jnp.zeros_like(car_i)

        _interleave(u_r, ui_s)
        u = ui_s[...]
        xsr_r[...] = _dot(u, bbr_r[...])
        xsi_r[...] = _dot(u, bbi_r[...])
        a_r, a_i = ta_r[0], ta_r[1]

        def local(step, carry):
            rows = _step_rows(step)
            xr, xi = _cmul_add(xsr_r[rows, :], xsi_r[rows, :], a_r, a_i, *carry)
            xsr_r[rows, :] = xr
            xsi_r[rows, :] = xi
            return xr, xi

        zero = jnp.zeros((SUBLANES, STRIP), F32)
        ends_r, ends_i = lax.fori_loop(0, SCAN_STEPS, local, (zero, zero), unroll=2)
        ent_r, ent_i, out_r, out_i = _segment_states(
            car_r[...], car_i[...], ends_r, ends_i, ta_r[2, 0:1, :], ta_r[3, 0:1, :], range(SUBLANES))
        car_r[...] = out_r
        car_i[...] = out_i

        def entering(step, _):
            rows = _step_rows(step)
            xr, xi = _cmul_add(xsr_r[rows, :], xsi_r[rows, :], tp_r[0, rows, :], tp_r[1, rows, :], ent_r, ent_i)
            xsr_r[rows, :] = xr
            xsi_r[rows, :] = xi
            return 0

        lax.fori_loop(0, SCAN_STEPS, entering, 0, unroll=4)
        yi_s[...] = _dot_t(xsr_r[...], ctr_r[...]) - _dot_t(xsi_r[...], cti_r[...]) + d_r[...] * u
        _deinterleave(yi_s, y_r)

    strip_mat = pl.BlockSpec((128, STRIP), lambda j, k: (j, 0))
    states = pl.BlockSpec((tc, STRIP), lambda j, k: (k, j))
    return _launch(
        body, comms, name="ssm_forward", grid=(N_STRIPS, t // tc),
        out_shape=(jax.ShapeDtypeStruct((t, STATE_COLS), F32), jax.ShapeDtypeStruct((t, STATE_COLS), F32),
                   jax.ShapeDtypeStruct((t, SSM_W), F32)),
        in_specs=[pl.BlockSpec((tc, 128), lambda j, k: (k, j)),
                  strip_mat, strip_mat, strip_mat, strip_mat,
                  pl.BlockSpec((1, 128), lambda j, k: (0, j)),
                  pl.BlockSpec((4, SUBLANES, STRIP), lambda j, k: (0, 0, j)),
                  pl.BlockSpec((2, tc, STRIP), lambda j, k: (0, 0, j))],
        out_specs=(states, states, pl.BlockSpec((tc, 128), lambda j, k: (k, j))),
        scratch_shapes=[pltpu.VMEM((tc, 128), F32), pltpu.VMEM((tc, 128), F32),
                        pltpu.VMEM((1, STRIP), F32), pltpu.VMEM((1, STRIP), F32)],
        sem=("parallel", "arbitrary"),
    )(proj, bbr, bbi, ctr, cti, d_skip, tab_a, tab_p)


def _shift_down(v, prev, n):
    row = lax.broadcasted_iota(jnp.int32, v.shape, 0)
    out = pltpu.roll(v, n, 0)
    for r in range(n):
        src = prev[SUBLANES - n + r:SUBLANES - n + r + 1, :]
        out = jnp.where(row == r, jnp.broadcast_to(src, v.shape), out)
    return out


def _shift_up(v, nxt, n):
    rows = v.shape[0]
    row = lax.broadcasted_iota(jnp.int32, v.shape, 0)
    out = pltpu.roll(v, rows - n, 0)
    for r in range(n):
        src = nxt[r:r + 1, :]
        out = jnp.where(row == rows - n + r, jnp.broadcast_to(src, v.shape), out)
    return out


def _conv3(q, q_prev, w):
    return w[2:3, :] * q + w[1:2, :] * _shift_down(q, q_prev, 1) + w[0:1, :] * _shift_down(q, q_prev, 2)


def _mixer_forward(x, proj, ya0, glu_w, glu_b, wso_st, conv_w8, wco_st, w_o, comms=(), tm=256):
    t = x.shape[0]
    hb = tm // SUBLANES

    def body(x_r, ya0_r, h_r, cg_r, bg_r, ga_r, gb_r, hp_r, cgp_r,
             glu_w_r, glu_b_r, wso_r, cw_r, wco_r, wo_r, xh_r, rstd_r, ya_r, yb_r):
        i = pl.program_id(0)
        g, _ = _gelu_parts(ya0_r[...])
        ya1 = g * _sigmoid(_dot(g, glu_w_r[...]) + glu_b_r[...])
        q = cg_r[...] * h_r[...]
        q_prev = jnp.where(i > 0, cgp_r[...] * hp_r[...], 0.0)
        yb0 = bg_r[...] * _conv3(q, q_prev, cw_r[...])
        for j in range(N_SHARDS):
            ya_r[:, 256 * j:256 * (j + 1)] = _dot(ya1, wso_r[j])
            yb_r[:, 256 * j:256 * (j + 1)] = _dot(yb0, wco_r[j])
        merged = _sigmoid(ga_r[...]) * ya_r[...] + _sigmoid(gb_r[...]) * yb_r[...]
        r1 = ALPHA * x_r[...] + _dot(merged, wo_r[...])
        mu = jnp.mean(r1, axis=-1, keepdims=True)
        cen = r1 - mu
        rstd = lax.rsqrt(jnp.mean(cen * cen, axis=-1, keepdims=True) + LN_EPS)
        xh_r[...] = cen * rstd
        rstd_r[...] = rstd

    def col(w, c):
        return pl.BlockSpec((tm, w), lambda i: (i, c))

    def prev(c):
        return pl.BlockSpec((SUBLANES, SSM_W), lambda i: (jnp.maximum(i * hb - 1, 0), c))

    return _launch(
        body, comms, name="mixer_forward", grid=(t // tm,),
        out_shape=(jax.ShapeDtypeStruct((t, D_MODEL), F32), jax.ShapeDtypeStruct((t, 1), F32),
                   jax.ShapeDtypeStruct((t, D_MODEL), F32), jax.ShapeDtypeStruct((t, D_MODEL), F32)),
        in_specs=[col(D_MODEL, 0), col(SSM_W, 0), col(SSM_W, 1), col(SSM_W, 2), col(SSM_W, 3),
                  col(D_MODEL, 2), col(D_MODEL, 3), prev(1), prev(2),
                  _const((SSM_W, SSM_W)), _const((1, SSM_W)), _const((N_SHARDS, SSM_W, 256)),
                  _const((SUBLANES, SSM_W)), _const((N_SHARDS, SSM_W, 256)), _const((D_MODEL, D_MODEL))],
        out_specs=(col(D_MODEL, 0), pl.BlockSpec((tm, 1), lambda i: (i, 0)), col(D_MODEL, 0), col(D_MODEL, 0)),
        sem=("parallel",),
    )(x, ya0, proj, proj, proj, proj, proj, proj, proj, glu_w, glu_b, wso_st, conv_w8, wco_st, w_o)


def _layer_norm_bwd(dxhat, xhat, rstd):
    m1 = jnp.mean(dxhat, axis=-1, keepdims=True)
    m2 = jnp.mean(dxhat * xhat, axis=-1, keepdims=True)
    return rstd * (dxhat - m1 - xhat * m2)


def _ffn_step(xhat1, rstd1, target, ln1_g, ln1_b, ln2_g, ln2_b, wg_st, wu_st, wd_st, tm=256):
    t = xhat1.shape[0]

    def body(xh_r, rstd_r, tgt_r, g1_r, b1_r, g2_r, b2_r, wg_r, wu_r, wd_r,
             loss_r, dr1_r, x1b_r, dr2b_r, hid_r, dhg_r, dhu_r, dg2_r, db2_r, dg1_r, db1_r,
             hg_s, hu_s):
        @pl.when(pl.program_id(0) == 0)
        def _():
            for r in (loss_r, dg2_r, db2_r, dg1_r, db1_r):
                r[...] = jnp.zeros_like(r)

        xhat1_v = xh_r[...]
        x1 = xhat1_v * g1_r[...] + b1_r[...]
        x1b = x1.astype(BF16)
        x1b_r[...] = x1b
        ffn = jnp.zeros((tm, D_MODEL), F32)
        for j in range(N_SHARDS):
            hg = lax.dot_general(x1b, wg_r[j], (((1,), (1,)), ((), ())), preferred_element_type=F32)
            hu = lax.dot_general(x1b, wu_r[j], (((1,), (1,)), ((), ())), preferred_element_type=F32)
            hg_s[j] = hg
            hu_s[j] = hu
            hid = (hg * _sigmoid(hg) * hu).astype(BF16)
            hid_r[j] = hid
            ffn = ffn + jnp.dot(hid, wd_r[j], preferred_element_type=F32)
        r2 = ALPHA * x1 + ffn
        mu = jnp.mean(r2, axis=-1, keepdims=True)
        cen = r2 - mu
        rstd2 = lax.rsqrt(jnp.mean(cen * cen, axis=-1, keepdims=True) + LN_EPS)
        xhat2 = cen * rstd2
        diff = (xhat2 * g2_r[...] + b2_r[...]) - tgt_r[...]
        loss_r[...] += 0.5 * jnp.sum(jnp.mean(diff * diff, axis=-1, keepdims=True), axis=0, keepdims=True)
        dy = diff * (1.0 / D_MODEL)
        dg2_r[...] += jnp.sum(dy * xhat2, axis=0, keepdims=True)
        db2_r[...] += jnp.sum(dy, axis=0, keepdims=True)
        dr2 = _layer_norm_bwd(dy * g2_r[...], xhat2, rstd2)
        dr2b = dr2.astype(BF16)
        dr2b_r[...] = dr2b
        dx1 = ALPHA * dr2
        for j in range(N_SHARDS):
            dhid = lax.dot_general(dr2b, wd_r[j], (((1,), (1,)), ((), ())), preferred_element_type=F32)
            hg = hg_s[j]
            hu = hu_s[j]
            sg = _sigmoid(hg)
            dhu = (dhid * (hg * sg)).astype(BF16)
            dhg = (dhid * hu * (sg * (1.0 + hg * (1.0 - sg)))).astype(BF16)
            dhg_r[j] = dhg
            dhu_r[j] = dhu
            dx1 = dx1 + jnp.dot(dhg, wg_r[j], preferred_element_type=F32)
            dx1 = dx1 + jnp.dot(dhu, wu_r[j], preferred_element_type=F32)
        dg1_r[...] += jnp.sum(dx1 * xhat1_v, axis=0, keepdims=True)
        db1_r[...] += jnp.sum(dx1, axis=0, keepdims=True)
        dr1_r[...] = _layer_norm_bwd(dx1 * g1_r[...], xhat1_v, rstd_r[...])

    tile = pl.BlockSpec((tm, D_MODEL), lambda i: (i, 0))
    hidden = pl.BlockSpec((N_SHARDS, tm, FFN_SHARD), lambda i: (0, i, 0))
    vec = _const((1, D_MODEL))
    hid_shape = jax.ShapeDtypeStruct((N_SHARDS, t, FFN_SHARD), BF16)
    vec_shape = jax.ShapeDtypeStruct((1, D_MODEL), F32)
    return pl.pallas_call(
        body, name="ffn_step", grid=(t // tm,),
        out_shape=(jax.ShapeDtypeStruct((1, 1), F32), jax.ShapeDtypeStruct((t, D_MODEL), F32),
                   jax.ShapeDtypeStruct((t, D_MODEL), BF16), jax.ShapeDtypeStruct((t, D_MODEL), BF16),
                   hid_shape, hid_shape, hid_shape, vec_shape, vec_shape, vec_shape, vec_shape),
        in_specs=[tile, pl.BlockSpec((tm, 1), lambda i: (i, 0)), tile, vec, vec, vec, vec,
                  _const((N_SHARDS, FFN_SHARD, D_MODEL)), _const((N_SHARDS, FFN_SHARD, D_MODEL)),
                  _const((N_SHARDS, FFN_SHARD, D_MODEL))],
        out_specs=(_const((1, 1)), tile, tile, tile, hidden, hidden, hidden, vec, vec, vec, vec),
        scratch_shapes=[pltpu.VMEM((N_SHARDS, tm, FFN_SHARD), F32), pltpu.VMEM((N_SHARDS, tm, FFN_SHARD), F32)],
        compiler_params=_params(("arbitrary",)),
    )(xhat1, rstd1, target, ln1_g, ln1_b, ln2_g, ln2_b, wg_st, wu_st, wd_st)


def _ffn_weight_grads(x1b, dr2b, hid, dhg, dhu, tk=2048):
    t = x1b.shape[0]

    def body(x_r, dr_r, hid_r, dhg_r, dhu_r, gwg_r, gwu_r, gwd_r):
        @pl.when(pl.program_id(1) == 0)
        def _():
            for r in (gwg_r, gwu_r, gwd_r):
                r[...] = jnp.zeros_like(r)

        gwg_r[0] += _tdot(dhg_r[0], x_r[...])
        gwu_r[0] += _tdot(dhu_r[0], x_r[...])
        gwd_r[0] += _tdot(hid_r[0], dr_r[...])

    tile = pl.BlockSpec((tk, D_MODEL), lambda j, k: (k, 0))
    hidden = pl.BlockSpec((1, tk, FFN_SHARD), lambda j, k: (j, k, 0))
    row = pl.BlockSpec((1, FFN_SHARD, D_MODEL), lambda j, k: (j, 0, 0))
    return pl.pallas_call(
        body, name="ffn_weight_grads", grid=(N_SHARDS, t // tk),
        out_shape=(jax.ShapeDtypeStruct((N_SHARDS, FFN_SHARD, D_MODEL), F32),) * 3,
        in_specs=[tile, tile, hidden, hidden, hidden],
        out_specs=(row, row, row),
        compiler_params=_params(("parallel", "arbitrary")),
    )(x1b, dr2b, hid, dhg, dhu)


def _mixer_backward(dr1, proj, ya0, ya, yb, glu_w, glu_b, wso_st, conv_w8, wco_st, w_o, comms=(), tm=256):
    t = dr1.shape[0]
    hb = tm // SUBLANES
    last_block = t // SUBLANES - 1

    def body(dr1_r, dr1n_r, ya0_r, ya_r, yb_r, h_r, cg_r, bg_r, ga_r, gb_r, hp_r, cgp_r, bgn_r, gbn_r,
             glu_w_r, glu_b_r, wso_r, cw_r, wco_r, wo_r,
             dya0_r, dproj_r, dbias_r, gwo_r, gwso_r, gwco_r, gglu_w_r, gglu_b_r, gconv_r):
        i = pl.program_id(0)

        @pl.when(i == 0)
        def _():
            for r in (dbias_r, gwo_r, gwso_r, gwco_r, gglu_w_r, gglu_b_r, gconv_r):
                r[...] = jnp.zeros_like(r)

        dr1_v = dr1_r[...]
        dmerged = _dot_t(dr1_v, wo_r[...])
        sa = _sigmoid(ga_r[...])
        sb = _sigmoid(gb_r[...])
        ya_v = ya_r[...]
        yb_v = yb_r[...]
        gwo_r[...] += _tdot(sa * ya_v + sb * yb_v, dr1_v)
        dya = dmerged * sa
        dyb = dmerged * sb
        dga = dmerged * ya_v * (sa * (1.0 - sa))
        dgb = dmerged * yb_v * (sb * (1.0 - sb))

        g, gelu_grad = _gelu_parts(ya0_r[...])
        s1 = _sigmoid(_dot(g, glu_w_r[...]) + glu_b_r[...])
        ya1 = g * s1
        dya1 = jnp.zeros((tm, SSM_W), F32)
        for j in range(N_SHARDS):
            dya_j = dya[:, 256 * j:256 * (j + 1)]
            gwso_r[j] += _tdot(ya1, dya_j)
            dya1 = dya1 + _dot_t(dya_j, wso_r[j])
        dz1 = dya1 * g * (s1 * (1.0 - s1))
        gglu_b_r[...] += jnp.sum(dz1, axis=0, keepdims=True)
        gglu_w_r[...] += _tdot(g, dz1)
        dya0_r[...] = (dya1 * s1 + _dot_t(dz1, glu_w_r[...])) * gelu_grad

        cw = cw_r[...]
        h = h_r[...]
        cg = cg_r[...]
        bg = bg_r[...]
        q = cg * h
        q_prev = jnp.where(i > 0, cgp_r[...] * hp_r[...], 0.0)
        q1 = _shift_down(q, q_prev, 1)
        q2 = _shift_down(q, q_prev, 2)
        z = cw[2:3, :] * q + cw[1:2, :] * q1 + cw[0:1, :] * q2
        yb0 = bg * z
        dyb0 = jnp.zeros((tm, SSM_W), F32)
        for j in range(N_SHARDS):
            dyb_j = dyb[:, 256 * j:256 * (j + 1)]
            gwco_r[j] += _tdot(yb0, dyb_j)
            dyb0 = dyb0 + _dot_t(dyb_j, wco_r[j])
        dbg = dyb0 * z
        dz = dyb0 * bg
        dyb_n = _dot_t(dr1n_r[...], wo_r[...]) * _sigmoid(gbn_r[...])
        dyb0_n = jnp.zeros((SUBLANES, SSM_W), F32)
        for j in range(N_SHARDS):
            dyb0_n = dyb0_n + _dot_t(dyb_n[:, 256 * j:256 * (j + 1)], wco_r[j])
        dz_next = jnp.where(i < pl.num_programs(0) - 1, dyb0_n * bgn_r[...], 0.0)
        dq = cw[2:3, :] * dz + cw[1:2, :] * _shift_up(dz, dz_next, 1) + cw[0:1, :] * _shift_up(dz, dz_next, 2)
        gconv_r[0:1, :] += jnp.sum(dz * q2, axis=0, keepdims=True)
        gconv_r[1:2, :] += jnp.sum(dz * q1, axis=0, keepdims=True)
        gconv_r[2:3, :] += jnp.sum(dz * q, axis=0, keepdims=True)
        dh = dq * cg
        dcg = dq * h

        dproj_r[:, 0:512] = jnp.zeros((tm, SSM_W), BF16)
        pieces = ((512, dh), (1024, dcg), (1536, dbg), (2048, dga), (3072, dgb))
        for off, val in pieces:
            w = val.shape[1]
            dproj_r[:, off:off + w] = val.astype(BF16)
            dbias_r[:, off:off + w] += jnp.sum(val, axis=0, keepdims=True)

    def col(w, c):
        return pl.BlockSpec((tm, w), lambda i: (i, c))

    def prev(c):
        return pl.BlockSpec((SUBLANES, SSM_W), lambda i: (jnp.maximum(i * hb - 1, 0), c))

    def nxt(w, c):
        return pl.BlockSpec((SUBLANES, w), lambda i: (jnp.minimum((i + 1) * hb, last_block), c))

    sh = jax.ShapeDtypeStruct
    return _launch(
        body, comms, name="mixer_backward", grid=(t // tm,),
        out_shape=(sh((t, SSM_W), F32), sh((t, IN_COLS), BF16), sh((1, IN_COLS), F32),
                   sh((D_MODEL, D_MODEL), F32), sh((N_SHARDS, SSM_W, 256), F32), sh((N_SHARDS, SSM_W, 256), F32),
                   sh((SSM_W, SSM_W), F32), sh((1, SSM_W), F32), sh((SUBLANES, SSM_W), F32)),
        in_specs=[col(D_MODEL, 0), nxt(D_MODEL, 0), col(SSM_W, 0), col(D_MODEL, 0), col(D_MODEL, 0),
                  col(SSM_W, 1), col(SSM_W, 2), col(SSM_W, 3), col(D_MODEL, 2), col(D_MODEL, 3),
                  prev(1), prev(2), nxt(SSM_W, 3), nxt(D_MODEL, 3),
                  _const((SSM_W, SSM_W)), _const((1, SSM_W)), _const((N_SHARDS, SSM_W, 256)),
                  _const((SUBLANES, SSM_W)), _const((N_SHARDS, SSM_W, 256)), _const((D_MODEL, D_MODEL))],
        out_specs=(col(SSM_W, 0), col(IN_COLS, 0), _const((1, IN_COLS)),
                   _const((D_MODEL, D_MODEL)), _const((N_SHARDS, SSM_W, 256)), _const((N_SHARDS, SSM_W, 256)),
                   _const((SSM_W, SSM_W)), _const((1, SSM_W)), _const((SUBLANES, SSM_W))),
        sem=("arbitrary",),
    )(dr1, dr1, ya0, ya, yb, proj, proj, proj, proj, proj, proj, proj, proj, proj,
      glu_w, glu_b, wso_st, conv_w8, wco_st, w_o)


def _cmulc_add(xr, xi, mr, mi, sr, si):
    return xr + (mr * sr + mi * si), xi + (mr * si - mi * sr)


def _ssm_backward(dya0, proj, xsr, xsi, bbr, bbi, ctr, cti, d_skip, tab_a, tab_p, dproj, comms=(), tc=SCAN_CHUNK):
    t = proj.shape[0]
    nk = t // tc

    def body(dy_r, u_r, xsr_r, xsi_r, bbr_r, bbi_r, ctr_r, cti_r, d_r, ta_r, tp_r, dproj_any,
             du_r, dus_r, gbbr_r, gbbi_r, gctr_r, gcti_r, glbr_r, glbi_r, gd_r,
             gr_s, gi_s, dyi_s, ui_s, dui_s, dun_s, car_r, car_i):
        del dproj_any

        @pl.when(pl.program_id(1) == 0)
        def _():
            for r in (car_r, car_i, dus_r, gbbr_r, gbbi_r, gctr_r, gcti_r, glbr_r, glbi_r, gd_r):
                r[...] = jnp.zeros_like(r)

        _interleave(dy_r, dyi_s)
        _interleave(u_r, ui_s)
        dy = dyi_s[...]
        u = ui_s[...]
        gr_s[...] = _dot(dy, ctr_r[...])
        gi_s[...] = -_dot(dy, cti_r[...])
        a_r, a_i = ta_r[0], ta_r[1]

        def local(n, carry):
            rows = _step_rows(SCAN_STEPS - 1 - n)
            gr, gi = _cmulc_add(gr_s[rows, :], gi_s[rows, :], a_r, a_i, *carry)
            gr_s[rows, :] = gr
            gi_s[rows, :] = gi
            return gr, gi

        zero = jnp.zeros((SUBLANES, STRIP), F32)
        ends_r, ends_i = lax.fori_loop(0, SCAN_STEPS, local, (zero, zero), unroll=2)
        ent_r, ent_i, out_r, out_i = _segment_states(
            car_r[...], car_i[...], ends_r, ends_i, ta_r[2, 0:1, :], -ta_r[3, 0:1, :], range(SUBLANES - 1, -1, -1))
        car_r[...] = out_r
        car_i[...] = out_i

        def entering(n, carry):
            gnr, gni, ar, ai = carry
            rows = _step_rows(SCAN_STEPS - 1 - n)
            power = _step_rows(n)
            gr, gi = _cmulc_add(gr_s[rows, :], gi_s[rows, :], tp_r[0, power, :], tp_r[1, power, :], ent_r, ent_i)
            gr_s[rows, :] = gr
            gi_s[rows, :] = gi
            xr = xsr_r[rows, :]
            xi = xsi_r[rows, :]
            return gr, gi, ar + (xr * gnr + xi * gni), ai + (xr * gni - xi * gnr)

        _, _, ar, ai = lax.fori_loop(0, SCAN_STEPS, entering, (ent_r, ent_i, zero, zero), unroll=2)
        glbr_r[...] += ar
        glbi_r[...] += ai
        gr = gr_s[...]
        gi = gi_s[...]
        dui_s[...] = _dot_t(gr, bbr_r[...]) + _dot_t(gi, bbi_r[...]) + d_r[...] * dy
        _deinterleave(dui_s, dun_s)
        du = dun_s[...]
        du_r[...] = du.astype(BF16)
        dus_r[...] += jnp.sum(du, axis=0, keepdims=True)
        gd_r[...] += jnp.sum(dy * u, axis=0, keepdims=True)
        gbbr_r[...] += _tdot(u, gr)
        gbbi_r[...] += _tdot(u, gi)
        gctr_r[...] += _tdot(dy, xsr_r[...])
        gcti_r[...] -= _tdot(dy, xsi_r[...])

    def rev(w):
        return pl.BlockSpec((tc, w), lambda j, k: (nk - 1 - k, j))

    strip_mat = pl.BlockSpec((128, STRIP), lambda j, k: (j, 0))
    vec = pl.BlockSpec((1, 128), lambda j, k: (0, j))
    lbacc = pl.BlockSpec((SUBLANES, STRIP), lambda j, k: (0, j))
    sh = jax.ShapeDtypeStruct
    return _launch(
        body, comms, name="ssm_backward", grid=(N_STRIPS, nk),
        out_shape=(sh((t, IN_COLS), BF16), sh((1, SSM_W), F32),
                   sh((SSM_W, STRIP), F32), sh((SSM_W, STRIP), F32), sh((SSM_W, STRIP), F32), sh((SSM_W, STRIP), F32),
                   sh((SUBLANES, STATE_COLS), F32), sh((SUBLANES, STATE_COLS), F32), sh((1, SSM_W), F32)),
        in_specs=[rev(128), rev(128), rev(STRIP), rev(STRIP),
                  strip_mat, strip_mat, strip_mat, strip_mat, vec,
                  pl.BlockSpec((4, SUBLANES, STRIP), lambda j, k: (0, 0, j)),
                  pl.BlockSpec((2, tc, STRIP), lambda j, k: (0, 0, j)), ANY],
        out_specs=(rev(128), vec, strip_mat, strip_mat, strip_mat, strip_mat, lbacc, lbacc, vec),
        scratch_shapes=[pltpu.VMEM((tc, STRIP), F32), pltpu.VMEM((tc, STRIP), F32)]
        + [pltpu.VMEM((tc, 128), F32)] * 4 + [pltpu.VMEM((1, STRIP), F32)] * 2,
        aliases={11: 0}, sem=("parallel", "arbitrary"),
    )(dya0, proj, xsr, xsi, bbr, bbi, ctr, cti, d_skip, tab_a, tab_p, dproj)


def _input_grad(dr1, dproj, w_in_st, comms=(), tm=512):
    t = dr1.shape[0]

    def body(dr1_r, dp_r, w_r, dx_r):
        acc = ALPHA * dr1_r[...]
        for j in range(N_SHARDS):
            acc = acc + lax.dot_general(dp_r[:, D_MODEL * j:D_MODEL * (j + 1)], w_r[j],
                                        (((1,), (1,)), ((), ())), preferred_element_type=F32)
        dx_r[...] = acc

    (dx,), sent = _launch(
        body, comms, name="input_grad", grid=(t // tm,),
        out_shape=(jax.ShapeDtypeStruct((t, D_MODEL), F32),),
        in_specs=[pl.BlockSpec((tm, D_MODEL), lambda i: (i, 0)), pl.BlockSpec((tm, IN_COLS), lambda i: (i, 0)),
                  _const((N_SHARDS, D_MODEL, D_MODEL))],
        out_specs=(pl.BlockSpec((tm, D_MODEL), lambda i: (i, 0)),),
        sem=("parallel",),
    )(dr1, dproj, w_in_st)
    return dx, sent


def _in_weight_grad(x, dproj, comms=(), tk=2048):
    t = x.shape[0]

    def body(x_r, dp_r, gw_r):
        @pl.when(pl.program_id(1) == 0)
        def _():
            gw_r[...] = jnp.zeros_like(gw_r)

        gw_r[0] += _tdot(x_r[...], dp_r[...])

    (g_w_in,), sent = _launch(
        body, comms, name="in_weight_grad", grid=(N_SHARDS, t // tk),
        out_shape=(jax.ShapeDtypeStruct((N_SHARDS, D_MODEL, D_MODEL), F32),),
        in_specs=[pl.BlockSpec((tk, D_MODEL), lambda j, k: (k, 0)), pl.BlockSpec((tk, D_MODEL), lambda j, k: (k, j))],
        out_specs=(pl.BlockSpec((1, D_MODEL, D_MODEL), lambda j, k: (j, 0, 0)),),
        sem=("parallel", "arbitrary"),
    )(x, dproj)
    return g_w_in, sent


MIXER_W = ("glu_w", "w_ssm_out", "w_conv_out", "w_o")
FFN_W = ("w_gate", "w_up", "w_down")


def _device_step(x, target, small, shards, c_arr, me_arr):
    lr, li = small["ssm_lambda_re"][0], small["ssm_lambda_im"][0]
    ldt = small["ssm_log_dt"][0][:, None]
    rep16 = lambda a: jnp.broadcast_to(a[:, None, :], (N_GROUPS, GROUP_C, a.shape[-1])).reshape(SSM_W, a.shape[-1])
    lr16, li16 = rep16(lr), rep16(li)
    ldt16 = rep16(jnp.broadcast_to(ldt, (N_GROUPS, N_STATE)))
    brt = small["ssm_b_re"][0].transpose(0, 2, 1).reshape(SSM_W, N_STATE)
    bit = small["ssm_b_im"][0].transpose(0, 2, 1).reshape(SSM_W, N_STATE)
    cre = small["ssm_c_re"][0].reshape(SSM_W, N_STATE)
    cim = small["ssm_c_im"][0].reshape(SSM_W, N_STATE)
    disc = (lr, li, ldt, lr16, li16, ldt16, brt, bit)

    pwr, pwi, bbr, bbi, ctr, cti = _ssm_prepare(*disc, cre, cim)
    tab_a, tab_p = _scan_tables(pwr, pwi)

    first_sh = [shards[n] for n in MIXER_W + FFN_W[:1]]
    second_sh = [shards[n] for n in FFN_W[1:]]
    (w_in_st,) = _gather_weights([shards["w_in"]])
    proj, (arrived,) = _in_proj(x, w_in_st, small["b_in"], comms=[_gather_ici(first_sh, [shards["conv_w"]])])
    (xsr, xsi, ya0), (second_part, first_st) = _ssm_forward(
        proj, bbr, bbi, ctr, cti, small["ssm_d"], tab_a, tab_p,
        comms=[_gather_ici(second_sh), _gather_d2d(arrived[:len(first_sh)], first_sh)])
    glu_st, wso_st, wco_st, wo_st, wg_st = (_own_slot(st, sh) for st, sh in zip(first_st, first_sh))
    conv_st = _own_slot(arrived[len(first_sh)], shards["conv_w"])
    conv_w8 = jnp.pad(conv_st[:, :3, :].transpose(1, 0, 2).reshape(3, SSM_W), ((0, SUBLANES - 3), (0, 0)))
    w_o = wo_st.reshape(D_MODEL, D_MODEL)
    glu_w = glu_st.reshape(SSM_W, SSM_W)
    (xhat1, rstd1, ya, yb), (second_st,) = _mixer_forward(
        x, proj, ya0, glu_w, small["glu_b"], wso_st, conv_w8, wco_st, w_o, comms=[_gather_d2d(second_part, second_sh)])
    wu_st, wd_st = (_own_slot(st, sh) for st, sh in zip(second_st, second_sh))
    (loss, dr1, x1b, dr2b, hid, dhg, dhu, g_ln2_g, g_ln2_b, g_ln1_g, g_ln1_b) = _ffn_step(
        xhat1, rstd1, target, small["ln1_g"], small["ln1_b"], small["ln2_g"], small["ln2_b"], wg_st, wu_st, wd_st)

    add_halves = lambda gs, rs: _per_shape(lambda a, b: _add_own_half(a, b, c_arr), list(gs), list(rs))
    sum_chips = lambda owns, slots: _per_shape(lambda a, b: _sum_chips(a, b, me_arr), list(owns), list(slots))
    g_ffn = _ffn_weight_grads(x1b, dr2b, hid, dhg, dhu)
    (dya0, dproj, dbias, g_wo, g_wso, g_wco, g_glu_w, g_glu_b, g_conv8), (got_ffn,) = _mixer_backward(
        dr1, proj, ya0, ya, yb, glu_w, small["glu_b"], wso_st, conv_w8, wco_st, w_o, comms=[_swap_comm(g_ffn)])
    chip_ffn = add_halves(g_ffn, got_ffn)
    g_mix = [g_glu_w.reshape(N_SHARDS, 128, SSM_W), g_wso, g_wco, g_wo.reshape(N_SHARDS, 256, D_MODEL)]
    (dproj, dus, gbbr, gbbi, gctr, gcti, glbr, glbi, g_d), (slots_ffn, got_mix) = _ssm_backward(
        dya0, proj, xsr, xsi, bbr, bbi, ctr, cti, small["ssm_d"], tab_a, tab_p, dproj,
        comms=[_scatter_comm(chip_ffn), _swap_comm(g_mix)])
    halves_ffn = sum_chips(chip_ffn, slots_ffn)
    chip_mix = add_halves(g_mix, got_mix)
    g_lr, g_li, g_ldt, g_brt, g_bit, g_cre, g_cim = _ssm_param_grads(
        *disc, glbr.reshape(SUBLANES, N_GROUPS, N_STATE), glbi.reshape(SUBLANES, N_GROUPS, N_STATE),
        gbbr, gbbi, gctr, gcti)
    g_w_in, (others_ffn, slots_mix) = _in_weight_grad(
        x, dproj, comms=[_send_comm(halves_ffn), _scatter_comm(chip_mix)])
    halves_mix = sum_chips(chip_mix, slots_mix)
    dx, _ = _input_grad(dr1, dproj, w_in_st)

    g_conv = jnp.pad(g_conv8[:3].reshape(3, N_SHARDS, 128).transpose(1, 0, 2), ((0, 0), (0, SUBLANES - 3), (0, 0)))
    pieces = [dus, dbias[:, SSM_W:], g_lr, g_li, g_ldt, g_brt, g_bit, g_cre, g_cim, g_d, g_glu_b,
              g_ln1_g, g_ln1_b, g_ln2_g, g_ln2_b, loss]
    flat = jnp.concatenate([p.reshape(-1) for p in pieces])
    g_packed = jnp.pad(flat, (0, PACKED_ROWS * 128 - flat.shape[0])).reshape(PACKED_ROWS, 128)
    ((got_w, got_conv, got_packed),) = _standalone([_swap_comm([g_w_in], [g_conv, g_packed])], "swap_with_sibling")
    (chip_w,) = add_halves([g_w_in], [got_w])
    chip_conv, chip_packed = _small_pair_sums([g_conv, g_packed], [got_conv, got_packed])
    ((slots_w, slots_conv, slots_packed),) = _standalone(
        [_scatter_comm([chip_w, chip_conv], [chip_packed])], "scatter_to_chips")
    (halves_w,) = sum_chips([chip_w], [slots_w])
    conv_total, packed_total = _small_totals(me_arr, chip_conv, slots_conv, chip_packed, slots_packed)
    (others_rest,) = _standalone([_send_comm([halves_w] + halves_mix)], "send_to_sibling")

    pairs = dict(zip(FFN_W, zip(halves_ffn, others_ffn)))
    pairs.update(zip(("w_in",) + MIXER_W, zip([halves_w] + halves_mix, others_rest)))
    return dx, pairs, conv_total, packed_total


PACKED_ROWS = 1136
PACKED_LAYOUT = (("b_in", IN_COLS), ("ssm_lambda_re", STATE_COLS), ("ssm_lambda_im", STATE_COLS),
                 ("ssm_log_dt", N_GROUPS), ("ssm_b_re", SSM_W * N_STATE), ("ssm_b_im", SSM_W * N_STATE),
                 ("ssm_c_re", SSM_W * N_STATE), ("ssm_c_im", SSM_W * N_STATE), ("ssm_d", SSM_W), ("glu_b", SSM_W),
                 ("ln1_g", D_MODEL), ("ln1_b", D_MODEL), ("ln2_g", D_MODEL), ("ln2_b", D_MODEL), ("loss", 1))


def _unpack_small(packed):
    flat = packed.reshape(-1)
    out, off = {}, 0
    for name, size in PACKED_LAYOUT:
        out[name] = flat[off:off + size]
        off += size
    for name in ("ssm_b_re", "ssm_b_im"):
        out[name] = out[name].reshape(N_GROUPS, GROUP_C, N_STATE).transpose(0, 2, 1)[None]
    for name in ("ssm_c_re", "ssm_c_im"):
        out[name] = out[name].reshape(1, N_GROUPS, GROUP_C, N_STATE)
    for name in ("ssm_lambda_re", "ssm_lambda_im"):
        out[name] = out[name].reshape(1, N_GROUPS, N_STATE)
    for name in ("b_in", "ssm_log_dt", "ssm_d", "glu_b", "ln1_g", "ln1_b", "ln2_g", "ln2_b"):
        out[name] = out[name][None]
    return out


BIG = ("w_in", "glu_w", "w_ssm_out", "w_conv_out", "w_o", "w_gate", "w_up", "w_down")
SMALL = ("b_in", "ssm_lambda_re", "ssm_lambda_im", "ssm_log_dt", "ssm_b_re", "ssm_b_im", "ssm_c_re", "ssm_c_im",
         "ssm_d", "glu_b", "ln1_g", "ln1_b", "ln2_g", "ln2_b")
WEIGHTS = ("w_in", "b_in", "ssm_lambda_re", "ssm_lambda_im", "ssm_log_dt", "ssm_b_re", "ssm_b_im", "ssm_c_re",
           "ssm_c_im", "ssm_d", "glu_w", "glu_b", "w_ssm_out", "conv_w", "w_conv_out", "w_o", "ln1_g", "ln1_b",
           "w_gate", "w_up", "w_down", "ln2_g", "ln2_b")


def _place():
    x, y, c = lax.axis_index("x"), lax.axis_index("y"), lax.axis_index("c")
    chips = [(1 - x, y), (x, 1 - y), (1 - x, 1 - y)]
    return x, y, c, chips


def _shard_of(chip):
    return 2 * chip[0] + chip[1]


def _remote(src, dst, send_sem, recv_sem, to):
    return pltpu.make_async_remote_copy(src_ref=src, dst_ref=dst, send_sem=send_sem, recv_sem=recv_sem,
                                        device_id=to, device_id_type=MESH)


def _half_rows(shard, which):
    r2 = shard.shape[0] // 2
    return pl.ds(pl.multiple_of(which * r2, 16), r2)


def _own_slot(stack, shard):
    me = _shard_of((lax.axis_index("x"), lax.axis_index("y")))
    return lax.dynamic_update_slice(stack, shard[None], (me,) + (0,) * shard.ndim)


def _gather_ici(halved, whole=()):
    shards = list(halved) + list(whole)
    nh = len(halved)

    def copies(src, dst, sems):
        send_sem, recv_sem = sems
        x, y, c, chips = _place()
        me = _shard_of((x, y))
        out = []
        for a in range(len(shards)):
            for k, chip in enumerate(chips):
                if a < nh:
                    rows = _half_rows(shards[a], c)
                    out.append(_remote(src[a].at[rows], dst[a].at[me, rows], send_sem.at[a, k], recv_sem.at[a, k],
                                       (*chip, c)))
                else:
                    out.append(_remote(src[a], dst[a].at[me], send_sem.at[a, k], recv_sem.at[a, k], (*chip, c)))
        return out

    n = len(shards)
    return _Comm(shards, [jax.ShapeDtypeStruct((N_SHARDS,) + s.shape, s.dtype) for s in shards],
                 [pltpu.SemaphoreType.DMA((n, 3))] * 2, copies)


def _gather_d2d(stacks, shards):
    def copies(src, dst, sems):
        del src
        send_sem, recv_sem = sems
        x, y, c, chips = _place()
        out = []
        for a in range(len(stacks)):
            for k, chip in enumerate(chips):
                rows = dst[a].at[_shard_of(chip), _half_rows(shards[a], c)]
                out.append(_remote(rows, rows, send_sem.at[a, k], recv_sem.at[a, k], (x, y, 1 - c)))
        return out

    n = len(stacks)
    return _Comm(stacks, [jax.ShapeDtypeStruct(s.shape, s.dtype) for s in stacks],
                 [pltpu.SemaphoreType.DMA((n, 3))] * 2, copies, aliased=True)


def _standalone(comms, name):
    return _launch(None, comms, name=name, grid=(), in_specs=[], out_specs=(), out_shape=())()[1]


def _gather_weights(shards):
    n = len(shards)

    def body(*refs):
        src, dst = refs[:n], refs[n:2 * n]
        send_sem, recv_sem, fsend_sem, frecv_sem = refs[2 * n:]
        x, y, c, chips = _place()
        me = _shard_of((x, y))
        sibling = (x, y, 1 - c)
        sends = []
        for a in range(n):
            mine = _half_rows(shards[a], c)
            for k, chip in enumerate(chips):
                cp = _remote(src[a].at[mine], dst[a].at[me, mine], send_sem.at[a, k], recv_sem.at[a, k], (*chip, c))
                cp.start()
                sends.append(cp)
        for a in range(n):
            for k, chip in enumerate(chips):
                rows = dst[a].at[_shard_of(chip), _half_rows(shards[a], c)]
                _remote(rows, rows, send_sem.at[a, k], recv_sem.at[a, k], sibling).wait_recv()
                cp = _remote(rows, rows, fsend_sem.at[a, k], frecv_sem.at[a, k], sibling)
                cp.start()
                sends.append(cp)
        for a in range(n):
            for k, chip in enumerate(chips):
                rows = dst[a].at[_shard_of(chip), _half_rows(shards[a], 1 - c)]
                _remote(rows, rows, fsend_sem.at[a, k], frecv_sem.at[a, k], sibling).wait_recv()
        for cp in sends:
            cp.wait_send()

    stacks = pl.pallas_call(
        body, name="gather_weights",
        out_shape=tuple(jax.ShapeDtypeStruct((N_SHARDS,) + s.shape, s.dtype) for s in shards),
        in_specs=[ANY] * n, out_specs=(ANY,) * n,
        scratch_shapes=[pltpu.SemaphoreType.DMA((n, 3))] * 4,
    )(*shards)
    return [_own_slot(st, sh) for st, sh in zip(stacks, shards)]


def _swap_comm(big, small=()):
    nb, n = len(big), len(big) + len(small)
    arrays = list(big) + list(small)

    def copies(src, dst, sems):
        send_sem, recv_sem = sems
        x, y, c, _ = _place()
        out = []
        for a in range(n):
            if a < nb:
                r2 = arrays[a].shape[1] // 2
                part = src[a].at[:, pl.ds(pl.multiple_of((1 - c) * r2, SUBLANES), r2), :]
            else:
                part = src[a]
            out.append(_remote(part, dst[a], send_sem.at[a], recv_sem.at[a], (x, y, 1 - c)))
        return out

    out_shape = [jax.ShapeDtypeStruct((N_SHARDS, g.shape[1] // 2, g.shape[2]), g.dtype) for g in big]
    out_shape += [jax.ShapeDtypeStruct(g.shape, g.dtype) for g in small]
    return _Comm(arrays, out_shape, [pltpu.SemaphoreType.DMA((n,))] * 2, copies)


def _scatter_comm(slabbed, small=()):
    ns, n = len(slabbed), len(slabbed) + len(small)
    arrays = list(slabbed) + list(small)

    def copies(src, dst, sems):
        send_sem, recv_sem = sems
        _, _, c, chips = _place()
        out = []
        for a in range(n):
            for k, chip in enumerate(chips):
                part = src[a].at[_shard_of(chip)] if a < ns else src[a]
                out.append(_remote(part, dst[a].at[k], send_sem.at[a, k], recv_sem.at[a, k], (*chip, c)))
        return out

    out_shape = [jax.ShapeDtypeStruct((3,) + g.shape[1:], g.dtype) for g in slabbed]
    out_shape += [jax.ShapeDtypeStruct((3,) + g.shape, g.dtype) for g in small]
    return _Comm(arrays, out_shape, [pltpu.SemaphoreType.DMA((n, 3))] * 2, copies)


def _send_comm(arrays):
    n = len(arrays)

    def copies(src, dst, sems):
        send_sem, recv_sem = sems
        x, y, c, _ = _place()
        return [_remote(src[a], dst[a], send_sem.at[a], recv_sem.at[a], (x, y, 1 - c)) for a in range(n)]

    return _Comm(arrays, [jax.ShapeDtypeStruct(h.shape, h.dtype) for h in arrays],
                 [pltpu.SemaphoreType.DMA((n,))] * 2, copies)


def _row_chunk(rows):
    for cand in (256, 176, 128, 64):
        if rows % cand == 0:
            return cand
    return rows


def _per_shape(fn, *lists):
    groups = {}
    for i, items in enumerate(zip(*lists)):
        groups.setdefault(tuple(a.shape for a in items), []).append(i)
    out = [None] * len(lists[0])
    for idx in groups.values():
        for i, r in zip(idx, fn(*([lst[i] for i in idx] for lst in lists))):
            out[i] = r
    return out


def _add_own_half(stacks, receiveds, c):
    n = len(stacks)
    _, r2, cols = receiveds[0].shape

    def body(c_ref, *refs):
        del c_ref
        for a in range(n):
            refs[2 * n + a][...] = (refs[a][...] + refs[n + a][...]).astype(BF16)

    own = pl.BlockSpec((1, r2, cols), lambda s, c_ref: (s, c_ref[0], 0))
    got = pl.BlockSpec((1, r2, cols), lambda s, c_ref: (s, 0, 0))
    return pl.pallas_call(
        body, name="add_own_half",
        grid_spec=pltpu.PrefetchScalarGridSpec(
            num_scalar_prefetch=1, grid=(N_SHARDS,), in_specs=[own] * n + [got] * n, out_specs=(got,) * n),
        out_shape=(jax.ShapeDtypeStruct(receiveds[0].shape, BF16),) * n,
        compiler_params=_params(("parallel",)),
    )(c, *stacks, *receiveds)


def _chip_order_sum(me, own, s0, s1, s2):
    terms = []
    for s in range(N_SHARDS):
        d = jnp.bitwise_xor(me, s)
        terms.append(jnp.where(d == 0, own, jnp.where(d == 2, s0, jnp.where(d == 1, s1, s2))))
    return ((terms[0] + terms[1]) + terms[2]) + terms[3]


def _sum_chips(own_stacks, slots, me):
    n = len(slots)
    _, rows, cols = slots[0].shape
    rc = _row_chunk(rows)

    def body(me_ref, *refs):
        del me_ref
        for a in range(n):
            own_r, s_r = refs[a], refs[n + a]
            refs[2 * n + a][...] = (((own_r[0].astype(F32) + s_r[0].astype(F32)) + s_r[1].astype(F32))
                                    + s_r[2].astype(F32))

    own = pl.BlockSpec((1, rc, cols), lambda i, me_ref: (me_ref[0], i, 0))
    three = pl.BlockSpec((3, rc, cols), lambda i, me_ref: (0, i, 0))
    total = pl.BlockSpec((rc, cols), lambda i, me_ref: (i, 0))
    return pl.pallas_call(
        body, name="sum_chips",
        grid_spec=pltpu.PrefetchScalarGridSpec(
            num_scalar_prefetch=1, grid=(rows // rc,), in_specs=[own] * n + [three] * n, out_specs=(total,) * n),
        out_shape=(jax.ShapeDtypeStruct((rows, cols), F32),) * n,
        compiler_params=_params(("parallel",)),
    )(me, *own_stacks, *slots)


def _small_pair_sums(mine, theirs):
    n = len(mine)

    def body(*refs):
        for a in range(n):
            refs[2 * n + a][...] = refs[a][...] + refs[n + a][...]

    vm = pl.BlockSpec(memory_space=pltpu.VMEM)
    return pl.pallas_call(
        body, name="small_pair_sums", out_shape=tuple(jax.ShapeDtypeStruct(g.shape, g.dtype) for g in mine),
        in_specs=[vm] * (2 * n), out_specs=(vm,) * n,
        compiler_params=pltpu.CompilerParams(vmem_limit_bytes=VMEM_LIMIT),
    )(*mine, *theirs)


def _adam_math(w, g, m, v):
    m = ADAM_B1 * m + (1.0 - ADAM_B1) * g
    v = ADAM_B2 * v + (1.0 - ADAM_B2) * (g * g)
    m_hat = m / (1.0 - ADAM_B1 ** ADAM_STEP)
    v_hat = v / (1.0 - ADAM_B2 ** ADAM_STEP)
    delta = -ADAM_LR * (m_hat / (jnp.sqrt(v_hat) + ADAM_EPS) + ADAM_WD * w)
    return delta, m, v


def _small_totals(me, conv_stack, conv_slots, packed, packed_slots):
    def body(me_ref, cs_r, cslot_r, p_r, pslot_r, conv_r, tot_r):
        me_ = me_ref[0]
        conv_r[...] = _chip_order_sum(me_, cs_r[me_], cslot_r[0], cslot_r[1], cslot_r[2])
        tot_r[...] = _chip_order_sum(me_, p_r[...], pslot_r[0], pslot_r[1], pslot_r[2])

    vm = pl.BlockSpec(memory_space=pltpu.VMEM)
    return pl.pallas_call(
        body, name="small_totals",
        out_shape=(jax.ShapeDtypeStruct(conv_stack.shape[1:], F32), jax.ShapeDtypeStruct(packed.shape, F32)),
        in_specs=[pl.BlockSpec(memory_space=pltpu.SMEM)] + [vm] * 4, out_specs=(vm, vm),
    )(me, conv_stack, conv_slots, packed, packed_slots)


def _adam_small(gs, ws, ms, vs):
    n = len(gs)

    def body(*refs):
        for a in range(n):
            g_r, w_r, m_r, v_r = (refs[i * n + a] for i in range(4))
            d_r, nm_r, nv_r = (refs[(4 + i) * n + a] for i in range(3))
            d_r[...], nm_r[...], nv_r[...] = _adam_math(w_r[...], g_r[...], m_r[...], v_r[...])

    vm = pl.BlockSpec(memory_space=pltpu.VMEM)
    shapes = tuple(jax.ShapeDtypeStruct(w.shape, F32) for w in ws)
    out = pl.pallas_call(
        body, name="adam_small", out_shape=shapes * 3, in_specs=[vm] * (4 * n), out_specs=(vm,) * (3 * n),
        compiler_params=pltpu.CompilerParams(vmem_limit_bytes=VMEM_LIMIT),
    )(*gs, *ws, *ms, *vs)
    return out[:n], out[n:2 * n], out[2 * n:]


def _adam_big(ws, mines, others, ms, vs, c):
    n = len(ws)
    r2, cols = mines[0].shape
    rc = _row_chunk(r2)
    nch = r2 // rc

    def body(c_ref, *refs):
        mine_is_here = pl.program_id(0) == c_ref[0]
        for a in range(n):
            w_r, mine_r, other_r, m_r, v_r = (refs[i * n + a] for i in range(5))
            g_r, d_r, nm_r, nv_r = (refs[(5 + i) * n + a] for i in range(4))
            g = jnp.where(mine_is_here, mine_r[...], other_r[...])
            g_r[...] = g
            d_r[...], nm_r[...], nv_r[...] = _adam_math(w_r[...], g, m_r[...], v_r[...])

    full = pl.BlockSpec((rc, cols), lambda h, i, c_ref: (h * nch + i, 0))
    half = pl.BlockSpec((rc, cols), lambda h, i, c_ref: (i, 0))
    shape = jax.ShapeDtypeStruct((2 * r2, cols), F32)
    out = pl.pallas_call(
        body, name="adam_big",
        grid_spec=pltpu.PrefetchScalarGridSpec(
            num_scalar_prefetch=1, grid=(2, nch),
            in_specs=[full] * n + [half] * (2 * n) + [full] * (2 * n), out_specs=(full,) * (4 * n)),
        out_shape=(shape,) * (4 * n), compiler_params=_params(("parallel", "parallel")),
    )(c, *ws, *mines, *others, *ms, *vs)
    return [tuple(out[i * n + a] for i in range(4)) for a in range(n)]


def kernel(x, w_in, b_in, ssm_lambda_re, ssm_lambda_im, ssm_log_dt, ssm_b_re, ssm_b_im, ssm_c_re, ssm_c_im, ssm_d, glu_w, glu_b, w_ssm_out, conv_w, w_conv_out, w_o, ln1_g, ln1_b, w_gate, w_up, w_down, ln2_g, ln2_b, loss_target, m_w_in, m_b_in, m_ssm_lambda_re, m_ssm_lambda_im, m_ssm_log_dt, m_ssm_b_re, m_ssm_b_im, m_ssm_c_re, m_ssm_c_im, m_ssm_d, m_glu_w, m_glu_b, m_w_ssm_out, m_conv_w, m_w_conv_out, m_w_o, m_ln1_g, m_ln1_b, m_w_gate, m_w_up, m_w_down, m_ln2_g, m_ln2_b, v_w_in, v_b_in, v_ssm_lambda_re, v_ssm_lambda_im, v_ssm_log_dt, v_ssm_b_re, v_ssm_b_im, v_ssm_c_re, v_ssm_c_im, v_ssm_d, v_glu_w, v_glu_b, v_w_ssm_out, v_conv_w, v_w_conv_out, v_w_o, v_ln1_g, v_ln1_b, v_w_gate, v_w_up, v_w_down, v_ln2_g, v_ln2_b):
    given = dict(locals())
    w = {n: given[n] for n in WEIGHTS}
    m = {n: given["m_" + n] for n in WEIGHTS}
    v = {n: given["v_" + n] for n in WEIGHTS}

    flip = lambda n, a: a.T if n in ("w_gate", "w_up") else a
    shards = {n: flip(n, w[n][0]).astype(BF16) for n in BIG}
    shards["conv_w"] = jnp.pad(conv_w[0], ((0, SUBLANES - 3), (0, 0)))
    c_arr = jnp.reshape(lax.axis_index("c"), (1,)).astype(jnp.int32)
    me = _shard_of((lax.axis_index("x"), lax.axis_index("y")))
    me_arr = jnp.reshape(me, (1,)).astype(jnp.int32)
    dx, pairs, conv_total, packed_total = _device_step(
        x[0], loss_target[0], {n: w[n] for n in SMALL}, shards, c_arr, me_arr)

    grad = _unpack_small(packed_total)
    loss_total = grad.pop("loss")[0]
    grad["conv_w"] = conv_total[:3][None]
    small_names = ("conv_w",) + SMALL
    swap = lambda n, a: a.transpose(0, 1, 3, 2) if n in ("ssm_b_re", "ssm_b_im") else a
    ds, nms, nvs = _adam_small(*([swap(n, d[n]) for n in small_names] for d in (grad, w, m, v)))
    delta, new_m, new_v = {}, {}, {}
    for i, n in enumerate(small_names):
        delta[n], new_m[n], new_v[n] = swap(n, ds[i]), swap(n, nms[i]), swap(n, nvs[i])
    updated = _per_shape(
        lambda *a: _adam_big(*a, c_arr),
        [flip(n, w[n][0]) for n in BIG], [pairs[n][0] for n in BIG], [pairs[n][1] for n in BIG],
        [flip(n, m[n][0]) for n in BIG], [flip(n, v[n][0]) for n in BIG])
    for n, results in zip(BIG, updated):
        grad[n], delta[n], new_m[n], new_v[n] = (flip(n, r)[None] for r in results)

    return (loss_total, dx[None], *[grad[n] for n in WEIGHTS], *[delta[n] for n in WEIGHTS],
            *[new_m[n] for n in WEIGHTS], *[new_v[n] for n in WEIGHTS])
```

```python
import functools
import math

import jax
import jax.numpy as jnp
from jax import lax
from jax.experimental import pallas as pl
from jax.experimental.pallas import tpu as pltpu

F32 = jnp.float32
BF16 = jnp.bfloat16

D_MODEL = 1024
IN_COLS = 4096
SSM_W = 512
N_GROUPS = 32
N_STATE = 64
GROUP_C = 16
STATE_COLS = N_GROUPS * N_STATE
STRIP = 512
N_STRIPS = STATE_COLS // STRIP
FFN_SHARD = 704
N_SHARDS = 4
ALPHA = 2.0 ** 0.25
LN_EPS = 1e-5
GELU_K = math.sqrt(2.0 / math.pi)
GELU_C = 0.044715

ADAM_LR = 0.001
ADAM_B1 = 0.9
ADAM_B2 = 0.999
ADAM_EPS = 1e-08
ADAM_WD = 0.01
ADAM_STEP = 10

V7X_VMEM_BYTES = 64 * 1024 * 1024
VMEM_LIMIT = V7X_VMEM_BYTES - 8 * 1024 * 1024
SUBLANES = 8
N_POWERS = 128

MESH = pl.DeviceIdType.MESH
ANY = pl.BlockSpec(memory_space=pl.ANY)


def _dot(a, b):
    return jnp.dot(a.astype(BF16), b.astype(BF16), preferred_element_type=F32)


def _dot_t(a, b):
    return lax.dot_general(a.astype(BF16), b.astype(BF16), (((1,), (1,)), ((), ())),
                           preferred_element_type=F32)


def _tdot(a, b):
    return lax.dot_general(a.astype(BF16), b.astype(BF16), (((0,), (0,)), ((), ())),
                           preferred_element_type=F32)


def _sigmoid(v):
    return 1.0 / (1.0 + jnp.exp(-v))


def _split3(v):
    hi = v.astype(BF16)
    r1 = v - hi.astype(F32)
    mid = r1.astype(BF16)
    lo = (r1 - mid.astype(F32)).astype(BF16)
    return hi, mid, lo


def _exact_dot(v, sel):
    hi, mid, lo = _split3(v)
    return (jnp.dot(hi, sel, preferred_element_type=F32)
            + jnp.dot(mid, sel, preferred_element_type=F32)
            + jnp.dot(lo, sel, preferred_element_type=F32))


def _const(shape):
    nd = len(shape)
    return pl.BlockSpec(shape, lambda *_: (0,) * nd)


def _params(sem, vmem=VMEM_LIMIT):
    return pltpu.CompilerParams(dimension_semantics=sem, vmem_limit_bytes=vmem)


def _gelu_parts(v):
    inner = GELU_K * (v + GELU_C * v * v * v)
    t = jnp.tanh(inner)
    g = 0.5 * v * (1.0 + t)
    dg = 0.5 * (1.0 + t) + 0.5 * v * (1.0 - t * t) * GELU_K * (1.0 + 3.0 * GELU_C * v * v)
    return g, dg


class _Comm:
    def __init__(self, inputs, out_shape, sems, copies, aliased=False):
        self.inputs, self.out_shape, self.sems = list(inputs), tuple(out_shape), list(sems)
        self.copies, self.aliased = copies, aliased


def _launch(body, comms, *, name, grid, in_specs, out_specs, out_shape, scratch_shapes=(), aliases=None, sem=None):
    comms = list(comms)
    n_in, n_out, n_scr = len(in_specs), len(out_specs), len(scratch_shapes)
    aliases = dict(aliases or {})
    layout = []
    p_in, p_out, p_sem = n_in, n_out, 0
    for cm in comms:
        layout.append((p_in, p_out, p_sem))
        if cm.aliased:
            for i in range(len(cm.inputs)):
                aliases[p_in + i] = p_out + i
        p_in, p_out, p_sem = p_in + len(cm.inputs), p_out + len(cm.out_shape), p_sem + len(cm.sems)
    tot_in, tot_out = p_in, p_out

    def fused(*refs):
        ins, outs = refs[:tot_in], refs[tot_in:tot_in + tot_out]
        scr = refs[tot_in + tot_out:tot_in + tot_out + n_scr]
        sems = refs[tot_in + tot_out + n_scr:]

        def descriptors():
            out = []
            for cm, (a, b, s) in zip(comms, layout):
                out += cm.copies(ins[a:a + len(cm.inputs)], outs[b:b + len(cm.out_shape)], sems[s:s + len(cm.sems)])
            return out

        steps = [pl.program_id(d) for d in range(len(grid))]
        first = functools.reduce(jnp.logical_and, [s == 0 for s in steps]) if grid else None
        last = functools.reduce(jnp.logical_and, [s == g - 1 for s, g in zip(steps, grid)]) if grid else None

        def start():
            for cp in descriptors():
                cp.start()

        def finish():
            for cp in descriptors():
                cp.wait()

        if comms:
            pl.when(first)(start) if grid else start()
        if body is not None:
            body(*ins[:n_in], *outs[:n_out], *scr)
        if comms:
            pl.when(last)(finish) if grid else finish()

    specs_in = list(in_specs) + [ANY] * (tot_in - n_in)
    specs_out = tuple(out_specs) + (ANY,) * (tot_out - n_out)
    shapes = tuple(out_shape) + tuple(s for cm in comms for s in cm.out_shape)
    scratch = list(scratch_shapes) + [s for cm in comms for s in cm.sems]
    if comms or sem is None:
        sem = ("arbitrary",) * len(grid)
    kwargs = dict(grid=grid) if grid else {}
    call = pl.pallas_call(fused, name=name, out_shape=shapes, in_specs=specs_in, out_specs=specs_out,
                          scratch_shapes=scratch, input_output_aliases=aliases,
                          compiler_params=_params(sem) if grid else None, **kwargs)

    def run(*args):
        out = call(*args, *(a for cm in comms for a in cm.inputs))
        results, rest = out[:n_out], out[n_out:]
        per_comm = []
        for cm in comms:
            per_comm.append(rest[:len(cm.out_shape)])
            rest = rest[len(cm.out_shape):]
        return results, per_comm

    return run


def _ssm_discretise(lr, li, ldt, lr16, li16, ldt16, brt, bit):
    def lam_bar(lr_, li_, ldt_):
        dt = jnp.exp(ldt_)
        mag = jnp.exp(lr_ * dt)
        return mag * jnp.cos(li_ * dt), mag * jnp.sin(li_ * dt)

    lb_re, lb_im = lam_bar(lr, li, ldt)
    l16_re, l16_im = lam_bar(lr16, li16, ldt16)
    den = lr16 * lr16 + li16 * li16
    num_re = l16_re - 1.0
    fr = (num_re * lr16 + l16_im * li16) / den
    fi = (l16_im * lr16 - num_re * li16) / den
    bb_re = fr * brt - fi * bit
    bb_im = fr * bit + fi * brt
    return lb_re, lb_im, bb_re, bb_im


def _strip_selectors():
    p = lax.broadcasted_iota(jnp.int32, (N_STATE, STRIP), 0)
    col = lax.broadcasted_iota(jnp.int32, (N_STATE, STRIP), 1)
    rep = ((col & (N_STATE - 1)) == p).astype(BF16)
    row = lax.broadcasted_iota(jnp.int32, (SSM_W, STRIP), 0)
    col2 = lax.broadcasted_iota(jnp.int32, (SSM_W, STRIP), 1)
    mask = (((row >> 4) & 7) == (col2 >> 6))
    return rep, mask


def _ssm_prepare(lr, li, ldt, lr16, li16, ldt16, brt, bit, cre, cim):
    def body(lr_r, li_r, ldt_r, lr16_r, li16_r, ldt16_r, brt_r, bit_r, cre_r, cim_r,
             pwr_r, pwi_r, bbr_r, bbi_r, ctr_r, cti_r):
        lb_re, lb_im, bb_re, bb_im = _ssm_discretise(
            lr_r[...], li_r[...], ldt_r[...], lr16_r[...], li16_r[...], ldt16_r[...], brt_r[...], bit_r[...])
        pr, pi_ = lb_re, lb_im
        pwr_r[0] = pr
        pwi_r[0] = pi_
        for k in range(1, N_POWERS):
            pr, pi_ = pr * lb_re - pi_ * lb_im, pr * lb_im + pi_ * lb_re
            pwr_r[k] = pr
            pwi_r[k] = pi_
        rep, mask = _strip_selectors()
        for src, dst in ((bb_re, bbr_r), (bb_im, bbi_r), (cre_r[...], ctr_r), (cim_r[...], cti_r)):
            wide = jnp.dot(src.astype(BF16), rep, preferred_element_type=F32)
            dst[...] = jnp.where(mask, wide, 0.0).astype(BF16)

    vm = pl.BlockSpec(memory_space=pltpu.VMEM)
    return pl.pallas_call(
        body, name="ssm_prepare",
        out_shape=(jax.ShapeDtypeStruct((N_POWERS, N_GROUPS, N_STATE), F32),) * 2
        + (jax.ShapeDtypeStruct((SSM_W, STRIP), BF16),) * 4,
        in_specs=[vm] * 10, out_specs=(vm,) * 6,
    )(lr, li, ldt, lr16, li16, ldt16, brt, bit, cre, cim)


def _scan_tables(pwr, pwi):
    pr = pwr.reshape(N_POWERS, STATE_COLS)
    pi_ = pwi.reshape(N_POWERS, STATE_COLS)
    rows8 = lambda v: jnp.broadcast_to(v[None], (SUBLANES, STATE_COLS))
    tab_a = jnp.stack([rows8(pr[0]), rows8(pi_[0]), rows8(pr[-1]), rows8(pi_[-1])])
    tab_p = jnp.stack([jnp.repeat(pr, SUBLANES, axis=0), jnp.repeat(pi_, SUBLANES, axis=0)])
    return tab_a, tab_p


def _ssm_param_grads(lr, li, ldt, lr16, li16, ldt16, brt, bit, dlbr, dlbi, dbbr, dbbi, dctr, dcti):
    def body(lr_r, li_r, ldt_r, lr16_r, li16_r, ldt16_r, brt_r, bit_r,
             dlbr_r, dlbi_r, dbbr_r, dbbi_r, dctr_r, dcti_r,
             glr_r, gli_r, gldt_r, gbrt_r, gbit_r, gcre_r, gcim_r):
        rep, mask = _strip_selectors()

        def fold(acc):
            return sum(lax.dot_general(t, rep, (((1,), (1,)), ((), ())), preferred_element_type=F32)
                       for t in _split3(jnp.where(mask, acc, 0.0)))

        g_lb_re = jnp.sum(dlbr_r[...], axis=0)
        g_lb_im = jnp.sum(dlbi_r[...], axis=0)
        g_bb_re = fold(dbbr_r[...])
        g_bb_im = fold(dbbi_r[...])
        gcre_r[...] = fold(dctr_r[...])
        gcim_r[...] = fold(dcti_r[...])
        prim = (lr_r[...], li_r[...], ldt_r[...], lr16_r[...], li16_r[...], ldt16_r[...], brt_r[...], bit_r[...])
        _, vjp = jax.vjp(_ssm_discretise, *prim)
        g_lr, g_li, g_ldt, g_lr16, g_li16, g_ldt16, g_brt, g_bit = vjp((g_lb_re, g_lb_im, g_bb_re, g_bb_im))
        grp = lax.broadcasted_iota(jnp.int32, (N_GROUPS, SSM_W), 0)
        rw = lax.broadcasted_iota(jnp.int32, (N_GROUPS, SSM_W), 1)
        gsum = ((rw >> 4) == grp).astype(BF16)

        def group_sum(v):
            return sum(jnp.dot(gsum, t, preferred_element_type=F32) for t in _split3(v))

        glr_r[...] = g_lr + group_sum(g_lr16)
        gli_r[...] = g_li + group_sum(g_li16)
        gldt_r[...] = g_ldt + jnp.sum(group_sum(g_ldt16), axis=1, keepdims=True)
        gbrt_r[...] = g_brt
        gbit_r[...] = g_bit

    vm = pl.BlockSpec(memory_space=pltpu.VMEM)
    gp = jax.ShapeDtypeStruct((N_GROUPS, N_STATE), F32)
    gb = jax.ShapeDtypeStruct((SSM_W, N_STATE), F32)
    return pl.pallas_call(
        body, name="ssm_param_grads",
        out_shape=(gp, gp, jax.ShapeDtypeStruct((N_GROUPS, 1), F32), gb, gb, gb, gb),
        in_specs=[vm] * 14, out_specs=(vm,) * 7,
    )(lr, li, ldt, lr16, li16, ldt16, brt, bit, dlbr, dlbi, dbbr, dbbi, dctr, dcti)


def _in_proj(x, w_in_st, b_in, comms=()):
    t = x.shape[0]
    tm = 512

    def body(x_r, w_r, b_r, o_r):
        xb = x_r[...].astype(BF16)
        for j in range(N_SHARDS):
            cols = slice(D_MODEL * j, D_MODEL * (j + 1))
            o_r[:, cols] = jnp.dot(xb, w_r[j], preferred_element_type=F32) + b_r[:, cols]

    (proj,), sent = _launch(
        body, comms, name="in_proj", grid=(t // tm,),
        out_shape=(jax.ShapeDtypeStruct((t, IN_COLS), F32),),
        in_specs=[pl.BlockSpec((tm, D_MODEL), lambda i: (i, 0)), _const((N_SHARDS, D_MODEL, D_MODEL)),
                  _const((1, IN_COLS))],
        out_specs=(pl.BlockSpec((tm, IN_COLS), lambda i: (i, 0)),),
        sem=("parallel",),
    )(x, w_in_st, b_in)
    return proj, sent


def _cmul_add(xr, xi, mr, mi, sr, si):
    return xr + (mr * sr - mi * si), xi + (mr * si + mi * sr)


SCAN_STEPS = N_POWERS
SCAN_CHUNK = SUBLANES * SCAN_STEPS


def _interleave(src_r, dst_r):
    for step in range(SCAN_STEPS):
        dst_r[SUBLANES * step:SUBLANES * (step + 1), :] = src_r[pl.ds(step, SUBLANES, stride=SCAN_STEPS), :]


def _deinterleave(src_r, dst_r):
    for step in range(SCAN_STEPS):
        dst_r[pl.ds(step, SUBLANES, stride=SCAN_STEPS), :] = src_r[SUBLANES * step:SUBLANES * (step + 1), :]


def _step_rows(step):
    return pl.ds(pl.multiple_of(step * SUBLANES, SUBLANES), SUBLANES)


def _scan_steps(body, init, by=4):
    def trip(t, carry):
        for u in range(by):
            carry = body(t * by + u, carry)
        return carry

    return lax.fori_loop(0, SCAN_STEPS // by, trip, init)


def _segment_states(first_r, first_i, ends_r, ends_i, a64_r, a64_i, order):
    row = lax.broadcasted_iota(jnp.int32, ends_r.shape, 0)
    cur_r, cur_i = first_r, first_i
    ent_r = jnp.zeros_like(ends_r)
    ent_i = jnp.zeros_like(ends_i)
    for s in order:
        ent_r = jnp.where(row == s, jnp.broadcast_to(cur_r, ends_r.shape), ent_r)
        ent_i = jnp.where(row == s, jnp.broadcast_to(cur_i, ends_i.shape), ent_i)
        cur_r, cur_i = _cmul_add(ends_r[s:s + 1, :], ends_i[s:s + 1, :], a64_r, a64_i, cur_r, cur_i)
    return ent_r, ent_i, cur_r, cur_i


def _ssm_forward(proj, bbr, bbi, ctr, cti, d_skip, tab_a, tab_p, comms=(), tc=SCAN_CHUNK):
    t = proj.shape[0]

    def body(u_r, bbr_r, bbi_r, ctr_r, cti_r, d_r, ta_r, tp_r, xsr_r, xsi_r, y_r, ui_s, yi_s, car_r, car_i):
        @pl.when(pl.program_id(1) == 0)
        def _():
            car_r[...] = jnp.zeros_like(car_r)
            car_i[...] = jnp.zeros_like(car_i)

        _interleave(u_r, ui_s)
        u = ui_s[...]
        xsr_r[...] = _dot(u, bbr_r[...])
        xsi_r[...] = _dot(u, bbi_r[...])
        a_r, a_i = ta_r[0], ta_r[1]

        def local(step, carry):
            rows = _step_rows(step)
            xr, xi = _cmul_add(xsr_r[rows, :], xsi_r[rows, :], a_r, a_i, *carry)
            xsr_r[rows, :] = xr
            xsi_r[rows, :] = xi
            return xr, xi

        zero = jnp.zeros((SUBLANES, STRIP), F32)
        ends_r, ends_i = _scan_steps(local, (zero, zero))
        ent_r, ent_i, out_r, out_i = _segment_states(
            car_r[...], car_i[...], ends_r, ends_i, ta_r[2, 0:1, :], ta_r[3, 0:1, :], range(SUBLANES))
        car_r[...] = out_r
        car_i[...] = out_i

        def entering(step, _):
            rows = _step_rows(step)
            xr, xi = _cmul_add(xsr_r[rows, :], xsi_r[rows, :], tp_r[0, rows, :], tp_r[1, rows, :], ent_r, ent_i)
            xsr_r[rows, :] = xr
            xsi_r[rows, :] = xi
            return 0

        _scan_steps(entering, 0)
        yi_s[...] = _dot_t(xsr_r[...], ctr_r[...]) - _dot_t(xsi_r[...], cti_r[...]) + d_r[...] * u
        _deinterleave(yi_s, y_r)

    strip_mat = pl.BlockSpec((128, STRIP), lambda j, k: (j, 0))
    states = pl.BlockSpec((tc, STRIP), lambda j, k: (k, j))
    return _launch(
        body, comms, name="ssm_forward", grid=(N_STRIPS, t // tc),
        out_shape=(jax.ShapeDtypeStruct((t, STATE_COLS), F32), jax.ShapeDtypeStruct((t, STATE_COLS), F32),
                   jax.ShapeDtypeStruct((t, SSM_W), F32)),
        in_specs=[pl.BlockSpec((tc, 128), lambda j, k: (k, j)),
                  strip_mat, strip_mat, strip_mat, strip_mat,
                  pl.BlockSpec((1, 128), lambda j, k: (0, j)),
                  pl.BlockSpec((4, SUBLANES, STRIP), lambda j, k: (0, 0, j)),
                  pl.BlockSpec((2, tc, STRIP), lambda j, k: (0, 0, j))],
        out_specs=(states, states, pl.BlockSpec((tc, 128), lambda j, k: (k, j))),
        scratch_shapes=[pltpu.VMEM((tc, 128), F32), pltpu.VMEM((tc, 128), F32),
                        pltpu.VMEM((1, STRIP), F32), pltpu.VMEM((1, STRIP), F32)],
        sem=("parallel", "arbitrary"),
    )(proj, bbr, bbi, ctr, cti, d_skip, tab_a, tab_p)


def _shift_down(v, prev, n):
    row = lax.broadcasted_iota(jnp.int32, v.shape, 0)
    out = pltpu.roll(v, n, 0)
    for r in range(n):
        src = prev[SUBLANES - n + r:SUBLANES - n + r + 1, :]
        out = jnp.where(row == r, jnp.broadcast_to(src, v.shape), out)
    return out


def _shift_up(v, nxt, n):
    rows = v.shape[0]
    row = lax.broadcasted_iota(jnp.int32, v.shape, 0)
    out = pltpu.roll(v, rows - n, 0)
    for r in range(n):
        src = nxt[r:r + 1, :]
        out = jnp.where(row == rows - n + r, jnp.broadcast_to(src, v.shape), out)
    return out


def _conv3(q, q_prev, w):
    return w[2:3, :] * q + w[1:2, :] * _shift_down(q, q_prev, 1) + w[0:1, :] * _shift_down(q, q_prev, 2)


def _mixer_forward(x, proj, ya0, glu_w, glu_b, wso_st, conv_w8, wco_st, w_o, comms=(), tm=256):
    t = x.shape[0]
    hb = tm // SUBLANES

    def body(x_r, ya0_r, h_r, cg_r, bg_r, ga_r, gb_r, hp_r, cgp_r,
             glu_w_r, glu_b_r, wso_r, cw_r, wco_r, wo_r, xh_r, rstd_r, ya_r, yb_r):
        i = pl.program_id(0)
        g, _ = _gelu_parts(ya0_r[...])
        ya1 = g * _sigmoid(_dot(g, glu_w_r[...]) + glu_b_r[...])
        q = cg_r[...] * h_r[...]
        q_prev = jnp.where(i > 0, cgp_r[...] * hp_r[...], 0.0)
        yb0 = bg_r[...] * _conv3(q, q_prev, cw_r[...])
        for j in range(N_SHARDS):
            ya_r[:, 256 * j:256 * (j + 1)] = _dot(ya1, wso_r[j])
            yb_r[:, 256 * j:256 * (j + 1)] = _dot(yb0, wco_r[j])
        merged = _sigmoid(ga_r[...]) * ya_r[...] + _sigmoid(gb_r[...]) * yb_r[...]
        r1 = ALPHA * x_r[...] + _dot(merged, wo_r[...])
        mu = jnp.mean(r1, axis=-1, keepdims=True)
        cen = r1 - mu
        rstd = lax.rsqrt(jnp.mean(cen * cen, axis=-1, keepdims=True) + LN_EPS)
        xh_r[...] = cen * rstd
        rstd_r[...] = rstd

    def col(w, c):
        return pl.BlockSpec((tm, w), lambda i: (i, c))

    def prev(c):
        return pl.BlockSpec((SUBLANES, SSM_W), lambda i: (jnp.maximum(i * hb - 1, 0), c))

    return _launch(
        body, comms, name="mixer_forward", grid=(t // tm,),
        out_shape=(jax.ShapeDtypeStruct((t, D_MODEL), F32), jax.ShapeDtypeStruct((t, 1), F32),
                   jax.ShapeDtypeStruct((t, D_MODEL), F32), jax.ShapeDtypeStruct((t, D_MODEL), F32)),
        in_specs=[col(D_MODEL, 0), col(SSM_W, 0), col(SSM_W, 1), col(SSM_W, 2), col(SSM_W, 3),
                  col(D_MODEL, 2), col(D_MODEL, 3), prev(1), prev(2),
                  _const((SSM_W, SSM_W)), _const((1, SSM_W)), _const((N_SHARDS, SSM_W, 256)),
                  _const((SUBLANES, SSM_W)), _const((N_SHARDS, SSM_W, 256)), _const((D_MODEL, D_MODEL))],
        out_specs=(col(D_MODEL, 0), pl.BlockSpec((tm, 1), lambda i: (i, 0)), col(D_MODEL, 0), col(D_MODEL, 0)),
        sem=("parallel",),
    )(x, ya0, proj, proj, proj, proj, proj, proj, proj, glu_w, glu_b, wso_st, conv_w8, wco_st, w_o)


def _layer_norm_bwd(dxhat, xhat, rstd):
    m1 = jnp.mean(dxhat, axis=-1, keepdims=True)
    m2 = jnp.mean(dxhat * xhat, axis=-1, keepdims=True)
    return rstd * (dxhat - m1 - xhat * m2)


def _ffn_step(xhat1, rstd1, target, ln1_g, ln1_b, ln2_g, ln2_b, wg_st, wu_st, wd_st, tm=256):
    t = xhat1.shape[0]

    def body(xh_r, rstd_r, tgt_r, g1_r, b1_r, g2_r, b2_r, wg_r, wu_r, wd_r,
             loss_r, dr1_r, x1b_r, dr2b_r, hid_r, dhg_r, dhu_r, dg2_r, db2_r, dg1_r, db1_r,
             hg_s, hu_s):
        @pl.when(pl.program_id(0) == 0)
        def _():
            for r in (loss_r, dg2_r, db2_r, dg1_r, db1_r):
                r[...] = jnp.zeros_like(r)

        xhat1_v = xh_r[...]
        x1 = xhat1_v * g1_r[...] + b1_r[...]
        x1b = x1.astype(BF16)
        x1b_r[...] = x1b
        ffn = jnp.zeros((tm, D_MODEL), F32)
        for j in range(N_SHARDS):
            hg = lax.dot_general(x1b, wg_r[j], (((1,), (1,)), ((), ())), preferred_element_type=F32)
            hu = lax.dot_general(x1b, wu_r[j], (((1,), (1,)), ((), ())), preferred_element_type=F32)
            hg_s[j] = hg
            hu_s[j] = hu
            hid = (hg * _sigmoid(hg) * hu).astype(BF16)
            hid_r[j] = hid
            ffn = ffn + jnp.dot(hid, wd_r[j], preferred_element_type=F32)
        r2 = ALPHA * x1 + ffn
        mu = jnp.mean(r2, axis=-1, keepdims=True)
        cen = r2 - mu
        rstd2 = lax.rsqrt(jnp.mean(cen * cen, axis=-1, keepdims=True) + LN_EPS)
        xhat2 = cen * rstd2
        diff = (xhat2 * g2_r[...] + b2_r[...]) - tgt_r[...]
        loss_r[...] += 0.5 * jnp.sum(jnp.mean(diff * diff, axis=-1, keepdims=True), axis=0, keepdims=True)
        dy = diff * (1.0 / D_MODEL)
        dg2_r[...] += jnp.sum(dy * xhat2, axis=0, keepdims=True)
        db2_r[...] += jnp.sum(dy, axis=0, keepdims=True)
        dr2 = _layer_norm_bwd(dy * g2_r[...], xhat2, rstd2)
        dr2b = dr2.astype(BF16)
        dr2b_r[...] = dr2b
        dx1 = ALPHA * dr2
        for j in range(N_SHARDS):
            dhid = lax.dot_general(dr2b, wd_r[j], (((1,), (1,)), ((), ())), preferred_element_type=F32)
            hg = hg_s[j]
            hu = hu_s[j]
            sg = _sigmoid(hg)
            dhu = (dhid * (hg * sg)).astype(BF16)
            dhg = (dhid * hu * (sg * (1.0 + hg * (1.0 - sg)))).astype(BF16)
            dhg_r[j] = dhg
            dhu_r[j] = dhu
            dx1 = dx1 + jnp.dot(dhg, wg_r[j], preferred_element_type=F32)
            dx1 = dx1 + jnp.dot(dhu, wu_r[j], preferred_element_type=F32)
        dg1_r[...] += jnp.sum(dx1 * xhat1_v, axis=0, keepdims=True)
        db1_r[...] += jnp.sum(dx1, axis=0, keepdims=True)
        dr1_r[...] = _layer_norm_bwd(dx1 * g1_r[...], xhat1_v, rstd_r[...])

    tile = pl.BlockSpec((tm, D_MODEL), lambda i: (i, 0))
    hidden = pl.BlockSpec((N_SHARDS, tm, FFN_SHARD), lambda i: (0, i, 0))
    vec = _const((1, D_MODEL))
    hid_shape = jax.ShapeDtypeStruct((N_SHARDS, t, FFN_SHARD), BF16)
    vec_shape = jax.ShapeDtypeStruct((1, D_MODEL), F32)
    return pl.pallas_call(
        body, name="ffn_step", grid=(t // tm,),
        out_shape=(jax.ShapeDtypeStruct((1, 1), F32), jax.ShapeDtypeStruct((t, D_MODEL), F32),
                   jax.ShapeDtypeStruct((t, D_MODEL), BF16), jax.ShapeDtypeStruct((t, D_MODEL), BF16),
                   hid_shape, hid_shape, hid_shape, vec_shape, vec_shape, vec_shape, vec_shape),
        in_specs=[tile, pl.BlockSpec((tm, 1), lambda i: (i, 0)), tile, vec, vec, vec, vec,
                  _const((N_SHARDS, FFN_SHARD, D_MODEL)), _const((N_SHARDS, FFN_SHARD, D_MODEL)),
                  _const((N_SHARDS, FFN_SHARD, D_MODEL))],
        out_specs=(_const((1, 1)), tile, tile, tile, hidden, hidden, hidden, vec, vec, vec, vec),
        scratch_shapes=[pltpu.VMEM((N_SHARDS, tm, FFN_SHARD), F32), pltpu.VMEM((N_SHARDS, tm, FFN_SHARD), F32)],
        compiler_params=_params(("arbitrary",)),
    )(xhat1, rstd1, target, ln1_g, ln1_b, ln2_g, ln2_b, wg_st, wu_st, wd_st)


def _ffn_weight_grads(x1b, dr2b, hid, dhg, dhu, tk=2048):
    t = x1b.shape[0]

    def body(x_r, dr_r, hid_r, dhg_r, dhu_r, gwg_r, gwu_r, gwd_r):
        @pl.when(pl.program_id(1) == 0)
        def _():
            for r in (gwg_r, gwu_r, gwd_r):
                r[...] = jnp.zeros_like(r)

        gwg_r[0] += _tdot(dhg_r[0], x_r[...])
        gwu_r[0] += _tdot(dhu_r[0], x_r[...])
        gwd_r[0] += _tdot(hid_r[0], dr_r[...])

    tile = pl.BlockSpec((tk, D_MODEL), lambda j, k: (k, 0))
    hidden = pl.BlockSpec((1, tk, FFN_SHARD), lambda j, k: (j, k, 0))
    row = pl.BlockSpec((1, FFN_SHARD, D_MODEL), lambda j, k: (j, 0, 0))
    return pl.pallas_call(
        body, name="ffn_weight_grads", grid=(N_SHARDS, t // tk),
        out_shape=(jax.ShapeDtypeStruct((N_SHARDS, FFN_SHARD, D_MODEL), F32),) * 3,
        in_specs=[tile, tile, hidden, hidden, hidden],
        out_specs=(row, row, row),
        compiler_params=_params(("parallel", "arbitrary")),
    )(x1b, dr2b, hid, dhg, dhu)


def _mixer_backward(dr1, proj, ya0, ya, yb, glu_w, glu_b, wso_st, conv_w8, wco_st, w_o, comms=(), tm=256):
    t = dr1.shape[0]
    hb = tm // SUBLANES
    last_block = t // SUBLANES - 1

    def body(dr1_r, dr1n_r, ya0_r, ya_r, yb_r, h_r, cg_r, bg_r, ga_r, gb_r, hp_r, cgp_r, bgn_r, gbn_r,
             glu_w_r, glu_b_r, wso_r, cw_r, wco_r, wo_r,
             dya0_r, dproj_r, dbias_r, gwo_r, gwso_r, gwco_r, gglu_w_r, gglu_b_r, gconv_r):
        i = pl.program_id(0)

        @pl.when(i == 0)
        def _():
            for r in (dbias_r, gwo_r, gwso_r, gwco_r, gglu_w_r, gglu_b_r, gconv_r):
                r[...] = jnp.zeros_like(r)

        dr1_v = dr1_r[...]
        dmerged = _dot_t(dr1_v, wo_r[...])
        sa = _sigmoid(ga_r[...])
        sb = _sigmoid(gb_r[...])
        ya_v = ya_r[...]
        yb_v = yb_r[...]
        gwo_r[...] += _tdot(sa * ya_v + sb * yb_v, dr1_v)
        dya = dmerged * sa
        dyb = dmerged * sb
        dga = dmerged * ya_v * (sa * (1.0 - sa))
        dgb = dmerged * yb_v * (sb * (1.0 - sb))

        g, gelu_grad = _gelu_parts(ya0_r[...])
        s1 = _sigmoid(_dot(g, glu_w_r[...]) + glu_b_r[...])
        ya1 = g * s1
        dya1 = jnp.zeros((tm, SSM_W), F32)
        for j in range(N_SHARDS):
            dya_j = dya[:, 256 * j:256 * (j + 1)]
            gwso_r[j] += _tdot(ya1, dya_j)
            dya1 = dya1 + _dot_t(dya_j, wso_r[j])
        dz1 = dya1 * g * (s1 * (1.0 - s1))
        gglu_b_r[...] += jnp.sum(dz1, axis=0, keepdims=True)
        gglu_w_r[...] += _tdot(g, dz1)
        dya0_r[...] = (dya1 * s1 + _dot_t(dz1, glu_w_r[...])) * gelu_grad

        cw = cw_r[...]
        h = h_r[...]
        cg = cg_r[...]
        bg = bg_r[...]
        q = cg * h
        q_prev = jnp.where(i > 0, cgp_r[...] * hp_r[...], 0.0)
        q1 = _shift_down(q, q_prev, 1)
        q2 = _shift_down(q, q_prev, 2)
        z = cw[2:3, :] * q + cw[1:2, :] * q1 + cw[0:1, :] * q2
        yb0 = bg * z
        dyb0 = jnp.zeros((tm, SSM_W), F32)
        for j in range(N_SHARDS):
            dyb_j = dyb[:, 256 * j:256 * (j + 1)]
            gwco_r[j] += _tdot(yb0, dyb_j)
            dyb0 = dyb0 + _dot_t(dyb_j, wco_r[j])
        dbg = dyb0 * z
        dz = dyb0 * bg
        dyb_n = _dot_t(dr1n_r[...], wo_r[...]) * _sigmoid(gbn_r[...])
        dyb0_n = jnp.zeros((SUBLANES, SSM_W), F32)
        for j in range(N_SHARDS):
            dyb0_n = dyb0_n + _dot_t(dyb_n[:, 256 * j:256 * (j + 1)], wco_r[j])
        dz_next = jnp.where(i < pl.num_programs(0) - 1, dyb0_n * bgn_r[...], 0.0)
        dq = cw[2:3, :] * dz + cw[1:2, :] * _shift_up(dz, dz_next, 1) + cw[0:1, :] * _shift_up(dz, dz_next, 2)
        gconv_r[0:1, :] += jnp.sum(dz * q2, axis=0, keepdims=True)
        gconv_r[1:2, :] += jnp.sum(dz * q1, axis=0, keepdims=True)
        gconv_r[2:3, :] += jnp.sum(dz * q, axis=0, keepdims=True)
        dh = dq * cg
        dcg = dq * h

        dproj_r[:, 0:512] = jnp.zeros((tm, SSM_W), BF16)
        pieces = ((512, dh), (1024, dcg), (1536, dbg), (2048, dga), (3072, dgb))
        for off, val in pieces:
            w = val.shape[1]
            dproj_r[:, off:off + w] = val.astype(BF16)
            dbias_r[:, off:off + w] += jnp.sum(val, axis=0, keepdims=True)

    def col(w, c):
        return pl.BlockSpec((tm, w), lambda i: (i, c))

    def prev(c):
        return pl.BlockSpec((SUBLANES, SSM_W), lambda i: (jnp.maximum(i * hb - 1, 0), c))

    def nxt(w, c):
        return pl.BlockSpec((SUBLANES, w), lambda i: (jnp.minimum((i + 1) * hb, last_block), c))

    sh = jax.ShapeDtypeStruct
    return _launch(
        body, comms, name="mixer_backward", grid=(t // tm,),
        out_shape=(sh((t, SSM_W), F32), sh((t, IN_COLS), BF16), sh((1, IN_COLS), F32),
                   sh((D_MODEL, D_MODEL), F32), sh((N_SHARDS, SSM_W, 256), F32), sh((N_SHARDS, SSM_W, 256), F32),
                   sh((SSM_W, SSM_W), F32), sh((1, SSM_W), F32), sh((SUBLANES, SSM_W), F32)),
        in_specs=[col(D_MODEL, 0), nxt(D_MODEL, 0), col(SSM_W, 0), col(D_MODEL, 0), col(D_MODEL, 0),
                  col(SSM_W, 1), col(SSM_W, 2), col(SSM_W, 3), col(D_MODEL, 2), col(D_MODEL, 3),
                  prev(1), prev(2), nxt(SSM_W, 3), nxt(D_MODEL, 3),
                  _const((SSM_W, SSM_W)), _const((1, SSM_W)), _const((N_SHARDS, SSM_W, 256)),
                  _const((SUBLANES, SSM_W)), _const((N_SHARDS, SSM_W, 256)), _const((D_MODEL, D_MODEL))],
        out_specs=(col(SSM_W, 0), col(IN_COLS, 0), _const((1, IN_COLS)),
                   _const((D_MODEL, D_MODEL)), _const((N_SHARDS, SSM_W, 256)), _const((N_SHARDS, SSM_W, 256)),
                   _const((SSM_W, SSM_W)), _const((1, SSM_W)), _const((SUBLANES, SSM_W))),
        sem=("arbitrary",),
    )(dr1, dr1, ya0, ya, yb, proj, proj, proj, proj, proj, proj, proj, proj, proj,
      glu_w, glu_b, wso_st, conv_w8, wco_st, w_o)


def _cmulc_add(xr, xi, mr, mi, sr, si):
    return xr + (mr * sr + mi * si), xi + (mr * si - mi * sr)


def _ssm_backward(dya0, proj, xsr, xsi, bbr, bbi, ctr, cti, d_skip, tab_a, tab_p, dproj, comms=(), tc=SCAN_CHUNK):
    t = proj.shape[0]
    nk = t // tc

    def body(dy_r, u_r, xsr_r, xsi_r, bbr_r, bbi_r, ctr_r, cti_r, d_r, ta_r, tp_r, dproj_any,
             du_r, dus_r, gbbr_r, gbbi_r, gctr_r, gcti_r, glbr_r, glbi_r, gd_r,
             gr_s, gi_s, dyi_s, ui_s, dui_s, dun_s, car_r, car_i):
        del dproj_any

        @pl.when(pl.program_id(1) == 0)
        def _():
            for r in (car_r, car_i, dus_r, gbbr_r, gbbi_r, gctr_r, gcti_r, glbr_r, glbi_r, gd_r):
                r[...] = jnp.zeros_like(r)

        _interleave(dy_r, dyi_s)
        _interleave(u_r, ui_s)
        dy = dyi_s[...]
        u = ui_s[...]
        gr_s[...] = _dot(dy, ctr_r[...])
        gi_s[...] = -_dot(dy, cti_r[...])
        a_r, a_i = ta_r[0], ta_r[1]

        def local(n, carry):
            rows = _step_rows(SCAN_STEPS - 1 - n)
            gr, gi = _cmulc_add(gr_s[rows, :], gi_s[rows, :], a_r, a_i, *carry)
            gr_s[rows, :] = gr
            gi_s[rows, :] = gi
            return gr, gi

        zero = jnp.zeros((SUBLANES, STRIP), F32)
        ends_r, ends_i = _scan_steps(local, (zero, zero))
        ent_r, ent_i, out_r, out_i = _segment_states(
            car_r[...], car_i[...], ends_r, ends_i, ta_r[2, 0:1, :], -ta_r[3, 0:1, :], range(SUBLANES - 1, -1, -1))
        car_r[...] = out_r
        car_i[...] = out_i

        def entering(n, carry):
            gnr, gni, ar, ai = carry
            rows = _step_rows(SCAN_STEPS - 1 - n)
            power = _step_rows(n)
            gr, gi = _cmulc_add(gr_s[rows, :], gi_s[rows, :], tp_r[0, power, :], tp_r[1, power, :], ent_r, ent_i)
            gr_s[rows, :] = gr
            gi_s[rows, :] = gi
            xr = xsr_r[rows, :]
            xi = xsi_r[rows, :]
            return gr, gi, ar + (xr * gnr + xi * gni), ai + (xr * gni - xi * gnr)

        _, _, ar, ai = _scan_steps(entering, (ent_r, ent_i, zero, zero))
        glbr_r[...] += ar
        glbi_r[...] += ai
        gr = gr_s[...]
        gi = gi_s[...]
        dui_s[...] = _dot_t(gr, bbr_r[...]) + _dot_t(gi, bbi_r[...]) + d_r[...] * dy
        _deinterleave(dui_s, dun_s)
        du = dun_s[...]
        du_r[...] = du.astype(BF16)
        dus_r[...] += jnp.sum(du, axis=0, keepdims=True)
        gd_r[...] += jnp.sum(dy * u, axis=0, keepdims=True)
        gbbr_r[...] += _tdot(u, gr)
        gbbi_r[...] += _tdot(u, gi)
        gctr_r[...] += _tdot(dy, xsr_r[...])
        gcti_r[...] -= _tdot(dy, xsi_r[...])

    def rev(w):
        return pl.BlockSpec((tc, w), lambda j, k: (nk - 1 - k, j))

    strip_mat = pl.BlockSpec((128, STRIP), lambda j, k: (j, 0))
    vec = pl.BlockSpec((1, 128), lambda j, k: (0, j))
    lbacc = pl.BlockSpec((SUBLANES, STRIP), lambda j, k: (0, j))
    sh = jax.ShapeDtypeStruct
    return _launch(
        body, comms, name="ssm_backward", grid=(N_STRIPS, nk),
        out_shape=(sh((t, IN_COLS), BF16), sh((1, SSM_W), F32),
                   sh((SSM_W, STRIP), F32), sh((SSM_W, STRIP), F32), sh((SSM_W, STRIP), F32), sh((SSM_W, STRIP), F32),
                   sh((SUBLANES, STATE_COLS), F32), sh((SUBLANES, STATE_COLS), F32), sh((1, SSM_W), F32)),
        in_specs=[rev(128), rev(128), rev(STRIP), rev(STRIP),
                  strip_mat, strip_mat, strip_mat, strip_mat, vec,
                  pl.BlockSpec((4, SUBLANES, STRIP), lambda j, k: (0, 0, j)),
                  pl.BlockSpec((2, tc, STRIP), lambda j, k: (0, 0, j)), ANY],
        out_specs=(rev(128), vec, strip_mat, strip_mat, strip_mat, strip_mat, lbacc, lbacc, vec),
        scratch_shapes=[pltpu.VMEM((tc, STRIP), F32), pltpu.VMEM((tc, STRIP), F32)]
        + [pltpu.VMEM((tc, 128), F32)] * 4 + [pltpu.VMEM((1, STRIP), F32)] * 2,
        aliases={11: 0}, sem=("parallel", "arbitrary"),
    )(dya0, proj, xsr, xsi, bbr, bbi, ctr, cti, d_skip, tab_a, tab_p, dproj)


def _input_grad(dr1, dproj, w_in_st, comms=(), tm=512):
    t = dr1.shape[0]

    def body(dr1_r, dp_r, w_r, dx_r):
        acc = ALPHA * dr1_r[...]
        for j in range(N_SHARDS):
            acc = acc + lax.dot_general(dp_r[:, D_MODEL * j:D_MODEL * (j + 1)], w_r[j],
                                        (((1,), (1,)), ((), ())), preferred_element_type=F32)
        dx_r[...] = acc

    (dx,), sent = _launch(
        body, comms, name="input_grad", grid=(t // tm,),
        out_shape=(jax.ShapeDtypeStruct((t, D_MODEL), F32),),
        in_specs=[pl.BlockSpec((tm, D_MODEL), lambda i: (i, 0)), pl.BlockSpec((tm, IN_COLS), lambda i: (i, 0)),
                  _const((N_SHARDS, D_MODEL, D_MODEL))],
        out_specs=(pl.BlockSpec((tm, D_MODEL), lambda i: (i, 0)),),
        sem=("parallel",),
    )(dr1, dproj, w_in_st)
    return dx, sent


def _in_weight_grad(x, dproj, comms=(), tk=2048):
    t = x.shape[0]

    def body(x_r, dp_r, gw_r):
        @pl.when(pl.program_id(1) == 0)
        def _():
            gw_r[...] = jnp.zeros_like(gw_r)

        gw_r[0] += _tdot(x_r[...], dp_r[...])

    (g_w_in,), sent = _launch(
        body, comms, name="in_weight_grad", grid=(N_SHARDS, t // tk),
        out_shape=(jax.ShapeDtypeStruct((N_SHARDS, D_MODEL, D_MODEL), F32),),
        in_specs=[pl.BlockSpec((tk, D_MODEL), lambda j, k: (k, 0)), pl.BlockSpec((tk, D_MODEL), lambda j, k: (k, j))],
        out_specs=(pl.BlockSpec((1, D_MODEL, D_MODEL), lambda j, k: (j, 0, 0)),),
        sem=("parallel", "arbitrary"),
    )(x, dproj)
    return g_w_in, sent


MIXER_W = ("glu_w", "w_ssm_out", "w_conv_out", "w_o")
FFN_W = ("w_gate", "w_up", "w_down")


def _device_step(x, target, small, shards, c_arr, me_arr):
    lr, li = small["ssm_lambda_re"][0], small["ssm_lambda_im"][0]
    ldt = small["ssm_log_dt"][0][:, None]
    rep16 = lambda a: jnp.broadcast_to(a[:, None, :], (N_GROUPS, GROUP_C, a.shape[-1])).reshape(SSM_W, a.shape[-1])
    lr16, li16 = rep16(lr), rep16(li)
    ldt16 = rep16(jnp.broadcast_to(ldt, (N_GROUPS, N_STATE)))
    brt = small["ssm_b_re"][0].transpose(0, 2, 1).reshape(SSM_W, N_STATE)
    bit = small["ssm_b_im"][0].transpose(0, 2, 1).reshape(SSM_W, N_STATE)
    cre = small["ssm_c_re"][0].reshape(SSM_W, N_STATE)
    cim = small["ssm_c_im"][0].reshape(SSM_W, N_STATE)
    disc = (lr, li, ldt, lr16, li16, ldt16, brt, bit)

    pwr, pwi, bbr, bbi, ctr, cti = _ssm_prepare(*disc, cre, cim)
    tab_a, tab_p = _scan_tables(pwr, pwi)

    first_sh = [shards[n] for n in MIXER_W + FFN_W[:1]]
    second_sh = [shards[n] for n in FFN_W[1:]]
    (w_in_st,) = _gather_weights([shards["w_in"]])
    proj, (arrived,) = _in_proj(x, w_in_st, small["b_in"], comms=[_gather_ici(first_sh, [shards["conv_w"]])])
    (xsr, xsi, ya0), (second_part, first_st) = _ssm_forward(
        proj, bbr, bbi, ctr, cti, small["ssm_d"], tab_a, tab_p,
        comms=[_gather_ici(second_sh), _gather_d2d(arrived[:len(first_sh)], first_sh)])
    glu_st, wso_st, wco_st, wo_st, wg_st = (_own_slot(st, sh) for st, sh in zip(first_st, first_sh))
    conv_st = _own_slot(arrived[len(first_sh)], shards["conv_w"])
    conv_w8 = jnp.pad(conv_st[:, :3, :].transpose(1, 0, 2).reshape(3, SSM_W), ((0, SUBLANES - 3), (0, 0)))
    w_o = wo_st.reshape(D_MODEL, D_MODEL)
    glu_w = glu_st.reshape(SSM_W, SSM_W)
    (xhat1, rstd1, ya, yb), (second_st,) = _mixer_forward(
        x, proj, ya0, glu_w, small["glu_b"], wso_st, conv_w8, wco_st, w_o, comms=[_gather_d2d(second_part, second_sh)])
    wu_st, wd_st = (_own_slot(st, sh) for st, sh in zip(second_st, second_sh))
    (loss, dr1, x1b, dr2b, hid, dhg, dhu, g_ln2_g, g_ln2_b, g_ln1_g, g_ln1_b) = _ffn_step(
        xhat1, rstd1, target, small["ln1_g"], small["ln1_b"], small["ln2_g"], small["ln2_b"], wg_st, wu_st, wd_st)

    add_halves = lambda gs, rs: _per_shape(lambda a, b: _add_own_half(a, b, c_arr), list(gs), list(rs))
    sum_chips = lambda owns, slots: _per_shape(lambda a, b: _sum_chips(a, b, me_arr), list(owns), list(slots))
    g_ffn = _ffn_weight_grads(x1b, dr2b, hid, dhg, dhu)
    (dya0, dproj, dbias, g_wo, g_wso, g_wco, g_glu_w, g_glu_b, g_conv8), (got_ffn,) = _mixer_backward(
        dr1, proj, ya0, ya, yb, glu_w, small["glu_b"], wso_st, conv_w8, wco_st, w_o, comms=[_swap_comm(g_ffn)])
    chip_ffn = add_halves(g_ffn, got_ffn)
    g_mix = [g_glu_w.reshape(N_SHARDS, 128, SSM_W), g_wso, g_wco, g_wo.reshape(N_SHARDS, 256, D_MODEL)]
    (dproj, dus, gbbr, gbbi, gctr, gcti, glbr, glbi, g_d), (slots_ffn, got_mix) = _ssm_backward(
        dya0, proj, xsr, xsi, bbr, bbi, ctr, cti, small["ssm_d"], tab_a, tab_p, dproj,
        comms=[_scatter_comm(chip_ffn), _swap_comm(g_mix)])
    halves_ffn = sum_chips(chip_ffn, slots_ffn)
    chip_mix = add_halves(g_mix, got_mix)
    g_lr, g_li, g_ldt, g_brt, g_bit, g_cre, g_cim = _ssm_param_grads(
        *disc, glbr.reshape(SUBLANES, N_GROUPS, N_STATE), glbi.reshape(SUBLANES, N_GROUPS, N_STATE),
        gbbr, gbbi, gctr, gcti)
    g_w_in, (others_ffn, slots_mix) = _in_weight_grad(
        x, dproj, comms=[_send_comm(halves_ffn), _scatter_comm(chip_mix)])
    halves_mix = sum_chips(chip_mix, slots_mix)
    dx, _ = _input_grad(dr1, dproj, w_in_st)

    g_conv = jnp.pad(g_conv8[:3].reshape(3, N_SHARDS, 128).transpose(1, 0, 2), ((0, 0), (0, SUBLANES - 3), (0, 0)))
    pieces = [dus, dbias[:, SSM_W:], g_lr, g_li, g_ldt, g_brt, g_bit, g_cre, g_cim, g_d, g_glu_b,
              g_ln1_g, g_ln1_b, g_ln2_g, g_ln2_b, loss]
    flat = jnp.concatenate([p.reshape(-1) for p in pieces])
    g_packed = jnp.pad(flat, (0, PACKED_ROWS * 128 - flat.shape[0])).reshape(PACKED_ROWS, 128)
    ((got_w, got_conv, got_packed),) = _standalone([_swap_comm([g_w_in], [g_conv, g_packed])], "swap_with_sibling")
    (chip_w,) = add_halves([g_w_in], [got_w])
    chip_conv, chip_packed = _small_pair_sums([g_conv, g_packed], [got_conv, got_packed])
    ((slots_w, slots_conv, slots_packed),) = _standalone(
        [_scatter_comm([chip_w, chip_conv], [chip_packed])], "scatter_to_chips")
    (halves_w,) = sum_chips([chip_w], [slots_w])
    conv_total, packed_total = _small_totals(me_arr, chip_conv, slots_conv, chip_packed, slots_packed)
    (others_rest,) = _standalone([_send_comm([halves_w] + halves_mix)], "send_to_sibling")

    pairs = dict(zip(FFN_W, zip(halves_ffn, others_ffn)))
    pairs.update(zip(("w_in",) + MIXER_W, zip([halves_w] + halves_mix, others_rest)))
    return dx, pairs, conv_total, packed_total


PACKED_ROWS = 1136
PACKED_LAYOUT = (("b_in", IN_COLS), ("ssm_lambda_re", STATE_COLS), ("ssm_lambda_im", STATE_COLS),
                 ("ssm_log_dt", N_GROUPS), ("ssm_b_re", SSM_W * N_STATE), ("ssm_b_im", SSM_W * N_STATE),
                 ("ssm_c_re", SSM_W * N_STATE), ("ssm_c_im", SSM_W * N_STATE), ("ssm_d", SSM_W), ("glu_b", SSM_W),
                 ("ln1_g", D_MODEL), ("ln1_b", D_MODEL), ("ln2_g", D_MODEL), ("ln2_b", D_MODEL), ("loss", 1))


def _unpack_small(packed):
    flat = packed.reshape(-1)
    out, off = {}, 0
    for name, size in PACKED_LAYOUT:
        out[name] = flat[off:off + size]
        off += size
    for name in ("ssm_b_re", "ssm_b_im"):
        out[name] = out[name].reshape(N_GROUPS, GROUP_C, N_STATE).transpose(0, 2, 1)[None]
    for name in ("ssm_c_re", "ssm_c_im"):
        out[name] = out[name].reshape(1, N_GROUPS, GROUP_C, N_STATE)
    for name in ("ssm_lambda_re", "ssm_lambda_im"):
        out[name] = out[name].reshape(1, N_GROUPS, N_STATE)
    for name in ("b_in", "ssm_log_dt", "ssm_d", "glu_b", "ln1_g", "ln1_b", "ln2_g", "ln2_b"):
        out[name] = out[name][None]
    return out


BIG = ("w_in", "glu_w", "w_ssm_out", "w_conv_out", "w_o", "w_gate", "w_up", "w_down")
SMALL = ("b_in", "ssm_lambda_re", "ssm_lambda_im", "ssm_log_dt", "ssm_b_re", "ssm_b_im", "ssm_c_re", "ssm_c_im",
         "ssm_d", "glu_b", "ln1_g", "ln1_b", "ln2_g", "ln2_b")
WEIGHTS = ("w_in", "b_in", "ssm_lambda_re", "ssm_lambda_im", "ssm_log_dt", "ssm_b_re", "ssm_b_im", "ssm_c_re",
           "ssm_c_im", "ssm_d", "glu_w", "glu_b", "w_ssm_out", "conv_w", "w_conv_out", "w_o", "ln1_g", "ln1_b",
           "w_gate", "w_up", "w_down", "ln2_g", "ln2_b")


def _place():
    x, y, c = lax.axis_index("x"), lax.axis_index("y"), lax.axis_index("c")
    chips = [(1 - x, y), (x, 1 - y), (1 - x, 1 - y)]
    return x, y, c, chips


def _shard_of(chip):
    return 2 * chip[0] + chip[1]


def _remote(src, dst, send_sem, recv_sem, to):
    return pltpu.make_async_remote_copy(src_ref=src, dst_ref=dst, send_sem=send_sem, recv_sem=recv_sem,
                                        device_id=to, device_id_type=MESH)


def _half_rows(shard, which):
    r2 = shard.shape[0] // 2
    return pl.ds(pl.multiple_of(which * r2, 16), r2)


def _own_slot(stack, shard):
    me = _shard_of((lax.axis_index("x"), lax.axis_index("y")))
    return lax.dynamic_update_slice(stack, shard[None], (me,) + (0,) * shard.ndim)


def _gather_ici(halved, whole=()):
    shards = list(halved) + list(whole)
    nh = len(halved)

    def copies(src, dst, sems):
        send_sem, recv_sem = sems
        x, y, c, chips = _place()
        me = _shard_of((x, y))
        out = []
        for a in range(len(shards)):
            for k, chip in enumerate(chips):
                if a < nh:
                    rows = _half_rows(shards[a], c)
                    out.append(_remote(src[a].at[rows], dst[a].at[me, rows], send_sem.at[a, k], recv_sem.at[a, k],
                                       (*chip, c)))
                else:
                    out.append(_remote(src[a], dst[a].at[me], send_sem.at[a, k], recv_sem.at[a, k], (*chip, c)))
        return out

    n = len(shards)
    return _Comm(shards, [jax.ShapeDtypeStruct((N_SHARDS,) + s.shape, s.dtype) for s in shards],
                 [pltpu.SemaphoreType.DMA((n, 3))] * 2, copies)


def _gather_d2d(stacks, shards):
    def copies(src, dst, sems):
        del src
        send_sem, recv_sem = sems
        x, y, c, chips = _place()
        out = []
        for a in range(len(stacks)):
            for k, chip in enumerate(chips):
                rows = dst[a].at[_shard_of(chip), _half_rows(shards[a], c)]
                out.append(_remote(rows, rows, send_sem.at[a, k], recv_sem.at[a, k], (x, y, 1 - c)))
        return out

    n = len(stacks)
    return _Comm(stacks, [jax.ShapeDtypeStruct(s.shape, s.dtype) for s in stacks],
                 [pltpu.SemaphoreType.DMA((n, 3))] * 2, copies, aliased=True)


def _standalone(comms, name):
    return _launch(None, comms, name=name, grid=(), in_specs=[], out_specs=(), out_shape=())()[1]


def _gather_weights(shards):
    n = len(shards)

    def body(*refs):
        src, dst = refs[:n], refs[n:2 * n]
        send_sem, recv_sem, fsend_sem, frecv_sem = refs[2 * n:]
        x, y, c, chips = _place()
        me = _shard_of((x, y))
        sibling = (x, y, 1 - c)
        sends = []
        for a in range(n):
            mine = _half_rows(shards[a], c)
            for k, chip in enumerate(chips):
                cp = _remote(src[a].at[mine], dst[a].at[me, mine], send_sem.at[a, k], recv_sem.at[a, k], (*chip, c))
                cp.start()
                sends.append(cp)
        for a in range(n):
            for k, chip in enumerate(chips):
                rows = dst[a].at[_shard_of(chip), _half_rows(shards[a], c)]
                _remote(rows, rows, send_sem.at[a, k], recv_sem.at[a, k], sibling).wait_recv()
                cp = _remote(rows, rows, fsend_sem.at[a, k], frecv_sem.at[a, k], sibling)
                cp.start()
                sends.append(cp)
        for a in range(n):
            for k, chip in enumerate(chips):
                rows = dst[a].at[_shard_of(chip), _half_rows(shards[a], 1 - c)]
                _remote(rows, rows, fsend_sem.at[a, k], frecv_sem.at[a, k], sibling).wait_recv()
        for cp in sends:
            cp.wait_send()

    stacks = pl.pallas_call(
        body, name="gather_weights",
        out_shape=tuple(jax.ShapeDtypeStruct((N_SHARDS,) + s.shape, s.dtype) for s in shards),
        in_specs=[ANY] * n, out_specs=(ANY,) * n,
        scratch_shapes=[pltpu.SemaphoreType.DMA((n, 3))] * 4,
    )(*shards)
    return [_own_slot(st, sh) for st, sh in zip(stacks, shards)]


def _swap_comm(big, small=()):
    nb, n = len(big), len(big) + len(small)
    arrays = list(big) + list(small)

    def copies(src, dst, sems):
        send_sem, recv_sem = sems
        x, y, c, _ = _place()
        out = []
        for a in range(n):
            if a < nb:
                r2 = arrays[a].shape[1] // 2
                part = src[a].at[:, pl.ds(pl.multiple_of((1 - c) * r2, SUBLANES), r2), :]
            else:
                part = src[a]
            out.append(_remote(part, dst[a], send_sem.at[a], recv_sem.at[a], (x, y, 1 - c)))
        return out

    out_shape = [jax.ShapeDtypeStruct((N_SHARDS, g.shape[1] // 2, g.shape[2]), g.dtype) for g in big]
    out_shape += [jax.ShapeDtypeStruct(g.shape, g.dtype) for g in small]
    return _Comm(arrays, out_shape, [pltpu.SemaphoreType.DMA((n,))] * 2, copies)


def _scatter_comm(slabbed, small=()):
    ns, n = len(slabbed), len(slabbed) + len(small)
    arrays = list(slabbed) + list(small)

    def copies(src, dst, sems):
        send_sem, recv_sem = sems
        _, _, c, chips = _place()
        out = []
        for a in range(n):
            for k, chip in enumerate(chips):
                part = src[a].at[_shard_of(chip)] if a < ns else src[a]
                out.append(_remote(part, dst[a].at[k], send_sem.at[a, k], recv_sem.at[a, k], (*chip, c)))
        return out

    out_shape = [jax.ShapeDtypeStruct((3,) + g.shape[1:], g.dtype) for g in slabbed]
    out_shape += [jax.ShapeDtypeStruct((3,) + g.shape, g.dtype) for g in small]
    return _Comm(arrays, out_shape, [pltpu.SemaphoreType.DMA((n, 3))] * 2, copies)


def _send_comm(arrays):
    n = len(arrays)

    def copies(src, dst, sems):
        send_sem, recv_sem = sems
        x, y, c, _ = _place()
        return [_remote(src[a], dst[a], send_sem.at[a], recv_sem.at[a], (x, y, 1 - c)) for a in range(n)]

    return _Comm(arrays, [jax.ShapeDtypeStruct(h.shape, h.dtype) for h in arrays],
                 [pltpu.SemaphoreType.DMA((n,))] * 2, copies)


def _row_chunk(rows):
    for cand in (256, 176, 128, 64):
        if rows % cand == 0:
            return cand
    return rows


def _per_shape(fn, *lists):
    groups = {}
    for i, items in enumerate(zip(*lists)):
        groups.setdefault(tuple(a.shape for a in items), []).append(i)
    out = [None] * len(lists[0])
    for idx in groups.values():
        for i, r in zip(idx, fn(*([lst[i] for i in idx] for lst in lists))):
            out[i] = r
    return out


def _add_own_half(stacks, receiveds, c):
    n = len(stacks)
    _, r2, cols = receiveds[0].shape

    def body(c_ref, *refs):
        del c_ref
        for a in range(n):
            refs[2 * n + a][...] = (refs[a][...] + refs[n + a][...]).astype(BF16)

    own = pl.BlockSpec((1, r2, cols), lambda s, c_ref: (s, c_ref[0], 0))
    got = pl.BlockSpec((1, r2, cols), lambda s, c_ref: (s, 0, 0))
    return pl.pallas_call(
        body, name="add_own_half",
        grid_spec=pltpu.PrefetchScalarGridSpec(
            num_scalar_prefetch=1, grid=(N_SHARDS,), in_specs=[own] * n + [got] * n, out_specs=(got,) * n),
        out_shape=(jax.ShapeDtypeStruct(receiveds[0].shape, BF16),) * n,
        compiler_params=_params(("parallel",)),
    )(c, *stacks, *receiveds)


def _chip_order_sum(me, own, s0, s1, s2):
    terms = []
    for s in range(N_SHARDS):
        d = jnp.bitwise_xor(me, s)
        terms.append(jnp.where(d == 0, own, jnp.where(d == 2, s0, jnp.where(d == 1, s1, s2))))
    return ((terms[0] + terms[1]) + terms[2]) + terms[3]


def _sum_chips(own_stacks, slots, me):
    n = len(slots)
    _, rows, cols = slots[0].shape
    rc = _row_chunk(rows)

    def body(me_ref, *refs):
        del me_ref
        for a in range(n):
            own_r, s_r = refs[a], refs[n + a]
            refs[2 * n + a][...] = (((own_r[0].astype(F32) + s_r[0].astype(F32)) + s_r[1].astype(F32))
                                    + s_r[2].astype(F32))

    own = pl.BlockSpec((1, rc, cols), lambda i, me_ref: (me_ref[0], i, 0))
    three = pl.BlockSpec((3, rc, cols), lambda i, me_ref: (0, i, 0))
    total = pl.BlockSpec((rc, cols), lambda i, me_ref: (i, 0))
    return pl.pallas_call(
        body, name="sum_chips",
        grid_spec=pltpu.PrefetchScalarGridSpec(
            num_scalar_prefetch=1, grid=(rows // rc,), in_specs=[own] * n + [three] * n, out_specs=(total,) * n),
        out_shape=(jax.ShapeDtypeStruct((rows, cols), F32),) * n,
        compiler_params=_params(("parallel",)),
    )(me, *own_stacks, *slots)


def _small_pair_sums(mine, theirs):
    n = len(mine)

    def body(*refs):
        for a in range(n):
            refs[2 * n + a][...] = refs[a][...] + refs[n + a][...]

    vm = pl.BlockSpec(memory_space=pltpu.VMEM)
    return pl.pallas_call(
        body, name="small_pair_sums", out_shape=tuple(jax.ShapeDtypeStruct(g.shape, g.dtype) for g in mine),
        in_specs=[vm] * (2 * n), out_specs=(vm,) * n,
        compiler_params=pltpu.CompilerParams(vmem_limit_bytes=VMEM_LIMIT),
    )(*mine, *theirs)


def _adam_math(w, g, m, v):
    m = ADAM_B1 * m + (1.0 - ADAM_B1) * g
    v = ADAM_B2 * v + (1.0 - ADAM_B2) * (g * g)
    m_hat = m / (1.0 - ADAM_B1 ** ADAM_STEP)
    v_hat = v / (1.0 - ADAM_B2 ** ADAM_STEP)
    delta = -ADAM_LR * (m_hat / (jnp.sqrt(v_hat) + ADAM_EPS) + ADAM_WD * w)
    return delta, m, v


def _small_totals(me, conv_stack, conv_slots, packed, packed_slots):
    def body(me_ref, cs_r, cslot_r, p_r, pslot_r, conv_r, tot_r):
        me_ = me_ref[0]
        conv_r[...] = _chip_order_sum(me_, cs_r[me_], cslot_r[0], cslot_r[1], cslot_r[2])
        tot_r[...] = _chip_order_sum(me_, p_r[...], pslot_r[0], pslot_r[1], pslot_r[2])

    vm = pl.BlockSpec(memory_space=pltpu.VMEM)
    return pl.pallas_call(
        body, name="small_totals",
        out_shape=(jax.ShapeDtypeStruct(conv_stack.shape[1:], F32), jax.ShapeDtypeStruct(packed.shape, F32)),
        in_specs=[pl.BlockSpec(memory_space=pltpu.SMEM)] + [vm] * 4, out_specs=(vm, vm),
    )(me, conv_stack, conv_slots, packed, packed_slots)


def _adam_small(gs, ws, ms, vs):
    n = len(gs)

    def body(*refs):
        for a in range(n):
            g_r, w_r, m_r, v_r = (refs[i * n + a] for i in range(4))
            d_r, nm_r, nv_r = (refs[(4 + i) * n + a] for i in range(3))
            d_r[...], nm_r[...], nv_r[...] = _adam_math(w_r[...], g_r[...], m_r[...], v_r[...])

    vm = pl.BlockSpec(memory_space=pltpu.VMEM)
    shapes = tuple(jax.ShapeDtypeStruct(w.shape, F32) for w in ws)
    out = pl.pallas_call(
        body, name="adam_small", out_shape=shapes * 3, in_specs=[vm] * (4 * n), out_specs=(vm,) * (3 * n),
        compiler_params=pltpu.CompilerParams(vmem_limit_bytes=VMEM_LIMIT),
    )(*gs, *ws, *ms, *vs)
    return out[:n], out[n:2 * n], out[2 * n:]


def _adam_big(ws, mines, others, ms, vs, c):
    n = len(ws)
    r2, cols = mines[0].shape
    rc = _row_chunk(r2)
    nch = r2 // rc

    def body(c_ref, *refs):
        mine_is_here = pl.program_id(0) == c_ref[0]
        for a in range(n):
            w_r, mine_r, other_r, m_r, v_r = (refs[i * n + a] for i in range(5))
            g_r, d_r, nm_r, nv_r = (refs[(5 + i) * n + a] for i in range(4))
            g = jnp.where(mine_is_here, mine_r[...], other_r[...])
            g_r[...] = g
            d_r[...], nm_r[...], nv_r[...] = _adam_math(w_r[...], g, m_r[...], v_r[...])

    full = pl.BlockSpec((rc, cols), lambda h, i, c_ref: (h * nch + i, 0))
    half = pl.BlockSpec((rc, cols), lambda h, i, c_ref: (i, 0))
    shape = jax.ShapeDtypeStruct((2 * r2, cols), F32)
    out = pl.pallas_call(
        body, name="adam_big",
        grid_spec=pltpu.PrefetchScalarGridSpec(
            num_scalar_prefetch=1, grid=(2, nch),
            in_specs=[full] * n + [half] * (2 * n) + [full] * (2 * n), out_specs=(full,) * (4 * n)),
        out_shape=(shape,) * (4 * n), compiler_params=_params(("parallel", "parallel")),
    )(c, *ws, *mines, *others, *ms, *vs)
    return [tuple(out[i * n + a] for i in range(4)) for a in range(n)]


def kernel(x, w_in, b_in, ssm_lambda_re, ssm_lambda_im, ssm_log_dt, ssm_b_re, ssm_b_im, ssm_c_re, ssm_c_im, ssm_d, glu_w, glu_b, w_ssm_out, conv_w, w_conv_out, w_o, ln1_g, ln1_b, w_gate, w_up, w_down, ln2_g, ln2_b, loss_target, m_w_in, m_b_in, m_ssm_lambda_re, m_ssm_lambda_im, m_ssm_log_dt, m_ssm_b_re, m_ssm_b_im, m_ssm_c_re, m_ssm_c_im, m_ssm_d, m_glu_w, m_glu_b, m_w_ssm_out, m_conv_w, m_w_conv_out, m_w_o, m_ln1_g, m_ln1_b, m_w_gate, m_w_up, m_w_down, m_ln2_g, m_ln2_b, v_w_in, v_b_in, v_ssm_lambda_re, v_ssm_lambda_im, v_ssm_log_dt, v_ssm_b_re, v_ssm_b_im, v_ssm_c_re, v_ssm_c_im, v_ssm_d, v_glu_w, v_glu_b, v_w_ssm_out, v_conv_w, v_w_conv_out, v_w_o, v_ln1_g, v_ln1_b, v_w_gate, v_w_up, v_w_down, v_ln2_g, v_ln2_b):
    given = dict(locals())
    w = {n: given[n] for n in WEIGHTS}
    m = {n: given["m_" + n] for n in WEIGHTS}
    v = {n: given["v_" + n] for n in WEIGHTS}

    flip = lambda n, a: a.T if n in ("w_gate", "w_up") else a
    shards = {n: flip(n, w[n][0]).astype(BF16) for n in BIG}
    shards["conv_w"] = jnp.pad(conv_w[0], ((0, SUBLANES - 3), (0, 0)))
    c_arr = jnp.reshape(lax.axis_index("c"), (1,)).astype(jnp.int32)
    me = _shard_of((lax.axis_index("x"), lax.axis_index("y")))
    me_arr = jnp.reshape(me, (1,)).astype(jnp.int32)
    dx, pairs, conv_total, packed_total = _device_step(
        x[0], loss_target[0], {n: w[n] for n in SMALL}, shards, c_arr, me_arr)

    grad = _unpack_small(packed_total)
    loss_total = grad.pop("loss")[0]
    grad["conv_w"] = conv_total[:3][None]
    small_names = ("conv_w",) + SMALL
    swap = lambda n, a: a.transpose(0, 1, 3, 2) if n in ("ssm_b_re", "ssm_b_im") else a
    ds, nms, nvs = _adam_small(*([swap(n, d[n]) for n in small_names] for d in (grad, w, m, v)))
    delta, new_m, new_v = {}, {}, {}
    for i, n in enumerate(small_names):
        delta[n], new_m[n], new_v[n] = swap(n, ds[i]), swap(n, nms[i]), swap(n, nvs[i])
    updated = _per_shape(
        lambda *a: _adam_big(*a, c_arr),
        [flip(n, w[n][0]) for n in BIG], [pairs[n][0] for n in BIG], [pairs[n][1] for n in BIG],
        [flip(n, m[n][0]) for n in BIG], [flip(n, v[n][0]) for n in BIG])
    for n, results in zip(BIG, updated):
        grad[n], delta[n], new_m[n], new_v[n] = (flip(n, r)[None] for r in results)

    return (loss_total, dx[None], *[grad[n] for n in WEIGHTS], *[delta[n] for n in WEIGHTS],
            *[new_m[n] for n in WEIGHTS], *[new_v[n] for n in WEIGHTS])
```

```python
import functools
import math

import jax
import jax.numpy as jnp
from jax import lax
from jax.experimental import pallas as pl
from jax.experimental.pallas import tpu as pltpu

F32 = jnp.float32
BF16 = jnp.bfloat16

D_MODEL = 1024
IN_COLS = 4096
SSM_W = 512
N_GROUPS = 32
N_STATE = 64
GROUP_C = 16
STATE_COLS = N_GROUPS * N_STATE
STRIP = 512
N_STRIPS = STATE_COLS // STRIP
FFN_SHARD = 704
N_SHARDS = 4
ALPHA = 2.0 ** 0.25
LN_EPS = 1e-5
GELU_K = math.sqrt(2.0 / math.pi)
GELU_C = 0.044715

ADAM_LR = 0.001
ADAM_B1 = 0.9
ADAM_B2 = 0.999
ADAM_EPS = 1e-08
ADAM_WD = 0.01
ADAM_STEP = 10

V7X_VMEM_BYTES = 64 * 1024 * 1024
VMEM_LIMIT = V7X_VMEM_BYTES - 8 * 1024 * 1024
SUBLANES = 8
N_POWERS = 128

MESH = pl.DeviceIdType.MESH
ANY = pl.BlockSpec(memory_space=pl.ANY)


def _dot(a, b):
    return jnp.dot(a.astype(BF16), b.astype(BF16), preferred_element_type=F32)


def _dot_t(a, b):
    return lax.dot_general(a.astype(BF16), b.astype(BF16), (((1,), (1,)), ((), ())),
                           preferred_element_type=F32)


def _tdot(a, b):
    return lax.dot_general(a.astype(BF16), b.astype(BF16), (((0,), (0,)), ((), ())),
                           preferred_element_type=F32)


def _sigmoid(v):
    return 1.0 / (1.0 + jnp.exp(-v))


def _split3(v):
    hi = v.astype(BF16)
    r1 = v - hi.astype(F32)
    mid = r1.astype(BF16)
    lo = (r1 - mid.astype(F32)).astype(BF16)
    return hi, mid, lo


def _const(shape):
    nd = len(shape)
    return pl.BlockSpec(shape, lambda *_: (0,) * nd)


def _params(sem, vmem=VMEM_LIMIT):
    return pltpu.CompilerParams(dimension_semantics=sem, vmem_limit_bytes=vmem)


def _gelu_parts(v):
    inner = GELU_K * (v + GELU_C * v * v * v)
    t = jnp.tanh(inner)
    g = 0.5 * v * (1.0 + t)
    dg = 0.5 * (1.0 + t) + 0.5 * v * (1.0 - t * t) * GELU_K * (1.0 + 3.0 * GELU_C * v * v)
    return g, dg


class _Comm:
    def __init__(self, inputs, out_shape, sems, copies, aliased=False, bounce=None):
        self.inputs, self.out_shape, self.sems = list(inputs), tuple(out_shape), list(sems)
        self.copies, self.aliased, self.bounce = copies, aliased, bounce


def _launch(body, comms, *, name, grid, in_specs, out_specs, out_shape, scratch_shapes=(), aliases=None, sem=None):
    comms = list(comms)
    n_in, n_out, n_scr = len(in_specs), len(out_specs), len(scratch_shapes)
    aliases = dict(aliases or {})
    layout = []
    p_in, p_out, p_sem = n_in, n_out, 0
    for cm in comms:
        layout.append((p_in, p_out, p_sem))
        if cm.aliased:
            for i in range(len(cm.inputs)):
                aliases[p_in + i] = p_out + i
        p_in, p_out, p_sem = p_in + len(cm.inputs), p_out + len(cm.out_shape), p_sem + len(cm.sems)
    tot_in, tot_out = p_in, p_out

    def fused(*refs):
        ins, outs = refs[:tot_in], refs[tot_in:tot_in + tot_out]
        scr = refs[tot_in + tot_out:tot_in + tot_out + n_scr]
        sems = refs[tot_in + tot_out + n_scr:]

        def descriptors(kind):
            out = []
            for cm, (a, b, s) in zip(comms, layout):
                make = cm.copies if kind == "remote" else cm.bounce
                if make is not None:
                    out += make(ins[a:a + len(cm.inputs)], outs[b:b + len(cm.out_shape)], sems[s:s + len(cm.sems)])
            return out

        steps = [pl.program_id(d) for d in range(len(grid))]
        first = functools.reduce(jnp.logical_and, [s == 0 for s in steps]) if grid else None
        last = functools.reduce(jnp.logical_and, [s == g - 1 for s, g in zip(steps, grid)]) if grid else None

        def start():
            for cp in descriptors("remote"):
                cp.start()
            for to_vmem, _ in descriptors("local"):
                to_vmem.start()

        def finish():
            for to_vmem, to_hbm in descriptors("local"):
                to_vmem.wait()
                to_hbm.start()
            for cp in descriptors("remote"):
                cp.wait()
            for _, to_hbm in descriptors("local"):
                to_hbm.wait()

        if comms:
            pl.when(first)(start) if grid else start()
        if body is not None:
            body(*ins[:n_in], *outs[:n_out], *scr)
        if comms:
            pl.when(last)(finish) if grid else finish()

    specs_in = list(in_specs) + [ANY] * (tot_in - n_in)
    specs_out = tuple(out_specs) + (ANY,) * (tot_out - n_out)
    shapes = tuple(out_shape) + tuple(s for cm in comms for s in cm.out_shape)
    scratch = list(scratch_shapes) + [s for cm in comms for s in cm.sems]
    if comms or sem is None:
        sem = ("arbitrary",) * len(grid)
    kwargs = dict(grid=grid) if grid else {}
    call = pl.pallas_call(fused, name=name, out_shape=shapes, in_specs=specs_in, out_specs=specs_out,
                          scratch_shapes=scratch, input_output_aliases=aliases,
                          compiler_params=_params(sem) if grid else None, **kwargs)

    def run(*args):
        out = call(*args, *(a for cm in comms for a in cm.inputs))
        results, rest = out[:n_out], out[n_out:]
        per_comm = []
        for cm in comms:
            per_comm.append(rest[:len(cm.out_shape)])
            rest = rest[len(cm.out_shape):]
        return results, per_comm

    return run


def _ssm_discretise(lr, li, ldt, lr16, li16, ldt16, brt, bit):
    def lam_bar(lr_, li_, ldt_):
        dt = jnp.exp(ldt_)
        mag = jnp.exp(lr_ * dt)
        return mag * jnp.cos(li_ * dt), mag * jnp.sin(li_ * dt)

    lb_re, lb_im = lam_bar(lr, li, ldt)
    l16_re, l16_im = lam_bar(lr16, li16, ldt16)
    den = lr16 * lr16 + li16 * li16
    num_re = l16_re - 1.0
    fr = (num_re * lr16 + l16_im * li16) / den
    fi = (l16_im * lr16 - num_re * li16) / den
    bb_re = fr * brt - fi * bit
    bb_im = fr * bit + fi * brt
    return lb_re, lb_im, bb_re, bb_im


def _strip_selectors():
    p = lax.broadcasted_iota(jnp.int32, (N_STATE, STRIP), 0)
    col = lax.broadcasted_iota(jnp.int32, (N_STATE, STRIP), 1)
    rep = ((col & (N_STATE - 1)) == p).astype(BF16)
    row = lax.broadcasted_iota(jnp.int32, (SSM_W, STRIP), 0)
    col2 = lax.broadcasted_iota(jnp.int32, (SSM_W, STRIP), 1)
    mask = (((row >> 4) & 7) == (col2 >> 6))
    return rep, mask


def _ssm_prepare(lr, li, ldt, lr16, li16, ldt16, brt, bit, cre, cim):
    def body(lr_r, li_r, ldt_r, lr16_r, li16_r, ldt16_r, brt_r, bit_r, cre_r, cim_r,
             pwr_r, pwi_r, bbr_r, bbi_r, ctr_r, cti_r):
        lb_re, lb_im, bb_re, bb_im = _ssm_discretise(
            lr_r[...], li_r[...], ldt_r[...], lr16_r[...], li16_r[...], ldt16_r[...], brt_r[...], bit_r[...])
        pr, pi_ = lb_re, lb_im
        pwr_r[0] = pr
        pwi_r[0] = pi_
        for k in range(1, N_POWERS):
            pr, pi_ = pr * lb_re - pi_ * lb_im, pr * lb_im + pi_ * lb_re
            pwr_r[k] = pr
            pwi_r[k] = pi_
        rep, mask = _strip_selectors()
        for src, dst in ((bb_re, bbr_r), (bb_im, bbi_r), (cre_r[...], ctr_r), (cim_r[...], cti_r)):
            wide = jnp.dot(src.astype(BF16), rep, preferred_element_type=F32)
            dst[...] = jnp.where(mask, wide, 0.0).astype(BF16)

    vm = pl.BlockSpec(memory_space=pltpu.VMEM)
    return pl.pallas_call(
        body, name="ssm_prepare",
        out_shape=(jax.ShapeDtypeStruct((N_POWERS, N_GROUPS, N_STATE), F32),) * 2
        + (jax.ShapeDtypeStruct((SSM_W, STRIP), BF16),) * 4,
        in_specs=[vm] * 10, out_specs=(vm,) * 6,
    )(lr, li, ldt, lr16, li16, ldt16, brt, bit, cre, cim)


def _scan_tables(pwr, pwi):
    pr = pwr.reshape(N_POWERS, STATE_COLS)
    pi_ = pwi.reshape(N_POWERS, STATE_COLS)
    rows8 = lambda v: jnp.broadcast_to(v[None], (SUBLANES, STATE_COLS))
    tab_a = jnp.stack([rows8(pr[0]), rows8(pi_[0]), rows8(pr[-1]), rows8(pi_[-1])])
    tab_p = jnp.stack([jnp.repeat(pr, SUBLANES, axis=0), jnp.repeat(pi_, SUBLANES, axis=0)])
    return tab_a, tab_p


def _ssm_param_grads(lr, li, ldt, lr16, li16, ldt16, brt, bit, dlbr, dlbi, dbbr, dbbi, dctr, dcti):
    def body(lr_r, li_r, ldt_r, lr16_r, li16_r, ldt16_r, brt_r, bit_r,
             dlbr_r, dlbi_r, dbbr_r, dbbi_r, dctr_r, dcti_r,
             glr_r, gli_r, gldt_r, gbrt_r, gbit_r, gcre_r, gcim_r):
        rep, mask = _strip_selectors()

        def fold(acc):
            return sum(lax.dot_general(t, rep, (((1,), (1,)), ((), ())), preferred_element_type=F32)
                       for t in _split3(jnp.where(mask, acc, 0.0)))

        g_lb_re = jnp.sum(dlbr_r[...], axis=0)
        g_lb_im = jnp.sum(dlbi_r[...], axis=0)
        g_bb_re = fold(dbbr_r[...])
        g_bb_im = fold(dbbi_r[...])
        gcre_r[...] = fold(dctr_r[...])
        gcim_r[...] = fold(dcti_r[...])
        prim = (lr_r[...], li_r[...], ldt_r[...], lr16_r[...], li16_r[...], ldt16_r[...], brt_r[...], bit_r[...])
        _, vjp = jax.vjp(_ssm_discretise, *prim)
        g_lr, g_li, g_ldt, g_lr16, g_li16, g_ldt16, g_brt, g_bit = vjp((g_lb_re, g_lb_im, g_bb_re, g_bb_im))
        grp = lax.broadcasted_iota(jnp.int32, (N_GROUPS, SSM_W), 0)
        rw = lax.broadcasted_iota(jnp.int32, (N_GROUPS, SSM_W), 1)
        gsum = ((rw >> 4) == grp).astype(BF16)

        def group_sum(v):
            return sum(jnp.dot(gsum, t, preferred_element_type=F32) for t in _split3(v))

        glr_r[...] = g_lr + group_sum(g_lr16)
        gli_r[...] = g_li + group_sum(g_li16)
        gldt_r[...] = g_ldt + jnp.sum(group_sum(g_ldt16), axis=1, keepdims=True)
        gbrt_r[...] = g_brt
        gbit_r[...] = g_bit

    vm = pl.BlockSpec(memory_space=pltpu.VMEM)
    gp = jax.ShapeDtypeStruct((N_GROUPS, N_STATE), F32)
    gb = jax.ShapeDtypeStruct((SSM_W, N_STATE), F32)
    return pl.pallas_call(
        body, name="ssm_param_grads",
        out_shape=(gp, gp, jax.ShapeDtypeStruct((N_GROUPS, 1), F32), gb, gb, gb, gb),
        in_specs=[vm] * 14, out_specs=(vm,) * 7,
    )(lr, li, ldt, lr16, li16, ldt16, brt, bit, dlbr, dlbi, dbbr, dbbi, dctr, dcti)


def _in_proj(x, w_in_st, b_in, comms=()):
    t = x.shape[0]
    tm = 512

    def body(x_r, w_r, b_r, o_r):
        xb = x_r[...].astype(BF16)
        for j in range(N_SHARDS):
            cols = slice(D_MODEL * j, D_MODEL * (j + 1))
            o_r[:, cols] = jnp.dot(xb, w_r[j], preferred_element_type=F32) + b_r[:, cols]

    (proj,), sent = _launch(
        body, comms, name="in_proj", grid=(t // tm,),
        out_shape=(jax.ShapeDtypeStruct((t, IN_COLS), F32),),
        in_specs=[pl.BlockSpec((tm, D_MODEL), lambda i: (i, 0)), _const((N_SHARDS, D_MODEL, D_MODEL)),
                  _const((1, IN_COLS))],
        out_specs=(pl.BlockSpec((tm, IN_COLS), lambda i: (i, 0)),),
        sem=("parallel",),
    )(x, w_in_st, b_in)
    return proj, sent


def _cmul_add(xr, xi, mr, mi, sr, si):
    return xr + (mr * sr - mi * si), xi + (mr * si + mi * sr)


SCAN_STEPS = N_POWERS
SCAN_CHUNK = SUBLANES * SCAN_STEPS


def _interleave(src_r, dst_r):
    for step in range(SCAN_STEPS):
        dst_r[SUBLANES * step:SUBLANES * (step + 1), :] = src_r[pl.ds(step, SUBLANES, stride=SCAN_STEPS), :]


def _deinterleave(src_r, dst_r):
    for step in range(SCAN_STEPS):
        dst_r[pl.ds(step, SUBLANES, stride=SCAN_STEPS), :] = src_r[SUBLANES * step:SUBLANES * (step + 1), :]


def _step_rows(step):
    return pl.ds(pl.multiple_of(step * SUBLANES, SUBLANES), SUBLANES)


def _scan_steps(body, init, by=4):
    def trip(t, carry):
        for u in range(by):
            carry = body(t * by + u, carry)
        return carry

    return lax.fori_loop(0, SCAN_STEPS // by, trip, init)


def _segment_states(first_r, first_i, ends_r, ends_i, a64_r, a64_i, order):
    row = lax.broadcasted_iota(jnp.int32, ends_r.shape, 0)
    cur_r, cur_i = first_r, first_i
    ent_r = jnp.zeros_like(ends_r)
    ent_i = jnp.zeros_like(ends_i)
    for s in order:
        ent_r = jnp.where(row == s, jnp.broadcast_to(cur_r, ends_r.shape), ent_r)
        ent_i = jnp.where(row == s, jnp.broadcast_to(cur_i, ends_i.shape), ent_i)
        cur_r, cur_i = _cmul_add(ends_r[s:s + 1, :], ends_i[s:s + 1, :], a64_r, a64_i, cur_r, cur_i)
    return ent_r, ent_i, cur_r, cur_i


def _ssm_forward(proj, bbr, bbi, ctr, cti, d_skip, tab_a, tab_p, comms=(), tc=SCAN_CHUNK):
    t = proj.shape[0]

    def body(u_r, bbr_r, bbi_r, ctr_r, cti_r, d_r, ta_r, tp_r, xsr_r, xsi_r, y_r, ui_s, yi_s, car_r, car_i):
        @pl.when(pl.program_id(1) == 0)
        def _():
            car_r[...] = jnp.zeros_like(car_r)
            car_i[...] = jnp.zeros_like(car_i)

        _interleave(u_r, ui_s)
        u = ui_s[...]
        xsr_r[...] = _dot(u, bbr_r[...])
        xsi_r[...] = _dot(u, bbi_r[...])
        a_r, a_i = ta_r[0], ta_r[1]

        def local(step, carry):
            rows = _step_rows(step)
            xr, xi = _cmul_add(xsr_r[rows, :], xsi_r[rows, :], a_r, a_i, *carry)
            xsr_r[rows, :] = xr
            xsi_r[rows, :] = xi
            return xr, xi

        zero = jnp.zeros((SUBLANES, STRIP), F32)
        ends_r, ends_i = _scan_steps(local, (zero, zero))
        ent_r, ent_i, out_r, out_i = _segment_states(
            car_r[...], car_i[...], ends_r, ends_i, ta_r[2, 0:1, :], ta_r[3, 0:1, :], range(SUBLANES))
        car_r[...] = out_r
        car_i[...] = out_i

        def entering(step, _):
            rows = _step_rows(step)
            xr, xi = _cmul_add(xsr_r[rows, :], xsi_r[rows, :], tp_r[0, rows, :], tp_r[1, rows, :], ent_r, ent_i)
            xsr_r[rows, :] = xr
            xsi_r[rows, :] = xi
            return 0

        _scan_steps(entering, 0)
        yi_s[...] = _dot_t(xsr_r[...], ctr_r[...]) - _dot_t(xsi_r[...], cti_r[...]) + d_r[...] * u
        _deinterleave(yi_s, y_r)

    strip_mat = pl.BlockSpec((128, STRIP), lambda j, k: (j, 0))
    states = pl.BlockSpec((tc, STRIP), lambda j, k: (k, j))
    return _launch(
        body, comms, name="ssm_forward", grid=(N_STRIPS, t // tc),
        out_shape=(jax.ShapeDtypeStruct((t, STATE_COLS), F32), jax.ShapeDtypeStruct((t, STATE_COLS), F32),
                   jax.ShapeDtypeStruct((t, SSM_W), F32)),
        in_specs=[pl.BlockSpec((tc, 128), lambda j, k: (k, j)),
                  strip_mat, strip_mat, strip_mat, strip_mat,
                  pl.BlockSpec((1, 128), lambda j, k: (0, j)),
                  pl.BlockSpec((4, SUBLANES, STRIP), lambda j, k: (0, 0, j)),
                  pl.BlockSpec((2, tc, STRIP), lambda j, k: (0, 0, j))],
        out_specs=(states, states, pl.BlockSpec((tc, 128), lambda j, k: (k, j))),
        scratch_shapes=[pltpu.VMEM((tc, 128), F32), pltpu.VMEM((tc, 128), F32),
                        pltpu.VMEM((1, STRIP), F32), pltpu.VMEM((1, STRIP), F32)],
        sem=("parallel", "arbitrary"),
    )(proj, bbr, bbi, ctr, cti, d_skip, tab_a, tab_p)


def _shift_down(v, prev, n):
    row = lax.broadcasted_iota(jnp.int32, v.shape, 0)
    out = pltpu.roll(v, n, 0)
    for r in range(n):
        src = prev[SUBLANES - n + r:SUBLANES - n + r + 1, :]
        out = jnp.where(row == r, jnp.broadcast_to(src, v.shape), out)
    return out


def _shift_up(v, nxt, n):
    rows = v.shape[0]
    row = lax.broadcasted_iota(jnp.int32, v.shape, 0)
    out = pltpu.roll(v, rows - n, 0)
    for r in range(n):
        src = nxt[r:r + 1, :]
        out = jnp.where(row == rows - n + r, jnp.broadcast_to(src, v.shape), out)
    return out


def _conv3(q, q_prev, w):
    return w[2:3, :] * q + w[1:2, :] * _shift_down(q, q_prev, 1) + w[0:1, :] * _shift_down(q, q_prev, 2)


def _mixer_forward(x, proj, ya0, glu_w, glu_b, wso_st, conv_w8, wco_st, w_o, comms=(), tm=256):
    t = x.shape[0]
    hb = tm // SUBLANES

    def body(x_r, ya0_r, h_r, cg_r, bg_r, ga_r, gb_r, hp_r, cgp_r,
             glu_w_r, glu_b_r, wso_r, cw_r, wco_r, wo_r, xh_r, rstd_r, ya_r, yb_r):
        i = pl.program_id(0)
        g, _ = _gelu_parts(ya0_r[...])
        ya1 = g * _sigmoid(_dot(g, glu_w_r[...]) + glu_b_r[...])
        q = cg_r[...] * h_r[...]
        q_prev = jnp.where(i > 0, cgp_r[...] * hp_r[...], 0.0)
        yb0 = bg_r[...] * _conv3(q, q_prev, cw_r[...])
        for j in range(N_SHARDS):
            ya_r[:, 256 * j:256 * (j + 1)] = _dot(ya1, wso_r[j])
            yb_r[:, 256 * j:256 * (j + 1)] = _dot(yb0, wco_r[j])
        merged = _sigmoid(ga_r[...]) * ya_r[...] + _sigmoid(gb_r[...]) * yb_r[...]
        r1 = ALPHA * x_r[...] + _dot(merged, wo_r[...])
        mu = jnp.mean(r1, axis=-1, keepdims=True)
        cen = r1 - mu
        rstd = lax.rsqrt(jnp.mean(cen * cen, axis=-1, keepdims=True) + LN_EPS)
        xh_r[...] = cen * rstd
        rstd_r[...] = rstd

    def col(w, c):
        return pl.BlockSpec((tm, w), lambda i: (i, c))

    def prev(c):
        return pl.BlockSpec((SUBLANES, SSM_W), lambda i: (jnp.maximum(i * hb - 1, 0), c))

    return _launch(
        body, comms, name="mixer_forward", grid=(t // tm,),
        out_shape=(jax.ShapeDtypeStruct((t, D_MODEL), F32), jax.ShapeDtypeStruct((t, 1), F32),
                   jax.ShapeDtypeStruct((t, D_MODEL), F32), jax.ShapeDtypeStruct((t, D_MODEL), F32)),
        in_specs=[col(D_MODEL, 0), col(SSM_W, 0), col(SSM_W, 1), col(SSM_W, 2), col(SSM_W, 3),
                  col(D_MODEL, 2), col(D_MODEL, 3), prev(1), prev(2),
                  _const((SSM_W, SSM_W)), _const((1, SSM_W)), _const((N_SHARDS, SSM_W, 256)),
                  _const((SUBLANES, SSM_W)), _const((N_SHARDS, SSM_W, 256)), _const((D_MODEL, D_MODEL))],
        out_specs=(col(D_MODEL, 0), pl.BlockSpec((tm, 1), lambda i: (i, 0)), col(D_MODEL, 0), col(D_MODEL, 0)),
        sem=("parallel",),
    )(x, ya0, proj, proj, proj, proj, proj, proj, proj, glu_w, glu_b, wso_st, conv_w8, wco_st, w_o)


def _layer_norm_bwd(dxhat, xhat, rstd):
    m1 = jnp.mean(dxhat, axis=-1, keepdims=True)
    m2 = jnp.mean(dxhat * xhat, axis=-1, keepdims=True)
    return rstd * (dxhat - m1 - xhat * m2)


def _ffn_step(xhat1, rstd1, target, ln1_g, ln1_b, ln2_g, ln2_b, wg_st, wu_st, wd_st, tm=256):
    t = xhat1.shape[0]

    def body(xh_r, rstd_r, tgt_r, g1_r, b1_r, g2_r, b2_r, wg_r, wu_r, wd_r,
             loss_r, dr1_r, x1b_r, dr2b_r, hid_r, dhg_r, dhu_r, dg2_r, db2_r, dg1_r, db1_r,
             hg_s, hu_s):
        @pl.when(pl.program_id(0) == 0)
        def _():
            for r in (loss_r, dg2_r, db2_r, dg1_r, db1_r):
                r[...] = jnp.zeros_like(r)

        xhat1_v = xh_r[...]
        x1 = xhat1_v * g1_r[...] + b1_r[...]
        x1b = x1.astype(BF16)
        x1b_r[...] = x1b
        ffn = jnp.zeros((tm, D_MODEL), F32)
        for j in range(N_SHARDS):
            hg = lax.dot_general(x1b, wg_r[j], (((1,), (1,)), ((), ())), preferred_element_type=F32)
            hu = lax.dot_general(x1b, wu_r[j], (((1,), (1,)), ((), ())), preferred_element_type=F32)
            hg_s[j] = hg
            hu_s[j] = hu
            hid = (hg * _sigmoid(hg) * hu).astype(BF16)
            hid_r[j] = hid
            ffn = ffn + jnp.dot(hid, wd_r[j], preferred_element_type=F32)
        r2 = ALPHA * x1 + ffn
        mu = jnp.mean(r2, axis=-1, keepdims=True)
        cen = r2 - mu
        rstd2 = lax.rsqrt(jnp.mean(cen * cen, axis=-1, keepdims=True) + LN_EPS)
        xhat2 = cen * rstd2
        diff = (xhat2 * g2_r[...] + b2_r[...]) - tgt_r[...]
        loss_r[...] += 0.5 * jnp.sum(jnp.mean(diff * diff, axis=-1, keepdims=True), axis=0, keepdims=True)
        dy = diff * (1.0 / D_MODEL)
        dg2_r[...] += jnp.sum(dy * xhat2, axis=0, keepdims=True)
        db2_r[...] += jnp.sum(dy, axis=0, keepdims=True)
        dr2 = _layer_norm_bwd(dy * g2_r[...], xhat2, rstd2)
        dr2b = dr2.astype(BF16)
        dr2b_r[...] = dr2b
        dx1 = ALPHA * dr2
        for j in range(N_SHARDS):
            dhid = lax.dot_general(dr2b, wd_r[j], (((1,), (1,)), ((), ())), preferred_element_type=F32)
            hg = hg_s[j]
            hu = hu_s[j]
            sg = _sigmoid(hg)
            dhu = (dhid * (hg * sg)).astype(BF16)
            dhg = (dhid * hu * (sg * (1.0 + hg * (1.0 - sg)))).astype(BF16)
            dhg_r[j] = dhg
            dhu_r[j] = dhu
            dx1 = dx1 + jnp.dot(dhg, wg_r[j], preferred_element_type=F32)
            dx1 = dx1 + jnp.dot(dhu, wu_r[j], preferred_element_type=F32)
        dg1_r[...] += jnp.sum(dx1 * xhat1_v, axis=0, keepdims=True)
        db1_r[...] += jnp.sum(dx1, axis=0, keepdims=True)
        dr1_r[...] = _layer_norm_bwd(dx1 * g1_r[...], xhat1_v, rstd_r[...])

    tile = pl.BlockSpec((tm, D_MODEL), lambda i: (i, 0))
    hidden = pl.BlockSpec((N_SHARDS, tm, FFN_SHARD), lambda i: (0, i, 0))
    vec = _const((1, D_MODEL))
    hid_shape = jax.ShapeDtypeStruct((N_SHARDS, t, FFN_SHARD), BF16)
    vec_shape = jax.ShapeDtypeStruct((1, D_MODEL), F32)
    return pl.pallas_call(
        body, name="ffn_step", grid=(t // tm,),
        out_shape=(jax.ShapeDtypeStruct((1, 1), F32), jax.ShapeDtypeStruct((t, D_MODEL), F32),
                   jax.ShapeDtypeStruct((t, D_MODEL), BF16), jax.ShapeDtypeStruct((t, D_MODEL), BF16),
                   hid_shape, hid_shape, hid_shape, vec_shape, vec_shape, vec_shape, vec_shape),
        in_specs=[tile, pl.BlockSpec((tm, 1), lambda i: (i, 0)), tile, vec, vec, vec, vec,
                  _const((N_SHARDS, FFN_SHARD, D_MODEL)), _const((N_SHARDS, FFN_SHARD, D_MODEL)),
                  _const((N_SHARDS, FFN_SHARD, D_MODEL))],
        out_specs=(_const((1, 1)), tile, tile, tile, hidden, hidden, hidden, vec, vec, vec, vec),
        scratch_shapes=[pltpu.VMEM((N_SHARDS, tm, FFN_SHARD), F32), pltpu.VMEM((N_SHARDS, tm, FFN_SHARD), F32)],
        compiler_params=_params(("arbitrary",)),
    )(xhat1, rstd1, target, ln1_g, ln1_b, ln2_g, ln2_b, wg_st, wu_st, wd_st)


def _ffn_weight_grads(x1b, dr2b, hid, dhg, dhu, tk=2048):
    t = x1b.shape[0]

    def body(x_r, dr_r, hid_r, dhg_r, dhu_r, gwg_r, gwu_r, gwd_r):
        @pl.when(pl.program_id(1) == 0)
        def _():
            for r in (gwg_r, gwu_r, gwd_r):
                r[...] = jnp.zeros_like(r)

        gwg_r[0] += _tdot(dhg_r[0], x_r[...])
        gwu_r[0] += _tdot(dhu_r[0], x_r[...])
        gwd_r[0] += _tdot(hid_r[0], dr_r[...])

    tile = pl.BlockSpec((tk, D_MODEL), lambda j, k: (k, 0))
    hidden = pl.BlockSpec((1, tk, FFN_SHARD), lambda j, k: (j, k, 0))
    row = pl.BlockSpec((1, FFN_SHARD, D_MODEL), lambda j, k: (j, 0, 0))
    return pl.pallas_call(
        body, name="ffn_weight_grads", grid=(N_SHARDS, t // tk),
        out_shape=(jax.ShapeDtypeStruct((N_SHARDS, FFN_SHARD, D_MODEL), F32),) * 3,
        in_specs=[tile, tile, hidden, hidden, hidden],
        out_specs=(row, row, row),
        compiler_params=_params(("parallel", "arbitrary")),
    )(x1b, dr2b, hid, dhg, dhu)


def _mixer_backward(dr1, proj, ya0, ya, yb, glu_w, glu_b, wso_st, conv_w8, wco_st, w_o, comms=(), tm=256):
    t = dr1.shape[0]
    hb = tm // SUBLANES
    last_block = t // SUBLANES - 1

    def body(dr1_r, dr1n_r, ya0_r, ya_r, yb_r, h_r, cg_r, bg_r, ga_r, gb_r, hp_r, cgp_r, bgn_r, gbn_r,
             glu_w_r, glu_b_r, wso_r, cw_r, wco_r, wo_r,
             dya0_r, dproj_r, dbias_r, gwo_r, gwso_r, gwco_r, gglu_w_r, gglu_b_r, gconv_r):
        i = pl.program_id(0)

        @pl.when(i == 0)
        def _():
            for r in (dbias_r, gwo_r, gwso_r, gwco_r, gglu_w_r, gglu_b_r, gconv_r):
                r[...] = jnp.zeros_like(r)

        dr1_v = dr1_r[...]
        dmerged = _dot_t(dr1_v, wo_r[...])
        sa = _sigmoid(ga_r[...])
        sb = _sigmoid(gb_r[...])
        ya_v = ya_r[...]
        yb_v = yb_r[...]
        gwo_r[...] += _tdot(sa * ya_v + sb * yb_v, dr1_v)
        dya = dmerged * sa
        dyb = dmerged * sb
        dga = dmerged * ya_v * (sa * (1.0 - sa))
        dgb = dmerged * yb_v * (sb * (1.0 - sb))

        g, gelu_grad = _gelu_parts(ya0_r[...])
        s1 = _sigmoid(_dot(g, glu_w_r[...]) + glu_b_r[...])
        ya1 = g * s1
        dya1 = jnp.zeros((tm, SSM_W), F32)
        for j in range(N_SHARDS):
            dya_j = dya[:, 256 * j:256 * (j + 1)]
            gwso_r[j] += _tdot(ya1, dya_j)
            dya1 = dya1 + _dot_t(dya_j, wso_r[j])
        dz1 = dya1 * g * (s1 * (1.0 - s1))
        gglu_b_r[...] += jnp.sum(dz1, axis=0, keepdims=True)
        gglu_w_r[...] += _tdot(g, dz1)
        dya0_r[...] = (dya1 * s1 + _dot_t(dz1, glu_w_r[...])) * gelu_grad

        cw = cw_r[...]
        h = h_r[...]
        cg = cg_r[...]
        bg = bg_r[...]
        q = cg * h
        q_prev = jnp.where(i > 0, cgp_r[...] * hp_r[...], 0.0)
        q1 = _shift_down(q, q_prev, 1)
        q2 = _shift_down(q, q_prev, 2)
        z = cw[2:3, :] * q + cw[1:2, :] * q1 + cw[0:1, :] * q2
        yb0 = bg * z
        dyb0 = jnp.zeros((tm, SSM_W), F32)
        for j in range(N_SHARDS):
            dyb_j = dyb[:, 256 * j:256 * (j + 1)]
            gwco_r[j] += _tdot(yb0, dyb_j)
            dyb0 = dyb0 + _dot_t(dyb_j, wco_r[j])
        dbg = dyb0 * z
        dz = dyb0 * bg
        dyb_n = _dot_t(dr1n_r[...], wo_r[...]) * _sigmoid(gbn_r[...])
        dyb0_n = jnp.zeros((SUBLANES, SSM_W), F32)
        for j in range(N_SHARDS):
            dyb0_n = dyb0_n + _dot_t(dyb_n[:, 256 * j:256 * (j + 1)], wco_r[j])
        dz_next = jnp.where(i < pl.num_programs(0) - 1, dyb0_n * bgn_r[...], 0.0)
        dq = cw[2:3, :] * dz + cw[1:2, :] * _shift_up(dz, dz_next, 1) + cw[0:1, :] * _shift_up(dz, dz_next, 2)
        gconv_r[0:1, :] += jnp.sum(dz * q2, axis=0, keepdims=True)
        gconv_r[1:2, :] += jnp.sum(dz * q1, axis=0, keepdims=True)
        gconv_r[2:3, :] += jnp.sum(dz * q, axis=0, keepdims=True)
        dh = dq * cg
        dcg = dq * h

        dproj_r[:, 0:512] = jnp.zeros((tm, SSM_W), BF16)
        pieces = ((512, dh), (1024, dcg), (1536, dbg), (2048, dga), (3072, dgb))
        for off, val in pieces:
            w = val.shape[1]
            dproj_r[:, off:off + w] = val.astype(BF16)
            dbias_r[:, off:off + w] += jnp.sum(val, axis=0, keepdims=True)

    def col(w, c):
        return pl.BlockSpec((tm, w), lambda i: (i, c))

    def prev(c):
        return pl.BlockSpec((SUBLANES, SSM_W), lambda i: (jnp.maximum(i * hb - 1, 0), c))

    def nxt(w, c):
        return pl.BlockSpec((SUBLANES, w), lambda i: (jnp.minimum((i + 1) * hb, last_block), c))

    sh = jax.ShapeDtypeStruct
    return _launch(
        body, comms, name="mixer_backward", grid=(t // tm,),
        out_shape=(sh((t, SSM_W), F32), sh((t, IN_COLS), BF16), sh((1, IN_COLS), F32),
                   sh((D_MODEL, D_MODEL), F32), sh((N_SHARDS, SSM_W, 256), F32), sh((N_SHARDS, SSM_W, 256), F32),
                   sh((SSM_W, SSM_W), F32), sh((1, SSM_W), F32), sh((SUBLANES, SSM_W), F32)),
        in_specs=[col(D_MODEL, 0), nxt(D_MODEL, 0), col(SSM_W, 0), col(D_MODEL, 0), col(D_MODEL, 0),
                  col(SSM_W, 1), col(SSM_W, 2), col(SSM_W, 3), col(D_MODEL, 2), col(D_MODEL, 3),
                  prev(1), prev(2), nxt(SSM_W, 3), nxt(D_MODEL, 3),
                  _const((SSM_W, SSM_W)), _const((1, SSM_W)), _const((N_SHARDS, SSM_W, 256)),
                  _const((SUBLANES, SSM_W)), _const((N_SHARDS, SSM_W, 256)), _const((D_MODEL, D_MODEL))],
        out_specs=(col(SSM_W, 0), col(IN_COLS, 0), _const((1, IN_COLS)),
                   _const((D_MODEL, D_MODEL)), _const((N_SHARDS, SSM_W, 256)), _const((N_SHARDS, SSM_W, 256)),
                   _const((SSM_W, SSM_W)), _const((1, SSM_W)), _const((SUBLANES, SSM_W))),
        sem=("arbitrary",),
    )(dr1, dr1, ya0, ya, yb, proj, proj, proj, proj, proj, proj, proj, proj, proj,
      glu_w, glu_b, wso_st, conv_w8, wco_st, w_o)


def _cmulc_add(xr, xi, mr, mi, sr, si):
    return xr + (mr * sr + mi * si), xi + (mr * si - mi * sr)


def _ssm_backward(dya0, proj, xsr, xsi, bbr, bbi, ctr, cti, d_skip, tab_a, tab_p, dproj, comms=(), tc=SCAN_CHUNK):
    t = proj.shape[0]
    nk = t // tc

    def body(dy_r, u_r, xsr_r, xsi_r, bbr_r, bbi_r, ctr_r, cti_r, d_r, ta_r, tp_r, dproj_any,
             du_r, dus_r, gbbr_r, gbbi_r, gctr_r, gcti_r, glbr_r, glbi_r, gd_r,
             gr_s, gi_s, dyi_s, ui_s, dui_s, dun_s, car_r, car_i):
        del dproj_any

        @pl.when(pl.program_id(1) == 0)
        def _():
            for r in (car_r, car_i, dus_r, gbbr_r, gbbi_r, gctr_r, gcti_r, glbr_r, glbi_r, gd_r):
                r[...] = jnp.zeros_like(r)

        _interleave(dy_r, dyi_s)
        _interleave(u_r, ui_s)
        dy = dyi_s[...]
        u = ui_s[...]
        gr_s[...] = _dot(dy, ctr_r[...])
        gi_s[...] = -_dot(dy, cti_r[...])
        a_r, a_i = ta_r[0], ta_r[1]

        def local(n, carry):
            rows = _step_rows(SCAN_STEPS - 1 - n)
            gr, gi = _cmulc_add(gr_s[rows, :], gi_s[rows, :], a_r, a_i, *carry)
            gr_s[rows, :] = gr
            gi_s[rows, :] = gi
            return gr, gi

        zero = jnp.zeros((SUBLANES, STRIP), F32)
        ends_r, ends_i = _scan_steps(local, (zero, zero))
        ent_r, ent_i, out_r, out_i = _segment_states(
            car_r[...], car_i[...], ends_r, ends_i, ta_r[2, 0:1, :], -ta_r[3, 0:1, :], range(SUBLANES - 1, -1, -1))
        car_r[...] = out_r
        car_i[...] = out_i

        def entering(n, carry):
            gnr, gni, ar, ai = carry
            rows = _step_rows(SCAN_STEPS - 1 - n)
            power = _step_rows(n)
            gr, gi = _cmulc_add(gr_s[rows, :], gi_s[rows, :], tp_r[0, power, :], tp_r[1, power, :], ent_r, ent_i)
            gr_s[rows, :] = gr
            gi_s[rows, :] = gi
            xr = xsr_r[rows, :]
            xi = xsi_r[rows, :]
            return gr, gi, ar + (xr * gnr + xi * gni), ai + (xr * gni - xi * gnr)

        _, _, ar, ai = _scan_steps(entering, (ent_r, ent_i, zero, zero))
        glbr_r[...] += ar
        glbi_r[...] += ai
        gr = gr_s[...]
        gi = gi_s[...]
        dui_s[...] = _dot_t(gr, bbr_r[...]) + _dot_t(gi, bbi_r[...]) + d_r[...] * dy
        _deinterleave(dui_s, dun_s)
        du = dun_s[...]
        du_r[...] = du.astype(BF16)
        dus_r[...] += jnp.sum(du, axis=0, keepdims=True)
        gd_r[...] += jnp.sum(dy * u, axis=0, keepdims=True)
        gbbr_r[...] += _tdot(u, gr)
        gbbi_r[...] += _tdot(u, gi)
        gctr_r[...] += _tdot(dy, xsr_r[...])
        gcti_r[...] -= _tdot(dy, xsi_r[...])

    def rev(w):
        return pl.BlockSpec((tc, w), lambda j, k: (nk - 1 - k, j))

    strip_mat = pl.BlockSpec((128, STRIP), lambda j, k: (j, 0))
    vec = pl.BlockSpec((1, 128), lambda j, k: (0, j))
    lbacc = pl.BlockSpec((SUBLANES, STRIP), lambda j, k: (0, j))
    sh = jax.ShapeDtypeStruct
    return _launch(
        body, comms, name="ssm_backward", grid=(N_STRIPS, nk),
        out_shape=(sh((t, IN_COLS), BF16), sh((1, SSM_W), F32),
                   sh((SSM_W, STRIP), F32), sh((SSM_W, STRIP), F32), sh((SSM_W, STRIP), F32), sh((SSM_W, STRIP), F32),
                   sh((SUBLANES, STATE_COLS), F32), sh((SUBLANES, STATE_COLS), F32), sh((1, SSM_W), F32)),
        in_specs=[rev(128), rev(128), rev(STRIP), rev(STRIP),
                  strip_mat, strip_mat, strip_mat, strip_mat, vec,
                  pl.BlockSpec((4, SUBLANES, STRIP), lambda j, k: (0, 0, j)),
                  pl.BlockSpec((2, tc, STRIP), lambda j, k: (0, 0, j)), ANY],
        out_specs=(rev(128), vec, strip_mat, strip_mat, strip_mat, strip_mat, lbacc, lbacc, vec),
        scratch_shapes=[pltpu.VMEM((tc, STRIP), F32), pltpu.VMEM((tc, STRIP), F32)]
        + [pltpu.VMEM((tc, 128), F32)] * 4 + [pltpu.VMEM((1, STRIP), F32)] * 2,
        aliases={11: 0}, sem=("parallel", "arbitrary"),
    )(dya0, proj, xsr, xsi, bbr, bbi, ctr, cti, d_skip, tab_a, tab_p, dproj)


def _input_grad(dr1, dproj, w_in_st, comms=(), tm=512):
    t = dr1.shape[0]

    def body(dr1_r, dp_r, w_r, dx_r):
        acc = ALPHA * dr1_r[...]
        for j in range(N_SHARDS):
            acc = acc + lax.dot_general(dp_r[:, D_MODEL * j:D_MODEL * (j + 1)], w_r[j],
                                        (((1,), (1,)), ((), ())), preferred_element_type=F32)
        dx_r[...] = acc

    (dx,), sent = _launch(
        body, comms, name="input_grad", grid=(t // tm,),
        out_shape=(jax.ShapeDtypeStruct((t, D_MODEL), F32),),
        in_specs=[pl.BlockSpec((tm, D_MODEL), lambda i: (i, 0)), pl.BlockSpec((tm, IN_COLS), lambda i: (i, 0)),
                  _const((N_SHARDS, D_MODEL, D_MODEL))],
        out_specs=(pl.BlockSpec((tm, D_MODEL), lambda i: (i, 0)),),
        sem=("parallel",),
    )(dr1, dproj, w_in_st)
    return dx, sent


def _in_weight_grad(x, dproj, comms=(), tk=2048):
    t = x.shape[0]

    def body(x_r, dp_r, gw_r):
        @pl.when(pl.program_id(1) == 0)
        def _():
            gw_r[...] = jnp.zeros_like(gw_r)

        gw_r[0] += _tdot(x_r[...], dp_r[...])

    (g_w_in,), sent = _launch(
        body, comms, name="in_weight_grad", grid=(N_SHARDS, t // tk),
        out_shape=(jax.ShapeDtypeStruct((N_SHARDS, D_MODEL, D_MODEL), F32),),
        in_specs=[pl.BlockSpec((tk, D_MODEL), lambda j, k: (k, 0)), pl.BlockSpec((tk, D_MODEL), lambda j, k: (k, j))],
        out_specs=(pl.BlockSpec((1, D_MODEL, D_MODEL), lambda j, k: (j, 0, 0)),),
        sem=("parallel", "arbitrary"),
    )(x, dproj)
    return g_w_in, sent


MIXER_W = ("glu_w", "w_ssm_out", "w_conv_out", "w_o")
FFN_W = ("w_gate", "w_up", "w_down")


def _device_step(x, target, small, shards, c_arr, me_arr):
    lr, li = small["ssm_lambda_re"][0], small["ssm_lambda_im"][0]
    ldt = small["ssm_log_dt"][0][:, None]
    rep16 = lambda a: jnp.broadcast_to(a[:, None, :], (N_GROUPS, GROUP_C, a.shape[-1])).reshape(SSM_W, a.shape[-1])
    lr16, li16 = rep16(lr), rep16(li)
    ldt16 = rep16(jnp.broadcast_to(ldt, (N_GROUPS, N_STATE)))
    brt = small["ssm_b_re"][0].transpose(0, 2, 1).reshape(SSM_W, N_STATE)
    bit = small["ssm_b_im"][0].transpose(0, 2, 1).reshape(SSM_W, N_STATE)
    cre = small["ssm_c_re"][0].reshape(SSM_W, N_STATE)
    cim = small["ssm_c_im"][0].reshape(SSM_W, N_STATE)
    disc = (lr, li, ldt, lr16, li16, ldt16, brt, bit)

    pwr, pwi, bbr, bbi, ctr, cti = _ssm_prepare(*disc, cre, cim)
    tab_a, tab_p = _scan_tables(pwr, pwi)

    first_sh = [shards[n] for n in MIXER_W + FFN_W[:1]]
    second_sh = [shards[n] for n in FFN_W[1:]]
    (w_in_st,) = _gather_weights([shards["w_in"]])
    proj, (arrived,) = _in_proj(x, w_in_st, small["b_in"], comms=[_gather_ici(first_sh, [shards["conv_w"]])])
    (xsr, xsi, ya0), (second_part, first_st) = _ssm_forward(
        proj, bbr, bbi, ctr, cti, small["ssm_d"], tab_a, tab_p,
        comms=[_gather_ici(second_sh), _gather_d2d(arrived[:len(first_sh)], first_sh)])
    glu_st, wso_st, wco_st, wo_st, wg_st = first_st
    conv_st = arrived[len(first_sh)]
    conv_w8 = jnp.pad(conv_st[:, :3, :].transpose(1, 0, 2).reshape(3, SSM_W), ((0, SUBLANES - 3), (0, 0)))
    w_o = wo_st.reshape(D_MODEL, D_MODEL)
    glu_w = glu_st.reshape(SSM_W, SSM_W)
    (xhat1, rstd1, ya, yb), (second_st,) = _mixer_forward(
        x, proj, ya0, glu_w, small["glu_b"], wso_st, conv_w8, wco_st, w_o, comms=[_gather_d2d(second_part, second_sh)])
    wu_st, wd_st = second_st
    (loss, dr1, x1b, dr2b, hid, dhg, dhu, g_ln2_g, g_ln2_b, g_ln1_g, g_ln1_b) = _ffn_step(
        xhat1, rstd1, target, small["ln1_g"], small["ln1_b"], small["ln2_g"], small["ln2_b"], wg_st, wu_st, wd_st)

    add_halves = lambda gs, rs: _per_shape(lambda a, b: _add_own_half(a, b, c_arr), list(gs), list(rs))
    sum_chips = lambda owns, slots: _per_shape(lambda a, b: _sum_chips(a, b, me_arr), list(owns), list(slots))
    g_ffn = _ffn_weight_grads(x1b, dr2b, hid, dhg, dhu)
    (dya0, dproj, dbias, g_wo, g_wso, g_wco, g_glu_w, g_glu_b, g_conv8), (got_ffn,) = _mixer_backward(
        dr1, proj, ya0, ya, yb, glu_w, small["glu_b"], wso_st, conv_w8, wco_st, w_o, comms=[_swap_comm(g_ffn)])
    chip_ffn = add_halves(g_ffn, got_ffn)
    g_mix = [g_glu_w.reshape(N_SHARDS, 128, SSM_W), g_wso, g_wco, g_wo.reshape(N_SHARDS, 256, D_MODEL)]
    (dproj, dus, gbbr, gbbi, gctr, gcti, glbr, glbi, g_d), (slots_ffn, got_mix) = _ssm_backward(
        dya0, proj, xsr, xsi, bbr, bbi, ctr, cti, small["ssm_d"], tab_a, tab_p, dproj,
        comms=[_scatter_comm(chip_ffn), _swap_comm(g_mix)])
    halves_ffn = sum_chips(chip_ffn, slots_ffn)
    chip_mix = add_halves(g_mix, got_mix)
    g_lr, g_li, g_ldt, g_brt, g_bit, g_cre, g_cim = _ssm_param_grads(
        *disc, glbr.reshape(SUBLANES, N_GROUPS, N_STATE), glbi.reshape(SUBLANES, N_GROUPS, N_STATE),
        gbbr, gbbi, gctr, gcti)
    g_w_in, (others_ffn, slots_mix) = _in_weight_grad(
        x, dproj, comms=[_send_comm(halves_ffn), _scatter_comm(chip_mix)])
    halves_mix = sum_chips(chip_mix, slots_mix)
    dx, _ = _input_grad(dr1, dproj, w_in_st)

    g_conv = jnp.pad(g_conv8[:3].reshape(3, N_SHARDS, 128).transpose(1, 0, 2), ((0, 0), (0, SUBLANES - 3), (0, 0)))
    pieces = [dus, dbias[:, SSM_W:], g_lr, g_li, g_ldt, g_brt, g_bit, g_cre, g_cim, g_d, g_glu_b,
              g_ln1_g, g_ln1_b, g_ln2_g, g_ln2_b, loss]
    flat = jnp.concatenate([p.reshape(-1) for p in pieces])
    g_packed = jnp.pad(flat, (0, PACKED_ROWS * 128 - flat.shape[0])).reshape(PACKED_ROWS, 128)
    ((got_w, got_conv, got_packed),) = _standalone([_swap_comm([g_w_in], [g_conv, g_packed])], "swap_with_sibling")
    (chip_w,) = add_halves([g_w_in], [got_w])
    chip_conv, chip_packed = _small_pair_sums([g_conv, g_packed], [got_conv, got_packed])
    ((slots_w, slots_conv, slots_packed),) = _standalone(
        [_scatter_comm([chip_w, chip_conv], [chip_packed])], "scatter_to_chips")
    (halves_w,) = sum_chips([chip_w], [slots_w])
    conv_total, packed_total = _small_totals(me_arr, chip_conv, slots_conv, chip_packed, slots_packed)
    (others_rest,) = _standalone([_send_comm([halves_w] + halves_mix)], "send_to_sibling")

    pairs = dict(zip(FFN_W, zip(halves_ffn, others_ffn)))
    pairs.update(zip(("w_in",) + MIXER_W, zip([halves_w] + halves_mix, others_rest)))
    return dx, pairs, conv_total, packed_total


PACKED_ROWS = 1136
PACKED_LAYOUT = (("b_in", IN_COLS), ("ssm_lambda_re", STATE_COLS), ("ssm_lambda_im", STATE_COLS),
                 ("ssm_log_dt", N_GROUPS), ("ssm_b_re", SSM_W * N_STATE), ("ssm_b_im", SSM_W * N_STATE),
                 ("ssm_c_re", SSM_W * N_STATE), ("ssm_c_im", SSM_W * N_STATE), ("ssm_d", SSM_W), ("glu_b", SSM_W),
                 ("ln1_g", D_MODEL), ("ln1_b", D_MODEL), ("ln2_g", D_MODEL), ("ln2_b", D_MODEL), ("loss", 1))


def _unpack_small(packed):
    flat = packed.reshape(-1)
    out, off = {}, 0
    for name, size in PACKED_LAYOUT:
        out[name] = flat[off:off + size]
        off += size
    for name in ("ssm_b_re", "ssm_b_im"):
        out[name] = out[name].reshape(N_GROUPS, GROUP_C, N_STATE).transpose(0, 2, 1)[None]
    for name in ("ssm_c_re", "ssm_c_im"):
        out[name] = out[name].reshape(1, N_GROUPS, GROUP_C, N_STATE)
    for name in ("ssm_lambda_re", "ssm_lambda_im"):
        out[name] = out[name].reshape(1, N_GROUPS, N_STATE)
    for name in ("b_in", "ssm_log_dt", "ssm_d", "glu_b", "ln1_g", "ln1_b", "ln2_g", "ln2_b"):
        out[name] = out[name][None]
    return out


BIG = ("w_in", "glu_w", "w_ssm_out", "w_conv_out", "w_o", "w_gate", "w_up", "w_down")
SMALL = ("b_in", "ssm_lambda_re", "ssm_lambda_im", "ssm_log_dt", "ssm_b_re", "ssm_b_im", "ssm_c_re", "ssm_c_im",
         "ssm_d", "glu_b", "ln1_g", "ln1_b", "ln2_g", "ln2_b")
WEIGHTS = ("w_in", "b_in", "ssm_lambda_re", "ssm_lambda_im", "ssm_log_dt", "ssm_b_re", "ssm_b_im", "ssm_c_re",
           "ssm_c_im", "ssm_d", "glu_w", "glu_b", "w_ssm_out", "conv_w", "w_conv_out", "w_o", "ln1_g", "ln1_b",
           "w_gate", "w_up", "w_down", "ln2_g", "ln2_b")


def _place():
    x, y, c = lax.axis_index("x"), lax.axis_index("y"), lax.axis_index("c")
    chips = [(1 - x, y), (x, 1 - y), (1 - x, 1 - y)]
    return x, y, c, chips


def _shard_of(chip):
    return 2 * chip[0] + chip[1]


def _remote(src, dst, send_sem, recv_sem, to):
    return pltpu.make_async_remote_copy(src_ref=src, dst_ref=dst, send_sem=send_sem, recv_sem=recv_sem,
                                        device_id=to, device_id_type=MESH)


def _half_rows(shard, which):
    r2 = shard.shape[0] // 2
    return pl.ds(pl.multiple_of(which * r2, 16), r2)


def _gather_ici(halved, whole=()):
    shards = list(halved) + list(whole)
    nh = len(halved)

    def copies(src, dst, sems):
        send_sem, recv_sem = sems[:2]
        x, y, c, chips = _place()
        me = _shard_of((x, y))
        out = []
        for a in range(len(shards)):
            for k, chip in enumerate(chips):
                if a < nh:
                    rows = _half_rows(shards[a], c)
                    out.append(_remote(src[a].at[rows], dst[a].at[me, rows], send_sem.at[a, k], recv_sem.at[a, k],
                                       (*chip, c)))
                else:
                    out.append(_remote(src[a], dst[a].at[me], send_sem.at[a, k], recv_sem.at[a, k], (*chip, c)))
        return out

    n = len(shards)

    def bounce(src, dst, sems):
        local_sem, buffers = sems[2], sems[3:]
        me = _shard_of((lax.axis_index("x"), lax.axis_index("y")))
        return [(pltpu.make_async_copy(src[a], buffers[a], local_sem.at[a, 0]),
                 pltpu.make_async_copy(buffers[a], dst[a].at[me], local_sem.at[a, 1])) for a in range(n)]

    scratch = [pltpu.SemaphoreType.DMA((n, 3))] * 2 + [pltpu.SemaphoreType.DMA((n, 2))]
    scratch += [pltpu.VMEM(s.shape, s.dtype) for s in shards]
    return _Comm(shards, [jax.ShapeDtypeStruct((N_SHARDS,) + s.shape, s.dtype) for s in shards], scratch, copies,
                 bounce=bounce)


def _gather_d2d(stacks, shards):
    def copies(src, dst, sems):
        del src
        send_sem, recv_sem = sems
        x, y, c, chips = _place()
        out = []
        for a in range(len(stacks)):
            for k, chip in enumerate(chips):
                rows = dst[a].at[_shard_of(chip), _half_rows(shards[a], c)]
                out.append(_remote(rows, rows, send_sem.at[a, k], recv_sem.at[a, k], (x, y, 1 - c)))
        return out

    n = len(stacks)
    return _Comm(stacks, [jax.ShapeDtypeStruct(s.shape, s.dtype) for s in stacks],
                 [pltpu.SemaphoreType.DMA((n, 3))] * 2, copies, aliased=True)


def _standalone(comms, name):
    return _launch(None, comms, name=name, grid=(), in_specs=[], out_specs=(), out_shape=())()[1]


def _gather_weights(shards):
    n = len(shards)

    def body(*refs):
        src, dst = refs[:n], refs[n:2 * n]
        send_sem, recv_sem, fsend_sem, frecv_sem, local_sem = refs[2 * n:2 * n + 5]
        buffers = refs[2 * n + 5:]
        x, y, c, chips = _place()
        me = _shard_of((x, y))
        sibling = (x, y, 1 - c)
        own = [(pltpu.make_async_copy(src[a], buffers[a], local_sem.at[a, 0]),
                pltpu.make_async_copy(buffers[a], dst[a].at[me], local_sem.at[a, 1])) for a in range(n)]
        for to_vmem, _ in own:
            to_vmem.start()
        sends = []
        for a in range(n):
            mine = _half_rows(shards[a], c)
            for k, chip in enumerate(chips):
                cp = _remote(src[a].at[mine], dst[a].at[me, mine], send_sem.at[a, k], recv_sem.at[a, k], (*chip, c))
                cp.start()
                sends.append(cp)
        for to_vmem, to_hbm in own:
            to_vmem.wait()
            to_hbm.start()
        for a in range(n):
            for k, chip in enumerate(chips):
                rows = dst[a].at[_shard_of(chip), _half_rows(shards[a], c)]
                _remote(rows, rows, send_sem.at[a, k], recv_sem.at[a, k], sibling).wait_recv()
                cp = _remote(rows, rows, fsend_sem.at[a, k], frecv_sem.at[a, k], sibling)
                cp.start()
                sends.append(cp)
        for a in range(n):
            for k, chip in enumerate(chips):
                rows = dst[a].at[_shard_of(chip), _half_rows(shards[a], 1 - c)]
                _remote(rows, rows, fsend_sem.at[a, k], frecv_sem.at[a, k], sibling).wait_recv()
        for cp in sends:
            cp.wait_send()
        for _, to_hbm in own:
            to_hbm.wait()

    return pl.pallas_call(
        body, name="gather_weights",
        out_shape=tuple(jax.ShapeDtypeStruct((N_SHARDS,) + s.shape, s.dtype) for s in shards),
        in_specs=[ANY] * n, out_specs=(ANY,) * n,
        scratch_shapes=[pltpu.SemaphoreType.DMA((n, 3))] * 4 + [pltpu.SemaphoreType.DMA((n, 2))]
        + [pltpu.VMEM(s.shape, s.dtype) for s in shards],
    )(*shards)


def _swap_comm(big, small=()):
    nb, n = len(big), len(big) + len(small)
    arrays = list(big) + list(small)

    def copies(src, dst, sems):
        send_sem, recv_sem = sems
        x, y, c, _ = _place()
        out = []
        for a in range(n):
            if a < nb:
                r2 = arrays[a].shape[1] // 2
                part = src[a].at[:, pl.ds(pl.multiple_of((1 - c) * r2, SUBLANES), r2), :]
            else:
                part = src[a]
            out.append(_remote(part, dst[a], send_sem.at[a], recv_sem.at[a], (x, y, 1 - c)))
        return out

    out_shape = [jax.ShapeDtypeStruct((N_SHARDS, g.shape[1] // 2, g.shape[2]), g.dtype) for g in big]
    out_shape += [jax.ShapeDtypeStruct(g.shape, g.dtype) for g in small]
    return _Comm(arrays, out_shape, [pltpu.SemaphoreType.DMA((n,))] * 2, copies)


def _scatter_comm(slabbed, small=()):
    ns, n = len(slabbed), len(slabbed) + len(small)
    arrays = list(slabbed) + list(small)

    def copies(src, dst, sems):
        send_sem, recv_sem = sems
        _, _, c, chips = _place()
        out = []
        for a in range(n):
            for k, chip in enumerate(chips):
                part = src[a].at[_shard_of(chip)] if a < ns else src[a]
                out.append(_remote(part, dst[a].at[k], send_sem.at[a, k], recv_sem.at[a, k], (*chip, c)))
        return out

    out_shape = [jax.ShapeDtypeStruct((3,) + g.shape[1:], g.dtype) for g in slabbed]
    out_shape += [jax.ShapeDtypeStruct((3,) + g.shape, g.dtype) for g in small]
    return _Comm(arrays, out_shape, [pltpu.SemaphoreType.DMA((n, 3))] * 2, copies)


def _send_comm(arrays):
    n = len(arrays)

    def copies(src, dst, sems):
        send_sem, recv_sem = sems
        x, y, c, _ = _place()
        return [_remote(src[a], dst[a], send_sem.at[a], recv_sem.at[a], (x, y, 1 - c)) for a in range(n)]

    return _Comm(arrays, [jax.ShapeDtypeStruct(h.shape, h.dtype) for h in arrays],
                 [pltpu.SemaphoreType.DMA((n,))] * 2, copies)


def _row_chunk(rows):
    for cand in (256, 176, 128, 64):
        if rows % cand == 0:
            return cand
    return rows


def _per_shape(fn, *lists):
    groups = {}
    for i, items in enumerate(zip(*lists)):
        groups.setdefault(tuple(a.shape for a in items), []).append(i)
    out = [None] * len(lists[0])
    for idx in groups.values():
        for i, r in zip(idx, fn(*([lst[i] for i in idx] for lst in lists))):
            out[i] = r
    return out


def _add_own_half(stacks, receiveds, c):
    n = len(stacks)
    _, r2, cols = receiveds[0].shape

    def body(c_ref, *refs):
        del c_ref
        for a in range(n):
            refs[2 * n + a][...] = (refs[a][...] + refs[n + a][...]).astype(BF16)

    own = pl.BlockSpec((1, r2, cols), lambda s, c_ref: (s, c_ref[0], 0))
    got = pl.BlockSpec((1, r2, cols), lambda s, c_ref: (s, 0, 0))
    return pl.pallas_call(
        body, name="add_own_half",
        grid_spec=pltpu.PrefetchScalarGridSpec(
            num_scalar_prefetch=1, grid=(N_SHARDS,), in_specs=[own] * n + [got] * n, out_specs=(got,) * n),
        out_shape=(jax.ShapeDtypeStruct(receiveds[0].shape, BF16),) * n,
        compiler_params=_params(("parallel",)),
    )(c, *stacks, *receiveds)


def _chip_order_sum(me, own, s0, s1, s2):
    terms = []
    for s in range(N_SHARDS):
        d = jnp.bitwise_xor(me, s)
        terms.append(jnp.where(d == 0, own, jnp.where(d == 2, s0, jnp.where(d == 1, s1, s2))))
    return ((terms[0] + terms[1]) + terms[2]) + terms[3]


def _sum_chips(own_stacks, slots, me):
    n = len(slots)
    _, rows, cols = slots[0].shape
    rc = _row_chunk(rows)

    def body(me_ref, *refs):
        del me_ref
        for a in range(n):
            own_r, s_r = refs[a], refs[n + a]
            refs[2 * n + a][...] = (((own_r[0].astype(F32) + s_r[0].astype(F32)) + s_r[1].astype(F32))
                                    + s_r[2].astype(F32))

    own = pl.BlockSpec((1, rc, cols), lambda i, me_ref: (me_ref[0], i, 0))
    three = pl.BlockSpec((3, rc, cols), lambda i, me_ref: (0, i, 0))
    total = pl.BlockSpec((rc, cols), lambda i, me_ref: (i, 0))
    return pl.pallas_call(
        body, name="sum_chips",
        grid_spec=pltpu.PrefetchScalarGridSpec(
            num_scalar_prefetch=1, grid=(rows // rc,), in_specs=[own] * n + [three] * n, out_specs=(total,) * n),
        out_shape=(jax.ShapeDtypeStruct((rows, cols), F32),) * n,
        compiler_params=_params(("parallel",)),
    )(me, *own_stacks, *slots)


def _small_pair_sums(mine, theirs):
    n = len(mine)

    def body(*refs):
        for a in range(n):
            refs[2 * n + a][...] = refs[a][...] + refs[n + a][...]

    vm = pl.BlockSpec(memory_space=pltpu.VMEM)
    return pl.pallas_call(
        body, name="small_pair_sums", out_shape=tuple(jax.ShapeDtypeStruct(g.shape, g.dtype) for g in mine),
        in_specs=[vm] * (2 * n), out_specs=(vm,) * n,
        compiler_params=pltpu.CompilerParams(vmem_limit_bytes=VMEM_LIMIT),
    )(*mine, *theirs)


def _adam_math(w, g, m, v):
    m = ADAM_B1 * m + (1.0 - ADAM_B1) * g
    v = ADAM_B2 * v + (1.0 - ADAM_B2) * (g * g)
    m_hat = m / (1.0 - ADAM_B1 ** ADAM_STEP)
    v_hat = v / (1.0 - ADAM_B2 ** ADAM_STEP)
    delta = -ADAM_LR * (m_hat / (jnp.sqrt(v_hat) + ADAM_EPS) + ADAM_WD * w)
    return delta, m, v


def _small_totals(me, conv_stack, conv_slots, packed, packed_slots):
    def body(me_ref, cs_r, cslot_r, p_r, pslot_r, conv_r, tot_r):
        me_ = me_ref[0]
        conv_r[...] = _chip_order_sum(me_, cs_r[me_], cslot_r[0], cslot_r[1], cslot_r[2])
        tot_r[...] = _chip_order_sum(me_, p_r[...], pslot_r[0], pslot_r[1], pslot_r[2])

    vm = pl.BlockSpec(memory_space=pltpu.VMEM)
    return pl.pallas_call(
        body, name="small_totals",
        out_shape=(jax.ShapeDtypeStruct(conv_stack.shape[1:], F32), jax.ShapeDtypeStruct(packed.shape, F32)),
        in_specs=[pl.BlockSpec(memory_space=pltpu.SMEM)] + [vm] * 4, out_specs=(vm, vm),
    )(me, conv_stack, conv_slots, packed, packed_slots)


def _adam_small(gs, ws, ms, vs):
    n = len(gs)

    def body(*refs):
        for a in range(n):
            g_r, w_r, m_r, v_r = (refs[i * n + a] for i in range(4))
            d_r, nm_r, nv_r = (refs[(4 + i) * n + a] for i in range(3))
            d_r[...], nm_r[...], nv_r[...] = _adam_math(w_r[...], g_r[...], m_r[...], v_r[...])

    vm = pl.BlockSpec(memory_space=pltpu.VMEM)
    shapes = tuple(jax.ShapeDtypeStruct(w.shape, F32) for w in ws)
    out = pl.pallas_call(
        body, name="adam_small", out_shape=shapes * 3, in_specs=[vm] * (4 * n), out_specs=(vm,) * (3 * n),
        compiler_params=pltpu.CompilerParams(vmem_limit_bytes=VMEM_LIMIT),
    )(*gs, *ws, *ms, *vs)
    return out[:n], out[n:2 * n], out[2 * n:]


def _adam_big(ws, mines, others, ms, vs, c):
    n = len(ws)
    r2, cols = mines[0].shape
    rc = _row_chunk(r2)
    nch = r2 // rc

    def body(c_ref, *refs):
        mine_is_here = pl.program_id(0) == c_ref[0]
        for a in range(n):
            w_r, mine_r, other_r, m_r, v_r = (refs[i * n + a] for i in range(5))
            g_r, d_r, nm_r, nv_r = (refs[(5 + i) * n + a] for i in range(4))
            g = jnp.where(mine_is_here, mine_r[...], other_r[...])
            g_r[...] = g
            d_r[...], nm_r[...], nv_r[...] = _adam_math(w_r[...], g, m_r[...], v_r[...])

    full = pl.BlockSpec((rc, cols), lambda h, i, c_ref: (h * nch + i, 0))
    half = pl.BlockSpec((rc, cols), lambda h, i, c_ref: (i, 0))
    shape = jax.ShapeDtypeStruct((2 * r2, cols), F32)
    out = pl.pallas_call(
        body, name="adam_big",
        grid_spec=pltpu.PrefetchScalarGridSpec(
            num_scalar_prefetch=1, grid=(2, nch),
            in_specs=[full] * n + [half] * (2 * n) + [full] * (2 * n), out_specs=(full,) * (4 * n)),
        out_shape=(shape,) * (4 * n), compiler_params=_params(("parallel", "parallel")),
    )(c, *ws, *mines, *others, *ms, *vs)
    return [tuple(out[i * n + a] for i in range(4)) for a in range(n)]


def kernel(x, w_in, b_in, ssm_lambda_re, ssm_lambda_im, ssm_log_dt, ssm_b_re, ssm_b_im, ssm_c_re, ssm_c_im, ssm_d, glu_w, glu_b, w_ssm_out, conv_w, w_conv_out, w_o, ln1_g, ln1_b, w_gate, w_up, w_down, ln2_g, ln2_b, loss_target, m_w_in, m_b_in, m_ssm_lambda_re, m_ssm_lambda_im, m_ssm_log_dt, m_ssm_b_re, m_ssm_b_im, m_ssm_c_re, m_ssm_c_im, m_ssm_d, m_glu_w, m_glu_b, m_w_ssm_out, m_conv_w, m_w_conv_out, m_w_o, m_ln1_g, m_ln1_b, m_w_gate, m_w_up, m_w_down, m_ln2_g, m_ln2_b, v_w_in, v_b_in, v_ssm_lambda_re, v_ssm_lambda_im, v_ssm_log_dt, v_ssm_b_re, v_ssm_b_im, v_ssm_c_re, v_ssm_c_im, v_ssm_d, v_glu_w, v_glu_b, v_w_ssm_out, v_conv_w, v_w_conv_out, v_w_o, v_ln1_g, v_ln1_b, v_w_gate, v_w_up, v_w_down, v_ln2_g, v_ln2_b):
    given = dict(locals())
    w = {n: given[n] for n in WEIGHTS}
    m = {n: given["m_" + n] for n in WEIGHTS}
    v = {n: given["v_" + n] for n in WEIGHTS}

    flip = lambda n, a: a.T if n in ("w_gate", "w_up") else a
    shards = {n: flip(n, w[n][0]).astype(BF16) for n in BIG}
    shards["conv_w"] = jnp.pad(conv_w[0], ((0, SUBLANES - 3), (0, 0)))
    c_arr = jnp.reshape(lax.axis_index("c"), (1,)).astype(jnp.int32)
    me = _shard_of((lax.axis_index("x"), lax.axis_index("y")))
    me_arr = jnp.reshape(me, (1,)).astype(jnp.int32)
    dx, pairs, conv_total, packed_total = _device_step(
        x[0], loss_target[0], {n: w[n] for n in SMALL}, shards, c_arr, me_arr)

    grad = _unpack_small(packed_total)
    loss_total = grad.pop("loss")[0]
    grad["conv_w"] = conv_total[:3][None]
    small_names = ("conv_w",) + SMALL
    swap = lambda n, a: a.transpose(0, 1, 3, 2) if n in ("ssm_b_re", "ssm_b_im") else a
    ds, nms, nvs = _adam_small(*([swap(n, d[n]) for n in small_names] for d in (grad, w, m, v)))
    delta, new_m, new_v = {}, {}, {}
    for i, n in enumerate(small_names):
        delta[n], new_m[n], new_v[n] = swap(n, ds[i]), swap(n, nms[i]), swap(n, nvs[i])
    updated = _per_shape(
        lambda *a: _adam_big(*a, c_arr),
        [flip(n, w[n][0]) for n in BIG], [pairs[n][0] for n in BIG], [pairs[n][1] for n in BIG],
        [flip(n, m[n][0]) for n in BIG], [flip(n, v[n][0]) for n in BIG])
    for n, results in zip(BIG, updated):
        grad[n], delta[n], new_m[n], new_v[n] = (flip(n, r)[None] for r in results)

    return (loss_total, dx[None], *[grad[n] for n in WEIGHTS], *[delta[n] for n in WEIGHTS],
            *[new_m[n] for n in WEIGHTS], *[new_v[n] for n in WEIGHTS])
```

```python
import functools
import math

import jax
import jax.numpy as jnp
from jax import lax
from jax.experimental import pallas as pl
from jax.experimental.pallas import tpu as pltpu

F32 = jnp.float32
BF16 = jnp.bfloat16

D_MODEL = 1024
IN_COLS = 4096
SSM_W = 512
N_GROUPS = 32
N_STATE = 64
GROUP_C = 16
STATE_COLS = N_GROUPS * N_STATE
STRIP = 512
N_STRIPS = STATE_COLS // STRIP
FFN_SHARD = 704
N_SHARDS = 4
ALPHA = 2.0 ** 0.25
LN_EPS = 1e-5
GELU_K = math.sqrt(2.0 / math.pi)
GELU_C = 0.044715

ADAM_LR = 0.001
ADAM_B1 = 0.9
ADAM_B2 = 0.999
ADAM_EPS = 1e-08
ADAM_WD = 0.01
ADAM_STEP = 10

V7X_VMEM_BYTES = 64 * 1024 * 1024
VMEM_LIMIT = V7X_VMEM_BYTES - 8 * 1024 * 1024
SUBLANES = 8
N_POWERS = 128

MESH = pl.DeviceIdType.MESH
ANY = pl.BlockSpec(memory_space=pl.ANY)


def _dot(a, b):
    return jnp.dot(a.astype(BF16), b.astype(BF16), preferred_element_type=F32)


def _dot_t(a, b):
    return lax.dot_general(a.astype(BF16), b.astype(BF16), (((1,), (1,)), ((), ())),
                           preferred_element_type=F32)


def _tdot(a, b):
    return lax.dot_general(a.astype(BF16), b.astype(BF16), (((0,), (0,)), ((), ())),
                           preferred_element_type=F32)


def _sigmoid(v):
    return 1.0 / (1.0 + jnp.exp(-v))


def _split3(v):
    hi = v.astype(BF16)
    r1 = v - hi.astype(F32)
    mid = r1.astype(BF16)
    lo = (r1 - mid.astype(F32)).astype(BF16)
    return hi, mid, lo


def _const(shape):
    nd = len(shape)
    return pl.BlockSpec(shape, lambda *_: (0,) * nd)


def _params(sem, vmem=VMEM_LIMIT):
    return pltpu.CompilerParams(dimension_semantics=sem, vmem_limit_bytes=vmem)


def _gelu_parts(v):
    inner = GELU_K * (v + GELU_C * v * v * v)
    t = jnp.tanh(inner)
    g = 0.5 * v * (1.0 + t)
    dg = 0.5 * (1.0 + t) + 0.5 * v * (1.0 - t * t) * GELU_K * (1.0 + 3.0 * GELU_C * v * v)
    return g, dg


class _Comm:
    def __init__(self, inputs, out_shape, sems, copies, aliased=False, bounce=None):
        self.inputs, self.out_shape, self.sems = list(inputs), tuple(out_shape), list(sems)
        self.copies, self.aliased, self.bounce = copies, aliased, bounce


def _launch(body, comms, *, name, grid, in_specs, out_specs, out_shape, scratch_shapes=(), aliases=None, sem=None):
    comms = list(comms)
    n_in, n_out, n_scr = len(in_specs), len(out_specs), len(scratch_shapes)
    aliases = dict(aliases or {})
    layout = []
    p_in, p_out, p_sem = n_in, n_out, 0
    for cm in comms:
        layout.append((p_in, p_out, p_sem))
        if cm.aliased:
            for i in range(len(cm.inputs)):
                aliases[p_in + i] = p_out + i
        p_in, p_out, p_sem = p_in + len(cm.inputs), p_out + len(cm.out_shape), p_sem + len(cm.sems)
    tot_in, tot_out = p_in, p_out

    def fused(*refs):
        ins, outs = refs[:tot_in], refs[tot_in:tot_in + tot_out]
        scr = refs[tot_in + tot_out:tot_in + tot_out + n_scr]
        sems = refs[tot_in + tot_out + n_scr:]

        def descriptors(kind):
            out = []
            for cm, (a, b, s) in zip(comms, layout):
                make = cm.copies if kind == "remote" else cm.bounce
                if make is not None:
                    out += make(ins[a:a + len(cm.inputs)], outs[b:b + len(cm.out_shape)], sems[s:s + len(cm.sems)])
            return out

        steps = [pl.program_id(d) for d in range(len(grid))]
        first = functools.reduce(jnp.logical_and, [s == 0 for s in steps]) if grid else None
        last = functools.reduce(jnp.logical_and, [s == g - 1 for s, g in zip(steps, grid)]) if grid else None

        def start():
            for cp in descriptors("remote"):
                cp.start()
            for to_vmem, _ in descriptors("local"):
                to_vmem.start()

        def finish():
            for to_vmem, to_hbm in descriptors("local"):
                to_vmem.wait()
                to_hbm.start()
            for cp in descriptors("remote"):
                cp.wait()
            for _, to_hbm in descriptors("local"):
                to_hbm.wait()

        if comms:
            pl.when(first)(start) if grid else start()
        if body is not None:
            body(*ins[:n_in], *outs[:n_out], *scr)
        if comms:
            pl.when(last)(finish) if grid else finish()

    specs_in = list(in_specs) + [ANY] * (tot_in - n_in)
    specs_out = tuple(out_specs) + (ANY,) * (tot_out - n_out)
    shapes = tuple(out_shape) + tuple(s for cm in comms for s in cm.out_shape)
    scratch = list(scratch_shapes) + [s for cm in comms for s in cm.sems]
    if comms or sem is None:
        sem = ("arbitrary",) * len(grid)
    kwargs = dict(grid=grid) if grid else {}
    call = pl.pallas_call(fused, name=name, out_shape=shapes, in_specs=specs_in, out_specs=specs_out,
                          scratch_shapes=scratch, input_output_aliases=aliases,
                          compiler_params=_params(sem) if grid else None, **kwargs)

    def run(*args):
        out = call(*args, *(a for cm in comms for a in cm.inputs))
        results, rest = out[:n_out], out[n_out:]
        per_comm = []
        for cm in comms:
            per_comm.append(rest[:len(cm.out_shape)])
            rest = rest[len(cm.out_shape):]
        return results, per_comm

    return run


def _ssm_discretise(lr, li, ldt, lr16, li16, ldt16, brt, bit):
    def lam_bar(lr_, li_, ldt_):
        dt = jnp.exp(ldt_)
        mag = jnp.exp(lr_ * dt)
        return mag * jnp.cos(li_ * dt), mag * jnp.sin(li_ * dt)

    lb_re, lb_im = lam_bar(lr, li, ldt)
    l16_re, l16_im = lam_bar(lr16, li16, ldt16)
    den = lr16 * lr16 + li16 * li16
    num_re = l16_re - 1.0
    fr = (num_re * lr16 + l16_im * li16) / den
    fi = (l16_im * lr16 - num_re * li16) / den
    bb_re = fr * brt - fi * bit
    bb_im = fr * bit + fi * brt
    return lb_re, lb_im, bb_re, bb_im


def _strip_selectors():
    p = lax.broadcasted_iota(jnp.int32, (N_STATE, STRIP), 0)
    col = lax.broadcasted_iota(jnp.int32, (N_STATE, STRIP), 1)
    rep = ((col & (N_STATE - 1)) == p).astype(BF16)
    row = lax.broadcasted_iota(jnp.int32, (SSM_W, STRIP), 0)
    col2 = lax.broadcasted_iota(jnp.int32, (SSM_W, STRIP), 1)
    mask = (((row >> 4) & 7) == (col2 >> 6))
    return rep, mask


def _ssm_prepare(lr, li, ldt, lr16, li16, ldt16, brt, bit, cre, cim):
    def body(lr_r, li_r, ldt_r, lr16_r, li16_r, ldt16_r, brt_r, bit_r, cre_r, cim_r,
             pwr_r, pwi_r, bbr_r, bbi_r, ctr_r, cti_r):
        lb_re, lb_im, bb_re, bb_im = _ssm_discretise(
            lr_r[...], li_r[...], ldt_r[...], lr16_r[...], li16_r[...], ldt16_r[...], brt_r[...], bit_r[...])
        pr, pi_ = lb_re, lb_im
        pwr_r[0] = pr
        pwi_r[0] = pi_
        for k in range(1, N_POWERS):
            pr, pi_ = pr * lb_re - pi_ * lb_im, pr * lb_im + pi_ * lb_re
            pwr_r[k] = pr
            pwi_r[k] = pi_
        rep, mask = _strip_selectors()
        for src, dst in ((bb_re, bbr_r), (bb_im, bbi_r), (cre_r[...], ctr_r), (cim_r[...], cti_r)):
            wide = jnp.dot(src.astype(BF16), rep, preferred_element_type=F32)
            dst[...] = jnp.where(mask, wide, 0.0).astype(BF16)

    vm = pl.BlockSpec(memory_space=pltpu.VMEM)
    return pl.pallas_call(
        body, name="ssm_prepare",
        out_shape=(jax.ShapeDtypeStruct((N_POWERS, N_GROUPS, N_STATE), F32),) * 2
        + (jax.ShapeDtypeStruct((SSM_W, STRIP), BF16),) * 4,
        in_specs=[vm] * 10, out_specs=(vm,) * 6,
    )(lr, li, ldt, lr16, li16, ldt16, brt, bit, cre, cim)


def _scan_tables(pwr, pwi):
    pr = pwr.reshape(N_POWERS, STATE_COLS)
    pi_ = pwi.reshape(N_POWERS, STATE_COLS)
    rows8 = lambda v: jnp.broadcast_to(v[None], (SUBLANES, STATE_COLS))
    tab_a = jnp.stack([rows8(pr[0]), rows8(pi_[0]), rows8(pr[-1]), rows8(pi_[-1])])
    tab_p = jnp.stack([jnp.repeat(pr, SUBLANES, axis=0), jnp.repeat(pi_, SUBLANES, axis=0)])
    return tab_a, tab_p


def _ssm_param_grads(lr, li, ldt, lr16, li16, ldt16, brt, bit, dlbr, dlbi, dbbr, dbbi, dctr, dcti):
    def body(lr_r, li_r, ldt_r, lr16_r, li16_r, ldt16_r, brt_r, bit_r,
             dlbr_r, dlbi_r, dbbr_r, dbbi_r, dctr_r, dcti_r,
             glr_r, gli_r, gldt_r, gbrt_r, gbit_r, gcre_r, gcim_r):
        rep, mask = _strip_selectors()

        def fold(acc):
            return sum(lax.dot_general(t, rep, (((1,), (1,)), ((), ())), preferred_element_type=F32)
                       for t in _split3(jnp.where(mask, acc, 0.0)))

        g_lb_re = jnp.sum(dlbr_r[...], axis=0)
        g_lb_im = jnp.sum(dlbi_r[...], axis=0)
        g_bb_re = fold(dbbr_r[...])
        g_bb_im = fold(dbbi_r[...])
        gcre_r[...] = fold(dctr_r[...])
        gcim_r[...] = fold(dcti_r[...])
        prim = (lr_r[...], li_r[...], ldt_r[...], lr16_r[...], li16_r[...], ldt16_r[...], brt_r[...], bit_r[...])
        _, vjp = jax.vjp(_ssm_discretise, *prim)
        g_lr, g_li, g_ldt, g_lr16, g_li16, g_ldt16, g_brt, g_bit = vjp((g_lb_re, g_lb_im, g_bb_re, g_bb_im))
        grp = lax.broadcasted_iota(jnp.int32, (N_GROUPS, SSM_W), 0)
        rw = lax.broadcasted_iota(jnp.int32, (N_GROUPS, SSM_W), 1)
        gsum = ((rw >> 4) == grp).astype(BF16)

        def group_sum(v):
            return sum(jnp.dot(gsum, t, preferred_element_type=F32) for t in _split3(v))

        glr_r[...] = g_lr + group_sum(g_lr16)
        gli_r[...] = g_li + group_sum(g_li16)
        gldt_r[...] = g_ldt + jnp.sum(group_sum(g_ldt16), axis=1, keepdims=True)
        gbrt_r[...] = g_brt
        gbit_r[...] = g_bit

    vm = pl.BlockSpec(memory_space=pltpu.VMEM)
    gp = jax.ShapeDtypeStruct((N_GROUPS, N_STATE), F32)
    gb = jax.ShapeDtypeStruct((SSM_W, N_STATE), F32)
    return pl.pallas_call(
        body, name="ssm_param_grads",
        out_shape=(gp, gp, jax.ShapeDtypeStruct((N_GROUPS, 1), F32), gb, gb, gb, gb),
        in_specs=[vm] * 14, out_specs=(vm,) * 7,
    )(lr, li, ldt, lr16, li16, ldt16, brt, bit, dlbr, dlbi, dbbr, dbbi, dctr, dcti)


def _in_proj(x, w_in_st, b_in, comms=()):
    t = x.shape[0]
    tm = 512

    def body(x_r, w_r, b_r, o_r):
        xb = x_r[...].astype(BF16)
        for j in range(N_SHARDS):
            cols = slice(D_MODEL * j, D_MODEL * (j + 1))
            o_r[:, cols] = jnp.dot(xb, w_r[j], preferred_element_type=F32) + b_r[:, cols]

    (proj,), sent = _launch(
        body, comms, name="in_proj", grid=(t // tm,),
        out_shape=(jax.ShapeDtypeStruct((t, IN_COLS), F32),),
        in_specs=[pl.BlockSpec((tm, D_MODEL), lambda i: (i, 0)), _const((N_SHARDS, D_MODEL, D_MODEL)),
                  _const((1, IN_COLS))],
        out_specs=(pl.BlockSpec((tm, IN_COLS), lambda i: (i, 0)),),
        sem=("parallel",),
    )(x, w_in_st, b_in)
    return proj, sent


def _cmul_add(xr, xi, mr, mi, sr, si):
    return xr + (mr * sr - mi * si), xi + (mr * si + mi * sr)


SCAN_STEPS = N_POWERS
SCAN_CHUNK = SUBLANES * SCAN_STEPS


def _interleave(src_r, dst_r):
    for step in range(SCAN_STEPS):
        dst_r[SUBLANES * step:SUBLANES * (step + 1), :] = src_r[pl.ds(step, SUBLANES, stride=SCAN_STEPS), :]


def _deinterleave(src_r, dst_r):
    for step in range(SCAN_STEPS):
        dst_r[pl.ds(step, SUBLANES, stride=SCAN_STEPS), :] = src_r[SUBLANES * step:SUBLANES * (step + 1), :]


def _step_rows(step):
    return pl.ds(pl.multiple_of(step * SUBLANES, SUBLANES), SUBLANES)


def _scan_steps(body, init, by=4):
    def trip(t, carry):
        for u in range(by):
            carry = body(t * by + u, carry)
        return carry

    return lax.fori_loop(0, SCAN_STEPS // by, trip, init)


def _segment_states(first_r, first_i, ends_r, ends_i, a64_r, a64_i, order):
    row = lax.broadcasted_iota(jnp.int32, ends_r.shape, 0)
    cur_r, cur_i = first_r, first_i
    ent_r = jnp.zeros_like(ends_r)
    ent_i = jnp.zeros_like(ends_i)
    for s in order:
        ent_r = jnp.where(row == s, jnp.broadcast_to(cur_r, ends_r.shape), ent_r)
        ent_i = jnp.where(row == s, jnp.broadcast_to(cur_i, ends_i.shape), ent_i)
        cur_r, cur_i = _cmul_add(ends_r[s:s + 1, :], ends_i[s:s + 1, :], a64_r, a64_i, cur_r, cur_i)
    return ent_r, ent_i, cur_r, cur_i


def _ssm_forward(proj, bbr, bbi, ctr, cti, d_skip, tab_a, tab_p, comms=(), tc=SCAN_CHUNK):
    t = proj.shape[0]

    def body(u_r, bbr_r, bbi_r, ctr_r, cti_r, d_r, ta_r, tp_r, xsr_r, xsi_r, y_r, ui_s, yi_s, car_r, car_i):
        @pl.when(pl.program_id(1) == 0)
        def _():
            car_r[...] = jnp.zeros_like(car_r)
            car_i[...] = jnp.zeros_like(car_i)

        _interleave(u_r, ui_s)
        u = ui_s[...]
        xsr_r[...] = _dot(u, bbr_r[...])
        xsi_r[...] = _dot(u, bbi_r[...])
        a_r, a_i = ta_r[0], ta_r[1]

        def local(step, carry):
            rows = _step_rows(step)
            xr, xi = _cmul_add(xsr_r[rows, :], xsi_r[rows, :], a_r, a_i, *carry)
            xsr_r[rows, :] = xr
            xsi_r[rows, :] = xi
            return xr, xi

        zero = jnp.zeros((SUBLANES, STRIP), F32)
        ends_r, ends_i = _scan_steps(local, (zero, zero))
        ent_r, ent_i, out_r, out_i = _segment_states(
            car_r[...], car_i[...], ends_r, ends_i, ta_r[2, 0:1, :], ta_r[3, 0:1, :], range(SUBLANES))
        car_r[...] = out_r
        car_i[...] = out_i

        def entering(step, _):
            rows = _step_rows(step)
            xr, xi = _cmul_add(xsr_r[rows, :], xsi_r[rows, :], tp_r[0, rows, :], tp_r[1, rows, :], ent_r, ent_i)
            xsr_r[rows, :] = xr
            xsi_r[rows, :] = xi
            return 0

        _scan_steps(entering, 0)
        yi_s[...] = _dot_t(xsr_r[...], ctr_r[...]) - _dot_t(xsi_r[...], cti_r[...]) + d_r[...] * u
        _deinterleave(yi_s, y_r)

    strip_mat = pl.BlockSpec((128, STRIP), lambda j, k: (j, 0))
    states = pl.BlockSpec((tc, STRIP), lambda j, k: (k, j))
    return _launch(
        body, comms, name="ssm_forward", grid=(N_STRIPS, t // tc),
        out_shape=(jax.ShapeDtypeStruct((t, STATE_COLS), F32), jax.ShapeDtypeStruct((t, STATE_COLS), F32),
                   jax.ShapeDtypeStruct((t, SSM_W), F32)),
        in_specs=[pl.BlockSpec((tc, 128), lambda j, k: (k, j)),
                  strip_mat, strip_mat, strip_mat, strip_mat,
                  pl.BlockSpec((1, 128), lambda j, k: (0, j)),
                  pl.BlockSpec((4, SUBLANES, STRIP), lambda j, k: (0, 0, j)),
                  pl.BlockSpec((2, tc, STRIP), lambda j, k: (0, 0, j))],
        out_specs=(states, states, pl.BlockSpec((tc, 128), lambda j, k: (k, j))),
        scratch_shapes=[pltpu.VMEM((tc, 128), F32), pltpu.VMEM((tc, 128), F32),
                        pltpu.VMEM((1, STRIP), F32), pltpu.VMEM((1, STRIP), F32)],
        sem=("parallel", "arbitrary"),
    )(proj, bbr, bbi, ctr, cti, d_skip, tab_a, tab_p)


def _shift_down(v, prev, n):
    row = lax.broadcasted_iota(jnp.int32, v.shape, 0)
    out = pltpu.roll(v, n, 0)
    for r in range(n):
        src = prev[SUBLANES - n + r:SUBLANES - n + r + 1, :]
        out = jnp.where(row == r, jnp.broadcast_to(src, v.shape), out)
    return out


def _shift_up(v, nxt, n):
    rows = v.shape[0]
    row = lax.broadcasted_iota(jnp.int32, v.shape, 0)
    out = pltpu.roll(v, rows - n, 0)
    for r in range(n):
        src = nxt[r:r + 1, :]
        out = jnp.where(row == rows - n + r, jnp.broadcast_to(src, v.shape), out)
    return out


def _conv3(q, q_prev, w):
    return w[2:3, :] * q + w[1:2, :] * _shift_down(q, q_prev, 1) + w[0:1, :] * _shift_down(q, q_prev, 2)


def _mixer_forward(x, proj, ya0, glu_w, glu_b, wso_st, conv_w8, wco_st, w_o, comms=(), tm=256):
    t = x.shape[0]
    hb = tm // SUBLANES

    def body(x_r, ya0_r, h_r, cg_r, bg_r, ga_r, gb_r, hp_r, cgp_r,
             glu_w_r, glu_b_r, wso_r, cw_r, wco_r, wo_r, xh_r, rstd_r, ya_r, yb_r):
        i = pl.program_id(0)
        g, _ = _gelu_parts(ya0_r[...])
        ya1 = g * _sigmoid(_dot(g, glu_w_r[...]) + glu_b_r[...])
        q = cg_r[...] * h_r[...]
        q_prev = jnp.where(i > 0, cgp_r[...] * hp_r[...], 0.0)
        yb0 = bg_r[...] * _conv3(q, q_prev, cw_r[...])
        for j in range(N_SHARDS):
            ya_r[:, 256 * j:256 * (j + 1)] = _dot(ya1, wso_r[j])
            yb_r[:, 256 * j:256 * (j + 1)] = _dot(yb0, wco_r[j])
        merged = _sigmoid(ga_r[...]) * ya_r[...] + _sigmoid(gb_r[...]) * yb_r[...]
        r1 = ALPHA * x_r[...] + _dot(merged, wo_r[...])
        mu = jnp.mean(r1, axis=-1, keepdims=True)
        cen = r1 - mu
        rstd = lax.rsqrt(jnp.mean(cen * cen, axis=-1, keepdims=True) + LN_EPS)
        xh_r[...] = cen * rstd
        rstd_r[...] = rstd

    def col(w, c):
        return pl.BlockSpec((tm, w), lambda i: (i, c))

    def prev(c):
        return pl.BlockSpec((SUBLANES, SSM_W), lambda i: (jnp.maximum(i * hb - 1, 0), c))

    return _launch(
        body, comms, name="mixer_forward", grid=(t // tm,),
        out_shape=(jax.ShapeDtypeStruct((t, D_MODEL), F32), jax.ShapeDtypeStruct((t, 1), F32),
                   jax.ShapeDtypeStruct((t, D_MODEL), F32), jax.ShapeDtypeStruct((t, D_MODEL), F32)),
        in_specs=[col(D_MODEL, 0), col(SSM_W, 0), col(SSM_W, 1), col(SSM_W, 2), col(SSM_W, 3),
                  col(D_MODEL, 2), col(D_MODEL, 3), prev(1), prev(2),
                  _const((SSM_W, SSM_W)), _const((1, SSM_W)), _const((N_SHARDS, SSM_W, 256)),
                  _const((SUBLANES, SSM_W)), _const((N_SHARDS, SSM_W, 256)), _const((D_MODEL, D_MODEL))],
        out_specs=(col(D_MODEL, 0), pl.BlockSpec((tm, 1), lambda i: (i, 0)), col(D_MODEL, 0), col(D_MODEL, 0)),
        sem=("parallel",),
    )(x, ya0, proj, proj, proj, proj, proj, proj, proj, glu_w, glu_b, wso_st, conv_w8, wco_st, w_o)


def _layer_norm_bwd(dxhat, xhat, rstd):
    m1 = jnp.mean(dxhat, axis=-1, keepdims=True)
    m2 = jnp.mean(dxhat * xhat, axis=-1, keepdims=True)
    return rstd * (dxhat - m1 - xhat * m2)


def _ffn_step(xhat1, rstd1, target, ln1_g, ln1_b, ln2_g, ln2_b, wg_st, wu_st, wd_st, tm=256):
    t = xhat1.shape[0]

    def body(xh_r, rstd_r, tgt_r, g1_r, b1_r, g2_r, b2_r, wg_r, wu_r, wd_r,
             loss_r, dr1_r, x1b_r, dr2b_r, hid_r, dhg_r, dhu_r, dg2_r, db2_r, dg1_r, db1_r,
             hg_s, hu_s):
        @pl.when(pl.program_id(0) == 0)
        def _():
            for r in (loss_r, dg2_r, db2_r, dg1_r, db1_r):
                r[...] = jnp.zeros_like(r)

        xhat1_v = xh_r[...]
        x1 = xhat1_v * g1_r[...] + b1_r[...]
        x1b = x1.astype(BF16)
        x1b_r[...] = x1b
        ffn = jnp.zeros((tm, D_MODEL), F32)
        for j in range(N_SHARDS):
            hg = lax.dot_general(x1b, wg_r[j], (((1,), (1,)), ((), ())), preferred_element_type=F32)
            hu = lax.dot_general(x1b, wu_r[j], (((1,), (1,)), ((), ())), preferred_element_type=F32)
            hg_s[j] = hg
            hu_s[j] = hu
            hid = (hg * _sigmoid(hg) * hu).astype(BF16)
            hid_r[j] = hid
            ffn = ffn + jnp.dot(hid, wd_r[j], preferred_element_type=F32)
        r2 = ALPHA * x1 + ffn
        mu = jnp.mean(r2, axis=-1, keepdims=True)
        cen = r2 - mu
        rstd2 = lax.rsqrt(jnp.mean(cen * cen, axis=-1, keepdims=True) + LN_EPS)
        xhat2 = cen * rstd2
        diff = (xhat2 * g2_r[...] + b2_r[...]) - tgt_r[...]
        loss_r[...] += 0.5 * jnp.sum(jnp.mean(diff * diff, axis=-1, keepdims=True), axis=0, keepdims=True)
        dy = diff * (1.0 / D_MODEL)
        dg2_r[...] += jnp.sum(dy * xhat2, axis=0, keepdims=True)
        db2_r[...] += jnp.sum(dy, axis=0, keepdims=True)
        dr2 = _layer_norm_bwd(dy * g2_r[...], xhat2, rstd2)
        dr2b = dr2.astype(BF16)
        dr2b_r[...] = dr2b
        dx1 = ALPHA * dr2
        for j in range(N_SHARDS):
            dhid = lax.dot_general(dr2b, wd_r[j], (((1,), (1,)), ((), ())), preferred_element_type=F32)
            hg = hg_s[j]
            hu = hu_s[j]
            sg = _sigmoid(hg)
            dhu = (dhid * (hg * sg)).astype(BF16)
            dhg = (dhid * hu * (sg * (1.0 + hg * (1.0 - sg)))).astype(BF16)
            dhg_r[j] = dhg
            dhu_r[j] = dhu
            dx1 = dx1 + jnp.dot(dhg, wg_r[j], preferred_element_type=F32)
            dx1 = dx1 + jnp.dot(dhu, wu_r[j], preferred_element_type=F32)
        dg1_r[...] += jnp.sum(dx1 * xhat1_v, axis=0, keepdims=True)
        db1_r[...] += jnp.sum(dx1, axis=0, keepdims=True)
        dr1_r[...] = _layer_norm_bwd(dx1 * g1_r[...], xhat1_v, rstd_r[...])

    tile = pl.BlockSpec((tm, D_MODEL), lambda i: (i, 0))
    hidden = pl.BlockSpec((N_SHARDS, tm, FFN_SHARD), lambda i: (0, i, 0))
    vec = _const((1, D_MODEL))
    hid_shape = jax.ShapeDtypeStruct((N_SHARDS, t, FFN_SHARD), BF16)
    vec_shape = jax.ShapeDtypeStruct((1, D_MODEL), F32)
    return pl.pallas_call(
        body, name="ffn_step", grid=(t // tm,),
        out_shape=(jax.ShapeDtypeStruct((1, 1), F32), jax.ShapeDtypeStruct((t, D_MODEL), F32),
                   jax.ShapeDtypeStruct((t, D_MODEL), BF16), jax.ShapeDtypeStruct((t, D_MODEL), BF16),
                   hid_shape, hid_shape, hid_shape, vec_shape, vec_shape, vec_shape, vec_shape),
        in_specs=[tile, pl.BlockSpec((tm, 1), lambda i: (i, 0)), tile, vec, vec, vec, vec,
                  _const((N_SHARDS, FFN_SHARD, D_MODEL)), _const((N_SHARDS, FFN_SHARD, D_MODEL)),
                  _const((N_SHARDS, FFN_SHARD, D_MODEL))],
        out_specs=(_const((1, 1)), tile, tile, tile, hidden, hidden, hidden, vec, vec, vec, vec),
        scratch_shapes=[pltpu.VMEM((N_SHARDS, tm, FFN_SHARD), F32), pltpu.VMEM((N_SHARDS, tm, FFN_SHARD), F32)],
        compiler_params=_params(("arbitrary",)),
    )(xhat1, rstd1, target, ln1_g, ln1_b, ln2_g, ln2_b, wg_st, wu_st, wd_st)


def _ffn_weight_grads(x1b, dr2b, hid, dhg, dhu, tk=2048):
    t = x1b.shape[0]

    def body(x_r, dr_r, hid_r, dhg_r, dhu_r, gwg_r, gwu_r, gwd_r):
        @pl.when(pl.program_id(1) == 0)
        def _():
            for r in (gwg_r, gwu_r, gwd_r):
                r[...] = jnp.zeros_like(r)

        gwg_r[0] += _tdot(dhg_r[0], x_r[...])
        gwu_r[0] += _tdot(dhu_r[0], x_r[...])
        gwd_r[0] += _tdot(hid_r[0], dr_r[...])

    tile = pl.BlockSpec((tk, D_MODEL), lambda j, k: (k, 0))
    hidden = pl.BlockSpec((1, tk, FFN_SHARD), lambda j, k: (j, k, 0))
    row = pl.BlockSpec((1, FFN_SHARD, D_MODEL), lambda j, k: (j, 0, 0))
    return pl.pallas_call(
        body, name="ffn_weight_grads", grid=(N_SHARDS, t // tk),
        out_shape=(jax.ShapeDtypeStruct((N_SHARDS, FFN_SHARD, D_MODEL), F32),) * 3,
        in_specs=[tile, tile, hidden, hidden, hidden],
        out_specs=(row, row, row),
        compiler_params=_params(("parallel", "arbitrary")),
    )(x1b, dr2b, hid, dhg, dhu)


def _mixer_backward(dr1, proj, ya0, ya, yb, glu_w, glu_b, wso_st, conv_w8, wco_st, w_o, comms=(), tm=256):
    t = dr1.shape[0]
    hb = tm // SUBLANES
    last_block = t // SUBLANES - 1

    def body(dr1_r, dr1n_r, ya0_r, ya_r, yb_r, h_r, cg_r, bg_r, ga_r, gb_r, hp_r, cgp_r, bgn_r, gbn_r,
             glu_w_r, glu_b_r, wso_r, cw_r, wco_r, wo_r,
             dya0_r, dproj_r, dbias_r, gwo_r, gwso_r, gwco_r, gglu_w_r, gglu_b_r, gconv_r):
        i = pl.program_id(0)

        @pl.when(i == 0)
        def _():
            for r in (dbias_r, gwo_r, gwso_r, gwco_r, gglu_w_r, gglu_b_r, gconv_r):
                r[...] = jnp.zeros_like(r)

        dr1_v = dr1_r[...]
        dmerged = _dot_t(dr1_v, wo_r[...])
        sa = _sigmoid(ga_r[...])
        sb = _sigmoid(gb_r[...])
        ya_v = ya_r[...]
        yb_v = yb_r[...]
        gwo_r[...] += _tdot(sa * ya_v + sb * yb_v, dr1_v)
        dya = dmerged * sa
        dyb = dmerged * sb
        dga = dmerged * ya_v * (sa * (1.0 - sa))
        dgb = dmerged * yb_v * (sb * (1.0 - sb))

        g, gelu_grad = _gelu_parts(ya0_r[...])
        s1 = _sigmoid(_dot(g, glu_w_r[...]) + glu_b_r[...])
        ya1 = g * s1
        dya1 = jnp.zeros((tm, SSM_W), F32)
        for j in range(N_SHARDS):
            dya_j = dya[:, 256 * j:256 * (j + 1)]
            gwso_r[j] += _tdot(ya1, dya_j)
            dya1 = dya1 + _dot_t(dya_j, wso_r[j])
        dz1 = dya1 * g * (s1 * (1.0 - s1))
        gglu_b_r[...] += jnp.sum(dz1, axis=0, keepdims=True)
        gglu_w_r[...] += _tdot(g, dz1)
        dya0_r[...] = (dya1 * s1 + _dot_t(dz1, glu_w_r[...])) * gelu_grad

        cw = cw_r[...]
        h = h_r[...]
        cg = cg_r[...]
        bg = bg_r[...]
        q = cg * h
        q_prev = jnp.where(i > 0, cgp_r[...] * hp_r[...], 0.0)
        q1 = _shift_down(q, q_prev, 1)
        q2 = _shift_down(q, q_prev, 2)
        z = cw[2:3, :] * q + cw[1:2, :] * q1 + cw[0:1, :] * q2
        yb0 = bg * z
        dyb0 = jnp.zeros((tm, SSM_W), F32)
        for j in range(N_SHARDS):
            dyb_j = dyb[:, 256 * j:256 * (j + 1)]
            gwco_r[j] += _tdot(yb0, dyb_j)
            dyb0 = dyb0 + _dot_t(dyb_j, wco_r[j])
        dbg = dyb0 * z
        dz = dyb0 * bg
        dyb_n = _dot_t(dr1n_r[...], wo_r[...]) * _sigmoid(gbn_r[...])
        dyb0_n = jnp.zeros((SUBLANES, SSM_W), F32)
        for j in range(N_SHARDS):
            dyb0_n = dyb0_n + _dot_t(dyb_n[:, 256 * j:256 * (j + 1)], wco_r[j])
        dz_next = jnp.where(i < pl.num_programs(0) - 1, dyb0_n * bgn_r[...], 0.0)
        dq = cw[2:3, :] * dz + cw[1:2, :] * _shift_up(dz, dz_next, 1) + cw[0:1, :] * _shift_up(dz, dz_next, 2)
        gconv_r[0:1, :] += jnp.sum(dz * q2, axis=0, keepdims=True)
        gconv_r[1:2, :] += jnp.sum(dz * q1, axis=0, keepdims=True)
        gconv_r[2:3, :] += jnp.sum(dz * q, axis=0, keepdims=True)
        dh = dq * cg
        dcg = dq * h

        dproj_r[:, 0:512] = jnp.zeros((tm, SSM_W), BF16)
        pieces = ((512, dh), (1024, dcg), (1536, dbg), (2048, dga), (3072, dgb))
        for off, val in pieces:
            w = val.shape[1]
            dproj_r[:, off:off + w] = val.astype(BF16)
            dbias_r[:, off:off + w] += jnp.sum(val, axis=0, keepdims=True)

    def col(w, c):
        return pl.BlockSpec((tm, w), lambda i: (i, c))

    def prev(c):
        return pl.BlockSpec((SUBLANES, SSM_W), lambda i: (jnp.maximum(i * hb - 1, 0), c))

    def nxt(w, c):
        return pl.BlockSpec((SUBLANES, w), lambda i: (jnp.minimum((i + 1) * hb, last_block), c))

    sh = jax.ShapeDtypeStruct
    return _launch(
        body, comms, name="mixer_backward", grid=(t // tm,),
        out_shape=(sh((t, SSM_W), F32), sh((t, IN_COLS), BF16), sh((1, IN_COLS), F32),
                   sh((D_MODEL, D_MODEL), F32), sh((N_SHARDS, SSM_W, 256), F32), sh((N_SHARDS, SSM_W, 256), F32),
                   sh((SSM_W, SSM_W), F32), sh((1, SSM_W), F32), sh((SUBLANES, SSM_W), F32)),
        in_specs=[col(D_MODEL, 0), nxt(D_MODEL, 0), col(SSM_W, 0), col(D_MODEL, 0), col(D_MODEL, 0),
                  col(SSM_W, 1), col(SSM_W, 2), col(SSM_W, 3), col(D_MODEL, 2), col(D_MODEL, 3),
                  prev(1), prev(2), nxt(SSM_W, 3), nxt(D_MODEL, 3),
                  _const((SSM_W, SSM_W)), _const((1, SSM_W)), _const((N_SHARDS, SSM_W, 256)),
                  _const((SUBLANES, SSM_W)), _const((N_SHARDS, SSM_W, 256)), _const((D_MODEL, D_MODEL))],
        out_specs=(col(SSM_W, 0), col(IN_COLS, 0), _const((1, IN_COLS)),
                   _const((D_MODEL, D_MODEL)), _const((N_SHARDS, SSM_W, 256)), _const((N_SHARDS, SSM_W, 256)),
                   _const((SSM_W, SSM_W)), _const((1, SSM_W)), _const((SUBLANES, SSM_W))),
        sem=("arbitrary",),
    )(dr1, dr1, ya0, ya, yb, proj, proj, proj, proj, proj, proj, proj, proj, proj,
      glu_w, glu_b, wso_st, conv_w8, wco_st, w_o)


def _cmulc_add(xr, xi, mr, mi, sr, si):
    return xr + (mr * sr + mi * si), xi + (mr * si - mi * sr)


def _ssm_backward(dya0, proj, xsr, xsi, bbr, bbi, ctr, cti, d_skip, tab_a, tab_p, dproj, comms=(), tc=SCAN_CHUNK):
    t = proj.shape[0]
    nk = t // tc

    def body(dy_r, u_r, xsr_r, xsi_r, bbr_r, bbi_r, ctr_r, cti_r, d_r, ta_r, tp_r, dproj_any,
             du_r, dus_r, gbbr_r, gbbi_r, gctr_r, gcti_r, glbr_r, glbi_r, gd_r,
             gr_s, gi_s, dyi_s, ui_s, dui_s, dun_s, car_r, car_i):
        del dproj_any

        @pl.when(pl.program_id(1) == 0)
        def _():
            for r in (car_r, car_i, dus_r, gbbr_r, gbbi_r, gctr_r, gcti_r, glbr_r, glbi_r, gd_r):
                r[...] = jnp.zeros_like(r)

        _interleave(dy_r, dyi_s)
        _interleave(u_r, ui_s)
        dy = dyi_s[...]
        u = ui_s[...]
        gr_s[...] = _dot(dy, ctr_r[...])
        gi_s[...] = -_dot(dy, cti_r[...])
        a_r, a_i = ta_r[0], ta_r[1]

        def local(n, carry):
            rows = _step_rows(SCAN_STEPS - 1 - n)
            gr, gi = _cmulc_add(gr_s[rows, :], gi_s[rows, :], a_r, a_i, *carry)
            gr_s[rows, :] = gr
            gi_s[rows, :] = gi
            return gr, gi

        zero = jnp.zeros((SUBLANES, STRIP), F32)
        ends_r, ends_i = _scan_steps(local, (zero, zero))
        ent_r, ent_i, out_r, out_i = _segment_states(
            car_r[...], car_i[...], ends_r, ends_i, ta_r[2, 0:1, :], -ta_r[3, 0:1, :], range(SUBLANES - 1, -1, -1))
        car_r[...] = out_r
        car_i[...] = out_i

        def entering(n, carry):
            gnr, gni, ar, ai = carry
            rows = _step_rows(SCAN_STEPS - 1 - n)
            power = _step_rows(n)
            gr, gi = _cmulc_add(gr_s[rows, :], gi_s[rows, :], tp_r[0, power, :], tp_r[1, power, :], ent_r, ent_i)
            gr_s[rows, :] = gr
            gi_s[rows, :] = gi
            xr = xsr_r[rows, :]
            xi = xsi_r[rows, :]
            return gr, gi, ar + (xr * gnr + xi * gni), ai + (xr * gni - xi * gnr)

        _, _, ar, ai = _scan_steps(entering, (ent_r, ent_i, zero, zero))
        glbr_r[...] += ar
        glbi_r[...] += ai
        gr = gr_s[...]
        gi = gi_s[...]
        dui_s[...] = _dot_t(gr, bbr_r[...]) + _dot_t(gi, bbi_r[...]) + d_r[...] * dy
        _deinterleave(dui_s, dun_s)
        du = dun_s[...]
        du_r[...] = du.astype(BF16)
        dus_r[...] += jnp.sum(du, axis=0, keepdims=True)
        gd_r[...] += jnp.sum(dy * u, axis=0, keepdims=True)
        gbbr_r[...] += _tdot(u, gr)
        gbbi_r[...] += _tdot(u, gi)
        gctr_r[...] += _tdot(dy, xsr_r[...])
        gcti_r[...] -= _tdot(dy, xsi_r[...])

    def rev(w):
        return pl.BlockSpec((tc, w), lambda j, k: (nk - 1 - k, j))

    strip_mat = pl.BlockSpec((128, STRIP), lambda j, k: (j, 0))
    vec = pl.BlockSpec((1, 128), lambda j, k: (0, j))
    lbacc = pl.BlockSpec((SUBLANES, STRIP), lambda j, k: (0, j))
    sh = jax.ShapeDtypeStruct
    return _launch(
        body, comms, name="ssm_backward", grid=(N_STRIPS, nk),
        out_shape=(sh((t, IN_COLS), BF16), sh((1, SSM_W), F32),
                   sh((SSM_W, STRIP), F32), sh((SSM_W, STRIP), F32), sh((SSM_W, STRIP), F32), sh((SSM_W, STRIP), F32),
                   sh((SUBLANES, STATE_COLS), F32), sh((SUBLANES, STATE_COLS), F32), sh((1, SSM_W), F32)),
        in_specs=[rev(128), rev(128), rev(STRIP), rev(STRIP),
                  strip_mat, strip_mat, strip_mat, strip_mat, vec,
                  pl.BlockSpec((4, SUBLANES, STRIP), lambda j, k: (0, 0, j)),
                  pl.BlockSpec((2, tc, STRIP), lambda j, k: (0, 0, j)), ANY],
        out_specs=(rev(128), vec, strip_mat, strip_mat, strip_mat, strip_mat, lbacc, lbacc, vec),
        scratch_shapes=[pltpu.VMEM((tc, STRIP), F32), pltpu.VMEM((tc, STRIP), F32)]
        + [pltpu.VMEM((tc, 128), F32)] * 4 + [pltpu.VMEM((1, STRIP), F32)] * 2,
        aliases={11: 0}, sem=("parallel", "arbitrary"),
    )(dya0, proj, xsr, xsi, bbr, bbi, ctr, cti, d_skip, tab_a, tab_p, dproj)


def _input_grad(dr1, dproj, w_in_st, comms=(), tm=512):
    t = dr1.shape[0]

    def body(dr1_r, dp_r, w_r, dx_r):
        acc = ALPHA * dr1_r[...]
        for j in range(N_SHARDS):
            acc = acc + lax.dot_general(dp_r[:, D_MODEL * j:D_MODEL * (j + 1)], w_r[j],
                                        (((1,), (1,)), ((), ())), preferred_element_type=F32)
        dx_r[...] = acc

    (dx,), sent = _launch(
        body, comms, name="input_grad", grid=(t // tm,),
        out_shape=(jax.ShapeDtypeStruct((t, D_MODEL), F32),),
        in_specs=[pl.BlockSpec((tm, D_MODEL), lambda i: (i, 0)), pl.BlockSpec((tm, IN_COLS), lambda i: (i, 0)),
                  _const((N_SHARDS, D_MODEL, D_MODEL))],
        out_specs=(pl.BlockSpec((tm, D_MODEL), lambda i: (i, 0)),),
        sem=("parallel",),
    )(dr1, dproj, w_in_st)
    return dx, sent


def _in_weight_grad(x, dproj, comms=(), tk=2048):
    t = x.shape[0]

    def body(x_r, dp_r, gw_r):
        @pl.when(pl.program_id(1) == 0)
        def _():
            gw_r[...] = jnp.zeros_like(gw_r)

        gw_r[0] += _tdot(x_r[...], dp_r[...])

    (g_w_in,), sent = _launch(
        body, comms, name="in_weight_grad", grid=(N_SHARDS, t // tk),
        out_shape=(jax.ShapeDtypeStruct((N_SHARDS, D_MODEL, D_MODEL), F32),),
        in_specs=[pl.BlockSpec((tk, D_MODEL), lambda j, k: (k, 0)), pl.BlockSpec((tk, D_MODEL), lambda j, k: (k, j))],
        out_specs=(pl.BlockSpec((1, D_MODEL, D_MODEL), lambda j, k: (j, 0, 0)),),
        sem=("parallel", "arbitrary"),
    )(x, dproj)
    return g_w_in, sent


MIXER_W = ("glu_w", "w_ssm_out", "w_conv_out", "w_o")
FFN_W = ("w_gate", "w_up", "w_down")


def _device_step(x, target, small, shards, c_arr, me_arr):
    lr, li = small["ssm_lambda_re"][0], small["ssm_lambda_im"][0]
    ldt = small["ssm_log_dt"][0][:, None]
    rep16 = lambda a: jnp.broadcast_to(a[:, None, :], (N_GROUPS, GROUP_C, a.shape[-1])).reshape(SSM_W, a.shape[-1])
    lr16, li16 = rep16(lr), rep16(li)
    ldt16 = rep16(jnp.broadcast_to(ldt, (N_GROUPS, N_STATE)))
    brt = small["ssm_b_re"][0].transpose(0, 2, 1).reshape(SSM_W, N_STATE)
    bit = small["ssm_b_im"][0].transpose(0, 2, 1).reshape(SSM_W, N_STATE)
    cre = small["ssm_c_re"][0].reshape(SSM_W, N_STATE)
    cim = small["ssm_c_im"][0].reshape(SSM_W, N_STATE)
    disc = (lr, li, ldt, lr16, li16, ldt16, brt, bit)

    pwr, pwi, bbr, bbi, ctr, cti = _ssm_prepare(*disc, cre, cim)
    tab_a, tab_p = _scan_tables(pwr, pwi)

    first_sh = [shards[n] for n in MIXER_W + FFN_W[:1]]
    second_sh = [shards[n] for n in FFN_W[1:]]
    (w_in_st,) = _gather_weights([shards["w_in"]])
    proj, (arrived,) = _in_proj(x, w_in_st, small["b_in"], comms=[_gather_ici(first_sh, [shards["conv_w"]])])
    (xsr, xsi, ya0), (second_part, first_st) = _ssm_forward(
        proj, bbr, bbi, ctr, cti, small["ssm_d"], tab_a, tab_p,
        comms=[_gather_ici(second_sh), _gather_d2d(arrived[:len(first_sh)], first_sh)])
    glu_st, wso_st, wco_st, wo_st, wg_st = first_st
    conv_st = arrived[len(first_sh)]
    conv_w8 = jnp.pad(conv_st[:, :3, :].transpose(1, 0, 2).reshape(3, SSM_W), ((0, SUBLANES - 3), (0, 0)))
    w_o = wo_st.reshape(D_MODEL, D_MODEL)
    glu_w = glu_st.reshape(SSM_W, SSM_W)
    (xhat1, rstd1, ya, yb), (second_st,) = _mixer_forward(
        x, proj, ya0, glu_w, small["glu_b"], wso_st, conv_w8, wco_st, w_o, comms=[_gather_d2d(second_part, second_sh)])
    wu_st, wd_st = second_st
    (loss, dr1, x1b, dr2b, hid, dhg, dhu, g_ln2_g, g_ln2_b, g_ln1_g, g_ln1_b) = _ffn_step(
        xhat1, rstd1, target, small["ln1_g"], small["ln1_b"], small["ln2_g"], small["ln2_b"], wg_st, wu_st, wd_st)

    add_halves = lambda gs, rs: _per_shape(lambda a, b: _add_own_half(a, b, c_arr), list(gs), list(rs))
    sum_chips = lambda owns, slots: _per_shape(lambda a, b: _sum_chips(a, b, me_arr), list(owns), list(slots))
    g_ffn = _ffn_weight_grads(x1b, dr2b, hid, dhg, dhu)
    (dya0, dproj, dbias, g_wo, g_wso, g_wco, g_glu_w, g_glu_b, g_conv8), (got_ffn,) = _mixer_backward(
        dr1, proj, ya0, ya, yb, glu_w, small["glu_b"], wso_st, conv_w8, wco_st, w_o, comms=[_swap_comm(g_ffn)])
    chip_ffn = add_halves(g_ffn, got_ffn)
    g_mix = [g_glu_w.reshape(N_SHARDS, 128, SSM_W), g_wso, g_wco, g_wo.reshape(N_SHARDS, 256, D_MODEL)]
    (dproj, dus, gbbr, gbbi, gctr, gcti, glbr, glbi, g_d), (slots_ffn, got_mix) = _ssm_backward(
        dya0, proj, xsr, xsi, bbr, bbi, ctr, cti, small["ssm_d"], tab_a, tab_p, dproj,
        comms=[_scatter_comm(chip_ffn), _swap_comm(g_mix)])
    halves_ffn = sum_chips(chip_ffn, slots_ffn)
    chip_mix = add_halves(g_mix, got_mix)
    g_lr, g_li, g_ldt, g_brt, g_bit, g_cre, g_cim = _ssm_param_grads(
        *disc, glbr.reshape(SUBLANES, N_GROUPS, N_STATE), glbi.reshape(SUBLANES, N_GROUPS, N_STATE),
        gbbr, gbbi, gctr, gcti)
    g_w_in, (others_ffn, slots_mix) = _in_weight_grad(
        x, dproj, comms=[_send_comm(halves_ffn), _scatter_comm(chip_mix)])
    halves_mix = sum_chips(chip_mix, slots_mix)
    dx, _ = _input_grad(dr1, dproj, w_in_st)

    g_conv = jnp.pad(g_conv8[:3].reshape(3, N_SHARDS, 128).transpose(1, 0, 2), ((0, 0), (0, SUBLANES - 3), (0, 0)))
    pieces = [dus, dbias[:, SSM_W:], g_lr, g_li, g_ldt, g_brt, g_bit, g_cre, g_cim, g_d, g_glu_b,
              g_ln1_g, g_ln1_b, g_ln2_g, g_ln2_b, loss]
    flat = jnp.concatenate([p.reshape(-1) for p in pieces])
    g_packed = jnp.pad(flat, (0, PACKED_ROWS * 128 - flat.shape[0])).reshape(PACKED_ROWS, 128)
    ((got_w, got_conv, got_packed),) = _standalone([_swap_comm([g_w_in], [g_conv, g_packed])], "swap_with_sibling")
    (chip_w,) = add_halves([g_w_in], [got_w])
    chip_conv, chip_packed = _small_pair_sums([g_conv, g_packed], [got_conv, got_packed])
    ((slots_w, slots_conv, slots_packed),) = _standalone(
        [_scatter_comm([chip_w, chip_conv], [chip_packed])], "scatter_to_chips")
    (halves_w,) = sum_chips([chip_w], [slots_w])
    conv_total, packed_mine = _small_totals(me_arr, c_arr, chip_conv, slots_conv, chip_packed, slots_packed)
    (others_rest,) = _standalone([_send_comm([halves_w] + halves_mix + [packed_mine])], "send_to_sibling")
    packed_other = others_rest[-1]
    south = c_arr[0] == 0
    packed_total = jnp.concatenate([jnp.where(south, packed_mine, packed_other),
                                    jnp.where(south, packed_other, packed_mine)])

    pairs = dict(zip(FFN_W, zip(halves_ffn, others_ffn)))
    pairs.update(zip(("w_in",) + MIXER_W, zip([halves_w] + halves_mix, others_rest[:-1])))
    return dx, pairs, conv_total, packed_total


PACKED_ROWS = 1136
PACKED_LAYOUT = (("b_in", IN_COLS), ("ssm_lambda_re", STATE_COLS), ("ssm_lambda_im", STATE_COLS),
                 ("ssm_log_dt", N_GROUPS), ("ssm_b_re", SSM_W * N_STATE), ("ssm_b_im", SSM_W * N_STATE),
                 ("ssm_c_re", SSM_W * N_STATE), ("ssm_c_im", SSM_W * N_STATE), ("ssm_d", SSM_W), ("glu_b", SSM_W),
                 ("ln1_g", D_MODEL), ("ln1_b", D_MODEL), ("ln2_g", D_MODEL), ("ln2_b", D_MODEL), ("loss", 1))


def _unpack_small(packed):
    flat = packed.reshape(-1)
    out, off = {}, 0
    for name, size in PACKED_LAYOUT:
        out[name] = flat[off:off + size]
        off += size
    for name in ("ssm_b_re", "ssm_b_im"):
        out[name] = out[name].reshape(N_GROUPS, GROUP_C, N_STATE).transpose(0, 2, 1)[None]
    for name in ("ssm_c_re", "ssm_c_im"):
        out[name] = out[name].reshape(1, N_GROUPS, GROUP_C, N_STATE)
    for name in ("ssm_lambda_re", "ssm_lambda_im"):
        out[name] = out[name].reshape(1, N_GROUPS, N_STATE)
    for name in ("b_in", "ssm_log_dt", "ssm_d", "glu_b", "ln1_g", "ln1_b", "ln2_g", "ln2_b"):
        out[name] = out[name][None]
    return out


BIG = ("w_in", "glu_w", "w_ssm_out", "w_conv_out", "w_o", "w_gate", "w_up", "w_down")
SMALL = ("b_in", "ssm_lambda_re", "ssm_lambda_im", "ssm_log_dt", "ssm_b_re", "ssm_b_im", "ssm_c_re", "ssm_c_im",
         "ssm_d", "glu_b", "ln1_g", "ln1_b", "ln2_g", "ln2_b")
WEIGHTS = ("w_in", "b_in", "ssm_lambda_re", "ssm_lambda_im", "ssm_log_dt", "ssm_b_re", "ssm_b_im", "ssm_c_re",
           "ssm_c_im", "ssm_d", "glu_w", "glu_b", "w_ssm_out", "conv_w", "w_conv_out", "w_o", "ln1_g", "ln1_b",
           "w_gate", "w_up", "w_down", "ln2_g", "ln2_b")


def _place():
    x, y, c = lax.axis_index("x"), lax.axis_index("y"), lax.axis_index("c")
    chips = [(1 - x, y), (x, 1 - y), (1 - x, 1 - y)]
    return x, y, c, chips


def _shard_of(chip):
    return 2 * chip[0] + chip[1]


def _remote(src, dst, send_sem, recv_sem, to):
    return pltpu.make_async_remote_copy(src_ref=src, dst_ref=dst, send_sem=send_sem, recv_sem=recv_sem,
                                        device_id=to, device_id_type=MESH)


def _half_rows(shard, which):
    r2 = shard.shape[0] // 2
    return pl.ds(pl.multiple_of(which * r2, 16), r2)


def _gather_ici(halved, whole=()):
    shards = list(halved) + list(whole)
    nh = len(halved)

    def copies(src, dst, sems):
        send_sem, recv_sem = sems[:2]
        x, y, c, chips = _place()
        me = _shard_of((x, y))
        out = []
        for a in range(len(shards)):
            for k, chip in enumerate(chips):
                if a < nh:
                    rows = _half_rows(shards[a], c)
                    out.append(_remote(src[a].at[rows], dst[a].at[me, rows], send_sem.at[a, k], recv_sem.at[a, k],
                                       (*chip, c)))
                else:
                    out.append(_remote(src[a], dst[a].at[me], send_sem.at[a, k], recv_sem.at[a, k], (*chip, c)))
        return out

    n = len(shards)

    def bounce(src, dst, sems):
        local_sem, buffers = sems[2], sems[3:]
        me = _shard_of((lax.axis_index("x"), lax.axis_index("y")))
        return [(pltpu.make_async_copy(src[a], buffers[a], local_sem.at[a, 0]),
                 pltpu.make_async_copy(buffers[a], dst[a].at[me], local_sem.at[a, 1])) for a in range(n)]

    scratch = [pltpu.SemaphoreType.DMA((n, 3))] * 2 + [pltpu.SemaphoreType.DMA((n, 2))]
    scratch += [pltpu.VMEM(s.shape, s.dtype) for s in shards]
    return _Comm(shards, [jax.ShapeDtypeStruct((N_SHARDS,) + s.shape, s.dtype) for s in shards], scratch, copies,
                 bounce=bounce)


def _gather_d2d(stacks, shards):
    def copies(src, dst, sems):
        del src
        send_sem, recv_sem = sems
        x, y, c, chips = _place()
        out = []
        for a in range(len(stacks)):
            for k, chip in enumerate(chips):
                rows = dst[a].at[_shard_of(chip), _half_rows(shards[a], c)]
                out.append(_remote(rows, rows, send_sem.at[a, k], recv_sem.at[a, k], (x, y, 1 - c)))
        return out

    n = len(stacks)
    return _Comm(stacks, [jax.ShapeDtypeStruct(s.shape, s.dtype) for s in stacks],
                 [pltpu.SemaphoreType.DMA((n, 3))] * 2, copies, aliased=True)


def _standalone(comms, name):
    return _launch(None, comms, name=name, grid=(), in_specs=[], out_specs=(), out_shape=())()[1]


def _gather_weights(shards):
    n = len(shards)

    def body(*refs):
        src, dst = refs[:n], refs[n:2 * n]
        send_sem, recv_sem, fsend_sem, frecv_sem, local_sem = refs[2 * n:2 * n + 5]
        buffers = refs[2 * n + 5:]
        x, y, c, chips = _place()
        me = _shard_of((x, y))
        sibling = (x, y, 1 - c)
        own = [(pltpu.make_async_copy(src[a], buffers[a], local_sem.at[a, 0]),
                pltpu.make_async_copy(buffers[a], dst[a].at[me], local_sem.at[a, 1])) for a in range(n)]
        for to_vmem, _ in own:
            to_vmem.start()
        sends = []
        for a in range(n):
            mine = _half_rows(shards[a], c)
            for k, chip in enumerate(chips):
                cp = _remote(src[a].at[mine], dst[a].at[me, mine], send_sem.at[a, k], recv_sem.at[a, k], (*chip, c))
                cp.start()
                sends.append(cp)
        for to_vmem, to_hbm in own:
            to_vmem.wait()
            to_hbm.start()
        for a in range(n):
            for k, chip in enumerate(chips):
                rows = dst[a].at[_shard_of(chip), _half_rows(shards[a], c)]
                _remote(rows, rows, send_sem.at[a, k], recv_sem.at[a, k], sibling).wait_recv()
                cp = _remote(rows, rows, fsend_sem.at[a, k], frecv_sem.at[a, k], sibling)
                cp.start()
                sends.append(cp)
        for a in range(n):
            for k, chip in enumerate(chips):
                rows = dst[a].at[_shard_of(chip), _half_rows(shards[a], 1 - c)]
                _remote(rows, rows, fsend_sem.at[a, k], frecv_sem.at[a, k], sibling).wait_recv()
        for cp in sends:
            cp.wait_send()
        for _, to_hbm in own:
            to_hbm.wait()

    return pl.pallas_call(
        body, name="gather_weights",
        out_shape=tuple(jax.ShapeDtypeStruct((N_SHARDS,) + s.shape, s.dtype) for s in shards),
        in_specs=[ANY] * n, out_specs=(ANY,) * n,
        scratch_shapes=[pltpu.SemaphoreType.DMA((n, 3))] * 4 + [pltpu.SemaphoreType.DMA((n, 2))]
        + [pltpu.VMEM(s.shape, s.dtype) for s in shards],
    )(*shards)


def _swap_comm(big, small=()):
    nb, n = len(big), len(big) + len(small)
    arrays = list(big) + list(small)

    def copies(src, dst, sems):
        send_sem, recv_sem = sems
        x, y, c, _ = _place()
        out = []
        for a in range(n):
            if a < nb:
                r2 = arrays[a].shape[1] // 2
                part = src[a].at[:, pl.ds(pl.multiple_of((1 - c) * r2, SUBLANES), r2), :]
            else:
                part = src[a]
            out.append(_remote(part, dst[a], send_sem.at[a], recv_sem.at[a], (x, y, 1 - c)))
        return out

    out_shape = [jax.ShapeDtypeStruct((N_SHARDS, g.shape[1] // 2, g.shape[2]), g.dtype) for g in big]
    out_shape += [jax.ShapeDtypeStruct(g.shape, g.dtype) for g in small]
    return _Comm(arrays, out_shape, [pltpu.SemaphoreType.DMA((n,))] * 2, copies)


def _scatter_comm(slabbed, small=()):
    ns, n = len(slabbed), len(slabbed) + len(small)
    arrays = list(slabbed) + list(small)

    def copies(src, dst, sems):
        send_sem, recv_sem = sems
        _, _, c, chips = _place()
        out = []
        for a in range(n):
            for k, chip in enumerate(chips):
                if a < ns:
                    part = src[a].at[_shard_of(chip)]
                else:
                    h = arrays[a].shape[0] // 2
                    part = src[a].at[pl.ds(pl.multiple_of(c * h, SUBLANES), h), :]
                out.append(_remote(part, dst[a].at[k], send_sem.at[a, k], recv_sem.at[a, k], (*chip, c)))
        return out

    out_shape = [jax.ShapeDtypeStruct((3,) + g.shape[1:], g.dtype) for g in slabbed]
    out_shape += [jax.ShapeDtypeStruct((3, g.shape[0] // 2, g.shape[1]), g.dtype) for g in small]
    return _Comm(arrays, out_shape, [pltpu.SemaphoreType.DMA((n, 3))] * 2, copies)


def _send_comm(arrays):
    n = len(arrays)

    def copies(src, dst, sems):
        send_sem, recv_sem = sems
        x, y, c, _ = _place()
        return [_remote(src[a], dst[a], send_sem.at[a], recv_sem.at[a], (x, y, 1 - c)) for a in range(n)]

    return _Comm(arrays, [jax.ShapeDtypeStruct(h.shape, h.dtype) for h in arrays],
                 [pltpu.SemaphoreType.DMA((n,))] * 2, copies)


def _row_chunk(rows):
    for cand in (256, 176, 128, 64):
        if rows % cand == 0:
            return cand
    return rows


def _per_shape(fn, *lists):
    groups = {}
    for i, items in enumerate(zip(*lists)):
        groups.setdefault(tuple(a.shape for a in items), []).append(i)
    out = [None] * len(lists[0])
    for idx in groups.values():
        for i, r in zip(idx, fn(*([lst[i] for i in idx] for lst in lists))):
            out[i] = r
    return out


def _add_own_half(stacks, receiveds, c):
    n = len(stacks)
    _, r2, cols = receiveds[0].shape

    def body(c_ref, *refs):
        del c_ref
        for a in range(n):
            refs[2 * n + a][...] = (refs[a][...] + refs[n + a][...]).astype(BF16)

    own = pl.BlockSpec((1, r2, cols), lambda s, c_ref: (s, c_ref[0], 0))
    got = pl.BlockSpec((1, r2, cols), lambda s, c_ref: (s, 0, 0))
    return pl.pallas_call(
        body, name="add_own_half",
        grid_spec=pltpu.PrefetchScalarGridSpec(
            num_scalar_prefetch=1, grid=(N_SHARDS,), in_specs=[own] * n + [got] * n, out_specs=(got,) * n),
        out_shape=(jax.ShapeDtypeStruct(receiveds[0].shape, BF16),) * n,
        compiler_params=_params(("parallel",)),
    )(c, *stacks, *receiveds)


def _chip_order_sum(me, own, s0, s1, s2):
    terms = []
    for s in range(N_SHARDS):
        d = jnp.bitwise_xor(me, s)
        terms.append(jnp.where(d == 0, own, jnp.where(d == 2, s0, jnp.where(d == 1, s1, s2))))
    return ((terms[0] + terms[1]) + terms[2]) + terms[3]


def _sum_chips(own_stacks, slots, me):
    n = len(slots)
    _, rows, cols = slots[0].shape
    rc = _row_chunk(rows)

    def body(me_ref, *refs):
        del me_ref
        for a in range(n):
            own_r, s_r = refs[a], refs[n + a]
            refs[2 * n + a][...] = (((own_r[0].astype(F32) + s_r[0].astype(F32)) + s_r[1].astype(F32))
                                    + s_r[2].astype(F32))

    own = pl.BlockSpec((1, rc, cols), lambda i, me_ref: (me_ref[0], i, 0))
    three = pl.BlockSpec((3, rc, cols), lambda i, me_ref: (0, i, 0))
    total = pl.BlockSpec((rc, cols), lambda i, me_ref: (i, 0))
    return pl.pallas_call(
        body, name="sum_chips",
        grid_spec=pltpu.PrefetchScalarGridSpec(
            num_scalar_prefetch=1, grid=(rows // rc,), in_specs=[own] * n + [three] * n, out_specs=(total,) * n),
        out_shape=(jax.ShapeDtypeStruct((rows, cols), F32),) * n,
        compiler_params=_params(("parallel",)),
    )(me, *own_stacks, *slots)


def _small_pair_sums(mine, theirs):
    n = len(mine)

    def body(*refs):
        for a in range(n):
            refs[2 * n + a][...] = refs[a][...] + refs[n + a][...]

    vm = pl.BlockSpec(memory_space=pltpu.VMEM)
    return pl.pallas_call(
        body, name="small_pair_sums", out_shape=tuple(jax.ShapeDtypeStruct(g.shape, g.dtype) for g in mine),
        in_specs=[vm] * (2 * n), out_specs=(vm,) * n,
        compiler_params=pltpu.CompilerParams(vmem_limit_bytes=VMEM_LIMIT),
    )(*mine, *theirs)


def _adam_math(w, g, m, v):
    m = ADAM_B1 * m + (1.0 - ADAM_B1) * g
    v = ADAM_B2 * v + (1.0 - ADAM_B2) * (g * g)
    m_hat = m / (1.0 - ADAM_B1 ** ADAM_STEP)
    v_hat = v / (1.0 - ADAM_B2 ** ADAM_STEP)
    delta = -ADAM_LR * (m_hat / (jnp.sqrt(v_hat) + ADAM_EPS) + ADAM_WD * w)
    return delta, m, v


def _small_totals(me, c, conv_stack, conv_slots, packed, packed_slots):
    half = packed.shape[0] // 2

    def body(me_ref, c_ref, cs_r, cslot_r, p_r, pslot_r, conv_r, tot_r):
        me_ = me_ref[0]
        conv_r[...] = _chip_order_sum(me_, cs_r[me_], cslot_r[0], cslot_r[1], cslot_r[2])
        own = p_r[pl.ds(pl.multiple_of(c_ref[0] * half, SUBLANES), half), :]
        tot_r[...] = _chip_order_sum(me_, own, pslot_r[0], pslot_r[1], pslot_r[2])

    vm = pl.BlockSpec(memory_space=pltpu.VMEM)
    sm = pl.BlockSpec(memory_space=pltpu.SMEM)
    return pl.pallas_call(
        body, name="small_totals",
        out_shape=(jax.ShapeDtypeStruct(conv_stack.shape[1:], F32), jax.ShapeDtypeStruct((half, packed.shape[1]), F32)),
        in_specs=[sm, sm] + [vm] * 4, out_specs=(vm, vm),
    )(me, c, conv_stack, conv_slots, packed, packed_slots)


def _adam_small(gs, ws, ms, vs):
    n = len(gs)

    def body(*refs):
        for a in range(n):
            g_r, w_r, m_r, v_r = (refs[i * n + a] for i in range(4))
            d_r, nm_r, nv_r = (refs[(4 + i) * n + a] for i in range(3))
            d_r[...], nm_r[...], nv_r[...] = _adam_math(w_r[...], g_r[...], m_r[...], v_r[...])

    vm = pl.BlockSpec(memory_space=pltpu.VMEM)
    shapes = tuple(jax.ShapeDtypeStruct(w.shape, F32) for w in ws)
    out = pl.pallas_call(
        body, name="adam_small", out_shape=shapes * 3, in_specs=[vm] * (4 * n), out_specs=(vm,) * (3 * n),
        compiler_params=pltpu.CompilerParams(vmem_limit_bytes=VMEM_LIMIT),
    )(*gs, *ws, *ms, *vs)
    return out[:n], out[n:2 * n], out[2 * n:]


def _adam_big(ws, mines, others, ms, vs, c):
    n = len(ws)
    r2, cols = mines[0].shape
    rc = _row_chunk(r2)
    nch = r2 // rc

    def body(c_ref, *refs):
        mine_is_here = pl.program_id(0) == c_ref[0]
        for a in range(n):
            w_r, mine_r, other_r, m_r, v_r = (refs[i * n + a] for i in range(5))
            g_r, d_r, nm_r, nv_r = (refs[(5 + i) * n + a] for i in range(4))
            g = jnp.where(mine_is_here, mine_r[...], other_r[...])
            g_r[...] = g
            d_r[...], nm_r[...], nv_r[...] = _adam_math(w_r[...], g, m_r[...], v_r[...])

    full = pl.BlockSpec((rc, cols), lambda h, i, c_ref: (h * nch + i, 0))
    half = pl.BlockSpec((rc, cols), lambda h, i, c_ref: (i, 0))
    shape = jax.ShapeDtypeStruct((2 * r2, cols), F32)
    out = pl.pallas_call(
        body, name="adam_big",
        grid_spec=pltpu.PrefetchScalarGridSpec(
            num_scalar_prefetch=1, grid=(2, nch),
            in_specs=[full] * n + [half] * (2 * n) + [full] * (2 * n), out_specs=(full,) * (4 * n)),
        out_shape=(shape,) * (4 * n), compiler_params=_params(("parallel", "parallel")),
    )(c, *ws, *mines, *others, *ms, *vs)
    return [tuple(out[i * n + a] for i in range(4)) for a in range(n)]


def kernel(x, w_in, b_in, ssm_lambda_re, ssm_lambda_im, ssm_log_dt, ssm_b_re, ssm_b_im, ssm_c_re, ssm_c_im, ssm_d, glu_w, glu_b, w_ssm_out, conv_w, w_conv_out, w_o, ln1_g, ln1_b, w_gate, w_up, w_down, ln2_g, ln2_b, loss_target, m_w_in, m_b_in, m_ssm_lambda_re, m_ssm_lambda_im, m_ssm_log_dt, m_ssm_b_re, m_ssm_b_im, m_ssm_c_re, m_ssm_c_im, m_ssm_d, m_glu_w, m_glu_b, m_w_ssm_out, m_conv_w, m_w_conv_out, m_w_o, m_ln1_g, m_ln1_b, m_w_gate, m_w_up, m_w_down, m_ln2_g, m_ln2_b, v_w_in, v_b_in, v_ssm_lambda_re, v_ssm_lambda_im, v_ssm_log_dt, v_ssm_b_re, v_ssm_b_im, v_ssm_c_re, v_ssm_c_im, v_ssm_d, v_glu_w, v_glu_b, v_w_ssm_out, v_conv_w, v_w_conv_out, v_w_o, v_ln1_g, v_ln1_b, v_w_gate, v_w_up, v_w_down, v_ln2_g, v_ln2_b):
    given = dict(locals())
    w = {n: given[n] for n in WEIGHTS}
    m = {n: given["m_" + n] for n in WEIGHTS}
    v = {n: given["v_" + n] for n in WEIGHTS}

    flip = lambda n, a: a.T if n in ("w_gate", "w_up") else a
    shards = {n: flip(n, w[n][0]).astype(BF16) for n in BIG}
    shards["conv_w"] = jnp.pad(conv_w[0], ((0, SUBLANES - 3), (0, 0)))
    c_arr = jnp.reshape(lax.axis_index("c"), (1,)).astype(jnp.int32)
    me = _shard_of((lax.axis_index("x"), lax.axis_index("y")))
    me_arr = jnp.reshape(me, (1,)).astype(jnp.int32)
    dx, pairs, conv_total, packed_total = _device_step(
        x[0], loss_target[0], {n: w[n] for n in SMALL}, shards, c_arr, me_arr)

    grad = _unpack_small(packed_total)
    loss_total = grad.pop("loss")[0]
    grad["conv_w"] = conv_total[:3][None]
    small_names = ("conv_w",) + SMALL
    swap = lambda n, a: a.transpose(0, 1, 3, 2) if n in ("ssm_b_re", "ssm_b_im") else a
    ds, nms, nvs = _adam_small(*([swap(n, d[n]) for n in small_names] for d in (grad, w, m, v)))
    delta, new_m, new_v = {}, {}, {}
    for i, n in enumerate(small_names):
        delta[n], new_m[n], new_v[n] = swap(n, ds[i]), swap(n, nms[i]), swap(n, nvs[i])
    updated = _per_shape(
        lambda *a: _adam_big(*a, c_arr),
        [flip(n, w[n][0]) for n in BIG], [pairs[n][0] for n in BIG], [pairs[n][1] for n in BIG],
        [flip(n, m[n][0]) for n in BIG], [flip(n, v[n][0]) for n in BIG])
    for n, results in zip(BIG, updated):
        grad[n], delta[n], new_m[n], new_v[n] = (flip(n, r)[None] for r in results)

    return (loss_total, dx[None], *[grad[n] for n in WEIGHTS], *[delta[n] for n in WEIGHTS],
            *[new_m[n] for n in WEIGHTS], *[new_v[n] for n in WEIGHTS])
```

```python
import functools
import math

import jax
import jax.numpy as jnp
from jax import lax
from jax.experimental import pallas as pl
from jax.experimental.pallas import tpu as pltpu

F32 = jnp.float32
BF16 = jnp.bfloat16

D_MODEL = 1024
IN_COLS = 4096
SSM_W = 512
N_GROUPS = 32
N_STATE = 64
GROUP_C = 16
STATE_COLS = N_GROUPS * N_STATE
STRIP = 512
N_STRIPS = STATE_COLS // STRIP
FFN_SHARD = 704
N_SHARDS = 4
ALPHA = 2.0 ** 0.25
LN_EPS = 1e-5
GELU_K = math.sqrt(2.0 / math.pi)
GELU_C = 0.044715

ADAM_LR = 0.001
ADAM_B1 = 0.9
ADAM_B2 = 0.999
ADAM_EPS = 1e-08
ADAM_WD = 0.01
ADAM_STEP = 10

V7X_VMEM_BYTES = 64 * 1024 * 1024
VMEM_LIMIT = V7X_VMEM_BYTES - 8 * 1024 * 1024
SUBLANES = 8
N_POWERS = 128

MESH = pl.DeviceIdType.MESH
ANY = pl.BlockSpec(memory_space=pl.ANY)


def _dot(a, b):
    return jnp.dot(a.astype(BF16), b.astype(BF16), preferred_element_type=F32)


def _dot_t(a, b):
    return lax.dot_general(a.astype(BF16), b.astype(BF16), (((1,), (1,)), ((), ())),
                           preferred_element_type=F32)


def _tdot(a, b):
    return lax.dot_general(a.astype(BF16), b.astype(BF16), (((0,), (0,)), ((), ())),
                           preferred_element_type=F32)


def _sigmoid(v):
    return 1.0 / (1.0 + jnp.exp(-v))


def _split3(v):
    hi = v.astype(BF16)
    r1 = v - hi.astype(F32)
    mid = r1.astype(BF16)
    lo = (r1 - mid.astype(F32)).astype(BF16)
    return hi, mid, lo


def _const(shape):
    nd = len(shape)
    return pl.BlockSpec(shape, lambda *_: (0,) * nd)


def _params(sem, vmem=VMEM_LIMIT):
    return pltpu.CompilerParams(dimension_semantics=sem, vmem_limit_bytes=vmem)


def _gelu_parts(v):
    inner = GELU_K * (v + GELU_C * v * v * v)
    t = jnp.tanh(inner)
    g = 0.5 * v * (1.0 + t)
    dg = 0.5 * (1.0 + t) + 0.5 * v * (1.0 - t * t) * GELU_K * (1.0 + 3.0 * GELU_C * v * v)
    return g, dg


class _Comm:
    def __init__(self, inputs, out_shape, sems, copies, aliased=False, bounce=None):
        self.inputs, self.out_shape, self.sems = list(inputs), tuple(out_shape), list(sems)
        self.copies, self.aliased, self.bounce = copies, aliased, bounce


def _launch(body, comms, *, name, grid, in_specs, out_specs, out_shape, scratch_shapes=(), aliases=None, sem=None):
    comms = list(comms)
    n_in, n_out, n_scr = len(in_specs), len(out_specs), len(scratch_shapes)
    aliases = dict(aliases or {})
    layout = []
    p_in, p_out, p_sem = n_in, n_out, 0
    for cm in comms:
        layout.append((p_in, p_out, p_sem))
        if cm.aliased:
            for i in range(len(cm.inputs)):
                aliases[p_in + i] = p_out + i
        p_in, p_out, p_sem = p_in + len(cm.inputs), p_out + len(cm.out_shape), p_sem + len(cm.sems)
    tot_in, tot_out = p_in, p_out

    def fused(*refs):
        ins, outs = refs[:tot_in], refs[tot_in:tot_in + tot_out]
        scr = refs[tot_in + tot_out:tot_in + tot_out + n_scr]
        sems = refs[tot_in + tot_out + n_scr:]

        def descriptors(kind):
            out = []
            for cm, (a, b, s) in zip(comms, layout):
                make = cm.copies if kind == "remote" else cm.bounce
                if make is not None:
                    out += make(ins[a:a + len(cm.inputs)], outs[b:b + len(cm.out_shape)], sems[s:s + len(cm.sems)])
            return out

        steps = [pl.program_id(d) for d in range(len(grid))]
        first = functools.reduce(jnp.logical_and, [s == 0 for s in steps]) if grid else None
        last = functools.reduce(jnp.logical_and, [s == g - 1 for s, g in zip(steps, grid)]) if grid else None

        def start():
            for cp in descriptors("remote"):
                cp.start()
            for to_vmem, _ in descriptors("local"):
                to_vmem.start()

        def finish():
            for to_vmem, to_hbm in descriptors("local"):
                to_vmem.wait()
                to_hbm.start()
            for cp in descriptors("remote"):
                cp.wait()
            for _, to_hbm in descriptors("local"):
                to_hbm.wait()

        if comms:
            pl.when(first)(start) if grid else start()
        if body is not None:
            body(*ins[:n_in], *outs[:n_out], *scr)
        if comms:
            pl.when(last)(finish) if grid else finish()

    specs_in = list(in_specs) + [ANY] * (tot_in - n_in)
    specs_out = tuple(out_specs) + (ANY,) * (tot_out - n_out)
    shapes = tuple(out_shape) + tuple(s for cm in comms for s in cm.out_shape)
    scratch = list(scratch_shapes) + [s for cm in comms for s in cm.sems]
    if comms or sem is None:
        sem = ("arbitrary",) * len(grid)
    kwargs = dict(grid=grid) if grid else {}
    call = pl.pallas_call(fused, name=name, out_shape=shapes, in_specs=specs_in, out_specs=specs_out,
                          scratch_shapes=scratch, input_output_aliases=aliases,
                          compiler_params=_params(sem) if grid else None, **kwargs)

    def run(*args):
        out = call(*args, *(a for cm in comms for a in cm.inputs))
        results, rest = out[:n_out], out[n_out:]
        per_comm = []
        for cm in comms:
            per_comm.append(rest[:len(cm.out_shape)])
            rest = rest[len(cm.out_shape):]
        return results, per_comm

    return run


def _ssm_discretise(lr, li, ldt, lr16, li16, ldt16, brt, bit):
    def lam_bar(lr_, li_, ldt_):
        dt = jnp.exp(ldt_)
        mag = jnp.exp(lr_ * dt)
        return mag * jnp.cos(li_ * dt), mag * jnp.sin(li_ * dt)

    lb_re, lb_im = lam_bar(lr, li, ldt)
    l16_re, l16_im = lam_bar(lr16, li16, ldt16)
    den = lr16 * lr16 + li16 * li16
    num_re = l16_re - 1.0
    fr = (num_re * lr16 + l16_im * li16) / den
    fi = (l16_im * lr16 - num_re * li16) / den
    bb_re = fr * brt - fi * bit
    bb_im = fr * bit + fi * brt
    return lb_re, lb_im, bb_re, bb_im


def _strip_selectors():
    p = lax.broadcasted_iota(jnp.int32, (N_STATE, STRIP), 0)
    col = lax.broadcasted_iota(jnp.int32, (N_STATE, STRIP), 1)
    rep = ((col & (N_STATE - 1)) == p).astype(BF16)
    row = lax.broadcasted_iota(jnp.int32, (SSM_W, STRIP), 0)
    col2 = lax.broadcasted_iota(jnp.int32, (SSM_W, STRIP), 1)
    mask = (((row >> 4) & 7) == (col2 >> 6))
    return rep, mask


def _ssm_prepare(lr, li, ldt, lr16, li16, ldt16, brt, bit, cre, cim):
    def body(lr_r, li_r, ldt_r, lr16_r, li16_r, ldt16_r, brt_r, bit_r, cre_r, cim_r,
             pwr_r, pwi_r, bbr_r, bbi_r, ctr_r, cti_r):
        lb_re, lb_im, bb_re, bb_im = _ssm_discretise(
            lr_r[...], li_r[...], ldt_r[...], lr16_r[...], li16_r[...], ldt16_r[...], brt_r[...], bit_r[...])
        pr, pi_ = lb_re, lb_im
        pwr_r[0] = pr
        pwi_r[0] = pi_
        for k in range(1, N_POWERS):
            pr, pi_ = pr * lb_re - pi_ * lb_im, pr * lb_im + pi_ * lb_re
            pwr_r[k] = pr
            pwi_r[k] = pi_
        rep, mask = _strip_selectors()
        for src, dst in ((bb_re, bbr_r), (bb_im, bbi_r), (cre_r[...], ctr_r), (cim_r[...], cti_r)):
            wide = jnp.dot(src.astype(BF16), rep, preferred_element_type=F32)
            dst[...] = jnp.where(mask, wide, 0.0).astype(BF16)

    vm = pl.BlockSpec(memory_space=pltpu.VMEM)
    return pl.pallas_call(
        body, name="ssm_prepare",
        out_shape=(jax.ShapeDtypeStruct((N_POWERS, N_GROUPS, N_STATE), F32),) * 2
        + (jax.ShapeDtypeStruct((SSM_W, STRIP), BF16),) * 4,
        in_specs=[vm] * 10, out_specs=(vm,) * 6,
    )(lr, li, ldt, lr16, li16, ldt16, brt, bit, cre, cim)


def _scan_tables(pwr, pwi):
    pr = pwr.reshape(N_POWERS, STATE_COLS)
    pi_ = pwi.reshape(N_POWERS, STATE_COLS)
    rows8 = lambda v: jnp.broadcast_to(v[None], (SUBLANES, STATE_COLS))
    tab_a = jnp.stack([rows8(pr[0]), rows8(pi_[0]), rows8(pr[-1]), rows8(pi_[-1])])
    return tab_a, jnp.stack([pr, pi_])


def _ssm_param_grads(lr, li, ldt, lr16, li16, ldt16, brt, bit, dlbr, dlbi, dbbr, dbbi, dctr, dcti):
    def body(lr_r, li_r, ldt_r, lr16_r, li16_r, ldt16_r, brt_r, bit_r,
             dlbr_r, dlbi_r, dbbr_r, dbbi_r, dctr_r, dcti_r,
             glr_r, gli_r, gldt_r, gbrt_r, gbit_r, gcre_r, gcim_r):
        rep, mask = _strip_selectors()

        def fold(acc):
            return sum(lax.dot_general(t, rep, (((1,), (1,)), ((), ())), preferred_element_type=F32)
                       for t in _split3(jnp.where(mask, acc, 0.0)))

        g_lb_re = jnp.sum(dlbr_r[...], axis=0)
        g_lb_im = jnp.sum(dlbi_r[...], axis=0)
        g_bb_re = fold(dbbr_r[...])
        g_bb_im = fold(dbbi_r[...])
        gcre_r[...] = fold(dctr_r[...])
        gcim_r[...] = fold(dcti_r[...])
        prim = (lr_r[...], li_r[...], ldt_r[...], lr16_r[...], li16_r[...], ldt16_r[...], brt_r[...], bit_r[...])
        _, vjp = jax.vjp(_ssm_discretise, *prim)
        g_lr, g_li, g_ldt, g_lr16, g_li16, g_ldt16, g_brt, g_bit = vjp((g_lb_re, g_lb_im, g_bb_re, g_bb_im))
        grp = lax.broadcasted_iota(jnp.int32, (N_GROUPS, SSM_W), 0)
        rw = lax.broadcasted_iota(jnp.int32, (N_GROUPS, SSM_W), 1)
        gsum = ((rw >> 4) == grp).astype(BF16)

        def group_sum(v):
            return sum(jnp.dot(gsum, t, preferred_element_type=F32) for t in _split3(v))

        glr_r[...] = g_lr + group_sum(g_lr16)
        gli_r[...] = g_li + group_sum(g_li16)
        gldt_r[...] = g_ldt + jnp.sum(group_sum(g_ldt16), axis=1, keepdims=True)
        gbrt_r[...] = g_brt
        gbit_r[...] = g_bit

    vm = pl.BlockSpec(memory_space=pltpu.VMEM)
    gp = jax.ShapeDtypeStruct((N_GROUPS, N_STATE), F32)
    gb = jax.ShapeDtypeStruct((SSM_W, N_STATE), F32)
    return pl.pallas_call(
        body, name="ssm_param_grads",
        out_shape=(gp, gp, jax.ShapeDtypeStruct((N_GROUPS, 1), F32), gb, gb, gb, gb),
        in_specs=[vm] * 14, out_specs=(vm,) * 7,
    )(lr, li, ldt, lr16, li16, ldt16, brt, bit, dlbr, dlbi, dbbr, dbbi, dctr, dcti)


def _in_proj(x, w_in_st, b_in, comms=()):
    t = x.shape[0]
    tm = 512

    def body(x_r, w_r, b_r, o_r):
        xb = x_r[...].astype(BF16)
        for j in range(N_SHARDS):
            cols = slice(D_MODEL * j, D_MODEL * (j + 1))
            o_r[:, cols] = jnp.dot(xb, w_r[j], preferred_element_type=F32) + b_r[:, cols]

    (proj,), sent = _launch(
        body, comms, name="in_proj", grid=(t // tm,),
        out_shape=(jax.ShapeDtypeStruct((t, IN_COLS), F32),),
        in_specs=[pl.BlockSpec((tm, D_MODEL), lambda i: (i, 0)), _const((N_SHARDS, D_MODEL, D_MODEL)),
                  _const((1, IN_COLS))],
        out_specs=(pl.BlockSpec((tm, IN_COLS), lambda i: (i, 0)),),
        sem=("parallel",),
    )(x, w_in_st, b_in)
    return proj, sent


def _cmul_add(xr, xi, mr, mi, sr, si):
    return xr + (mr * sr - mi * si), xi + (mr * si + mi * sr)


SCAN_STEPS = N_POWERS
SCAN_CHUNK = SUBLANES * SCAN_STEPS


def _interleave(src_r, dst_r):
    for step in range(SCAN_STEPS):
        dst_r[SUBLANES * step:SUBLANES * (step + 1), :] = src_r[pl.ds(step, SUBLANES, stride=SCAN_STEPS), :]


def _deinterleave(src_r, dst_r):
    for step in range(SCAN_STEPS):
        dst_r[pl.ds(step, SUBLANES, stride=SCAN_STEPS), :] = src_r[SUBLANES * step:SUBLANES * (step + 1), :]


def _step_rows(step):
    return pl.ds(pl.multiple_of(step * SUBLANES, SUBLANES), SUBLANES)


def _scan_steps(body, init, by=4, powers=None):
    if powers is not None:
        by = SUBLANES

    def trip(t, carry):
        if powers is not None:
            rows_re, rows_im = powers[0, _step_rows(t), :], powers[1, _step_rows(t), :]
        for u in range(by):
            if powers is None:
                carry = body(t * by + u, carry)
            else:
                carry = body(t * by + u, carry, jnp.broadcast_to(rows_re[u:u + 1, :], rows_re.shape),
                             jnp.broadcast_to(rows_im[u:u + 1, :], rows_im.shape))
        return carry

    return lax.fori_loop(0, SCAN_STEPS // by, trip, init)


def _segment_states(first_r, first_i, ends_r, ends_i, a64_r, a64_i, order):
    row = lax.broadcasted_iota(jnp.int32, ends_r.shape, 0)
    cur_r, cur_i = first_r, first_i
    ent_r = jnp.zeros_like(ends_r)
    ent_i = jnp.zeros_like(ends_i)
    for s in order:
        ent_r = jnp.where(row == s, jnp.broadcast_to(cur_r, ends_r.shape), ent_r)
        ent_i = jnp.where(row == s, jnp.broadcast_to(cur_i, ends_i.shape), ent_i)
        cur_r, cur_i = _cmul_add(ends_r[s:s + 1, :], ends_i[s:s + 1, :], a64_r, a64_i, cur_r, cur_i)
    return ent_r, ent_i, cur_r, cur_i


def _ssm_forward(proj, bbr, bbi, ctr, cti, d_skip, tab_a, tab_p, comms=(), tc=SCAN_CHUNK):
    t = proj.shape[0]

    def body(u_r, bbr_r, bbi_r, ctr_r, cti_r, d_r, ta_r, tp_r, xsr_r, xsi_r, y_r, ui_s, yi_s, car_r, car_i):
        @pl.when(pl.program_id(1) == 0)
        def _():
            car_r[...] = jnp.zeros_like(car_r)
            car_i[...] = jnp.zeros_like(car_i)

        _interleave(u_r, ui_s)
        u = ui_s[...]
        xsr_r[...] = _dot(u, bbr_r[...])
        xsi_r[...] = _dot(u, bbi_r[...])
        a_r, a_i = ta_r[0], ta_r[1]

        def local(step, carry):
            rows = _step_rows(step)
            xr, xi = _cmul_add(xsr_r[rows, :], xsi_r[rows, :], a_r, a_i, *carry)
            xsr_r[rows, :] = xr
            xsi_r[rows, :] = xi
            return xr, xi

        zero = jnp.zeros((SUBLANES, STRIP), F32)
        ends_r, ends_i = _scan_steps(local, (zero, zero))
        ent_r, ent_i, out_r, out_i = _segment_states(
            car_r[...], car_i[...], ends_r, ends_i, ta_r[2, 0:1, :], ta_r[3, 0:1, :], range(SUBLANES))
        car_r[...] = out_r
        car_i[...] = out_i

        def entering(step, _, power_r, power_i):
            rows = _step_rows(step)
            xr, xi = _cmul_add(xsr_r[rows, :], xsi_r[rows, :], power_r, power_i, ent_r, ent_i)
            xsr_r[rows, :] = xr
            xsi_r[rows, :] = xi
            return 0

        _scan_steps(entering, 0, powers=tp_r)
        yi_s[...] = _dot_t(xsr_r[...], ctr_r[...]) - _dot_t(xsi_r[...], cti_r[...]) + d_r[...] * u
        _deinterleave(yi_s, y_r)

    strip_mat = pl.BlockSpec((128, STRIP), lambda j, k: (j, 0))
    states = pl.BlockSpec((tc, STRIP), lambda j, k: (k, j))
    return _launch(
        body, comms, name="ssm_forward", grid=(N_STRIPS, t // tc),
        out_shape=(jax.ShapeDtypeStruct((t, STATE_COLS), F32), jax.ShapeDtypeStruct((t, STATE_COLS), F32),
                   jax.ShapeDtypeStruct((t, SSM_W), F32)),
        in_specs=[pl.BlockSpec((tc, 128), lambda j, k: (k, j)),
                  strip_mat, strip_mat, strip_mat, strip_mat,
                  pl.BlockSpec((1, 128), lambda j, k: (0, j)),
                  pl.BlockSpec((4, SUBLANES, STRIP), lambda j, k: (0, 0, j)),
                  pl.BlockSpec((2, SCAN_STEPS, STRIP), lambda j, k: (0, 0, j))],
        out_specs=(states, states, pl.BlockSpec((tc, 128), lambda j, k: (k, j))),
        scratch_shapes=[pltpu.VMEM((tc, 128), F32), pltpu.VMEM((tc, 128), F32),
                        pltpu.VMEM((1, STRIP), F32), pltpu.VMEM((1, STRIP), F32)],
        sem=("parallel", "arbitrary"),
    )(proj, bbr, bbi, ctr, cti, d_skip, tab_a, tab_p)


def _shift_down(v, prev, n):
    row = lax.broadcasted_iota(jnp.int32, v.shape, 0)
    out = pltpu.roll(v, n, 0)
    for r in range(n):
        src = prev[SUBLANES - n + r:SUBLANES - n + r + 1, :]
        out = jnp.where(row == r, jnp.broadcast_to(src, v.shape), out)
    return out


def _shift_up(v, nxt, n):
    rows = v.shape[0]
    row = lax.broadcasted_iota(jnp.int32, v.shape, 0)
    out = pltpu.roll(v, rows - n, 0)
    for r in range(n):
        src = nxt[r:r + 1, :]
        out = jnp.where(row == rows - n + r, jnp.broadcast_to(src, v.shape), out)
    return out


def _conv3(q, q_prev, w):
    return w[2:3, :] * q + w[1:2, :] * _shift_down(q, q_prev, 1) + w[0:1, :] * _shift_down(q, q_prev, 2)


def _mixer_forward(x, proj, ya0, glu_w, glu_b, wso_st, conv_w8, wco_st, w_o, comms=(), tm=256):
    t = x.shape[0]
    hb = tm // SUBLANES

    def body(x_r, ya0_r, h_r, cg_r, bg_r, ga_r, gb_r, hp_r, cgp_r,
             glu_w_r, glu_b_r, wso_r, cw_r, wco_r, wo_r, xh_r, rstd_r, ya_r, yb_r):
        i = pl.program_id(0)
        g, _ = _gelu_parts(ya0_r[...])
        ya1 = g * _sigmoid(_dot(g, glu_w_r[...]) + glu_b_r[...])
        q = cg_r[...] * h_r[...]
        q_prev = jnp.where(i > 0, cgp_r[...] * hp_r[...], 0.0)
        yb0 = bg_r[...] * _conv3(q, q_prev, cw_r[...])
        for j in range(N_SHARDS):
            ya_r[:, 256 * j:256 * (j + 1)] = _dot(ya1, wso_r[j])
            yb_r[:, 256 * j:256 * (j + 1)] = _dot(yb0, wco_r[j])
        merged = _sigmoid(ga_r[...]) * ya_r[...] + _sigmoid(gb_r[...]) * yb_r[...]
        r1 = ALPHA * x_r[...] + _dot(merged, wo_r[...])
        mu = jnp.mean(r1, axis=-1, keepdims=True)
        cen = r1 - mu
        rstd = lax.rsqrt(jnp.mean(cen * cen, axis=-1, keepdims=True) + LN_EPS)
        xh_r[...] = cen * rstd
        rstd_r[...] = rstd

    def col(w, c):
        return pl.BlockSpec((tm, w), lambda i: (i, c))

    def prev(c):
        return pl.BlockSpec((SUBLANES, SSM_W), lambda i: (jnp.maximum(i * hb - 1, 0), c))

    return _launch(
        body, comms, name="mixer_forward", grid=(t // tm,),
        out_shape=(jax.ShapeDtypeStruct((t, D_MODEL), F32), jax.ShapeDtypeStruct((t, 1), F32),
                   jax.ShapeDtypeStruct((t, D_MODEL), F32), jax.ShapeDtypeStruct((t, D_MODEL), F32)),
        in_specs=[col(D_MODEL, 0), col(SSM_W, 0), col(SSM_W, 1), col(SSM_W, 2), col(SSM_W, 3),
                  col(D_MODEL, 2), col(D_MODEL, 3), prev(1), prev(2),
                  _const((SSM_W, SSM_W)), _const((1, SSM_W)), _const((N_SHARDS, SSM_W, 256)),
                  _const((SUBLANES, SSM_W)), _const((N_SHARDS, SSM_W, 256)), _const((D_MODEL, D_MODEL))],
        out_specs=(col(D_MODEL, 0), pl.BlockSpec((tm, 1), lambda i: (i, 0)), col(D_MODEL, 0), col(D_MODEL, 0)),
        sem=("parallel",),
    )(x, ya0, proj, proj, proj, proj, proj, proj, proj, glu_w, glu_b, wso_st, conv_w8, wco_st, w_o)


def _layer_norm_bwd(dxhat, xhat, rstd):
    m1 = jnp.mean(dxhat, axis=-1, keepdims=True)
    m2 = jnp.mean(dxhat * xhat, axis=-1, keepdims=True)
    return rstd * (dxhat - m1 - xhat * m2)


def _ffn_step(xhat1, rstd1, target, ln1_g, ln1_b, ln2_g, ln2_b, wg_st, wu_st, wd_st, tm=256):
    t = xhat1.shape[0]

    def body(xh_r, rstd_r, tgt_r, g1_r, b1_r, g2_r, b2_r, wg_r, wu_r, wd_r,
             loss_r, dr1_r, x1b_r, dr2b_r, hid_r, dhg_r, dhu_r, dg2_r, db2_r, dg1_r, db1_r,
             hg_s, hu_s):
        @pl.when(pl.program_id(0) == 0)
        def _():
            for r in (loss_r, dg2_r, db2_r, dg1_r, db1_r):
                r[...] = jnp.zeros_like(r)

        xhat1_v = xh_r[...]
        x1 = xhat1_v * g1_r[...] + b1_r[...]
        x1b = x1.astype(BF16)
        x1b_r[...] = x1b
        ffn = jnp.zeros((tm, D_MODEL), F32)
        for j in range(N_SHARDS):
            hg = lax.dot_general(x1b, wg_r[j], (((1,), (1,)), ((), ())), preferred_element_type=F32)
            hu = lax.dot_general(x1b, wu_r[j], (((1,), (1,)), ((), ())), preferred_element_type=F32)
            hg_s[j] = hg
            hu_s[j] = hu
            hid = (hg * _sigmoid(hg) * hu).astype(BF16)
            hid_r[j] = hid
            ffn = ffn + jnp.dot(hid, wd_r[j], preferred_element_type=F32)
        r2 = ALPHA * x1 + ffn
        mu = jnp.mean(r2, axis=-1, keepdims=True)
        cen = r2 - mu
        rstd2 = lax.rsqrt(jnp.mean(cen * cen, axis=-1, keepdims=True) + LN_EPS)
        xhat2 = cen * rstd2
        diff = (xhat2 * g2_r[...] + b2_r[...]) - tgt_r[...]
        loss_r[...] += 0.5 * jnp.sum(jnp.mean(diff * diff, axis=-1, keepdims=True), axis=0, keepdims=True)
        dy = diff * (1.0 / D_MODEL)
        dg2_r[...] += jnp.sum(dy * xhat2, axis=0, keepdims=True)
        db2_r[...] += jnp.sum(dy, axis=0, keepdims=True)
        dr2 = _layer_norm_bwd(dy * g2_r[...], xhat2, rstd2)
        dr2b = dr2.astype(BF16)
        dr2b_r[...] = dr2b
        dx1 = ALPHA * dr2
        for j in range(N_SHARDS):
            dhid = lax.dot_general(dr2b, wd_r[j], (((1,), (1,)), ((), ())), preferred_element_type=F32)
            hg = hg_s[j]
            hu = hu_s[j]
            sg = _sigmoid(hg)
            dhu = (dhid * (hg * sg)).astype(BF16)
            dhg = (dhid * hu * (sg * (1.0 + hg * (1.0 - sg)))).astype(BF16)
            dhg_r[j] = dhg
            dhu_r[j] = dhu
            dx1 = dx1 + jnp.dot(dhg, wg_r[j], preferred_element_type=F32)
            dx1 = dx1 + jnp.dot(dhu, wu_r[j], preferred_element_type=F32)
        dg1_r[...] += jnp.sum(dx1 * xhat1_v, axis=0, keepdims=True)
        db1_r[...] += jnp.sum(dx1, axis=0, keepdims=True)
        dr1_r[...] = _layer_norm_bwd(dx1 * g1_r[...], xhat1_v, rstd_r[...])

    tile = pl.BlockSpec((tm, D_MODEL), lambda i: (i, 0))
    hidden = pl.BlockSpec((N_SHARDS, tm, FFN_SHARD), lambda i: (0, i, 0))
    vec = _const((1, D_MODEL))
    hid_shape = jax.ShapeDtypeStruct((N_SHARDS, t, FFN_SHARD), BF16)
    vec_shape = jax.ShapeDtypeStruct((1, D_MODEL), F32)
    return pl.pallas_call(
        body, name="ffn_step", grid=(t // tm,),
        out_shape=(jax.ShapeDtypeStruct((1, 1), F32), jax.ShapeDtypeStruct((t, D_MODEL), F32),
                   jax.ShapeDtypeStruct((t, D_MODEL), BF16), jax.ShapeDtypeStruct((t, D_MODEL), BF16),
                   hid_shape, hid_shape, hid_shape, vec_shape, vec_shape, vec_shape, vec_shape),
        in_specs=[tile, pl.BlockSpec((tm, 1), lambda i: (i, 0)), tile, vec, vec, vec, vec,
                  _const((N_SHARDS, FFN_SHARD, D_MODEL)), _const((N_SHARDS, FFN_SHARD, D_MODEL)),
                  _const((N_SHARDS, FFN_SHARD, D_MODEL))],
        out_specs=(_const((1, 1)), tile, tile, tile, hidden, hidden, hidden, vec, vec, vec, vec),
        scratch_shapes=[pltpu.VMEM((N_SHARDS, tm, FFN_SHARD), F32), pltpu.VMEM((N_SHARDS, tm, FFN_SHARD), F32)],
        compiler_params=_params(("arbitrary",)),
    )(xhat1, rstd1, target, ln1_g, ln1_b, ln2_g, ln2_b, wg_st, wu_st, wd_st)


def _ffn_weight_grads(x1b, dr2b, hid, dhg, dhu, tk=2048):
    t = x1b.shape[0]

    def body(x_r, dr_r, hid_r, dhg_r, dhu_r, gwg_r, gwu_r, gwd_r):
        @pl.when(pl.program_id(1) == 0)
        def _():
            for r in (gwg_r, gwu_r, gwd_r):
                r[...] = jnp.zeros_like(r)

        gwg_r[0] += _tdot(dhg_r[0], x_r[...])
        gwu_r[0] += _tdot(dhu_r[0], x_r[...])
        gwd_r[0] += _tdot(hid_r[0], dr_r[...])

    tile = pl.BlockSpec((tk, D_MODEL), lambda j, k: (k, 0))
    hidden = pl.BlockSpec((1, tk, FFN_SHARD), lambda j, k: (j, k, 0))
    row = pl.BlockSpec((1, FFN_SHARD, D_MODEL), lambda j, k: (j, 0, 0))
    return pl.pallas_call(
        body, name="ffn_weight_grads", grid=(N_SHARDS, t // tk),
        out_shape=(jax.ShapeDtypeStruct((N_SHARDS, FFN_SHARD, D_MODEL), F32),) * 3,
        in_specs=[tile, tile, hidden, hidden, hidden],
        out_specs=(row, row, row),
        compiler_params=_params(("parallel", "arbitrary")),
    )(x1b, dr2b, hid, dhg, dhu)


def _mixer_backward(dr1, proj, ya0, ya, yb, glu_w, glu_b, wso_st, conv_w8, wco_st, w_o, comms=(), tm=256):
    t = dr1.shape[0]
    hb = tm // SUBLANES
    last_block = t // SUBLANES - 1

    def body(dr1_r, dr1n_r, ya0_r, ya_r, yb_r, h_r, cg_r, bg_r, ga_r, gb_r, hp_r, cgp_r, bgn_r, gbn_r,
             glu_w_r, glu_b_r, wso_r, cw_r, wco_r, wo_r,
             dya0_r, dproj_r, dbias_r, gwo_r, gwso_r, gwco_r, gglu_w_r, gglu_b_r, gconv_r):
        i = pl.program_id(0)

        @pl.when(i == 0)
        def _():
            for r in (dbias_r, gwo_r, gwso_r, gwco_r, gglu_w_r, gglu_b_r, gconv_r):
                r[...] = jnp.zeros_like(r)

        dr1_v = dr1_r[...]
        dmerged = _dot_t(dr1_v, wo_r[...])
        sa = _sigmoid(ga_r[...])
        sb = _sigmoid(gb_r[...])
        ya_v = ya_r[...]
        yb_v = yb_r[...]
        gwo_r[...] += _tdot(sa * ya_v + sb * yb_v, dr1_v)
        dya = dmerged * sa
        dyb = dmerged * sb
        dga = dmerged * ya_v * (sa * (1.0 - sa))
        dgb = dmerged * yb_v * (sb * (1.0 - sb))

        g, gelu_grad = _gelu_parts(ya0_r[...])
        s1 = _sigmoid(_dot(g, glu_w_r[...]) + glu_b_r[...])
        ya1 = g * s1
        dya1 = jnp.zeros((tm, SSM_W), F32)
        for j in range(N_SHARDS):
            dya_j = dya[:, 256 * j:256 * (j + 1)]
            gwso_r[j] += _tdot(ya1, dya_j)
            dya1 = dya1 + _dot_t(dya_j, wso_r[j])
        dz1 = dya1 * g * (s1 * (1.0 - s1))
        gglu_b_r[...] += jnp.sum(dz1, axis=0, keepdims=True)
        gglu_w_r[...] += _tdot(g, dz1)
        dya0_r[...] = (dya1 * s1 + _dot_t(dz1, glu_w_r[...])) * gelu_grad

        cw = cw_r[...]
        h = h_r[...]
        cg = cg_r[...]
        bg = bg_r[...]
        q = cg * h
        q_prev = jnp.where(i > 0, cgp_r[...] * hp_r[...], 0.0)
        q1 = _shift_down(q, q_prev, 1)
        q2 = _shift_down(q, q_prev, 2)
        z = cw[2:3, :] * q + cw[1:2, :] * q1 + cw[0:1, :] * q2
        yb0 = bg * z
        dyb0 = jnp.zeros((tm, SSM_W), F32)
        for j in range(N_SHARDS):
            dyb_j = dyb[:, 256 * j:256 * (j + 1)]
            gwco_r[j] += _tdot(yb0, dyb_j)
            dyb0 = dyb0 + _dot_t(dyb_j, wco_r[j])
        dbg = dyb0 * z
        dz = dyb0 * bg
        dyb_n = _dot_t(dr1n_r[...], wo_r[...]) * _sigmoid(gbn_r[...])
        dyb0_n = jnp.zeros((SUBLANES, SSM_W), F32)
        for j in range(N_SHARDS):
            dyb0_n = dyb0_n + _dot_t(dyb_n[:, 256 * j:256 * (j + 1)], wco_r[j])
        dz_next = jnp.where(i < pl.num_programs(0) - 1, dyb0_n * bgn_r[...], 0.0)
        dq = cw[2:3, :] * dz + cw[1:2, :] * _shift_up(dz, dz_next, 1) + cw[0:1, :] * _shift_up(dz, dz_next, 2)
        gconv_r[0:1, :] += jnp.sum(dz * q2, axis=0, keepdims=True)
        gconv_r[1:2, :] += jnp.sum(dz * q1, axis=0, keepdims=True)
        gconv_r[2:3, :] += jnp.sum(dz * q, axis=0, keepdims=True)
        dh = dq * cg
        dcg = dq * h

        dproj_r[:, 0:512] = jnp.zeros((tm, SSM_W), BF16)
        pieces = ((512, dh), (1024, dcg), (1536, dbg), (2048, dga), (3072, dgb))
        for off, val in pieces:
            w = val.shape[1]
            dproj_r[:, off:off + w] = val.astype(BF16)
            dbias_r[:, off:off + w] += jnp.sum(val, axis=0, keepdims=True)

    def col(w, c):
        return pl.BlockSpec((tm, w), lambda i: (i, c))

    def prev(c):
        return pl.BlockSpec((SUBLANES, SSM_W), lambda i: (jnp.maximum(i * hb - 1, 0), c))

    def nxt(w, c):
        return pl.BlockSpec((SUBLANES, w), lambda i: (jnp.minimum((i + 1) * hb, last_block), c))

    sh = jax.ShapeDtypeStruct
    return _launch(
        body, comms, name="mixer_backward", grid=(t // tm,),
        out_shape=(sh((t, SSM_W), F32), sh((t, IN_COLS), BF16), sh((1, IN_COLS), F32),
                   sh((D_MODEL, D_MODEL), F32), sh((N_SHARDS, SSM_W, 256), F32), sh((N_SHARDS, SSM_W, 256), F32),
                   sh((SSM_W, SSM_W), F32), sh((1, SSM_W), F32), sh((SUBLANES, SSM_W), F32)),
        in_specs=[col(D_MODEL, 0), nxt(D_MODEL, 0), col(SSM_W, 0), col(D_MODEL, 0), col(D_MODEL, 0),
                  col(SSM_W, 1), col(SSM_W, 2), col(SSM_W, 3), col(D_MODEL, 2), col(D_MODEL, 3),
                  prev(1), prev(2), nxt(SSM_W, 3), nxt(D_MODEL, 3),
                  _const((SSM_W, SSM_W)), _const((1, SSM_W)), _const((N_SHARDS, SSM_W, 256)),
                  _const((SUBLANES, SSM_W)), _const((N_SHARDS, SSM_W, 256)), _const((D_MODEL, D_MODEL))],
        out_specs=(col(SSM_W, 0), col(IN_COLS, 0), _const((1, IN_COLS)),
                   _const((D_MODEL, D_MODEL)), _const((N_SHARDS, SSM_W, 256)), _const((N_SHARDS, SSM_W, 256)),
                   _const((SSM_W, SSM_W)), _const((1, SSM_W)), _const((SUBLANES, SSM_W))),
        sem=("arbitrary",),
    )(dr1, dr1, ya0, ya, yb, proj, proj, proj, proj, proj, proj, proj, proj, proj,
      glu_w, glu_b, wso_st, conv_w8, wco_st, w_o)


def _cmulc_add(xr, xi, mr, mi, sr, si):
    return xr + (mr * sr + mi * si), xi + (mr * si - mi * sr)


def _ssm_backward(dya0, proj, xsr, xsi, bbr, bbi, ctr, cti, d_skip, tab_a, tab_p, dproj, comms=(), tc=SCAN_CHUNK):
    t = proj.shape[0]
    nk = t // tc

    def body(dy_r, u_r, xsr_r, xsi_r, bbr_r, bbi_r, ctr_r, cti_r, d_r, ta_r, tp_r, dproj_any,
             du_r, dus_r, gbbr_r, gbbi_r, gctr_r, gcti_r, glbr_r, glbi_r, gd_r,
             gr_s, gi_s, dyi_s, ui_s, dui_s, dun_s, car_r, car_i):
        del dproj_any

        @pl.when(pl.program_id(1) == 0)
        def _():
            for r in (car_r, car_i, dus_r, gbbr_r, gbbi_r, gctr_r, gcti_r, glbr_r, glbi_r, gd_r):
                r[...] = jnp.zeros_like(r)

        _interleave(dy_r, dyi_s)
        _interleave(u_r, ui_s)
        dy = dyi_s[...]
        u = ui_s[...]
        gr_s[...] = _dot(dy, ctr_r[...])
        gi_s[...] = -_dot(dy, cti_r[...])
        a_r, a_i = ta_r[0], ta_r[1]

        def local(n, carry):
            rows = _step_rows(SCAN_STEPS - 1 - n)
            gr, gi = _cmulc_add(gr_s[rows, :], gi_s[rows, :], a_r, a_i, *carry)
            gr_s[rows, :] = gr
            gi_s[rows, :] = gi
            return gr, gi

        zero = jnp.zeros((SUBLANES, STRIP), F32)
        ends_r, ends_i = _scan_steps(local, (zero, zero))
        ent_r, ent_i, out_r, out_i = _segment_states(
            car_r[...], car_i[...], ends_r, ends_i, ta_r[2, 0:1, :], -ta_r[3, 0:1, :], range(SUBLANES - 1, -1, -1))
        car_r[...] = out_r
        car_i[...] = out_i

        def entering(n, carry, power_r, power_i):
            gnr, gni, ar, ai = carry
            rows = _step_rows(SCAN_STEPS - 1 - n)
            gr, gi = _cmulc_add(gr_s[rows, :], gi_s[rows, :], power_r, power_i, ent_r, ent_i)
            gr_s[rows, :] = gr
            gi_s[rows, :] = gi
            xr = xsr_r[rows, :]
            xi = xsi_r[rows, :]
            return gr, gi, ar + (xr * gnr + xi * gni), ai + (xr * gni - xi * gnr)

        _, _, ar, ai = _scan_steps(entering, (ent_r, ent_i, zero, zero), powers=tp_r)
        glbr_r[...] += ar
        glbi_r[...] += ai
        gr = gr_s[...]
        gi = gi_s[...]
        dui_s[...] = _dot_t(gr, bbr_r[...]) + _dot_t(gi, bbi_r[...]) + d_r[...] * dy
        _deinterleave(dui_s, dun_s)
        du = dun_s[...]
        du_r[...] = du.astype(BF16)
        dus_r[...] += jnp.sum(du, axis=0, keepdims=True)
        gd_r[...] += jnp.sum(dy * u, axis=0, keepdims=True)
        gbbr_r[...] += _tdot(u, gr)
        gbbi_r[...] += _tdot(u, gi)
        gctr_r[...] += _tdot(dy, xsr_r[...])
        gcti_r[...] -= _tdot(dy, xsi_r[...])

    def rev(w):
        return pl.BlockSpec((tc, w), lambda j, k: (nk - 1 - k, j))

    strip_mat = pl.BlockSpec((128, STRIP), lambda j, k: (j, 0))
    vec = pl.BlockSpec((1, 128), lambda j, k: (0, j))
    lbacc = pl.BlockSpec((SUBLANES, STRIP), lambda j, k: (0, j))
    sh = jax.ShapeDtypeStruct
    return _launch(
        body, comms, name="ssm_backward", grid=(N_STRIPS, nk),
        out_shape=(sh((t, IN_COLS), BF16), sh((1, SSM_W), F32),
                   sh((SSM_W, STRIP), F32), sh((SSM_W, STRIP), F32), sh((SSM_W, STRIP), F32), sh((SSM_W, STRIP), F32),
                   sh((SUBLANES, STATE_COLS), F32), sh((SUBLANES, STATE_COLS), F32), sh((1, SSM_W), F32)),
        in_specs=[rev(128), rev(128), rev(STRIP), rev(STRIP),
                  strip_mat, strip_mat, strip_mat, strip_mat, vec,
                  pl.BlockSpec((4, SUBLANES, STRIP), lambda j, k: (0, 0, j)),
                  pl.BlockSpec((2, SCAN_STEPS, STRIP), lambda j, k: (0, 0, j)), ANY],
        out_specs=(rev(128), vec, strip_mat, strip_mat, strip_mat, strip_mat, lbacc, lbacc, vec),
        scratch_shapes=[pltpu.VMEM((tc, STRIP), F32), pltpu.VMEM((tc, STRIP), F32)]
        + [pltpu.VMEM((tc, 128), F32)] * 4 + [pltpu.VMEM((1, STRIP), F32)] * 2,
        aliases={11: 0}, sem=("parallel", "arbitrary"),
    )(dya0, proj, xsr, xsi, bbr, bbi, ctr, cti, d_skip, tab_a, tab_p, dproj)


def _input_grad(dr1, dproj, w_in_st, comms=(), tm=512):
    t = dr1.shape[0]

    def body(dr1_r, dp_r, w_r, dx_r):
        acc = ALPHA * dr1_r[...]
        for j in range(N_SHARDS):
            acc = acc + lax.dot_general(dp_r[:, D_MODEL * j:D_MODEL * (j + 1)], w_r[j],
                                        (((1,), (1,)), ((), ())), preferred_element_type=F32)
        dx_r[...] = acc

    (dx,), sent = _launch(
        body, comms, name="input_grad", grid=(t // tm,),
        out_shape=(jax.ShapeDtypeStruct((t, D_MODEL), F32),),
        in_specs=[pl.BlockSpec((tm, D_MODEL), lambda i: (i, 0)), pl.BlockSpec((tm, IN_COLS), lambda i: (i, 0)),
                  _const((N_SHARDS, D_MODEL, D_MODEL))],
        out_specs=(pl.BlockSpec((tm, D_MODEL), lambda i: (i, 0)),),
        sem=("parallel",),
    )(dr1, dproj, w_in_st)
    return dx, sent


def _in_weight_grad(x, dproj, comms=(), tk=2048):
    t = x.shape[0]

    def body(x_r, dp_r, gw_r):
        @pl.when(pl.program_id(1) == 0)
        def _():
            gw_r[...] = jnp.zeros_like(gw_r)

        gw_r[0] += _tdot(x_r[...], dp_r[...])

    (g_w_in,), sent = _launch(
        body, comms, name="in_weight_grad", grid=(N_SHARDS, t // tk),
        out_shape=(jax.ShapeDtypeStruct((N_SHARDS, D_MODEL, D_MODEL), F32),),
        in_specs=[pl.BlockSpec((tk, D_MODEL), lambda j, k: (k, 0)), pl.BlockSpec((tk, D_MODEL), lambda j, k: (k, j))],
        out_specs=(pl.BlockSpec((1, D_MODEL, D_MODEL), lambda j, k: (j, 0, 0)),),
        sem=("parallel", "arbitrary"),
    )(x, dproj)
    return g_w_in, sent


MIXER_W = ("glu_w", "w_ssm_out", "w_conv_out", "w_o")
FFN_W = ("w_gate", "w_up", "w_down")


def _device_step(x, target, small, shards, c_arr, me_arr):
    lr, li = small["ssm_lambda_re"][0], small["ssm_lambda_im"][0]
    ldt = small["ssm_log_dt"][0][:, None]
    rep16 = lambda a: jnp.broadcast_to(a[:, None, :], (N_GROUPS, GROUP_C, a.shape[-1])).reshape(SSM_W, a.shape[-1])
    lr16, li16 = rep16(lr), rep16(li)
    ldt16 = rep16(jnp.broadcast_to(ldt, (N_GROUPS, N_STATE)))
    brt = small["ssm_b_re"][0].transpose(0, 2, 1).reshape(SSM_W, N_STATE)
    bit = small["ssm_b_im"][0].transpose(0, 2, 1).reshape(SSM_W, N_STATE)
    cre = small["ssm_c_re"][0].reshape(SSM_W, N_STATE)
    cim = small["ssm_c_im"][0].reshape(SSM_W, N_STATE)
    disc = (lr, li, ldt, lr16, li16, ldt16, brt, bit)

    pwr, pwi, bbr, bbi, ctr, cti = _ssm_prepare(*disc, cre, cim)
    tab_a, tab_p = _scan_tables(pwr, pwi)

    first_sh = [shards[n] for n in MIXER_W + FFN_W[:1]]
    second_sh = [shards[n] for n in FFN_W[1:]]
    (w_in_st,) = _gather_weights([shards["w_in"]])
    proj, (arrived,) = _in_proj(x, w_in_st, small["b_in"], comms=[_gather_ici(first_sh, [shards["conv_w"]])])
    (xsr, xsi, ya0), (second_part, first_st) = _ssm_forward(
        proj, bbr, bbi, ctr, cti, small["ssm_d"], tab_a, tab_p,
        comms=[_gather_ici(second_sh), _gather_d2d(arrived[:len(first_sh)], first_sh)])
    glu_st, wso_st, wco_st, wo_st, wg_st = first_st
    conv_st = arrived[len(first_sh)]
    conv_w8 = jnp.pad(conv_st[:, :3, :].transpose(1, 0, 2).reshape(3, SSM_W), ((0, SUBLANES - 3), (0, 0)))
    w_o = wo_st.reshape(D_MODEL, D_MODEL)
    glu_w = glu_st.reshape(SSM_W, SSM_W)
    (xhat1, rstd1, ya, yb), (second_st,) = _mixer_forward(
        x, proj, ya0, glu_w, small["glu_b"], wso_st, conv_w8, wco_st, w_o, comms=[_gather_d2d(second_part, second_sh)])
    wu_st, wd_st = second_st
    (loss, dr1, x1b, dr2b, hid, dhg, dhu, g_ln2_g, g_ln2_b, g_ln1_g, g_ln1_b) = _ffn_step(
        xhat1, rstd1, target, small["ln1_g"], small["ln1_b"], small["ln2_g"], small["ln2_b"], wg_st, wu_st, wd_st)

    add_halves = lambda gs, rs: _per_shape(lambda a, b: _add_own_half(a, b, c_arr), list(gs), list(rs))
    sum_chips = lambda owns, slots: _per_shape(lambda a, b: _sum_chips(a, b, me_arr), list(owns), list(slots))
    g_ffn = _ffn_weight_grads(x1b, dr2b, hid, dhg, dhu)
    (dya0, dproj, dbias, g_wo, g_wso, g_wco, g_glu_w, g_glu_b, g_conv8), (got_ffn,) = _mixer_backward(
        dr1, proj, ya0, ya, yb, glu_w, small["glu_b"], wso_st, conv_w8, wco_st, w_o, comms=[_swap_comm(g_ffn)])
    chip_ffn = add_halves(g_ffn, got_ffn)
    g_mix = [g_glu_w.reshape(N_SHARDS, 128, SSM_W), g_wso, g_wco, g_wo.reshape(N_SHARDS, 256, D_MODEL)]
    (dproj, dus, gbbr, gbbi, gctr, gcti, glbr, glbi, g_d), (slots_ffn, got_mix) = _ssm_backward(
        dya0, proj, xsr, xsi, bbr, bbi, ctr, cti, small["ssm_d"], tab_a, tab_p, dproj,
        comms=[_scatter_comm(chip_ffn), _swap_comm(g_mix)])
    halves_ffn = sum_chips(chip_ffn, slots_ffn)
    chip_mix = add_halves(g_mix, got_mix)
    g_lr, g_li, g_ldt, g_brt, g_bit, g_cre, g_cim = _ssm_param_grads(
        *disc, glbr.reshape(SUBLANES, N_GROUPS, N_STATE), glbi.reshape(SUBLANES, N_GROUPS, N_STATE),
        gbbr, gbbi, gctr, gcti)
    g_w_in, (others_ffn, slots_mix) = _in_weight_grad(
        x, dproj, comms=[_send_comm(halves_ffn), _scatter_comm(chip_mix)])
    halves_mix = sum_chips(chip_mix, slots_mix)
    dx, _ = _input_grad(dr1, dproj, w_in_st)

    g_conv = jnp.pad(g_conv8[:3].reshape(3, N_SHARDS, 128).transpose(1, 0, 2), ((0, 0), (0, SUBLANES - 3), (0, 0)))
    pieces = [dus, dbias[:, SSM_W:], g_lr, g_li, g_ldt, g_brt, g_bit, g_cre, g_cim, g_d, g_glu_b,
              g_ln1_g, g_ln1_b, g_ln2_g, g_ln2_b, loss]
    flat = jnp.concatenate([p.reshape(-1) for p in pieces])
    g_packed = jnp.pad(flat, (0, PACKED_ROWS * 128 - flat.shape[0])).reshape(PACKED_ROWS, 128)
    ((got_w, got_conv, got_packed),) = _standalone([_swap_comm([g_w_in], [g_conv, g_packed])], "swap_with_sibling")
    (chip_w,) = add_halves([g_w_in], [got_w])
    chip_conv, chip_packed = _small_pair_sums([g_conv, g_packed], [got_conv, got_packed])
    ((slots_w, slots_conv, slots_packed),) = _standalone(
        [_scatter_comm([chip_w, chip_conv], [chip_packed])], "scatter_to_chips")
    (halves_w,) = sum_chips([chip_w], [slots_w])
    conv_total, packed_mine = _small_totals(me_arr, c_arr, chip_conv, slots_conv, chip_packed, slots_packed)
    (others_rest,) = _standalone([_send_comm([halves_w] + halves_mix + [packed_mine])], "send_to_sibling")
    packed_other = others_rest[-1]
    south = c_arr[0] == 0
    packed_total = jnp.concatenate([jnp.where(south, packed_mine, packed_other),
                                    jnp.where(south, packed_other, packed_mine)])

    pairs = dict(zip(FFN_W, zip(halves_ffn, others_ffn)))
    pairs.update(zip(("w_in",) + MIXER_W, zip([halves_w] + halves_mix, others_rest[:-1])))
    return dx, pairs, conv_total, packed_total


PACKED_ROWS = 1136
PACKED_LAYOUT = (("b_in", IN_COLS), ("ssm_lambda_re", STATE_COLS), ("ssm_lambda_im", STATE_COLS),
                 ("ssm_log_dt", N_GROUPS), ("ssm_b_re", SSM_W * N_STATE), ("ssm_b_im", SSM_W * N_STATE),
                 ("ssm_c_re", SSM_W * N_STATE), ("ssm_c_im", SSM_W * N_STATE), ("ssm_d", SSM_W), ("glu_b", SSM_W),
                 ("ln1_g", D_MODEL), ("ln1_b", D_MODEL), ("ln2_g", D_MODEL), ("ln2_b", D_MODEL), ("loss", 1))


def _unpack_small(packed):
    flat = packed.reshape(-1)
    out, off = {}, 0
    for name, size in PACKED_LAYOUT:
        out[name] = flat[off:off + size]
        off += size
    for name in ("ssm_b_re", "ssm_b_im"):
        out[name] = out[name].reshape(N_GROUPS, GROUP_C, N_STATE).transpose(0, 2, 1)[None]
    for name in ("ssm_c_re", "ssm_c_im"):
        out[name] = out[name].reshape(1, N_GROUPS, GROUP_C, N_STATE)
    for name in ("ssm_lambda_re", "ssm_lambda_im"):
        out[name] = out[name].reshape(1, N_GROUPS, N_STATE)
    for name in ("b_in", "ssm_log_dt", "ssm_d", "glu_b", "ln1_g", "ln1_b", "ln2_g", "ln2_b"):
        out[name] = out[name][None]
    return out


BIG = ("w_in", "glu_w", "w_ssm_out", "w_conv_out", "w_o", "w_gate", "w_up", "w_down")
SMALL = ("b_in", "ssm_lambda_re", "ssm_lambda_im", "ssm_log_dt", "ssm_b_re", "ssm_b_im", "ssm_c_re", "ssm_c_im",
         "ssm_d", "glu_b", "ln1_g", "ln1_b", "ln2_g", "ln2_b")
WEIGHTS = ("w_in", "b_in", "ssm_lambda_re", "ssm_lambda_im", "ssm_log_dt", "ssm_b_re", "ssm_b_im", "ssm_c_re",
           "ssm_c_im", "ssm_d", "glu_w", "glu_b", "w_ssm_out", "conv_w", "w_conv_out", "w_o", "ln1_g", "ln1_b",
           "w_gate", "w_up", "w_down", "ln2_g", "ln2_b")


def _place():
    x, y, c = lax.axis_index("x"), lax.axis_index("y"), lax.axis_index("c")
    chips = [(1 - x, y), (x, 1 - y), (1 - x, 1 - y)]
    return x, y, c, chips


def _shard_of(chip):
    return 2 * chip[0] + chip[1]


def _remote(src, dst, send_sem, recv_sem, to):
    return pltpu.make_async_remote_copy(src_ref=src, dst_ref=dst, send_sem=send_sem, recv_sem=recv_sem,
                                        device_id=to, device_id_type=MESH)


def _half_rows(shard, which):
    r2 = shard.shape[0] // 2
    return pl.ds(pl.multiple_of(which * r2, 16), r2)


def _gather_ici(halved, whole=()):
    shards = list(halved) + list(whole)
    nh = len(halved)

    def copies(src, dst, sems):
        send_sem, recv_sem = sems[:2]
        x, y, c, chips = _place()
        me = _shard_of((x, y))
        out = []
        for a in range(len(shards)):
            for k, chip in enumerate(chips):
                if a < nh:
                    rows = _half_rows(shards[a], c)
                    out.append(_remote(src[a].at[rows], dst[a].at[me, rows], send_sem.at[a, k], recv_sem.at[a, k],
                                       (*chip, c)))
                else:
                    out.append(_remote(src[a], dst[a].at[me], send_sem.at[a, k], recv_sem.at[a, k], (*chip, c)))
        return out

    n = len(shards)

    def bounce(src, dst, sems):
        local_sem, buffers = sems[2], sems[3:]
        me = _shard_of((lax.axis_index("x"), lax.axis_index("y")))
        return [(pltpu.make_async_copy(src[a], buffers[a], local_sem.at[a, 0]),
                 pltpu.make_async_copy(buffers[a], dst[a].at[me], local_sem.at[a, 1])) for a in range(n)]

    scratch = [pltpu.SemaphoreType.DMA((n, 3))] * 2 + [pltpu.SemaphoreType.DMA((n, 2))]
    scratch += [pltpu.VMEM(s.shape, s.dtype) for s in shards]
    return _Comm(shards, [jax.ShapeDtypeStruct((N_SHARDS,) + s.shape, s.dtype) for s in shards], scratch, copies,
                 bounce=bounce)


def _gather_d2d(stacks, shards):
    def copies(src, dst, sems):
        del src
        send_sem, recv_sem = sems
        x, y, c, chips = _place()
        out = []
        for a in range(len(stacks)):
            for k, chip in enumerate(chips):
                rows = dst[a].at[_shard_of(chip), _half_rows(shards[a], c)]
                out.append(_remote(rows, rows, send_sem.at[a, k], recv_sem.at[a, k], (x, y, 1 - c)))
        return out

    n = len(stacks)
    return _Comm(stacks, [jax.ShapeDtypeStruct(s.shape, s.dtype) for s in stacks],
                 [pltpu.SemaphoreType.DMA((n, 3))] * 2, copies, aliased=True)


def _standalone(comms, name):
    return _launch(None, comms, name=name, grid=(), in_specs=[], out_specs=(), out_shape=())()[1]


def _gather_weights(shards):
    n = len(shards)

    def body(*refs):
        src, dst = refs[:n], refs[n:2 * n]
        send_sem, recv_sem, fsend_sem, frecv_sem, local_sem = refs[2 * n:2 * n + 5]
        buffers = refs[2 * n + 5:]
        x, y, c, chips = _place()
        me = _shard_of((x, y))
        sibling = (x, y, 1 - c)
        own = [(pltpu.make_async_copy(src[a], buffers[a], local_sem.at[a, 0]),
                pltpu.make_async_copy(buffers[a], dst[a].at[me], local_sem.at[a, 1])) for a in range(n)]
        for to_vmem, _ in own:
            to_vmem.start()
        sends = []
        for a in range(n):
            mine = _half_rows(shards[a], c)
            for k, chip in enumerate(chips):
                cp = _remote(src[a].at[mine], dst[a].at[me, mine], send_sem.at[a, k], recv_sem.at[a, k], (*chip, c))
                cp.start()
                sends.append(cp)
        for to_vmem, to_hbm in own:
            to_vmem.wait()
            to_hbm.start()
        for a in range(n):
            for k, chip in enumerate(chips):
                rows = dst[a].at[_shard_of(chip), _half_rows(shards[a], c)]
                _remote(rows, rows, send_sem.at[a, k], recv_sem.at[a, k], sibling).wait_recv()
                cp = _remote(rows, rows, fsend_sem.at[a, k], frecv_sem.at[a, k], sibling)
                cp.start()
                sends.append(cp)
        for a in range(n):
            for k, chip in enumerate(chips):
                rows = dst[a].at[_shard_of(chip), _half_rows(shards[a], 1 - c)]
                _remote(rows, rows, fsend_sem.at[a, k], frecv_sem.at[a, k], sibling).wait_recv()
        for cp in sends:
            cp.wait_send()
        for _, to_hbm in own:
            to_hbm.wait()

    return pl.pallas_call(
        body, name="gather_weights",
        out_shape=tuple(jax.ShapeDtypeStruct((N_SHARDS,) + s.shape, s.dtype) for s in shards),
        in_specs=[ANY] * n, out_specs=(ANY,) * n,
        scratch_shapes=[pltpu.SemaphoreType.DMA((n, 3))] * 4 + [pltpu.SemaphoreType.DMA((n, 2))]
        + [pltpu.VMEM(s.shape, s.dtype) for s in shards],
    )(*shards)


def _swap_comm(big, small=()):
    nb, n = len(big), len(big) + len(small)
    arrays = list(big) + list(small)

    def copies(src, dst, sems):
        send_sem, recv_sem = sems
        x, y, c, _ = _place()
        out = []
        for a in range(n):
            if a < nb:
                r2 = arrays[a].shape[1] // 2
                part = src[a].at[:, pl.ds(pl.multiple_of((1 - c) * r2, SUBLANES), r2), :]
            else:
                part = src[a]
            out.append(_remote(part, dst[a], send_sem.at[a], recv_sem.at[a], (x, y, 1 - c)))
        return out

    out_shape = [jax.ShapeDtypeStruct((N_SHARDS, g.shape[1] // 2, g.shape[2]), g.dtype) for g in big]
    out_shape += [jax.ShapeDtypeStruct(g.shape, g.dtype) for g in small]
    return _Comm(arrays, out_shape, [pltpu.SemaphoreType.DMA((n,))] * 2, copies)


def _scatter_comm(slabbed, small=()):
    ns, n = len(slabbed), len(slabbed) + len(small)
    arrays = list(slabbed) + list(small)

    def copies(src, dst, sems):
        send_sem, recv_sem = sems
        _, _, c, chips = _place()
        out = []
        for a in range(n):
            for k, chip in enumerate(chips):
                if a < ns:
                    part = src[a].at[_shard_of(chip)]
                else:
                    h = arrays[a].shape[0] // 2
                    part = src[a].at[pl.ds(pl.multiple_of(c * h, SUBLANES), h), :]
                out.append(_remote(part, dst[a].at[k], send_sem.at[a, k], recv_sem.at[a, k], (*chip, c)))
        return out

    out_shape = [jax.ShapeDtypeStruct((3,) + g.shape[1:], g.dtype) for g in slabbed]
    out_shape += [jax.ShapeDtypeStruct((3, g.shape[0] // 2, g.shape[1]), g.dtype) for g in small]
    return _Comm(arrays, out_shape, [pltpu.SemaphoreType.DMA((n, 3))] * 2, copies)


def _send_comm(arrays):
    n = len(arrays)

    def copies(src, dst, sems):
        send_sem, recv_sem = sems
        x, y, c, _ = _place()
        return [_remote(src[a], dst[a], send_sem.at[a], recv_sem.at[a], (x, y, 1 - c)) for a in range(n)]

    return _Comm(arrays, [jax.ShapeDtypeStruct(h.shape, h.dtype) for h in arrays],
                 [pltpu.SemaphoreType.DMA((n,))] * 2, copies)


def _row_chunk(rows):
    for cand in (256, 176, 128, 64):
        if rows % cand == 0:
            return cand
    return rows


def _per_shape(fn, *lists):
    groups = {}
    for i, items in enumerate(zip(*lists)):
        groups.setdefault(tuple(a.shape for a in items), []).append(i)
    out = [None] * len(lists[0])
    for idx in groups.values():
        for i, r in zip(idx, fn(*([lst[i] for i in idx] for lst in lists))):
            out[i] = r
    return out


def _add_own_half(stacks, receiveds, c):
    n = len(stacks)
    _, r2, cols = receiveds[0].shape

    def body(c_ref, *refs):
        del c_ref
        for a in range(n):
            refs[2 * n + a][...] = (refs[a][...] + refs[n + a][...]).astype(BF16)

    own = pl.BlockSpec((1, r2, cols), lambda s, c_ref: (s, c_ref[0], 0))
    got = pl.BlockSpec((1, r2, cols), lambda s, c_ref: (s, 0, 0))
    return pl.pallas_call(
        body, name="add_own_half",
        grid_spec=pltpu.PrefetchScalarGridSpec(
            num_scalar_prefetch=1, grid=(N_SHARDS,), in_specs=[own] * n + [got] * n, out_specs=(got,) * n),
        out_shape=(jax.ShapeDtypeStruct(receiveds[0].shape, BF16),) * n,
        compiler_params=_params(("parallel",)),
    )(c, *stacks, *receiveds)


def _chip_order_sum(me, own, s0, s1, s2):
    terms = []
    for s in range(N_SHARDS):
        d = jnp.bitwise_xor(me, s)
        terms.append(jnp.where(d == 0, own, jnp.where(d == 2, s0, jnp.where(d == 1, s1, s2))))
    return ((terms[0] + terms[1]) + terms[2]) + terms[3]


def _sum_chips(own_stacks, slots, me):
    n = len(slots)
    _, rows, cols = slots[0].shape
    rc = _row_chunk(rows)

    def body(me_ref, *refs):
        del me_ref
        for a in range(n):
            own_r, s_r = refs[a], refs[n + a]
            refs[2 * n + a][...] = (((own_r[0].astype(F32) + s_r[0].astype(F32)) + s_r[1].astype(F32))
                                    + s_r[2].astype(F32))

    own = pl.BlockSpec((1, rc, cols), lambda i, me_ref: (me_ref[0], i, 0))
    three = pl.BlockSpec((3, rc, cols), lambda i, me_ref: (0, i, 0))
    total = pl.BlockSpec((rc, cols), lambda i, me_ref: (i, 0))
    return pl.pallas_call(
        body, name="sum_chips",
        grid_spec=pltpu.PrefetchScalarGridSpec(
            num_scalar_prefetch=1, grid=(rows // rc,), in_specs=[own] * n + [three] * n, out_specs=(total,) * n),
        out_shape=(jax.ShapeDtypeStruct((rows, cols), F32),) * n,
        compiler_params=_params(("parallel",)),
    )(me, *own_stacks, *slots)


def _small_pair_sums(mine, theirs):
    n = len(mine)

    def body(*refs):
        for a in range(n):
            refs[2 * n + a][...] = refs[a][...] + refs[n + a][...]

    vm = pl.BlockSpec(memory_space=pltpu.VMEM)
    return pl.pallas_call(
        body, name="small_pair_sums", out_shape=tuple(jax.ShapeDtypeStruct(g.shape, g.dtype) for g in mine),
        in_specs=[vm] * (2 * n), out_specs=(vm,) * n,
        compiler_params=pltpu.CompilerParams(vmem_limit_bytes=VMEM_LIMIT),
    )(*mine, *theirs)


def _adam_math(w, g, m, v):
    m = ADAM_B1 * m + (1.0 - ADAM_B1) * g
    v = ADAM_B2 * v + (1.0 - ADAM_B2) * (g * g)
    m_hat = m / (1.0 - ADAM_B1 ** ADAM_STEP)
    v_hat = v / (1.0 - ADAM_B2 ** ADAM_STEP)
    delta = -ADAM_LR * (m_hat / (jnp.sqrt(v_hat) + ADAM_EPS) + ADAM_WD * w)
    return delta, m, v


def _small_totals(me, c, conv_stack, conv_slots, packed, packed_slots):
    half = packed.shape[0] // 2

    def body(me_ref, c_ref, cs_r, cslot_r, p_r, pslot_r, conv_r, tot_r):
        me_ = me_ref[0]
        conv_r[...] = _chip_order_sum(me_, cs_r[me_], cslot_r[0], cslot_r[1], cslot_r[2])
        own = p_r[pl.ds(pl.multiple_of(c_ref[0] * half, SUBLANES), half), :]
        tot_r[...] = _chip_order_sum(me_, own, pslot_r[0], pslot_r[1], pslot_r[2])

    vm = pl.BlockSpec(memory_space=pltpu.VMEM)
    sm = pl.BlockSpec(memory_space=pltpu.SMEM)
    return pl.pallas_call(
        body, name="small_totals",
        out_shape=(jax.ShapeDtypeStruct(conv_stack.shape[1:], F32), jax.ShapeDtypeStruct((half, packed.shape[1]), F32)),
        in_specs=[sm, sm] + [vm] * 4, out_specs=(vm, vm),
    )(me, c, conv_stack, conv_slots, packed, packed_slots)


def _adam_small(gs, ws, ms, vs):
    n = len(gs)

    def body(*refs):
        for a in range(n):
            g_r, w_r, m_r, v_r = (refs[i * n + a] for i in range(4))
            d_r, nm_r, nv_r = (refs[(4 + i) * n + a] for i in range(3))
            d_r[...], nm_r[...], nv_r[...] = _adam_math(w_r[...], g_r[...], m_r[...], v_r[...])

    vm = pl.BlockSpec(memory_space=pltpu.VMEM)
    shapes = tuple(jax.ShapeDtypeStruct(w.shape, F32) for w in ws)
    out = pl.pallas_call(
        body, name="adam_small", out_shape=shapes * 3, in_specs=[vm] * (4 * n), out_specs=(vm,) * (3 * n),
        compiler_params=pltpu.CompilerParams(vmem_limit_bytes=VMEM_LIMIT),
    )(*gs, *ws, *ms, *vs)
    return out[:n], out[n:2 * n], out[2 * n:]


def _adam_big(ws, mines, others, ms, vs, c):
    n = len(ws)
    r2, cols = mines[0].shape
    rc = _row_chunk(r2)
    nch = r2 // rc

    def body(c_ref, *refs):
        mine_is_here = pl.program_id(0) == c_ref[0]
        for a in range(n):
            w_r, mine_r, other_r, m_r, v_r = (refs[i * n + a] for i in range(5))
            g_r, d_r, nm_r, nv_r = (refs[(5 + i) * n + a] for i in range(4))
            g = jnp.where(mine_is_here, mine_r[...], other_r[...])
            g_r[...] = g
            d_r[...], nm_r[...], nv_r[...] = _adam_math(w_r[...], g, m_r[...], v_r[...])

    full = pl.BlockSpec((rc, cols), lambda h, i, c_ref: (h * nch + i, 0))
    half = pl.BlockSpec((rc, cols), lambda h, i, c_ref: (i, 0))
    shape = jax.ShapeDtypeStruct((2 * r2, cols), F32)
    out = pl.pallas_call(
        body, name="adam_big",
        grid_spec=pltpu.PrefetchScalarGridSpec(
            num_scalar_prefetch=1, grid=(2, nch),
            in_specs=[full] * n + [half] * (2 * n) + [full] * (2 * n), out_specs=(full,) * (4 * n)),
        out_shape=(shape,) * (4 * n), compiler_params=_params(("parallel", "parallel")),
    )(c, *ws, *mines, *others, *ms, *vs)
    return [tuple(out[i * n + a] for i in range(4)) for a in range(n)]


def kernel(x, w_in, b_in, ssm_lambda_re, ssm_lambda_im, ssm_log_dt, ssm_b_re, ssm_b_im, ssm_c_re, ssm_c_im, ssm_d, glu_w, glu_b, w_ssm_out, conv_w, w_conv_out, w_o, ln1_g, ln1_b, w_gate, w_up, w_down, ln2_g, ln2_b, loss_target, m_w_in, m_b_in, m_ssm_lambda_re, m_ssm_lambda_im, m_ssm_log_dt, m_ssm_b_re, m_ssm_b_im, m_ssm_c_re, m_ssm_c_im, m_ssm_d, m_glu_w, m_glu_b, m_w_ssm_out, m_conv_w, m_w_conv_out, m_w_o, m_ln1_g, m_ln1_b, m_w_gate, m_w_up, m_w_down, m_ln2_g, m_ln2_b, v_w_in, v_b_in, v_ssm_lambda_re, v_ssm_lambda_im, v_ssm_log_dt, v_ssm_b_re, v_ssm_b_im, v_ssm_c_re, v_ssm_c_im, v_ssm_d, v_glu_w, v_glu_b, v_w_ssm_out, v_conv_w, v_w_conv_out, v_w_o, v_ln1_g, v_ln1_b, v_w_gate, v_w_up, v_w_down, v_ln2_g, v_ln2_b):
    given = dict(locals())
    w = {n: given[n] for n in WEIGHTS}
    m = {n: given["m_" + n] for n in WEIGHTS}
    v = {n: given["v_" + n] for n in WEIGHTS}

    flip = lambda n, a: a.T if n in ("w_gate", "w_up") else a
    shards = {n: flip(n, w[n][0]).astype(BF16) for n in BIG}
    shards["conv_w"] = jnp.pad(conv_w[0], ((0, SUBLANES - 3), (0, 0)))
    c_arr = jnp.reshape(lax.axis_index("c"), (1,)).astype(jnp.int32)
    me = _shard_of((lax.axis_index("x"), lax.axis_index("y")))
    me_arr = jnp.reshape(me, (1,)).astype(jnp.int32)
    dx, pairs, conv_total, packed_total = _device_step(
        x[0], loss_target[0], {n: w[n] for n in SMALL}, shards, c_arr, me_arr)

    grad = _unpack_small(packed_total)
    loss_total = grad.pop("loss")[0]
    grad["conv_w"] = conv_total[:3][None]
    small_names = ("conv_w",) + SMALL
    swap = lambda n, a: a.transpose(0, 1, 3, 2) if n in ("ssm_b_re", "ssm_b_im") else a
    ds, nms, nvs = _adam_small(*([swap(n, d[n]) for n in small_names] for d in (grad, w, m, v)))
    delta, new_m, new_v = {}, {}, {}
    for i, n in enumerate(small_names):
        delta[n], new_m[n], new_v[n] = swap(n, ds[i]), swap(n, nms[i]), swap(n, nvs[i])
    updated = _per_shape(
        lambda *a: _adam_big(*a, c_arr),
        [flip(n, w[n][0]) for n in BIG], [pairs[n][0] for n in BIG], [pairs[n][1] for n in BIG],
        [flip(n, m[n][0]) for n in BIG], [flip(n, v[n][0]) for n in BIG])
    for n, results in zip(BIG, updated):
        grad[n], delta[n], new_m[n], new_v[n] = (flip(n, r)[None] for r in results)

    return (loss_total, dx[None], *[grad[n] for n in WEIGHTS], *[delta[n] for n in WEIGHTS],
            *[new_m[n] for n in WEIGHTS], *[new_v[n] for n in WEIGHTS])
```

```python
import functools
import math

import jax
import jax.numpy as jnp
from jax import lax
from jax.experimental import pallas as pl
from jax.experimental.pallas import tpu as pltpu

F32 = jnp.float32
BF16 = jnp.bfloat16

D_MODEL = 1024
IN_COLS = 4096
SSM_W = 512
N_GROUPS = 32
N_STATE = 64
GROUP_C = 16
STATE_COLS = N_GROUPS * N_STATE
STRIP = 512
N_STRIPS = STATE_COLS // STRIP
FFN_SHARD = 704
N_SHARDS = 4
ALPHA = 2.0 ** 0.25
LN_EPS = 1e-5
GELU_K = math.sqrt(2.0 / math.pi)
GELU_C = 0.044715

ADAM_LR = 0.001
ADAM_B1 = 0.9
ADAM_B2 = 0.999
ADAM_EPS = 1e-08
ADAM_WD = 0.01
ADAM_STEP = 10

V7X_VMEM_BYTES = 64 * 1024 * 1024
VMEM_LIMIT = V7X_VMEM_BYTES - 8 * 1024 * 1024
SUBLANES = 8
N_POWERS = 128

MESH = pl.DeviceIdType.MESH
ANY = pl.BlockSpec(memory_space=pl.ANY)


def _dot(a, b):
    return jnp.dot(a.astype(BF16), b.astype(BF16), preferred_element_type=F32)


def _dot_t(a, b):
    return lax.dot_general(a.astype(BF16), b.astype(BF16), (((1,), (1,)), ((), ())),
                           preferred_element_type=F32)


def _tdot(a, b):
    return lax.dot_general(a.astype(BF16), b.astype(BF16), (((0,), (0,)), ((), ())),
                           preferred_element_type=F32)


def _sigmoid(v):
    return 1.0 / (1.0 + jnp.exp(-v))


def _split3(v):
    hi = v.astype(BF16)
    r1 = v - hi.astype(F32)
    mid = r1.astype(BF16)
    lo = (r1 - mid.astype(F32)).astype(BF16)
    return hi, mid, lo


def _const(shape):
    nd = len(shape)
    return pl.BlockSpec(shape, lambda *_: (0,) * nd)


def _params(sem, vmem=VMEM_LIMIT):
    return pltpu.CompilerParams(dimension_semantics=sem, vmem_limit_bytes=vmem)


def _gelu_parts(v):
    inner = GELU_K * (v + GELU_C * v * v * v)
    t = jnp.tanh(inner)
    g = 0.5 * v * (1.0 + t)
    dg = 0.5 * (1.0 + t) + 0.5 * v * (1.0 - t * t) * GELU_K * (1.0 + 3.0 * GELU_C * v * v)
    return g, dg


class _Comm:
    def __init__(self, inputs, out_shape, sems, copies, aliased=False, bounce=None):
        self.inputs, self.out_shape, self.sems = list(inputs), tuple(out_shape), list(sems)
        self.copies, self.aliased, self.bounce = copies, aliased, bounce


def _launch(body, comms, *, name, grid, in_specs, out_specs, out_shape, scratch_shapes=(), aliases=None, sem=None):
    comms = list(comms)
    n_in, n_out, n_scr = len(in_specs), len(out_specs), len(scratch_shapes)
    aliases = dict(aliases or {})
    layout = []
    p_in, p_out, p_sem = n_in, n_out, 0
    for cm in comms:
        layout.append((p_in, p_out, p_sem))
        if cm.aliased:
            for i in range(len(cm.inputs)):
                aliases[p_in + i] = p_out + i
        p_in, p_out, p_sem = p_in + len(cm.inputs), p_out + len(cm.out_shape), p_sem + len(cm.sems)
    tot_in, tot_out = p_in, p_out

    def fused(*refs):
        ins, outs = refs[:tot_in], refs[tot_in:tot_in + tot_out]
        scr = refs[tot_in + tot_out:tot_in + tot_out + n_scr]
        sems = refs[tot_in + tot_out + n_scr:]

        def descriptors(kind):
            out = []
            for cm, (a, b, s) in zip(comms, layout):
                make = cm.copies if kind == "remote" else cm.bounce
                if make is not None:
                    out += make(ins[a:a + len(cm.inputs)], outs[b:b + len(cm.out_shape)], sems[s:s + len(cm.sems)])
            return out

        steps = [pl.program_id(d) for d in range(len(grid))]
        first = functools.reduce(jnp.logical_and, [s == 0 for s in steps]) if grid else None
        last = functools.reduce(jnp.logical_and, [s == g - 1 for s, g in zip(steps, grid)]) if grid else None

        def start():
            for cp in descriptors("remote"):
                cp.start()
            for to_vmem, _ in descriptors("local"):
                to_vmem.start()

        def finish():
            for to_vmem, to_hbm in descriptors("local"):
                to_vmem.wait()
                to_hbm.start()
            for cp in descriptors("remote"):
                cp.wait()
            for _, to_hbm in descriptors("local"):
                to_hbm.wait()

        if comms:
            pl.when(first)(start) if grid else start()
        if body is not None:
            body(*ins[:n_in], *outs[:n_out], *scr)
        if comms:
            pl.when(last)(finish) if grid else finish()

    specs_in = list(in_specs) + [ANY] * (tot_in - n_in)
    specs_out = tuple(out_specs) + (ANY,) * (tot_out - n_out)
    shapes = tuple(out_shape) + tuple(s for cm in comms for s in cm.out_shape)
    scratch = list(scratch_shapes) + [s for cm in comms for s in cm.sems]
    if comms or sem is None:
        sem = ("arbitrary",) * len(grid)
    kwargs = dict(grid=grid) if grid else {}
    call = pl.pallas_call(fused, name=name, out_shape=shapes, in_specs=specs_in, out_specs=specs_out,
                          scratch_shapes=scratch, input_output_aliases=aliases,
                          compiler_params=_params(sem) if grid else None, **kwargs)

    def run(*args):
        out = call(*args, *(a for cm in comms for a in cm.inputs))
        results, rest = out[:n_out], out[n_out:]
        per_comm = []
        for cm in comms:
            per_comm.append(rest[:len(cm.out_shape)])
            rest = rest[len(cm.out_shape):]
        return results, per_comm

    return run


def _ssm_discretise(lr, li, ldt, lr16, li16, ldt16, brt, bit):
    def lam_bar(lr_, li_, ldt_):
        dt = jnp.exp(ldt_)
        mag = jnp.exp(lr_ * dt)
        return mag * jnp.cos(li_ * dt), mag * jnp.sin(li_ * dt)

    lb_re, lb_im = lam_bar(lr, li, ldt)
    l16_re, l16_im = lam_bar(lr16, li16, ldt16)
    den = lr16 * lr16 + li16 * li16
    num_re = l16_re - 1.0
    fr = (num_re * lr16 + l16_im * li16) / den
    fi = (l16_im * lr16 - num_re * li16) / den
    bb_re = fr * brt - fi * bit
    bb_im = fr * bit + fi * brt
    return lb_re, lb_im, bb_re, bb_im


def _strip_selectors():
    p = lax.broadcasted_iota(jnp.int32, (N_STATE, STRIP), 0)
    col = lax.broadcasted_iota(jnp.int32, (N_STATE, STRIP), 1)
    rep = ((col & (N_STATE - 1)) == p).astype(BF16)
    row = lax.broadcasted_iota(jnp.int32, (SSM_W, STRIP), 0)
    col2 = lax.broadcasted_iota(jnp.int32, (SSM_W, STRIP), 1)
    mask = (((row >> 4) & 7) == (col2 >> 6))
    return rep, mask


def _ssm_prepare(lr, li, ldt, lr16, li16, ldt16, brt, bit, cre, cim):
    def body(lr_r, li_r, ldt_r, lr16_r, li16_r, ldt16_r, brt_r, bit_r, cre_r, cim_r,
             pwr_r, pwi_r, bbr_r, bbi_r, ctr_r, cti_r):
        lb_re, lb_im, bb_re, bb_im = _ssm_discretise(
            lr_r[...], li_r[...], ldt_r[...], lr16_r[...], li16_r[...], ldt16_r[...], brt_r[...], bit_r[...])
        pr, pi_ = lb_re, lb_im
        pwr_r[0] = pr
        pwi_r[0] = pi_
        for k in range(1, N_POWERS):
            pr, pi_ = pr * lb_re - pi_ * lb_im, pr * lb_im + pi_ * lb_re
            pwr_r[k] = pr
            pwi_r[k] = pi_
        rep, mask = _strip_selectors()
        for src, dst in ((bb_re, bbr_r), (bb_im, bbi_r), (cre_r[...], ctr_r), (cim_r[...], cti_r)):
            wide = jnp.dot(src.astype(BF16), rep, preferred_element_type=F32)
            dst[...] = jnp.where(mask, wide, 0.0).astype(BF16)

    vm = pl.BlockSpec(memory_space=pltpu.VMEM)
    return pl.pallas_call(
        body, name="ssm_prepare",
        out_shape=(jax.ShapeDtypeStruct((N_POWERS, N_GROUPS, N_STATE), F32),) * 2
        + (jax.ShapeDtypeStruct((SSM_W, STRIP), BF16),) * 4,
        in_specs=[vm] * 10, out_specs=(vm,) * 6,
    )(lr, li, ldt, lr16, li16, ldt16, brt, bit, cre, cim)


def _scan_tables(pwr, pwi):
    pr = pwr.reshape(N_POWERS, STATE_COLS)
    pi_ = pwi.reshape(N_POWERS, STATE_COLS)
    rows8 = lambda v: jnp.broadcast_to(v[None], (SUBLANES, STATE_COLS))
    tab_a = jnp.stack([rows8(pr[0]), rows8(pi_[0]), rows8(pr[-1]), rows8(pi_[-1])])
    return tab_a, jnp.stack([pr, pi_])


def _ssm_param_grads(lr, li, ldt, lr16, li16, ldt16, brt, bit, dlbr, dlbi, dbbr, dbbi, dctr, dcti):
    def body(lr_r, li_r, ldt_r, lr16_r, li16_r, ldt16_r, brt_r, bit_r,
             dlbr_r, dlbi_r, dbbr_r, dbbi_r, dctr_r, dcti_r,
             glr_r, gli_r, gldt_r, gbrt_r, gbit_r, gcre_r, gcim_r):
        rep, mask = _strip_selectors()

        def fold(acc):
            return sum(lax.dot_general(t, rep, (((1,), (1,)), ((), ())), preferred_element_type=F32)
                       for t in _split3(jnp.where(mask, acc, 0.0)))

        g_lb_re = jnp.sum(dlbr_r[...], axis=0)
        g_lb_im = jnp.sum(dlbi_r[...], axis=0)
        g_bb_re = fold(dbbr_r[...])
        g_bb_im = fold(dbbi_r[...])
        gcre_r[...] = fold(dctr_r[...])
        gcim_r[...] = fold(dcti_r[...])
        prim = (lr_r[...], li_r[...], ldt_r[...], lr16_r[...], li16_r[...], ldt16_r[...], brt_r[...], bit_r[...])
        _, vjp = jax.vjp(_ssm_discretise, *prim)
        g_lr, g_li, g_ldt, g_lr16, g_li16, g_ldt16, g_brt, g_bit = vjp((g_lb_re, g_lb_im, g_bb_re, g_bb_im))
        grp = lax.broadcasted_iota(jnp.int32, (N_GROUPS, SSM_W), 0)
        rw = lax.broadcasted_iota(jnp.int32, (N_GROUPS, SSM_W), 1)
        gsum = ((rw >> 4) == grp).astype(BF16)

        def group_sum(v):
            return sum(jnp.dot(gsum, t, preferred_element_type=F32) for t in _split3(v))

        glr_r[...] = g_lr + group_sum(g_lr16)
        gli_r[...] = g_li + group_sum(g_li16)
        gldt_r[...] = g_ldt + jnp.sum(group_sum(g_ldt16), axis=1, keepdims=True)
        gbrt_r[...] = g_brt
        gbit_r[...] = g_bit

    vm = pl.BlockSpec(memory_space=pltpu.VMEM)
    gp = jax.ShapeDtypeStruct((N_GROUPS, N_STATE), F32)
    gb = jax.ShapeDtypeStruct((SSM_W, N_STATE), F32)
    return pl.pallas_call(
        body, name="ssm_param_grads",
        out_shape=(gp, gp, jax.ShapeDtypeStruct((N_GROUPS, 1), F32), gb, gb, gb, gb),
        in_specs=[vm] * 14, out_specs=(vm,) * 7,
    )(lr, li, ldt, lr16, li16, ldt16, brt, bit, dlbr, dlbi, dbbr, dbbi, dctr, dcti)


def _in_proj(x, w_in_st, b_in, comms=()):
    t = x.shape[0]
    tm = 512

    def body(x_r, w_r, b_r, o_r):
        xb = x_r[...].astype(BF16)
        for j in range(N_SHARDS):
            cols = slice(D_MODEL * j, D_MODEL * (j + 1))
            o_r[:, cols] = jnp.dot(xb, w_r[j], preferred_element_type=F32) + b_r[:, cols]

    (proj,), sent = _launch(
        body, comms, name="in_proj", grid=(t // tm,),
        out_shape=(jax.ShapeDtypeStruct((t, IN_COLS), F32),),
        in_specs=[pl.BlockSpec((tm, D_MODEL), lambda i: (i, 0)), _const((N_SHARDS, D_MODEL, D_MODEL)),
                  _const((1, IN_COLS))],
        out_specs=(pl.BlockSpec((tm, IN_COLS), lambda i: (i, 0)),),
        sem=("parallel",),
    )(x, w_in_st, b_in)
    return proj, sent


def _cmul_add(xr, xi, mr, mi, sr, si):
    return xr + (mr * sr - mi * si), xi + (mr * si + mi * sr)


SCAN_STEPS = N_POWERS
SCAN_CHUNK = SUBLANES * SCAN_STEPS


def _interleave(src_r, dst_r):
    for step in range(SCAN_STEPS):
        dst_r[SUBLANES * step:SUBLANES * (step + 1), :] = src_r[pl.ds(step, SUBLANES, stride=SCAN_STEPS), :]


def _deinterleave(src_r, dst_r):
    for step in range(SCAN_STEPS):
        dst_r[pl.ds(step, SUBLANES, stride=SCAN_STEPS), :] = src_r[SUBLANES * step:SUBLANES * (step + 1), :]


def _step_rows(step):
    return pl.ds(pl.multiple_of(step * SUBLANES, SUBLANES), SUBLANES)


def _scan_steps(body, init, by=4, powers=None):
    if powers is not None:
        by = SUBLANES

    def trip(t, carry):
        if powers is not None:
            rows_re, rows_im = powers[0, _step_rows(t), :], powers[1, _step_rows(t), :]
        for u in range(by):
            if powers is None:
                carry = body(t * by + u, carry)
            else:
                carry = body(t * by + u, carry, jnp.broadcast_to(rows_re[u:u + 1, :], rows_re.shape),
                             jnp.broadcast_to(rows_im[u:u + 1, :], rows_im.shape))
        return carry

    return lax.fori_loop(0, SCAN_STEPS // by, trip, init)


def _segment_states(first_r, first_i, ends_r, ends_i, a64_r, a64_i, order):
    row = lax.broadcasted_iota(jnp.int32, ends_r.shape, 0)
    cur_r, cur_i = first_r, first_i
    ent_r = jnp.zeros_like(ends_r)
    ent_i = jnp.zeros_like(ends_i)
    for s in order:
        ent_r = jnp.where(row == s, jnp.broadcast_to(cur_r, ends_r.shape), ent_r)
        ent_i = jnp.where(row == s, jnp.broadcast_to(cur_i, ends_i.shape), ent_i)
        cur_r, cur_i = _cmul_add(ends_r[s:s + 1, :], ends_i[s:s + 1, :], a64_r, a64_i, cur_r, cur_i)
    return ent_r, ent_i, cur_r, cur_i


def _ssm_forward(proj, bbr, bbi, ctr, cti, d_skip, tab_a, tab_p, comms=(), tc=SCAN_CHUNK):
    t = proj.shape[0]

    def body(u_r, bbr_r, bbi_r, ctr_r, cti_r, d_r, ta_r, tp_r, xsr_r, xsi_r, y_r, ui_s, yi_s, car_r, car_i):
        @pl.when(pl.program_id(1) == 0)
        def _():
            car_r[...] = jnp.zeros_like(car_r)
            car_i[...] = jnp.zeros_like(car_i)

        _interleave(u_r, ui_s)
        u = ui_s[...]
        xsr_r[...] = _dot(u, bbr_r[...])
        xsi_r[...] = _dot(u, bbi_r[...])
        a_r, a_i = ta_r[0], ta_r[1]

        def local(step, carry):
            rows = _step_rows(step)
            xr, xi = _cmul_add(xsr_r[rows, :], xsi_r[rows, :], a_r, a_i, *carry)
            xsr_r[rows, :] = xr
            xsi_r[rows, :] = xi
            return xr, xi

        zero = jnp.zeros((SUBLANES, STRIP), F32)
        ends_r, ends_i = _scan_steps(local, (zero, zero))
        ent_r, ent_i, out_r, out_i = _segment_states(
            car_r[...], car_i[...], ends_r, ends_i, ta_r[2, 0:1, :], ta_r[3, 0:1, :], range(SUBLANES))
        car_r[...] = out_r
        car_i[...] = out_i

        def entering(step, _, power_r, power_i):
            rows = _step_rows(step)
            xr, xi = _cmul_add(xsr_r[rows, :], xsi_r[rows, :], power_r, power_i, ent_r, ent_i)
            xsr_r[rows, :] = xr
            xsi_r[rows, :] = xi
            return 0

        _scan_steps(entering, 0, powers=tp_r)
        yi_s[...] = _dot_t(xsr_r[...], ctr_r[...]) - _dot_t(xsi_r[...], cti_r[...]) + d_r[...] * u
        _deinterleave(yi_s, y_r)

    strip_mat = pl.BlockSpec((128, STRIP), lambda j, k: (j, 0))
    states = pl.BlockSpec((tc, STRIP), lambda j, k: (k, j))
    return _launch(
        body, comms, name="ssm_forward", grid=(N_STRIPS, t // tc),
        out_shape=(jax.ShapeDtypeStruct((t, STATE_COLS), F32), jax.ShapeDtypeStruct((t, STATE_COLS), F32),
                   jax.ShapeDtypeStruct((t, SSM_W), F32)),
        in_specs=[pl.BlockSpec((tc, 128), lambda j, k: (k, j)),
                  strip_mat, strip_mat, strip_mat, strip_mat,
                  pl.BlockSpec((1, 128), lambda j, k: (0, j)),
                  pl.BlockSpec((4, SUBLANES, STRIP), lambda j, k: (0, 0, j)),
                  pl.BlockSpec((2, SCAN_STEPS, STRIP), lambda j, k: (0, 0, j))],
        out_specs=(states, states, pl.BlockSpec((tc, 128), lambda j, k: (k, j))),
        scratch_shapes=[pltpu.VMEM((tc, 128), F32), pltpu.VMEM((tc, 128), F32),
                        pltpu.VMEM((1, STRIP), F32), pltpu.VMEM((1, STRIP), F32)],
        sem=("parallel", "arbitrary"),
    )(proj, bbr, bbi, ctr, cti, d_skip, tab_a, tab_p)


def _shift_down(v, prev, n):
    row = lax.broadcasted_iota(jnp.int32, v.shape, 0)
    out = pltpu.roll(v, n, 0)
    for r in range(n):
        src = prev[SUBLANES - n + r:SUBLANES - n + r + 1, :]
        out = jnp.where(row == r, jnp.broadcast_to(src, v.shape), out)
    return out


def _shift_up(v, nxt, n):
    rows = v.shape[0]
    row = lax.broadcasted_iota(jnp.int32, v.shape, 0)
    out = pltpu.roll(v, rows - n, 0)
    for r in range(n):
        src = nxt[r:r + 1, :]
        out = jnp.where(row == rows - n + r, jnp.broadcast_to(src, v.shape), out)
    return out


def _conv3(q, q_prev, w):
    return w[2:3, :] * q + w[1:2, :] * _shift_down(q, q_prev, 1) + w[0:1, :] * _shift_down(q, q_prev, 2)


def _mixer_forward(x, proj, ya0, glu_w, glu_b, wso_st, conv_w8, wco_st, w_o, comms=(), tm=256):
    t = x.shape[0]
    hb = tm // SUBLANES

    def body(x_r, ya0_r, h_r, cg_r, bg_r, ga_r, gb_r, hp_r, cgp_r,
             glu_w_r, glu_b_r, wso_r, cw_r, wco_r, wo_r, xh_r, rstd_r, ya_r, yb_r):
        i = pl.program_id(0)
        g, _ = _gelu_parts(ya0_r[...])
        ya1 = g * _sigmoid(_dot(g, glu_w_r[...]) + glu_b_r[...])
        q = cg_r[...] * h_r[...]
        q_prev = jnp.where(i > 0, cgp_r[...] * hp_r[...], 0.0)
        yb0 = bg_r[...] * _conv3(q, q_prev, cw_r[...])
        for j in range(N_SHARDS):
            ya_r[:, 256 * j:256 * (j + 1)] = _dot(ya1, wso_r[j])
            yb_r[:, 256 * j:256 * (j + 1)] = _dot(yb0, wco_r[j])
        merged = _sigmoid(ga_r[...]) * ya_r[...] + _sigmoid(gb_r[...]) * yb_r[...]
        r1 = ALPHA * x_r[...] + _dot(merged, wo_r[...])
        mu = jnp.mean(r1, axis=-1, keepdims=True)
        cen = r1 - mu
        rstd = lax.rsqrt(jnp.mean(cen * cen, axis=-1, keepdims=True) + LN_EPS)
        xh_r[...] = cen * rstd
        rstd_r[...] = rstd

    def col(w, c):
        return pl.BlockSpec((tm, w), lambda i: (i, c))

    def prev(c):
        return pl.BlockSpec((SUBLANES, SSM_W), lambda i: (jnp.maximum(i * hb - 1, 0), c))

    return _launch(
        body, comms, name="mixer_forward", grid=(t // tm,),
        out_shape=(jax.ShapeDtypeStruct((t, D_MODEL), F32), jax.ShapeDtypeStruct((t, 1), F32),
                   jax.ShapeDtypeStruct((t, D_MODEL), F32), jax.ShapeDtypeStruct((t, D_MODEL), F32)),
        in_specs=[col(D_MODEL, 0), col(SSM_W, 0), col(SSM_W, 1), col(SSM_W, 2), col(SSM_W, 3),
                  col(D_MODEL, 2), col(D_MODEL, 3), prev(1), prev(2),
                  _const((SSM_W, SSM_W)), _const((1, SSM_W)), _const((N_SHARDS, SSM_W, 256)),
                  _const((SUBLANES, SSM_W)), _const((N_SHARDS, SSM_W, 256)), _const((D_MODEL, D_MODEL))],
        out_specs=(col(D_MODEL, 0), pl.BlockSpec((tm, 1), lambda i: (i, 0)), col(D_MODEL, 0), col(D_MODEL, 0)),
        sem=("parallel",),
    )(x, ya0, proj, proj, proj, proj, proj, proj, proj, glu_w, glu_b, wso_st, conv_w8, wco_st, w_o)


def _layer_norm_bwd(dxhat, xhat, rstd):
    m1 = jnp.mean(dxhat, axis=-1, keepdims=True)
    m2 = jnp.mean(dxhat * xhat, axis=-1, keepdims=True)
    return rstd * (dxhat - m1 - xhat * m2)


def _ffn_step(xhat1, rstd1, target, ln1_g, ln1_b, ln2_g, ln2_b, wg_st, wu_st, wd_st, tm=256):
    t = xhat1.shape[0]

    def body(xh_r, rstd_r, tgt_r, g1_r, b1_r, g2_r, b2_r, wg_r, wu_r, wd_r,
             loss_r, dr1_r, x1b_r, dr2b_r, hid_r, dhg_r, dhu_r, dg2_r, db2_r, dg1_r, db1_r,
             hg_s, hu_s):
        @pl.when(pl.program_id(0) == 0)
        def _():
            for r in (loss_r, dg2_r, db2_r, dg1_r, db1_r):
                r[...] = jnp.zeros_like(r)

        xhat1_v = xh_r[...]
        x1 = xhat1_v * g1_r[...] + b1_r[...]
        x1b = x1.astype(BF16)
        x1b_r[...] = x1b
        ffn = jnp.zeros((tm, D_MODEL), F32)
        for j in range(N_SHARDS):
            hg = lax.dot_general(x1b, wg_r[j], (((1,), (1,)), ((), ())), preferred_element_type=F32)
            hu = lax.dot_general(x1b, wu_r[j], (((1,), (1,)), ((), ())), preferred_element_type=F32)
            hg_s[j] = hg
            hu_s[j] = hu
            hid = (hg * _sigmoid(hg) * hu).astype(BF16)
            hid_r[j] = hid
            ffn = ffn + jnp.dot(hid, wd_r[j], preferred_element_type=F32)
        r2 = ALPHA * x1 + ffn
        mu = jnp.mean(r2, axis=-1, keepdims=True)
        cen = r2 - mu
        rstd2 = lax.rsqrt(jnp.mean(cen * cen, axis=-1, keepdims=True) + LN_EPS)
        xhat2 = cen * rstd2
        diff = (xhat2 * g2_r[...] + b2_r[...]) - tgt_r[...]
        loss_r[...] += 0.5 * jnp.sum(jnp.mean(diff * diff, axis=-1, keepdims=True), axis=0, keepdims=True)
        dy = diff * (1.0 / D_MODEL)
        dg2_r[...] += jnp.sum(dy * xhat2, axis=0, keepdims=True)
        db2_r[...] += jnp.sum(dy, axis=0, keepdims=True)
        dr2 = _layer_norm_bwd(dy * g2_r[...], xhat2, rstd2)
        dr2b = dr2.astype(BF16)
        dr2b_r[...] = dr2b
        dx1 = ALPHA * dr2
        for j in range(N_SHARDS):
            dhid = lax.dot_general(dr2b, wd_r[j], (((1,), (1,)), ((), ())), preferred_element_type=F32)
            hg = hg_s[j]
            hu = hu_s[j]
            sg = _sigmoid(hg)
            dhu = (dhid * (hg * sg)).astype(BF16)
            dhg = (dhid * hu * (sg * (1.0 + hg * (1.0 - sg)))).astype(BF16)
            dhg_r[j] = dhg
            dhu_r[j] = dhu
            dx1 = dx1 + jnp.dot(dhg, wg_r[j], preferred_element_type=F32)
            dx1 = dx1 + jnp.dot(dhu, wu_r[j], preferred_element_type=F32)
        dg1_r[...] += jnp.sum(dx1 * xhat1_v, axis=0, keepdims=True)
        db1_r[...] += jnp.sum(dx1, axis=0, keepdims=True)
        dr1_r[...] = _layer_norm_bwd(dx1 * g1_r[...], xhat1_v, rstd_r[...])

    tile = pl.BlockSpec((tm, D_MODEL), lambda i: (i, 0))
    hidden = pl.BlockSpec((N_SHARDS, tm, FFN_SHARD), lambda i: (0, i, 0))
    vec = _const((1, D_MODEL))
    hid_shape = jax.ShapeDtypeStruct((N_SHARDS, t, FFN_SHARD), BF16)
    vec_shape = jax.ShapeDtypeStruct((1, D_MODEL), F32)
    return pl.pallas_call(
        body, name="ffn_step", grid=(t // tm,),
        out_shape=(jax.ShapeDtypeStruct((1, 1), F32), jax.ShapeDtypeStruct((t, D_MODEL), F32),
                   jax.ShapeDtypeStruct((t, D_MODEL), BF16), jax.ShapeDtypeStruct((t, D_MODEL), BF16),
                   hid_shape, hid_shape, hid_shape, vec_shape, vec_shape, vec_shape, vec_shape),
        in_specs=[tile, pl.BlockSpec((tm, 1), lambda i: (i, 0)), tile, vec, vec, vec, vec,
                  _const((N_SHARDS, FFN_SHARD, D_MODEL)), _const((N_SHARDS, FFN_SHARD, D_MODEL)),
                  _const((N_SHARDS, FFN_SHARD, D_MODEL))],
        out_specs=(_const((1, 1)), tile, tile, tile, hidden, hidden, hidden, vec, vec, vec, vec),
        scratch_shapes=[pltpu.VMEM((N_SHARDS, tm, FFN_SHARD), F32), pltpu.VMEM((N_SHARDS, tm, FFN_SHARD), F32)],
        compiler_params=_params(("arbitrary",)),
    )(xhat1, rstd1, target, ln1_g, ln1_b, ln2_g, ln2_b, wg_st, wu_st, wd_st)


def _ffn_weight_grads(x1b, dr2b, hid, dhg, dhu, tk=2048):
    t = x1b.shape[0]

    def body(x_r, dr_r, hid_r, dhg_r, dhu_r, gwg_r, gwu_r, gwd_r):
        @pl.when(pl.program_id(1) == 0)
        def _():
            for r in (gwg_r, gwu_r, gwd_r):
                r[...] = jnp.zeros_like(r)

        gwg_r[0] += _tdot(dhg_r[0], x_r[...])
        gwu_r[0] += _tdot(dhu_r[0], x_r[...])
        gwd_r[0] += _tdot(hid_r[0], dr_r[...])

    tile = pl.BlockSpec((tk, D_MODEL), lambda j, k: (k, 0))
    hidden = pl.BlockSpec((1, tk, FFN_SHARD), lambda j, k: (j, k, 0))
    row = pl.BlockSpec((1, FFN_SHARD, D_MODEL), lambda j, k: (j, 0, 0))
    return pl.pallas_call(
        body, name="ffn_weight_grads", grid=(N_SHARDS, t // tk),
        out_shape=(jax.ShapeDtypeStruct((N_SHARDS, FFN_SHARD, D_MODEL), F32),) * 3,
        in_specs=[tile, tile, hidden, hidden, hidden],
        out_specs=(row, row, row),
        compiler_params=_params(("parallel", "arbitrary")),
    )(x1b, dr2b, hid, dhg, dhu)


def _mixer_backward(dr1, proj, ya0, ya, yb, glu_w, glu_b, wso_st, conv_w8, wco_st, w_o, comms=(), tm=256):
    t = dr1.shape[0]
    hb = tm // SUBLANES
    last_block = t // SUBLANES - 1

    def body(dr1_r, dr1n_r, ya0_r, ya_r, yb_r, h_r, cg_r, bg_r, ga_r, gb_r, hp_r, cgp_r, bgn_r, gbn_r,
             glu_w_r, glu_b_r, wso_r, cw_r, wco_r, wo_r,
             dya0_r, dproj_r, dbias_r, gwo_r, gwso_r, gwco_r, gglu_w_r, gglu_b_r, gconv_r):
        i = pl.program_id(0)

        @pl.when(i == 0)
        def _():
            for r in (dbias_r, gwo_r, gwso_r, gwco_r, gglu_w_r, gglu_b_r, gconv_r):
                r[...] = jnp.zeros_like(r)

        dr1_v = dr1_r[...]
        dmerged = _dot_t(dr1_v, wo_r[...])
        sa = _sigmoid(ga_r[...])
        sb = _sigmoid(gb_r[...])
        ya_v = ya_r[...]
        yb_v = yb_r[...]
        gwo_r[...] += _tdot(sa * ya_v + sb * yb_v, dr1_v)
        dya = dmerged * sa
        dyb = dmerged * sb
        dga = dmerged * ya_v * (sa * (1.0 - sa))
        dgb = dmerged * yb_v * (sb * (1.0 - sb))

        g, gelu_grad = _gelu_parts(ya0_r[...])
        s1 = _sigmoid(_dot(g, glu_w_r[...]) + glu_b_r[...])
        ya1 = g * s1
        dya1 = jnp.zeros((tm, SSM_W), F32)
        for j in range(N_SHARDS):
            dya_j = dya[:, 256 * j:256 * (j + 1)]
            gwso_r[j] += _tdot(ya1, dya_j)
            dya1 = dya1 + _dot_t(dya_j, wso_r[j])
        dz1 = dya1 * g * (s1 * (1.0 - s1))
        gglu_b_r[...] += jnp.sum(dz1, axis=0, keepdims=True)
        gglu_w_r[...] += _tdot(g, dz1)
        dya0_r[...] = (dya1 * s1 + _dot_t(dz1, glu_w_r[...])) * gelu_grad

        cw = cw_r[...]
        h = h_r[...]
        cg = cg_r[...]
        bg = bg_r[...]
        q = cg * h
        q_prev = jnp.where(i > 0, cgp_r[...] * hp_r[...], 0.0)
        q1 = _shift_down(q, q_prev, 1)
        q2 = _shift_down(q, q_prev, 2)
        z = cw[2:3, :] * q + cw[1:2, :] * q1 + cw[0:1, :] * q2
        yb0 = bg * z
        dyb0 = jnp.zeros((tm, SSM_W), F32)
        for j in range(N_SHARDS):
            dyb_j = dyb[:, 256 * j:256 * (j + 1)]
            gwco_r[j] += _tdot(yb0, dyb_j)
            dyb0 = dyb0 + _dot_t(dyb_j, wco_r[j])
        dbg = dyb0 * z
        dz = dyb0 * bg
        dyb_n = _dot_t(dr1n_r[...], wo_r[...]) * _sigmoid(gbn_r[...])
        dyb0_n = jnp.zeros((SUBLANES, SSM_W), F32)
        for j in range(N_SHARDS):
            dyb0_n = dyb0_n + _dot_t(dyb_n[:, 256 * j:256 * (j + 1)], wco_r[j])
        dz_next = jnp.where(i < pl.num_programs(0) - 1, dyb0_n * bgn_r[...], 0.0)
        dq = cw[2:3, :] * dz + cw[1:2, :] * _shift_up(dz, dz_next, 1) + cw[0:1, :] * _shift_up(dz, dz_next, 2)
        gconv_r[0:1, :] += jnp.sum(dz * q2, axis=0, keepdims=True)
        gconv_r[1:2, :] += jnp.sum(dz * q1, axis=0, keepdims=True)
        gconv_r[2:3, :] += jnp.sum(dz * q, axis=0, keepdims=True)
        dh = dq * cg
        dcg = dq * h

        dproj_r[:, 0:512] = jnp.zeros((tm, SSM_W), BF16)
        pieces = ((512, dh), (1024, dcg), (1536, dbg), (2048, dga), (3072, dgb))
        for off, val in pieces:
            w = val.shape[1]
            dproj_r[:, off:off + w] = val.astype(BF16)
            dbias_r[:, off:off + w] += jnp.sum(val, axis=0, keepdims=True)

    def col(w, c):
        return pl.BlockSpec((tm, w), lambda i: (i, c))

    def prev(c):
        return pl.BlockSpec((SUBLANES, SSM_W), lambda i: (jnp.maximum(i * hb - 1, 0), c))

    def nxt(w, c):
        return pl.BlockSpec((SUBLANES, w), lambda i: (jnp.minimum((i + 1) * hb, last_block), c))

    sh = jax.ShapeDtypeStruct
    return _launch(
        body, comms, name="mixer_backward", grid=(t // tm,),
        out_shape=(sh((t, SSM_W), F32), sh((t, IN_COLS), BF16), sh((1, IN_COLS), F32),
                   sh((D_MODEL, D_MODEL), F32), sh((N_SHARDS, SSM_W, 256), F32), sh((N_SHARDS, SSM_W, 256), F32),
                   sh((SSM_W, SSM_W), F32), sh((1, SSM_W), F32), sh((SUBLANES, SSM_W), F32)),
        in_specs=[col(D_MODEL, 0), nxt(D_MODEL, 0), col(SSM_W, 0), col(D_MODEL, 0), col(D_MODEL, 0),
                  col(SSM_W, 1), col(SSM_W, 2), col(SSM_W, 3), col(D_MODEL, 2), col(D_MODEL, 3),
                  prev(1), prev(2), nxt(SSM_W, 3), nxt(D_MODEL, 3),
                  _const((SSM_W, SSM_W)), _const((1, SSM_W)), _const((N_SHARDS, SSM_W, 256)),
                  _const((SUBLANES, SSM_W)), _const((N_SHARDS, SSM_W, 256)), _const((D_MODEL, D_MODEL))],
        out_specs=(col(SSM_W, 0), col(IN_COLS, 0), _const((1, IN_COLS)),
                   _const((D_MODEL, D_MODEL)), _const((N_SHARDS, SSM_W, 256)), _const((N_SHARDS, SSM_W, 256)),
                   _const((SSM_W, SSM_W)), _const((1, SSM_W)), _const((SUBLANES, SSM_W))),
        sem=("arbitrary",),
    )(dr1, dr1, ya0, ya, yb, proj, proj, proj, proj, proj, proj, proj, proj, proj,
      glu_w, glu_b, wso_st, conv_w8, wco_st, w_o)


def _cmulc_add(xr, xi, mr, mi, sr, si):
    return xr + (mr * sr + mi * si), xi + (mr * si - mi * sr)


def _ssm_backward(dya0, proj, xsr, xsi, bbr, bbi, ctr, cti, d_skip, tab_a, tab_p, dproj, comms=(), tc=SCAN_CHUNK):
    t = proj.shape[0]
    nk = t // tc

    def body(dy_r, u_r, xsr_r, xsi_r, bbr_r, bbi_r, ctr_r, cti_r, d_r, ta_r, tp_r, dproj_any,
             du_r, dus_r, gbbr_r, gbbi_r, gctr_r, gcti_r, glbr_r, glbi_r, gd_r,
             gr_s, gi_s, dyi_s, ui_s, dui_s, dun_s, car_r, car_i):
        del dproj_any

        @pl.when(pl.program_id(1) == 0)
        def _():
            for r in (car_r, car_i, dus_r, gbbr_r, gbbi_r, gctr_r, gcti_r, glbr_r, glbi_r, gd_r):
                r[...] = jnp.zeros_like(r)

        _interleave(dy_r, dyi_s)
        _interleave(u_r, ui_s)
        dy = dyi_s[...]
        u = ui_s[...]
        gr_s[...] = _dot(dy, ctr_r[...])
        gi_s[...] = -_dot(dy, cti_r[...])
        a_r, a_i = ta_r[0], ta_r[1]

        def local(n, carry):
            rows = _step_rows(SCAN_STEPS - 1 - n)
            gr, gi = _cmulc_add(gr_s[rows, :], gi_s[rows, :], a_r, a_i, *carry)
            gr_s[rows, :] = gr
            gi_s[rows, :] = gi
            return gr, gi

        zero = jnp.zeros((SUBLANES, STRIP), F32)
        ends_r, ends_i = _scan_steps(local, (zero, zero))
        ent_r, ent_i, out_r, out_i = _segment_states(
            car_r[...], car_i[...], ends_r, ends_i, ta_r[2, 0:1, :], -ta_r[3, 0:1, :], range(SUBLANES - 1, -1, -1))
        car_r[...] = out_r
        car_i[...] = out_i

        def entering(n, carry, power_r, power_i):
            gnr, gni, ar, ai = carry
            rows = _step_rows(SCAN_STEPS - 1 - n)
            gr, gi = _cmulc_add(gr_s[rows, :], gi_s[rows, :], power_r, power_i, ent_r, ent_i)
            gr_s[rows, :] = gr
            gi_s[rows, :] = gi
            xr = xsr_r[rows, :]
            xi = xsi_r[rows, :]
            return gr, gi, ar + (xr * gnr + xi * gni), ai + (xr * gni - xi * gnr)

        _, _, ar, ai = _scan_steps(entering, (ent_r, ent_i, zero, zero), powers=tp_r)
        glbr_r[...] += ar
        glbi_r[...] += ai
        gr = gr_s[...]
        gi = gi_s[...]
        dui_s[...] = _dot_t(gr, bbr_r[...]) + _dot_t(gi, bbi_r[...]) + d_r[...] * dy
        _deinterleave(dui_s, dun_s)
        du = dun_s[...]
        du_r[...] = du.astype(BF16)
        dus_r[...] += jnp.sum(du, axis=0, keepdims=True)
        gd_r[...] += jnp.sum(dy * u, axis=0, keepdims=True)
        gbbr_r[...] += _tdot(u, gr)
        gbbi_r[...] += _tdot(u, gi)
        gctr_r[...] += _tdot(dy, xsr_r[...])
        gcti_r[...] -= _tdot(dy, xsi_r[...])

    def rev(w):
        return pl.BlockSpec((tc, w), lambda j, k: (nk - 1 - k, j))

    strip_mat = pl.BlockSpec((128, STRIP), lambda j, k: (j, 0))
    vec = pl.BlockSpec((1, 128), lambda j, k: (0, j))
    lbacc = pl.BlockSpec((SUBLANES, STRIP), lambda j, k: (0, j))
    sh = jax.ShapeDtypeStruct
    return _launch(
        body, comms, name="ssm_backward", grid=(N_STRIPS, nk),
        out_shape=(sh((t, IN_COLS), BF16), sh((1, SSM_W), F32),
                   sh((SSM_W, STRIP), F32), sh((SSM_W, STRIP), F32), sh((SSM_W, STRIP), F32), sh((SSM_W, STRIP), F32),
                   sh((SUBLANES, STATE_COLS), F32), sh((SUBLANES, STATE_COLS), F32), sh((1, SSM_W), F32)),
        in_specs=[rev(128), rev(128), rev(STRIP), rev(STRIP),
                  strip_mat, strip_mat, strip_mat, strip_mat, vec,
                  pl.BlockSpec((4, SUBLANES, STRIP), lambda j, k: (0, 0, j)),
                  pl.BlockSpec((2, SCAN_STEPS, STRIP), lambda j, k: (0, 0, j)), ANY],
        out_specs=(rev(128), vec, strip_mat, strip_mat, strip_mat, strip_mat, lbacc, lbacc, vec),
        scratch_shapes=[pltpu.VMEM((tc, STRIP), F32), pltpu.VMEM((tc, STRIP), F32)]
        + [pltpu.VMEM((tc, 128), F32)] * 4 + [pltpu.VMEM((1, STRIP), F32)] * 2,
        aliases={11: 0}, sem=("parallel", "arbitrary"),
    )(dya0, proj, xsr, xsi, bbr, bbi, ctr, cti, d_skip, tab_a, tab_p, dproj)


def _input_grad(dr1, dproj, w_in_st, comms=(), tm=512):
    t = dr1.shape[0]

    def body(dr1_r, dp_r, w_r, dx_r):
        acc = ALPHA * dr1_r[...]
        for j in range(N_SHARDS):
            acc = acc + lax.dot_general(dp_r[:, D_MODEL * j:D_MODEL * (j + 1)], w_r[j],
                                        (((1,), (1,)), ((), ())), preferred_element_type=F32)
        dx_r[...] = acc

    (dx,), sent = _launch(
        body, comms, name="input_grad", grid=(t // tm,),
        out_shape=(jax.ShapeDtypeStruct((t, D_MODEL), F32),),
        in_specs=[pl.BlockSpec((tm, D_MODEL), lambda i: (i, 0)), pl.BlockSpec((tm, IN_COLS), lambda i: (i, 0)),
                  _const((N_SHARDS, D_MODEL, D_MODEL))],
        out_specs=(pl.BlockSpec((tm, D_MODEL), lambda i: (i, 0)),),
        sem=("parallel",),
    )(dr1, dproj, w_in_st)
    return dx, sent


def _in_weight_grad(x, dproj, comms=(), tk=2048):
    t = x.shape[0]

    def body(x_r, dp_r, gw_r):
        @pl.when(pl.program_id(1) == 0)
        def _():
            gw_r[...] = jnp.zeros_like(gw_r)

        gw_r[0] += _tdot(x_r[...], dp_r[...])

    (g_w_in,), sent = _launch(
        body, comms, name="in_weight_grad", grid=(N_SHARDS, t // tk),
        out_shape=(jax.ShapeDtypeStruct((N_SHARDS, D_MODEL, D_MODEL), F32),),
        in_specs=[pl.BlockSpec((tk, D_MODEL), lambda j, k: (k, 0)), pl.BlockSpec((tk, D_MODEL), lambda j, k: (k, j))],
        out_specs=(pl.BlockSpec((1, D_MODEL, D_MODEL), lambda j, k: (j, 0, 0)),),
        sem=("parallel", "arbitrary"),
    )(x, dproj)
    return g_w_in, sent


MIXER_W = ("glu_w", "w_ssm_out", "w_conv_out", "w_o")
FFN_W = ("w_gate", "w_up", "w_down")


def _device_step(x, target, small, shards, c_arr, me_arr):
    lr, li = small["ssm_lambda_re"][0], small["ssm_lambda_im"][0]
    ldt = small["ssm_log_dt"][0][:, None]
    rep16 = lambda a: jnp.broadcast_to(a[:, None, :], (N_GROUPS, GROUP_C, a.shape[-1])).reshape(SSM_W, a.shape[-1])
    lr16, li16 = rep16(lr), rep16(li)
    ldt16 = rep16(jnp.broadcast_to(ldt, (N_GROUPS, N_STATE)))
    brt = small["ssm_b_re"][0].transpose(0, 2, 1).reshape(SSM_W, N_STATE)
    bit = small["ssm_b_im"][0].transpose(0, 2, 1).reshape(SSM_W, N_STATE)
    cre = small["ssm_c_re"][0].reshape(SSM_W, N_STATE)
    cim = small["ssm_c_im"][0].reshape(SSM_W, N_STATE)
    disc = (lr, li, ldt, lr16, li16, ldt16, brt, bit)

    pwr, pwi, bbr, bbi, ctr, cti = _ssm_prepare(*disc, cre, cim)
    tab_a, tab_p = _scan_tables(pwr, pwi)

    first_sh = [shards[n] for n in MIXER_W + FFN_W[:1]]
    second_sh = [shards[n] for n in FFN_W[1:]]
    (w_in_st,) = _gather_weights([shards["w_in"]])
    proj, (arrived,) = _in_proj(x, w_in_st, small["b_in"], comms=[_gather_ici(first_sh, [shards["conv_w"]])])
    (xsr, xsi, ya0), (second_part, first_st) = _ssm_forward(
        proj, bbr, bbi, ctr, cti, small["ssm_d"], tab_a, tab_p,
        comms=[_gather_ici(second_sh), _gather_d2d(arrived[:len(first_sh)], first_sh)])
    glu_st, wso_st, wco_st, wo_st, wg_st = first_st
    conv_st = arrived[len(first_sh)]
    conv_w8 = jnp.pad(conv_st[:, :3, :].transpose(1, 0, 2).reshape(3, SSM_W), ((0, SUBLANES - 3), (0, 0)))
    w_o = wo_st.reshape(D_MODEL, D_MODEL)
    glu_w = glu_st.reshape(SSM_W, SSM_W)
    (xhat1, rstd1, ya, yb), (second_st,) = _mixer_forward(
        x, proj, ya0, glu_w, small["glu_b"], wso_st, conv_w8, wco_st, w_o, comms=[_gather_d2d(second_part, second_sh)])
    wu_st, wd_st = second_st
    (loss, dr1, x1b, dr2b, hid, dhg, dhu, g_ln2_g, g_ln2_b, g_ln1_g, g_ln1_b) = _ffn_step(
        xhat1, rstd1, target, small["ln1_g"], small["ln1_b"], small["ln2_g"], small["ln2_b"], wg_st, wu_st, wd_st)

    add_halves = lambda gs, rs: _per_shape(lambda a, b: _add_own_half(a, b, c_arr), list(gs), list(rs))
    sum_chips = lambda owns, slots: _per_shape(lambda a, b: _sum_chips(a, b, me_arr), list(owns), list(slots))
    g_ffn = _ffn_weight_grads(x1b, dr2b, hid, dhg, dhu)
    (dya0, dproj, dbias, g_wo, g_wso, g_wco, g_glu_w, g_glu_b, g_conv8), (got_ffn,) = _mixer_backward(
        dr1, proj, ya0, ya, yb, glu_w, small["glu_b"], wso_st, conv_w8, wco_st, w_o, comms=[_swap_comm(g_ffn)])
    chip_ffn = add_halves(g_ffn, got_ffn)
    g_mix = [g_glu_w.reshape(N_SHARDS, 128, SSM_W), g_wso, g_wco, g_wo.reshape(N_SHARDS, 256, D_MODEL)]
    (dproj, dus, gbbr, gbbi, gctr, gcti, glbr, glbi, g_d), (slots_ffn, got_mix) = _ssm_backward(
        dya0, proj, xsr, xsi, bbr, bbi, ctr, cti, small["ssm_d"], tab_a, tab_p, dproj,
        comms=[_scatter_comm(chip_ffn), _swap_comm(g_mix)])
    halves_ffn = sum_chips(chip_ffn, slots_ffn)
    chip_mix = add_halves(g_mix, got_mix)
    g_lr, g_li, g_ldt, g_brt, g_bit, g_cre, g_cim = _ssm_param_grads(
        *disc, glbr.reshape(SUBLANES, N_GROUPS, N_STATE), glbi.reshape(SUBLANES, N_GROUPS, N_STATE),
        gbbr, gbbi, gctr, gcti)
    g_w_in, (others_ffn, slots_mix) = _in_weight_grad(
        x, dproj, comms=[_send_comm(halves_ffn), _scatter_comm(chip_mix)])
    halves_mix = sum_chips(chip_mix, slots_mix)
    dx, _ = _input_grad(dr1, dproj, w_in_st)

    g_conv = jnp.pad(g_conv8[:3].reshape(3, N_SHARDS, 128).transpose(1, 0, 2), ((0, 0), (0, SUBLANES - 3), (0, 0)))
    pieces = [dus, dbias[:, SSM_W:], g_lr, g_li, g_ldt, g_brt, g_bit, g_cre, g_cim, g_d, g_glu_b,
              g_ln1_g, g_ln1_b, g_ln2_g, g_ln2_b, loss]
    flat = jnp.concatenate([p.reshape(-1) for p in pieces])
    g_packed = jnp.pad(flat, (0, PACKED_ROWS * 128 - flat.shape[0])).reshape(PACKED_ROWS, 128)
    ((got_w, got_conv, got_packed),) = _standalone([_swap_comm([g_w_in], [g_conv, g_packed])], "swap_with_sibling")
    (chip_w,) = add_halves([g_w_in], [got_w])
    chip_conv, chip_packed = _small_pair_sums([g_conv, g_packed], [got_conv, got_packed])
    ((slots_w, slots_conv, slots_packed),) = _standalone(
        [_scatter_comm([chip_w, chip_conv], [chip_packed])], "scatter_to_chips")
    (halves_w,) = sum_chips([chip_w], [slots_w])
    conv_total, packed_mine = _small_totals(me_arr, c_arr, chip_conv, slots_conv, chip_packed, slots_packed)
    (others_rest,) = _standalone([_send_comm([halves_w] + halves_mix + [packed_mine])], "send_to_sibling")
    packed_other = others_rest[-1]
    south = c_arr[0] == 0
    packed_total = jnp.concatenate([jnp.where(south, packed_mine, packed_other),
                                    jnp.where(south, packed_other, packed_mine)])

    pairs = dict(zip(FFN_W, zip(halves_ffn, others_ffn)))
    pairs.update(zip(("w_in",) + MIXER_W, zip([halves_w] + halves_mix, others_rest[:-1])))
    return dx, pairs, conv_total, packed_total


PACKED_ROWS = 1136
PACKED_LAYOUT = (("b_in", IN_COLS), ("ssm_lambda_re", STATE_COLS), ("ssm_lambda_im", STATE_COLS),
                 ("ssm_log_dt", N_GROUPS), ("ssm_b_re", SSM_W * N_STATE), ("ssm_b_im", SSM_W * N_STATE),
                 ("ssm_c_re", SSM_W * N_STATE), ("ssm_c_im", SSM_W * N_STATE), ("ssm_d", SSM_W), ("glu_b", SSM_W),
                 ("ln1_g", D_MODEL), ("ln1_b", D_MODEL), ("ln2_g", D_MODEL), ("ln2_b", D_MODEL), ("loss", 1))


def _unpack_small(packed):
    flat = packed.reshape(-1)
    out, off = {}, 0
    for name, size in PACKED_LAYOUT:
        out[name] = flat[off:off + size]
        off += size
    for name in ("ssm_b_re", "ssm_b_im"):
        out[name] = out[name].reshape(N_GROUPS, GROUP_C, N_STATE).transpose(0, 2, 1)[None]
    for name in ("ssm_c_re", "ssm_c_im"):
        out[name] = out[name].reshape(1, N_GROUPS, GROUP_C, N_STATE)
    for name in ("ssm_lambda_re", "ssm_lambda_im"):
        out[name] = out[name].reshape(1, N_GROUPS, N_STATE)
    for name in ("b_in", "ssm_log_dt", "ssm_d", "glu_b", "ln1_g", "ln1_b", "ln2_g", "ln2_b"):
        out[name] = out[name][None]
    return out


BIG = ("w_in", "glu_w", "w_ssm_out", "w_conv_out", "w_o", "w_gate", "w_up", "w_down")
SMALL = ("b_in", "ssm_lambda_re", "ssm_lambda_im", "ssm_log_dt", "ssm_b_re", "ssm_b_im", "ssm_c_re", "ssm_c_im",
         "ssm_d", "glu_b", "ln1_g", "ln1_b", "ln2_g", "ln2_b")
WEIGHTS = ("w_in", "b_in", "ssm_lambda_re", "ssm_lambda_im", "ssm_log_dt", "ssm_b_re", "ssm_b_im", "ssm_c_re",
           "ssm_c_im", "ssm_d", "glu_w", "glu_b", "w_ssm_out", "conv_w", "w_conv_out", "w_o", "ln1_g", "ln1_b",
           "w_gate", "w_up", "w_down", "ln2_g", "ln2_b")


def _place():
    x, y, c = lax.axis_index("x"), lax.axis_index("y"), lax.axis_index("c")
    chips = [(1 - x, y), (x, 1 - y), (1 - x, 1 - y)]
    return x, y, c, chips


def _shard_of(chip):
    return 2 * chip[0] + chip[1]


def _remote(src, dst, send_sem, recv_sem, to):
    return pltpu.make_async_remote_copy(src_ref=src, dst_ref=dst, send_sem=send_sem, recv_sem=recv_sem,
                                        device_id=to, device_id_type=MESH)


def _half_rows(shard, which):
    r2 = shard.shape[0] // 2
    return pl.ds(pl.multiple_of(which * r2, 16), r2)


def _gather_ici(halved, whole=()):
    shards = list(halved) + list(whole)
    nh = len(halved)

    def copies(src, dst, sems):
        send_sem, recv_sem = sems[:2]
        x, y, c, chips = _place()
        me = _shard_of((x, y))
        out = []
        for a in range(len(shards)):
            for k, chip in enumerate(chips):
                if a < nh:
                    rows = _half_rows(shards[a], c)
                    out.append(_remote(src[a].at[rows], dst[a].at[me, rows], send_sem.at[a, k], recv_sem.at[a, k],
                                       (*chip, c)))
                else:
                    out.append(_remote(src[a], dst[a].at[me], send_sem.at[a, k], recv_sem.at[a, k], (*chip, c)))
        return out

    n = len(shards)

    def bounce(src, dst, sems):
        local_sem, buffers = sems[2], sems[3:]
        me = _shard_of((lax.axis_index("x"), lax.axis_index("y")))
        return [(pltpu.make_async_copy(src[a], buffers[a], local_sem.at[a, 0]),
                 pltpu.make_async_copy(buffers[a], dst[a].at[me], local_sem.at[a, 1])) for a in range(n)]

    scratch = [pltpu.SemaphoreType.DMA((n, 3))] * 2 + [pltpu.SemaphoreType.DMA((n, 2))]
    scratch += [pltpu.VMEM(s.shape, s.dtype) for s in shards]
    return _Comm(shards, [jax.ShapeDtypeStruct((N_SHARDS,) + s.shape, s.dtype) for s in shards], scratch, copies,
                 bounce=bounce)


def _gather_d2d(stacks, shards):
    def copies(src, dst, sems):
        del src
        send_sem, recv_sem = sems
        x, y, c, chips = _place()
        out = []
        for a in range(len(stacks)):
            for k, chip in enumerate(chips):
                rows = dst[a].at[_shard_of(chip), _half_rows(shards[a], c)]
                out.append(_remote(rows, rows, send_sem.at[a, k], recv_sem.at[a, k], (x, y, 1 - c)))
        return out

    n = len(stacks)
    return _Comm(stacks, [jax.ShapeDtypeStruct(s.shape, s.dtype) for s in stacks],
                 [pltpu.SemaphoreType.DMA((n, 3))] * 2, copies, aliased=True)


def _standalone(comms, name):
    return _launch(None, comms, name=name, grid=(), in_specs=[], out_specs=(), out_shape=())()[1]


def _gather_weights(shards):
    n = len(shards)

    def body(*refs):
        src, dst = refs[:n], refs[n:2 * n]
        send_sem, recv_sem, fsend_sem, frecv_sem, local_sem = refs[2 * n:2 * n + 5]
        buffers = refs[2 * n + 5:]
        x, y, c, chips = _place()
        me = _shard_of((x, y))
        sibling = (x, y, 1 - c)
        relay_from = (x ^ (1 - c), y ^ c)
        relay_to = (x ^ c, y ^ (1 - c))
        own = [(pltpu.make_async_copy(src[a], buffers[a], local_sem.at[a, 0]),
                pltpu.make_async_copy(buffers[a], dst[a].at[me], local_sem.at[a, 1])) for a in range(n)]
        for to_vmem, _ in own:
            to_vmem.start()
        sends = []
        for a in range(n):
            mine = _half_rows(shards[a], c)
            for k in range(2):
                cp = _remote(src[a].at[mine], dst[a].at[me, mine], send_sem.at[a, k], recv_sem.at[a, k],
                             (*chips[k], c))
                cp.start()
                sends.append(cp)
        for to_vmem, to_hbm in own:
            to_vmem.wait()
            to_hbm.start()
        for a in range(n):
            mine = _half_rows(shards[a], c)
            for k in range(2):
                rows = dst[a].at[_shard_of(chips[k]), mine]
                _remote(rows, rows, send_sem.at[a, k], recv_sem.at[a, k], sibling).wait_recv()
            passed = dst[a].at[_shard_of(relay_from), mine]
            cp = _remote(passed, passed, send_sem.at[a, 2], recv_sem.at[a, 2], (*relay_to, c))
            cp.start()
            sends.append(cp)
        for a in range(n):
            for k, chip in enumerate(chips):
                rows = dst[a].at[_shard_of(chip), _half_rows(shards[a], c)]
                if k == 2:
                    _remote(rows, rows, send_sem.at[a, k], recv_sem.at[a, k], sibling).wait_recv()
                cp = _remote(rows, rows, fsend_sem.at[a, k], frecv_sem.at[a, k], sibling)
                cp.start()
                sends.append(cp)
        for a in range(n):
            for k, chip in enumerate(chips):
                rows = dst[a].at[_shard_of(chip), _half_rows(shards[a], 1 - c)]
                _remote(rows, rows, fsend_sem.at[a, k], frecv_sem.at[a, k], sibling).wait_recv()
        for cp in sends:
            cp.wait_send()
        for _, to_hbm in own:
            to_hbm.wait()

    return pl.pallas_call(
        body, name="gather_weights",
        out_shape=tuple(jax.ShapeDtypeStruct((N_SHARDS,) + s.shape, s.dtype) for s in shards),
        in_specs=[ANY] * n, out_specs=(ANY,) * n,
        scratch_shapes=[pltpu.SemaphoreType.DMA((n, 3))] * 4 + [pltpu.SemaphoreType.DMA((n, 2))]
        + [pltpu.VMEM(s.shape, s.dtype) for s in shards],
    )(*shards)


def _swap_comm(big, small=()):
    nb, n = len(big), len(big) + len(small)
    arrays = list(big) + list(small)

    def copies(src, dst, sems):
        send_sem, recv_sem = sems
        x, y, c, _ = _place()
        out = []
        for a in range(n):
            if a < nb:
                r2 = arrays[a].shape[1] // 2
                part = src[a].at[:, pl.ds(pl.multiple_of((1 - c) * r2, SUBLANES), r2), :]
            else:
                part = src[a]
            out.append(_remote(part, dst[a], send_sem.at[a], recv_sem.at[a], (x, y, 1 - c)))
        return out

    out_shape = [jax.ShapeDtypeStruct((N_SHARDS, g.shape[1] // 2, g.shape[2]), g.dtype) for g in big]
    out_shape += [jax.ShapeDtypeStruct(g.shape, g.dtype) for g in small]
    return _Comm(arrays, out_shape, [pltpu.SemaphoreType.DMA((n,))] * 2, copies)


def _scatter_comm(slabbed, small=()):
    ns, n = len(slabbed), len(slabbed) + len(small)
    arrays = list(slabbed) + list(small)

    def copies(src, dst, sems):
        send_sem, recv_sem = sems
        _, _, c, chips = _place()
        out = []
        for a in range(n):
            for k, chip in enumerate(chips):
                if a < ns:
                    part = src[a].at[_shard_of(chip)]
                else:
                    h = arrays[a].shape[0] // 2
                    part = src[a].at[pl.ds(pl.multiple_of(c * h, SUBLANES), h), :]
                out.append(_remote(part, dst[a].at[k], send_sem.at[a, k], recv_sem.at[a, k], (*chip, c)))
        return out

    out_shape = [jax.ShapeDtypeStruct((3,) + g.shape[1:], g.dtype) for g in slabbed]
    out_shape += [jax.ShapeDtypeStruct((3, g.shape[0] // 2, g.shape[1]), g.dtype) for g in small]
    return _Comm(arrays, out_shape, [pltpu.SemaphoreType.DMA((n, 3))] * 2, copies)


def _send_comm(arrays):
    n = len(arrays)

    def copies(src, dst, sems):
        send_sem, recv_sem = sems
        x, y, c, _ = _place()
        return [_remote(src[a], dst[a], send_sem.at[a], recv_sem.at[a], (x, y, 1 - c)) for a in range(n)]

    return _Comm(arrays, [jax.ShapeDtypeStruct(h.shape, h.dtype) for h in arrays],
                 [pltpu.SemaphoreType.DMA((n,))] * 2, copies)


def _row_chunk(rows):
    for cand in (256, 176, 128, 64):
        if rows % cand == 0:
            return cand
    return rows


def _per_shape(fn, *lists):
    groups = {}
    for i, items in enumerate(zip(*lists)):
        groups.setdefault(tuple(a.shape for a in items), []).append(i)
    out = [None] * len(lists[0])
    for idx in groups.values():
        for i, r in zip(idx, fn(*([lst[i] for i in idx] for lst in lists))):
            out[i] = r
    return out


def _add_own_half(stacks, receiveds, c):
    n = len(stacks)
    _, r2, cols = receiveds[0].shape

    def body(c_ref, *refs):
        del c_ref
        for a in range(n):
            refs[2 * n + a][...] = (refs[a][...] + refs[n + a][...]).astype(BF16)

    own = pl.BlockSpec((1, r2, cols), lambda s, c_ref: (s, c_ref[0], 0))
    got = pl.BlockSpec((1, r2, cols), lambda s, c_ref: (s, 0, 0))
    return pl.pallas_call(
        body, name="add_own_half",
        grid_spec=pltpu.PrefetchScalarGridSpec(
            num_scalar_prefetch=1, grid=(N_SHARDS,), in_specs=[own] * n + [got] * n, out_specs=(got,) * n),
        out_shape=(jax.ShapeDtypeStruct(receiveds[0].shape, BF16),) * n,
        compiler_params=_params(("parallel",)),
    )(c, *stacks, *receiveds)


def _chip_order_sum(me, own, s0, s1, s2):
    terms = []
    for s in range(N_SHARDS):
        d = jnp.bitwise_xor(me, s)
        terms.append(jnp.where(d == 0, own, jnp.where(d == 2, s0, jnp.where(d == 1, s1, s2))))
    return ((terms[0] + terms[1]) + terms[2]) + terms[3]


def _sum_chips(own_stacks, slots, me):
    n = len(slots)
    _, rows, cols = slots[0].shape
    rc = _row_chunk(rows)

    def body(me_ref, *refs):
        del me_ref
        for a in range(n):
            own_r, s_r = refs[a], refs[n + a]
            refs[2 * n + a][...] = (((own_r[0].astype(F32) + s_r[0].astype(F32)) + s_r[1].astype(F32))
                                    + s_r[2].astype(F32))

    own = pl.BlockSpec((1, rc, cols), lambda i, me_ref: (me_ref[0], i, 0))
    three = pl.BlockSpec((3, rc, cols), lambda i, me_ref: (0, i, 0))
    total = pl.BlockSpec((rc, cols), lambda i, me_ref: (i, 0))
    return pl.pallas_call(
        body, name="sum_chips",
        grid_spec=pltpu.PrefetchScalarGridSpec(
            num_scalar_prefetch=1, grid=(rows // rc,), in_specs=[own] * n + [three] * n, out_specs=(total,) * n),
        out_shape=(jax.ShapeDtypeStruct((rows, cols), F32),) * n,
        compiler_params=_params(("parallel",)),
    )(me, *own_stacks, *slots)


def _small_pair_sums(mine, theirs):
    n = len(mine)

    def body(*refs):
        for a in range(n):
            refs[2 * n + a][...] = refs[a][...] + refs[n + a][...]

    vm = pl.BlockSpec(memory_space=pltpu.VMEM)
    return pl.pallas_call(
        body, name="small_pair_sums", out_shape=tuple(jax.ShapeDtypeStruct(g.shape, g.dtype) for g in mine),
        in_specs=[vm] * (2 * n), out_specs=(vm,) * n,
        compiler_params=pltpu.CompilerParams(vmem_limit_bytes=VMEM_LIMIT),
    )(*mine, *theirs)


def _adam_math(w, g, m, v):
    m = ADAM_B1 * m + (1.0 - ADAM_B1) * g
    v = ADAM_B2 * v + (1.0 - ADAM_B2) * (g * g)
    m_hat = m / (1.0 - ADAM_B1 ** ADAM_STEP)
    v_hat = v / (1.0 - ADAM_B2 ** ADAM_STEP)
    delta = -ADAM_LR * (m_hat / (jnp.sqrt(v_hat) + ADAM_EPS) + ADAM_WD * w)
    return delta, m, v


def _small_totals(me, c, conv_stack, conv_slots, packed, packed_slots):
    half = packed.shape[0] // 2

    def body(me_ref, c_ref, cs_r, cslot_r, p_r, pslot_r, conv_r, tot_r):
        me_ = me_ref[0]
        conv_r[...] = _chip_order_sum(me_, cs_r[me_], cslot_r[0], cslot_r[1], cslot_r[2])
        own = p_r[pl.ds(pl.multiple_of(c_ref[0] * half, SUBLANES), half), :]
        tot_r[...] = _chip_order_sum(me_, own, pslot_r[0], pslot_r[1], pslot_r[2])

    vm = pl.BlockSpec(memory_space=pltpu.VMEM)
    sm = pl.BlockSpec(memory_space=pltpu.SMEM)
    return pl.pallas_call(
        body, name="small_totals",
        out_shape=(jax.ShapeDtypeStruct(conv_stack.shape[1:], F32), jax.ShapeDtypeStruct((half, packed.shape[1]), F32)),
        in_specs=[sm, sm] + [vm] * 4, out_specs=(vm, vm),
    )(me, c, conv_stack, conv_slots, packed, packed_slots)


def _adam_small(gs, ws, ms, vs):
    n = len(gs)

    def body(*refs):
        for a in range(n):
            g_r, w_r, m_r, v_r = (refs[i * n + a] for i in range(4))
            d_r, nm_r, nv_r = (refs[(4 + i) * n + a] for i in range(3))
            d_r[...], nm_r[...], nv_r[...] = _adam_math(w_r[...], g_r[...], m_r[...], v_r[...])

    vm = pl.BlockSpec(memory_space=pltpu.VMEM)
    shapes = tuple(jax.ShapeDtypeStruct(w.shape, F32) for w in ws)
    out = pl.pallas_call(
        body, name="adam_small", out_shape=shapes * 3, in_specs=[vm] * (4 * n), out_specs=(vm,) * (3 * n),
        compiler_params=pltpu.CompilerParams(vmem_limit_bytes=VMEM_LIMIT),
    )(*gs, *ws, *ms, *vs)
    return out[:n], out[n:2 * n], out[2 * n:]


def _adam_big(ws, mines, others, ms, vs, c):
    n = len(ws)
    r2, cols = mines[0].shape
    rc = _row_chunk(r2)
    nch = r2 // rc

    def body(c_ref, *refs):
        mine_is_here = pl.program_id(0) == c_ref[0]
        for a in range(n):
            w_r, mine_r, other_r, m_r, v_r = (refs[i * n + a] for i in range(5))
            g_r, d_r, nm_r, nv_r = (refs[(5 + i) * n + a] for i in range(4))
            g = jnp.where(mine_is_here, mine_r[...], other_r[...])
            g_r[...] = g
            d_r[...], nm_r[...], nv_r[...] = _adam_math(w_r[...], g, m_r[...], v_r[...])

    full = pl.BlockSpec((rc, cols), lambda h, i, c_ref: (h * nch + i, 0))
    half = pl.BlockSpec((rc, cols), lambda h, i, c_ref: (i, 0))
    shape = jax.ShapeDtypeStruct((2 * r2, cols), F32)
    out = pl.pallas_call(
        body, name="adam_big",
        grid_spec=pltpu.PrefetchScalarGridSpec(
            num_scalar_prefetch=1, grid=(2, nch),
            in_specs=[full] * n + [half] * (2 * n) + [full] * (2 * n), out_specs=(full,) * (4 * n)),
        out_shape=(shape,) * (4 * n), compiler_params=_params(("parallel", "parallel")),
    )(c, *ws, *mines, *others, *ms, *vs)
    return [tuple(out[i * n + a] for i in range(4)) for a in range(n)]


def kernel(x, w_in, b_in, ssm_lambda_re, ssm_lambda_im, ssm_log_dt, ssm_b_re, ssm_b_im, ssm_c_re, ssm_c_im, ssm_d, glu_w, glu_b, w_ssm_out, conv_w, w_conv_out, w_o, ln1_g, ln1_b, w_gate, w_up, w_down, ln2_g, ln2_b, loss_target, m_w_in, m_b_in, m_ssm_lambda_re, m_ssm_lambda_im, m_ssm_log_dt, m_ssm_b_re, m_ssm_b_im, m_ssm_c_re, m_ssm_c_im, m_ssm_d, m_glu_w, m_glu_b, m_w_ssm_out, m_conv_w, m_w_conv_out, m_w_o, m_ln1_g, m_ln1_b, m_w_gate, m_w_up, m_w_down, m_ln2_g, m_ln2_b, v_w_in, v_b_in, v_ssm_lambda_re, v_ssm_lambda_im, v_ssm_log_dt, v_ssm_b_re, v_ssm_b_im, v_ssm_c_re, v_ssm_c_im, v_ssm_d, v_glu_w, v_glu_b, v_w_ssm_out, v_conv_w, v_w_conv_out, v_w_o, v_ln1_g, v_ln1_b, v_w_gate, v_w_up, v_w_down, v_ln2_g, v_ln2_b):
    given = dict(locals())
    w = {n: given[n] for n in WEIGHTS}
    m = {n: given["m_" + n] for n in WEIGHTS}
    v = {n: given["v_" + n] for n in WEIGHTS}

    flip = lambda n, a: a.T if n in ("w_gate", "w_up") else a
    shards = {n: flip(n, w[n][0]).astype(BF16) for n in BIG}
    shards["conv_w"] = jnp.pad(conv_w[0], ((0, SUBLANES - 3), (0, 0)))
    c_arr = jnp.reshape(lax.axis_index("c"), (1,)).astype(jnp.int32)
    me = _shard_of((lax.axis_index("x"), lax.axis_index("y")))
    me_arr = jnp.reshape(me, (1,)).astype(jnp.int32)
    dx, pairs, conv_total, packed_total = _device_step(
        x[0], loss_target[0], {n: w[n] for n in SMALL}, shards, c_arr, me_arr)

    grad = _unpack_small(packed_total)
    loss_total = grad.pop("loss")[0]
    grad["conv_w"] = conv_total[:3][None]
    small_names = ("conv_w",) + SMALL
    swap = lambda n, a: a.transpose(0, 1, 3, 2) if n in ("ssm_b_re", "ssm_b_im") else a
    ds, nms, nvs = _adam_small(*([swap(n, d[n]) for n in small_names] for d in (grad, w, m, v)))
    delta, new_m, new_v = {}, {}, {}
    for i, n in enumerate(small_names):
        delta[n], new_m[n], new_v[n] = swap(n, ds[i]), swap(n, nms[i]), swap(n, nvs[i])
    updated = _per_shape(
        lambda *a: _adam_big(*a, c_arr),
        [flip(n, w[n][0]) for n in BIG], [pairs[n][0] for n in BIG], [pairs[n][1] for n in BIG],
        [flip(n, m[n][0]) for n in BIG], [flip(n, v[n][0]) for n in BIG])
    for n, results in zip(BIG, updated):
        grad[n], delta[n], new_m[n], new_v[n] = (flip(n, r)[None] for r in results)

    return (loss_total, dx[None], *[grad[n] for n in WEIGHTS], *[delta[n] for n in WEIGHTS],
            *[new_m[n] for n in WEIGHTS], *[new_v[n] for n in WEIGHTS])
```

```python
import functools
import math

import jax
import jax.numpy as jnp
from jax import lax
from jax.experimental import pallas as pl
from jax.experimental.pallas import tpu as pltpu

F32 = jnp.float32
BF16 = jnp.bfloat16

D_MODEL = 1024
IN_COLS = 4096
SSM_W = 512
N_GROUPS = 32
N_STATE = 64
GROUP_C = 16
STATE_COLS = N_GROUPS * N_STATE
STRIP = 512
N_STRIPS = STATE_COLS // STRIP
FFN_SHARD = 704
N_SHARDS = 4
ALPHA = 2.0 ** 0.25
LN_EPS = 1e-5
GELU_K = math.sqrt(2.0 / math.pi)
GELU_C = 0.044715

ADAM_LR = 0.001
ADAM_B1 = 0.9
ADAM_B2 = 0.999
ADAM_EPS = 1e-08
ADAM_WD = 0.01
ADAM_STEP = 10

V7X_VMEM_BYTES = 64 * 1024 * 1024
VMEM_LIMIT = V7X_VMEM_BYTES - 8 * 1024 * 1024
SUBLANES = 8
N_POWERS = 128

MESH = pl.DeviceIdType.MESH
ANY = pl.BlockSpec(memory_space=pl.ANY)


def _dot(a, b):
    return jnp.dot(a.astype(BF16), b.astype(BF16), preferred_element_type=F32)


def _dot_t(a, b):
    return lax.dot_general(a.astype(BF16), b.astype(BF16), (((1,), (1,)), ((), ())),
                           preferred_element_type=F32)


def _tdot(a, b):
    return lax.dot_general(a.astype(BF16), b.astype(BF16), (((0,), (0,)), ((), ())),
                           preferred_element_type=F32)


def _sigmoid(v):
    return 1.0 / (1.0 + jnp.exp(-v))


def _split3(v):
    hi = v.astype(BF16)
    r1 = v - hi.astype(F32)
    mid = r1.astype(BF16)
    lo = (r1 - mid.astype(F32)).astype(BF16)
    return hi, mid, lo


def _const(shape):
    nd = len(shape)
    return pl.BlockSpec(shape, lambda *_: (0,) * nd)


def _params(sem, vmem=VMEM_LIMIT):
    return pltpu.CompilerParams(dimension_semantics=sem, vmem_limit_bytes=vmem)


def _gelu_parts(v):
    inner = GELU_K * (v + GELU_C * v * v * v)
    t = jnp.tanh(inner)
    g = 0.5 * v * (1.0 + t)
    dg = 0.5 * (1.0 + t) + 0.5 * v * (1.0 - t * t) * GELU_K * (1.0 + 3.0 * GELU_C * v * v)
    return g, dg


class _Comm:
    def __init__(self, inputs, out_shape, sems, copies, aliased=False, bounce=None):
        self.inputs, self.out_shape, self.sems = list(inputs), tuple(out_shape), list(sems)
        self.copies, self.aliased, self.bounce = copies, aliased, bounce


def _launch(body, comms, *, name, grid, in_specs, out_specs, out_shape, scratch_shapes=(), aliases=None, sem=None):
    comms = list(comms)
    n_in, n_out, n_scr = len(in_specs), len(out_specs), len(scratch_shapes)
    aliases = dict(aliases or {})
    layout = []
    p_in, p_out, p_sem = n_in, n_out, 0
    for cm in comms:
        layout.append((p_in, p_out, p_sem))
        if cm.aliased:
            for i in range(len(cm.inputs)):
                aliases[p_in + i] = p_out + i
        p_in, p_out, p_sem = p_in + len(cm.inputs), p_out + len(cm.out_shape), p_sem + len(cm.sems)
    tot_in, tot_out = p_in, p_out

    def fused(*refs):
        ins, outs = refs[:tot_in], refs[tot_in:tot_in + tot_out]
        scr = refs[tot_in + tot_out:tot_in + tot_out + n_scr]
        sems = refs[tot_in + tot_out + n_scr:]

        def descriptors(kind):
            out = []
            for cm, (a, b, s) in zip(comms, layout):
                make = cm.copies if kind == "remote" else cm.bounce
                if make is not None:
                    out += make(ins[a:a + len(cm.inputs)], outs[b:b + len(cm.out_shape)], sems[s:s + len(cm.sems)])
            return out

        steps = [pl.program_id(d) for d in range(len(grid))]
        first = functools.reduce(jnp.logical_and, [s == 0 for s in steps]) if grid else None
        last = functools.reduce(jnp.logical_and, [s == g - 1 for s, g in zip(steps, grid)]) if grid else None

        def start():
            for cp in descriptors("remote"):
                cp.start()
            for to_vmem, _ in descriptors("local"):
                to_vmem.start()

        def finish():
            for to_vmem, to_hbm in descriptors("local"):
                to_vmem.wait()
                to_hbm.start()
            for cp in descriptors("remote"):
                cp.wait()
            for _, to_hbm in descriptors("local"):
                to_hbm.wait()

        if comms:
            pl.when(first)(start) if grid else start()
        if body is not None:
            body(*ins[:n_in], *outs[:n_out], *scr)
        if comms:
            pl.when(last)(finish) if grid else finish()

    specs_in = list(in_specs) + [ANY] * (tot_in - n_in)
    specs_out = tuple(out_specs) + (ANY,) * (tot_out - n_out)
    shapes = tuple(out_shape) + tuple(s for cm in comms for s in cm.out_shape)
    scratch = list(scratch_shapes) + [s for cm in comms for s in cm.sems]
    if comms or sem is None:
        sem = ("arbitrary",) * len(grid)
    kwargs = dict(grid=grid) if grid else {}
    call = pl.pallas_call(fused, name=name, out_shape=shapes, in_specs=specs_in, out_specs=specs_out,
                          scratch_shapes=scratch, input_output_aliases=aliases,
                          compiler_params=_params(sem) if grid else None, **kwargs)

    def run(*args):
        out = call(*args, *(a for cm in comms for a in cm.inputs))
        results, rest = out[:n_out], out[n_out:]
        per_comm = []
        for cm in comms:
            per_comm.append(rest[:len(cm.out_shape)])
            rest = rest[len(cm.out_shape):]
        return results, per_comm

    return run


def _ssm_discretise(lr, li, ldt, lr16, li16, ldt16, brt, bit):
    def lam_bar(lr_, li_, ldt_):
        dt = jnp.exp(ldt_)
        mag = jnp.exp(lr_ * dt)
        return mag * jnp.cos(li_ * dt), mag * jnp.sin(li_ * dt)

    lb_re, lb_im = lam_bar(lr, li, ldt)
    l16_re, l16_im = lam_bar(lr16, li16, ldt16)
    den = lr16 * lr16 + li16 * li16
    num_re = l16_re - 1.0
    fr = (num_re * lr16 + l16_im * li16) / den
    fi = (l16_im * lr16 - num_re * li16) / den
    bb_re = fr * brt - fi * bit
    bb_im = fr * bit + fi * brt
    return lb_re, lb_im, bb_re, bb_im


def _strip_selectors():
    p = lax.broadcasted_iota(jnp.int32, (N_STATE, STRIP), 0)
    col = lax.broadcasted_iota(jnp.int32, (N_STATE, STRIP), 1)
    rep = ((col & (N_STATE - 1)) == p).astype(BF16)
    row = lax.broadcasted_iota(jnp.int32, (SSM_W, STRIP), 0)
    col2 = lax.broadcasted_iota(jnp.int32, (SSM_W, STRIP), 1)
    mask = (((row >> 4) & 7) == (col2 >> 6))
    return rep, mask


def _ssm_prepare(lr, li, ldt, lr16, li16, ldt16, brt, bit, cre, cim):
    def body(lr_r, li_r, ldt_r, lr16_r, li16_r, ldt16_r, brt_r, bit_r, cre_r, cim_r,
             pwr_r, pwi_r, bbr_r, bbi_r, ctr_r, cti_r):
        lb_re, lb_im, bb_re, bb_im = _ssm_discretise(
            lr_r[...], li_r[...], ldt_r[...], lr16_r[...], li16_r[...], ldt16_r[...], brt_r[...], bit_r[...])
        pr, pi_ = lb_re, lb_im
        pwr_r[0] = pr
        pwi_r[0] = pi_
        for k in range(1, N_POWERS):
            pr, pi_ = pr * lb_re - pi_ * lb_im, pr * lb_im + pi_ * lb_re
            pwr_r[k] = pr
            pwi_r[k] = pi_
        rep, mask = _strip_selectors()
        for src, dst in ((bb_re, bbr_r), (bb_im, bbi_r), (cre_r[...], ctr_r), (cim_r[...], cti_r)):
            wide = jnp.dot(src.astype(BF16), rep, preferred_element_type=F32)
            dst[...] = jnp.where(mask, wide, 0.0).astype(BF16)

    vm = pl.BlockSpec(memory_space=pltpu.VMEM)
    return pl.pallas_call(
        body, name="ssm_prepare",
        out_shape=(jax.ShapeDtypeStruct((N_POWERS, N_GROUPS, N_STATE), F32),) * 2
        + (jax.ShapeDtypeStruct((SSM_W, STRIP), BF16),) * 4,
        in_specs=[vm] * 10, out_specs=(vm,) * 6,
    )(lr, li, ldt, lr16, li16, ldt16, brt, bit, cre, cim)


def _scan_tables(pwr, pwi):
    pr = pwr.reshape(N_POWERS, STATE_COLS)
    pi_ = pwi.reshape(N_POWERS, STATE_COLS)
    rows8 = lambda v: jnp.broadcast_to(v[None], (SUBLANES, STATE_COLS))
    tab_a = jnp.stack([rows8(pr[0]), rows8(pi_[0]), rows8(pr[-1]), rows8(pi_[-1])])
    return tab_a, jnp.stack([pr, pi_])


def _ssm_param_grads(lr, li, ldt, lr16, li16, ldt16, brt, bit, dlbr, dlbi, dbbr, dbbi, dctr, dcti):
    def body(lr_r, li_r, ldt_r, lr16_r, li16_r, ldt16_r, brt_r, bit_r,
             dlbr_r, dlbi_r, dbbr_r, dbbi_r, dctr_r, dcti_r,
             glr_r, gli_r, gldt_r, gbrt_r, gbit_r, gcre_r, gcim_r):
        rep, mask = _strip_selectors()

        def fold(acc):
            return sum(lax.dot_general(t, rep, (((1,), (1,)), ((), ())), preferred_element_type=F32)
                       for t in _split3(jnp.where(mask, acc, 0.0)))

        g_lb_re = jnp.sum(dlbr_r[...], axis=0)
        g_lb_im = jnp.sum(dlbi_r[...], axis=0)
        g_bb_re = fold(dbbr_r[...])
        g_bb_im = fold(dbbi_r[...])
        gcre_r[...] = fold(dctr_r[...])
        gcim_r[...] = fold(dcti_r[...])
        prim = (lr_r[...], li_r[...], ldt_r[...], lr16_r[...], li16_r[...], ldt16_r[...], brt_r[...], bit_r[...])
        _, vjp = jax.vjp(_ssm_discretise, *prim)
        g_lr, g_li, g_ldt, g_lr16, g_li16, g_ldt16, g_brt, g_bit = vjp((g_lb_re, g_lb_im, g_bb_re, g_bb_im))
        grp = lax.broadcasted_iota(jnp.int32, (N_GROUPS, SSM_W), 0)
        rw = lax.broadcasted_iota(jnp.int32, (N_GROUPS, SSM_W), 1)
        gsum = ((rw >> 4) == grp).astype(BF16)

        def group_sum(v):
            return sum(jnp.dot(gsum, t, preferred_element_type=F32) for t in _split3(v))

        glr_r[...] = g_lr + group_sum(g_lr16)
        gli_r[...] = g_li + group_sum(g_li16)
        gldt_r[...] = g_ldt + jnp.sum(group_sum(g_ldt16), axis=1, keepdims=True)
        gbrt_r[...] = g_brt
        gbit_r[...] = g_bit

    vm = pl.BlockSpec(memory_space=pltpu.VMEM)
    gp = jax.ShapeDtypeStruct((N_GROUPS, N_STATE), F32)
    gb = jax.ShapeDtypeStruct((SSM_W, N_STATE), F32)
    return pl.pallas_call(
        body, name="ssm_param_grads",
        out_shape=(gp, gp, jax.ShapeDtypeStruct((N_GROUPS, 1), F32), gb, gb, gb, gb),
        in_specs=[vm] * 14, out_specs=(vm,) * 7,
    )(lr, li, ldt, lr16, li16, ldt16, brt, bit, dlbr, dlbi, dbbr, dbbi, dctr, dcti)


def _in_proj(x, w_in_st, b_in, comms=()):
    t = x.shape[0]
    tm = 512

    def body(x_r, w_r, b_r, o_r):
        xb = x_r[...].astype(BF16)
        for j in range(N_SHARDS):
            cols = slice(D_MODEL * j, D_MODEL * (j + 1))
            o_r[:, cols] = jnp.dot(xb, w_r[j], preferred_element_type=F32) + b_r[:, cols]

    (proj,), sent = _launch(
        body, comms, name="in_proj", grid=(t // tm,),
        out_shape=(jax.ShapeDtypeStruct((t, IN_COLS), F32),),
        in_specs=[pl.BlockSpec((tm, D_MODEL), lambda i: (i, 0)), _const((N_SHARDS, D_MODEL, D_MODEL)),
                  _const((1, IN_COLS))],
        out_specs=(pl.BlockSpec((tm, IN_COLS), lambda i: (i, 0)),),
        sem=("parallel",),
    )(x, w_in_st, b_in)
    return proj, sent


def _cmul_add(xr, xi, mr, mi, sr, si):
    return xr + (mr * sr - mi * si), xi + (mr * si + mi * sr)


SCAN_STEPS = N_POWERS
SCAN_CHUNK = SUBLANES * SCAN_STEPS


def _interleave(src_r, dst_r):
    for step in range(SCAN_STEPS):
        dst_r[SUBLANES * step:SUBLANES * (step + 1), :] = src_r[pl.ds(step, SUBLANES, stride=SCAN_STEPS), :]


def _deinterleave(src_r, dst_r):
    for step in range(SCAN_STEPS):
        dst_r[pl.ds(step, SUBLANES, stride=SCAN_STEPS), :] = src_r[SUBLANES * step:SUBLANES * (step + 1), :]


def _step_rows(step):
    return pl.ds(pl.multiple_of(step * SUBLANES, SUBLANES), SUBLANES)


def _scan_steps(body, init, by=4, powers=None):
    if powers is not None:
        by = SUBLANES

    def trip(t, carry):
        if powers is not None:
            rows_re, rows_im = powers[0, _step_rows(t), :], powers[1, _step_rows(t), :]
        for u in range(by):
            if powers is None:
                carry = body(t * by + u, carry)
            else:
                carry = body(t * by + u, carry, jnp.broadcast_to(rows_re[u:u + 1, :], rows_re.shape),
                             jnp.broadcast_to(rows_im[u:u + 1, :], rows_im.shape))
        return carry

    return lax.fori_loop(0, SCAN_STEPS // by, trip, init)


def _segment_states(first_r, first_i, ends_r, ends_i, a64_r, a64_i, order):
    row = lax.broadcasted_iota(jnp.int32, ends_r.shape, 0)
    cur_r, cur_i = first_r, first_i
    ent_r = jnp.zeros_like(ends_r)
    ent_i = jnp.zeros_like(ends_i)
    for s in order:
        ent_r = jnp.where(row == s, jnp.broadcast_to(cur_r, ends_r.shape), ent_r)
        ent_i = jnp.where(row == s, jnp.broadcast_to(cur_i, ends_i.shape), ent_i)
        cur_r, cur_i = _cmul_add(ends_r[s:s + 1, :], ends_i[s:s + 1, :], a64_r, a64_i, cur_r, cur_i)
    return ent_r, ent_i, cur_r, cur_i


def _ssm_forward(proj, bbr, bbi, ctr, cti, d_skip, tab_a, tab_p, comms=(), tc=SCAN_CHUNK):
    t = proj.shape[0]

    def body(u_r, bbr_r, bbi_r, ctr_r, cti_r, d_r, ta_r, tp_r, xsr_r, xsi_r, y_r, ui_s, yi_s, car_r, car_i):
        @pl.when(pl.program_id(1) == 0)
        def _():
            car_r[...] = jnp.zeros_like(car_r)
            car_i[...] = jnp.zeros_like(car_i)

        _interleave(u_r, ui_s)
        u = ui_s[...]
        xsr_r[...] = _dot(u, bbr_r[...])
        xsi_r[...] = _dot(u, bbi_r[...])
        a_r, a_i = ta_r[0], ta_r[1]

        def local(step, carry):
            rows = _step_rows(step)
            xr, xi = _cmul_add(xsr_r[rows, :], xsi_r[rows, :], a_r, a_i, *carry)
            xsr_r[rows, :] = xr
            xsi_r[rows, :] = xi
            return xr, xi

        zero = jnp.zeros((SUBLANES, STRIP), F32)
        ends_r, ends_i = _scan_steps(local, (zero, zero))
        ent_r, ent_i, out_r, out_i = _segment_states(
            car_r[...], car_i[...], ends_r, ends_i, ta_r[2, 0:1, :], ta_r[3, 0:1, :], range(SUBLANES))
        car_r[...] = out_r
        car_i[...] = out_i

        def entering(step, _, power_r, power_i):
            rows = _step_rows(step)
            xr, xi = _cmul_add(xsr_r[rows, :], xsi_r[rows, :], power_r, power_i, ent_r, ent_i)
            xsr_r[rows, :] = xr
            xsi_r[rows, :] = xi
            return 0

        _scan_steps(entering, 0, powers=tp_r)
        yi_s[...] = _dot_t(xsr_r[...], ctr_r[...]) - _dot_t(xsi_r[...], cti_r[...]) + d_r[...] * u
        _deinterleave(yi_s, y_r)

    strip_mat = pl.BlockSpec((128, STRIP), lambda j, k: (j, 0))
    states = pl.BlockSpec((tc, STRIP), lambda j, k: (k, j))
    return _launch(
        body, comms, name="ssm_forward", grid=(N_STRIPS, t // tc),
        out_shape=(jax.ShapeDtypeStruct((t, STATE_COLS), F32), jax.ShapeDtypeStruct((t, STATE_COLS), F32),
                   jax.ShapeDtypeStruct((t, SSM_W), F32)),
        in_specs=[pl.BlockSpec((tc, 128), lambda j, k: (k, j)),
                  strip_mat, strip_mat, strip_mat, strip_mat,
                  pl.BlockSpec((1, 128), lambda j, k: (0, j)),
                  pl.BlockSpec((4, SUBLANES, STRIP), lambda j, k: (0, 0, j)),
                  pl.BlockSpec((2, SCAN_STEPS, STRIP), lambda j, k: (0, 0, j))],
        out_specs=(states, states, pl.BlockSpec((tc, 128), lambda j, k: (k, j))),
        scratch_shapes=[pltpu.VMEM((tc, 128), F32), pltpu.VMEM((tc, 128), F32),
                        pltpu.VMEM((1, STRIP), F32), pltpu.VMEM((1, STRIP), F32)],
        sem=("parallel", "arbitrary"),
    )(proj, bbr, bbi, ctr, cti, d_skip, tab_a, tab_p)


def _shift_down(v, prev, n):
    row = lax.broadcasted_iota(jnp.int32, v.shape, 0)
    out = pltpu.roll(v, n, 0)
    for r in range(n):
        src = prev[SUBLANES - n + r:SUBLANES - n + r + 1, :]
        out = jnp.where(row == r, jnp.broadcast_to(src, v.shape), out)
    return out


def _shift_up(v, nxt, n):
    rows = v.shape[0]
    row = lax.broadcasted_iota(jnp.int32, v.shape, 0)
    out = pltpu.roll(v, rows - n, 0)
    for r in range(n):
        src = nxt[r:r + 1, :]
        out = jnp.where(row == rows - n + r, jnp.broadcast_to(src, v.shape), out)
    return out


def _conv3(q, q_prev, w):
    return w[2:3, :] * q + w[1:2, :] * _shift_down(q, q_prev, 1) + w[0:1, :] * _shift_down(q, q_prev, 2)


def _mixer_forward(x, proj, ya0, glu_w, glu_b, wso_st, conv_w8, wco_st, w_o, comms=(), tm=256):
    t = x.shape[0]
    hb = tm // SUBLANES

    def body(x_r, ya0_r, h_r, cg_r, bg_r, ga_r, gb_r, hp_r, cgp_r,
             glu_w_r, glu_b_r, wso_r, cw_r, wco_r, wo_r, xh_r, rstd_r, ya_r, yb_r):
        i = pl.program_id(0)
        g, _ = _gelu_parts(ya0_r[...])
        ya1 = g * _sigmoid(_dot(g, glu_w_r[...]) + glu_b_r[...])
        q = cg_r[...] * h_r[...]
        q_prev = jnp.where(i > 0, cgp_r[...] * hp_r[...], 0.0)
        yb0 = bg_r[...] * _conv3(q, q_prev, cw_r[...])
        for j in range(N_SHARDS):
            ya_r[:, 256 * j:256 * (j + 1)] = _dot(ya1, wso_r[j])
            yb_r[:, 256 * j:256 * (j + 1)] = _dot(yb0, wco_r[j])
        merged = _sigmoid(ga_r[...]) * ya_r[...] + _sigmoid(gb_r[...]) * yb_r[...]
        r1 = ALPHA * x_r[...] + _dot(merged, wo_r[...])
        mu = jnp.mean(r1, axis=-1, keepdims=True)
        cen = r1 - mu
        rstd = lax.rsqrt(jnp.mean(cen * cen, axis=-1, keepdims=True) + LN_EPS)
        xh_r[...] = cen * rstd
        rstd_r[...] = rstd

    def col(w, c):
        return pl.BlockSpec((tm, w), lambda i: (i, c))

    def prev(c):
        return pl.BlockSpec((SUBLANES, SSM_W), lambda i: (jnp.maximum(i * hb - 1, 0), c))

    return _launch(
        body, comms, name="mixer_forward", grid=(t // tm,),
        out_shape=(jax.ShapeDtypeStruct((t, D_MODEL), F32), jax.ShapeDtypeStruct((t, 1), F32),
                   jax.ShapeDtypeStruct((t, D_MODEL), F32), jax.ShapeDtypeStruct((t, D_MODEL), F32)),
        in_specs=[col(D_MODEL, 0), col(SSM_W, 0), col(SSM_W, 1), col(SSM_W, 2), col(SSM_W, 3),
                  col(D_MODEL, 2), col(D_MODEL, 3), prev(1), prev(2),
                  _const((SSM_W, SSM_W)), _const((1, SSM_W)), _const((N_SHARDS, SSM_W, 256)),
                  _const((SUBLANES, SSM_W)), _const((N_SHARDS, SSM_W, 256)), _const((D_MODEL, D_MODEL))],
        out_specs=(col(D_MODEL, 0), pl.BlockSpec((tm, 1), lambda i: (i, 0)), col(D_MODEL, 0), col(D_MODEL, 0)),
        sem=("parallel",),
    )(x, ya0, proj, proj, proj, proj, proj, proj, proj, glu_w, glu_b, wso_st, conv_w8, wco_st, w_o)


def _layer_norm_bwd(dxhat, xhat, rstd):
    m1 = jnp.mean(dxhat, axis=-1, keepdims=True)
    m2 = jnp.mean(dxhat * xhat, axis=-1, keepdims=True)
    return rstd * (dxhat - m1 - xhat * m2)


def _ffn_step(xhat1, rstd1, target, ln1_g, ln1_b, ln2_g, ln2_b, wg_st, wu_st, wd_st, tm=256):
    t = xhat1.shape[0]

    def body(xh_r, rstd_r, tgt_r, g1_r, b1_r, g2_r, b2_r, wg_r, wu_r, wd_r,
             loss_r, dr1_r, x1b_r, dr2b_r, hid_r, dhg_r, dhu_r, dg2_r, db2_r, dg1_r, db1_r,
             hg_s, hu_s):
        @pl.when(pl.program_id(0) == 0)
        def _():
            for r in (loss_r, dg2_r, db2_r, dg1_r, db1_r):
                r[...] = jnp.zeros_like(r)

        xhat1_v = xh_r[...]
        x1 = xhat1_v * g1_r[...] + b1_r[...]
        x1b = x1.astype(BF16)
        x1b_r[...] = x1b
        ffn = jnp.zeros((tm, D_MODEL), F32)
        for j in range(N_SHARDS):
            hg = lax.dot_general(x1b, wg_r[j], (((1,), (1,)), ((), ())), preferred_element_type=F32)
            hu = lax.dot_general(x1b, wu_r[j], (((1,), (1,)), ((), ())), preferred_element_type=F32)
            hg_s[j] = hg
            hu_s[j] = hu
            hid = (hg * _sigmoid(hg) * hu).astype(BF16)
            hid_r[j] = hid
            ffn = ffn + jnp.dot(hid, wd_r[j], preferred_element_type=F32)
        r2 = ALPHA * x1 + ffn
        mu = jnp.mean(r2, axis=-1, keepdims=True)
        cen = r2 - mu
        rstd2 = lax.rsqrt(jnp.mean(cen * cen, axis=-1, keepdims=True) + LN_EPS)
        xhat2 = cen * rstd2
        diff = (xhat2 * g2_r[...] + b2_r[...]) - tgt_r[...]
        loss_r[...] += 0.5 * jnp.sum(jnp.mean(diff * diff, axis=-1, keepdims=True), axis=0, keepdims=True)
        dy = diff * (1.0 / D_MODEL)
        dg2_r[...] += jnp.sum(dy * xhat2, axis=0, keepdims=True)
        db2_r[...] += jnp.sum(dy, axis=0, keepdims=True)
        dr2 = _layer_norm_bwd(dy * g2_r[...], xhat2, rstd2)
        dr2b = dr2.astype(BF16)
        dr2b_r[...] = dr2b
        dx1 = ALPHA * dr2
        for j in range(N_SHARDS):
            dhid = lax.dot_general(dr2b, wd_r[j], (((1,), (1,)), ((), ())), preferred_element_type=F32)
            hg = hg_s[j]
            hu = hu_s[j]
            sg = _sigmoid(hg)
            dhu = (dhid * (hg * sg)).astype(BF16)
            dhg = (dhid * hu * (sg * (1.0 + hg * (1.0 - sg)))).astype(BF16)
            dhg_r[j] = dhg
            dhu_r[j] = dhu
            dx1 = dx1 + jnp.dot(dhg, wg_r[j], preferred_element_type=F32)
            dx1 = dx1 + jnp.dot(dhu, wu_r[j], preferred_element_type=F32)
        dg1_r[...] += jnp.sum(dx1 * xhat1_v, axis=0, keepdims=True)
        db1_r[...] += jnp.sum(dx1, axis=0, keepdims=True)
        dr1_r[...] = _layer_norm_bwd(dx1 * g1_r[...], xhat1_v, rstd_r[...])

    tile = pl.BlockSpec((tm, D_MODEL), lambda i: (i, 0))
    hidden = pl.BlockSpec((N_SHARDS, tm, FFN_SHARD), lambda i: (0, i, 0))
    vec = _const((1, D_MODEL))
    hid_shape = jax.ShapeDtypeStruct((N_SHARDS, t, FFN_SHARD), BF16)
    vec_shape = jax.ShapeDtypeStruct((1, D_MODEL), F32)
    return pl.pallas_call(
        body, name="ffn_step", grid=(t // tm,),
        out_shape=(jax.ShapeDtypeStruct((1, 1), F32), jax.ShapeDtypeStruct((t, D_MODEL), F32),
                   jax.ShapeDtypeStruct((t, D_MODEL), BF16), jax.ShapeDtypeStruct((t, D_MODEL), BF16),
                   hid_shape, hid_shape, hid_shape, vec_shape, vec_shape, vec_shape, vec_shape),
        in_specs=[tile, pl.BlockSpec((tm, 1), lambda i: (i, 0)), tile, vec, vec, vec, vec,
                  _const((N_SHARDS, FFN_SHARD, D_MODEL)), _const((N_SHARDS, FFN_SHARD, D_MODEL)),
                  _const((N_SHARDS, FFN_SHARD, D_MODEL))],
        out_specs=(_const((1, 1)), tile, tile, tile, hidden, hidden, hidden, vec, vec, vec, vec),
        scratch_shapes=[pltpu.VMEM((N_SHARDS, tm, FFN_SHARD), F32), pltpu.VMEM((N_SHARDS, tm, FFN_SHARD), F32)],
        compiler_params=_params(("arbitrary",)),
    )(xhat1, rstd1, target, ln1_g, ln1_b, ln2_g, ln2_b, wg_st, wu_st, wd_st)


def _ffn_weight_grads(x1b, dr2b, hid, dhg, dhu, tk=2048):
    t = x1b.shape[0]

    def body(x_r, dr_r, hid_r, dhg_r, dhu_r, gwg_r, gwu_r, gwd_r):
        @pl.when(pl.program_id(1) == 0)
        def _():
            for r in (gwg_r, gwu_r, gwd_r):
                r[...] = jnp.zeros_like(r)

        gwg_r[0] += _tdot(dhg_r[0], x_r[...])
        gwu_r[0] += _tdot(dhu_r[0], x_r[...])
        gwd_r[0] += _tdot(hid_r[0], dr_r[...])

    tile = pl.BlockSpec((tk, D_MODEL), lambda j, k: (k, 0))
    hidden = pl.BlockSpec((1, tk, FFN_SHARD), lambda j, k: (j, k, 0))
    row = pl.BlockSpec((1, FFN_SHARD, D_MODEL), lambda j, k: (j, 0, 0))
    return pl.pallas_call(
        body, name="ffn_weight_grads", grid=(N_SHARDS, t // tk),
        out_shape=(jax.ShapeDtypeStruct((N_SHARDS, FFN_SHARD, D_MODEL), F32),) * 3,
        in_specs=[tile, tile, hidden, hidden, hidden],
        out_specs=(row, row, row),
        compiler_params=_params(("parallel", "arbitrary")),
    )(x1b, dr2b, hid, dhg, dhu)


def _mixer_backward(dr1, proj, ya0, ya, yb, glu_w, glu_b, wso_st, conv_w8, wco_st, w_o, comms=(), tm=256):
    t = dr1.shape[0]
    hb = tm // SUBLANES
    last_block = t // SUBLANES - 1

    def body(dr1_r, dr1n_r, ya0_r, ya_r, yb_r, h_r, cg_r, bg_r, ga_r, gb_r, hp_r, cgp_r, bgn_r, gbn_r,
             glu_w_r, glu_b_r, wso_r, cw_r, wco_r, wo_r,
             dya0_r, dproj_r, dbias_r, gwo_r, gwso_r, gwco_r, gglu_w_r, gglu_b_r, gconv_r):
        i = pl.program_id(0)

        @pl.when(i == 0)
        def _():
            for r in (dbias_r, gwo_r, gwso_r, gwco_r, gglu_w_r, gglu_b_r, gconv_r):
                r[...] = jnp.zeros_like(r)

        dr1_v = dr1_r[...]
        dmerged = _dot_t(dr1_v, wo_r[...])
        sa = _sigmoid(ga_r[...])
        sb = _sigmoid(gb_r[...])
        ya_v = ya_r[...]
        yb_v = yb_r[...]
        gwo_r[...] += _tdot(sa * ya_v + sb * yb_v, dr1_v)
        dya = dmerged * sa
        dyb = dmerged * sb
        dga = dmerged * ya_v * (sa * (1.0 - sa))
        dgb = dmerged * yb_v * (sb * (1.0 - sb))

        g, gelu_grad = _gelu_parts(ya0_r[...])
        s1 = _sigmoid(_dot(g, glu_w_r[...]) + glu_b_r[...])
        ya1 = g * s1
        dya1 = jnp.zeros((tm, SSM_W), F32)
        for j in range(N_SHARDS):
            dya_j = dya[:, 256 * j:256 * (j + 1)]
            gwso_r[j] += _tdot(ya1, dya_j)
            dya1 = dya1 + _dot_t(dya_j, wso_r[j])
        dz1 = dya1 * g * (s1 * (1.0 - s1))
        gglu_b_r[...] += jnp.sum(dz1, axis=0, keepdims=True)
        gglu_w_r[...] += _tdot(g, dz1)
        dya0_r[...] = (dya1 * s1 + _dot_t(dz1, glu_w_r[...])) * gelu_grad

        cw = cw_r[...]
        h = h_r[...]
        cg = cg_r[...]
        bg = bg_r[...]
        q = cg * h
        q_prev = jnp.where(i > 0, cgp_r[...] * hp_r[...], 0.0)
        q1 = _shift_down(q, q_prev, 1)
        q2 = _shift_down(q, q_prev, 2)
        z = cw[2:3, :] * q + cw[1:2, :] * q1 + cw[0:1, :] * q2
        yb0 = bg * z
        dyb0 = jnp.zeros((tm, SSM_W), F32)
        for j in range(N_SHARDS):
            dyb_j = dyb[:, 256 * j:256 * (j + 1)]
            gwco_r[j] += _tdot(yb0, dyb_j)
            dyb0 = dyb0 + _dot_t(dyb_j, wco_r[j])
        dbg = dyb0 * z
        dz = dyb0 * bg
        dyb_n = _dot_t(dr1n_r[...], wo_r[...]) * _sigmoid(gbn_r[...])
        dyb0_n = jnp.zeros((SUBLANES, SSM_W), F32)
        for j in range(N_SHARDS):
            dyb0_n = dyb0_n + _dot_t(dyb_n[:, 256 * j:256 * (j + 1)], wco_r[j])
        dz_next = jnp.where(i < pl.num_programs(0) - 1, dyb0_n * bgn_r[...], 0.0)
        dq = cw[2:3, :] * dz + cw[1:2, :] * _shift_up(dz, dz_next, 1) + cw[0:1, :] * _shift_up(dz, dz_next, 2)
        gconv_r[0:1, :] += jnp.sum(dz * q2, axis=0, keepdims=True)
        gconv_r[1:2, :] += jnp.sum(dz * q1, axis=0, keepdims=True)
        gconv_r[2:3, :] += jnp.sum(dz * q, axis=0, keepdims=True)
        dh = dq * cg
        dcg = dq * h

        dproj_r[:, 0:512] = jnp.zeros((tm, SSM_W), BF16)
        pieces = ((512, dh), (1024, dcg), (1536, dbg), (2048, dga), (3072, dgb))
        for off, val in pieces:
            w = val.shape[1]
            dproj_r[:, off:off + w] = val.astype(BF16)
            dbias_r[:, off:off + w] += jnp.sum(val, axis=0, keepdims=True)

    def col(w, c):
        return pl.BlockSpec((tm, w), lambda i: (i, c))

    def prev(c):
        return pl.BlockSpec((SUBLANES, SSM_W), lambda i: (jnp.maximum(i * hb - 1, 0), c))

    def nxt(w, c):
        return pl.BlockSpec((SUBLANES, w), lambda i: (jnp.minimum((i + 1) * hb, last_block), c))

    sh = jax.ShapeDtypeStruct
    return _launch(
        body, comms, name="mixer_backward", grid=(t // tm,),
        out_shape=(sh((t, SSM_W), F32), sh((t, IN_COLS), BF16), sh((1, IN_COLS), F32),
                   sh((D_MODEL, D_MODEL), F32), sh((N_SHARDS, SSM_W, 256), F32), sh((N_SHARDS, SSM_W, 256), F32),
                   sh((SSM_W, SSM_W), F32), sh((1, SSM_W), F32), sh((SUBLANES, SSM_W), F32)),
        in_specs=[col(D_MODEL, 0), nxt(D_MODEL, 0), col(SSM_W, 0), col(D_MODEL, 0), col(D_MODEL, 0),
                  col(SSM_W, 1), col(SSM_W, 2), col(SSM_W, 3), col(D_MODEL, 2), col(D_MODEL, 3),
                  prev(1), prev(2), nxt(SSM_W, 3), nxt(D_MODEL, 3),
                  _const((SSM_W, SSM_W)), _const((1, SSM_W)), _const((N_SHARDS, SSM_W, 256)),
                  _const((SUBLANES, SSM_W)), _const((N_SHARDS, SSM_W, 256)), _const((D_MODEL, D_MODEL))],
        out_specs=(col(SSM_W, 0), col(IN_COLS, 0), _const((1, IN_COLS)),
                   _const((D_MODEL, D_MODEL)), _const((N_SHARDS, SSM_W, 256)), _const((N_SHARDS, SSM_W, 256)),
                   _const((SSM_W, SSM_W)), _const((1, SSM_W)), _const((SUBLANES, SSM_W))),
        sem=("arbitrary",),
    )(dr1, dr1, ya0, ya, yb, proj, proj, proj, proj, proj, proj, proj, proj, proj,
      glu_w, glu_b, wso_st, conv_w8, wco_st, w_o)


def _cmulc_add(xr, xi, mr, mi, sr, si):
    return xr + (mr * sr + mi * si), xi + (mr * si - mi * sr)


def _ssm_backward(dya0, proj, xsr, xsi, bbr, bbi, ctr, cti, d_skip, tab_a, tab_p, dproj, comms=(), tc=SCAN_CHUNK):
    t = proj.shape[0]
    nk = t // tc

    def body(dy_r, u_r, xsr_r, xsi_r, bbr_r, bbi_r, ctr_r, cti_r, d_r, ta_r, tp_r, dproj_any,
             du_r, dus_r, gbbr_r, gbbi_r, gctr_r, gcti_r, glbr_r, glbi_r, gd_r,
             gr_s, gi_s, dyi_s, ui_s, dui_s, dun_s, car_r, car_i):
        del dproj_any

        @pl.when(pl.program_id(1) == 0)
        def _():
            for r in (car_r, car_i, dus_r, gbbr_r, gbbi_r, gctr_r, gcti_r, glbr_r, glbi_r, gd_r):
                r[...] = jnp.zeros_like(r)

        _interleave(dy_r, dyi_s)
        _interleave(u_r, ui_s)
        dy = dyi_s[...]
        u = ui_s[...]
        gr_s[...] = _dot(dy, ctr_r[...])
        gi_s[...] = -_dot(dy, cti_r[...])
        a_r, a_i = ta_r[0], ta_r[1]

        def local(n, carry):
            rows = _step_rows(SCAN_STEPS - 1 - n)
            gr, gi = _cmulc_add(gr_s[rows, :], gi_s[rows, :], a_r, a_i, *carry)
            gr_s[rows, :] = gr
            gi_s[rows, :] = gi
            return gr, gi

        zero = jnp.zeros((SUBLANES, STRIP), F32)
        ends_r, ends_i = _scan_steps(local, (zero, zero))
        ent_r, ent_i, out_r, out_i = _segment_states(
            car_r[...], car_i[...], ends_r, ends_i, ta_r[2, 0:1, :], -ta_r[3, 0:1, :], range(SUBLANES - 1, -1, -1))
        car_r[...] = out_r
        car_i[...] = out_i

        def entering(n, carry, power_r, power_i):
            gnr, gni, ar, ai = carry
            rows = _step_rows(SCAN_STEPS - 1 - n)
            gr, gi = _cmulc_add(gr_s[rows, :], gi_s[rows, :], power_r, power_i, ent_r, ent_i)
            gr_s[rows, :] = gr
            gi_s[rows, :] = gi
            xr = xsr_r[rows, :]
            xi = xsi_r[rows, :]
            return gr, gi, ar + (xr * gnr + xi * gni), ai + (xr * gni - xi * gnr)

        _, _, ar, ai = _scan_steps(entering, (ent_r, ent_i, zero, zero), powers=tp_r)
        glbr_r[...] += ar
        glbi_r[...] += ai
        gr = gr_s[...]
        gi = gi_s[...]
        dui_s[...] = _dot_t(gr, bbr_r[...]) + _dot_t(gi, bbi_r[...]) + d_r[...] * dy
        _deinterleave(dui_s, dun_s)
        du = dun_s[...]
        du_r[...] = du.astype(BF16)
        dus_r[...] += jnp.sum(du, axis=0, keepdims=True)
        gd_r[...] += jnp.sum(dy * u, axis=0, keepdims=True)
        gbbr_r[...] += _tdot(u, gr)
        gbbi_r[...] += _tdot(u, gi)
        gctr_r[...] += _tdot(dy, xsr_r[...])
        gcti_r[...] -= _tdot(dy, xsi_r[...])

    def rev(w):
        return pl.BlockSpec((tc, w), lambda j, k: (nk - 1 - k, j))

    strip_mat = pl.BlockSpec((128, STRIP), lambda j, k: (j, 0))
    vec = pl.BlockSpec((1, 128), lambda j, k: (0, j))
    lbacc = pl.BlockSpec((SUBLANES, STRIP), lambda j, k: (0, j))
    sh = jax.ShapeDtypeStruct
    return _launch(
        body, comms, name="ssm_backward", grid=(N_STRIPS, nk),
        out_shape=(sh((t, IN_COLS), BF16), sh((1, SSM_W), F32),
                   sh((SSM_W, STRIP), F32), sh((SSM_W, STRIP), F32), sh((SSM_W, STRIP), F32), sh((SSM_W, STRIP), F32),
                   sh((SUBLANES, STATE_COLS), F32), sh((SUBLANES, STATE_COLS), F32), sh((1, SSM_W), F32)),
        in_specs=[rev(128), rev(128), rev(STRIP), rev(STRIP),
                  strip_mat, strip_mat, strip_mat, strip_mat, vec,
                  pl.BlockSpec((4, SUBLANES, STRIP), lambda j, k: (0, 0, j)),
                  pl.BlockSpec((2, SCAN_STEPS, STRIP), lambda j, k: (0, 0, j)), ANY],
        out_specs=(rev(128), vec, strip_mat, strip_mat, strip_mat, strip_mat, lbacc, lbacc, vec),
        scratch_shapes=[pltpu.VMEM((tc, STRIP), F32), pltpu.VMEM((tc, STRIP), F32)]
        + [pltpu.VMEM((tc, 128), F32)] * 4 + [pltpu.VMEM((1, STRIP), F32)] * 2,
        aliases={11: 0}, sem=("parallel", "arbitrary"),
    )(dya0, proj, xsr, xsi, bbr, bbi, ctr, cti, d_skip, tab_a, tab_p, dproj)


def _input_grad(dr1, dproj, w_in_st, comms=(), tm=512):
    t = dr1.shape[0]

    def body(dr1_r, dp_r, w_r, dx_r):
        acc = ALPHA * dr1_r[...]
        for j in range(N_SHARDS):
            acc = acc + lax.dot_general(dp_r[:, D_MODEL * j:D_MODEL * (j + 1)], w_r[j],
                                        (((1,), (1,)), ((), ())), preferred_element_type=F32)
        dx_r[...] = acc

    (dx,), sent = _launch(
        body, comms, name="input_grad", grid=(t // tm,),
        out_shape=(jax.ShapeDtypeStruct((t, D_MODEL), F32),),
        in_specs=[pl.BlockSpec((tm, D_MODEL), lambda i: (i, 0)), pl.BlockSpec((tm, IN_COLS), lambda i: (i, 0)),
                  _const((N_SHARDS, D_MODEL, D_MODEL))],
        out_specs=(pl.BlockSpec((tm, D_MODEL), lambda i: (i, 0)),),
        sem=("parallel",),
    )(dr1, dproj, w_in_st)
    return dx, sent


def _in_weight_grad(x, dproj, comms=(), tk=2048):
    t = x.shape[0]

    def body(x_r, dp_r, gw_r):
        @pl.when(pl.program_id(1) == 0)
        def _():
            gw_r[...] = jnp.zeros_like(gw_r)

        gw_r[0] += _tdot(x_r[...], dp_r[...])

    (g_w_in,), sent = _launch(
        body, comms, name="in_weight_grad", grid=(N_SHARDS, t // tk),
        out_shape=(jax.ShapeDtypeStruct((N_SHARDS, D_MODEL, D_MODEL), F32),),
        in_specs=[pl.BlockSpec((tk, D_MODEL), lambda j, k: (k, 0)), pl.BlockSpec((tk, D_MODEL), lambda j, k: (k, j))],
        out_specs=(pl.BlockSpec((1, D_MODEL, D_MODEL), lambda j, k: (j, 0, 0)),),
        sem=("parallel", "arbitrary"),
    )(x, dproj)
    return g_w_in, sent


MIXER_W = ("glu_w", "w_ssm_out", "w_conv_out", "w_o")
FFN_W = ("w_gate", "w_up", "w_down")


def _device_step(x, target, small, shards, c_arr, me_arr):
    lr, li = small["ssm_lambda_re"][0], small["ssm_lambda_im"][0]
    ldt = small["ssm_log_dt"][0][:, None]
    rep16 = lambda a: jnp.broadcast_to(a[:, None, :], (N_GROUPS, GROUP_C, a.shape[-1])).reshape(SSM_W, a.shape[-1])
    lr16, li16 = rep16(lr), rep16(li)
    ldt16 = rep16(jnp.broadcast_to(ldt, (N_GROUPS, N_STATE)))
    brt = small["ssm_b_re"][0].transpose(0, 2, 1).reshape(SSM_W, N_STATE)
    bit = small["ssm_b_im"][0].transpose(0, 2, 1).reshape(SSM_W, N_STATE)
    cre = small["ssm_c_re"][0].reshape(SSM_W, N_STATE)
    cim = small["ssm_c_im"][0].reshape(SSM_W, N_STATE)
    disc = (lr, li, ldt, lr16, li16, ldt16, brt, bit)

    pwr, pwi, bbr, bbi, ctr, cti = _ssm_prepare(*disc, cre, cim)
    tab_a, tab_p = _scan_tables(pwr, pwi)

    first_sh = [shards[n] for n in MIXER_W + FFN_W[:1]]
    second_sh = [shards[n] for n in FFN_W[1:]]
    (w_in_st,) = _gather_weights([shards["w_in"]])
    proj, (arrived,) = _in_proj(x, w_in_st, small["b_in"], comms=[_gather_ici(first_sh, [shards["conv_w"]])])
    (xsr, xsi, ya0), (second_part, first_st) = _ssm_forward(
        proj, bbr, bbi, ctr, cti, small["ssm_d"], tab_a, tab_p,
        comms=[_gather_ici(second_sh), _gather_d2d(arrived[:len(first_sh)], first_sh)])
    glu_st, wso_st, wco_st, wo_st, wg_st = first_st
    conv_st = arrived[len(first_sh)]
    conv_w8 = jnp.pad(conv_st[:, :3, :].transpose(1, 0, 2).reshape(3, SSM_W), ((0, SUBLANES - 3), (0, 0)))
    w_o = wo_st.reshape(D_MODEL, D_MODEL)
    glu_w = glu_st.reshape(SSM_W, SSM_W)
    (xhat1, rstd1, ya, yb), (second_st,) = _mixer_forward(
        x, proj, ya0, glu_w, small["glu_b"], wso_st, conv_w8, wco_st, w_o, comms=[_gather_d2d(second_part, second_sh)])
    wu_st, wd_st = second_st
    (loss, dr1, x1b, dr2b, hid, dhg, dhu, g_ln2_g, g_ln2_b, g_ln1_g, g_ln1_b) = _ffn_step(
        xhat1, rstd1, target, small["ln1_g"], small["ln1_b"], small["ln2_g"], small["ln2_b"], wg_st, wu_st, wd_st)

    add_halves = lambda gs, rs: _per_shape(lambda a, b: _add_own_half(a, b, c_arr), list(gs), list(rs))
    sum_chips = lambda owns, slots: _per_shape(lambda a, b: _sum_chips(a, b, me_arr), list(owns), list(slots))
    g_ffn = _ffn_weight_grads(x1b, dr2b, hid, dhg, dhu)
    (dya0, dproj, dbias, g_wo, g_wso, g_wco, g_glu_w, g_glu_b, g_conv8), (got_ffn,) = _mixer_backward(
        dr1, proj, ya0, ya, yb, glu_w, small["glu_b"], wso_st, conv_w8, wco_st, w_o, comms=[_swap_comm(g_ffn)])
    chip_ffn = add_halves(g_ffn, got_ffn)
    g_mix = [g_glu_w.reshape(N_SHARDS, 128, SSM_W), g_wso, g_wco, g_wo.reshape(N_SHARDS, 256, D_MODEL)]
    (dproj, dus, gbbr, gbbi, gctr, gcti, glbr, glbi, g_d), (slots_ffn, got_mix) = _ssm_backward(
        dya0, proj, xsr, xsi, bbr, bbi, ctr, cti, small["ssm_d"], tab_a, tab_p, dproj,
        comms=[_scatter_comm(chip_ffn), _swap_comm(g_mix)])
    halves_ffn = sum_chips(chip_ffn, slots_ffn)
    chip_mix = add_halves(g_mix, got_mix)
    g_lr, g_li, g_ldt, g_brt, g_bit, g_cre, g_cim = _ssm_param_grads(
        *disc, glbr.reshape(SUBLANES, N_GROUPS, N_STATE), glbi.reshape(SUBLANES, N_GROUPS, N_STATE),
        gbbr, gbbi, gctr, gcti)
    g_w_in, (others_ffn, slots_mix) = _in_weight_grad(
        x, dproj, comms=[_send_comm(halves_ffn), _scatter_comm(chip_mix)])
    halves_mix = sum_chips(chip_mix, slots_mix)
    dx, _ = _input_grad(dr1, dproj, w_in_st)

    g_conv = jnp.pad(g_conv8[:3].reshape(3, N_SHARDS, 128).transpose(1, 0, 2), ((0, 0), (0, SUBLANES - 3), (0, 0)))
    pieces = [dus, dbias[:, SSM_W:], g_lr, g_li, g_ldt, g_brt, g_bit, g_cre, g_cim, g_d, g_glu_b,
              g_ln1_g, g_ln1_b, g_ln2_g, g_ln2_b, loss]
    flat = jnp.concatenate([p.reshape(-1) for p in pieces])
    g_packed = jnp.pad(flat, (0, PACKED_ROWS * 128 - flat.shape[0])).reshape(PACKED_ROWS, 128)
    ((got_w, got_conv, got_packed),) = _standalone([_swap_comm([g_w_in], [g_conv, g_packed])], "swap_with_sibling")
    (chip_w,) = add_halves([g_w_in], [got_w])
    chip_conv, chip_packed = _small_pair_sums([g_conv, g_packed], [got_conv, got_packed])
    ((slots_w, slots_conv, slots_packed),) = _standalone(
        [_scatter_comm([chip_w, chip_conv], [chip_packed])], "scatter_to_chips")
    (halves_w,) = sum_chips([chip_w], [slots_w])
    conv_total, packed_mine = _small_totals(me_arr, c_arr, chip_conv, slots_conv, chip_packed, slots_packed)
    (others_rest,) = _standalone([_send_comm([halves_w] + halves_mix + [packed_mine])], "send_to_sibling")
    packed_other = others_rest[-1]
    south = c_arr[0] == 0
    packed_total = jnp.concatenate([jnp.where(south, packed_mine, packed_other),
                                    jnp.where(south, packed_other, packed_mine)])

    pairs = dict(zip(FFN_W, zip(halves_ffn, others_ffn)))
    pairs.update(zip(("w_in",) + MIXER_W, zip([halves_w] + halves_mix, others_rest[:-1])))
    return dx, pairs, conv_total, packed_total


PACKED_ROWS = 1136
PACKED_LAYOUT = (("b_in", IN_COLS), ("ssm_lambda_re", STATE_COLS), ("ssm_lambda_im", STATE_COLS),
                 ("ssm_log_dt", N_GROUPS), ("ssm_b_re", SSM_W * N_STATE), ("ssm_b_im", SSM_W * N_STATE),
                 ("ssm_c_re", SSM_W * N_STATE), ("ssm_c_im", SSM_W * N_STATE), ("ssm_d", SSM_W), ("glu_b", SSM_W),
                 ("ln1_g", D_MODEL), ("ln1_b", D_MODEL), ("ln2_g", D_MODEL), ("ln2_b", D_MODEL), ("loss", 1))


def _unpack_small(packed):
    flat = packed.reshape(-1)
    out, off = {}, 0
    for name, size in PACKED_LAYOUT:
        out[name] = flat[off:off + size]
        off += size
    for name in ("ssm_b_re", "ssm_b_im"):
        out[name] = out[name].reshape(N_GROUPS, GROUP_C, N_STATE).transpose(0, 2, 1)[None]
    for name in ("ssm_c_re", "ssm_c_im"):
        out[name] = out[name].reshape(1, N_GROUPS, GROUP_C, N_STATE)
    for name in ("ssm_lambda_re", "ssm_lambda_im"):
        out[name] = out[name].reshape(1, N_GROUPS, N_STATE)
    for name in ("b_in", "ssm_log_dt", "ssm_d", "glu_b", "ln1_g", "ln1_b", "ln2_g", "ln2_b"):
        out[name] = out[name][None]
    return out


BIG = ("w_in", "glu_w", "w_ssm_out", "w_conv_out", "w_o", "w_gate", "w_up", "w_down")
SMALL = ("b_in", "ssm_lambda_re", "ssm_lambda_im", "ssm_log_dt", "ssm_b_re", "ssm_b_im", "ssm_c_re", "ssm_c_im",
         "ssm_d", "glu_b", "ln1_g", "ln1_b", "ln2_g", "ln2_b")
WEIGHTS = ("w_in", "b_in", "ssm_lambda_re", "ssm_lambda_im", "ssm_log_dt", "ssm_b_re", "ssm_b_im", "ssm_c_re",
           "ssm_c_im", "ssm_d", "glu_w", "glu_b", "w_ssm_out", "conv_w", "w_conv_out", "w_o", "ln1_g", "ln1_b",
           "w_gate", "w_up", "w_down", "ln2_g", "ln2_b")


def _place():
    x, y, c = lax.axis_index("x"), lax.axis_index("y"), lax.axis_index("c")
    chips = [(1 - x, y), (x, 1 - y), (1 - x, 1 - y)]
    return x, y, c, chips


def _shard_of(chip):
    return 2 * chip[0] + chip[1]


def _remote(src, dst, send_sem, recv_sem, to):
    return pltpu.make_async_remote_copy(src_ref=src, dst_ref=dst, send_sem=send_sem, recv_sem=recv_sem,
                                        device_id=to, device_id_type=MESH)


def _half_rows(shard, which):
    r2 = shard.shape[0] // 2
    return pl.ds(pl.multiple_of(which * r2, 16), r2)


def _gather_ici(halved, whole=()):
    shards = list(halved) + list(whole)
    nh = len(halved)

    def copies(src, dst, sems):
        send_sem, recv_sem = sems[:2]
        x, y, c, chips = _place()
        me = _shard_of((x, y))
        out = []
        for a in range(len(shards)):
            for k, chip in enumerate(chips):
                if a < nh:
                    rows = _half_rows(shards[a], c)
                    out.append(_remote(src[a].at[rows], dst[a].at[me, rows], send_sem.at[a, k], recv_sem.at[a, k],
                                       (*chip, c)))
                else:
                    out.append(_remote(src[a], dst[a].at[me], send_sem.at[a, k], recv_sem.at[a, k], (*chip, c)))
        return out

    n = len(shards)

    def bounce(src, dst, sems):
        local_sem, buffers = sems[2], sems[3:]
        me = _shard_of((lax.axis_index("x"), lax.axis_index("y")))
        return [(pltpu.make_async_copy(src[a], buffers[a], local_sem.at[a, 0]),
                 pltpu.make_async_copy(buffers[a], dst[a].at[me], local_sem.at[a, 1])) for a in range(n)]

    scratch = [pltpu.SemaphoreType.DMA((n, 3))] * 2 + [pltpu.SemaphoreType.DMA((n, 2))]
    scratch += [pltpu.VMEM(s.shape, s.dtype) for s in shards]
    return _Comm(shards, [jax.ShapeDtypeStruct((N_SHARDS,) + s.shape, s.dtype) for s in shards], scratch, copies,
                 bounce=bounce)


def _gather_d2d(stacks, shards):
    def copies(src, dst, sems):
        del src
        send_sem, recv_sem = sems
        x, y, c, chips = _place()
        out = []
        for a in range(len(stacks)):
            for k, chip in enumerate(chips):
                rows = dst[a].at[_shard_of(chip), _half_rows(shards[a], c)]
                out.append(_remote(rows, rows, send_sem.at[a, k], recv_sem.at[a, k], (x, y, 1 - c)))
        return out

    n = len(stacks)
    return _Comm(stacks, [jax.ShapeDtypeStruct(s.shape, s.dtype) for s in stacks],
                 [pltpu.SemaphoreType.DMA((n, 3))] * 2, copies, aliased=True)


def _standalone(comms, name):
    return _launch(None, comms, name=name, grid=(), in_specs=[], out_specs=(), out_shape=())()[1]


def _gather_weights(shards):
    n = len(shards)

    def body(*refs):
        src, dst = refs[:n], refs[n:2 * n]
        send_sem, recv_sem, fsend_sem, frecv_sem, local_sem = refs[2 * n:2 * n + 5]
        buffers = refs[2 * n + 5:]
        x, y, c, chips = _place()
        me = _shard_of((x, y))
        sibling = (x, y, 1 - c)
        relay_from = (x ^ (1 - c), y ^ c)
        relay_to = (x ^ c, y ^ (1 - c))
        own = [(pltpu.make_async_copy(src[a], buffers[a], local_sem.at[a, 0]),
                pltpu.make_async_copy(buffers[a], dst[a].at[me], local_sem.at[a, 1])) for a in range(n)]
        for to_vmem, _ in own:
            to_vmem.start()
        sends = []
        for a in range(n):
            mine = _half_rows(shards[a], c)
            for k in range(2):
                cp = _remote(src[a].at[mine], dst[a].at[me, mine], send_sem.at[a, k], recv_sem.at[a, k],
                             (*chips[k], c))
                cp.start()
                sends.append(cp)
        for to_vmem, to_hbm in own:
            to_vmem.wait()
            to_hbm.start()
        for a in range(n):
            mine = _half_rows(shards[a], c)
            for k in range(2):
                rows = dst[a].at[_shard_of(chips[k]), mine]
                _remote(rows, rows, send_sem.at[a, k], recv_sem.at[a, k], sibling).wait_recv()
            passed = dst[a].at[_shard_of(relay_from), mine]
            cp = _remote(passed, passed, send_sem.at[a, 2], recv_sem.at[a, 2], (*relay_to, c))
            cp.start()
            sends.append(cp)
        for a in range(n):
            for k, chip in enumerate(chips):
                rows = dst[a].at[_shard_of(chip), _half_rows(shards[a], c)]
                if k == 2:
                    _remote(rows, rows, send_sem.at[a, k], recv_sem.at[a, k], sibling).wait_recv()
                cp = _remote(rows, rows, fsend_sem.at[a, k], frecv_sem.at[a, k], sibling)
                cp.start()
                sends.append(cp)
        for a in range(n):
            for k, chip in enumerate(chips):
                rows = dst[a].at[_shard_of(chip), _half_rows(shards[a], 1 - c)]
                _remote(rows, rows, fsend_sem.at[a, k], frecv_sem.at[a, k], sibling).wait_recv()
        for cp in sends:
            cp.wait_send()
        for _, to_hbm in own:
            to_hbm.wait()

    return pl.pallas_call(
        body, name="gather_weights",
        out_shape=tuple(jax.ShapeDtypeStruct((N_SHARDS,) + s.shape, s.dtype) for s in shards),
        in_specs=[ANY] * n, out_specs=(ANY,) * n,
        scratch_shapes=[pltpu.SemaphoreType.DMA((n, 3))] * 4 + [pltpu.SemaphoreType.DMA((n, 2))]
        + [pltpu.VMEM(s.shape, s.dtype) for s in shards],
    )(*shards)


def _swap_comm(big, small=()):
    nb, n = len(big), len(big) + len(small)
    arrays = list(big) + list(small)

    def copies(src, dst, sems):
        send_sem, recv_sem = sems
        x, y, c, _ = _place()
        out = []
        for a in range(n):
            if a < nb:
                r2 = arrays[a].shape[1] // 2
                part = src[a].at[:, pl.ds(pl.multiple_of((1 - c) * r2, SUBLANES), r2), :]
            else:
                part = src[a]
            out.append(_remote(part, dst[a], send_sem.at[a], recv_sem.at[a], (x, y, 1 - c)))
        return out

    out_shape = [jax.ShapeDtypeStruct((N_SHARDS, g.shape[1] // 2, g.shape[2]), g.dtype) for g in big]
    out_shape += [jax.ShapeDtypeStruct(g.shape, g.dtype) for g in small]
    return _Comm(arrays, out_shape, [pltpu.SemaphoreType.DMA((n,))] * 2, copies)


def _scatter_comm(slabbed, small=()):
    ns, n = len(slabbed), len(slabbed) + len(small)
    arrays = list(slabbed) + list(small)

    def copies(src, dst, sems):
        send_sem, recv_sem = sems
        _, _, c, chips = _place()
        out = []
        for a in range(n):
            for k, chip in enumerate(chips):
                if a < ns:
                    part = src[a].at[_shard_of(chip)]
                else:
                    h = arrays[a].shape[0] // 2
                    part = src[a].at[pl.ds(pl.multiple_of(c * h, SUBLANES), h), :]
                out.append(_remote(part, dst[a].at[k], send_sem.at[a, k], recv_sem.at[a, k], (*chip, c)))
        return out

    out_shape = [jax.ShapeDtypeStruct((3,) + g.shape[1:], g.dtype) for g in slabbed]
    out_shape += [jax.ShapeDtypeStruct((3, g.shape[0] // 2, g.shape[1]), g.dtype) for g in small]
    return _Comm(arrays, out_shape, [pltpu.SemaphoreType.DMA((n, 3))] * 2, copies)


def _send_comm(arrays):
    n = len(arrays)

    def copies(src, dst, sems):
        send_sem, recv_sem = sems
        x, y, c, _ = _place()
        return [_remote(src[a], dst[a], send_sem.at[a], recv_sem.at[a], (x, y, 1 - c)) for a in range(n)]

    return _Comm(arrays, [jax.ShapeDtypeStruct(h.shape, h.dtype) for h in arrays],
                 [pltpu.SemaphoreType.DMA((n,))] * 2, copies)


def _row_chunk(rows):
    for cand in (256, 176, 128, 64):
        if rows % cand == 0:
            return cand
    return rows


def _per_shape(fn, *lists):
    groups = {}
    for i, items in enumerate(zip(*lists)):
        groups.setdefault(tuple(a.shape for a in items), []).append(i)
    out = [None] * len(lists[0])
    for idx in groups.values():
        for i, r in zip(idx, fn(*([lst[i] for i in idx] for lst in lists))):
            out[i] = r
    return out


def _add_own_half(stacks, receiveds, c):
    n = len(stacks)
    _, r2, cols = receiveds[0].shape

    def body(c_ref, *refs):
        del c_ref
        for a in range(n):
            refs[2 * n + a][...] = (refs[a][...] + refs[n + a][...]).astype(BF16)

    own = pl.BlockSpec((1, r2, cols), lambda s, c_ref: (s, c_ref[0], 0))
    got = pl.BlockSpec((1, r2, cols), lambda s, c_ref: (s, 0, 0))
    return pl.pallas_call(
        body, name="add_own_half",
        grid_spec=pltpu.PrefetchScalarGridSpec(
            num_scalar_prefetch=1, grid=(N_SHARDS,), in_specs=[own] * n + [got] * n, out_specs=(got,) * n),
        out_shape=(jax.ShapeDtypeStruct(receiveds[0].shape, BF16),) * n,
        compiler_params=_params(("parallel",)),
    )(c, *stacks, *receiveds)


def _chip_order_sum(me, own, s0, s1, s2):
    terms = []
    for s in range(N_SHARDS):
        d = jnp.bitwise_xor(me, s)
        terms.append(jnp.where(d == 0, own, jnp.where(d == 2, s0, jnp.where(d == 1, s1, s2))))
    return ((terms[0] + terms[1]) + terms[2]) + terms[3]


def _sum_chips(own_stacks, slots, me):
    n = len(slots)
    _, rows, cols = slots[0].shape
    rc = _row_chunk(rows)

    def body(me_ref, *refs):
        del me_ref
        for a in range(n):
            own_r, s_r = refs[a], refs[n + a]
            refs[2 * n + a][...] = (((own_r[0].astype(F32) + s_r[0].astype(F32)) + s_r[1].astype(F32))
                                    + s_r[2].astype(F32))

    own = pl.BlockSpec((1, rc, cols), lambda i, me_ref: (me_ref[0], i, 0))
    three = pl.BlockSpec((3, rc, cols), lambda i, me_ref: (0, i, 0))
    total = pl.BlockSpec((rc, cols), lambda i, me_ref: (i, 0))
    return pl.pallas_call(
        body, name="sum_chips",
        grid_spec=pltpu.PrefetchScalarGridSpec(
            num_scalar_prefetch=1, grid=(rows // rc,), in_specs=[own] * n + [three] * n, out_specs=(total,) * n),
        out_shape=(jax.ShapeDtypeStruct((rows, cols), F32),) * n,
        compiler_params=_params(("parallel",)),
    )(me, *own_stacks, *slots)


def _small_pair_sums(mine, theirs):
    n = len(mine)

    def body(*refs):
        for a in range(n):
            refs[2 * n + a][...] = refs[a][...] + refs[n + a][...]

    vm = pl.BlockSpec(memory_space=pltpu.VMEM)
    return pl.pallas_call(
        body, name="small_pair_sums", out_shape=tuple(jax.ShapeDtypeStruct(g.shape, g.dtype) for g in mine),
        in_specs=[vm] * (2 * n), out_specs=(vm,) * n,
        compiler_params=pltpu.CompilerParams(vmem_limit_bytes=VMEM_LIMIT),
    )(*mine, *theirs)


def _adam_math(w, g, m, v):
    m = ADAM_B1 * m + (1.0 - ADAM_B1) * g
    v = ADAM_B2 * v + (1.0 - ADAM_B2) * (g * g)
    m_hat = m / (1.0 - ADAM_B1 ** ADAM_STEP)
    v_hat = v / (1.0 - ADAM_B2 ** ADAM_STEP)
    delta = -ADAM_LR * (m_hat / (jnp.sqrt(v_hat) + ADAM_EPS) + ADAM_WD * w)
    return delta, m, v


def _small_totals(me, c, conv_stack, conv_slots, packed, packed_slots):
    half = packed.shape[0] // 2

    def body(me_ref, c_ref, cs_r, cslot_r, p_r, pslot_r, conv_r, tot_r):
        me_ = me_ref[0]
        conv_r[...] = _chip_order_sum(me_, cs_r[me_], cslot_r[0], cslot_r[1], cslot_r[2])
        own = p_r[pl.ds(pl.multiple_of(c_ref[0] * half, SUBLANES), half), :]
        tot_r[...] = _chip_order_sum(me_, own, pslot_r[0], pslot_r[1], pslot_r[2])

    vm = pl.BlockSpec(memory_space=pltpu.VMEM)
    sm = pl.BlockSpec(memory_space=pltpu.SMEM)
    return pl.pallas_call(
        body, name="small_totals",
        out_shape=(jax.ShapeDtypeStruct(conv_stack.shape[1:], F32), jax.ShapeDtypeStruct((half, packed.shape[1]), F32)),
        in_specs=[sm, sm] + [vm] * 4, out_specs=(vm, vm),
    )(me, c, conv_stack, conv_slots, packed, packed_slots)


def _adam_small(gs, ws, ms, vs):
    n = len(gs)

    def body(*refs):
        for a in range(n):
            g_r, w_r, m_r, v_r = (refs[i * n + a] for i in range(4))
            d_r, nm_r, nv_r = (refs[(4 + i) * n + a] for i in range(3))
            d_r[...], nm_r[...], nv_r[...] = _adam_math(w_r[...], g_r[...], m_r[...], v_r[...])

    vm = pl.BlockSpec(memory_space=pltpu.VMEM)
    shapes = tuple(jax.ShapeDtypeStruct(w.shape, F32) for w in ws)
    out = pl.pallas_call(
        body, name="adam_small", out_shape=shapes * 3, in_specs=[vm] * (4 * n), out_specs=(vm,) * (3 * n),
        compiler_params=pltpu.CompilerParams(vmem_limit_bytes=VMEM_LIMIT),
    )(*gs, *ws, *ms, *vs)
    return out[:n], out[n:2 * n], out[2 * n:]


def _adam_big(ws, mines, others, ms, vs, c):
    n = len(ws)
    r2, cols = mines[0].shape
    rc = _row_chunk(r2)
    nch = r2 // rc

    def body(c_ref, *refs):
        mine_is_here = pl.program_id(0) == c_ref[0]
        for a in range(n):
            w_r, mine_r, other_r, m_r, v_r = (refs[i * n + a] for i in range(5))
            g_r, d_r, nm_r, nv_r = (refs[(5 + i) * n + a] for i in range(4))
            g = jnp.where(mine_is_here, mine_r[...], other_r[...])
            g_r[...] = g
            d_r[...], nm_r[...], nv_r[...] = _adam_math(w_r[...], g, m_r[...], v_r[...])

    full = pl.BlockSpec((rc, cols), lambda h, i, c_ref: (h * nch + i, 0))
    mine = pl.BlockSpec((rc, cols), lambda h, i, c_ref: (jnp.where(h == c_ref[0], i, 0), 0))
    other = pl.BlockSpec((rc, cols), lambda h, i, c_ref: (jnp.where(h == c_ref[0], 0, i), 0))
    shape = jax.ShapeDtypeStruct((2 * r2, cols), F32)
    out = pl.pallas_call(
        body, name="adam_big",
        grid_spec=pltpu.PrefetchScalarGridSpec(
            num_scalar_prefetch=1, grid=(2, nch),
            in_specs=[full] * n + [mine] * n + [other] * n + [full] * (2 * n), out_specs=(full,) * (4 * n)),
        out_shape=(shape,) * (4 * n), compiler_params=_params(("parallel", "parallel")),
    )(c, *ws, *mines, *others, *ms, *vs)
    return [tuple(out[i * n + a] for i in range(4)) for a in range(n)]


def kernel(x, w_in, b_in, ssm_lambda_re, ssm_lambda_im, ssm_log_dt, ssm_b_re, ssm_b_im, ssm_c_re, ssm_c_im, ssm_d, glu_w, glu_b, w_ssm_out, conv_w, w_conv_out, w_o, ln1_g, ln1_b, w_gate, w_up, w_down, ln2_g, ln2_b, loss_target, m_w_in, m_b_in, m_ssm_lambda_re, m_ssm_lambda_im, m_ssm_log_dt, m_ssm_b_re, m_ssm_b_im, m_ssm_c_re, m_ssm_c_im, m_ssm_d, m_glu_w, m_glu_b, m_w_ssm_out, m_conv_w, m_w_conv_out, m_w_o, m_ln1_g, m_ln1_b, m_w_gate, m_w_up, m_w_down, m_ln2_g, m_ln2_b, v_w_in, v_b_in, v_ssm_lambda_re, v_ssm_lambda_im, v_ssm_log_dt, v_ssm_b_re, v_ssm_b_im, v_ssm_c_re, v_ssm_c_im, v_ssm_d, v_glu_w, v_glu_b, v_w_ssm_out, v_conv_w, v_w_conv_out, v_w_o, v_ln1_g, v_ln1_b, v_w_gate, v_w_up, v_w_down, v_ln2_g, v_ln2_b):
    given = dict(locals())
    w = {n: given[n] for n in WEIGHTS}
    m = {n: given["m_" + n] for n in WEIGHTS}
    v = {n: given["v_" + n] for n in WEIGHTS}

    flip = lambda n, a: a.T if n in ("w_gate", "w_up") else a
    shards = {n: flip(n, w[n][0]).astype(BF16) for n in BIG}
    shards["conv_w"] = jnp.pad(conv_w[0], ((0, SUBLANES - 3), (0, 0)))
    c_arr = jnp.reshape(lax.axis_index("c"), (1,)).astype(jnp.int32)
    me = _shard_of((lax.axis_index("x"), lax.axis_index("y")))
    me_arr = jnp.reshape(me, (1,)).astype(jnp.int32)
    dx, pairs, conv_total, packed_total = _device_step(
        x[0], loss_target[0], {n: w[n] for n in SMALL}, shards, c_arr, me_arr)

    grad = _unpack_small(packed_total)
    loss_total = grad.pop("loss")[0]
    grad["conv_w"] = conv_total[:3][None]
    small_names = ("conv_w",) + SMALL
    swap = lambda n, a: a.transpose(0, 1, 3, 2) if n in ("ssm_b_re", "ssm_b_im") else a
    ds, nms, nvs = _adam_small(*([swap(n, d[n]) for n in small_names] for d in (grad, w, m, v)))
    delta, new_m, new_v = {}, {}, {}
    for i, n in enumerate(small_names):
        delta[n], new_m[n], new_v[n] = swap(n, ds[i]), swap(n, nms[i]), swap(n, nvs[i])
    updated = _per_shape(
        lambda *a: _adam_big(*a, c_arr),
        [flip(n, w[n][0]) for n in BIG], [pairs[n][0] for n in BIG], [pairs[n][1] for n in BIG],
        [flip(n, m[n][0]) for n in BIG], [flip(n, v[n][0]) for n in BIG])
    for n, results in zip(BIG, updated):
        grad[n], delta[n], new_m[n], new_v[n] = (flip(n, r)[None] for r in results)

    return (loss_total, dx[None], *[grad[n] for n in WEIGHTS], *[delta[n] for n in WEIGHTS],
            *[new_m[n] for n in WEIGHTS], *[new_v[n] for n in WEIGHTS])
```

```python
import functools
import math

import jax
import jax.numpy as jnp
from jax import lax
from jax.experimental import pallas as pl
from jax.experimental.pallas import tpu as pltpu

F32 = jnp.float32
BF16 = jnp.bfloat16

D_MODEL = 1024
IN_COLS = 4096
SSM_W = 512
N_GROUPS = 32
N_STATE = 64
GROUP_C = 16
STATE_COLS = N_GROUPS * N_STATE
STRIP = 512
N_STRIPS = STATE_COLS // STRIP
FFN_SHARD = 704
N_SHARDS = 4
ALPHA = 2.0 ** 0.25
LN_EPS = 1e-5
GELU_K = math.sqrt(2.0 / math.pi)
GELU_C = 0.044715

ADAM_LR = 0.001
ADAM_B1 = 0.9
ADAM_B2 = 0.999
ADAM_EPS = 1e-08
ADAM_WD = 0.01
ADAM_STEP = 10

V7X_VMEM_BYTES = 64 * 1024 * 1024
VMEM_LIMIT = V7X_VMEM_BYTES - 8 * 1024 * 1024
SUBLANES = 8
N_POWERS = 128

MESH = pl.DeviceIdType.MESH
ANY = pl.BlockSpec(memory_space=pl.ANY)


def _dot(a, b):
    return jnp.dot(a.astype(BF16), b.astype(BF16), preferred_element_type=F32)


def _dot_t(a, b):
    return lax.dot_general(a.astype(BF16), b.astype(BF16), (((1,), (1,)), ((), ())),
                           preferred_element_type=F32)


def _tdot(a, b):
    return lax.dot_general(a.astype(BF16), b.astype(BF16), (((0,), (0,)), ((), ())),
                           preferred_element_type=F32)


def _sigmoid(v):
    return 1.0 / (1.0 + jnp.exp(-v))


def _split3(v):
    hi = v.astype(BF16)
    r1 = v - hi.astype(F32)
    mid = r1.astype(BF16)
    lo = (r1 - mid.astype(F32)).astype(BF16)
    return hi, mid, lo


def _const(shape):
    nd = len(shape)
    return pl.BlockSpec(shape, lambda *_: (0,) * nd)


def _params(sem, vmem=VMEM_LIMIT):
    return pltpu.CompilerParams(dimension_semantics=sem, vmem_limit_bytes=vmem)


def _gelu_parts(v):
    inner = GELU_K * (v + GELU_C * v * v * v)
    t = jnp.tanh(inner)
    g = 0.5 * v * (1.0 + t)
    dg = 0.5 * (1.0 + t) + 0.5 * v * (1.0 - t * t) * GELU_K * (1.0 + 3.0 * GELU_C * v * v)
    return g, dg


class _Comm:
    def __init__(self, inputs, out_shape, sems, copies, aliased=False, bounce=None):
        self.inputs, self.out_shape, self.sems = list(inputs), tuple(out_shape), list(sems)
        self.copies, self.aliased, self.bounce = copies, aliased, bounce


def _launch(body, comms, *, name, grid, in_specs, out_specs, out_shape, scratch_shapes=(), aliases=None, sem=None):
    comms = list(comms)
    n_in, n_out, n_scr = len(in_specs), len(out_specs), len(scratch_shapes)
    aliases = dict(aliases or {})
    layout = []
    p_in, p_out, p_sem = n_in, n_out, 0
    for cm in comms:
        layout.append((p_in, p_out, p_sem))
        if cm.aliased:
            for i in range(len(cm.inputs)):
                aliases[p_in + i] = p_out + i
        p_in, p_out, p_sem = p_in + len(cm.inputs), p_out + len(cm.out_shape), p_sem + len(cm.sems)
    tot_in, tot_out = p_in, p_out

    def fused(*refs):
        ins, outs = refs[:tot_in], refs[tot_in:tot_in + tot_out]
        scr = refs[tot_in + tot_out:tot_in + tot_out + n_scr]
        sems = refs[tot_in + tot_out + n_scr:]

        def descriptors(kind):
            out = []
            for cm, (a, b, s) in zip(comms, layout):
                make = cm.copies if kind == "remote" else cm.bounce
                if make is not None:
                    out += make(ins[a:a + len(cm.inputs)], outs[b:b + len(cm.out_shape)], sems[s:s + len(cm.sems)])
            return out

        steps = [pl.program_id(d) for d in range(len(grid))]
        first = functools.reduce(jnp.logical_and, [s == 0 for s in steps]) if grid else None
        last = functools.reduce(jnp.logical_and, [s == g - 1 for s, g in zip(steps, grid)]) if grid else None

        def start():
            for cp in descriptors("remote"):
                cp.start()
            for to_vmem, _ in descriptors("local"):
                to_vmem.start()

        def finish():
            for to_vmem, to_hbm in descriptors("local"):
                to_vmem.wait()
                to_hbm.start()
            for cp in descriptors("remote"):
                cp.wait()
            for _, to_hbm in descriptors("local"):
                to_hbm.wait()

        if comms:
            pl.when(first)(start) if grid else start()
        if body is not None:
            body(*ins[:n_in], *outs[:n_out], *scr)
        if comms:
            pl.when(last)(finish) if grid else finish()

    specs_in = list(in_specs) + [ANY] * (tot_in - n_in)
    specs_out = tuple(out_specs) + (ANY,) * (tot_out - n_out)
    shapes = tuple(out_shape) + tuple(s for cm in comms for s in cm.out_shape)
    scratch = list(scratch_shapes) + [s for cm in comms for s in cm.sems]
    if comms or sem is None:
        sem = ("arbitrary",) * len(grid)
    kwargs = dict(grid=grid) if grid else {}
    call = pl.pallas_call(fused, name=name, out_shape=shapes, in_specs=specs_in, out_specs=specs_out,
                          scratch_shapes=scratch, input_output_aliases=aliases,
                          compiler_params=_params(sem) if grid else None, **kwargs)

    def run(*args):
        out = call(*args, *(a for cm in comms for a in cm.inputs))
        results, rest = out[:n_out], out[n_out:]
        per_comm = []
        for cm in comms:
            per_comm.append(rest[:len(cm.out_shape)])
            rest = rest[len(cm.out_shape):]
        return results, per_comm

    return run


def _ssm_discretise(lr, li, ldt, lr16, li16, ldt16, brt, bit):
    def lam_bar(lr_, li_, ldt_):
        dt = jnp.exp(ldt_)
        mag = jnp.exp(lr_ * dt)
        return mag * jnp.cos(li_ * dt), mag * jnp.sin(li_ * dt)

    lb_re, lb_im = lam_bar(lr, li, ldt)
    l16_re, l16_im = lam_bar(lr16, li16, ldt16)
    den = lr16 * lr16 + li16 * li16
    num_re = l16_re - 1.0
    fr = (num_re * lr16 + l16_im * li16) / den
    fi = (l16_im * lr16 - num_re * li16) / den
    bb_re = fr * brt - fi * bit
    bb_im = fr * bit + fi * brt
    return lb_re, lb_im, bb_re, bb_im


def _strip_selectors():
    p = lax.broadcasted_iota(jnp.int32, (N_STATE, STRIP), 0)
    col = lax.broadcasted_iota(jnp.int32, (N_STATE, STRIP), 1)
    rep = ((col & (N_STATE - 1)) == p).astype(BF16)
    row = lax.broadcasted_iota(jnp.int32, (SSM_W, STRIP), 0)
    col2 = lax.broadcasted_iota(jnp.int32, (SSM_W, STRIP), 1)
    mask = (((row >> 4) & 7) == (col2 >> 6))
    return rep, mask


def _ssm_prepare(lr, li, ldt, lr16, li16, ldt16, brt, bit, cre, cim):
    def body(lr_r, li_r, ldt_r, lr16_r, li16_r, ldt16_r, brt_r, bit_r, cre_r, cim_r,
             pwr_r, pwi_r, bbr_r, bbi_r, ctr_r, cti_r):
        lb_re, lb_im, bb_re, bb_im = _ssm_discretise(
            lr_r[...], li_r[...], ldt_r[...], lr16_r[...], li16_r[...], ldt16_r[...], brt_r[...], bit_r[...])
        pr, pi_ = lb_re, lb_im
        pwr_r[0] = pr
        pwi_r[0] = pi_
        for k in range(1, N_POWERS):
            pr, pi_ = pr * lb_re - pi_ * lb_im, pr * lb_im + pi_ * lb_re
            pwr_r[k] = pr
            pwi_r[k] = pi_
        rep, mask = _strip_selectors()
        for src, dst in ((bb_re, bbr_r), (bb_im, bbi_r), (cre_r[...], ctr_r), (cim_r[...], cti_r)):
            wide = jnp.dot(src.astype(BF16), rep, preferred_element_type=F32)
            dst[...] = jnp.where(mask, wide, 0.0).astype(BF16)

    vm = pl.BlockSpec(memory_space=pltpu.VMEM)
    return pl.pallas_call(
        body, name="ssm_prepare",
        out_shape=(jax.ShapeDtypeStruct((N_POWERS, N_GROUPS, N_STATE), F32),) * 2
        + (jax.ShapeDtypeStruct((SSM_W, STRIP), BF16),) * 4,
        in_specs=[vm] * 10, out_specs=(vm,) * 6,
    )(lr, li, ldt, lr16, li16, ldt16, brt, bit, cre, cim)


def _scan_tables(pwr, pwi):
    pr = pwr.reshape(N_POWERS, STATE_COLS)
    pi_ = pwi.reshape(N_POWERS, STATE_COLS)
    rows8 = lambda v: jnp.broadcast_to(v[None], (SUBLANES, STATE_COLS))
    tab_a = jnp.stack([rows8(pr[0]), rows8(pi_[0]), rows8(pr[-1]), rows8(pi_[-1])])
    return tab_a, jnp.stack([pr, pi_])


def _ssm_param_grads(lr, li, ldt, lr16, li16, ldt16, brt, bit, dlbr, dlbi, dbbr, dbbi, dctr, dcti):
    def body(lr_r, li_r, ldt_r, lr16_r, li16_r, ldt16_r, brt_r, bit_r,
             dlbr_r, dlbi_r, dbbr_r, dbbi_r, dctr_r, dcti_r,
             glr_r, gli_r, gldt_r, gbrt_r, gbit_r, gcre_r, gcim_r):
        rep, mask = _strip_selectors()

        def fold(acc):
            return sum(lax.dot_general(t, rep, (((1,), (1,)), ((), ())), preferred_element_type=F32)
                       for t in _split3(jnp.where(mask, acc, 0.0)))

        g_lb_re = jnp.sum(dlbr_r[...], axis=0)
        g_lb_im = jnp.sum(dlbi_r[...], axis=0)
        g_bb_re = fold(dbbr_r[...])
        g_bb_im = fold(dbbi_r[...])
        gcre_r[...] = fold(dctr_r[...])
        gcim_r[...] = fold(dcti_r[...])
        prim = (lr_r[...], li_r[...], ldt_r[...], lr16_r[...], li16_r[...], ldt16_r[...], brt_r[...], bit_r[...])
        _, vjp = jax.vjp(_ssm_discretise, *prim)
        g_lr, g_li, g_ldt, g_lr16, g_li16, g_ldt16, g_brt, g_bit = vjp((g_lb_re, g_lb_im, g_bb_re, g_bb_im))
        grp = lax.broadcasted_iota(jnp.int32, (N_GROUPS, SSM_W), 0)
        rw = lax.broadcasted_iota(jnp.int32, (N_GROUPS, SSM_W), 1)
        gsum = ((rw >> 4) == grp).astype(BF16)

        def group_sum(v):
            return sum(jnp.dot(gsum, t, preferred_element_type=F32) for t in _split3(v))

        glr_r[...] = g_lr + group_sum(g_lr16)
        gli_r[...] = g_li + group_sum(g_li16)
        gldt_r[...] = g_ldt + jnp.sum(group_sum(g_ldt16), axis=1, keepdims=True)
        gbrt_r[...] = g_brt
        gbit_r[...] = g_bit

    vm = pl.BlockSpec(memory_space=pltpu.VMEM)
    gp = jax.ShapeDtypeStruct((N_GROUPS, N_STATE), F32)
    gb = jax.ShapeDtypeStruct((SSM_W, N_STATE), F32)
    return pl.pallas_call(
        body, name="ssm_param_grads",
        out_shape=(gp, gp, jax.ShapeDtypeStruct((N_GROUPS, 1), F32), gb, gb, gb, gb),
        in_specs=[vm] * 14, out_specs=(vm,) * 7,
    )(lr, li, ldt, lr16, li16, ldt16, brt, bit, dlbr, dlbi, dbbr, dbbi, dctr, dcti)


def _in_proj(x, w_in_st, b_in, comms=()):
    t = x.shape[0]
    tm = 512

    def body(x_r, w_r, b_r, o_r, xb_r):
        xb = x_r[...].astype(BF16)
        xb_r[...] = xb
        for j in range(N_SHARDS):
            cols = slice(D_MODEL * j, D_MODEL * (j + 1))
            o_r[:, cols] = jnp.dot(xb, w_r[j], preferred_element_type=F32) + b_r[:, cols]

    (proj, xb), sent = _launch(
        body, comms, name="in_proj", grid=(t // tm,),
        out_shape=(jax.ShapeDtypeStruct((t, IN_COLS), F32), jax.ShapeDtypeStruct((t, D_MODEL), BF16)),
        in_specs=[pl.BlockSpec((tm, D_MODEL), lambda i: (i, 0)), _const((N_SHARDS, D_MODEL, D_MODEL)),
                  _const((1, IN_COLS))],
        out_specs=(pl.BlockSpec((tm, IN_COLS), lambda i: (i, 0)), pl.BlockSpec((tm, D_MODEL), lambda i: (i, 0))),
        sem=("parallel",),
    )(x, w_in_st, b_in)
    return proj, xb, sent


def _cmul_add(xr, xi, mr, mi, sr, si):
    return xr + (mr * sr - mi * si), xi + (mr * si + mi * sr)


SCAN_STEPS = N_POWERS
SCAN_CHUNK = SUBLANES * SCAN_STEPS


def _interleave(src_r, dst_r):
    for step in range(SCAN_STEPS):
        dst_r[SUBLANES * step:SUBLANES * (step + 1), :] = src_r[pl.ds(step, SUBLANES, stride=SCAN_STEPS), :]


def _deinterleave(src_r, dst_r):
    for step in range(SCAN_STEPS):
        dst_r[pl.ds(step, SUBLANES, stride=SCAN_STEPS), :] = src_r[SUBLANES * step:SUBLANES * (step + 1), :]


def _step_rows(step):
    return pl.ds(pl.multiple_of(step * SUBLANES, SUBLANES), SUBLANES)


def _scan_steps(body, init, by=4, powers=None):
    if powers is not None:
        by = SUBLANES

    def trip(t, carry):
        if powers is not None:
            rows_re, rows_im = powers[0, _step_rows(t), :], powers[1, _step_rows(t), :]
        for u in range(by):
            if powers is None:
                carry = body(t * by + u, carry)
            else:
                carry = body(t * by + u, carry, jnp.broadcast_to(rows_re[u:u + 1, :], rows_re.shape),
                             jnp.broadcast_to(rows_im[u:u + 1, :], rows_im.shape))
        return carry

    return lax.fori_loop(0, SCAN_STEPS // by, trip, init)


def _segment_states(first_r, first_i, ends_r, ends_i, a64_r, a64_i, order):
    row = lax.broadcasted_iota(jnp.int32, ends_r.shape, 0)
    cur_r, cur_i = first_r, first_i
    ent_r = jnp.zeros_like(ends_r)
    ent_i = jnp.zeros_like(ends_i)
    for s in order:
        ent_r = jnp.where(row == s, jnp.broadcast_to(cur_r, ends_r.shape), ent_r)
        ent_i = jnp.where(row == s, jnp.broadcast_to(cur_i, ends_i.shape), ent_i)
        cur_r, cur_i = _cmul_add(ends_r[s:s + 1, :], ends_i[s:s + 1, :], a64_r, a64_i, cur_r, cur_i)
    return ent_r, ent_i, cur_r, cur_i


def _ssm_forward(proj, bbr, bbi, ctr, cti, d_skip, tab_a, tab_p, comms=(), tc=SCAN_CHUNK):
    t = proj.shape[0]

    def body(u_r, bbr_r, bbi_r, ctr_r, cti_r, d_r, ta_r, tp_r, xsr_r, xsi_r, y_r, ui_s, yi_s, car_r, car_i):
        @pl.when(pl.program_id(1) == 0)
        def _():
            car_r[...] = jnp.zeros_like(car_r)
            car_i[...] = jnp.zeros_like(car_i)

        _interleave(u_r, ui_s)
        u = ui_s[...]
        xsr_r[...] = _dot(u, bbr_r[...])
        xsi_r[...] = _dot(u, bbi_r[...])
        a_r, a_i = ta_r[0], ta_r[1]

        def local(step, carry):
            rows = _step_rows(step)
            xr, xi = _cmul_add(xsr_r[rows, :], xsi_r[rows, :], a_r, a_i, *carry)
            xsr_r[rows, :] = xr
            xsi_r[rows, :] = xi
            return xr, xi

        zero = jnp.zeros((SUBLANES, STRIP), F32)
        ends_r, ends_i = _scan_steps(local, (zero, zero))
        ent_r, ent_i, out_r, out_i = _segment_states(
            car_r[...], car_i[...], ends_r, ends_i, ta_r[2, 0:1, :], ta_r[3, 0:1, :], range(SUBLANES))
        car_r[...] = out_r
        car_i[...] = out_i

        def entering(step, _, power_r, power_i):
            rows = _step_rows(step)
            xr, xi = _cmul_add(xsr_r[rows, :], xsi_r[rows, :], power_r, power_i, ent_r, ent_i)
            xsr_r[rows, :] = xr
            xsi_r[rows, :] = xi
            return 0

        _scan_steps(entering, 0, powers=tp_r)
        yi_s[...] = _dot_t(xsr_r[...], ctr_r[...]) - _dot_t(xsi_r[...], cti_r[...]) + d_r[...] * u
        _deinterleave(yi_s, y_r)

    strip_mat = pl.BlockSpec((128, STRIP), lambda j, k: (j, 0))
    states = pl.BlockSpec((tc, STRIP), lambda j, k: (k, j))
    return _launch(
        body, comms, name="ssm_forward", grid=(N_STRIPS, t // tc),
        out_shape=(jax.ShapeDtypeStruct((t, STATE_COLS), F32), jax.ShapeDtypeStruct((t, STATE_COLS), F32),
                   jax.ShapeDtypeStruct((t, SSM_W), F32)),
        in_specs=[pl.BlockSpec((tc, 128), lambda j, k: (k, j)),
                  strip_mat, strip_mat, strip_mat, strip_mat,
                  pl.BlockSpec((1, 128), lambda j, k: (0, j)),
                  pl.BlockSpec((4, SUBLANES, STRIP), lambda j, k: (0, 0, j)),
                  pl.BlockSpec((2, SCAN_STEPS, STRIP), lambda j, k: (0, 0, j))],
        out_specs=(states, states, pl.BlockSpec((tc, 128), lambda j, k: (k, j))),
        scratch_shapes=[pltpu.VMEM((tc, 128), F32), pltpu.VMEM((tc, 128), F32),
                        pltpu.VMEM((1, STRIP), F32), pltpu.VMEM((1, STRIP), F32)],
        sem=("parallel", "arbitrary"),
    )(proj, bbr, bbi, ctr, cti, d_skip, tab_a, tab_p)


def _shift_down(v, prev, n):
    row = lax.broadcasted_iota(jnp.int32, v.shape, 0)
    out = pltpu.roll(v, n, 0)
    for r in range(n):
        src = prev[SUBLANES - n + r:SUBLANES - n + r + 1, :]
        out = jnp.where(row == r, jnp.broadcast_to(src, v.shape), out)
    return out


def _shift_up(v, nxt, n):
    rows = v.shape[0]
    row = lax.broadcasted_iota(jnp.int32, v.shape, 0)
    out = pltpu.roll(v, rows - n, 0)
    for r in range(n):
        src = nxt[r:r + 1, :]
        out = jnp.where(row == rows - n + r, jnp.broadcast_to(src, v.shape), out)
    return out


def _conv3(q, q_prev, w):
    return w[2:3, :] * q + w[1:2, :] * _shift_down(q, q_prev, 1) + w[0:1, :] * _shift_down(q, q_prev, 2)


def _mixer_forward(x, proj, ya0, glu_w, glu_b, wso_st, conv_w8, wco_st, w_o, comms=(), tm=256):
    t = x.shape[0]
    hb = tm // SUBLANES

    def body(x_r, ya0_r, h_r, cg_r, bg_r, ga_r, gb_r, hp_r, cgp_r,
             glu_w_r, glu_b_r, wso_r, cw_r, wco_r, wo_r, xh_r, rstd_r, ya_r, yb_r):
        i = pl.program_id(0)
        g, _ = _gelu_parts(ya0_r[...])
        ya1 = g * _sigmoid(_dot(g, glu_w_r[...]) + glu_b_r[...])
        q = cg_r[...] * h_r[...]
        q_prev = jnp.where(i > 0, cgp_r[...] * hp_r[...], 0.0)
        yb0 = bg_r[...] * _conv3(q, q_prev, cw_r[...])
        for j in range(N_SHARDS):
            ya_r[:, 256 * j:256 * (j + 1)] = _dot(ya1, wso_r[j])
            yb_r[:, 256 * j:256 * (j + 1)] = _dot(yb0, wco_r[j])
        merged = _sigmoid(ga_r[...]) * ya_r[...] + _sigmoid(gb_r[...]) * yb_r[...]
        r1 = ALPHA * x_r[...] + _dot(merged, wo_r[...])
        mu = jnp.mean(r1, axis=-1, keepdims=True)
        cen = r1 - mu
        rstd = lax.rsqrt(jnp.mean(cen * cen, axis=-1, keepdims=True) + LN_EPS)
        xh_r[...] = cen * rstd
        rstd_r[...] = rstd

    def col(w, c):
        return pl.BlockSpec((tm, w), lambda i: (i, c))

    def prev(c):
        return pl.BlockSpec((SUBLANES, SSM_W), lambda i: (jnp.maximum(i * hb - 1, 0), c))

    return _launch(
        body, comms, name="mixer_forward", grid=(t // tm,),
        out_shape=(jax.ShapeDtypeStruct((t, D_MODEL), F32), jax.ShapeDtypeStruct((t, 1), F32),
                   jax.ShapeDtypeStruct((t, D_MODEL), F32), jax.ShapeDtypeStruct((t, D_MODEL), F32)),
        in_specs=[col(D_MODEL, 0), col(SSM_W, 0), col(SSM_W, 1), col(SSM_W, 2), col(SSM_W, 3),
                  col(D_MODEL, 2), col(D_MODEL, 3), prev(1), prev(2),
                  _const((SSM_W, SSM_W)), _const((1, SSM_W)), _const((N_SHARDS, SSM_W, 256)),
                  _const((SUBLANES, SSM_W)), _const((N_SHARDS, SSM_W, 256)), _const((D_MODEL, D_MODEL))],
        out_specs=(col(D_MODEL, 0), pl.BlockSpec((tm, 1), lambda i: (i, 0)), col(D_MODEL, 0), col(D_MODEL, 0)),
        sem=("parallel",),
    )(x, ya0, proj, proj, proj, proj, proj, proj, proj, glu_w, glu_b, wso_st, conv_w8, wco_st, w_o)


def _layer_norm_bwd(dxhat, xhat, rstd):
    m1 = jnp.mean(dxhat, axis=-1, keepdims=True)
    m2 = jnp.mean(dxhat * xhat, axis=-1, keepdims=True)
    return rstd * (dxhat - m1 - xhat * m2)


def _ffn_step(xhat1, rstd1, target, ln1_g, ln1_b, ln2_g, ln2_b, wg_st, wu_st, wd_st, tm=256):
    t = xhat1.shape[0]

    def body(xh_r, rstd_r, tgt_r, g1_r, b1_r, g2_r, b2_r, wg_r, wu_r, wd_r,
             loss_r, dr1_r, x1b_r, dr2b_r, hid_r, dhg_r, dhu_r, dg2_r, db2_r, dg1_r, db1_r,
             hg_s, hu_s):
        @pl.when(pl.program_id(0) == 0)
        def _():
            for r in (loss_r, dg2_r, db2_r, dg1_r, db1_r):
                r[...] = jnp.zeros_like(r)

        xhat1_v = xh_r[...]
        x1 = xhat1_v * g1_r[...] + b1_r[...]
        x1b = x1.astype(BF16)
        x1b_r[...] = x1b
        ffn = jnp.zeros((tm, D_MODEL), F32)
        for j in range(N_SHARDS):
            hg = lax.dot_general(x1b, wg_r[j], (((1,), (1,)), ((), ())), preferred_element_type=F32)
            hu = lax.dot_general(x1b, wu_r[j], (((1,), (1,)), ((), ())), preferred_element_type=F32)
            hg_s[j] = hg
            hu_s[j] = hu
            hid = (hg * _sigmoid(hg) * hu).astype(BF16)
            hid_r[j] = hid
            ffn = ffn + jnp.dot(hid, wd_r[j], preferred_element_type=F32)
        r2 = ALPHA * x1 + ffn
        mu = jnp.mean(r2, axis=-1, keepdims=True)
        cen = r2 - mu
        rstd2 = lax.rsqrt(jnp.mean(cen * cen, axis=-1, keepdims=True) + LN_EPS)
        xhat2 = cen * rstd2
        diff = (xhat2 * g2_r[...] + b2_r[...]) - tgt_r[...]
        loss_r[...] += 0.5 * jnp.sum(jnp.mean(diff * diff, axis=-1, keepdims=True), axis=0, keepdims=True)
        dy = diff * (1.0 / D_MODEL)
        dg2_r[...] += jnp.sum(dy * xhat2, axis=0, keepdims=True)
        db2_r[...] += jnp.sum(dy, axis=0, keepdims=True)
        dr2 = _layer_norm_bwd(dy * g2_r[...], xhat2, rstd2)
        dr2b = dr2.astype(BF16)
        dr2b_r[...] = dr2b
        dx1 = ALPHA * dr2
        for j in range(N_SHARDS):
            dhid = lax.dot_general(dr2b, wd_r[j], (((1,), (1,)), ((), ())), preferred_element_type=F32)
            hg = hg_s[j]
            hu = hu_s[j]
            sg = _sigmoid(hg)
            dhu = (dhid * (hg * sg)).astype(BF16)
            dhg = (dhid * hu * (sg * (1.0 + hg * (1.0 - sg)))).astype(BF16)
            dhg_r[j] = dhg
            dhu_r[j] = dhu
            dx1 = dx1 + jnp.dot(dhg, wg_r[j], preferred_element_type=F32)
            dx1 = dx1 + jnp.dot(dhu, wu_r[j], preferred_element_type=F32)
        dg1_r[...] += jnp.sum(dx1 * xhat1_v, axis=0, keepdims=True)
        db1_r[...] += jnp.sum(dx1, axis=0, keepdims=True)
        dr1_r[...] = _layer_norm_bwd(dx1 * g1_r[...], xhat1_v, rstd_r[...])

    tile = pl.BlockSpec((tm, D_MODEL), lambda i: (i, 0))
    hidden = pl.BlockSpec((N_SHARDS, tm, FFN_SHARD), lambda i: (0, i, 0))
    vec = _const((1, D_MODEL))
    hid_shape = jax.ShapeDtypeStruct((N_SHARDS, t, FFN_SHARD), BF16)
    vec_shape = jax.ShapeDtypeStruct((1, D_MODEL), F32)
    return pl.pallas_call(
        body, name="ffn_step", grid=(t // tm,),
        out_shape=(jax.ShapeDtypeStruct((1, 1), F32), jax.ShapeDtypeStruct((t, D_MODEL), F32),
                   jax.ShapeDtypeStruct((t, D_MODEL), BF16), jax.ShapeDtypeStruct((t, D_MODEL), BF16),
                   hid_shape, hid_shape, hid_shape, vec_shape, vec_shape, vec_shape, vec_shape),
        in_specs=[tile, pl.BlockSpec((tm, 1), lambda i: (i, 0)), tile, vec, vec, vec, vec,
                  _const((N_SHARDS, FFN_SHARD, D_MODEL)), _const((N_SHARDS, FFN_SHARD, D_MODEL)),
                  _const((N_SHARDS, FFN_SHARD, D_MODEL))],
        out_specs=(_const((1, 1)), tile, tile, tile, hidden, hidden, hidden, vec, vec, vec, vec),
        scratch_shapes=[pltpu.VMEM((N_SHARDS, tm, FFN_SHARD), F32), pltpu.VMEM((N_SHARDS, tm, FFN_SHARD), F32)],
        compiler_params=_params(("arbitrary",)),
    )(xhat1, rstd1, target, ln1_g, ln1_b, ln2_g, ln2_b, wg_st, wu_st, wd_st)


def _ffn_weight_grads(x1b, dr2b, hid, dhg, dhu, tk=2048):
    t = x1b.shape[0]

    def body(x_r, dr_r, hid_r, dhg_r, dhu_r, gwg_r, gwu_r, gwd_r):
        @pl.when(pl.program_id(1) == 0)
        def _():
            for r in (gwg_r, gwu_r, gwd_r):
                r[...] = jnp.zeros_like(r)

        gwg_r[0] += _tdot(dhg_r[0], x_r[...])
        gwu_r[0] += _tdot(dhu_r[0], x_r[...])
        gwd_r[0] += _tdot(hid_r[0], dr_r[...])

    tile = pl.BlockSpec((tk, D_MODEL), lambda j, k: (k, 0))
    hidden = pl.BlockSpec((1, tk, FFN_SHARD), lambda j, k: (j, k, 0))
    row = pl.BlockSpec((1, FFN_SHARD, D_MODEL), lambda j, k: (j, 0, 0))
    return pl.pallas_call(
        body, name="ffn_weight_grads", grid=(N_SHARDS, t // tk),
        out_shape=(jax.ShapeDtypeStruct((N_SHARDS, FFN_SHARD, D_MODEL), F32),) * 3,
        in_specs=[tile, tile, hidden, hidden, hidden],
        out_specs=(row, row, row),
        compiler_params=_params(("parallel", "arbitrary")),
    )(x1b, dr2b, hid, dhg, dhu)


def _mixer_backward(dr1, proj, ya0, ya, yb, glu_w, glu_b, wso_st, conv_w8, wco_st, w_o, comms=(), tm=256):
    t = dr1.shape[0]
    hb = tm // SUBLANES
    last_block = t // SUBLANES - 1

    def body(dr1_r, dr1n_r, ya0_r, ya_r, yb_r, h_r, cg_r, bg_r, ga_r, gb_r, hp_r, cgp_r, bgn_r, gbn_r,
             glu_w_r, glu_b_r, wso_r, cw_r, wco_r, wo_r,
             dya0_r, dproj_r, dbias_r, gwo_r, gwso_r, gwco_r, gglu_w_r, gglu_b_r, gconv_r):
        i = pl.program_id(0)

        @pl.when(i == 0)
        def _():
            for r in (dbias_r, gwo_r, gwso_r, gwco_r, gglu_w_r, gglu_b_r, gconv_r):
                r[...] = jnp.zeros_like(r)

        dr1_v = dr1_r[...]
        dmerged = _dot_t(dr1_v, wo_r[...])
        sa = _sigmoid(ga_r[...])
        sb = _sigmoid(gb_r[...])
        ya_v = ya_r[...]
        yb_v = yb_r[...]
        gwo_r[...] += _tdot(sa * ya_v + sb * yb_v, dr1_v)
        dya = dmerged * sa
        dyb = dmerged * sb
        dga = dmerged * ya_v * (sa * (1.0 - sa))
        dgb = dmerged * yb_v * (sb * (1.0 - sb))

        g, gelu_grad = _gelu_parts(ya0_r[...])
        s1 = _sigmoid(_dot(g, glu_w_r[...]) + glu_b_r[...])
        ya1 = g * s1
        dya1 = jnp.zeros((tm, SSM_W), F32)
        for j in range(N_SHARDS):
            dya_j = dya[:, 256 * j:256 * (j + 1)]
            gwso_r[j] += _tdot(ya1, dya_j)
            dya1 = dya1 + _dot_t(dya_j, wso_r[j])
        dz1 = dya1 * g * (s1 * (1.0 - s1))
        gglu_b_r[...] += jnp.sum(dz1, axis=0, keepdims=True)
        gglu_w_r[...] += _tdot(g, dz1)
        dya0_r[...] = (dya1 * s1 + _dot_t(dz1, glu_w_r[...])) * gelu_grad

        cw = cw_r[...]
        h = h_r[...]
        cg = cg_r[...]
        bg = bg_r[...]
        q = cg * h
        q_prev = jnp.where(i > 0, cgp_r[...] * hp_r[...], 0.0)
        q1 = _shift_down(q, q_prev, 1)
        q2 = _shift_down(q, q_prev, 2)
        z = cw[2:3, :] * q + cw[1:2, :] * q1 + cw[0:1, :] * q2
        yb0 = bg * z
        dyb0 = jnp.zeros((tm, SSM_W), F32)
        for j in range(N_SHARDS):
            dyb_j = dyb[:, 256 * j:256 * (j + 1)]
            gwco_r[j] += _tdot(yb0, dyb_j)
            dyb0 = dyb0 + _dot_t(dyb_j, wco_r[j])
        dbg = dyb0 * z
        dz = dyb0 * bg
        dyb_n = _dot_t(dr1n_r[...], wo_r[...]) * _sigmoid(gbn_r[...])
        dyb0_n = jnp.zeros((SUBLANES, SSM_W), F32)
        for j in range(N_SHARDS):
            dyb0_n = dyb0_n + _dot_t(dyb_n[:, 256 * j:256 * (j + 1)], wco_r[j])
        dz_next = jnp.where(i < pl.num_programs(0) - 1, dyb0_n * bgn_r[...], 0.0)
        dq = cw[2:3, :] * dz + cw[1:2, :] * _shift_up(dz, dz_next, 1) + cw[0:1, :] * _shift_up(dz, dz_next, 2)
        gconv_r[0:1, :] += jnp.sum(dz * q2, axis=0, keepdims=True)
        gconv_r[1:2, :] += jnp.sum(dz * q1, axis=0, keepdims=True)
        gconv_r[2:3, :] += jnp.sum(dz * q, axis=0, keepdims=True)
        dh = dq * cg
        dcg = dq * h

        dproj_r[:, 0:512] = jnp.zeros((tm, SSM_W), BF16)
        pieces = ((512, dh), (1024, dcg), (1536, dbg), (2048, dga), (3072, dgb))
        for off, val in pieces:
            w = val.shape[1]
            dproj_r[:, off:off + w] = val.astype(BF16)
            dbias_r[:, off:off + w] += jnp.sum(val, axis=0, keepdims=True)

    def col(w, c):
        return pl.BlockSpec((tm, w), lambda i: (i, c))

    def prev(c):
        return pl.BlockSpec((SUBLANES, SSM_W), lambda i: (jnp.maximum(i * hb - 1, 0), c))

    def nxt(w, c):
        return pl.BlockSpec((SUBLANES, w), lambda i: (jnp.minimum((i + 1) * hb, last_block), c))

    sh = jax.ShapeDtypeStruct
    return _launch(
        body, comms, name="mixer_backward", grid=(t // tm,),
        out_shape=(sh((t, SSM_W), F32), sh((t, IN_COLS), BF16), sh((1, IN_COLS), F32),
                   sh((D_MODEL, D_MODEL), F32), sh((N_SHARDS, SSM_W, 256), F32), sh((N_SHARDS, SSM_W, 256), F32),
                   sh((SSM_W, SSM_W), F32), sh((1, SSM_W), F32), sh((SUBLANES, SSM_W), F32)),
        in_specs=[col(D_MODEL, 0), nxt(D_MODEL, 0), col(SSM_W, 0), col(D_MODEL, 0), col(D_MODEL, 0),
                  col(SSM_W, 1), col(SSM_W, 2), col(SSM_W, 3), col(D_MODEL, 2), col(D_MODEL, 3),
                  prev(1), prev(2), nxt(SSM_W, 3), nxt(D_MODEL, 3),
                  _const((SSM_W, SSM_W)), _const((1, SSM_W)), _const((N_SHARDS, SSM_W, 256)),
                  _const((SUBLANES, SSM_W)), _const((N_SHARDS, SSM_W, 256)), _const((D_MODEL, D_MODEL))],
        out_specs=(col(SSM_W, 0), col(IN_COLS, 0), _const((1, IN_COLS)),
                   _const((D_MODEL, D_MODEL)), _const((N_SHARDS, SSM_W, 256)), _const((N_SHARDS, SSM_W, 256)),
                   _const((SSM_W, SSM_W)), _const((1, SSM_W)), _const((SUBLANES, SSM_W))),
        sem=("arbitrary",),
    )(dr1, dr1, ya0, ya, yb, proj, proj, proj, proj, proj, proj, proj, proj, proj,
      glu_w, glu_b, wso_st, conv_w8, wco_st, w_o)


def _cmulc_add(xr, xi, mr, mi, sr, si):
    return xr + (mr * sr + mi * si), xi + (mr * si - mi * sr)


def _ssm_backward(dya0, proj, xsr, xsi, bbr, bbi, ctr, cti, d_skip, tab_a, tab_p, dproj, comms=(), tc=SCAN_CHUNK):
    t = proj.shape[0]
    nk = t // tc

    def body(dy_r, u_r, xsr_r, xsi_r, bbr_r, bbi_r, ctr_r, cti_r, d_r, ta_r, tp_r, dproj_any,
             du_r, dus_r, gbbr_r, gbbi_r, gctr_r, gcti_r, glbr_r, glbi_r, gd_r,
             gr_s, gi_s, dyi_s, ui_s, dui_s, dun_s, car_r, car_i):
        del dproj_any

        @pl.when(pl.program_id(1) == 0)
        def _():
            for r in (car_r, car_i, dus_r, gbbr_r, gbbi_r, gctr_r, gcti_r, glbr_r, glbi_r, gd_r):
                r[...] = jnp.zeros_like(r)

        _interleave(dy_r, dyi_s)
        _interleave(u_r, ui_s)
        dy = dyi_s[...]
        u = ui_s[...]
        gr_s[...] = _dot(dy, ctr_r[...])
        gi_s[...] = -_dot(dy, cti_r[...])
        a_r, a_i = ta_r[0], ta_r[1]

        def local(n, carry):
            rows = _step_rows(SCAN_STEPS - 1 - n)
            gr, gi = _cmulc_add(gr_s[rows, :], gi_s[rows, :], a_r, a_i, *carry)
            gr_s[rows, :] = gr
            gi_s[rows, :] = gi
            return gr, gi

        zero = jnp.zeros((SUBLANES, STRIP), F32)
        ends_r, ends_i = _scan_steps(local, (zero, zero))
        ent_r, ent_i, out_r, out_i = _segment_states(
            car_r[...], car_i[...], ends_r, ends_i, ta_r[2, 0:1, :], -ta_r[3, 0:1, :], range(SUBLANES - 1, -1, -1))
        car_r[...] = out_r
        car_i[...] = out_i

        def entering(n, carry, power_r, power_i):
            gnr, gni, ar, ai = carry
            rows = _step_rows(SCAN_STEPS - 1 - n)
            gr, gi = _cmulc_add(gr_s[rows, :], gi_s[rows, :], power_r, power_i, ent_r, ent_i)
            gr_s[rows, :] = gr
            gi_s[rows, :] = gi
            xr = xsr_r[rows, :]
            xi = xsi_r[rows, :]
            return gr, gi, ar + (xr * gnr + xi * gni), ai + (xr * gni - xi * gnr)

        _, _, ar, ai = _scan_steps(entering, (ent_r, ent_i, zero, zero), powers=tp_r)
        glbr_r[...] += ar
        glbi_r[...] += ai
        gr = gr_s[...]
        gi = gi_s[...]
        dui_s[...] = _dot_t(gr, bbr_r[...]) + _dot_t(gi, bbi_r[...]) + d_r[...] * dy
        _deinterleave(dui_s, dun_s)
        du = dun_s[...]
        du_r[...] = du.astype(BF16)
        dus_r[...] += jnp.sum(du, axis=0, keepdims=True)
        gd_r[...] += jnp.sum(dy * u, axis=0, keepdims=True)
        gbbr_r[...] += _tdot(u, gr)
        gbbi_r[...] += _tdot(u, gi)
        gctr_r[...] += _tdot(dy, xsr_r[...])
        gcti_r[...] -= _tdot(dy, xsi_r[...])

    def rev(w):
        return pl.BlockSpec((tc, w), lambda j, k: (nk - 1 - k, j))

    strip_mat = pl.BlockSpec((128, STRIP), lambda j, k: (j, 0))
    vec = pl.BlockSpec((1, 128), lambda j, k: (0, j))
    lbacc = pl.BlockSpec((SUBLANES, STRIP), lambda j, k: (0, j))
    sh = jax.ShapeDtypeStruct
    return _launch(
        body, comms, name="ssm_backward", grid=(N_STRIPS, nk),
        out_shape=(sh((t, IN_COLS), BF16), sh((1, SSM_W), F32),
                   sh((SSM_W, STRIP), F32), sh((SSM_W, STRIP), F32), sh((SSM_W, STRIP), F32), sh((SSM_W, STRIP), F32),
                   sh((SUBLANES, STATE_COLS), F32), sh((SUBLANES, STATE_COLS), F32), sh((1, SSM_W), F32)),
        in_specs=[rev(128), rev(128), rev(STRIP), rev(STRIP),
                  strip_mat, strip_mat, strip_mat, strip_mat, vec,
                  pl.BlockSpec((4, SUBLANES, STRIP), lambda j, k: (0, 0, j)),
                  pl.BlockSpec((2, SCAN_STEPS, STRIP), lambda j, k: (0, 0, j)), ANY],
        out_specs=(rev(128), vec, strip_mat, strip_mat, strip_mat, strip_mat, lbacc, lbacc, vec),
        scratch_shapes=[pltpu.VMEM((tc, STRIP), F32), pltpu.VMEM((tc, STRIP), F32)]
        + [pltpu.VMEM((tc, 128), F32)] * 4 + [pltpu.VMEM((1, STRIP), F32)] * 2,
        aliases={11: 0}, sem=("parallel", "arbitrary"),
    )(dya0, proj, xsr, xsi, bbr, bbi, ctr, cti, d_skip, tab_a, tab_p, dproj)


def _input_grad(dr1, dproj, w_in_st, comms=(), tm=512):
    t = dr1.shape[0]

    def body(dr1_r, dp_r, w_r, dx_r):
        acc = ALPHA * dr1_r[...]
        for j in range(N_SHARDS):
            acc = acc + lax.dot_general(dp_r[:, D_MODEL * j:D_MODEL * (j + 1)], w_r[j],
                                        (((1,), (1,)), ((), ())), preferred_element_type=F32)
        dx_r[...] = acc

    (dx,), sent = _launch(
        body, comms, name="input_grad", grid=(t // tm,),
        out_shape=(jax.ShapeDtypeStruct((t, D_MODEL), F32),),
        in_specs=[pl.BlockSpec((tm, D_MODEL), lambda i: (i, 0)), pl.BlockSpec((tm, IN_COLS), lambda i: (i, 0)),
                  _const((N_SHARDS, D_MODEL, D_MODEL))],
        out_specs=(pl.BlockSpec((tm, D_MODEL), lambda i: (i, 0)),),
        sem=("parallel",),
    )(dr1, dproj, w_in_st)
    return dx, sent


def _in_weight_grad(x, dproj, comms=(), tk=2048):
    t = x.shape[0]

    def body(x_r, dp_r, gw_r):
        @pl.when(pl.program_id(1) == 0)
        def _():
            gw_r[...] = jnp.zeros_like(gw_r)

        gw_r[0] += _tdot(x_r[...], dp_r[...])

    (g_w_in,), sent = _launch(
        body, comms, name="in_weight_grad", grid=(N_SHARDS, t // tk),
        out_shape=(jax.ShapeDtypeStruct((N_SHARDS, D_MODEL, D_MODEL), F32),),
        in_specs=[pl.BlockSpec((tk, D_MODEL), lambda j, k: (k, 0)), pl.BlockSpec((tk, D_MODEL), lambda j, k: (k, j))],
        out_specs=(pl.BlockSpec((1, D_MODEL, D_MODEL), lambda j, k: (j, 0, 0)),),
        sem=("parallel", "arbitrary"),
    )(x, dproj)
    return g_w_in, sent


MIXER_W = ("glu_w", "w_ssm_out", "w_conv_out", "w_o")
FFN_W = ("w_gate", "w_up", "w_down")


def _device_step(x, target, small, shards, c_arr, me_arr):
    lr, li = small["ssm_lambda_re"][0], small["ssm_lambda_im"][0]
    ldt = small["ssm_log_dt"][0][:, None]
    rep16 = lambda a: jnp.broadcast_to(a[:, None, :], (N_GROUPS, GROUP_C, a.shape[-1])).reshape(SSM_W, a.shape[-1])
    lr16, li16 = rep16(lr), rep16(li)
    ldt16 = rep16(jnp.broadcast_to(ldt, (N_GROUPS, N_STATE)))
    brt = small["ssm_b_re"][0].transpose(0, 2, 1).reshape(SSM_W, N_STATE)
    bit = small["ssm_b_im"][0].transpose(0, 2, 1).reshape(SSM_W, N_STATE)
    cre = small["ssm_c_re"][0].reshape(SSM_W, N_STATE)
    cim = small["ssm_c_im"][0].reshape(SSM_W, N_STATE)
    disc = (lr, li, ldt, lr16, li16, ldt16, brt, bit)

    pwr, pwi, bbr, bbi, ctr, cti = _ssm_prepare(*disc, cre, cim)
    tab_a, tab_p = _scan_tables(pwr, pwi)

    first_sh = [shards[n] for n in MIXER_W + FFN_W[:1]]
    second_sh = [shards[n] for n in FFN_W[1:]]
    (w_in_st,) = _gather_weights([shards["w_in"]])
    proj, xb, (arrived,) = _in_proj(x, w_in_st, small["b_in"], comms=[_gather_ici(first_sh, [shards["conv_w"]])])
    (xsr, xsi, ya0), (second_part, first_st) = _ssm_forward(
        proj, bbr, bbi, ctr, cti, small["ssm_d"], tab_a, tab_p,
        comms=[_gather_ici(second_sh), _gather_d2d(arrived[:len(first_sh)], first_sh)])
    glu_st, wso_st, wco_st, wo_st, wg_st = first_st
    conv_st = arrived[len(first_sh)]
    conv_w8 = jnp.pad(conv_st[:, :3, :].transpose(1, 0, 2).reshape(3, SSM_W), ((0, SUBLANES - 3), (0, 0)))
    w_o = wo_st.reshape(D_MODEL, D_MODEL)
    glu_w = glu_st.reshape(SSM_W, SSM_W)
    (xhat1, rstd1, ya, yb), (second_st,) = _mixer_forward(
        x, proj, ya0, glu_w, small["glu_b"], wso_st, conv_w8, wco_st, w_o, comms=[_gather_d2d(second_part, second_sh)])
    wu_st, wd_st = second_st
    (loss, dr1, x1b, dr2b, hid, dhg, dhu, g_ln2_g, g_ln2_b, g_ln1_g, g_ln1_b) = _ffn_step(
        xhat1, rstd1, target, small["ln1_g"], small["ln1_b"], small["ln2_g"], small["ln2_b"], wg_st, wu_st, wd_st)

    add_halves = lambda gs, rs: _per_shape(lambda a, b: _add_own_half(a, b, c_arr), list(gs), list(rs))
    sum_chips = lambda owns, slots: _per_shape(lambda a, b: _sum_chips(a, b, me_arr), list(owns), list(slots))
    g_ffn = _ffn_weight_grads(x1b, dr2b, hid, dhg, dhu)
    (dya0, dproj, dbias, g_wo, g_wso, g_wco, g_glu_w, g_glu_b, g_conv8), (got_ffn,) = _mixer_backward(
        dr1, proj, ya0, ya, yb, glu_w, small["glu_b"], wso_st, conv_w8, wco_st, w_o, comms=[_swap_comm(g_ffn)])
    chip_ffn = add_halves(g_ffn, got_ffn)
    g_mix = [g_glu_w.reshape(N_SHARDS, 128, SSM_W), g_wso, g_wco, g_wo.reshape(N_SHARDS, 256, D_MODEL)]
    (dproj, dus, gbbr, gbbi, gctr, gcti, glbr, glbi, g_d), (slots_ffn, got_mix) = _ssm_backward(
        dya0, proj, xsr, xsi, bbr, bbi, ctr, cti, small["ssm_d"], tab_a, tab_p, dproj,
        comms=[_scatter_comm(chip_ffn), _swap_comm(g_mix)])
    halves_ffn = sum_chips(chip_ffn, slots_ffn)
    chip_mix = add_halves(g_mix, got_mix)
    g_lr, g_li, g_ldt, g_brt, g_bit, g_cre, g_cim = _ssm_param_grads(
        *disc, glbr.reshape(SUBLANES, N_GROUPS, N_STATE), glbi.reshape(SUBLANES, N_GROUPS, N_STATE),
        gbbr, gbbi, gctr, gcti)
    g_w_in, (others_ffn, slots_mix) = _in_weight_grad(
        xb, dproj, comms=[_send_comm(halves_ffn), _scatter_comm(chip_mix)])
    halves_mix = sum_chips(chip_mix, slots_mix)
    dx, _ = _input_grad(dr1, dproj, w_in_st)

    g_conv = jnp.pad(g_conv8[:3].reshape(3, N_SHARDS, 128).transpose(1, 0, 2), ((0, 0), (0, SUBLANES - 3), (0, 0)))
    pieces = [dus, dbias[:, SSM_W:], g_lr, g_li, g_ldt, g_brt, g_bit, g_cre, g_cim, g_d, g_glu_b,
              g_ln1_g, g_ln1_b, g_ln2_g, g_ln2_b, loss]
    flat = jnp.concatenate([p.reshape(-1) for p in pieces])
    g_packed = jnp.pad(flat, (0, PACKED_ROWS * 128 - flat.shape[0])).reshape(PACKED_ROWS, 128)
    ((got_w, got_conv, got_packed),) = _standalone([_swap_comm([g_w_in], [g_conv, g_packed])], "swap_with_sibling")
    (chip_w,) = add_halves([g_w_in], [got_w])
    chip_conv, chip_packed = _small_pair_sums([g_conv, g_packed], [got_conv, got_packed])
    ((slots_w, slots_conv, slots_packed),) = _standalone(
        [_scatter_comm([chip_w, chip_conv], [chip_packed])], "scatter_to_chips")
    (halves_w,) = sum_chips([chip_w], [slots_w])
    conv_total, packed_mine = _small_totals(me_arr, c_arr, chip_conv, slots_conv, chip_packed, slots_packed)
    (others_rest,) = _standalone([_send_comm([halves_w] + halves_mix + [packed_mine])], "send_to_sibling")
    packed_other = others_rest[-1]
    south = c_arr[0] == 0
    packed_total = jnp.concatenate([jnp.where(south, packed_mine, packed_other),
                                    jnp.where(south, packed_other, packed_mine)])

    pairs = dict(zip(FFN_W, zip(halves_ffn, others_ffn)))
    pairs.update(zip(("w_in",) + MIXER_W, zip([halves_w] + halves_mix, others_rest[:-1])))
    return dx, pairs, conv_total, packed_total


PACKED_ROWS = 1136
PACKED_LAYOUT = (("b_in", IN_COLS), ("ssm_lambda_re", STATE_COLS), ("ssm_lambda_im", STATE_COLS),
                 ("ssm_log_dt", N_GROUPS), ("ssm_b_re", SSM_W * N_STATE), ("ssm_b_im", SSM_W * N_STATE),
                 ("ssm_c_re", SSM_W * N_STATE), ("ssm_c_im", SSM_W * N_STATE), ("ssm_d", SSM_W), ("glu_b", SSM_W),
                 ("ln1_g", D_MODEL), ("ln1_b", D_MODEL), ("ln2_g", D_MODEL), ("ln2_b", D_MODEL), ("loss", 1))


def _unpack_small(packed):
    flat = packed.reshape(-1)
    out, off = {}, 0
    for name, size in PACKED_LAYOUT:
        out[name] = flat[off:off + size]
        off += size
    for name in ("ssm_b_re", "ssm_b_im"):
        out[name] = out[name].reshape(N_GROUPS, GROUP_C, N_STATE).transpose(0, 2, 1)[None]
    for name in ("ssm_c_re", "ssm_c_im"):
        out[name] = out[name].reshape(1, N_GROUPS, GROUP_C, N_STATE)
    for name in ("ssm_lambda_re", "ssm_lambda_im"):
        out[name] = out[name].reshape(1, N_GROUPS, N_STATE)
    for name in ("b_in", "ssm_log_dt", "ssm_d", "glu_b", "ln1_g", "ln1_b", "ln2_g", "ln2_b"):
        out[name] = out[name][None]
    return out


BIG = ("w_in", "glu_w", "w_ssm_out", "w_conv_out", "w_o", "w_gate", "w_up", "w_down")
SMALL = ("b_in", "ssm_lambda_re", "ssm_lambda_im", "ssm_log_dt", "ssm_b_re", "ssm_b_im", "ssm_c_re", "ssm_c_im",
         "ssm_d", "glu_b", "ln1_g", "ln1_b", "ln2_g", "ln2_b")
WEIGHTS = ("w_in", "b_in", "ssm_lambda_re", "ssm_lambda_im", "ssm_log_dt", "ssm_b_re", "ssm_b_im", "ssm_c_re",
           "ssm_c_im", "ssm_d", "glu_w", "glu_b", "w_ssm_out", "conv_w", "w_conv_out", "w_o", "ln1_g", "ln1_b",
           "w_gate", "w_up", "w_down", "ln2_g", "ln2_b")


def _place():
    x, y, c = lax.axis_index("x"), lax.axis_index("y"), lax.axis_index("c")
    chips = [(1 - x, y), (x, 1 - y), (1 - x, 1 - y)]
    return x, y, c, chips


def _shard_of(chip):
    return 2 * chip[0] + chip[1]


def _remote(src, dst, send_sem, recv_sem, to):
    return pltpu.make_async_remote_copy(src_ref=src, dst_ref=dst, send_sem=send_sem, recv_sem=recv_sem,
                                        device_id=to, device_id_type=MESH)


def _half_rows(shard, which):
    r2 = shard.shape[0] // 2
    return pl.ds(pl.multiple_of(which * r2, 16), r2)


def _gather_ici(halved, whole=()):
    shards = list(halved) + list(whole)
    nh = len(halved)

    def copies(src, dst, sems):
        send_sem, recv_sem = sems[:2]
        x, y, c, chips = _place()
        me = _shard_of((x, y))
        out = []
        for a in range(len(shards)):
            for k, chip in enumerate(chips):
                if a < nh:
                    rows = _half_rows(shards[a], c)
                    out.append(_remote(src[a].at[rows], dst[a].at[me, rows], send_sem.at[a, k], recv_sem.at[a, k],
                                       (*chip, c)))
                else:
                    out.append(_remote(src[a], dst[a].at[me], send_sem.at[a, k], recv_sem.at[a, k], (*chip, c)))
        return out

    n = len(shards)

    def bounce(src, dst, sems):
        local_sem, buffers = sems[2], sems[3:]
        me = _shard_of((lax.axis_index("x"), lax.axis_index("y")))
        return [(pltpu.make_async_copy(src[a], buffers[a], local_sem.at[a, 0]),
                 pltpu.make_async_copy(buffers[a], dst[a].at[me], local_sem.at[a, 1])) for a in range(n)]

    scratch = [pltpu.SemaphoreType.DMA((n, 3))] * 2 + [pltpu.SemaphoreType.DMA((n, 2))]
    scratch += [pltpu.VMEM(s.shape, s.dtype) for s in shards]
    return _Comm(shards, [jax.ShapeDtypeStruct((N_SHARDS,) + s.shape, s.dtype) for s in shards], scratch, copies,
                 bounce=bounce)


def _gather_d2d(stacks, shards):
    def copies(src, dst, sems):
        del src
        send_sem, recv_sem = sems
        x, y, c, chips = _place()
        out = []
        for a in range(len(stacks)):
            for k, chip in enumerate(chips):
                rows = dst[a].at[_shard_of(chip), _half_rows(shards[a], c)]
                out.append(_remote(rows, rows, send_sem.at[a, k], recv_sem.at[a, k], (x, y, 1 - c)))
        return out

    n = len(stacks)
    return _Comm(stacks, [jax.ShapeDtypeStruct(s.shape, s.dtype) for s in stacks],
                 [pltpu.SemaphoreType.DMA((n, 3))] * 2, copies, aliased=True)


def _standalone(comms, name):
    return _launch(None, comms, name=name, grid=(), in_specs=[], out_specs=(), out_shape=())()[1]


def _gather_weights(shards):
    n = len(shards)

    def body(*refs):
        src, dst = refs[:n], refs[n:2 * n]
        send_sem, recv_sem, fsend_sem, frecv_sem, local_sem = refs[2 * n:2 * n + 5]
        buffers = refs[2 * n + 5:]
        x, y, c, chips = _place()
        me = _shard_of((x, y))
        sibling = (x, y, 1 - c)
        relay_from = (x ^ (1 - c), y ^ c)
        relay_to = (x ^ c, y ^ (1 - c))
        own = [(pltpu.make_async_copy(src[a], buffers[a], local_sem.at[a, 0]),
                pltpu.make_async_copy(buffers[a], dst[a].at[me], local_sem.at[a, 1])) for a in range(n)]
        for to_vmem, _ in own:
            to_vmem.start()
        sends = []
        for a in range(n):
            mine = _half_rows(shards[a], c)
            for k in range(2):
                cp = _remote(src[a].at[mine], dst[a].at[me, mine], send_sem.at[a, k], recv_sem.at[a, k],
                             (*chips[k], c))
                cp.start()
                sends.append(cp)
        for to_vmem, to_hbm in own:
            to_vmem.wait()
            to_hbm.start()
        for a in range(n):
            mine = _half_rows(shards[a], c)
            for k in range(2):
                rows = dst[a].at[_shard_of(chips[k]), mine]
                _remote(rows, rows, send_sem.at[a, k], recv_sem.at[a, k], sibling).wait_recv()
            passed = dst[a].at[_shard_of(relay_from), mine]
            cp = _remote(passed, passed, send_sem.at[a, 2], recv_sem.at[a, 2], (*relay_to, c))
            cp.start()
            sends.append(cp)
        for a in range(n):
            for k, chip in enumerate(chips):
                rows = dst[a].at[_shard_of(chip), _half_rows(shards[a], c)]
                if k == 2:
                    _remote(rows, rows, send_sem.at[a, k], recv_sem.at[a, k], sibling).wait_recv()
                cp = _remote(rows, rows, fsend_sem.at[a, k], frecv_sem.at[a, k], sibling)
                cp.start()
                sends.append(cp)
        for a in range(n):
            for k, chip in enumerate(chips):
                rows = dst[a].at[_shard_of(chip), _half_rows(shards[a], 1 - c)]
                _remote(rows, rows, fsend_sem.at[a, k], frecv_sem.at[a, k], sibling).wait_recv()
        for cp in sends:
            cp.wait_send()
        for _, to_hbm in own:
            to_hbm.wait()

    return pl.pallas_call(
        body, name="gather_weights",
        out_shape=tuple(jax.ShapeDtypeStruct((N_SHARDS,) + s.shape, s.dtype) for s in shards),
        in_specs=[ANY] * n, out_specs=(ANY,) * n,
        scratch_shapes=[pltpu.SemaphoreType.DMA((n, 3))] * 4 + [pltpu.SemaphoreType.DMA((n, 2))]
        + [pltpu.VMEM(s.shape, s.dtype) for s in shards],
    )(*shards)


def _swap_comm(big, small=()):
    nb, n = len(big), len(big) + len(small)
    arrays = list(big) + list(small)

    def copies(src, dst, sems):
        send_sem, recv_sem = sems
        x, y, c, _ = _place()
        out = []
        for a in range(n):
            if a < nb:
                r2 = arrays[a].shape[1] // 2
                part = src[a].at[:, pl.ds(pl.multiple_of((1 - c) * r2, SUBLANES), r2), :]
            else:
                part = src[a]
            out.append(_remote(part, dst[a], send_sem.at[a], recv_sem.at[a], (x, y, 1 - c)))
        return out

    out_shape = [jax.ShapeDtypeStruct((N_SHARDS, g.shape[1] // 2, g.shape[2]), g.dtype) for g in big]
    out_shape += [jax.ShapeDtypeStruct(g.shape, g.dtype) for g in small]
    return _Comm(arrays, out_shape, [pltpu.SemaphoreType.DMA((n,))] * 2, copies)


def _scatter_comm(slabbed, small=()):
    ns, n = len(slabbed), len(slabbed) + len(small)
    arrays = list(slabbed) + list(small)

    def copies(src, dst, sems):
        send_sem, recv_sem = sems
        _, _, c, chips = _place()
        out = []
        for a in range(n):
            for k, chip in enumerate(chips):
                if a < ns:
                    part = src[a].at[_shard_of(chip)]
                else:
                    h = arrays[a].shape[0] // 2
                    part = src[a].at[pl.ds(pl.multiple_of(c * h, SUBLANES), h), :]
                out.append(_remote(part, dst[a].at[k], send_sem.at[a, k], recv_sem.at[a, k], (*chip, c)))
        return out

    out_shape = [jax.ShapeDtypeStruct((3,) + g.shape[1:], g.dtype) for g in slabbed]
    out_shape += [jax.ShapeDtypeStruct((3, g.shape[0] // 2, g.shape[1]), g.dtype) for g in small]
    return _Comm(arrays, out_shape, [pltpu.SemaphoreType.DMA((n, 3))] * 2, copies)


def _send_comm(arrays):
    n = len(arrays)

    def copies(src, dst, sems):
        send_sem, recv_sem = sems
        x, y, c, _ = _place()
        return [_remote(src[a], dst[a], send_sem.at[a], recv_sem.at[a], (x, y, 1 - c)) for a in range(n)]

    return _Comm(arrays, [jax.ShapeDtypeStruct(h.shape, h.dtype) for h in arrays],
                 [pltpu.SemaphoreType.DMA((n,))] * 2, copies)


def _row_chunk(rows):
    for cand in (256, 176, 128, 64):
        if rows % cand == 0:
            return cand
    return rows


def _per_shape(fn, *lists):
    groups = {}
    for i, items in enumerate(zip(*lists)):
        groups.setdefault(tuple(a.shape for a in items), []).append(i)
    out = [None] * len(lists[0])
    for idx in groups.values():
        for i, r in zip(idx, fn(*([lst[i] for i in idx] for lst in lists))):
            out[i] = r
    return out


def _add_own_half(stacks, receiveds, c):
    n = len(stacks)
    _, r2, cols = receiveds[0].shape

    def body(c_ref, *refs):
        del c_ref
        for a in range(n):
            refs[2 * n + a][...] = (refs[a][...] + refs[n + a][...]).astype(BF16)

    own = pl.BlockSpec((1, r2, cols), lambda s, c_ref: (s, c_ref[0], 0))
    got = pl.BlockSpec((1, r2, cols), lambda s, c_ref: (s, 0, 0))
    return pl.pallas_call(
        body, name="add_own_half",
        grid_spec=pltpu.PrefetchScalarGridSpec(
            num_scalar_prefetch=1, grid=(N_SHARDS,), in_specs=[own] * n + [got] * n, out_specs=(got,) * n),
        out_shape=(jax.ShapeDtypeStruct(receiveds[0].shape, BF16),) * n,
        compiler_params=_params(("parallel",)),
    )(c, *stacks, *receiveds)


def _chip_order_sum(me, own, s0, s1, s2):
    terms = []
    for s in range(N_SHARDS):
        d = jnp.bitwise_xor(me, s)
        terms.append(jnp.where(d == 0, own, jnp.where(d == 2, s0, jnp.where(d == 1, s1, s2))))
    return ((terms[0] + terms[1]) + terms[2]) + terms[3]


def _sum_chips(own_stacks, slots, me):
    n = len(slots)
    _, rows, cols = slots[0].shape
    rc = _row_chunk(rows)

    def body(me_ref, *refs):
        del me_ref
        for a in range(n):
            own_r, s_r = refs[a], refs[n + a]
            refs[2 * n + a][...] = (((own_r[0].astype(F32) + s_r[0].astype(F32)) + s_r[1].astype(F32))
                                    + s_r[2].astype(F32))

    own = pl.BlockSpec((1, rc, cols), lambda i, me_ref: (me_ref[0], i, 0))
    three = pl.BlockSpec((3, rc, cols), lambda i, me_ref: (0, i, 0))
    total = pl.BlockSpec((rc, cols), lambda i, me_ref: (i, 0))
    return pl.pallas_call(
        body, name="sum_chips",
        grid_spec=pltpu.PrefetchScalarGridSpec(
            num_scalar_prefetch=1, grid=(rows // rc,), in_specs=[own] * n + [three] * n, out_specs=(total,) * n),
        out_shape=(jax.ShapeDtypeStruct((rows, cols), F32),) * n,
        compiler_params=_params(("parallel",)),
    )(me, *own_stacks, *slots)


def _small_pair_sums(mine, theirs):
    n = len(mine)

    def body(*refs):
        for a in range(n):
            refs[2 * n + a][...] = refs[a][...] + refs[n + a][...]

    vm = pl.BlockSpec(memory_space=pltpu.VMEM)
    return pl.pallas_call(
        body, name="small_pair_sums", out_shape=tuple(jax.ShapeDtypeStruct(g.shape, g.dtype) for g in mine),
        in_specs=[vm] * (2 * n), out_specs=(vm,) * n,
        compiler_params=pltpu.CompilerParams(vmem_limit_bytes=VMEM_LIMIT),
    )(*mine, *theirs)


def _adam_math(w, g, m, v):
    m = ADAM_B1 * m + (1.0 - ADAM_B1) * g
    v = ADAM_B2 * v + (1.0 - ADAM_B2) * (g * g)
    m_hat = m / (1.0 - ADAM_B1 ** ADAM_STEP)
    v_hat = v / (1.0 - ADAM_B2 ** ADAM_STEP)
    delta = -ADAM_LR * (m_hat / (jnp.sqrt(v_hat) + ADAM_EPS) + ADAM_WD * w)
    return delta, m, v


def _small_totals(me, c, conv_stack, conv_slots, packed, packed_slots):
    half = packed.shape[0] // 2

    def body(me_ref, c_ref, cs_r, cslot_r, p_r, pslot_r, conv_r, tot_r):
        me_ = me_ref[0]
        conv_r[...] = _chip_order_sum(me_, cs_r[me_], cslot_r[0], cslot_r[1], cslot_r[2])
        own = p_r[pl.ds(pl.multiple_of(c_ref[0] * half, SUBLANES), half), :]
        tot_r[...] = _chip_order_sum(me_, own, pslot_r[0], pslot_r[1], pslot_r[2])

    vm = pl.BlockSpec(memory_space=pltpu.VMEM)
    sm = pl.BlockSpec(memory_space=pltpu.SMEM)
    return pl.pallas_call(
        body, name="small_totals",
        out_shape=(jax.ShapeDtypeStruct(conv_stack.shape[1:], F32), jax.ShapeDtypeStruct((half, packed.shape[1]), F32)),
        in_specs=[sm, sm] + [vm] * 4, out_specs=(vm, vm),
    )(me, c, conv_stack, conv_slots, packed, packed_slots)


def _adam_small(gs, ws, ms, vs):
    n = len(gs)

    def body(*refs):
        for a in range(n):
            g_r, w_r, m_r, v_r = (refs[i * n + a] for i in range(4))
            d_r, nm_r, nv_r = (refs[(4 + i) * n + a] for i in range(3))
            d_r[...], nm_r[...], nv_r[...] = _adam_math(w_r[...], g_r[...], m_r[...], v_r[...])

    vm = pl.BlockSpec(memory_space=pltpu.VMEM)
    shapes = tuple(jax.ShapeDtypeStruct(w.shape, F32) for w in ws)
    out = pl.pallas_call(
        body, name="adam_small", out_shape=shapes * 3, in_specs=[vm] * (4 * n), out_specs=(vm,) * (3 * n),
        compiler_params=pltpu.CompilerParams(vmem_limit_bytes=VMEM_LIMIT),
    )(*gs, *ws, *ms, *vs)
    return out[:n], out[n:2 * n], out[2 * n:]


def _adam_big(ws, mines, others, ms, vs, c):
    n = len(ws)
    r2, cols = mines[0].shape
    rc = _row_chunk(r2)
    nch = r2 // rc

    def body(c_ref, *refs):
        mine_is_here = pl.program_id(0) == c_ref[0]
        for a in range(n):
            w_r, mine_r, other_r, m_r, v_r = (refs[i * n + a] for i in range(5))
            g_r, d_r, nm_r, nv_r = (refs[(5 + i) * n + a] for i in range(4))
            g = jnp.where(mine_is_here, mine_r[...], other_r[...])
            g_r[...] = g
            d_r[...], nm_r[...], nv_r[...] = _adam_math(w_r[...], g, m_r[...], v_r[...])

    full = pl.BlockSpec((rc, cols), lambda h, i, c_ref: (h * nch + i, 0))
    mine = pl.BlockSpec((rc, cols), lambda h, i, c_ref: (jnp.where(h == c_ref[0], i, 0), 0))
    other = pl.BlockSpec((rc, cols), lambda h, i, c_ref: (jnp.where(h == c_ref[0], 0, i), 0))
    shape = jax.ShapeDtypeStruct((2 * r2, cols), F32)
    out = pl.pallas_call(
        body, name="adam_big",
        grid_spec=pltpu.PrefetchScalarGridSpec(
            num_scalar_prefetch=1, grid=(2, nch),
            in_specs=[full] * n + [mine] * n + [other] * n + [full] * (2 * n), out_specs=(full,) * (4 * n)),
        out_shape=(shape,) * (4 * n), compiler_params=_params(("parallel", "parallel")),
    )(c, *ws, *mines, *others, *ms, *vs)
    return [tuple(out[i * n + a] for i in range(4)) for a in range(n)]


def kernel(x, w_in, b_in, ssm_lambda_re, ssm_lambda_im, ssm_log_dt, ssm_b_re, ssm_b_im, ssm_c_re, ssm_c_im, ssm_d, glu_w, glu_b, w_ssm_out, conv_w, w_conv_out, w_o, ln1_g, ln1_b, w_gate, w_up, w_down, ln2_g, ln2_b, loss_target, m_w_in, m_b_in, m_ssm_lambda_re, m_ssm_lambda_im, m_ssm_log_dt, m_ssm_b_re, m_ssm_b_im, m_ssm_c_re, m_ssm_c_im, m_ssm_d, m_glu_w, m_glu_b, m_w_ssm_out, m_conv_w, m_w_conv_out, m_w_o, m_ln1_g, m_ln1_b, m_w_gate, m_w_up, m_w_down, m_ln2_g, m_ln2_b, v_w_in, v_b_in, v_ssm_lambda_re, v_ssm_lambda_im, v_ssm_log_dt, v_ssm_b_re, v_ssm_b_im, v_ssm_c_re, v_ssm_c_im, v_ssm_d, v_glu_w, v_glu_b, v_w_ssm_out, v_conv_w, v_w_conv_out, v_w_o, v_ln1_g, v_ln1_b, v_w_gate, v_w_up, v_w_down, v_ln2_g, v_ln2_b):
    given = dict(locals())
    w = {n: given[n] for n in WEIGHTS}
    m = {n: given["m_" + n] for n in WEIGHTS}
    v = {n: given["v_" + n] for n in WEIGHTS}

    flip = lambda n, a: a.T if n in ("w_gate", "w_up") else a
    shards = {n: flip(n, w[n][0]).astype(BF16) for n in BIG}
    shards["conv_w"] = jnp.pad(conv_w[0], ((0, SUBLANES - 3), (0, 0)))
    c_arr = jnp.reshape(lax.axis_index("c"), (1,)).astype(jnp.int32)
    me = _shard_of((lax.axis_index("x"), lax.axis_index("y")))
    me_arr = jnp.reshape(me, (1,)).astype(jnp.int32)
    dx, pairs, conv_total, packed_total = _device_step(
        x[0], loss_target[0], {n: w[n] for n in SMALL}, shards, c_arr, me_arr)

    grad = _unpack_small(packed_total)
    loss_total = grad.pop("loss")[0]
    grad["conv_w"] = conv_total[:3][None]
    small_names = ("conv_w",) + SMALL
    swap = lambda n, a: a.transpose(0, 1, 3, 2) if n in ("ssm_b_re", "ssm_b_im") else a
    ds, nms, nvs = _adam_small(*([swap(n, d[n]) for n in small_names] for d in (grad, w, m, v)))
    delta, new_m, new_v = {}, {}, {}
    for i, n in enumerate(small_names):
        delta[n], new_m[n], new_v[n] = swap(n, ds[i]), swap(n, nms[i]), swap(n, nvs[i])
    updated = _per_shape(
        lambda *a: _adam_big(*a, c_arr),
        [flip(n, w[n][0]) for n in BIG], [pairs[n][0] for n in BIG], [pairs[n][1] for n in BIG],
        [flip(n, m[n][0]) for n in BIG], [flip(n, v[n][0]) for n in BIG])
    for n, results in zip(BIG, updated):
        grad[n], delta[n], new_m[n], new_v[n] = (flip(n, r)[None] for r in results)

    return (loss_total, dx[None], *[grad[n] for n in WEIGHTS], *[delta[n] for n in WEIGHTS],
            *[new_m[n] for n in WEIGHTS], *[new_v[n] for n in WEIGHTS])
```
